```python
import jax, jax.numpy as jnp
from jax import lax
import numpy as np

D_MODEL = 2048
BATCH = 1
SEQ = 16384
DEPTH = 1

HG_HEADS = 8
HG_DK = 128
HG_DV = 128
GD_HEADS = 8
GD_DK = 128
GD_DV = 128
CONV_WIDTH = 4
CHUNK = 64
N_EXPERTS = 64
N_GROUPS = 8
EXPERTS_PER_GROUP = N_EXPERTS // N_GROUPS
TOPK_GROUPS = 4
TOP_K = 8
EXPERT_FF = 512
SHARED_FF = 512
ROUTED_SCALE = 2.5
EXPERT_BLOCK = 256
NORM_EPS = 1e-6

HG_KEY = HG_HEADS * HG_DK
HG_VAL = HG_HEADS * HG_DV
GD_KEY = GD_HEADS * GD_DK
GD_VAL = GD_HEADS * GD_DV
IN_SPLITS = (HG_KEY, HG_KEY, HG_VAL, HG_VAL, GD_KEY + GD_KEY + GD_VAL, GD_HEADS, GD_HEADS, GD_VAL, D_MODEL, D_MODEL)
IN_WIDTH = sum(IN_SPLITS)

kernel_name = 'hybrid_hgrn2_gdn_moe_adaln_block'


def rms_norm(x, g):
    xf = x.astype(jnp.float32)
    y = xf * lax.rsqrt(jnp.mean(xf * xf, axis=-1, keepdims=True) + NORM_EPS)
    return (y * g.astype(jnp.float32)).astype(x.dtype)


def l2_normalize(x):
    return x * lax.rsqrt(jnp.sum(x * x, axis=-1, keepdims=True) + 1e-6)


def to_chunks(t):
    b, s, h = t.shape[:3]
    t = t.reshape((b, s // CHUNK, CHUNK, h) + t.shape[3:])
    return jnp.moveaxis(t, 3, 1)


def from_chunks(t):
    b, h, n, c = t.shape[:4]
    return jnp.moveaxis(t, 1, 3).reshape((b, n * c, h) + t.shape[4:])


def causal_depthwise_conv(x, w):
    ch = x.shape[-1]
    return lax.conv_general_dilated(x, w[:, None, :].astype(x.dtype), window_strides=(1,),
                                    padding=[(CONV_WIDTH - 1, 0)],
                                    dimension_numbers=('NWC', 'WIO', 'NWC'),
                                    feature_group_count=ch)


def hgrn2_branch(q, f_logit, i, g, lb, onorm_g):
    f32 = jnp.float32
    B, S, _ = q.shape
    causal = jnp.tril(jnp.ones((CHUNK, CHUNK), bool))
    lb = lb.reshape(HG_HEADS, HG_DK)
    f = lb + (1.0 - lb) * jax.nn.sigmoid(f_logit.reshape(B, S, HG_HEADS, HG_DK).astype(f32))
    log_f = jnp.log(f)
    k = 1.0 - f
    q = jax.nn.silu(q.reshape(B, S, HG_HEADS, HG_DK).astype(f32)) * (HG_DK ** -0.5)
    v = i.reshape(B, S, HG_HEADS, HG_DV).astype(f32)
    qc, kc, vc, lc = to_chunks(q), to_chunks(k), to_chunks(v), to_chunks(log_f)
    b = jnp.cumsum(lc, axis=3)
    b_mid = b[:, :, :, CHUNK // 2:CHUNK // 2 + 1]
    b_last = b[:, :, :, -1:]
    scores = jnp.einsum('bhncd,bhnsd->bhncs', qc * jnp.exp(b - b_mid), kc * jnp.exp(b_mid - b))
    scores = jnp.where(causal, scores, 0.0)
    o_intra = jnp.einsum('bhncs,bhnsv->bhncv', scores, vc)
    q_inter = qc * jnp.exp(b)
    k_upd = kc * jnp.exp(b_last - b)
    decay = jnp.exp(b_last[:, :, :, 0])

    def step(state, inp):
        q_n, k_n, v_n, d_n = inp
        o_n = jnp.einsum('bhcd,bhdv->bhcv', q_n, state)
        state = d_n[..., None] * state + jnp.einsum('bhcd,bhcv->bhdv', k_n, v_n)
        return state, o_n

    s0 = jnp.zeros((B, HG_HEADS, HG_DK, HG_DV), f32)
    xs = tuple(jnp.moveaxis(t, 2, 0) for t in (q_inter, k_upd, vc, decay))
    _, o_inter = lax.scan(step, s0, xs)
    o = from_chunks(o_intra + jnp.moveaxis(o_inter, 0, 2))
    o = rms_norm(o, onorm_g) * jax.nn.silu(g.reshape(B, S, HG_HEADS, HG_DV).astype(f32))
    return o.reshape(B, S, HG_VAL)


def gated_deltanet_branch(qkv, a_logit, b_logit, g, conv_w, a_log, dt_bias, onorm_g):
    f32 = jnp.float32
    B, S, _ = qkv.shape
    causal = jnp.tril(jnp.ones((CHUNK, CHUNK), bool))
    strict = jnp.tril(jnp.ones((CHUNK, CHUNK), bool), -1)
    qkv = jax.nn.silu(causal_depthwise_conv(qkv, conv_w)).astype(f32)
    q, k, v = jnp.split(qkv, [GD_KEY, 2 * GD_KEY], axis=-1)
    q = l2_normalize(q.reshape(B, S, GD_HEADS, GD_DK)) * (GD_DK ** -0.5)
    k = l2_normalize(k.reshape(B, S, GD_HEADS, GD_DK))
    v = v.reshape(B, S, GD_HEADS, GD_DV)
    beta = jax.nn.sigmoid(b_logit.astype(f32))
    log_decay = -jnp.exp(a_log.astype(f32)) * jax.nn.softplus(a_logit.astype(f32) + dt_bias.astype(f32))
    qc, kc, vc = to_chunks(q), to_chunks(k), to_chunks(v)
    betac = to_chunks(beta)
    gcum = jnp.cumsum(to_chunks(log_decay), axis=-1)
    diff = gcum[..., :, None] - gcum[..., None, :]
    decay_mat = jnp.where(causal, jnp.exp(jnp.where(causal, diff, 0.0)), 0.0)
    kb = kc * betac[..., None]
    a_mat = jnp.where(strict, jnp.einsum('bhncd,bhnsd->bhncs', kb, kc) * decay_mat, 0.0)
    lhs = a_mat + jnp.eye(CHUNK, dtype=f32)
    rhs = jnp.concatenate([vc * betac[..., None], kb * jnp.exp(gcum)[..., None]], axis=-1)
    sol = lax.linalg.triangular_solve(lhs, rhs, left_side=True, lower=True, unit_diagonal=True)
    u, w = sol[..., :GD_DV], sol[..., GD_DV:]
    attn = jnp.where(causal, jnp.einsum('bhncd,bhnsd->bhncs', qc, kc) * decay_mat, 0.0)
    q_dec = qc * jnp.exp(gcum)[..., None]
    k_upd = kc * jnp.exp(gcum[..., -1:] - gcum)[..., None]
    decay_last = jnp.exp(gcum[..., -1])

    def step(state, inp):
        u_n, w_n, attn_n, q_n, k_n, d_n = inp
        v_new = u_n - jnp.einsum('bhcd,bhdv->bhcv', w_n, state)
        o_n = jnp.einsum('bhcd,bhdv->bhcv', q_n, state) + jnp.einsum('bhcs,bhsv->bhcv', attn_n, v_new)
        state = d_n[..., None, None] * state + jnp.einsum('bhcd,bhcv->bhdv', k_n, v_new)
        return state, o_n

    s0 = jnp.zeros((B, GD_HEADS, GD_DK, GD_DV), f32)
    xs = tuple(jnp.moveaxis(t, 2, 0) for t in (u, w, attn, q_dec, k_upd, decay_last))
    _, o = lax.scan(step, s0, xs)
    o = from_chunks(jnp.moveaxis(o, 0, 2))
    o = rms_norm(o, onorm_g) * jax.nn.silu(g.reshape(B, S, GD_HEADS, GD_DV).astype(f32))
    return o.reshape(B, S, GD_VAL)


def hybrid_mixer(h, w_in, lb, hgrn_onorm_g, gdn_conv_w, gdn_a_log, gdn_dt_bias, gdn_onorm_g,
                 w_branch_hgrn, w_branch_gdn, w_out):
    proj = h @ w_in
    (hq, hf, hi, hg, gqkv, ga, gb, gg, mg_a, mg_b) = jnp.split(
        proj, np.cumsum(IN_SPLITS)[:-1].tolist(), axis=-1)
    o_a = hgrn2_branch(hq, hf, hi, hg, lb, hgrn_onorm_g).astype(h.dtype)
    o_b = gated_deltanet_branch(gqkv, ga, gb, gg, gdn_conv_w, gdn_a_log, gdn_dt_bias,
                                gdn_onorm_g).astype(h.dtype)
    merged = jax.nn.sigmoid(mg_a) * (o_a @ w_branch_hgrn) + jax.nn.sigmoid(mg_b) * (o_b @ w_branch_gdn)
    return merged @ w_out


def moe_ffn(h, w_router, router_bias, w_exp_gate, w_exp_up, w_exp_down, w_sh_gate, w_sh_up, w_sh_down):
    B, S, D = h.shape
    T = B * S
    TK = T * TOP_K
    hf = h.reshape(T, D)
    scores = jax.nn.sigmoid((hf @ w_router).astype(jnp.float32))
    biased = scores + router_bias.astype(jnp.float32)
    grp_score = lax.top_k(biased.reshape(T, N_GROUPS, EXPERTS_PER_GROUP), 2)[0].sum(-1)
    _, grp_idx = lax.top_k(grp_score, TOPK_GROUPS)
    grp_mask = jnp.zeros((T, N_GROUPS), bool).at[jnp.arange(T)[:, None], grp_idx].set(True)
    exp_mask = jnp.repeat(grp_mask, EXPERTS_PER_GROUP, axis=1)
    _, idx = lax.top_k(jnp.where(exp_mask, biased, -jnp.inf), TOP_K)
    wts = jnp.take_along_axis(scores, idx, axis=1)
    wts = wts / jnp.sum(wts, axis=-1, keepdims=True) * ROUTED_SCALE
    n_blocks = -(-(TK + N_EXPERTS * (EXPERT_BLOCK - 1)) // EXPERT_BLOCK)
    P = n_blocks * EXPERT_BLOCK
    flat_e = idx.reshape(-1)
    flat_w = wts.reshape(-1)
    order = jnp.argsort(flat_e, stable=True)
    e_sorted = flat_e[order]
    counts = jnp.bincount(flat_e, length=N_EXPERTS)
    padded = (counts + EXPERT_BLOCK - 1) // EXPERT_BLOCK * EXPERT_BLOCK
    start = jnp.cumsum(counts) - counts
    pend = jnp.cumsum(padded)
    pstart = pend - padded
    dest = pstart[e_sorted] + jnp.arange(TK, dtype=jnp.int32) - start[e_sorted]
    buf_tok = jnp.zeros((P,), jnp.int32).at[dest].set((order // TOP_K).astype(jnp.int32))
    buf_w = jnp.zeros((P,), jnp.float32).at[dest].set(flat_w[order])
    block_e = jnp.minimum(jnp.searchsorted(pend, jnp.arange(n_blocks) * EXPERT_BLOCK, side='right'),
                          N_EXPERTS - 1).astype(jnp.int32)

    def expert_block(acc, blk):
        tok, wt, e = blk
        xb = hf[tok]
        hid = jax.nn.silu(xb @ w_exp_gate[e]) * (xb @ w_exp_up[e])
        yb = (hid @ w_exp_down[e]) * wt[:, None].astype(h.dtype)
        return acc.at[tok].add(yb), None

    routed, _ = lax.scan(expert_block, jnp.zeros((T, D), h.dtype),
                         (buf_tok.reshape(n_blocks, EXPERT_BLOCK),
                          buf_w.reshape(n_blocks, EXPERT_BLOCK), block_e))
    shared = (jax.nn.silu(hf @ w_sh_gate) * (hf @ w_sh_up)) @ w_sh_down
    return (routed + shared).reshape(B, S, D)


def setup_inputs(seed: int = 0) -> dict:
    key = jax.random.key(seed)
    ks = jax.random.split(key, 24)
    D, L = D_MODEL, DEPTH
    nrm = jax.random.normal
    dt = jnp.exp(jax.random.uniform(ks[11], (L, GD_HEADS)) * (np.log(0.1) - np.log(0.001)) + np.log(0.001))
    return {
        'x': nrm(ks[0], (BATCH, SEQ, D), jnp.float32),
        'c': nrm(ks[1], (BATCH, D), jnp.float32),
        'w_ada': nrm(ks[2], (L, D, 6 * D)) * (0.5 * D ** -0.5),
        'b_ada': nrm(ks[3], (L, 6 * D)) * 0.02,
        'norm1_g': 1.0 + 0.02 * nrm(ks[4], (L, D)),
        'norm2_g': 1.0 + 0.02 * nrm(ks[5], (L, D)),
        'w_in': nrm(ks[6], (L, D, IN_WIDTH)) * D ** -0.5,
        'hgrn_lb_table': 0.1 * nrm(ks[7], (L + 1, HG_KEY)),
        'hgrn_onorm_g': 1.0 + 0.02 * nrm(ks[8], (L, HG_DV)),
        'gdn_conv_w': nrm(ks[9], (L, CONV_WIDTH, 2 * GD_KEY + GD_VAL)) * 0.5,
        'gdn_a_log': jnp.log(jax.random.uniform(ks[10], (L, GD_HEADS), minval=1.0, maxval=16.0)),
        'gdn_dt_bias': dt + jnp.log(-jnp.expm1(-dt)),
        'gdn_onorm_g': 1.0 + 0.02 * nrm(ks[12], (L, GD_DV)),
        'w_branch_hgrn': nrm(ks[13], (L, HG_VAL, D)) * HG_VAL ** -0.5,
        'w_branch_gdn': nrm(ks[14], (L, GD_VAL, D)) * GD_VAL ** -0.5,
        'w_out': nrm(ks[15], (L, D, D)) * D ** -0.5,
        'w_router': nrm(ks[16], (L, D, N_EXPERTS)) * D ** -0.5,
        'router_bias': 0.01 * nrm(ks[17], (L, N_EXPERTS)),
        'w_exp_gate': nrm(ks[18], (L, N_EXPERTS, D, EXPERT_FF)) * D ** -0.5,
        'w_exp_up': nrm(ks[19], (L, N_EXPERTS, D, EXPERT_FF)) * D ** -0.5,
        'w_exp_down': nrm(ks[20], (L, N_EXPERTS, EXPERT_FF, D)) * EXPERT_FF ** -0.5,
        'w_sh_gate': nrm(ks[21], (L, D, SHARED_FF)) * D ** -0.5,
        'w_sh_up': nrm(ks[22], (L, D, SHARED_FF)) * D ** -0.5,
        'w_sh_down': nrm(ks[23], (L, SHARED_FF, D)) * SHARED_FF ** -0.5,
        'normf_g': 1.0 + 0.02 * nrm(jax.random.fold_in(key, 99), (D,)),
    }


def reference(x, c, w_ada, b_ada, norm1_g, norm2_g, w_in, hgrn_lb_table, hgrn_onorm_g, gdn_conv_w,
              gdn_a_log, gdn_dt_bias, gdn_onorm_g, w_branch_hgrn, w_branch_gdn, w_out, w_router,
              router_bias, w_exp_gate, w_exp_up, w_exp_down, w_sh_gate, w_sh_up, w_sh_down, normf_g):
    lb_all = jnp.cumsum(jax.nn.softmax(hgrn_lb_table.astype(jnp.float32), axis=0), axis=0)
    cond = jax.nn.silu(c)
    for l in range(DEPTH):
        mod = (cond @ w_ada[l] + b_ada[l])[:, None, :]
        sh1, sc1, gt1, sh2, sc2, gt2 = jnp.split(mod, 6, axis=-1)
        h = rms_norm(x, norm1_g[l]) * (1.0 + sc1) + sh1
        x = x + gt1 * hybrid_mixer(h, w_in[l], lb_all[l], hgrn_onorm_g[l], gdn_conv_w[l], gdn_a_log[l],
                                   gdn_dt_bias[l], gdn_onorm_g[l], w_branch_hgrn[l], w_branch_gdn[l], w_out[l])
        h = rms_norm(x, norm2_g[l]) * (1.0 + sc2) + sh2
        x = x + gt2 * moe_ffn(h, w_router[l], router_bias[l], w_exp_gate[l], w_exp_up[l], w_exp_down[l],
                              w_sh_gate[l], w_sh_up[l], w_sh_down[l])
    return rms_norm(x, normf_g)
```

```python
import functools

import jax
import jax.numpy as jnp
from jax import lax
from jax.experimental import pallas as pl
from jax.experimental.pallas import tpu as pltpu

F32 = jnp.float32
BF16 = jnp.bfloat16
I32 = jnp.int32

NORM_EPS = 1e-6
L2_EPS = 1e-6
HEADS = 8
HEAD_DIM = 128
CONV_WIDTH = 4
CHUNK = 64
N_EXPERTS = 64
N_GROUPS = 8
GROUP_SIZE = N_EXPERTS // N_GROUPS
TOPK_GROUPS = 4
TOP_K = 8
ROUTED_SCALE = 2.5
EXPERT_BLOCK = 256

LANES = 128
VMEM_LIMIT = 56 * 1024 * 1024

NT = (((1,), (1,)), ((), ()))
TN = (((0,), (0,)), ((), ()))


def _params(sem, **kw):
    return pltpu.CompilerParams(dimension_semantics=sem, vmem_limit_bytes=VMEM_LIMIT, **kw)


def _dot(a, b):
    return jnp.dot(a, b, preferred_element_type=F32)


def _dg(a, b, dims):
    return lax.dot_general(a, b, dims, preferred_element_type=F32)


def _split(x):
    hi = x.astype(BF16)
    lo = (x - hi.astype(F32)).astype(BF16)
    return hi, lo


def _dot_exact_lhs(a_bf16, x, dims=None):
    hi, lo = _split(x)
    if dims is None:
        return _dot(a_bf16, hi) + _dot(a_bf16, lo)
    return _dg(a_bf16, hi, dims) + _dg(a_bf16, lo, dims)


def _dot_exact_rhs(x, b_bf16):
    hi, lo = _split(x)
    return _dot(hi, b_bf16) + _dot(lo, b_bf16)


def _dot3(a, b):
    ah, al = _split(a)
    bh, bl = _split(b)
    return _dot(ah, bh) + (_dot(ah, bl) + _dot(al, bh))


def _sigmoid(x):
    return 1.0 / (1.0 + jnp.exp(-x))


def _silu(x):
    return x * _sigmoid(x)


def _rms(x, eps):
    return x * lax.rsqrt(jnp.mean(x * x, axis=-1, keepdims=True) + eps)


def _iota2(shape, dim):
    return lax.broadcasted_iota(I32, shape, dim)


def _ada_kernel(c_ref, w_ref, b_ref, o_ref):
    cond = _silu(c_ref[...])
    o_ref[...] = jnp.dot(cond, w_ref[...], preferred_element_type=F32,
                         precision=lax.Precision.HIGHEST) + b_ref[...]


def _ada(c, w_ada, b_ada):
    d, n = w_ada.shape
    tn = 1024
    c8 = jnp.broadcast_to(c, (8, d))
    out = pl.pallas_call(
        _ada_kernel,
        grid=(n // tn,),
        in_specs=[pl.BlockSpec((8, d), lambda j: (0, 0)),
                  pl.BlockSpec((d, tn), lambda j: (0, j)),
                  pl.BlockSpec((1, tn), lambda j: (0, j))],
        out_specs=pl.BlockSpec((8, tn), lambda j: (0, j)),
        out_shape=jax.ShapeDtypeStruct((8, n), F32),
        compiler_params=_params(("arbitrary",)),
    )(c8, w_ada, b_ada.reshape(1, n))
    return out[0:1]


def _inproj_kernel(x_ref, g_ref, sc_ref, sh_ref, w_ref, wst_ref, proj_ref, smallt_ref, h_scr):
    @pl.when(pl.program_id(1) == 0)
    def _():
        h = _rms(x_ref[...], NORM_EPS) * g_ref[...] * (1.0 + sc_ref[...]) + sh_ref[...]
        hb = h.astype(BF16)
        h_scr[...] = hb
        smallt_ref[...] = _dg(wst_ref[...], hb, NT)

    proj_ref[...] = _dot(h_scr[...], w_ref[...]).astype(BF16)


def _inproj(x, g, sc, sh, w_main, w_small_t, tm, tn):
    s, d = x.shape
    n = w_main.shape[1]
    ns = w_small_t.shape[0]
    row = lambda i, j: (0, 0)
    return pl.pallas_call(
        _inproj_kernel,
        grid=(s // tm, n // tn),
        in_specs=[pl.BlockSpec((tm, d), lambda i, j: (i, 0)),
                  pl.BlockSpec((1, d), row), pl.BlockSpec((1, d), row), pl.BlockSpec((1, d), row),
                  pl.BlockSpec((d, tn), lambda i, j: (0, j)),
                  pl.BlockSpec((ns, d), row)],
        out_specs=[pl.BlockSpec((tm, tn), lambda i, j: (i, j)),
                   pl.BlockSpec((ns, tm), lambda i, j: (0, i))],
        out_shape=[jax.ShapeDtypeStruct((s, n), BF16), jax.ShapeDtypeStruct((ns, s), F32)],
        scratch_shapes=[pltpu.VMEM((tm, d), BF16)],
        compiler_params=_params(("arbitrary", "arbitrary")),
    )(x, g, sc, sh, w_main, w_small_t)


def _hgrn_kernel(q_ref, f_ref, i_ref, g_ref, lb_ref, on_ref, o_ref, st_scr, *, n_chunks):
    @pl.when(pl.program_id(1) == 0)
    def _():
        st_scr[...] = jnp.zeros_like(st_scr)

    c = CHUNK
    r = _iota2((c, c), 0)
    cidx = _iota2((c, c), 1)
    causal = cidx <= r
    tri = causal.astype(BF16)
    lb = lb_ref[...]
    on_g = on_ref[...]
    for n in range(n_chunks):
        rows = pl.ds(n * c, c)
        f = lb + (1.0 - lb) * _sigmoid(f_ref[rows, :].astype(F32))
        b = _dot_exact_lhs(tri, jnp.log(f))
        k = 1.0 - f
        q = _silu(q_ref[rows, :].astype(F32)) * (HEAD_DIM ** -0.5)
        v = i_ref[rows, :]
        b_mid = b[c // 2:c // 2 + 1, :]
        b_last = b[c - 1:c, :]
        qa = (q * jnp.exp(b - b_mid)).astype(BF16)
        ka = (k * jnp.exp(b_mid - b)).astype(BF16)
        scores = jnp.where(causal, _dg(qa, ka, NT), 0.0)
        o = _dot(scores.astype(BF16), v)
        st = st_scr[...]
        o = o + _dg((q * jnp.exp(b)).astype(BF16), st.astype(BF16), NT)
        k_upd = (k * jnp.exp(b_last - b)).astype(BF16)
        st_scr[...] = jnp.exp(b_last) * st + _dg(v, k_upd, TN)
        o = _rms(o, NORM_EPS) * on_g * _silu(g_ref[rows, :].astype(F32))
        o_ref[rows, :] = o.astype(BF16)


def _hgrn(proj, lb, onorm_g, ts):
    s = proj.shape[0]
    hd = HEAD_DIM
    col = lambda off: pl.BlockSpec((ts, hd), lambda h, j, off=off: (j, off + h))
    return pl.pallas_call(
        functools.partial(_hgrn_kernel, n_chunks=ts // CHUNK),
        grid=(HEADS, s // ts),
        in_specs=[col(0), col(HEADS), col(2 * HEADS), col(3 * HEADS),
                  pl.BlockSpec((1, hd), lambda h, j: (0, h)),
                  pl.BlockSpec((1, hd), lambda h, j: (0, 0))],
        out_specs=pl.BlockSpec((ts, hd), lambda h, j: (j, h)),
        out_shape=jax.ShapeDtypeStruct((s, HEADS * hd), BF16),
        scratch_shapes=[pltpu.VMEM((hd, hd), F32)],
        compiler_params=_params(("arbitrary", "arbitrary")),
    )(proj, proj, proj, proj, lb, onorm_g)


def _gdn_kernel(q_ref, k_ref, v_ref, qp_ref, kp_ref, vp_ref, wq_ref, wk_ref, wv_ref, ab_ref, alog_ref,
                dtb_ref, g_ref, on_ref, o_ref, st_scr, cat_scr, *, n_chunks, ts):
    h = pl.program_id(0)
    first = pl.program_id(1) == 0

    @pl.when(first)
    def _():
        st_scr[...] = jnp.zeros_like(st_scr)

    def conv_silu(cur_ref, prev_ref, w_ref):
        cat_scr[0:8, :] = jnp.where(first, 0.0, prev_ref[...].astype(F32))
        cat_scr[8:8 + ts, :] = cur_ref[...].astype(F32)
        acc = None
        for j in range(CONV_WIDTH):
            off = 8 - (CONV_WIDTH - 1) + j
            term = cat_scr[off:off + ts, :] * w_ref[j:j + 1, :]
            acc = term if acc is None else acc + term
        return _silu(acc)

    def l2n(x):
        return x * lax.rsqrt(jnp.sum(x * x, axis=-1, keepdims=True) + L2_EPS)

    q_all = l2n(conv_silu(q_ref, qp_ref, wq_ref)) * (HEAD_DIM ** -0.5)
    k_all = l2n(conv_silu(k_ref, kp_ref, wk_ref))
    v_all = conv_silu(v_ref, vp_ref, wv_ref)

    a_row = ab_ref[0]
    b_row = ab_ref[1]
    a_log = jnp.full(a_row.shape, alog_ref[h], F32)
    z = a_row + dtb_ref[h]
    softplus = jnp.maximum(z, 0.0) + jnp.log(1.0 + jnp.exp(-jnp.abs(z)))
    ld_all = -jnp.exp(a_log) * softplus
    beta_all = _sigmoid(b_row)

    c = CHUNK
    r = _iota2((c, c), 0)
    cidx = _iota2((c, c), 1)
    causal = cidx <= r
    strict = cidx < r
    tri = causal.astype(BF16)
    tri_t = (r <= cidx).astype(BF16)
    eye = (r == cidx)
    eye_b = eye.astype(BF16)
    eye_f = eye.astype(F32)
    on_g = on_ref[...]

    for n in range(n_chunks):
        lo, hi = n * c, (n + 1) * c
        q = q_all[lo:hi]
        k = k_all[lo:hi]
        v = v_all[lo:hi]
        ld = jnp.broadcast_to(ld_all[:, lo:hi], (HEAD_DIM, c))
        beta = jnp.broadcast_to(beta_all[:, lo:hi], (HEAD_DIM, c))
        gc = _dot_exact_lhs(tri, ld, NT)
        g_row = _dot_exact_rhs(ld[0:8], tri_t)[0:1]
        bc = _dot_exact_lhs(eye_b, beta, NT)
        g_last = gc[c - 1:c, :]
        diff = gc[:, 0:c] - g_row
        dm = jnp.where(causal, jnp.exp(jnp.where(causal, diff, 0.0)), 0.0)
        kb16 = k.astype(BF16)
        kq = _dg(jnp.concatenate([kb16, q.astype(BF16)], axis=0), kb16, NT)
        a_mat = jnp.where(strict, bc[:, 0:c] * kq[0:c] * dm, 0.0)
        attn = kq[c:2 * c] * dm
        bm = -a_mat
        p = eye_f + bm
        bm = _dot3(bm, bm)
        for _ in range(c.bit_length() - 3):
            res = _dot3(bm, jnp.concatenate([bm, p], axis=1))
            p = p + res[:, c:2 * c]
            bm = res[:, 0:c]
        p = p + _dot3(bm, p)
        egc = jnp.exp(gc)
        rhs = jnp.concatenate([v * bc, k * (bc * egc)], axis=1)
        sol = _dot3(p, rhs)
        u = sol[:, 0:HEAD_DIM]
        w = sol[:, HEAD_DIM:2 * HEAD_DIM]
        q_dec = q * egc
        k_upd = k * jnp.exp(g_last - gc)
        st = st_scr[...]
        wq = _dot(jnp.concatenate([w.astype(BF16), q_dec.astype(BF16)], axis=0), st.astype(BF16))
        v_new = u - wq[0:c]
        vn16 = v_new.astype(BF16)
        o = wq[c:2 * c] + _dot(attn.astype(BF16), vn16)
        st_scr[...] = jnp.exp(g_last) * st + _dg(k_upd.astype(BF16), vn16, TN)
        o = _rms(o, NORM_EPS) * on_g * _silu(g_ref[lo:hi, :].astype(F32))
        o_ref[lo:hi, :] = o.astype(BF16)


def _gdn(proj, conv_w, ab_t, a_log, dt_bias, onorm_g, ts):
    s = proj.shape[0]
    hd = HEAD_DIM
    q0 = 4 * HEADS
    cur = lambda off: pl.BlockSpec((ts, hd), lambda h, j, off=off: (j, off + h))
    prev = lambda off: pl.BlockSpec((8, hd), lambda h, j, off=off: (jnp.maximum(j * (ts // 8) - 1, 0), off + h))
    cw = lambda off: pl.BlockSpec((CONV_WIDTH, hd), lambda h, j, off=off: (0, off + h))
    smem = pl.BlockSpec(memory_space=pltpu.SMEM)
    return pl.pallas_call(
        functools.partial(_gdn_kernel, n_chunks=ts // CHUNK, ts=ts),
        grid=(HEADS, s // ts),
        in_specs=[cur(q0), cur(q0 + HEADS), cur(q0 + 2 * HEADS),
                  prev(q0), prev(q0 + HEADS), prev(q0 + 2 * HEADS),
                  cw(0), cw(HEADS), cw(2 * HEADS),
                  pl.BlockSpec((2, None, 1, ts), lambda h, j: (0, h, 0, j)),
                  smem, smem,
                  cur(q0 + 3 * HEADS),
                  pl.BlockSpec((1, hd), lambda h, j: (0, 0))],
        out_specs=pl.BlockSpec((ts, hd), lambda h, j: (j, h)),
        out_shape=jax.ShapeDtypeStruct((s, HEADS * hd), BF16),
        scratch_shapes=[pltpu.VMEM((hd, hd), F32), pltpu.VMEM((ts + 8, hd), F32)],
        compiler_params=_params(("arbitrary", "arbitrary")),
    )(proj, proj, proj, proj, proj, proj, conv_w, conv_w, conv_w,
      ab_t.reshape(2, HEADS, 1, s), a_log, dt_bias, proj, onorm_g)


def _merge_kernel(oa_ref, ob_ref, mga_ref, mgb_ref, x_ref, wa_ref, wb_ref, wo_ref, gt_ref, g2_ref, sc_ref,
                  sh_ref, x1_ref, h2_ref):
    ya = _dot(oa_ref[...], wa_ref[...])
    yb = _dot(ob_ref[...], wb_ref[...])
    merged = _sigmoid(mga_ref[...].astype(F32)) * ya + _sigmoid(mgb_ref[...].astype(F32)) * yb
    x1 = x_ref[...] + gt_ref[...] * _dot(merged.astype(BF16), wo_ref[...])
    x1_ref[...] = x1
    h2 = _rms(x1, NORM_EPS) * g2_ref[...] * (1.0 + sc_ref[...]) + sh_ref[...]
    h2_ref[...] = h2.astype(BF16)


def _merge(o_a, o_b, proj, x, w_a, w_b, w_o, gt1, g2, sc2, sh2, tm):
    s, d = x.shape
    dv = o_a.shape[1]
    mg0 = (8 * HEADS * HEAD_DIM) // d
    const = lambda shape: pl.BlockSpec(shape, lambda i: (0, 0), pipeline_mode=pl.Buffered(1))
    return pl.pallas_call(
        _merge_kernel,
        grid=(s // tm,),
        in_specs=[pl.BlockSpec((tm, dv), lambda i: (i, 0)),
                  pl.BlockSpec((tm, dv), lambda i: (i, 0)),
                  pl.BlockSpec((tm, d), lambda i: (i, mg0)),
                  pl.BlockSpec((tm, d), lambda i: (i, mg0 + 1)),
                  pl.BlockSpec((tm, d), lambda i: (i, 0)),
                  const((dv, d)), const((dv, d)), const((d, d)),
                  const((1, d)), const((1, d)), const((1, d)), const((1, d))],
        out_specs=[pl.BlockSpec((tm, d), lambda i: (i, 0)), pl.BlockSpec((tm, d), lambda i: (i, 0))],
        out_shape=[jax.ShapeDtypeStruct((s, d), F32), jax.ShapeDtypeStruct((s, d), BF16)],
        compiler_params=_params(("arbitrary",)),
    )(o_a, o_b, proj, proj, x, w_a, w_b, w_o, gt1, g2, sc2, sh2)


def _first_max(vals, iota, size, axis):
    m = jnp.max(vals, axis=axis, keepdims=True)
    idx = jnp.min(jnp.where(vals == m, iota, size), axis=axis, keepdims=True)
    return m, idx


def _router_kernel(x1_ref, g2_ref, sc_ref, sh_ref, wrt_ref, bias_ref, upper_ref, idx_ref, wts_ref, rank_ref,
                   wcol_ref, cnt_ref, cnt_scr, *, tm):
    @pl.when(pl.program_id(0) == 0)
    def _():
        cnt_scr[...] = jnp.zeros_like(cnt_scr)

    e = N_EXPERTS
    h2 = _rms(x1_ref[...], NORM_EPS) * g2_ref[...] * (1.0 + sc_ref[...]) + sh_ref[...]
    logits = lax.dot_general(wrt_ref[...], h2, NT, preferred_element_type=F32,
                             precision=lax.Precision.HIGHEST)
    scores = _sigmoid(logits)
    biased = scores + bias_ref[...]
    neg = -jnp.inf

    g3 = biased.reshape(N_GROUPS, GROUP_SIZE, tm)
    i3 = lax.broadcasted_iota(I32, g3.shape, 1)
    m1, a1 = _first_max(g3, i3, GROUP_SIZE, 1)
    m2 = jnp.max(jnp.where(i3 == a1, neg, g3), axis=1, keepdims=True)
    gs = (m1 + m2).reshape(N_GROUPS, tm)
    ig = _iota2(gs.shape, 0)
    gmask = jnp.zeros(gs.shape, jnp.bool_)
    for _ in range(TOPK_GROUPS):
        _, a = _first_max(gs, ig, N_GROUPS, 0)
        pick = ig == a
        gmask = jnp.logical_or(gmask, pick)
        gs = jnp.where(pick, neg, gs)
    emask = jnp.broadcast_to(gmask.reshape(N_GROUPS, 1, tm), (N_GROUPS, GROUP_SIZE, tm)).reshape(e, tm)

    cand = jnp.where(emask, biased, neg)
    ie = _iota2((e, tm), 0)
    sel_all = jnp.zeros((e, tm), jnp.bool_)
    idx_rows, w_rows, picks = [], [], []
    for _ in range(TOP_K):
        _, a = _first_max(cand, ie, e, 0)
        pick = ie == a
        picks.append(pick)
        idx_rows.append(a)
        w_rows.append(jnp.sum(jnp.where(pick, scores, 0.0), axis=0, keepdims=True))
        sel_all = jnp.logical_or(sel_all, pick)
        cand = jnp.where(pick, neg, cand)
    w_sum = w_rows[0]
    for wr in w_rows[1:]:
        w_sum = w_sum + wr
    wts = jnp.concatenate(w_rows, axis=0) / w_sum * ROUTED_SCALE
    idx_ref[...] = jnp.concatenate(idx_rows, axis=0)
    wts_ref[...] = wts

    sel = sel_all.astype(BF16)
    rank = _dot(sel, upper_ref[...]) + cnt_scr[...]
    rank_rows = [jnp.sum(jnp.where(pk, rank, 0.0), axis=0, keepdims=True) for pk in picks]
    rank_ref[...] = jnp.concatenate(rank_rows, axis=0).astype(I32)
    cnt_scr[...] = cnt_scr[...] + jnp.sum(sel_all.astype(F32), axis=1, keepdims=True)
    cnt_ref[...] = jnp.broadcast_to(cnt_scr[...], cnt_ref.shape).astype(I32)

    eye = (_iota2((tm, tm), 0) == _iota2((tm, tm), 1)).astype(BF16)
    w_pad = jnp.concatenate([wts, jnp.zeros((LANES - TOP_K, tm), F32)], axis=0)
    hi, lo = _split(w_pad)
    lo2 = (w_pad - hi.astype(F32) - lo.astype(F32)).astype(BF16)
    wcol_ref[...] = _dg(eye, hi, NT) + _dg(eye, lo, NT) + _dg(eye, lo2, NT)


def _router(x1, g2, sc2, sh2, w_router_t, bias_col, tm):
    s, d = x1.shape
    e = N_EXPERTS
    upper = (jnp.arange(tm)[:, None] < jnp.arange(tm)[None, :]).astype(BF16)
    const = lambda shape: pl.BlockSpec(shape, lambda i: (0, 0))
    tok = lambda rows: pl.BlockSpec((rows, tm), lambda i: (0, i))
    return pl.pallas_call(
        functools.partial(_router_kernel, tm=tm),
        grid=(s // tm,),
        in_specs=[pl.BlockSpec((tm, d), lambda i: (i, 0)),
                  const((1, d)), const((1, d)), const((1, d)),
                  const((e, d)), const((e, 1)), const((tm, tm))],
        out_specs=[tok(TOP_K), tok(TOP_K), tok(TOP_K),
                   pl.BlockSpec((tm, LANES), lambda i: (i, 0)),
                   const((e, LANES))],
        out_shape=[jax.ShapeDtypeStruct((TOP_K, s), I32), jax.ShapeDtypeStruct((TOP_K, s), F32),
                   jax.ShapeDtypeStruct((TOP_K, s), I32), jax.ShapeDtypeStruct((s, LANES), F32),
                   jax.ShapeDtypeStruct((e, LANES), I32)],
        scratch_shapes=[pltpu.VMEM((e, 1), F32)],
        compiler_params=_params(("arbitrary",)),
    )(x1, g2, sc2, sh2, w_router_t, bias_col, upper)


def _dest_kernel(pstart_ref, idx_ref, rank_ref, dest_ref):
    idx = idx_ref[...]
    base = jnp.zeros(idx.shape, I32)
    for ex in range(N_EXPERTS):
        base = jnp.where(idx == ex, pstart_ref[ex], base)
    dest_ref[...] = base + rank_ref[...]


def _dest(pstart, idx_t, rank_t, tm):
    k, s = idx_t.shape
    tok = pl.BlockSpec((k, tm), lambda i: (0, i))
    return pl.pallas_call(
        _dest_kernel,
        grid=(s // tm,),
        in_specs=[pl.BlockSpec(memory_space=pltpu.SMEM), tok, tok],
        out_specs=tok,
        out_shape=jax.ShapeDtypeStruct((k, s), I32),
        compiler_params=_params(("arbitrary",)),
    )(pstart, idx_t, rank_t)


def _dispatch_kernel(dest_ref, h_ref, zero_ref, xs_ref, sem, *, tm, s):
    del zero_ref
    base = pl.program_id(0) * tm

    def row_copy(t, k):
        slot = dest_ref[k * s + base + t]
        return pltpu.make_async_copy(h_ref.at[pl.ds(t, 1), :], xs_ref.at[pl.ds(slot, 1), :], sem)

    def start(t, carry):
        for k in range(TOP_K):
            row_copy(t, k).start()
        return carry

    lax.fori_loop(0, tm, start, 0)

    def wait(t, carry):
        for k in range(TOP_K):
            row_copy(t, k).wait()
        return carry

    lax.fori_loop(0, tm, wait, 0)


def _dispatch(dest_flat, h2f, n_rows, tm):
    s, d = h2f.shape
    zeros = jnp.zeros((n_rows, d), F32)
    return pl.pallas_call(
        functools.partial(_dispatch_kernel, tm=tm, s=s),
        grid_spec=pltpu.PrefetchScalarGridSpec(
            num_scalar_prefetch=1,
            grid=(s // tm,),
            in_specs=[pl.BlockSpec((tm, d), lambda i, dest: (i, 0)),
                      pl.BlockSpec(memory_space=pl.ANY)],
            out_specs=pl.BlockSpec(memory_space=pl.ANY),
            scratch_shapes=[pltpu.SemaphoreType.DMA(())]),
        out_shape=jax.ShapeDtypeStruct((n_rows, d), F32),
        input_output_aliases={2: 0},
        compiler_params=_params(("arbitrary",), has_side_effects=True, disable_bounds_checks=True),
    )(dest_flat, h2f, zeros)


def _expert_kernel(be_ref, nu_ref, x_ref, wg_ref, wu_ref, wd_ref, y_ref):
    del be_ref

    @pl.when(pl.program_id(0) < nu_ref[0])
    def _():
        xb = x_ref[...].astype(BF16)
        hid = _silu(_dot(xb, wg_ref[...])) * _dot(xb, wu_ref[...])
        y_ref[...] = _dot(hid.astype(BF16), wd_ref[...])


def _experts(block_e, n_used, xs, w_gate, w_up, w_down):
    p, d = xs.shape
    ff = w_gate.shape[2]
    bm = EXPERT_BLOCK
    blk = lambda b, be, nu: (jnp.minimum(b, nu[0] - 1), 0)
    return pl.pallas_call(
        _expert_kernel,
        grid_spec=pltpu.PrefetchScalarGridSpec(
            num_scalar_prefetch=2,
            grid=(p // bm,),
            in_specs=[pl.BlockSpec((bm, d), blk),
                      pl.BlockSpec((None, d, ff), lambda b, be, nu: (be[b], 0, 0)),
                      pl.BlockSpec((None, d, ff), lambda b, be, nu: (be[b], 0, 0)),
                      pl.BlockSpec((None, ff, d), lambda b, be, nu: (be[b], 0, 0))],
            out_specs=pl.BlockSpec((bm, d), blk)),
        out_shape=jax.ShapeDtypeStruct((p, d), F32),
        compiler_params=_params(("arbitrary",)),
    )(block_e, n_used, xs, w_gate, w_up, w_down)


def _combine_kernel(dest_ref, ys_ref, h_ref, x1_ref, wcol_ref, wg_ref, wu_ref, wd_ref, gt_ref, gf_ref, o_ref,
                    rows_scr, sem, *, tm, s):
    base = pl.program_id(0) * tm

    def row_copy(t, k):
        slot = dest_ref[k * s + base + t]
        return pltpu.make_async_copy(ys_ref.at[pl.ds(slot, 1), :], rows_scr.at[k, pl.ds(t, 1), :], sem)

    def start(t, carry):
        for k in range(TOP_K):
            row_copy(t, k).start()
        return carry

    lax.fori_loop(0, tm, start, 0)

    hb = h_ref[...]
    hid = _silu(_dot(hb, wg_ref[...])) * _dot(hb, wu_ref[...])
    acc = _dot(hid.astype(BF16), wd_ref[...])

    def wait(t, carry):
        for k in range(TOP_K):
            row_copy(t, k).wait()
        return carry

    lax.fori_loop(0, tm, wait, 0)

    wcol = wcol_ref[...]
    for k in range(TOP_K):
        acc = acc + rows_scr[k] * wcol[:, k:k + 1]
    x2 = x1_ref[...] + gt_ref[...] * acc
    o_ref[...] = _rms(x2, NORM_EPS) * gf_ref[...]


def _combine(dest_flat, ys, h2, x1, wcol, w_gate, w_up, w_down, gt2, gf, tm):
    s, d = x1.shape
    ff = w_gate.shape[1]
    const = lambda shape: pl.BlockSpec(shape, lambda i, dest: (0, 0))
    tile = lambda cols: pl.BlockSpec((tm, cols), lambda i, dest: (i, 0))
    return pl.pallas_call(
        functools.partial(_combine_kernel, tm=tm, s=s),
        grid_spec=pltpu.PrefetchScalarGridSpec(
            num_scalar_prefetch=1,
            grid=(s // tm,),
            in_specs=[pl.BlockSpec(memory_space=pl.ANY),
                      tile(d), tile(d), tile(LANES),
                      const((d, ff)), const((d, ff)), const((ff, d)), const((1, d)), const((1, d))],
            out_specs=tile(d),
            scratch_shapes=[pltpu.VMEM((TOP_K, tm, d), F32), pltpu.SemaphoreType.DMA(())]),
        out_shape=jax.ShapeDtypeStruct((s, d), F32),
        compiler_params=_params(("arbitrary",), disable_bounds_checks=True),
    )(dest_flat, ys, h2, x1, wcol, w_gate, w_up, w_down, gt2, gf)


def _mixer(x2d, mod, norm1_g, norm2_g, w_in, lb, hgrn_onorm_g, gdn_conv_w, gdn_a_log, gdn_dt_bias, gdn_onorm_g,
           w_branch_hgrn, w_branch_gdn, w_out, tiles):
    d = x2d.shape[1]
    sh1, sc1, gt1, sh2, sc2, _ = [mod[:, i * d:(i + 1) * d] for i in range(6)]
    key = HEADS * HEAD_DIM
    small0 = 4 * key + 3 * key
    small1 = small0 + 2 * HEADS
    w_main = jnp.concatenate([w_in[:, :small0], w_in[:, small1:]], axis=1).astype(BF16)
    w_small_t = w_in[:, small0:small1].T.astype(BF16)
    proj, ab_t = _inproj(x2d, norm1_g, sc1, sh1, w_main, w_small_t, tiles["in_tm"], tiles["in_tn"])
    o_a = _hgrn(proj, lb, hgrn_onorm_g, tiles["mix_ts"])
    o_b = _gdn(proj, gdn_conv_w, ab_t, gdn_a_log, gdn_dt_bias, gdn_onorm_g, tiles["mix_ts"])
    return _merge(o_a, o_b, proj, x2d, w_branch_hgrn.astype(BF16), w_branch_gdn.astype(BF16),
                  w_out.astype(BF16), gt1, norm2_g, sc2, sh2, tiles["merge_tm"])


def _moe(x1, h2, mod, norm2_g, normf_g, w_router, router_bias, w_exp_gate, w_exp_up, w_exp_down, w_sh_gate,
         w_sh_up, w_sh_down, tiles):
    s, d = x1.shape
    sh2, sc2, gt2 = [mod[:, i * d:(i + 1) * d] for i in (3, 4, 5)]
    idx_t, wts_t, rank_t, wcol, counts = _router(x1, norm2_g, sc2, sh2, w_router.T, router_bias.reshape(-1, 1),
                                                 tiles["route_tm"])
    del wts_t
    bm = EXPERT_BLOCK
    n_blocks = -(-(s * TOP_K + N_EXPERTS * (bm - 1)) // bm)
    counts = counts[:, 0]
    padded = (counts + bm - 1) // bm * bm
    pend = jnp.cumsum(padded)
    pstart = (pend - padded).astype(I32)
    block_e = jnp.minimum(jnp.searchsorted(pend, jnp.arange(n_blocks, dtype=I32) * bm, side="right"),
                          N_EXPERTS - 1).astype(I32)
    n_used = (pend[-1:] // bm).astype(I32)
    dest_flat = _dest(pstart, idx_t, rank_t, tiles["route_tm"]).reshape(-1)
    xs = _dispatch(dest_flat, h2.astype(F32), n_blocks * bm, tiles["disp_tm"])
    ys = _experts(block_e, n_used, xs, w_exp_gate.astype(BF16), w_exp_up.astype(BF16), w_exp_down.astype(BF16))
    return _combine(dest_flat, ys, h2, x1, wcol, w_sh_gate.astype(BF16), w_sh_up.astype(BF16),
                    w_sh_down.astype(BF16), gt2, normf_g, tiles["comb_tm"])


def _tiles(s):
    pick = lambda want: min(want, s)
    return dict(in_tm=pick(1024), in_tn=1024, mix_ts=pick(256), merge_tm=pick(512), route_tm=pick(512),
                disp_tm=pick(256), comb_tm=pick(128))


def kernel(x, c, w_ada, b_ada, norm1_g, norm2_g, w_in, hgrn_lb_table, hgrn_onorm_g, gdn_conv_w, gdn_a_log, gdn_dt_bias, gdn_onorm_g, w_branch_hgrn, w_branch_gdn, w_out, w_router, router_bias, w_exp_gate, w_exp_up, w_exp_down, w_sh_gate, w_sh_up, w_sh_down, normf_g):
    b, s, d = x.shape
    assert b == 1 and w_ada.shape[0] == 1, "one sequence, one layer"
    tiles = _tiles(s)
    lb = jnp.cumsum(jax.nn.softmax(hgrn_lb_table.astype(F32), axis=0), axis=0)[0:1]
    mod = _ada(c, w_ada[0], b_ada[0])
    row = lambda v: v.reshape(1, -1)
    x1, h2 = _mixer(x[0], mod, row(norm1_g[0]), row(norm2_g[0]), w_in[0], lb, row(hgrn_onorm_g[0]), gdn_conv_w[0],
                    gdn_a_log[0], gdn_dt_bias[0], row(gdn_onorm_g[0]), w_branch_hgrn[0], w_branch_gdn[0], w_out[0],
                    tiles)
    out = _moe(x1, h2, mod, row(norm2_g[0]), row(normf_g), w_router[0], router_bias[0], w_exp_gate[0],
               w_exp_up[0], w_exp_down[0], w_sh_gate[0], w_sh_up[0], w_sh_down[0], tiles)
    return out[None]
```

```python
import functools

import jax
import jax.numpy as jnp
from jax import lax
from jax.experimental import pallas as pl
from jax.experimental.pallas import tpu as pltpu

F32 = jnp.float32
BF16 = jnp.bfloat16
I32 = jnp.int32

NORM_EPS = 1e-6
L2_EPS = 1e-6
HEADS = 8
HEAD_DIM = 128
CONV_WIDTH = 4
CHUNK = 64
N_EXPERTS = 64
N_GROUPS = 8
GROUP_SIZE = N_EXPERTS // N_GROUPS
TOPK_GROUPS = 4
TOP_K = 8
ROUTED_SCALE = 2.5
EXPERT_BLOCK = 256

LANES = 128
VMEM_LIMIT = 56 * 1024 * 1024

NT = (((1,), (1,)), ((), ()))
TN = (((0,), (0,)), ((), ()))


def _params(sem, **kw):
    return pltpu.CompilerParams(dimension_semantics=sem, vmem_limit_bytes=VMEM_LIMIT, **kw)


def _dot(a, b):
    return jnp.dot(a, b, preferred_element_type=F32)


def _dg(a, b, dims):
    return lax.dot_general(a, b, dims, preferred_element_type=F32)


def _split(x):
    hi = x.astype(BF16)
    lo = (x - hi.astype(F32)).astype(BF16)
    return hi, lo


def _dot_exact_lhs(a_bf16, x, dims=None):
    hi, lo = _split(x)
    if dims is None:
        return _dot(a_bf16, hi) + _dot(a_bf16, lo)
    return _dg(a_bf16, hi, dims) + _dg(a_bf16, lo, dims)


def _dot_exact_rhs(x, b_bf16):
    hi, lo = _split(x)
    return _dot(hi, b_bf16) + _dot(lo, b_bf16)


def _dot3(a, b):
    ah, al = _split(a)
    bh, bl = _split(b)
    return _dot(ah, bh) + (_dot(ah, bl) + _dot(al, bh))


def _sigmoid(x):
    return 1.0 / (1.0 + jnp.exp(-x))


def _silu(x):
    return x * _sigmoid(x)


def _rms(x, eps):
    return x * lax.rsqrt(jnp.mean(x * x, axis=-1, keepdims=True) + eps)


def _iota2(shape, dim):
    return lax.broadcasted_iota(I32, shape, dim)


def _ada_kernel(c_ref, w_ref, b_ref, o_ref):
    cond = _silu(c_ref[...])
    o_ref[...] = jnp.dot(cond, w_ref[...], preferred_element_type=F32,
                         precision=lax.Precision.HIGHEST) + b_ref[...]


def _ada(c, w_ada, b_ada):
    d, n = w_ada.shape
    tn = 1024
    c8 = jnp.broadcast_to(c, (8, d))
    out = pl.pallas_call(
        _ada_kernel,
        grid=(n // tn,),
        in_specs=[pl.BlockSpec((8, d), lambda j: (0, 0)),
                  pl.BlockSpec((d, tn), lambda j: (0, j)),
                  pl.BlockSpec((1, tn), lambda j: (0, j))],
        out_specs=pl.BlockSpec((8, tn), lambda j: (0, j)),
        out_shape=jax.ShapeDtypeStruct((8, n), F32),
        compiler_params=_params(("arbitrary",)),
    )(c8, w_ada, b_ada.reshape(1, n))
    return out[0:1]


def _inproj_kernel(x_ref, g_ref, sc_ref, sh_ref, w_ref, wst_ref, proj_ref, smallt_ref, h_scr):
    @pl.when(pl.program_id(1) == 0)
    def _():
        h = _rms(x_ref[...], NORM_EPS) * g_ref[...] * (1.0 + sc_ref[...]) + sh_ref[...]
        hb = h.astype(BF16)
        h_scr[...] = hb
        smallt_ref[...] = _dg(wst_ref[...], hb, NT)

    proj_ref[...] = _dot(h_scr[...], w_ref[...]).astype(BF16)


def _inproj(x, g, sc, sh, w_main, w_small_t, tm, tn):
    s, d = x.shape
    n = w_main.shape[1]
    ns = w_small_t.shape[0]
    row = lambda i, j: (0, 0)
    return pl.pallas_call(
        _inproj_kernel,
        grid=(s // tm, n // tn),
        in_specs=[pl.BlockSpec((tm, d), lambda i, j: (i, 0)),
                  pl.BlockSpec((1, d), row), pl.BlockSpec((1, d), row), pl.BlockSpec((1, d), row),
                  pl.BlockSpec((d, tn), lambda i, j: (0, j)),
                  pl.BlockSpec((ns, d), row)],
        out_specs=[pl.BlockSpec((tm, tn), lambda i, j: (i, j)),
                   pl.BlockSpec((ns, tm), lambda i, j: (0, i))],
        out_shape=[jax.ShapeDtypeStruct((s, n), BF16), jax.ShapeDtypeStruct((ns, s), F32)],
        scratch_shapes=[pltpu.VMEM((tm, d), BF16)],
        compiler_params=_params(("arbitrary", "arbitrary")),
    )(x, g, sc, sh, w_main, w_small_t)


def _hgrn_kernel(q_ref, f_ref, i_ref, g_ref, lb_ref, on_ref, o_ref, st_scr, *, n_chunks):
    @pl.when(pl.program_id(1) == 0)
    def _():
        st_scr[...] = jnp.zeros_like(st_scr)

    c = CHUNK
    r = _iota2((c, c), 0)
    cidx = _iota2((c, c), 1)
    causal = cidx <= r
    tri = causal.astype(BF16)
    lb = lb_ref[...]
    on_g = on_ref[...]
    for n in range(n_chunks):
        rows = pl.ds(n * c, c)
        f = lb + (1.0 - lb) * _sigmoid(f_ref[rows, :].astype(F32))
        b = _dot_exact_lhs(tri, jnp.log(f))
        k = 1.0 - f
        q = _silu(q_ref[rows, :].astype(F32)) * (HEAD_DIM ** -0.5)
        v = i_ref[rows, :]
        b_mid = b[c // 2:c // 2 + 1, :]
        b_last = b[c - 1:c, :]
        qa = (q * jnp.exp(b - b_mid)).astype(BF16)
        ka = (k * jnp.exp(b_mid - b)).astype(BF16)
        scores = jnp.where(causal, _dg(qa, ka, NT), 0.0)
        o = _dot(scores.astype(BF16), v)
        st = st_scr[...]
        o = o + _dg((q * jnp.exp(b)).astype(BF16), st.astype(BF16), NT)
        k_upd = (k * jnp.exp(b_last - b)).astype(BF16)
        st_scr[...] = jnp.exp(b_last) * st + _dg(v, k_upd, TN)
        o = _rms(o, NORM_EPS) * on_g * _silu(g_ref[rows, :].astype(F32))
        o_ref[rows, :] = o.astype(BF16)


def _hgrn(proj, lb, onorm_g, ts):
    s = proj.shape[0]
    hd = HEAD_DIM
    col = lambda off: pl.BlockSpec((ts, hd), lambda h, j, off=off: (j, off + h))
    return pl.pallas_call(
        functools.partial(_hgrn_kernel, n_chunks=ts // CHUNK),
        grid=(HEADS, s // ts),
        in_specs=[col(0), col(HEADS), col(2 * HEADS), col(3 * HEADS),
                  pl.BlockSpec((1, hd), lambda h, j: (0, h)),
                  pl.BlockSpec((1, hd), lambda h, j: (0, 0))],
        out_specs=pl.BlockSpec((ts, hd), lambda h, j: (j, h)),
        out_shape=jax.ShapeDtypeStruct((s, HEADS * hd), BF16),
        scratch_shapes=[pltpu.VMEM((hd, hd), F32)],
        compiler_params=_params(("arbitrary", "arbitrary")),
    )(proj, proj, proj, proj, lb, onorm_g)


def _gdn_kernel(q_ref, k_ref, v_ref, qp_ref, kp_ref, vp_ref, wq_ref, wk_ref, wv_ref, ab_ref, alog_ref,
                dtb_ref, g_ref, on_ref, o_ref, st_scr, cat_scr, *, n_chunks, ts):
    h = pl.program_id(0)
    first = pl.program_id(1) == 0

    @pl.when(first)
    def _():
        st_scr[...] = jnp.zeros_like(st_scr)

    def conv_silu(cur_ref, prev_ref, w_ref):
        cat_scr[0:8, :] = jnp.where(first, 0.0, prev_ref[...].astype(F32))
        cat_scr[8:8 + ts, :] = cur_ref[...].astype(F32)
        acc = None
        for j in range(CONV_WIDTH):
            off = 8 - (CONV_WIDTH - 1) + j
            term = cat_scr[off:off + ts, :] * w_ref[j:j + 1, :]
            acc = term if acc is None else acc + term
        return _silu(acc)

    def l2n(x):
        return x * lax.rsqrt(jnp.sum(x * x, axis=-1, keepdims=True) + L2_EPS)

    q_all = l2n(conv_silu(q_ref, qp_ref, wq_ref)) * (HEAD_DIM ** -0.5)
    k_all = l2n(conv_silu(k_ref, kp_ref, wk_ref))
    v_all = conv_silu(v_ref, vp_ref, wv_ref)

    a_row = ab_ref[0]
    b_row = ab_ref[1]
    a_log = jnp.full(a_row.shape, alog_ref[h], F32)
    z = a_row + dtb_ref[h]
    softplus = jnp.maximum(z, 0.0) + jnp.log(1.0 + jnp.exp(-jnp.abs(z)))
    ld_all = -jnp.exp(a_log) * softplus
    beta_all = _sigmoid(b_row)

    c = CHUNK
    r = _iota2((c, c), 0)
    cidx = _iota2((c, c), 1)
    causal = cidx <= r
    strict = cidx < r
    tri = causal.astype(BF16)
    tri_t = (r <= cidx).astype(BF16)
    eye = (r == cidx)
    eye_b = eye.astype(BF16)
    eye_f = eye.astype(F32)
    on_g = on_ref[...]

    for n in range(n_chunks):
        lo, hi = n * c, (n + 1) * c
        q = q_all[lo:hi]
        k = k_all[lo:hi]
        v = v_all[lo:hi]
        ld = jnp.broadcast_to(ld_all[:, lo:hi], (HEAD_DIM, c))
        beta = jnp.broadcast_to(beta_all[:, lo:hi], (HEAD_DIM, c))
        gc = _dot_exact_lhs(tri, ld, NT)
        g_row = _dot_exact_rhs(ld[0:8], tri_t)[0:1]
        bc = _dot_exact_lhs(eye_b, beta, NT)
        g_last = gc[c - 1:c, :]
        diff = gc[:, 0:c] - g_row
        dm = jnp.where(causal, jnp.exp(jnp.where(causal, diff, 0.0)), 0.0)
        kb16 = k.astype(BF16)
        kq = _dg(jnp.concatenate([kb16, q.astype(BF16)], axis=0), kb16, NT)
        a_mat = jnp.where(strict, bc[:, 0:c] * kq[0:c] * dm, 0.0)
        attn = kq[c:2 * c] * dm
        bm = -a_mat
        p = eye_f + bm
        bm = _dot3(bm, bm)
        for _ in range(c.bit_length() - 3):
            res = _dot3(bm, jnp.concatenate([bm, p], axis=1))
            p = p + res[:, c:2 * c]
            bm = res[:, 0:c]
        p = p + _dot3(bm, p)
        egc = jnp.exp(gc)
        rhs = jnp.concatenate([v * bc, k * (bc * egc)], axis=1)
        sol = _dot3(p, rhs)
        u = sol[:, 0:HEAD_DIM]
        w = sol[:, HEAD_DIM:2 * HEAD_DIM]
        q_dec = q * egc
        k_upd = k * jnp.exp(g_last - gc)
        st = st_scr[...]
        wq = _dot(jnp.concatenate([w.astype(BF16), q_dec.astype(BF16)], axis=0), st.astype(BF16))
        v_new = u - wq[0:c]
        vn16 = v_new.astype(BF16)
        o = wq[c:2 * c] + _dot(attn.astype(BF16), vn16)
        st_scr[...] = jnp.exp(g_last) * st + _dg(k_upd.astype(BF16), vn16, TN)
        o = _rms(o, NORM_EPS) * on_g * _silu(g_ref[lo:hi, :].astype(F32))
        o_ref[lo:hi, :] = o.astype(BF16)


def _gdn(proj, conv_w, ab_t, a_log, dt_bias, onorm_g, ts):
    s = proj.shape[0]
    hd = HEAD_DIM
    q0 = 4 * HEADS
    cur = lambda off: pl.BlockSpec((ts, hd), lambda h, j, off=off: (j, off + h))
    prev = lambda off: pl.BlockSpec((8, hd), lambda h, j, off=off: (jnp.maximum(j * (ts // 8) - 1, 0), off + h))
    cw = lambda off: pl.BlockSpec((CONV_WIDTH, hd), lambda h, j, off=off: (0, off + h))
    smem = pl.BlockSpec(memory_space=pltpu.SMEM)
    return pl.pallas_call(
        functools.partial(_gdn_kernel, n_chunks=ts // CHUNK, ts=ts),
        grid=(HEADS, s // ts),
        in_specs=[cur(q0), cur(q0 + HEADS), cur(q0 + 2 * HEADS),
                  prev(q0), prev(q0 + HEADS), prev(q0 + 2 * HEADS),
                  cw(0), cw(HEADS), cw(2 * HEADS),
                  pl.BlockSpec((2, None, 1, ts), lambda h, j: (0, h, 0, j)),
                  smem, smem,
                  cur(q0 + 3 * HEADS),
                  pl.BlockSpec((1, hd), lambda h, j: (0, 0))],
        out_specs=pl.BlockSpec((ts, hd), lambda h, j: (j, h)),
        out_shape=jax.ShapeDtypeStruct((s, HEADS * hd), BF16),
        scratch_shapes=[pltpu.VMEM((hd, hd), F32), pltpu.VMEM((ts + 8, hd), F32)],
        compiler_params=_params(("arbitrary", "arbitrary")),
    )(proj, proj, proj, proj, proj, proj, conv_w, conv_w, conv_w,
      ab_t.reshape(2, HEADS, 1, s), a_log, dt_bias, proj, onorm_g)


def _hgrn_all_kernel(q_ref, f_ref, i_ref, g_ref, lb_ref, on_ref, o_ref, st_scr, *, n_chunks):
    @pl.when(pl.program_id(0) == 0)
    def _():
        st_scr[...] = jnp.zeros_like(st_scr)

    c = CHUNK
    hd = HEAD_DIM
    causal = _iota2((c, c), 1) <= _iota2((c, c), 0)
    tri = causal.astype(BF16)
    lb = lb_ref[...]
    on_g = on_ref[...]
    heads = [slice(h * hd, (h + 1) * hd) for h in range(HEADS)]

    def chunk(n, carry):
        rows = pl.ds(pl.multiple_of(n * c, c), c)
        f = lb + (1.0 - lb) * _sigmoid(f_ref[rows, :].astype(F32))
        b = _dot_exact_lhs(tri, jnp.log(f))
        k = 1.0 - f
        q = _silu(q_ref[rows, :].astype(F32)) * (hd ** -0.5)
        v = i_ref[rows, :]
        b_mid = b[c // 2:c // 2 + 1, :]
        b_last = b[c - 1:c, :]
        qa = (q * jnp.exp(b - b_mid)).astype(BF16)
        ka = (k * jnp.exp(b_mid - b)).astype(BF16)
        qi = (q * jnp.exp(b)).astype(BF16)
        ku = (k * jnp.exp(b_last - b)).astype(BF16)
        dec = jnp.exp(b_last)
        gate = on_g * _silu(g_ref[rows, :].astype(F32))
        sts = [st_scr[h] for h in range(HEADS)]
        scores = [jnp.where(causal, _dg(qa[:, sl], ka[:, sl], NT), 0.0).astype(BF16) for sl in heads]
        inter = [_dg(qi[:, sl], st.astype(BF16), NT) for sl, st in zip(heads, sts)]
        kv = [_dg(v[:, sl], ku[:, sl], TN) for sl in heads]
        for h, sl in enumerate(heads):
            st_scr[h] = dec[:, sl] * sts[h] + kv[h]
        outs = [_rms(_dot(sc, v[:, sl]) + it, NORM_EPS) for sc, sl, it in zip(scores, heads, inter)]
        o_ref[rows, :] = (jnp.concatenate(outs, axis=1) * gate).astype(BF16)
        return carry

    lax.fori_loop(0, n_chunks, chunk, 0)


def _hgrn_all(proj, lb, onorm_g, ts):
    s = proj.shape[0]
    width = HEADS * HEAD_DIM
    col = lambda blk: pl.BlockSpec((ts, width), lambda j, blk=blk: (j, blk))
    const = pl.BlockSpec((1, width), lambda j: (0, 0))
    return pl.pallas_call(
        functools.partial(_hgrn_all_kernel, n_chunks=ts // CHUNK),
        grid=(s // ts,),
        in_specs=[col(0), col(1), col(2), col(3), const, const],
        out_specs=pl.BlockSpec((ts, width), lambda j: (j, 0)),
        out_shape=jax.ShapeDtypeStruct((s, width), BF16),
        scratch_shapes=[pltpu.VMEM((HEADS, HEAD_DIM, HEAD_DIM), F32)],
        compiler_params=_params(("arbitrary",)),
    )(proj, proj, proj, proj, lb, jnp.tile(onorm_g, (1, HEADS)))


def _gdn_prep_kernel(q_ref, k_ref, v_ref, qp_ref, kp_ref, vp_ref, wq_ref, wk_ref, wv_ref, ab_ref, alog_ref,
                     dtb_ref, tri_ref, eye_ref, u_ref, wqd_ref, ku_ref, attn_ref, dl_ref, cat_scr, *, n_chunks, ts):
    h = pl.program_id(0)
    first = pl.program_id(1) == 0
    c = CHUNK
    hd = HEAD_DIM

    def conv_silu(cur_ref, prev_ref, w_ref):
        cat_scr[0:8, :] = jnp.where(first, 0.0, prev_ref[...].astype(F32))
        cat_scr[8:8 + ts, :] = cur_ref[...].astype(F32)
        acc = None
        for j in range(CONV_WIDTH):
            off = 8 - (CONV_WIDTH - 1) + j
            term = cat_scr[off:off + ts, :] * w_ref[j:j + 1, :]
            acc = term if acc is None else acc + term
        return _silu(acc)

    def l2n(x):
        return x * lax.rsqrt(jnp.sum(x * x, axis=-1, keepdims=True) + L2_EPS)

    q_all = l2n(conv_silu(q_ref, qp_ref, wq_ref)) * (hd ** -0.5)
    k_all = l2n(conv_silu(k_ref, kp_ref, wk_ref))
    v_all = conv_silu(v_ref, vp_ref, wv_ref)

    a_row = ab_ref[0]
    b_row = ab_ref[1]
    z = a_row + dtb_ref[h]
    softplus = jnp.maximum(z, 0.0) + jnp.log(1.0 + jnp.exp(-jnp.abs(z)))
    ld_row = -jnp.exp(jnp.full(a_row.shape, alog_ref[h], F32)) * softplus
    beta_row = _sigmoid(b_row)
    tri_blocks = tri_ref[...]
    ld_b = jnp.broadcast_to(ld_row, (hd, ts))
    gc_all = _dot_exact_lhs(tri_blocks, ld_b, NT)
    hi, lo = _split(ld_b[0:8])
    g_rows = _dg(hi, tri_blocks, NT) + _dg(lo, tri_blocks, NT)
    bc_all = _dot_exact_lhs(eye_ref[...], jnp.broadcast_to(beta_row, (hd, ts)), NT)
    egc_all = jnp.exp(gc_all)

    r = _iota2((c, c), 0)
    cidx = _iota2((c, c), 1)
    causal = cidx <= r
    strict = cidx < r
    eye_f = (r == cidx).astype(F32)
    chunks = [slice(n * c, (n + 1) * c) for n in range(n_chunks)]

    q16 = q_all.astype(BF16)
    k16 = k_all.astype(BF16)
    kq = [_dg(jnp.concatenate([k16[sl], q16[sl]], axis=0), k16[sl], NT) for sl in chunks]
    dm = []
    for sl in chunks:
        diff = gc_all[sl, 0:c] - g_rows[0:1, sl]
        dm.append(jnp.where(causal, jnp.exp(jnp.where(causal, diff, 0.0)), 0.0))
    bm = [-jnp.where(strict, bc_all[sl, 0:c] * x[0:c] * d, 0.0) for sl, x, d in zip(chunks, kq, dm)]
    p = [eye_f + b for b in bm]
    bm = [_dot(b.astype(BF16), b.astype(BF16)) for b in bm]
    for _ in range(c.bit_length() - 3):
        res = [_dot(b.astype(BF16), jnp.concatenate([b, pp], axis=1).astype(BF16)) for b, pp in zip(bm, p)]
        p = [pp + x[:, c:2 * c] for pp, x in zip(p, res)]
        bm = [x[:, 0:c] for x in res]
    p = [pp + _dot(b.astype(BF16), pp.astype(BF16)) for b, pp in zip(bm, p)]
    rhs = jnp.concatenate([v_all * bc_all, k_all * (bc_all * egc_all)], axis=1).astype(BF16)
    sol = [_dot(pp.astype(BF16), rhs[sl]) for pp, sl in zip(p, chunks)]
    qd_all = (q_all * egc_all).astype(BF16)
    for n, sl in enumerate(chunks):
        g_last = gc_all[(n + 1) * c - 1:(n + 1) * c, :]
        u_ref[sl, :] = sol[n][:, 0:hd].astype(BF16)
        wqd_ref[2 * n * c:(2 * n + 1) * c, :] = sol[n][:, hd:2 * hd].astype(BF16)
        wqd_ref[(2 * n + 1) * c:(2 * n + 2) * c, :] = qd_all[sl]
        ku_ref[sl, :] = (k_all[sl] * jnp.exp(g_last - gc_all[sl])).astype(BF16)
        attn_ref[sl, :] = (kq[n][c:2 * c] * dm[n]).astype(BF16)
        dl_ref[n:n + 1, :] = jnp.exp(g_last)


def _gdn_prep(proj, conv_w, ab_t, a_log, dt_bias, ts):
    s = proj.shape[0]
    hd = HEAD_DIM
    c = CHUNK
    q0 = 4 * HEADS
    cur = lambda off: pl.BlockSpec((ts, hd), lambda h, j, off=off: (j, off + h))
    prev = lambda off: pl.BlockSpec((8, hd), lambda h, j, off=off: (jnp.maximum(j * (ts // 8) - 1, 0), off + h))
    cw = lambda off: pl.BlockSpec((CONV_WIDTH, hd), lambda h, j, off=off: (0, off + h))
    smem = pl.BlockSpec(memory_space=pltpu.SMEM)
    const = pl.BlockSpec((ts, ts), lambda h, j: (0, 0))
    pos = jnp.arange(ts)
    tri_blocks = ((pos[:, None] // c == pos[None, :] // c) & (pos[None, :] <= pos[:, None])).astype(BF16)
    eye = (pos[:, None] == pos[None, :]).astype(BF16)
    per_head = lambda rows, cols: pl.BlockSpec((None, rows, cols), lambda h, j: (h, j, 0))
    return pl.pallas_call(
        functools.partial(_gdn_prep_kernel, n_chunks=ts // c, ts=ts),
        grid=(HEADS, s // ts),
        in_specs=[cur(q0), cur(q0 + HEADS), cur(q0 + 2 * HEADS),
                  prev(q0), prev(q0 + HEADS), prev(q0 + 2 * HEADS),
                  cw(0), cw(HEADS), cw(2 * HEADS),
                  pl.BlockSpec((2, None, 1, ts), lambda h, j: (0, h, 0, j)),
                  smem, smem, const, const],
        out_specs=[pl.BlockSpec((ts, hd), lambda h, j: (j, h)),
                   pl.BlockSpec((2 * ts, hd), lambda h, j: (j, h)),
                   pl.BlockSpec((ts, hd), lambda h, j: (j, h)),
                   per_head(ts, c),
                   per_head(ts // c, hd)],
        out_shape=[jax.ShapeDtypeStruct((s, HEADS * hd), BF16),
                   jax.ShapeDtypeStruct((2 * s, HEADS * hd), BF16),
                   jax.ShapeDtypeStruct((s, HEADS * hd), BF16),
                   jax.ShapeDtypeStruct((HEADS, s, c), BF16),
                   jax.ShapeDtypeStruct((HEADS, s // c, hd), F32)],
        scratch_shapes=[pltpu.VMEM((ts + 8, hd), F32)],
        compiler_params=_params(("arbitrary", "arbitrary")),
    )(proj, proj, proj, proj, proj, proj, conv_w, conv_w, conv_w,
      ab_t.reshape(2, HEADS, 1, s), a_log, dt_bias, tri_blocks, eye)


def _gdn_scan_kernel(u_ref, wqd_ref, ku_ref, attn_ref, dl_ref, g_ref, on_ref, o_ref, st_scr, *, n_chunks):
    @pl.when(pl.program_id(0) == 0)
    def _():
        st_scr[...] = jnp.zeros_like(st_scr)

    c = CHUNK
    hd = HEAD_DIM
    on_g = on_ref[...]
    heads = [slice(h * hd, (h + 1) * hd) for h in range(HEADS)]

    def chunk(n, carry):
        rows = pl.ds(pl.multiple_of(n * c, c), c)
        rows2 = pl.ds(pl.multiple_of(2 * n * c, 2 * c), 2 * c)
        sts = [st_scr[h] for h in range(HEADS)]
        wq = [_dot(wqd_ref[rows2, sl], st.astype(BF16)) for sl, st in zip(heads, sts)]
        vn = [(u_ref[rows, sl].astype(F32) - x[0:c]).astype(BF16) for sl, x in zip(heads, wq)]
        upd = [_dg(ku_ref[rows, sl], v, TN) for sl, v in zip(heads, vn)]
        for h in range(HEADS):
            st_scr[h] = dl_ref[h, pl.ds(n, 1), :] * sts[h] + upd[h]
        outs = [_rms(x[c:2 * c] + _dot(attn_ref[h, rows, :], v), NORM_EPS)
                for h, (x, v) in enumerate(zip(wq, vn))]
        gate = jnp.tile(on_g, (1, HEADS)) * _silu(g_ref[rows, :].astype(F32))
        o_ref[rows, :] = (jnp.concatenate(outs, axis=1) * gate).astype(BF16)
        return carry

    lax.fori_loop(0, n_chunks, chunk, 0)


def _gdn_scan(u, wqd, ku, attn, dl, proj, onorm_g, ts):
    s, width = u.shape
    c = CHUNK
    gate_blk = (4 * HEADS + 3 * HEADS) * HEAD_DIM // width
    return pl.pallas_call(
        functools.partial(_gdn_scan_kernel, n_chunks=ts // c),
        grid=(s // ts,),
        in_specs=[pl.BlockSpec((ts, width), lambda j: (j, 0)),
                  pl.BlockSpec((2 * ts, width), lambda j: (j, 0)),
                  pl.BlockSpec((ts, width), lambda j: (j, 0)),
                  pl.BlockSpec((HEADS, ts, c), lambda j: (0, j, 0)),
                  pl.BlockSpec((HEADS, ts // c, HEAD_DIM), lambda j: (0, j, 0)),
                  pl.BlockSpec((ts, width), lambda j: (j, gate_blk)),
                  pl.BlockSpec((1, HEAD_DIM), lambda j: (0, 0))],
        out_specs=pl.BlockSpec((ts, width), lambda j: (j, 0)),
        out_shape=jax.ShapeDtypeStruct((s, width), BF16),
        scratch_shapes=[pltpu.VMEM((HEADS, HEAD_DIM, HEAD_DIM), F32)],
        compiler_params=_params(("arbitrary",)),
    )(u, wqd, ku, attn, dl, proj, onorm_g)


def _merge_kernel(oa_ref, ob_ref, mga_ref, mgb_ref, x_ref, wa_ref, wb_ref, wo_ref, gt_ref, g2_ref, sc_ref,
                  sh_ref, x1_ref, h2_ref):
    ya = _dot(oa_ref[...], wa_ref[...])
    yb = _dot(ob_ref[...], wb_ref[...])
    merged = _sigmoid(mga_ref[...].astype(F32)) * ya + _sigmoid(mgb_ref[...].astype(F32)) * yb
    x1 = x_ref[...] + gt_ref[...] * _dot(merged.astype(BF16), wo_ref[...])
    x1_ref[...] = x1
    h2 = _rms(x1, NORM_EPS) * g2_ref[...] * (1.0 + sc_ref[...]) + sh_ref[...]
    h2_ref[...] = h2.astype(BF16)


def _merge(o_a, o_b, proj, x, w_a, w_b, w_o, gt1, g2, sc2, sh2, tm):
    s, d = x.shape
    dv = o_a.shape[1]
    mg0 = (8 * HEADS * HEAD_DIM) // d
    const = lambda shape: pl.BlockSpec(shape, lambda i: (0, 0), pipeline_mode=pl.Buffered(1))
    return pl.pallas_call(
        _merge_kernel,
        grid=(s // tm,),
        in_specs=[pl.BlockSpec((tm, dv), lambda i: (i, 0)),
                  pl.BlockSpec((tm, dv), lambda i: (i, 0)),
                  pl.BlockSpec((tm, d), lambda i: (i, mg0)),
                  pl.BlockSpec((tm, d), lambda i: (i, mg0 + 1)),
                  pl.BlockSpec((tm, d), lambda i: (i, 0)),
                  const((dv, d)), const((dv, d)), const((d, d)),
                  const((1, d)), const((1, d)), const((1, d)), const((1, d))],
        out_specs=[pl.BlockSpec((tm, d), lambda i: (i, 0)), pl.BlockSpec((tm, d), lambda i: (i, 0))],
        out_shape=[jax.ShapeDtypeStruct((s, d), F32), jax.ShapeDtypeStruct((s, d), BF16)],
        compiler_params=_params(("arbitrary",)),
    )(o_a, o_b, proj, proj, x, w_a, w_b, w_o, gt1, g2, sc2, sh2)


def _first_max(vals, iota, size, axis):
    m = jnp.max(vals, axis=axis, keepdims=True)
    idx = jnp.min(jnp.where(vals == m, iota, size), axis=axis, keepdims=True)
    return m, idx


def _router_kernel(x1_ref, g2_ref, sc_ref, sh_ref, wrt_ref, bias_ref, upper_ref, idx_ref, wts_ref, rank_ref,
                   wcol_ref, cnt_ref, cnt_scr, *, tm):
    @pl.when(pl.program_id(0) == 0)
    def _():
        cnt_scr[...] = jnp.zeros_like(cnt_scr)

    e = N_EXPERTS
    h2 = _rms(x1_ref[...], NORM_EPS) * g2_ref[...] * (1.0 + sc_ref[...]) + sh_ref[...]
    logits = lax.dot_general(wrt_ref[...], h2, NT, preferred_element_type=F32,
                             precision=lax.Precision.HIGHEST)
    scores = _sigmoid(logits)
    biased = scores + bias_ref[...]
    neg = -jnp.inf

    g3 = biased.reshape(N_GROUPS, GROUP_SIZE, tm)
    i3 = lax.broadcasted_iota(I32, g3.shape, 1)
    m1, a1 = _first_max(g3, i3, GROUP_SIZE, 1)
    m2 = jnp.max(jnp.where(i3 == a1, neg, g3), axis=1, keepdims=True)
    gs = (m1 + m2).reshape(N_GROUPS, tm)
    ig = _iota2(gs.shape, 0)
    gmask = jnp.zeros(gs.shape, jnp.bool_)
    for _ in range(TOPK_GROUPS):
        _, a = _first_max(gs, ig, N_GROUPS, 0)
        pick = ig == a
        gmask = jnp.logical_or(gmask, pick)
        gs = jnp.where(pick, neg, gs)
    emask = jnp.broadcast_to(gmask.reshape(N_GROUPS, 1, tm), (N_GROUPS, GROUP_SIZE, tm)).reshape(e, tm)

    cand = jnp.where(emask, biased, neg)
    ie = _iota2((e, tm), 0)
    sel_all = jnp.zeros((e, tm), jnp.bool_)
    idx_rows, w_rows, picks = [], [], []
    for _ in range(TOP_K):
        _, a = _first_max(cand, ie, e, 0)
        pick = ie == a
        picks.append(pick)
        idx_rows.append(a)
        w_rows.append(jnp.sum(jnp.where(pick, scores, 0.0), axis=0, keepdims=True))
        sel_all = jnp.logical_or(sel_all, pick)
        cand = jnp.where(pick, neg, cand)
    w_sum = w_rows[0]
    for wr in w_rows[1:]:
        w_sum = w_sum + wr
    wts = jnp.concatenate(w_rows, axis=0) / w_sum * ROUTED_SCALE
    idx_ref[...] = jnp.concatenate(idx_rows, axis=0)
    wts_ref[...] = wts

    sel = sel_all.astype(BF16)
    rank = _dot(sel, upper_ref[...]) + cnt_scr[...]
    rank_rows = [jnp.sum(jnp.where(pk, rank, 0.0), axis=0, keepdims=True) for pk in picks]
    rank_ref[...] = jnp.concatenate(rank_rows, axis=0).astype(I32)
    cnt_scr[...] = cnt_scr[...] + jnp.sum(sel_all.astype(F32), axis=1, keepdims=True)
    cnt_ref[...] = jnp.broadcast_to(cnt_scr[...], cnt_ref.shape).astype(I32)

    eye = (_iota2((tm, tm), 0) == _iota2((tm, tm), 1)).astype(BF16)
    w_pad = jnp.concatenate([wts, jnp.zeros((LANES - TOP_K, tm), F32)], axis=0)
    hi, lo = _split(w_pad)
    lo2 = (w_pad - hi.astype(F32) - lo.astype(F32)).astype(BF16)
    wcol_ref[...] = _dg(eye, hi, NT) + _dg(eye, lo, NT) + _dg(eye, lo2, NT)


def _router(x1, g2, sc2, sh2, w_router_t, bias_col, tm):
    s, d = x1.shape
    e = N_EXPERTS
    upper = (jnp.arange(tm)[:, None] < jnp.arange(tm)[None, :]).astype(BF16)
    const = lambda shape: pl.BlockSpec(shape, lambda i: (0, 0))
    tok = lambda rows: pl.BlockSpec((rows, tm), lambda i: (0, i))
    return pl.pallas_call(
        functools.partial(_router_kernel, tm=tm),
        grid=(s // tm,),
        in_specs=[pl.BlockSpec((tm, d), lambda i: (i, 0)),
                  const((1, d)), const((1, d)), const((1, d)),
                  const((e, d)), const((e, 1)), const((tm, tm))],
        out_specs=[tok(TOP_K), tok(TOP_K), tok(TOP_K),
                   pl.BlockSpec((tm, LANES), lambda i: (i, 0)),
                   const((e, LANES))],
        out_shape=[jax.ShapeDtypeStruct((TOP_K, s), I32), jax.ShapeDtypeStruct((TOP_K, s), F32),
                   jax.ShapeDtypeStruct((TOP_K, s), I32), jax.ShapeDtypeStruct((s, LANES), F32),
                   jax.ShapeDtypeStruct((e, LANES), I32)],
        scratch_shapes=[pltpu.VMEM((e, 1), F32)],
        compiler_params=_params(("arbitrary",)),
    )(x1, g2, sc2, sh2, w_router_t, bias_col, upper)


def _dest_kernel(pstart_ref, idx_ref, rank_ref, dest_ref):
    idx = idx_ref[...]
    base = jnp.zeros(idx.shape, I32)
    for ex in range(N_EXPERTS):
        base = jnp.where(idx == ex, pstart_ref[ex], base)
    dest_ref[...] = base + rank_ref[...]


def _dest(pstart, idx_t, rank_t, tm):
    k, s = idx_t.shape
    tok = pl.BlockSpec((k, tm), lambda i: (0, i))
    return pl.pallas_call(
        _dest_kernel,
        grid=(s // tm,),
        in_specs=[pl.BlockSpec(memory_space=pltpu.SMEM), tok, tok],
        out_specs=tok,
        out_shape=jax.ShapeDtypeStruct((k, s), I32),
        compiler_params=_params(("arbitrary",)),
    )(pstart, idx_t, rank_t)


def _dispatch_kernel(dest_ref, pad_lo_ref, pad_hi_ref, h_ref, xs_ref, hf_scr, zero_scr, sem, pad_sem, *, tm, s):
    step = pl.program_id(0)
    base = step * tm
    hf_scr[...] = h_ref[...].astype(F32)

    def pad_copy(slot):
        return pltpu.make_async_copy(zero_scr.at[pl.ds(0, 1), :], xs_ref.at[pl.ds(slot, 1), :], pad_sem)

    def for_each_pad_row(fn):
        def per_expert(ex, carry):
            def body(slot, inner):
                fn(pad_copy(slot))
                return inner
            return lax.fori_loop(pad_lo_ref[ex], pad_hi_ref[ex], body, carry)
        lax.fori_loop(0, N_EXPERTS, per_expert, 0)

    @pl.when(step == 0)
    def _():
        zero_scr[...] = jnp.zeros_like(zero_scr)
        for_each_pad_row(lambda cp: cp.start())

    def row_copy(t, k):
        slot = dest_ref[k * s + base + t]
        return pltpu.make_async_copy(hf_scr.at[pl.ds(t, 1), :], xs_ref.at[pl.ds(slot, 1), :], sem)

    def start(t, carry):
        for k in range(TOP_K):
            row_copy(t, k).start()
        return carry

    lax.fori_loop(0, tm, start, 0)

    def wait(t, carry):
        for k in range(TOP_K):
            row_copy(t, k).wait()
        return carry

    lax.fori_loop(0, tm, wait, 0)

    @pl.when(step == 0)
    def _():
        for_each_pad_row(lambda cp: cp.wait())


def _dispatch(dest_flat, pad_lo, pad_hi, h2, n_rows, tm):
    s, d = h2.shape
    return pl.pallas_call(
        functools.partial(_dispatch_kernel, tm=tm, s=s),
        grid_spec=pltpu.PrefetchScalarGridSpec(
            num_scalar_prefetch=3,
            grid=(s // tm,),
            in_specs=[pl.BlockSpec((tm, d), lambda i, *_: (i, 0))],
            out_specs=pl.BlockSpec(memory_space=pl.ANY),
            scratch_shapes=[pltpu.VMEM((tm, d), F32), pltpu.VMEM((8, d), F32),
                            pltpu.SemaphoreType.DMA(()), pltpu.SemaphoreType.DMA(())]),
        out_shape=jax.ShapeDtypeStruct((n_rows, d), F32),
        compiler_params=_params(("arbitrary",), has_side_effects=True, disable_bounds_checks=True),
    )(dest_flat, pad_lo, pad_hi, h2)


def _expert_kernel(be_ref, nu_ref, x_ref, wg_ref, wu_ref, wd_ref, y_ref, wg_scr, wu_scr, wd_scr):
    b = pl.program_id(0)
    active = b < nu_ref[0]
    new_expert = jnp.logical_or(b == 0, be_ref[b] != be_ref[jnp.maximum(b - 1, 0)])

    @pl.when(jnp.logical_and(active, new_expert))
    def _():
        wg_scr[...] = wg_ref[...].astype(BF16)
        wu_scr[...] = wu_ref[...].astype(BF16)
        wd_scr[...] = wd_ref[...].astype(BF16)

    @pl.when(active)
    def _():
        xb = x_ref[...].astype(BF16)
        hid = _silu(_dot(xb, wg_scr[...])) * _dot(xb, wu_scr[...])
        y_ref[...] = _dot(hid.astype(BF16), wd_scr[...])


def _experts(block_e, n_used, xs, w_gate, w_up, w_down):
    p, d = xs.shape
    ff = w_gate.shape[2]
    bm = EXPERT_BLOCK
    last = lambda b, nu: jnp.minimum(b, nu[0] - 1)
    blk = lambda b, be, nu: (last(b, nu), 0)
    wsel = lambda b, be, nu: (be[last(b, nu)], 0, 0)
    return pl.pallas_call(
        _expert_kernel,
        grid_spec=pltpu.PrefetchScalarGridSpec(
            num_scalar_prefetch=2,
            grid=(p // bm,),
            in_specs=[pl.BlockSpec((bm, d), blk),
                      pl.BlockSpec((None, d, ff), wsel),
                      pl.BlockSpec((None, d, ff), wsel),
                      pl.BlockSpec((None, ff, d), wsel)],
            out_specs=pl.BlockSpec((bm, d), blk),
            scratch_shapes=[pltpu.VMEM((d, ff), BF16), pltpu.VMEM((d, ff), BF16), pltpu.VMEM((ff, d), BF16)]),
        out_shape=jax.ShapeDtypeStruct((p, d), F32),
        compiler_params=_params(("arbitrary",)),
    )(block_e, n_used, xs, w_gate, w_up, w_down)


def _combine_kernel(dest_ref, ys_ref, h_ref, x1_ref, wcol_ref, wg_ref, wu_ref, wd_ref, gt_ref, gf_ref, o_ref,
                    rows_scr, sem, *, tm, s):
    base = pl.program_id(0) * tm

    def row_copy(t, k):
        slot = dest_ref[k * s + base + t]
        return pltpu.make_async_copy(ys_ref.at[pl.ds(slot, 1), :], rows_scr.at[k, pl.ds(t, 1), :], sem)

    def start(t, carry):
        for k in range(TOP_K):
            row_copy(t, k).start()
        return carry

    lax.fori_loop(0, tm, start, 0)

    hb = h_ref[...]
    hid = _silu(_dot(hb, wg_ref[...])) * _dot(hb, wu_ref[...])
    acc = _dot(hid.astype(BF16), wd_ref[...])

    def wait(t, carry):
        for k in range(TOP_K):
            row_copy(t, k).wait()
        return carry

    lax.fori_loop(0, tm, wait, 0)

    wcol = wcol_ref[...]
    for k in range(TOP_K):
        acc = acc + rows_scr[k] * wcol[:, k:k + 1]
    x2 = x1_ref[...] + gt_ref[...] * acc
    o_ref[...] = _rms(x2, NORM_EPS) * gf_ref[...]


def _combine(dest_flat, ys, h2, x1, wcol, w_gate, w_up, w_down, gt2, gf, tm):
    s, d = x1.shape
    ff = w_gate.shape[1]
    const = lambda shape: pl.BlockSpec(shape, lambda i, dest: (0, 0))
    tile = lambda cols: pl.BlockSpec((tm, cols), lambda i, dest: (i, 0))
    return pl.pallas_call(
        functools.partial(_combine_kernel, tm=tm, s=s),
        grid_spec=pltpu.PrefetchScalarGridSpec(
            num_scalar_prefetch=1,
            grid=(s // tm,),
            in_specs=[pl.BlockSpec(memory_space=pl.ANY),
                      tile(d), tile(d), tile(LANES),
                      const((d, ff)), const((d, ff)), const((ff, d)), const((1, d)), const((1, d))],
            out_specs=tile(d),
            scratch_shapes=[pltpu.VMEM((TOP_K, tm, d), F32), pltpu.SemaphoreType.DMA(())]),
        out_shape=jax.ShapeDtypeStruct((s, d), F32),
        compiler_params=_params(("arbitrary",), disable_bounds_checks=True),
    )(dest_flat, ys, h2, x1, wcol, w_gate, w_up, w_down, gt2, gf)


def _mixer(x2d, mod, norm1_g, norm2_g, w_in, lb, hgrn_onorm_g, gdn_conv_w, gdn_a_log, gdn_dt_bias, gdn_onorm_g,
           w_branch_hgrn, w_branch_gdn, w_out, tiles):
    d = x2d.shape[1]
    sh1, sc1, gt1, sh2, sc2, _ = [mod[:, i * d:(i + 1) * d] for i in range(6)]
    key = HEADS * HEAD_DIM
    small0 = 4 * key + 3 * key
    small1 = small0 + 2 * HEADS
    w_main = jnp.concatenate([w_in[:, :small0], w_in[:, small1:]], axis=1).astype(BF16)
    w_small_t = w_in[:, small0:small1].T.astype(BF16)
    proj, ab_t = _inproj(x2d, norm1_g, sc1, sh1, w_main, w_small_t, tiles["in_tm"], tiles["in_tn"])
    o_a = _hgrn_all(proj, lb, hgrn_onorm_g, tiles["mix_ts"])
    u, wqd, ku, attn, dl = _gdn_prep(proj, gdn_conv_w, ab_t, gdn_a_log, gdn_dt_bias, tiles["mix_ts"])
    o_b = _gdn_scan(u, wqd, ku, attn, dl, proj, gdn_onorm_g, tiles["mix_ts"])
    return _merge(o_a, o_b, proj, x2d, w_branch_hgrn.astype(BF16), w_branch_gdn.astype(BF16),
                  w_out.astype(BF16), gt1, norm2_g, sc2, sh2, tiles["merge_tm"])


def _moe(x1, h2, mod, norm2_g, normf_g, w_router, router_bias, w_exp_gate, w_exp_up, w_exp_down, w_sh_gate,
         w_sh_up, w_sh_down, tiles):
    s, d = x1.shape
    sh2, sc2, gt2 = [mod[:, i * d:(i + 1) * d] for i in (3, 4, 5)]
    idx_t, wts_t, rank_t, wcol, counts = _router(x1, norm2_g, sc2, sh2, w_router.T, router_bias.reshape(-1, 1),
                                                 tiles["route_tm"])
    del wts_t
    bm = EXPERT_BLOCK
    n_blocks = -(-(s * TOP_K + N_EXPERTS * (bm - 1)) // bm)
    counts = counts[:, 0]
    padded = (counts + bm - 1) // bm * bm
    pend = jnp.cumsum(padded)
    pstart = (pend - padded).astype(I32)
    block_start = jnp.arange(n_blocks, dtype=I32) * bm
    block_e = jnp.minimum(jnp.sum(pend[None, :] <= block_start[:, None], axis=1), N_EXPERTS - 1).astype(I32)
    n_used = (pend[-1:] // bm).astype(I32)
    dest_flat = _dest(pstart, idx_t, rank_t, tiles["route_tm"]).reshape(-1)
    xs = _dispatch(dest_flat, pstart + counts, pend.astype(I32), h2, n_blocks * bm, tiles["disp_tm"])
    ys = _experts(block_e, n_used, xs, w_exp_gate, w_exp_up, w_exp_down)
    return _combine(dest_flat, ys, h2, x1, wcol, w_sh_gate.astype(BF16), w_sh_up.astype(BF16),
                    w_sh_down.astype(BF16), gt2, normf_g, tiles["comb_tm"])


def _tiles(s):
    pick = lambda want: min(want, s)
    return dict(in_tm=pick(1024), in_tn=1024, mix_ts=pick(512), merge_tm=pick(512), route_tm=pick(512),
                disp_tm=pick(256), comb_tm=pick(128))


def kernel(x, c, w_ada, b_ada, norm1_g, norm2_g, w_in, hgrn_lb_table, hgrn_onorm_g, gdn_conv_w, gdn_a_log, gdn_dt_bias, gdn_onorm_g, w_branch_hgrn, w_branch_gdn, w_out, w_router, router_bias, w_exp_gate, w_exp_up, w_exp_down, w_sh_gate, w_sh_up, w_sh_down, normf_g):
    b, s, d = x.shape
    assert b == 1 and w_ada.shape[0] == 1, "one sequence, one layer"
    tiles = _tiles(s)
    lb = jnp.cumsum(jax.nn.softmax(hgrn_lb_table.astype(F32), axis=0), axis=0)[0:1]
    mod = _ada(c, w_ada[0], b_ada[0])
    row = lambda v: v.reshape(1, -1)
    x1, h2 = _mixer(x[0], mod, row(norm1_g[0]), row(norm2_g[0]), w_in[0], lb, row(hgrn_onorm_g[0]), gdn_conv_w[0],
                    gdn_a_log[0], gdn_dt_bias[0], row(gdn_onorm_g[0]), w_branch_hgrn[0], w_branch_gdn[0], w_out[0],
                    tiles)
    out = _moe(x1, h2, mod, row(norm2_g[0]), row(normf_g), w_router[0], router_bias[0], w_exp_gate[0],
               w_exp_up[0], w_exp_down[0], w_sh_gate[0], w_sh_up[0], w_sh_down[0], tiles)
    return out[None]
```

```python
import functools

import jax
import jax.numpy as jnp
from jax import lax
from jax.experimental import pallas as pl
from jax.experimental.pallas import tpu as pltpu

F32 = jnp.float32
BF16 = jnp.bfloat16
I32 = jnp.int32
U32 = jnp.uint32

NORM_EPS = 1e-6
L2_EPS = 1e-6
HEADS = 8
HEAD_DIM = 128
CONV_WIDTH = 4
CHUNK = 64
N_EXPERTS = 64
N_GROUPS = 8
GROUP_SIZE = N_EXPERTS // N_GROUPS
TOPK_GROUPS = 4
TOP_K = 8
ROUTED_SCALE = 2.5
EXPERT_BLOCK = 256

LANES = 128
SUBLANES = 8
VMEM_LIMIT = 56 * 1024 * 1024

NT = (((1,), (1,)), ((), ()))
TN = (((0,), (0,)), ((), ()))


def _params(sem, **kw):
    return pltpu.CompilerParams(dimension_semantics=sem, vmem_limit_bytes=VMEM_LIMIT, **kw)


def _dot(a, b):
    return jnp.dot(a, b, preferred_element_type=F32)


def _dg(a, b, dims):
    return lax.dot_general(a, b, dims, preferred_element_type=F32)


def _split(x):
    hi = x.astype(BF16)
    lo = (x - hi.astype(F32)).astype(BF16)
    return hi, lo


def _dot_exact_lhs(a_bf16, x, dims=None):
    hi, lo = _split(x)
    if dims is None:
        return _dot(a_bf16, hi) + _dot(a_bf16, lo)
    return _dg(a_bf16, hi, dims) + _dg(a_bf16, lo, dims)


def _sigmoid(x):
    return 1.0 / (1.0 + jnp.exp(-x))


def _silu(x):
    return x * _sigmoid(x)


def _rms(x, eps):
    return x * lax.rsqrt(jnp.mean(x * x, axis=-1, keepdims=True) + eps)


def _iota2(shape, dim):
    return lax.broadcasted_iota(I32, shape, dim)


def _pack_halves(lo, hi):
    lo_bits = lax.shift_right_logical(pltpu.bitcast(lo, U32), U32(16))
    hi_bits = pltpu.bitcast(hi, U32) & U32(0xFFFF0000)
    return lo_bits | hi_bits


def _unpack_halves(word):
    lo = pltpu.bitcast(lax.shift_left(word, U32(16)), F32)
    hi = pltpu.bitcast(word & U32(0xFFFF0000), F32)
    return lo, hi


def _round_bf16(x):
    return x.astype(BF16).astype(F32)


def _ada_kernel(c_ref, w_ref, b_ref, o_ref):
    cond = _silu(c_ref[...])
    o_ref[...] = jnp.dot(cond, w_ref[...], preferred_element_type=F32,
                         precision=lax.Precision.HIGHEST) + b_ref[...]


def _ada(c, w_ada, b_ada):
    d, n = w_ada.shape
    tn = 1024
    c8 = jnp.broadcast_to(c, (SUBLANES, d))
    out = pl.pallas_call(
        _ada_kernel,
        grid=(n // tn,),
        in_specs=[pl.BlockSpec((SUBLANES, d), lambda j: (0, 0)),
                  pl.BlockSpec((d, tn), lambda j: (0, j)),
                  pl.BlockSpec((1, tn), lambda j: (0, j))],
        out_specs=pl.BlockSpec((SUBLANES, tn), lambda j: (0, j)),
        out_shape=jax.ShapeDtypeStruct((SUBLANES, n), F32),
        compiler_params=_params(("arbitrary",)),
    )(c8, w_ada, b_ada.reshape(1, n))
    return out[0:1]


def _inproj_kernel(x_ref, g_ref, sc_ref, sh_ref, w_ref, wst_ref, proj_ref, smallt_ref, h_scr):
    @pl.when(pl.program_id(1) == 0)
    def _():
        h = _rms(x_ref[...], NORM_EPS) * g_ref[...] * (1.0 + sc_ref[...]) + sh_ref[...]
        hb = h.astype(BF16)
        h_scr[...] = hb
        smallt_ref[...] = _dg(wst_ref[...], hb, NT)

    proj_ref[...] = _dot(h_scr[...], w_ref[...]).astype(BF16)


def _inproj(x, g, sc, sh, w_main, w_small_t, tm, tn):
    s, d = x.shape
    n = w_main.shape[1]
    ns = w_small_t.shape[0]
    row = lambda i, j: (0, 0)
    return pl.pallas_call(
        _inproj_kernel,
        grid=(s // tm, n // tn),
        in_specs=[pl.BlockSpec((tm, d), lambda i, j: (i, 0)),
                  pl.BlockSpec((1, d), row), pl.BlockSpec((1, d), row), pl.BlockSpec((1, d), row),
                  pl.BlockSpec((d, tn), lambda i, j: (0, j)),
                  pl.BlockSpec((ns, d), row)],
        out_specs=[pl.BlockSpec((tm, tn), lambda i, j: (i, j)),
                   pl.BlockSpec((ns, tm), lambda i, j: (0, i))],
        out_shape=[jax.ShapeDtypeStruct((s, n), BF16), jax.ShapeDtypeStruct((ns, s), F32)],
        scratch_shapes=[pltpu.VMEM((tm, d), BF16)],
        compiler_params=_params(("arbitrary", "arbitrary")),
    )(x, g, sc, sh, w_main, w_small_t)


def _hgrn_kernel(q_ref, f_ref, i_ref, g_ref, lb_ref, on_ref, o_ref, st_scr, *, n_chunks):
    @pl.when(pl.program_id(0) == 0)
    def _():
        st_scr[...] = jnp.zeros_like(st_scr)

    c = CHUNK
    hd = HEAD_DIM
    causal = _iota2((c, c), 1) <= _iota2((c, c), 0)
    tri = causal.astype(BF16)
    lb = lb_ref[...]
    on_g = on_ref[...]
    heads = [slice(h * hd, (h + 1) * hd) for h in range(HEADS)]

    def chunk(n, carry):
        rows = pl.ds(pl.multiple_of(n * c, c), c)
        f = lb + (1.0 - lb) * _sigmoid(f_ref[rows, :].astype(F32))
        b = _dot_exact_lhs(tri, jnp.log(f))
        k = 1.0 - f
        q = _silu(q_ref[rows, :].astype(F32)) * (hd ** -0.5)
        v = i_ref[rows, :]
        b_mid = b[c // 2:c // 2 + 1, :]
        b_last = b[c - 1:c, :]
        qa = (q * jnp.exp(b - b_mid)).astype(BF16)
        ka = (k * jnp.exp(b_mid - b)).astype(BF16)
        qi = (q * jnp.exp(b)).astype(BF16)
        ku = (k * jnp.exp(b_last - b)).astype(BF16)
        dec = jnp.exp(b_last)
        gate = on_g * _silu(g_ref[rows, :].astype(F32))
        sts = [st_scr[h] for h in range(HEADS)]
        scores = [jnp.where(causal, _dg(qa[:, sl], ka[:, sl], NT), 0.0).astype(BF16) for sl in heads]
        inter = [_dg(qi[:, sl], st.astype(BF16), NT) for sl, st in zip(heads, sts)]
        kv = [_dg(v[:, sl], ku[:, sl], TN) for sl in heads]
        for h, sl in enumerate(heads):
            st_scr[h] = dec[:, sl] * sts[h] + kv[h]
        outs = [_rms(_dot(sc, v[:, sl]) + it, NORM_EPS) for sc, sl, it in zip(scores, heads, inter)]
        o_ref[rows, :] = (jnp.concatenate(outs, axis=1) * gate).astype(BF16)
        return carry

    lax.fori_loop(0, n_chunks, chunk, 0)


def _hgrn(proj, lb, onorm_g, ts):
    s = proj.shape[0]
    width = HEADS * HEAD_DIM
    col = lambda blk: pl.BlockSpec((ts, width), lambda j, blk=blk: (j, blk))
    const = pl.BlockSpec((1, width), lambda j: (0, 0))
    return pl.pallas_call(
        functools.partial(_hgrn_kernel, n_chunks=ts // CHUNK),
        grid=(s // ts,),
        in_specs=[col(0), col(1), col(2), col(3), const, const],
        out_specs=pl.BlockSpec((ts, width), lambda j: (j, 0)),
        out_shape=jax.ShapeDtypeStruct((s, width), BF16),
        scratch_shapes=[pltpu.VMEM((HEADS, HEAD_DIM, HEAD_DIM), F32)],
        compiler_params=_params(("arbitrary",)),
    )(proj, proj, proj, proj, lb, jnp.tile(onorm_g, (1, HEADS)))


def _gdn_prep_kernel(q_ref, k_ref, v_ref, qp_ref, kp_ref, vp_ref, wq_ref, wk_ref, wv_ref, ab_ref, alog_ref,
                     dtb_ref, tri_ref, eye_ref, u_ref, wqd_ref, ku_ref, attn_ref, dl_ref, cat_scr, *, n_chunks, ts):
    h = pl.program_id(0)
    first = pl.program_id(1) == 0
    c = CHUNK
    hd = HEAD_DIM

    def conv_silu(cur_ref, prev_ref, w_ref):
        cat_scr[0:8, :] = jnp.where(first, 0.0, prev_ref[...].astype(F32))
        cat_scr[8:8 + ts, :] = cur_ref[...].astype(F32)
        acc = None
        for j in range(CONV_WIDTH):
            off = 8 - (CONV_WIDTH - 1) + j
            term = cat_scr[off:off + ts, :] * w_ref[j:j + 1, :]
            acc = term if acc is None else acc + term
        return _silu(acc)

    def l2n(x):
        return x * lax.rsqrt(jnp.sum(x * x, axis=-1, keepdims=True) + L2_EPS)

    q_all = l2n(conv_silu(q_ref, qp_ref, wq_ref)) * (hd ** -0.5)
    k_all = l2n(conv_silu(k_ref, kp_ref, wk_ref))
    v_all = conv_silu(v_ref, vp_ref, wv_ref)

    a_row = ab_ref[0]
    b_row = ab_ref[1]
    z = a_row + dtb_ref[h]
    softplus = jnp.maximum(z, 0.0) + jnp.log(1.0 + jnp.exp(-jnp.abs(z)))
    ld_row = -jnp.exp(jnp.full(a_row.shape, alog_ref[h], F32)) * softplus
    beta_row = _sigmoid(b_row)
    tri_blocks = tri_ref[...]
    ld_b = jnp.broadcast_to(ld_row, (hd, ts))
    gc_all = _dot_exact_lhs(tri_blocks, ld_b, NT)
    hi, lo = _split(ld_b[0:8])
    g_rows = _dg(hi, tri_blocks, NT) + _dg(lo, tri_blocks, NT)
    bc_all = _dot_exact_lhs(eye_ref[...], jnp.broadcast_to(beta_row, (hd, ts)), NT)
    egc_all = jnp.exp(gc_all)

    r = _iota2((c, c), 0)
    cidx = _iota2((c, c), 1)
    causal = cidx <= r
    strict = cidx < r
    eye_f = (r == cidx).astype(F32)
    chunks = [slice(n * c, (n + 1) * c) for n in range(n_chunks)]

    q16 = q_all.astype(BF16)
    k16 = k_all.astype(BF16)
    kq = [_dg(jnp.concatenate([k16[sl], q16[sl]], axis=0), k16[sl], NT) for sl in chunks]
    dm = []
    for sl in chunks:
        diff = gc_all[sl, 0:c] - g_rows[0:1, sl]
        dm.append(jnp.where(causal, jnp.exp(jnp.where(causal, diff, 0.0)), 0.0))
    bm = [-jnp.where(strict, bc_all[sl, 0:c] * x[0:c] * d, 0.0) for sl, x, d in zip(chunks, kq, dm)]
    p = [eye_f + b for b in bm]
    bm = [_dot(b.astype(BF16), b.astype(BF16)) for b in bm]
    for _ in range(c.bit_length() - 3):
        res = [_dot(b.astype(BF16), jnp.concatenate([b, pp], axis=1).astype(BF16)) for b, pp in zip(bm, p)]
        p = [pp + x[:, c:2 * c] for pp, x in zip(p, res)]
        bm = [x[:, 0:c] for x in res]
    p = [pp + _dot(b.astype(BF16), pp.astype(BF16)) for b, pp in zip(bm, p)]
    rhs = jnp.concatenate([v_all * bc_all, k_all * (bc_all * egc_all)], axis=1).astype(BF16)
    sol = [_dot(pp.astype(BF16), rhs[sl]) for pp, sl in zip(p, chunks)]
    qd_all = (q_all * egc_all).astype(BF16)
    for n, sl in enumerate(chunks):
        g_last = gc_all[(n + 1) * c - 1:(n + 1) * c, :]
        u_ref[sl, :] = sol[n][:, 0:hd].astype(BF16)
        wqd_ref[2 * n * c:(2 * n + 1) * c, :] = sol[n][:, hd:2 * hd].astype(BF16)
        wqd_ref[(2 * n + 1) * c:(2 * n + 2) * c, :] = qd_all[sl]
        ku_ref[sl, :] = (k_all[sl] * jnp.exp(g_last - gc_all[sl])).astype(BF16)
        attn_ref[sl, :] = (kq[n][c:2 * c] * dm[n]).astype(BF16)
        dl_ref[n:n + 1, :] = jnp.exp(g_last)


def _gdn_prep(proj, conv_w, ab_t, a_log, dt_bias, ts):
    s = proj.shape[0]
    hd = HEAD_DIM
    c = CHUNK
    q0 = 4 * HEADS
    cur = lambda off: pl.BlockSpec((ts, hd), lambda h, j, off=off: (j, off + h))
    prev = lambda off: pl.BlockSpec((8, hd), lambda h, j, off=off: (jnp.maximum(j * (ts // 8) - 1, 0), off + h))
    cw = lambda off: pl.BlockSpec((CONV_WIDTH, hd), lambda h, j, off=off: (0, off + h))
    smem = pl.BlockSpec(memory_space=pltpu.SMEM)
    const = pl.BlockSpec((ts, ts), lambda h, j: (0, 0))
    pos = jnp.arange(ts)
    tri_blocks = ((pos[:, None] // c == pos[None, :] // c) & (pos[None, :] <= pos[:, None])).astype(BF16)
    eye = (pos[:, None] == pos[None, :]).astype(BF16)
    per_head = lambda rows, cols: pl.BlockSpec((None, rows, cols), lambda h, j: (h, j, 0))
    return pl.pallas_call(
        functools.partial(_gdn_prep_kernel, n_chunks=ts // c, ts=ts),
        grid=(HEADS, s // ts),
        in_specs=[cur(q0), cur(q0 + HEADS), cur(q0 + 2 * HEADS),
                  prev(q0), prev(q0 + HEADS), prev(q0 + 2 * HEADS),
                  cw(0), cw(HEADS), cw(2 * HEADS),
                  pl.BlockSpec((2, None, 1, ts), lambda h, j: (0, h, 0, j)),
                  smem, smem, const, const],
        out_specs=[pl.BlockSpec((ts, hd), lambda h, j: (j, h)),
                   pl.BlockSpec((2 * ts, hd), lambda h, j: (j, h)),
                   pl.BlockSpec((ts, hd), lambda h, j: (j, h)),
                   per_head(ts, c),
                   per_head(ts // c, hd)],
        out_shape=[jax.ShapeDtypeStruct((s, HEADS * hd), BF16),
                   jax.ShapeDtypeStruct((2 * s, HEADS * hd), BF16),
                   jax.ShapeDtypeStruct((s, HEADS * hd), BF16),
                   jax.ShapeDtypeStruct((HEADS, s, c), BF16),
                   jax.ShapeDtypeStruct((HEADS, s // c, hd), F32)],
        scratch_shapes=[pltpu.VMEM((ts + 8, hd), F32)],
        compiler_params=_params(("arbitrary", "arbitrary")),
    )(proj, proj, proj, proj, proj, proj, conv_w, conv_w, conv_w,
      ab_t.reshape(2, HEADS, 1, s), a_log, dt_bias, tri_blocks, eye)


def _gdn_scan_kernel(u_ref, wqd_ref, ku_ref, attn_ref, dl_ref, g_ref, on_ref, o_ref, st_scr, *, n_chunks):
    @pl.when(pl.program_id(0) == 0)
    def _():
        st_scr[...] = jnp.zeros_like(st_scr)

    c = CHUNK
    hd = HEAD_DIM
    on_g = on_ref[...]
    heads = [slice(h * hd, (h + 1) * hd) for h in range(HEADS)]

    def chunk(n, carry):
        rows = pl.ds(pl.multiple_of(n * c, c), c)
        rows2 = pl.ds(pl.multiple_of(2 * n * c, 2 * c), 2 * c)
        sts = [st_scr[h] for h in range(HEADS)]
        wq = [_dot(wqd_ref[rows2, sl], st.astype(BF16)) for sl, st in zip(heads, sts)]
        vn = [(u_ref[rows, sl].astype(F32) - x[0:c]).astype(BF16) for sl, x in zip(heads, wq)]
        upd = [_dg(ku_ref[rows, sl], v, TN) for sl, v in zip(heads, vn)]
        for h in range(HEADS):
            st_scr[h] = dl_ref[h, pl.ds(n, 1), :] * sts[h] + upd[h]
        outs = [_rms(x[c:2 * c] + _dot(attn_ref[h, rows, :], v), NORM_EPS)
                for h, (x, v) in enumerate(zip(wq, vn))]
        gate = jnp.tile(on_g, (1, HEADS)) * _silu(g_ref[rows, :].astype(F32))
        o_ref[rows, :] = (jnp.concatenate(outs, axis=1) * gate).astype(BF16)
        return carry

    lax.fori_loop(0, n_chunks, chunk, 0)


def _gdn_scan(u, wqd, ku, attn, dl, proj, onorm_g, ts):
    s, width = u.shape
    c = CHUNK
    gate_blk = (4 * HEADS + 3 * HEADS) * HEAD_DIM // width
    return pl.pallas_call(
        functools.partial(_gdn_scan_kernel, n_chunks=ts // c),
        grid=(s // ts,),
        in_specs=[pl.BlockSpec((ts, width), lambda j: (j, 0)),
                  pl.BlockSpec((2 * ts, width), lambda j: (j, 0)),
                  pl.BlockSpec((ts, width), lambda j: (j, 0)),
                  pl.BlockSpec((HEADS, ts, c), lambda j: (0, j, 0)),
                  pl.BlockSpec((HEADS, ts // c, HEAD_DIM), lambda j: (0, j, 0)),
                  pl.BlockSpec((ts, width), lambda j: (j, gate_blk)),
                  pl.BlockSpec((1, HEAD_DIM), lambda j: (0, 0))],
        out_specs=pl.BlockSpec((ts, width), lambda j: (j, 0)),
        out_shape=jax.ShapeDtypeStruct((s, width), BF16),
        scratch_shapes=[pltpu.VMEM((HEADS, HEAD_DIM, HEAD_DIM), F32)],
        compiler_params=_params(("arbitrary",)),
    )(u, wqd, ku, attn, dl, proj, onorm_g)


def _merge_kernel(oa_ref, ob_ref, mga_ref, mgb_ref, x_ref, wa_ref, wb_ref, wo_ref, gt_ref, g2_ref, sc_ref,
                  sh_ref, x1_ref, h2_ref):
    ya = _dot(oa_ref[...], wa_ref[...])
    yb = _dot(ob_ref[...], wb_ref[...])
    merged = _sigmoid(mga_ref[...].astype(F32)) * ya + _sigmoid(mgb_ref[...].astype(F32)) * yb
    x1 = x_ref[...] + gt_ref[...] * _dot(merged.astype(BF16), wo_ref[...])
    x1_ref[...] = x1
    h2 = _rms(x1, NORM_EPS) * g2_ref[...] * (1.0 + sc_ref[...]) + sh_ref[...]
    h2_ref[...] = h2.astype(BF16)


def _merge(o_a, o_b, proj, x, w_a, w_b, w_o, gt1, g2, sc2, sh2, tm):
    s, d = x.shape
    dv = o_a.shape[1]
    mg0 = (8 * HEADS * HEAD_DIM) // d
    const = lambda shape: pl.BlockSpec(shape, lambda i: (0, 0), pipeline_mode=pl.Buffered(1))
    return pl.pallas_call(
        _merge_kernel,
        grid=(s // tm,),
        in_specs=[pl.BlockSpec((tm, dv), lambda i: (i, 0)),
                  pl.BlockSpec((tm, dv), lambda i: (i, 0)),
                  pl.BlockSpec((tm, d), lambda i: (i, mg0)),
                  pl.BlockSpec((tm, d), lambda i: (i, mg0 + 1)),
                  pl.BlockSpec((tm, d), lambda i: (i, 0)),
                  const((dv, d)), const((dv, d)), const((d, d)),
                  const((1, d)), const((1, d)), const((1, d)), const((1, d))],
        out_specs=[pl.BlockSpec((tm, d), lambda i: (i, 0)), pl.BlockSpec((tm, d), lambda i: (i, 0))],
        out_shape=[jax.ShapeDtypeStruct((s, d), F32), jax.ShapeDtypeStruct((s, d), BF16)],
        compiler_params=_params(("arbitrary",)),
    )(o_a, o_b, proj, proj, x, w_a, w_b, w_o, gt1, g2, sc2, sh2)


def _first_max(vals, iota, size, axis):
    m = jnp.max(vals, axis=axis, keepdims=True)
    idx = jnp.min(jnp.where(vals == m, iota, size), axis=axis, keepdims=True)
    return m, idx


def _router_kernel(x1_ref, g2_ref, sc_ref, sh_ref, wrt_ref, bias_ref, upper_ref, pos_ref, wcol_ref, before_ref,
                   ntile_ref, cnt_scr, *, tm):
    @pl.when(pl.program_id(0) == 0)
    def _():
        cnt_scr[...] = jnp.zeros_like(cnt_scr)

    e = N_EXPERTS
    h2 = _rms(x1_ref[...], NORM_EPS) * g2_ref[...] * (1.0 + sc_ref[...]) + sh_ref[...]
    logits = lax.dot_general(wrt_ref[...], h2, NT, preferred_element_type=F32,
                             precision=lax.Precision.HIGHEST)
    scores = _sigmoid(logits)
    biased = scores + bias_ref[...]
    neg = -jnp.inf

    g3 = biased.reshape(N_GROUPS, GROUP_SIZE, tm)
    i3 = lax.broadcasted_iota(I32, g3.shape, 1)
    m1, a1 = _first_max(g3, i3, GROUP_SIZE, 1)
    m2 = jnp.max(jnp.where(i3 == a1, neg, g3), axis=1, keepdims=True)
    gs = (m1 + m2).reshape(N_GROUPS, tm)
    ig = _iota2(gs.shape, 0)
    gmask = jnp.zeros(gs.shape, jnp.bool_)
    for _ in range(TOPK_GROUPS):
        _, a = _first_max(gs, ig, N_GROUPS, 0)
        pick = ig == a
        gmask = jnp.logical_or(gmask, pick)
        gs = jnp.where(pick, neg, gs)
    emask = jnp.broadcast_to(gmask.reshape(N_GROUPS, 1, tm), (N_GROUPS, GROUP_SIZE, tm)).reshape(e, tm)

    cand = jnp.where(emask, biased, neg)
    ie = _iota2((e, tm), 0)
    sel_all = jnp.zeros((e, tm), jnp.bool_)
    w_rows, picks = [], []
    for _ in range(TOP_K):
        _, a = _first_max(cand, ie, e, 0)
        pick = ie == a
        picks.append(pick)
        w_rows.append(jnp.sum(jnp.where(pick, scores, 0.0), axis=0, keepdims=True))
        sel_all = jnp.logical_or(sel_all, pick)
        cand = jnp.where(pick, neg, cand)
    w_sum = w_rows[0]
    for wr in w_rows[1:]:
        w_sum = w_sum + wr
    wts = jnp.concatenate(w_rows, axis=0) / w_sum * ROUTED_SCALE

    sel = sel_all.astype(BF16)
    in_expert = _dot(sel, upper_ref[...])
    n_tile = jnp.sum(sel_all.astype(F32), axis=1, keepdims=True)
    lower = (_iota2((e, e), 1) < _iota2((e, e), 0)).astype(BF16)
    expert_off = _dot_exact_lhs(lower, jnp.broadcast_to(n_tile, (e, LANES)))[:, 0:1]
    place = in_expert + expert_off
    pos = jnp.concatenate([jnp.sum(jnp.where(pk, place, 0.0), axis=0, keepdims=True) for pk in picks], axis=0)
    pos_ref[...] = pos.astype(I32)
    before_ref[...] = jnp.broadcast_to(cnt_scr[...], before_ref.shape).astype(I32)
    ntile_ref[...] = jnp.broadcast_to(n_tile, ntile_ref.shape).astype(I32)
    cnt_scr[...] = cnt_scr[...] + n_tile

    eye = (_iota2((tm, tm), 0) == _iota2((tm, tm), 1)).astype(BF16)
    rows = jnp.concatenate([wts, pos, jnp.zeros((LANES - 2 * TOP_K, tm), F32)], axis=0)
    hi, lo = _split(rows)
    lo2 = (rows - hi.astype(F32) - lo.astype(F32)).astype(BF16)
    wcol_ref[...] = _dg(eye, hi, NT) + _dg(eye, lo, NT) + _dg(eye, lo2, NT)


def _router(x1, g2, sc2, sh2, w_router_t, bias_col, tm):
    s, d = x1.shape
    e = N_EXPERTS
    nt = s // tm
    upper = (jnp.arange(tm)[:, None] < jnp.arange(tm)[None, :]).astype(BF16)
    const = lambda shape: pl.BlockSpec(shape, lambda i: (0, 0))
    per_tile = pl.BlockSpec((None, e, LANES), lambda i: (i, 0, 0))
    return pl.pallas_call(
        functools.partial(_router_kernel, tm=tm),
        grid=(nt,),
        in_specs=[pl.BlockSpec((tm, d), lambda i: (i, 0)),
                  const((1, d)), const((1, d)), const((1, d)),
                  const((e, d)), const((e, 1)), const((tm, tm))],
        out_specs=[pl.BlockSpec((TOP_K, tm), lambda i: (0, i)),
                   pl.BlockSpec((tm, LANES), lambda i: (i, 0)),
                   per_tile, per_tile],
        out_shape=[jax.ShapeDtypeStruct((TOP_K, s), I32), jax.ShapeDtypeStruct((s, LANES), F32),
                   jax.ShapeDtypeStruct((nt, e, LANES), I32), jax.ShapeDtypeStruct((nt, e, LANES), I32)],
        scratch_shapes=[pltpu.VMEM((e, 1), F32)],
        compiler_params=_params(("arbitrary",)),
    )(x1, g2, sc2, sh2, w_router_t, bias_col, upper)


def _run_sizes(limit):
    return [1 << b for b in range(limit.bit_length() - 1, -1, -1)]


def _for_each_run(step, run_len_ref, run_dst_ref, tm, make_copy, fn):
    def per_expert(ex, tile_slot):
        n = run_len_ref[step * N_EXPERTS + ex]
        dst = run_dst_ref[step * N_EXPERTS + ex]
        for size in _run_sizes(tm):
            take = (n & size) != 0

            @pl.when(take)
            def _(tile_slot=tile_slot, dst=dst, size=size):
                fn(make_copy(tile_slot, dst, size))

            inc = jnp.where(take, size, 0)
            tile_slot = tile_slot + inc
            dst = dst + inc
        return tile_slot

    lax.fori_loop(0, N_EXPERTS, per_expert, jnp.int32(0))


def _slot_rows(slot, n_slots):
    return pl.ds(pl.multiple_of(slot * SUBLANES, SUBLANES), n_slots * SUBLANES)


def _dispatch_kernel(run_len_ref, run_dst_ref, pad_lo_ref, pad_hi_ref, pos_ref, h_ref, xs_ref, stage, zero_scr,
                     sem, pad_sem, *, tm, rows_per_pass):
    step = pl.program_id(0)
    na = TOP_K * tm
    d = h_ref.shape[1]
    half = d // 2
    n_words = half // LANES

    def pad_copy(slot, n_slots):
        return pltpu.make_async_copy(zero_scr.at[pl.ds(0, n_slots * SUBLANES), :],
                                     xs_ref.at[_slot_rows(slot, n_slots), :], pad_sem)

    def for_each_pad(fn):
        def per_expert(ex, carry):
            slot = pad_lo_ref[ex]
            n = pad_hi_ref[ex] - slot
            for size in _run_sizes(EXPERT_BLOCK - 1):
                take = (n & size) != 0

                @pl.when(take)
                def _(slot=slot, size=size):
                    fn(pad_copy(slot, size))

                slot = slot + jnp.where(take, size, 0)
            return carry
        lax.fori_loop(0, N_EXPERTS, per_expert, 0)

    @pl.when(step == 0)
    def _():
        zero_scr[...] = jnp.zeros_like(zero_scr)
        for_each_pad(lambda cp: cp.start())

    pos = pos_ref[...]
    h = h_ref[...]
    for a0 in range(0, na, rows_per_pass):
        slot_id = a0 + _iota2((rows_per_pass, tm), 0)
        hit = pos[0:1, :] == slot_id
        for k in range(1, TOP_K):
            hit = jnp.logical_or(hit, pos[k:k + 1, :] == slot_id)
        rows = _dot(hit.astype(BF16), h)
        for i in range(n_words):
            word = _pack_halves(rows[:, i * LANES:(i + 1) * LANES], rows[:, half + i * LANES:half + (i + 1) * LANES])
            stage[pl.ds(a0 * SUBLANES + i, rows_per_pass, stride=SUBLANES), :] = word

    def run_copy(tile_slot, sorted_slot, n_slots):
        return pltpu.make_async_copy(stage.at[_slot_rows(tile_slot, n_slots), :],
                                     xs_ref.at[_slot_rows(sorted_slot, n_slots), :], sem)

    _for_each_run(step, run_len_ref, run_dst_ref, tm, run_copy, lambda cp: cp.start())
    _for_each_run(step, run_len_ref, run_dst_ref, tm, run_copy, lambda cp: cp.wait())

    @pl.when(step == 0)
    def _():
        for_each_pad(lambda cp: cp.wait())


def _dispatch(run_len, run_dst, pad_lo, pad_hi, pos_t, h2, n_slots, tm):
    s, d = h2.shape
    assert (d // 2) % LANES == 0 and (d // 2) // LANES == SUBLANES, "one token row must pack into one (8, 128) tile"
    na = TOP_K * tm
    return pl.pallas_call(
        functools.partial(_dispatch_kernel, tm=tm, rows_per_pass=min(512, na)),
        grid_spec=pltpu.PrefetchScalarGridSpec(
            num_scalar_prefetch=4,
            grid=(s // tm,),
            in_specs=[pl.BlockSpec((TOP_K, tm), lambda i, *_: (0, i)),
                      pl.BlockSpec((tm, d), lambda i, *_: (i, 0))],
            out_specs=pl.BlockSpec(memory_space=pl.ANY),
            scratch_shapes=[pltpu.VMEM((na * SUBLANES, LANES), U32),
                            pltpu.VMEM((EXPERT_BLOCK // 2 * SUBLANES, LANES), U32),
                            pltpu.SemaphoreType.DMA(()), pltpu.SemaphoreType.DMA(())]),
        out_shape=jax.ShapeDtypeStruct((n_slots * SUBLANES, LANES), U32),
        compiler_params=_params(("arbitrary",), has_side_effects=True, disable_bounds_checks=True),
    )(run_len, run_dst, pad_lo, pad_hi, pos_t, h2)


def _expert_kernel(be_ref, nu_ref, x_ref, wg_ref, wu_ref, wd_ref, y_ref, wg_scr, wu_scr, wd_scr):
    b = pl.program_id(0)
    bm = EXPERT_BLOCK
    active = b < nu_ref[0]
    new_expert = jnp.logical_or(b == 0, be_ref[b] != be_ref[jnp.maximum(b - 1, 0)])

    @pl.when(jnp.logical_and(active, new_expert))
    def _():
        wg_scr[...] = wg_ref[...].astype(BF16)
        wu_scr[...] = wu_ref[...].astype(BF16)
        wd_scr[...] = wd_ref[...].astype(BF16)

    @pl.when(active)
    def _():
        los, his = [], []
        for i in range(SUBLANES):
            lo, hi = _unpack_halves(x_ref[pl.ds(i, bm, stride=SUBLANES), :])
            los.append(lo.astype(BF16))
            his.append(hi.astype(BF16))
        xb = jnp.concatenate(los + his, axis=1)
        hid = _silu(_dot(xb, wg_scr[...])) * _dot(xb, wu_scr[...])
        y = _dot(hid.astype(BF16), wd_scr[...])
        half = y.shape[1] // 2
        for i in range(SUBLANES):
            word = _pack_halves(_round_bf16(y[:, i * LANES:(i + 1) * LANES]),
                                _round_bf16(y[:, half + i * LANES:half + (i + 1) * LANES]))
            y_ref[pl.ds(i, bm, stride=SUBLANES), :] = word


def _experts(block_e, n_used, xs, w_gate, w_up, w_down):
    d, ff = w_gate.shape[1], w_gate.shape[2]
    bm = EXPERT_BLOCK
    n_blocks = xs.shape[0] // (bm * SUBLANES)
    last = lambda b, nu: jnp.minimum(b, nu[0] - 1)
    blk = lambda b, be, nu: (last(b, nu), 0)
    wsel = lambda b, be, nu: (be[last(b, nu)], 0, 0)
    return pl.pallas_call(
        _expert_kernel,
        grid_spec=pltpu.PrefetchScalarGridSpec(
            num_scalar_prefetch=2,
            grid=(n_blocks,),
            in_specs=[pl.BlockSpec((bm * SUBLANES, LANES), blk),
                      pl.BlockSpec((None, d, ff), wsel),
                      pl.BlockSpec((None, d, ff), wsel),
                      pl.BlockSpec((None, ff, d), wsel)],
            out_specs=pl.BlockSpec((bm * SUBLANES, LANES), blk),
            scratch_shapes=[pltpu.VMEM((d, ff), BF16), pltpu.VMEM((d, ff), BF16), pltpu.VMEM((ff, d), BF16)]),
        out_shape=jax.ShapeDtypeStruct(xs.shape, U32),
        compiler_params=_params(("arbitrary",)),
    )(block_e, n_used, xs, w_gate, w_up, w_down)


def _combine_kernel(run_len_ref, run_src_ref, ys_ref, h_ref, x1_ref, wcol_ref, wg_ref, wu_ref, wd_ref, gt_ref, gf_ref,
                    o_ref, stage, y_scr, sem, *, tm):
    step = pl.program_id(0)
    na = TOP_K * tm
    d = h_ref.shape[1]
    half = d // 2

    def run_copy(tile_slot, sorted_slot, n_slots):
        return pltpu.make_async_copy(ys_ref.at[_slot_rows(sorted_slot, n_slots), :],
                                     stage.at[_slot_rows(tile_slot, n_slots), :], sem)

    _for_each_run(step, run_len_ref, run_src_ref, tm, run_copy, lambda cp: cp.start())

    hb = h_ref[...]
    hid = _silu(_dot(hb, wg_ref[...])) * _dot(hb, wu_ref[...])
    acc = _dot(hid.astype(BF16), wd_ref[...])

    _for_each_run(step, run_len_ref, run_src_ref, tm, run_copy, lambda cp: cp.wait())

    for i in range(SUBLANES):
        lo, hi = _unpack_halves(stage[pl.ds(i, na, stride=SUBLANES), :])
        y_scr[:, i * LANES:(i + 1) * LANES] = lo.astype(BF16)
        y_scr[:, half + i * LANES:half + (i + 1) * LANES] = hi.astype(BF16)

    wcol = wcol_ref[...]
    slot_id = _iota2((tm, na), 1)
    wmat = jnp.zeros((tm, na), F32)
    for k in range(TOP_K):
        at = wcol[:, TOP_K + k:TOP_K + k + 1].astype(I32)
        wmat = wmat + jnp.where(at == slot_id, wcol[:, k:k + 1], 0.0)
    w_hi, w_lo = _split(wmat)
    y_all = y_scr[...]
    acc = acc + _dot(w_hi, y_all) + _dot(w_lo, y_all)
    x2 = x1_ref[...] + gt_ref[...] * acc
    o_ref[...] = _rms(x2, NORM_EPS) * gf_ref[...]


def _combine(run_len, run_src, ys, h2, x1, wcol, w_gate, w_up, w_down, gt2, gf, tm):
    s, d = x1.shape
    ff = w_gate.shape[1]
    na = TOP_K * tm
    const = lambda shape: pl.BlockSpec(shape, lambda i, *_: (0, 0))
    tile = lambda cols: pl.BlockSpec((tm, cols), lambda i, *_: (i, 0))
    return pl.pallas_call(
        functools.partial(_combine_kernel, tm=tm),
        grid_spec=pltpu.PrefetchScalarGridSpec(
            num_scalar_prefetch=2,
            grid=(s // tm,),
            in_specs=[pl.BlockSpec(memory_space=pl.ANY),
                      tile(d), tile(d), tile(LANES),
                      const((d, ff)), const((d, ff)), const((ff, d)), const((1, d)), const((1, d))],
            out_specs=tile(d),
            scratch_shapes=[pltpu.VMEM((na * SUBLANES, LANES), U32), pltpu.VMEM((na, d), BF16),
                            pltpu.SemaphoreType.DMA(())]),
        out_shape=jax.ShapeDtypeStruct((s, d), F32),
        compiler_params=_params(("arbitrary",), disable_bounds_checks=True),
    )(run_len, run_src, ys, h2, x1, wcol, w_gate, w_up, w_down, gt2, gf)


def _mixer(x2d, mod, norm1_g, norm2_g, w_in, lb, hgrn_onorm_g, gdn_conv_w, gdn_a_log, gdn_dt_bias, gdn_onorm_g,
           w_branch_hgrn, w_branch_gdn, w_out, tiles):
    d = x2d.shape[1]
    sh1, sc1, gt1, sh2, sc2, _ = [mod[:, i * d:(i + 1) * d] for i in range(6)]
    key = HEADS * HEAD_DIM
    small0 = 4 * key + 3 * key
    small1 = small0 + 2 * HEADS
    w_main = jnp.concatenate([w_in[:, :small0], w_in[:, small1:]], axis=1).astype(BF16)
    w_small_t = w_in[:, small0:small1].T.astype(BF16)
    proj, ab_t = _inproj(x2d, norm1_g, sc1, sh1, w_main, w_small_t, tiles["in_tm"], tiles["in_tn"])
    o_a = _hgrn(proj, lb, hgrn_onorm_g, tiles["mix_ts"])
    u, wqd, ku, attn, dl = _gdn_prep(proj, gdn_conv_w, ab_t, gdn_a_log, gdn_dt_bias, tiles["mix_ts"])
    o_b = _gdn_scan(u, wqd, ku, attn, dl, proj, gdn_onorm_g, tiles["mix_ts"])
    return _merge(o_a, o_b, proj, x2d, w_branch_hgrn.astype(BF16), w_branch_gdn.astype(BF16),
                  w_out.astype(BF16), gt1, norm2_g, sc2, sh2, tiles["merge_tm"])


def _moe(x1, h2, mod, norm2_g, normf_g, w_router, router_bias, w_exp_gate, w_exp_up, w_exp_down, w_sh_gate,
         w_sh_up, w_sh_down, tiles):
    s, d = x1.shape
    tm = tiles["moe_tm"]
    sh2, sc2, gt2 = [mod[:, i * d:(i + 1) * d] for i in (3, 4, 5)]
    pos_t, wcol, before, ntile = _router(x1, norm2_g, sc2, sh2, w_router.T, router_bias.reshape(-1, 1), tm)
    bm = EXPERT_BLOCK
    n_blocks = -(-(s * TOP_K + N_EXPERTS * (bm - 1)) // bm)
    before = before[:, :, 0]
    ntile = ntile[:, :, 0]
    counts = before[-1] + ntile[-1]
    padded = (counts + bm - 1) // bm * bm
    pend = jnp.cumsum(padded).astype(I32)
    pstart = pend - padded
    block_start = jnp.arange(n_blocks, dtype=I32) * bm
    block_e = jnp.minimum(jnp.sum(pend[None, :] <= block_start[:, None], axis=1), N_EXPERTS - 1).astype(I32)
    n_used = pend[-1:] // bm
    run_len = ntile.reshape(-1)
    run_dst = (before + pstart[None, :]).reshape(-1)
    xs = _dispatch(run_len, run_dst, pstart + counts, pend, pos_t, h2, n_blocks * bm, tm)
    ys = _experts(block_e, n_used, xs, w_exp_gate, w_exp_up, w_exp_down)
    return _combine(run_len, run_dst, ys, h2, x1, wcol, w_sh_gate.astype(BF16), w_sh_up.astype(BF16),
                    w_sh_down.astype(BF16), gt2, normf_g, tm)


def _tiles(s):
    pick = lambda want: min(want, s)
    return dict(in_tm=pick(1024), in_tn=1024, mix_ts=pick(512), merge_tm=pick(512), moe_tm=pick(256))


def kernel(x, c, w_ada, b_ada, norm1_g, norm2_g, w_in, hgrn_lb_table, hgrn_onorm_g, gdn_conv_w, gdn_a_log, gdn_dt_bias, gdn_onorm_g, w_branch_hgrn, w_branch_gdn, w_out, w_router, router_bias, w_exp_gate, w_exp_up, w_exp_down, w_sh_gate, w_sh_up, w_sh_down, normf_g):
    b, s, d = x.shape
    assert b == 1 and w_ada.shape[0] == 1, "one sequence, one layer"
    tiles = _tiles(s)
    lb = jnp.cumsum(jax.nn.softmax(hgrn_lb_table.astype(F32), axis=0), axis=0)[0:1]
    mod = _ada(c, w_ada[0], b_ada[0])
    row = lambda v: v.reshape(1, -1)
    x1, h2 = _mixer(x[0], mod, row(norm1_g[0]), row(norm2_g[0]), w_in[0], lb, row(hgrn_onorm_g[0]), gdn_conv_w[0],
                    gdn_a_log[0], gdn_dt_bias[0], row(gdn_onorm_g[0]), w_branch_hgrn[0], w_branch_gdn[0], w_out[0],
                    tiles)
    out = _moe(x1, h2, mod, row(norm2_g[0]), row(normf_g), w_router[0], router_bias[0], w_exp_gate[0],
               w_exp_up[0], w_exp_down[0], w_sh_gate[0], w_sh_up[0], w_sh_down[0], tiles)
    return out[None]
```

```python
import functools

import jax
import jax.numpy as jnp
from jax import lax
from jax.experimental import pallas as pl
from jax.experimental.pallas import tpu as pltpu

F32 = jnp.float32
BF16 = jnp.bfloat16
I32 = jnp.int32
U32 = jnp.uint32

NORM_EPS = 1e-6
L2_EPS = 1e-6
HEADS = 8
HEAD_DIM = 128
CONV_WIDTH = 4
CHUNK = 64
N_EXPERTS = 64
N_GROUPS = 8
GROUP_SIZE = N_EXPERTS // N_GROUPS
TOPK_GROUPS = 4
TOP_K = 8
ROUTED_SCALE = 2.5
EXPERT_BLOCK = 256

LANES = 128
SUBLANES = 8
VMEM_LIMIT = 56 * 1024 * 1024

NT = (((1,), (1,)), ((), ()))
TN = (((0,), (0,)), ((), ()))


def _params(sem, **kw):
    return pltpu.CompilerParams(dimension_semantics=sem, vmem_limit_bytes=VMEM_LIMIT, **kw)


def _dot(a, b):
    return jnp.dot(a, b, preferred_element_type=F32)


def _dg(a, b, dims):
    return lax.dot_general(a, b, dims, preferred_element_type=F32)


def _split(x):
    hi = x.astype(BF16)
    lo = (x - hi.astype(F32)).astype(BF16)
    return hi, lo


def _dot_exact_lhs(a_bf16, x, dims=None):
    hi, lo = _split(x)
    if dims is None:
        return _dot(a_bf16, hi) + _dot(a_bf16, lo)
    return _dg(a_bf16, hi, dims) + _dg(a_bf16, lo, dims)


def _sigmoid(x):
    return 1.0 / (1.0 + jnp.exp(-x))


def _silu(x):
    return x * _sigmoid(x)


def _rms(x, eps):
    return x * lax.rsqrt(jnp.mean(x * x, axis=-1, keepdims=True) + eps)


def _iota2(shape, dim):
    return lax.broadcasted_iota(I32, shape, dim)


def _pack_halves(lo, hi):
    lo_bits = lax.shift_right_logical(pltpu.bitcast(lo, U32), U32(16))
    hi_bits = pltpu.bitcast(hi, U32) & U32(0xFFFF0000)
    return lo_bits | hi_bits


def _unpack_halves(word):
    lo = pltpu.bitcast(lax.shift_left(word, U32(16)), F32)
    hi = pltpu.bitcast(word & U32(0xFFFF0000), F32)
    return lo, hi


def _round_bf16(x):
    return x.astype(BF16).astype(F32)


def _ada_kernel(c_ref, w_ref, b_ref, o_ref):
    cond = _silu(c_ref[...])
    o_ref[...] = jnp.dot(cond, w_ref[...], preferred_element_type=F32,
                         precision=lax.Precision.HIGHEST) + b_ref[...]


def _ada(c, w_ada, b_ada):
    d, n = w_ada.shape
    tn = 1024
    c8 = jnp.broadcast_to(c, (SUBLANES, d))
    out = pl.pallas_call(
        _ada_kernel,
        grid=(n // tn,),
        in_specs=[pl.BlockSpec((SUBLANES, d), lambda j: (0, 0)),
                  pl.BlockSpec((d, tn), lambda j: (0, j)),
                  pl.BlockSpec((1, tn), lambda j: (0, j))],
        out_specs=pl.BlockSpec((SUBLANES, tn), lambda j: (0, j)),
        out_shape=jax.ShapeDtypeStruct((SUBLANES, n), F32),
        compiler_params=_params(("arbitrary",)),
    )(c8, w_ada, b_ada.reshape(1, n))
    return out[0:1]


def _inproj_kernel(x_ref, g_ref, sc_ref, sh_ref, w_ref, wst_ref, proj_ref, smallt_ref, h_scr):
    @pl.when(pl.program_id(1) == 0)
    def _():
        h = _rms(x_ref[...], NORM_EPS) * g_ref[...] * (1.0 + sc_ref[...]) + sh_ref[...]
        hb = h.astype(BF16)
        h_scr[...] = hb
        smallt_ref[...] = _dg(wst_ref[...], hb, NT)

    proj_ref[...] = _dot(h_scr[...], w_ref[...]).astype(BF16)


def _inproj(x, g, sc, sh, w_main, w_small_t, tm, tn):
    s, d = x.shape
    n = w_main.shape[1]
    ns = w_small_t.shape[0]
    row = lambda i, j: (0, 0)
    return pl.pallas_call(
        _inproj_kernel,
        grid=(s // tm, n // tn),
        in_specs=[pl.BlockSpec((tm, d), lambda i, j: (i, 0)),
                  pl.BlockSpec((1, d), row), pl.BlockSpec((1, d), row), pl.BlockSpec((1, d), row),
                  pl.BlockSpec((d, tn), lambda i, j: (0, j)),
                  pl.BlockSpec((ns, d), row)],
        out_specs=[pl.BlockSpec((tm, tn), lambda i, j: (i, j)),
                   pl.BlockSpec((ns, tm), lambda i, j: (0, i))],
        out_shape=[jax.ShapeDtypeStruct((s, n), BF16), jax.ShapeDtypeStruct((ns, s), F32)],
        scratch_shapes=[pltpu.VMEM((tm, d), BF16)],
        compiler_params=_params(("arbitrary", "arbitrary")),
    )(x, g, sc, sh, w_main, w_small_t)


def _hgrn_kernel(q_ref, f_ref, i_ref, g_ref, lb_ref, on_ref, o_ref, st_scr, *, n_chunks):
    @pl.when(pl.program_id(0) == 0)
    def _():
        st_scr[...] = jnp.zeros_like(st_scr)

    c = CHUNK
    hd = HEAD_DIM
    causal = _iota2((c, c), 1) <= _iota2((c, c), 0)
    tri = causal.astype(BF16)
    lb = lb_ref[...]
    on_g = on_ref[...]
    heads = [slice(h * hd, (h + 1) * hd) for h in range(HEADS)]

    def chunk(n, carry):
        rows = pl.ds(pl.multiple_of(n * c, c), c)
        f = lb + (1.0 - lb) * _sigmoid(f_ref[rows, :].astype(F32))
        b = _dot_exact_lhs(tri, jnp.log(f))
        k = 1.0 - f
        q = _silu(q_ref[rows, :].astype(F32)) * (hd ** -0.5)
        v = i_ref[rows, :]
        b_mid = b[c // 2:c // 2 + 1, :]
        b_last = b[c - 1:c, :]
        qa = (q * jnp.exp(b - b_mid)).astype(BF16)
        ka = (k * jnp.exp(b_mid - b)).astype(BF16)
        qi = (q * jnp.exp(b)).astype(BF16)
        ku = (k * jnp.exp(b_last - b)).astype(BF16)
        dec = jnp.exp(b_last)
        gate = on_g * _silu(g_ref[rows, :].astype(F32))
        sts = [st_scr[h] for h in range(HEADS)]
        scores = [jnp.where(causal, _dg(qa[:, sl], ka[:, sl], NT), 0.0).astype(BF16) for sl in heads]
        inter = [_dg(qi[:, sl], st.astype(BF16), NT) for sl, st in zip(heads, sts)]
        kv = [_dg(v[:, sl], ku[:, sl], TN) for sl in heads]
        for h, sl in enumerate(heads):
            st_scr[h] = dec[:, sl] * sts[h] + kv[h]
        outs = [_rms(_dot(sc, v[:, sl]) + it, NORM_EPS) for sc, sl, it in zip(scores, heads, inter)]
        o_ref[rows, :] = (jnp.concatenate(outs, axis=1) * gate).astype(BF16)
        return carry

    lax.fori_loop(0, n_chunks, chunk, 0)


def _hgrn(proj, lb, onorm_g, ts):
    s = proj.shape[0]
    width = HEADS * HEAD_DIM
    col = lambda blk: pl.BlockSpec((ts, width), lambda j, blk=blk: (j, blk))
    const = pl.BlockSpec((1, width), lambda j: (0, 0))
    return pl.pallas_call(
        functools.partial(_hgrn_kernel, n_chunks=ts // CHUNK),
        grid=(s // ts,),
        in_specs=[col(0), col(1), col(2), col(3), const, const],
        out_specs=pl.BlockSpec((ts, width), lambda j: (j, 0)),
        out_shape=jax.ShapeDtypeStruct((s, width), BF16),
        scratch_shapes=[pltpu.VMEM((HEADS, HEAD_DIM, HEAD_DIM), F32)],
        compiler_params=_params(("arbitrary",)),
    )(proj, proj, proj, proj, lb, jnp.tile(onorm_g, (1, HEADS)))


def _gdn_prep_kernel(q_ref, k_ref, v_ref, qp_ref, kp_ref, vp_ref, wq_ref, wk_ref, wv_ref, ab_ref, alog_ref,
                     dtb_ref, tri_ref, eye_ref, u_ref, wqd_ref, ku_ref, attn_ref, dl_ref, cat_scr, *, n_chunks, ts):
    h = pl.program_id(0)
    first = pl.program_id(1) == 0
    c = CHUNK
    hd = HEAD_DIM

    def conv_silu(cur_ref, prev_ref, w_ref):
        cat_scr[0:8, :] = jnp.where(first, 0.0, prev_ref[...].astype(F32))
        cat_scr[8:8 + ts, :] = cur_ref[...].astype(F32)
        acc = None
        for j in range(CONV_WIDTH):
            off = 8 - (CONV_WIDTH - 1) + j
            term = cat_scr[off:off + ts, :] * w_ref[j:j + 1, :]
            acc = term if acc is None else acc + term
        return _silu(acc)

    def l2n(x):
        return x * lax.rsqrt(jnp.sum(x * x, axis=-1, keepdims=True) + L2_EPS)

    q_all = l2n(conv_silu(q_ref, qp_ref, wq_ref)) * (hd ** -0.5)
    k_all = l2n(conv_silu(k_ref, kp_ref, wk_ref))
    v_all = conv_silu(v_ref, vp_ref, wv_ref)

    a_row = ab_ref[0]
    b_row = ab_ref[1]
    z = a_row + dtb_ref[h]
    softplus = jnp.maximum(z, 0.0) + jnp.log(1.0 + jnp.exp(-jnp.abs(z)))
    ld_row = -jnp.exp(jnp.full(a_row.shape, alog_ref[h], F32)) * softplus
    beta_row = _sigmoid(b_row)
    tri_blocks = tri_ref[...]
    ld_b = jnp.broadcast_to(ld_row, (hd, ts))
    gc_all = _dot_exact_lhs(tri_blocks, ld_b, NT)
    hi, lo = _split(ld_b[0:8])
    g_rows = _dg(hi, tri_blocks, NT) + _dg(lo, tri_blocks, NT)
    bc_all = _dot_exact_lhs(eye_ref[...], jnp.broadcast_to(beta_row, (hd, ts)), NT)
    egc_all = jnp.exp(gc_all)

    r = _iota2((c, c), 0)
    cidx = _iota2((c, c), 1)
    causal = cidx <= r
    strict = cidx < r
    eye_f = (r == cidx).astype(F32)
    chunks = [slice(n * c, (n + 1) * c) for n in range(n_chunks)]

    q16 = q_all.astype(BF16)
    k16 = k_all.astype(BF16)
    kq = [_dg(jnp.concatenate([k16[sl], q16[sl]], axis=0), k16[sl], NT) for sl in chunks]
    dm = []
    for sl in chunks:
        diff = gc_all[sl, 0:c] - g_rows[0:1, sl]
        dm.append(jnp.where(causal, jnp.exp(jnp.where(causal, diff, 0.0)), 0.0))
    bm = [-jnp.where(strict, bc_all[sl, 0:c] * x[0:c] * d, 0.0) for sl, x, d in zip(chunks, kq, dm)]
    p = [eye_f + b for b in bm]
    bm = [_dot(b.astype(BF16), b.astype(BF16)) for b in bm]
    for _ in range(c.bit_length() - 3):
        res = [_dot(b.astype(BF16), jnp.concatenate([b, pp], axis=1).astype(BF16)) for b, pp in zip(bm, p)]
        p = [pp + x[:, c:2 * c] for pp, x in zip(p, res)]
        bm = [x[:, 0:c] for x in res]
    p = [pp + _dot(b.astype(BF16), pp.astype(BF16)) for b, pp in zip(bm, p)]
    rhs = jnp.concatenate([v_all * bc_all, k_all * (bc_all * egc_all)], axis=1).astype(BF16)
    sol = [_dot(pp.astype(BF16), rhs[sl]) for pp, sl in zip(p, chunks)]
    qd_all = (q_all * egc_all).astype(BF16)
    for n, sl in enumerate(chunks):
        g_last = gc_all[(n + 1) * c - 1:(n + 1) * c, :]
        u_ref[sl, :] = sol[n][:, 0:hd].astype(BF16)
        wqd_ref[2 * n * c:(2 * n + 1) * c, :] = sol[n][:, hd:2 * hd].astype(BF16)
        wqd_ref[(2 * n + 1) * c:(2 * n + 2) * c, :] = qd_all[sl]
        ku_ref[sl, :] = (k_all[sl] * jnp.exp(g_last - gc_all[sl])).astype(BF16)
        attn_ref[sl, :] = (kq[n][c:2 * c] * dm[n]).astype(BF16)
        dl_ref[n:n + 1, :] = jnp.exp(g_last)


def _gdn_prep(proj, conv_w, ab_t, a_log, dt_bias, ts):
    s = proj.shape[0]
    hd = HEAD_DIM
    c = CHUNK
    q0 = 4 * HEADS
    cur = lambda off: pl.BlockSpec((ts, hd), lambda h, j, off=off: (j, off + h))
    prev = lambda off: pl.BlockSpec((8, hd), lambda h, j, off=off: (jnp.maximum(j * (ts // 8) - 1, 0), off + h))
    cw = lambda off: pl.BlockSpec((CONV_WIDTH, hd), lambda h, j, off=off: (0, off + h))
    smem = pl.BlockSpec(memory_space=pltpu.SMEM)
    const = pl.BlockSpec((ts, ts), lambda h, j: (0, 0))
    pos = jnp.arange(ts)
    tri_blocks = ((pos[:, None] // c == pos[None, :] // c) & (pos[None, :] <= pos[:, None])).astype(BF16)
    eye = (pos[:, None] == pos[None, :]).astype(BF16)
    per_head = lambda rows, cols: pl.BlockSpec((None, rows, cols), lambda h, j: (h, j, 0))
    return pl.pallas_call(
        functools.partial(_gdn_prep_kernel, n_chunks=ts // c, ts=ts),
        grid=(HEADS, s // ts),
        in_specs=[cur(q0), cur(q0 + HEADS), cur(q0 + 2 * HEADS),
                  prev(q0), prev(q0 + HEADS), prev(q0 + 2 * HEADS),
                  cw(0), cw(HEADS), cw(2 * HEADS),
                  pl.BlockSpec((2, None, 1, ts), lambda h, j: (0, h, 0, j)),
                  smem, smem, const, const],
        out_specs=[pl.BlockSpec((ts, hd), lambda h, j: (j, h)),
                   pl.BlockSpec((2 * ts, hd), lambda h, j: (j, h)),
                   pl.BlockSpec((ts, hd), lambda h, j: (j, h)),
                   per_head(ts, c),
                   per_head(ts // c, hd)],
        out_shape=[jax.ShapeDtypeStruct((s, HEADS * hd), BF16),
                   jax.ShapeDtypeStruct((2 * s, HEADS * hd), BF16),
                   jax.ShapeDtypeStruct((s, HEADS * hd), BF16),
                   jax.ShapeDtypeStruct((HEADS, s, c), BF16),
                   jax.ShapeDtypeStruct((HEADS, s // c, hd), F32)],
        scratch_shapes=[pltpu.VMEM((ts + 8, hd), F32)],
        compiler_params=_params(("arbitrary", "arbitrary")),
    )(proj, proj, proj, proj, proj, proj, conv_w, conv_w, conv_w,
      ab_t.reshape(2, HEADS, 1, s), a_log, dt_bias, tri_blocks, eye)


def _gdn_scan_kernel(u_ref, wqd_ref, ku_ref, attn_ref, dl_ref, g_ref, on_ref, o_ref, st_scr, *, n_chunks):
    @pl.when(pl.program_id(0) == 0)
    def _():
        st_scr[...] = jnp.zeros_like(st_scr)

    c = CHUNK
    hd = HEAD_DIM
    on_g = on_ref[...]
    heads = [slice(h * hd, (h + 1) * hd) for h in range(HEADS)]

    def chunk(n, carry):
        rows = pl.ds(pl.multiple_of(n * c, c), c)
        rows2 = pl.ds(pl.multiple_of(2 * n * c, 2 * c), 2 * c)
        sts = [st_scr[h] for h in range(HEADS)]
        wq = [_dot(wqd_ref[rows2, sl], st.astype(BF16)) for sl, st in zip(heads, sts)]
        vn = [(u_ref[rows, sl].astype(F32) - x[0:c]).astype(BF16) for sl, x in zip(heads, wq)]
        upd = [_dg(ku_ref[rows, sl], v, TN) for sl, v in zip(heads, vn)]
        for h in range(HEADS):
            st_scr[h] = dl_ref[h, pl.ds(n, 1), :] * sts[h] + upd[h]
        outs = [_rms(x[c:2 * c] + _dot(attn_ref[h, rows, :], v), NORM_EPS)
                for h, (x, v) in enumerate(zip(wq, vn))]
        gate = jnp.tile(on_g, (1, HEADS)) * _silu(g_ref[rows, :].astype(F32))
        o_ref[rows, :] = (jnp.concatenate(outs, axis=1) * gate).astype(BF16)
        return carry

    lax.fori_loop(0, n_chunks, chunk, 0)


def _gdn_scan(u, wqd, ku, attn, dl, proj, onorm_g, ts):
    s, width = u.shape
    c = CHUNK
    gate_blk = (4 * HEADS + 3 * HEADS) * HEAD_DIM // width
    return pl.pallas_call(
        functools.partial(_gdn_scan_kernel, n_chunks=ts // c),
        grid=(s // ts,),
        in_specs=[pl.BlockSpec((ts, width), lambda j: (j, 0)),
                  pl.BlockSpec((2 * ts, width), lambda j: (j, 0)),
                  pl.BlockSpec((ts, width), lambda j: (j, 0)),
                  pl.BlockSpec((HEADS, ts, c), lambda j: (0, j, 0)),
                  pl.BlockSpec((HEADS, ts // c, HEAD_DIM), lambda j: (0, j, 0)),
                  pl.BlockSpec((ts, width), lambda j: (j, gate_blk)),
                  pl.BlockSpec((1, HEAD_DIM), lambda j: (0, 0))],
        out_specs=pl.BlockSpec((ts, width), lambda j: (j, 0)),
        out_shape=jax.ShapeDtypeStruct((s, width), BF16),
        scratch_shapes=[pltpu.VMEM((HEADS, HEAD_DIM, HEAD_DIM), F32)],
        compiler_params=_params(("arbitrary",)),
    )(u, wqd, ku, attn, dl, proj, onorm_g)


def _merge_kernel(oa_ref, ob_ref, mga_ref, mgb_ref, x_ref, wa_ref, wb_ref, wo_ref, gt_ref, g2_ref, sc_ref,
                  sh_ref, x1_ref, h2_ref):
    ya = _dot(oa_ref[...], wa_ref[...])
    yb = _dot(ob_ref[...], wb_ref[...])
    merged = _sigmoid(mga_ref[...].astype(F32)) * ya + _sigmoid(mgb_ref[...].astype(F32)) * yb
    x1 = x_ref[...] + gt_ref[...] * _dot(merged.astype(BF16), wo_ref[...])
    x1_ref[...] = x1
    h2 = _rms(x1, NORM_EPS) * g2_ref[...] * (1.0 + sc_ref[...]) + sh_ref[...]
    h2_ref[...] = h2.astype(BF16)


def _merge(o_a, o_b, proj, x, w_a, w_b, w_o, gt1, g2, sc2, sh2, tm):
    s, d = x.shape
    dv = o_a.shape[1]
    mg0 = (8 * HEADS * HEAD_DIM) // d
    const = lambda shape: pl.BlockSpec(shape, lambda i: (0, 0), pipeline_mode=pl.Buffered(1))
    return pl.pallas_call(
        _merge_kernel,
        grid=(s // tm,),
        in_specs=[pl.BlockSpec((tm, dv), lambda i: (i, 0)),
                  pl.BlockSpec((tm, dv), lambda i: (i, 0)),
                  pl.BlockSpec((tm, d), lambda i: (i, mg0)),
                  pl.BlockSpec((tm, d), lambda i: (i, mg0 + 1)),
                  pl.BlockSpec((tm, d), lambda i: (i, 0)),
                  const((dv, d)), const((dv, d)), const((d, d)),
                  const((1, d)), const((1, d)), const((1, d)), const((1, d))],
        out_specs=[pl.BlockSpec((tm, d), lambda i: (i, 0)), pl.BlockSpec((tm, d), lambda i: (i, 0))],
        out_shape=[jax.ShapeDtypeStruct((s, d), F32), jax.ShapeDtypeStruct((s, d), BF16)],
        compiler_params=_params(("arbitrary",)),
    )(o_a, o_b, proj, proj, x, w_a, w_b, w_o, gt1, g2, sc2, sh2)


def _first_max(vals, iota, size, axis):
    m = jnp.max(vals, axis=axis, keepdims=True)
    idx = jnp.min(jnp.where(vals == m, iota, size), axis=axis, keepdims=True)
    return m, idx


def _router_kernel(x1_ref, g2_ref, sc_ref, sh_ref, wrt_ref, bias_ref, upper_ref, pos_ref, wcol_ref, before_ref,
                   ntile_ref, cnt_scr, *, tm):
    @pl.when(pl.program_id(0) == 0)
    def _():
        cnt_scr[...] = jnp.zeros_like(cnt_scr)

    e = N_EXPERTS
    h2 = _rms(x1_ref[...], NORM_EPS) * g2_ref[...] * (1.0 + sc_ref[...]) + sh_ref[...]
    logits = lax.dot_general(wrt_ref[...], h2, NT, preferred_element_type=F32,
                             precision=lax.Precision.HIGHEST)
    scores = _sigmoid(logits)
    biased = scores + bias_ref[...]
    neg = -jnp.inf

    g3 = biased.reshape(N_GROUPS, GROUP_SIZE, tm)
    i3 = lax.broadcasted_iota(I32, g3.shape, 1)
    m1, a1 = _first_max(g3, i3, GROUP_SIZE, 1)
    m2 = jnp.max(jnp.where(i3 == a1, neg, g3), axis=1, keepdims=True)
    gs = (m1 + m2).reshape(N_GROUPS, tm)
    ig = _iota2(gs.shape, 0)
    gmask = jnp.zeros(gs.shape, jnp.bool_)
    for _ in range(TOPK_GROUPS):
        _, a = _first_max(gs, ig, N_GROUPS, 0)
        pick = ig == a
        gmask = jnp.logical_or(gmask, pick)
        gs = jnp.where(pick, neg, gs)
    emask = jnp.broadcast_to(gmask.reshape(N_GROUPS, 1, tm), (N_GROUPS, GROUP_SIZE, tm)).reshape(e, tm)

    cand = jnp.where(emask, biased, neg)
    ie = _iota2((e, tm), 0)
    sel_all = jnp.zeros((e, tm), jnp.bool_)
    w_rows, picks = [], []
    for _ in range(TOP_K):
        _, a = _first_max(cand, ie, e, 0)
        pick = ie == a
        picks.append(pick)
        w_rows.append(jnp.sum(jnp.where(pick, scores, 0.0), axis=0, keepdims=True))
        sel_all = jnp.logical_or(sel_all, pick)
        cand = jnp.where(pick, neg, cand)
    w_sum = w_rows[0]
    for wr in w_rows[1:]:
        w_sum = w_sum + wr
    wts = jnp.concatenate(w_rows, axis=0) / w_sum * ROUTED_SCALE

    sel = sel_all.astype(BF16)
    in_expert = _dot(sel, upper_ref[...])
    n_tile = jnp.sum(sel_all.astype(F32), axis=1, keepdims=True)
    lower = (_iota2((e, e), 1) < _iota2((e, e), 0)).astype(BF16)
    expert_off = _dot_exact_lhs(lower, jnp.broadcast_to(n_tile, (e, LANES)))[:, 0:1]
    place = in_expert + expert_off
    pos = jnp.concatenate([jnp.sum(jnp.where(pk, place, 0.0), axis=0, keepdims=True) for pk in picks], axis=0)
    pos_ref[...] = pos.astype(I32)
    before_ref[...] = jnp.broadcast_to(cnt_scr[...], before_ref.shape).astype(I32)
    ntile_ref[...] = jnp.broadcast_to(n_tile, ntile_ref.shape).astype(I32)
    cnt_scr[...] = cnt_scr[...] + n_tile

    eye = (_iota2((tm, tm), 0) == _iota2((tm, tm), 1)).astype(BF16)
    rows = jnp.concatenate([wts, pos, jnp.zeros((LANES - 2 * TOP_K, tm), F32)], axis=0)
    hi, lo = _split(rows)
    lo2 = (rows - hi.astype(F32) - lo.astype(F32)).astype(BF16)
    wcol_ref[...] = _dg(eye, hi, NT) + _dg(eye, lo, NT) + _dg(eye, lo2, NT)


def _router(x1, g2, sc2, sh2, w_router_t, bias_col, tm):
    s, d = x1.shape
    e = N_EXPERTS
    nt = s // tm
    upper = (jnp.arange(tm)[:, None] < jnp.arange(tm)[None, :]).astype(BF16)
    const = lambda shape: pl.BlockSpec(shape, lambda i: (0, 0))
    per_tile = pl.BlockSpec((None, e, LANES), lambda i: (i, 0, 0))
    return pl.pallas_call(
        functools.partial(_router_kernel, tm=tm),
        grid=(nt,),
        in_specs=[pl.BlockSpec((tm, d), lambda i: (i, 0)),
                  const((1, d)), const((1, d)), const((1, d)),
                  const((e, d)), const((e, 1)), const((tm, tm))],
        out_specs=[pl.BlockSpec((TOP_K, tm), lambda i: (0, i)),
                   pl.BlockSpec((tm, LANES), lambda i: (i, 0)),
                   per_tile, per_tile],
        out_shape=[jax.ShapeDtypeStruct((TOP_K, s), I32), jax.ShapeDtypeStruct((s, LANES), F32),
                   jax.ShapeDtypeStruct((nt, e, LANES), I32), jax.ShapeDtypeStruct((nt, e, LANES), I32)],
        scratch_shapes=[pltpu.VMEM((e, 1), F32)],
        compiler_params=_params(("arbitrary",)),
    )(x1, g2, sc2, sh2, w_router_t, bias_col, upper)


def _run_sizes(limit):
    return [1 << b for b in range(limit.bit_length() - 1, -1, -1)]


def _for_each_run(step, run_len_ref, run_dst_ref, tm, make_copy, fn):
    def per_expert(ex, tile_slot):
        n = run_len_ref[step * N_EXPERTS + ex]
        dst = run_dst_ref[step * N_EXPERTS + ex]
        for size in _run_sizes(tm):
            take = (n & size) != 0

            @pl.when(take)
            def _(tile_slot=tile_slot, dst=dst, size=size):
                fn(make_copy(tile_slot, dst, size))

            inc = jnp.where(take, size, 0)
            tile_slot = tile_slot + inc
            dst = dst + inc
        return tile_slot

    lax.fori_loop(0, N_EXPERTS, per_expert, jnp.int32(0))


def _slot_rows(slot, n_slots):
    return pl.ds(pl.multiple_of(slot * SUBLANES, SUBLANES), n_slots * SUBLANES)


def _dispatch_kernel(run_len_ref, run_dst_ref, pad_lo_ref, pad_hi_ref, pos_ref, h_ref, xs_ref, stage, zero_scr,
                     sem, pad_sem, *, tm, rows_per_pass):
    step = pl.program_id(0)
    na = TOP_K * tm
    d = h_ref.shape[1]
    half = d // 2
    n_words = half // LANES

    def pad_copy(slot, n_slots):
        return pltpu.make_async_copy(zero_scr.at[pl.ds(0, n_slots * SUBLANES), :],
                                     xs_ref.at[_slot_rows(slot, n_slots), :], pad_sem)

    def for_each_pad(fn):
        def per_expert(ex, carry):
            slot = pad_lo_ref[ex]
            n = pad_hi_ref[ex] - slot
            for size in _run_sizes(EXPERT_BLOCK - 1):
                take = (n & size) != 0

                @pl.when(take)
                def _(slot=slot, size=size):
                    fn(pad_copy(slot, size))

                slot = slot + jnp.where(take, size, 0)
            return carry
        lax.fori_loop(0, N_EXPERTS, per_expert, 0)

    @pl.when(step == 0)
    def _():
        zero_scr[...] = jnp.zeros_like(zero_scr)
        for_each_pad(lambda cp: cp.start())

    buf = step % 2
    pos = pos_ref[...]
    h = h_ref[...]
    for a0 in range(0, na, rows_per_pass):
        slot_id = a0 + _iota2((rows_per_pass, tm), 0)
        hit = pos[0:1, :] == slot_id
        for k in range(1, TOP_K):
            hit = jnp.logical_or(hit, pos[k:k + 1, :] == slot_id)
        rows = _dot(hit.astype(BF16), h)
        for i in range(n_words):
            word = _pack_halves(rows[:, i * LANES:(i + 1) * LANES], rows[:, half + i * LANES:half + (i + 1) * LANES])
            stage[buf, pl.ds(a0 * SUBLANES + i, rows_per_pass, stride=SUBLANES), :] = word

    def run_copy(tile_slot, sorted_slot, n_slots):
        return pltpu.make_async_copy(stage.at[buf, _slot_rows(tile_slot, n_slots), :],
                                     xs_ref.at[_slot_rows(sorted_slot, n_slots), :], sem.at[buf])

    def wait_tile(which):
        pltpu.make_async_copy(stage.at[which], xs_ref.at[pl.ds(0, na * SUBLANES), :], sem.at[which]).wait()

    _for_each_run(step, run_len_ref, run_dst_ref, tm, run_copy, lambda cp: cp.start())

    @pl.when(step > 0)
    def _():
        wait_tile(1 - buf)

    @pl.when(step == pl.num_programs(0) - 1)
    def _():
        wait_tile(buf)

    @pl.when(step == 0)
    def _():
        for_each_pad(lambda cp: cp.wait())


def _dispatch(run_len, run_dst, pad_lo, pad_hi, pos_t, h2, n_slots, tm):
    s, d = h2.shape
    assert (d // 2) % LANES == 0 and (d // 2) // LANES == SUBLANES, "one token row must pack into one (8, 128) tile"
    na = TOP_K * tm
    return pl.pallas_call(
        functools.partial(_dispatch_kernel, tm=tm, rows_per_pass=min(512, na)),
        grid_spec=pltpu.PrefetchScalarGridSpec(
            num_scalar_prefetch=4,
            grid=(s // tm,),
            in_specs=[pl.BlockSpec((TOP_K, tm), lambda i, *_: (0, i)),
                      pl.BlockSpec((tm, d), lambda i, *_: (i, 0))],
            out_specs=pl.BlockSpec(memory_space=pl.ANY),
            scratch_shapes=[pltpu.VMEM((2, na * SUBLANES, LANES), U32),
                            pltpu.VMEM((EXPERT_BLOCK // 2 * SUBLANES, LANES), U32),
                            pltpu.SemaphoreType.DMA((2,)), pltpu.SemaphoreType.DMA(())]),
        out_shape=jax.ShapeDtypeStruct((n_slots * SUBLANES, LANES), U32),
        compiler_params=_params(("arbitrary",), has_side_effects=True, disable_bounds_checks=True),
    )(run_len, run_dst, pad_lo, pad_hi, pos_t, h2)


def _expert_kernel(be_ref, nu_ref, next_ref, par_ref, x_ref, wg_hbm, wu_hbm, wd_hbm, y_ref, wg_f32, wu_f32, wd_f32,
                   wg_scr, wu_scr, wd_scr, sem):
    b = pl.program_id(0)
    bm = EXPERT_BLOCK
    active = b < nu_ref[0]
    new_expert = jnp.logical_or(b == 0, be_ref[b] != be_ref[jnp.maximum(b - 1, 0)])

    def weight_copies(ex, which):
        return [pltpu.make_async_copy(src.at[ex], dst.at[which], sem.at[which])
                for src, dst in ((wg_hbm, wg_f32), (wu_hbm, wu_f32), (wd_hbm, wd_f32))]

    @pl.when(jnp.logical_and(active, new_expert))
    def _():
        which = par_ref[b]

        @pl.when(b == 0)
        def _():
            for cp in weight_copies(be_ref[b], which):
                cp.start()

        for cp in weight_copies(be_ref[b], which):
            cp.wait()
        wg_scr[...] = wg_f32[which].astype(BF16)
        wu_scr[...] = wu_f32[which].astype(BF16)
        wd_scr[...] = wd_f32[which].astype(BF16)
        nb = next_ref[b]

        @pl.when(nb < nu_ref[0])
        def _():
            for cp in weight_copies(be_ref[nb], 1 - which):
                cp.start()

    @pl.when(active)
    def _():
        los, his = [], []
        for i in range(SUBLANES):
            lo, hi = _unpack_halves(x_ref[pl.ds(i, bm, stride=SUBLANES), :])
            los.append(lo.astype(BF16))
            his.append(hi.astype(BF16))
        xb = jnp.concatenate(los + his, axis=1)
        hid = _silu(_dot(xb, wg_scr[...])) * _dot(xb, wu_scr[...])
        y = _dot(hid.astype(BF16), wd_scr[...])
        half = y.shape[1] // 2
        for i in range(SUBLANES):
            word = _pack_halves(_round_bf16(y[:, i * LANES:(i + 1) * LANES]),
                                _round_bf16(y[:, half + i * LANES:half + (i + 1) * LANES]))
            y_ref[pl.ds(i, bm, stride=SUBLANES), :] = word


def _experts(block_e, n_used, next_block, parity, xs, w_gate, w_up, w_down):
    d, ff = w_gate.shape[1], w_gate.shape[2]
    bm = EXPERT_BLOCK
    n_blocks = xs.shape[0] // (bm * SUBLANES)
    blk = lambda b, be, nu, *_: (jnp.minimum(b, nu[0] - 1), 0)
    hbm = pl.BlockSpec(memory_space=pl.ANY)
    return pl.pallas_call(
        _expert_kernel,
        grid_spec=pltpu.PrefetchScalarGridSpec(
            num_scalar_prefetch=4,
            grid=(n_blocks,),
            in_specs=[pl.BlockSpec((bm * SUBLANES, LANES), blk), hbm, hbm, hbm],
            out_specs=pl.BlockSpec((bm * SUBLANES, LANES), blk),
            scratch_shapes=[pltpu.VMEM((2, d, ff), F32), pltpu.VMEM((2, d, ff), F32), pltpu.VMEM((2, ff, d), F32),
                            pltpu.VMEM((d, ff), BF16), pltpu.VMEM((d, ff), BF16), pltpu.VMEM((ff, d), BF16),
                            pltpu.SemaphoreType.DMA((2,))]),
        out_shape=jax.ShapeDtypeStruct(xs.shape, U32),
        compiler_params=_params(("arbitrary",)),
    )(block_e, n_used, next_block, parity, xs, w_gate, w_up, w_down)


def _combine_kernel(run_len_ref, run_src_ref, ys_ref, h_ref, x1_ref, wcol_ref, wg_ref, wu_ref, wd_ref, gt_ref, gf_ref,
                    o_ref, stage, sem, *, tm, rows_per_pass):
    step = pl.program_id(0)
    na = TOP_K * tm
    buf = step % 2

    def fetch_tile(tile, which):
        def run_copy(tile_slot, sorted_slot, n_slots):
            return pltpu.make_async_copy(ys_ref.at[_slot_rows(sorted_slot, n_slots), :],
                                         stage.at[which, _slot_rows(tile_slot, n_slots), :], sem.at[which])
        _for_each_run(tile, run_len_ref, run_src_ref, tm, run_copy, lambda cp: cp.start())

    @pl.when(step == 0)
    def _():
        fetch_tile(step, buf)

    @pl.when(step + 1 < pl.num_programs(0))
    def _():
        fetch_tile(step + 1, 1 - buf)

    hb = h_ref[...]
    hid = _silu(_dot(hb, wg_ref[...])) * _dot(hb, wu_ref[...])
    acc = _dot(hid.astype(BF16), wd_ref[...])

    pltpu.make_async_copy(ys_ref.at[pl.ds(0, na * SUBLANES), :], stage.at[buf], sem.at[buf]).wait()

    wcol = wcol_ref[...]
    at = [wcol[:, TOP_K + k:TOP_K + k + 1].astype(I32) for k in range(TOP_K)]
    for a0 in range(0, na, rows_per_pass):
        los, his = [], []
        for i in range(SUBLANES):
            lo, hi = _unpack_halves(stage[buf, pl.ds(a0 * SUBLANES + i, rows_per_pass, stride=SUBLANES), :])
            los.append(lo.astype(BF16))
            his.append(hi.astype(BF16))
        y_rows = jnp.concatenate(los + his, axis=1)
        slot_id = a0 + _iota2((tm, rows_per_pass), 1)
        wmat = jnp.zeros((tm, rows_per_pass), F32)
        for k in range(TOP_K):
            wmat = wmat + jnp.where(at[k] == slot_id, wcol[:, k:k + 1], 0.0)
        w_hi, w_lo = _split(wmat)
        acc = acc + _dot(w_hi, y_rows) + _dot(w_lo, y_rows)
    x2 = x1_ref[...] + gt_ref[...] * acc
    o_ref[...] = _rms(x2, NORM_EPS) * gf_ref[...]


def _combine(run_len, run_src, ys, h2, x1, wcol, w_gate, w_up, w_down, gt2, gf, tm):
    s, d = x1.shape
    ff = w_gate.shape[1]
    na = TOP_K * tm
    const = lambda shape: pl.BlockSpec(shape, lambda i, *_: (0, 0), pipeline_mode=pl.Buffered(1))
    tile = lambda cols: pl.BlockSpec((tm, cols), lambda i, *_: (i, 0))
    return pl.pallas_call(
        functools.partial(_combine_kernel, tm=tm, rows_per_pass=min(512, na)),
        grid_spec=pltpu.PrefetchScalarGridSpec(
            num_scalar_prefetch=2,
            grid=(s // tm,),
            in_specs=[pl.BlockSpec(memory_space=pl.ANY),
                      tile(d), tile(d), tile(LANES),
                      const((d, ff)), const((d, ff)), const((ff, d)), const((1, d)), const((1, d))],
            out_specs=tile(d),
            scratch_shapes=[pltpu.VMEM((2, na * SUBLANES, LANES), U32), pltpu.SemaphoreType.DMA((2,))]),
        out_shape=jax.ShapeDtypeStruct((s, d), F32),
        compiler_params=_params(("arbitrary",), disable_bounds_checks=True),
    )(run_len, run_src, ys, h2, x1, wcol, w_gate, w_up, w_down, gt2, gf)


def _mixer(x2d, mod, norm1_g, norm2_g, w_in, lb, hgrn_onorm_g, gdn_conv_w, gdn_a_log, gdn_dt_bias, gdn_onorm_g,
           w_branch_hgrn, w_branch_gdn, w_out, tiles):
    d = x2d.shape[1]
    sh1, sc1, gt1, sh2, sc2, _ = [mod[:, i * d:(i + 1) * d] for i in range(6)]
    key = HEADS * HEAD_DIM
    small0 = 4 * key + 3 * key
    small1 = small0 + 2 * HEADS
    w_main = jnp.concatenate([w_in[:, :small0], w_in[:, small1:]], axis=1).astype(BF16)
    w_small_t = w_in[:, small0:small1].T.astype(BF16)
    proj, ab_t = _inproj(x2d, norm1_g, sc1, sh1, w_main, w_small_t, tiles["in_tm"], tiles["in_tn"])
    o_a = _hgrn(proj, lb, hgrn_onorm_g, tiles["mix_ts"])
    u, wqd, ku, attn, dl = _gdn_prep(proj, gdn_conv_w, ab_t, gdn_a_log, gdn_dt_bias, tiles["mix_ts"])
    o_b = _gdn_scan(u, wqd, ku, attn, dl, proj, gdn_onorm_g, tiles["mix_ts"])
    return _merge(o_a, o_b, proj, x2d, w_branch_hgrn.astype(BF16), w_branch_gdn.astype(BF16),
                  w_out.astype(BF16), gt1, norm2_g, sc2, sh2, tiles["merge_tm"])


def _moe(x1, h2, mod, norm2_g, normf_g, w_router, router_bias, w_exp_gate, w_exp_up, w_exp_down, w_sh_gate,
         w_sh_up, w_sh_down, tiles):
    s, d = x1.shape
    tm = tiles["moe_tm"]
    sh2, sc2, gt2 = [mod[:, i * d:(i + 1) * d] for i in (3, 4, 5)]
    pos_t, wcol, before, ntile = _router(x1, norm2_g, sc2, sh2, w_router.T, router_bias.reshape(-1, 1), tm)
    bm = EXPERT_BLOCK
    n_blocks = -(-(s * TOP_K + N_EXPERTS * (bm - 1)) // bm)
    before = before[:, :, 0]
    ntile = ntile[:, :, 0]
    counts = before[-1] + ntile[-1]
    padded = (counts + bm - 1) // bm * bm
    pend = jnp.cumsum(padded).astype(I32)
    pstart = pend - padded
    block_start = jnp.arange(n_blocks, dtype=I32) * bm
    block_e = jnp.minimum(jnp.sum(pend[None, :] <= block_start[:, None], axis=1), N_EXPERTS - 1).astype(I32)
    n_used = pend[-1:] // bm
    run_len = ntile.reshape(-1)
    run_dst = (before + pstart[None, :]).reshape(-1)
    xs = _dispatch(run_len, run_dst, pstart + counts, pend, pos_t, h2, n_blocks * bm, tm)
    next_block = pend[block_e] // bm
    switches = jnp.concatenate([jnp.zeros((1,), I32), (block_e[1:] != block_e[:-1]).astype(I32)])
    parity = jnp.cumsum(switches).astype(I32) % 2
    ys = _experts(block_e, n_used, next_block, parity, xs, w_exp_gate, w_exp_up, w_exp_down)
    return _combine(run_len, run_dst, ys, h2, x1, wcol, w_sh_gate.astype(BF16), w_sh_up.astype(BF16),
                    w_sh_down.astype(BF16), gt2, normf_g, tm)


def _tiles(s):
    pick = lambda want: min(want, s)
    return dict(in_tm=pick(1024), in_tn=1024, mix_ts=pick(512), merge_tm=pick(512), moe_tm=pick(256))


def kernel(x, c, w_ada, b_ada, norm1_g, norm2_g, w_in, hgrn_lb_table, hgrn_onorm_g, gdn_conv_w, gdn_a_log, gdn_dt_bias, gdn_onorm_g, w_branch_hgrn, w_branch_gdn, w_out, w_router, router_bias, w_exp_gate, w_exp_up, w_exp_down, w_sh_gate, w_sh_up, w_sh_down, normf_g):
    b, s, d = x.shape
    assert b == 1 and w_ada.shape[0] == 1, "one sequence, one layer"
    tiles = _tiles(s)
    lb = jnp.cumsum(jax.nn.softmax(hgrn_lb_table.astype(F32), axis=0), axis=0)[0:1]
    mod = _ada(c, w_ada[0], b_ada[0])
    row = lambda v: v.reshape(1, -1)
    x1, h2 = _mixer(x[0], mod, row(norm1_g[0]), row(norm2_g[0]), w_in[0], lb, row(hgrn_onorm_g[0]), gdn_conv_w[0],
                    gdn_a_log[0], gdn_dt_bias[0], row(gdn_onorm_g[0]), w_branch_hgrn[0], w_branch_gdn[0], w_out[0],
                    tiles)
    out = _moe(x1, h2, mod, row(norm2_g[0]), row(normf_g), w_router[0], router_bias[0], w_exp_gate[0],
               w_exp_up[0], w_exp_down[0], w_sh_gate[0], w_sh_up[0], w_sh_down[0], tiles)
    return out[None]
```

```python
import functools

import jax
import jax.numpy as jnp
from jax import lax
from jax.experimental import pallas as pl
from jax.experimental.pallas import tpu as pltpu

F32 = jnp.float32
BF16 = jnp.bfloat16
I32 = jnp.int32
U32 = jnp.uint32

NORM_EPS = 1e-6
L2_EPS = 1e-6
HEADS = 8
HEAD_DIM = 128
CONV_WIDTH = 4
CHUNK = 64
N_EXPERTS = 64
N_GROUPS = 8
GROUP_SIZE = N_EXPERTS // N_GROUPS
TOPK_GROUPS = 4
TOP_K = 8
ROUTED_SCALE = 2.5
EXPERT_BLOCK = 512

LANES = 128
SUBLANES = 8
VMEM_LIMIT = 56 * 1024 * 1024

NT = (((1,), (1,)), ((), ()))
TN = (((0,), (0,)), ((), ()))


def _params(sem, **kw):
    return pltpu.CompilerParams(dimension_semantics=sem, vmem_limit_bytes=VMEM_LIMIT, **kw)


def _dot(a, b):
    return jnp.dot(a, b, preferred_element_type=F32)


def _dg(a, b, dims):
    return lax.dot_general(a, b, dims, preferred_element_type=F32)


def _split(x):
    hi = x.astype(BF16)
    lo = (x - hi.astype(F32)).astype(BF16)
    return hi, lo


def _dot_exact_lhs(a_bf16, x, dims=None):
    hi, lo = _split(x)
    if dims is None:
        return _dot(a_bf16, hi) + _dot(a_bf16, lo)
    return _dg(a_bf16, hi, dims) + _dg(a_bf16, lo, dims)


def _sigmoid(x):
    return 1.0 / (1.0 + jnp.exp(-x))


def _silu(x):
    return x * _sigmoid(x)


def _rms(x, eps):
    return x * lax.rsqrt(jnp.mean(x * x, axis=-1, keepdims=True) + eps)


def _iota2(shape, dim):
    return lax.broadcasted_iota(I32, shape, dim)


def _pack_halves(lo, hi):
    lo_bits = lax.shift_right_logical(pltpu.bitcast(lo, U32), U32(16))
    hi_bits = pltpu.bitcast(hi, U32) & U32(0xFFFF0000)
    return lo_bits | hi_bits


def _unpack_halves(word):
    lo = pltpu.bitcast(lax.shift_left(word, U32(16)), F32)
    hi = pltpu.bitcast(word & U32(0xFFFF0000), F32)
    return lo, hi


def _round_bf16(x):
    return x.astype(BF16).astype(F32)


def _ada_kernel(c_ref, w_ref, b_ref, o_ref):
    cond = _silu(c_ref[...])
    o_ref[...] = jnp.dot(cond, w_ref[...], preferred_element_type=F32,
                         precision=lax.Precision.HIGHEST) + b_ref[...]


def _ada(c, w_ada, b_ada):
    d, n = w_ada.shape
    tn = 1024
    c8 = jnp.broadcast_to(c, (SUBLANES, d))
    out = pl.pallas_call(
        _ada_kernel,
        grid=(n // tn,),
        in_specs=[pl.BlockSpec((SUBLANES, d), lambda j: (0, 0)),
                  pl.BlockSpec((d, tn), lambda j: (0, j)),
                  pl.BlockSpec((1, tn), lambda j: (0, j))],
        out_specs=pl.BlockSpec((SUBLANES, tn), lambda j: (0, j)),
        out_shape=jax.ShapeDtypeStruct((SUBLANES, n), F32),
        compiler_params=_params(("arbitrary",)),
    )(c8, w_ada, b_ada.reshape(1, n))
    return out[0:1]


def _inproj_kernel(x_ref, g_ref, sc_ref, sh_ref, w_ref, wst_ref, proj_ref, smallt_ref, h_scr):
    @pl.when(pl.program_id(1) == 0)
    def _():
        h = _rms(x_ref[...], NORM_EPS) * g_ref[...] * (1.0 + sc_ref[...]) + sh_ref[...]
        hb = h.astype(BF16)
        h_scr[...] = hb
        smallt_ref[...] = _dg(wst_ref[...], hb, NT)

    proj_ref[...] = _dot(h_scr[...], w_ref[...]).astype(BF16)


def _inproj(x, g, sc, sh, w_main, w_small_t, tm, tn):
    s, d = x.shape
    n = w_main.shape[1]
    ns = w_small_t.shape[0]
    row = lambda i, j: (0, 0)
    return pl.pallas_call(
        _inproj_kernel,
        grid=(s // tm, n // tn),
        in_specs=[pl.BlockSpec((tm, d), lambda i, j: (i, 0)),
                  pl.BlockSpec((1, d), row), pl.BlockSpec((1, d), row), pl.BlockSpec((1, d), row),
                  pl.BlockSpec((d, tn), lambda i, j: (0, j)),
                  pl.BlockSpec((ns, d), row)],
        out_specs=[pl.BlockSpec((tm, tn), lambda i, j: (i, j)),
                   pl.BlockSpec((ns, tm), lambda i, j: (0, i))],
        out_shape=[jax.ShapeDtypeStruct((s, n), BF16), jax.ShapeDtypeStruct((ns, s), F32)],
        scratch_shapes=[pltpu.VMEM((tm, d), BF16)],
        compiler_params=_params(("arbitrary", "arbitrary")),
    )(x, g, sc, sh, w_main, w_small_t)


def _hgrn_kernel(q_ref, f_ref, i_ref, g_ref, lb_ref, on_ref, o_ref, st_scr, *, n_chunks):
    @pl.when(pl.program_id(0) == 0)
    def _():
        st_scr[...] = jnp.zeros_like(st_scr)

    c = CHUNK
    hd = HEAD_DIM
    causal = _iota2((c, c), 1) <= _iota2((c, c), 0)
    tri = causal.astype(BF16)
    lb = lb_ref[...]
    on_g = on_ref[...]
    heads = [slice(h * hd, (h + 1) * hd) for h in range(HEADS)]

    def chunk(n, carry):
        rows = pl.ds(pl.multiple_of(n * c, c), c)
        f = lb + (1.0 - lb) * _sigmoid(f_ref[rows, :].astype(F32))
        b = _dot_exact_lhs(tri, jnp.log(f))
        k = 1.0 - f
        q = _silu(q_ref[rows, :].astype(F32)) * (hd ** -0.5)
        v = i_ref[rows, :]
        b_mid = b[c // 2:c // 2 + 1, :]
        b_last = b[c - 1:c, :]
        qa = (q * jnp.exp(b - b_mid)).astype(BF16)
        ka = (k * jnp.exp(b_mid - b)).astype(BF16)
        qi = (q * jnp.exp(b)).astype(BF16)
        ku = (k * jnp.exp(b_last - b)).astype(BF16)
        dec = jnp.exp(b_last)
        gate = on_g * _silu(g_ref[rows, :].astype(F32))
        sts = [st_scr[h] for h in range(HEADS)]
        scores = [jnp.where(causal, _dg(qa[:, sl], ka[:, sl], NT), 0.0).astype(BF16) for sl in heads]
        inter = [_dg(qi[:, sl], st.astype(BF16), NT) for sl, st in zip(heads, sts)]
        kv = [_dg(v[:, sl], ku[:, sl], TN) for sl in heads]
        for h, sl in enumerate(heads):
            st_scr[h] = dec[:, sl] * sts[h] + kv[h]
        outs = [_rms(_dot(sc, v[:, sl]) + it, NORM_EPS) for sc, sl, it in zip(scores, heads, inter)]
        o_ref[rows, :] = (jnp.concatenate(outs, axis=1) * gate).astype(BF16)
        return carry

    lax.fori_loop(0, n_chunks, chunk, 0)


def _hgrn(proj, lb, onorm_g, ts):
    s = proj.shape[0]
    width = HEADS * HEAD_DIM
    col = lambda blk: pl.BlockSpec((ts, width), lambda j, blk=blk: (j, blk))
    const = pl.BlockSpec((1, width), lambda j: (0, 0))
    return pl.pallas_call(
        functools.partial(_hgrn_kernel, n_chunks=ts // CHUNK),
        grid=(s // ts,),
        in_specs=[col(0), col(1), col(2), col(3), const, const],
        out_specs=pl.BlockSpec((ts, width), lambda j: (j, 0)),
        out_shape=jax.ShapeDtypeStruct((s, width), BF16),
        scratch_shapes=[pltpu.VMEM((HEADS, HEAD_DIM, HEAD_DIM), F32)],
        compiler_params=_params(("arbitrary",)),
    )(proj, proj, proj, proj, lb, jnp.tile(onorm_g, (1, HEADS)))


def _gdn_prep_kernel(q_ref, k_ref, v_ref, qp_ref, kp_ref, vp_ref, wq_ref, wk_ref, wv_ref, ab_ref, alog_ref,
                     dtb_ref, tri_ref, eye_ref, u_ref, wqd_ref, ku_ref, attn_ref, dl_ref, cat_scr, rows_scr, cols_scr,
                     *, n_chunks, ts):
    h = pl.program_id(1)
    first = pl.program_id(0) == 0
    c = CHUNK
    hd = HEAD_DIM

    def conv_silu(cur_ref, prev_ref, w_ref):
        cat_scr[0:8, :] = jnp.where(first, 0.0, prev_ref[...].astype(F32))
        cat_scr[8:8 + ts, :] = cur_ref[...].astype(F32)
        acc = None
        for j in range(CONV_WIDTH):
            off = 8 - (CONV_WIDTH - 1) + j
            term = cat_scr[off:off + ts, :] * w_ref[j:j + 1, :]
            acc = term if acc is None else acc + term
        return _silu(acc)

    def l2n(x):
        return x * lax.rsqrt(jnp.sum(x * x, axis=-1, keepdims=True) + L2_EPS)

    q_all = l2n(conv_silu(q_ref, qp_ref, wq_ref)) * (hd ** -0.5)
    k_all = l2n(conv_silu(k_ref, kp_ref, wk_ref))
    v_all = conv_silu(v_ref, vp_ref, wv_ref)

    @pl.when(h == 0)
    def _():
        z = ab_ref[0:HEADS, :] + dtb_ref[...]
        softplus = jnp.maximum(z, 0.0) + jnp.log(1.0 + jnp.exp(-jnp.abs(z)))
        ld_rows = -jnp.exp(alog_ref[...]) * softplus
        hi, lo = _split(ld_rows)
        tri_blocks = tri_ref[...]
        g_rows = _dg(hi, tri_blocks, NT) + _dg(lo, tri_blocks, NT)
        beta_rows = _sigmoid(ab_ref[HEADS:2 * HEADS, :])
        rows_scr[...] = g_rows
        rows = jnp.concatenate([g_rows, beta_rows, jnp.zeros((LANES - 2 * HEADS, ts), F32)], axis=0)
        r_hi, r_lo = _split(rows)
        r_lo2 = (rows - r_hi.astype(F32) - r_lo.astype(F32)).astype(BF16)
        eye_ts = eye_ref[...]
        cols_scr[...] = _dg(eye_ts, r_hi, NT) + _dg(eye_ts, r_lo, NT) + _dg(eye_ts, r_lo2, NT)

    lane = _iota2((ts, LANES), 1)
    cols = cols_scr[...]
    gc_all = jnp.sum(jnp.where(lane == h, cols, 0.0), axis=1, keepdims=True)
    bc_all = jnp.sum(jnp.where(lane == h + HEADS, cols, 0.0), axis=1, keepdims=True)
    g_row = rows_scr[pl.ds(h, 1), :]
    egc_all = jnp.exp(gc_all)

    r = _iota2((c, c), 0)
    cidx = _iota2((c, c), 1)
    causal = cidx <= r
    strict = cidx < r
    eye_f = (r == cidx).astype(F32)
    chunks = [slice(n * c, (n + 1) * c) for n in range(n_chunks)]

    q16 = q_all.astype(BF16)
    k16 = k_all.astype(BF16)
    kq = [_dg(jnp.concatenate([k16[sl], q16[sl]], axis=0), k16[sl], NT) for sl in chunks]
    dm = []
    for sl in chunks:
        diff = gc_all[sl] - g_row[:, sl]
        dm.append(jnp.where(causal, jnp.exp(jnp.where(causal, diff, 0.0)), 0.0))
    bm = [-jnp.where(strict, bc_all[sl] * x[0:c] * d, 0.0) for sl, x, d in zip(chunks, kq, dm)]
    p = [eye_f + b for b in bm]
    bm = [_dot(b.astype(BF16), b.astype(BF16)) for b in bm]
    for _ in range(c.bit_length() - 3):
        res = [_dot(b.astype(BF16), jnp.concatenate([b, pp], axis=1).astype(BF16)) for b, pp in zip(bm, p)]
        p = [pp + x[:, c:2 * c] for pp, x in zip(p, res)]
        bm = [x[:, 0:c] for x in res]
    p = [pp + _dot(b.astype(BF16), pp.astype(BF16)) for b, pp in zip(bm, p)]
    rhs = jnp.concatenate([v_all * bc_all, k_all * (bc_all * egc_all)], axis=1).astype(BF16)
    sol = [_dot(pp.astype(BF16), rhs[sl]) for pp, sl in zip(p, chunks)]
    qd_all = (q_all * egc_all).astype(BF16)
    for n, sl in enumerate(chunks):
        g_last = gc_all[(n + 1) * c - 1:(n + 1) * c, :]
        u_ref[sl, :] = sol[n][:, 0:hd].astype(BF16)
        wqd_ref[2 * n * c:(2 * n + 1) * c, :] = sol[n][:, hd:2 * hd].astype(BF16)
        wqd_ref[(2 * n + 1) * c:(2 * n + 2) * c, :] = qd_all[sl]
        ku_ref[sl, :] = (k_all[sl] * jnp.exp(g_last - gc_all[sl])).astype(BF16)
        attn_ref[sl, :] = (kq[n][c:2 * c] * dm[n]).astype(BF16)
        dl_ref[n:n + 1, :] = jnp.broadcast_to(jnp.exp(g_last), (1, hd))


def _gdn_prep(proj, conv_w, ab_t, a_log, dt_bias, ts):
    s = proj.shape[0]
    hd = HEAD_DIM
    c = CHUNK
    q0 = 4 * HEADS
    cur = lambda off: pl.BlockSpec((ts, hd), lambda j, h, off=off: (j, off + h))
    prev = lambda off: pl.BlockSpec((8, hd), lambda j, h, off=off: (jnp.maximum(j * (ts // 8) - 1, 0), off + h))
    cw = lambda off: pl.BlockSpec((CONV_WIDTH, hd), lambda j, h, off=off: (0, off + h))
    per_head_scalar = pl.BlockSpec((HEADS, 1), lambda j, h: (0, 0))
    const = pl.BlockSpec((ts, ts), lambda j, h: (0, 0))
    pos = jnp.arange(ts)
    tri_blocks = ((pos[:, None] // c == pos[None, :] // c) & (pos[None, :] <= pos[:, None])).astype(BF16)
    eye = (pos[:, None] == pos[None, :]).astype(BF16)
    per_head = lambda rows, cols: pl.BlockSpec((None, rows, cols), lambda j, h: (h, j, 0))
    return pl.pallas_call(
        functools.partial(_gdn_prep_kernel, n_chunks=ts // c, ts=ts),
        grid=(s // ts, HEADS),
        in_specs=[cur(q0), cur(q0 + HEADS), cur(q0 + 2 * HEADS),
                  prev(q0), prev(q0 + HEADS), prev(q0 + 2 * HEADS),
                  cw(0), cw(HEADS), cw(2 * HEADS),
                  pl.BlockSpec((2 * HEADS, ts), lambda j, h: (0, j)),
                  per_head_scalar, per_head_scalar, const, const],
        out_specs=[pl.BlockSpec((ts, hd), lambda j, h: (j, h)),
                   pl.BlockSpec((2 * ts, hd), lambda j, h: (j, h)),
                   pl.BlockSpec((ts, hd), lambda j, h: (j, h)),
                   per_head(ts, c),
                   per_head(ts // c, hd)],
        out_shape=[jax.ShapeDtypeStruct((s, HEADS * hd), BF16),
                   jax.ShapeDtypeStruct((2 * s, HEADS * hd), BF16),
                   jax.ShapeDtypeStruct((s, HEADS * hd), BF16),
                   jax.ShapeDtypeStruct((HEADS, s, c), BF16),
                   jax.ShapeDtypeStruct((HEADS, s // c, hd), F32)],
        scratch_shapes=[pltpu.VMEM((ts + 8, hd), F32), pltpu.VMEM((HEADS, ts), F32), pltpu.VMEM((ts, LANES), F32)],
        compiler_params=_params(("arbitrary", "arbitrary")),
    )(proj, proj, proj, proj, proj, proj, conv_w, conv_w, conv_w,
      ab_t, a_log.reshape(HEADS, 1), dt_bias.reshape(HEADS, 1), tri_blocks, eye)


def _gdn_scan_kernel(u_ref, wqd_ref, ku_ref, attn_ref, dl_ref, g_ref, on_ref, o_ref, st_scr, *, n_chunks):
    @pl.when(pl.program_id(0) == 0)
    def _():
        st_scr[...] = jnp.zeros_like(st_scr)

    c = CHUNK
    hd = HEAD_DIM
    on_g = on_ref[...]
    heads = [slice(h * hd, (h + 1) * hd) for h in range(HEADS)]

    def chunk(n, carry):
        rows = pl.ds(pl.multiple_of(n * c, c), c)
        rows2 = pl.ds(pl.multiple_of(2 * n * c, 2 * c), 2 * c)
        sts = [st_scr[h] for h in range(HEADS)]
        wq = [_dot(wqd_ref[rows2, sl], st.astype(BF16)) for sl, st in zip(heads, sts)]
        vn = [(u_ref[rows, sl].astype(F32) - x[0:c]).astype(BF16) for sl, x in zip(heads, wq)]
        upd = [_dg(ku_ref[rows, sl], v, TN) for sl, v in zip(heads, vn)]
        for h in range(HEADS):
            st_scr[h] = dl_ref[h, pl.ds(n, 1), :] * sts[h] + upd[h]
        outs = [_rms(x[c:2 * c] + _dot(attn_ref[h, rows, :], v), NORM_EPS)
                for h, (x, v) in enumerate(zip(wq, vn))]
        gate = jnp.tile(on_g, (1, HEADS)) * _silu(g_ref[rows, :].astype(F32))
        o_ref[rows, :] = (jnp.concatenate(outs, axis=1) * gate).astype(BF16)
        return carry

    lax.fori_loop(0, n_chunks, chunk, 0)


def _gdn_scan(u, wqd, ku, attn, dl, proj, onorm_g, ts):
    s, width = u.shape
    c = CHUNK
    gate_blk = (4 * HEADS + 3 * HEADS) * HEAD_DIM // width
    return pl.pallas_call(
        functools.partial(_gdn_scan_kernel, n_chunks=ts // c),
        grid=(s // ts,),
        in_specs=[pl.BlockSpec((ts, width), lambda j: (j, 0)),
                  pl.BlockSpec((2 * ts, width), lambda j: (j, 0)),
                  pl.BlockSpec((ts, width), lambda j: (j, 0)),
                  pl.BlockSpec((HEADS, ts, c), lambda j: (0, j, 0)),
                  pl.BlockSpec((HEADS, ts // c, HEAD_DIM), lambda j: (0, j, 0)),
                  pl.BlockSpec((ts, width), lambda j: (j, gate_blk)),
                  pl.BlockSpec((1, HEAD_DIM), lambda j: (0, 0))],
        out_specs=pl.BlockSpec((ts, width), lambda j: (j, 0)),
        out_shape=jax.ShapeDtypeStruct((s, width), BF16),
        scratch_shapes=[pltpu.VMEM((HEADS, HEAD_DIM, HEAD_DIM), F32)],
        compiler_params=_params(("arbitrary",)),
    )(u, wqd, ku, attn, dl, proj, onorm_g)


def _merge_kernel(oa_ref, ob_ref, mga_ref, mgb_ref, x_ref, wa_ref, wb_ref, wo_ref, gt_ref, g2_ref, sc_ref,
                  sh_ref, x1_ref, h2_ref):
    ya = _dot(oa_ref[...], wa_ref[...])
    yb = _dot(ob_ref[...], wb_ref[...])
    merged = _sigmoid(mga_ref[...].astype(F32)) * ya + _sigmoid(mgb_ref[...].astype(F32)) * yb
    x1 = x_ref[...] + gt_ref[...] * _dot(merged.astype(BF16), wo_ref[...])
    x1_ref[...] = x1
    h2 = _rms(x1, NORM_EPS) * g2_ref[...] * (1.0 + sc_ref[...]) + sh_ref[...]
    h2_ref[...] = h2.astype(BF16)


def _merge(o_a, o_b, proj, x, w_a, w_b, w_o, gt1, g2, sc2, sh2, tm):
    s, d = x.shape
    dv = o_a.shape[1]
    mg0 = (8 * HEADS * HEAD_DIM) // d
    const = lambda shape: pl.BlockSpec(shape, lambda i: (0, 0), pipeline_mode=pl.Buffered(1))
    return pl.pallas_call(
        _merge_kernel,
        grid=(s // tm,),
        in_specs=[pl.BlockSpec((tm, dv), lambda i: (i, 0)),
                  pl.BlockSpec((tm, dv), lambda i: (i, 0)),
                  pl.BlockSpec((tm, d), lambda i: (i, mg0)),
                  pl.BlockSpec((tm, d), lambda i: (i, mg0 + 1)),
                  pl.BlockSpec((tm, d), lambda i: (i, 0)),
                  const((dv, d)), const((dv, d)), const((d, d)),
                  const((1, d)), const((1, d)), const((1, d)), const((1, d))],
        out_specs=[pl.BlockSpec((tm, d), lambda i: (i, 0)), pl.BlockSpec((tm, d), lambda i: (i, 0))],
        out_shape=[jax.ShapeDtypeStruct((s, d), F32), jax.ShapeDtypeStruct((s, d), BF16)],
        compiler_params=_params(("arbitrary",)),
    )(o_a, o_b, proj, proj, x, w_a, w_b, w_o, gt1, g2, sc2, sh2)


def _first_max(vals, iota, size, axis):
    m = jnp.max(vals, axis=axis, keepdims=True)
    idx = jnp.min(jnp.where(vals == m, iota, size), axis=axis, keepdims=True)
    return m, idx


def _router_kernel(x1_ref, g2_ref, sc_ref, sh_ref, wrt_ref, bias_ref, upper_ref, pos_ref, wcol_ref, before_ref,
                   ntile_ref, cnt_scr, *, tm):
    @pl.when(pl.program_id(0) == 0)
    def _():
        cnt_scr[...] = jnp.zeros_like(cnt_scr)

    e = N_EXPERTS
    h2 = _rms(x1_ref[...], NORM_EPS) * g2_ref[...] * (1.0 + sc_ref[...]) + sh_ref[...]
    logits = lax.dot_general(wrt_ref[...], h2, NT, preferred_element_type=F32,
                             precision=lax.Precision.HIGHEST)
    scores = _sigmoid(logits)
    biased = scores + bias_ref[...]
    neg = -jnp.inf

    g3 = biased.reshape(N_GROUPS, GROUP_SIZE, tm)
    i3 = lax.broadcasted_iota(I32, g3.shape, 1)
    m1, a1 = _first_max(g3, i3, GROUP_SIZE, 1)
    m2 = jnp.max(jnp.where(i3 == a1, neg, g3), axis=1, keepdims=True)
    gs = (m1 + m2).reshape(N_GROUPS, tm)
    ig = _iota2(gs.shape, 0)
    gmask = jnp.zeros(gs.shape, jnp.bool_)
    for _ in range(TOPK_GROUPS):
        _, a = _first_max(gs, ig, N_GROUPS, 0)
        pick = ig == a
        gmask = jnp.logical_or(gmask, pick)
        gs = jnp.where(pick, neg, gs)
    emask = jnp.broadcast_to(gmask.reshape(N_GROUPS, 1, tm), (N_GROUPS, GROUP_SIZE, tm)).reshape(e, tm)

    cand = jnp.where(emask, biased, neg)
    ie = _iota2((e, tm), 0)
    sel_all = jnp.zeros((e, tm), jnp.bool_)
    w_rows, picks = [], []
    for _ in range(TOP_K):
        _, a = _first_max(cand, ie, e, 0)
        pick = ie == a
        picks.append(pick)
        w_rows.append(jnp.sum(jnp.where(pick, scores, 0.0), axis=0, keepdims=True))
        sel_all = jnp.logical_or(sel_all, pick)
        cand = jnp.where(pick, neg, cand)
    w_sum = w_rows[0]
    for wr in w_rows[1:]:
        w_sum = w_sum + wr
    wts = jnp.concatenate(w_rows, axis=0) / w_sum * ROUTED_SCALE

    sel = sel_all.astype(BF16)
    in_expert = _dot(sel, upper_ref[...])
    n_tile = jnp.sum(sel_all.astype(F32), axis=1, keepdims=True)
    lower = (_iota2((e, e), 1) < _iota2((e, e), 0)).astype(BF16)
    expert_off = _dot_exact_lhs(lower, jnp.broadcast_to(n_tile, (e, LANES)))[:, 0:1]
    place = in_expert + expert_off
    pos = jnp.concatenate([jnp.sum(jnp.where(pk, place, 0.0), axis=0, keepdims=True) for pk in picks], axis=0)
    pos_ref[...] = pos.astype(I32)
    before_ref[...] = jnp.broadcast_to(cnt_scr[...], before_ref.shape).astype(I32)
    ntile_ref[...] = jnp.broadcast_to(n_tile, ntile_ref.shape).astype(I32)
    cnt_scr[...] = cnt_scr[...] + n_tile

    eye = (_iota2((tm, tm), 0) == _iota2((tm, tm), 1)).astype(BF16)
    rows = jnp.concatenate([wts, pos, jnp.zeros((LANES - 2 * TOP_K, tm), F32)], axis=0)
    hi, lo = _split(rows)
    lo2 = (rows - hi.astype(F32) - lo.astype(F32)).astype(BF16)
    wcol_ref[...] = _dg(eye, hi, NT) + _dg(eye, lo, NT) + _dg(eye, lo2, NT)


def _router(x1, g2, sc2, sh2, w_router_t, bias_col, tm):
    s, d = x1.shape
    e = N_EXPERTS
    nt = s // tm
    upper = (jnp.arange(tm)[:, None] < jnp.arange(tm)[None, :]).astype(BF16)
    const = lambda shape: pl.BlockSpec(shape, lambda i: (0, 0))
    per_tile = pl.BlockSpec((None, e, LANES), lambda i: (i, 0, 0))
    return pl.pallas_call(
        functools.partial(_router_kernel, tm=tm),
        grid=(nt,),
        in_specs=[pl.BlockSpec((tm, d), lambda i: (i, 0)),
                  const((1, d)), const((1, d)), const((1, d)),
                  const((e, d)), const((e, 1)), const((tm, tm))],
        out_specs=[pl.BlockSpec((TOP_K, tm), lambda i: (0, i)),
                   pl.BlockSpec((tm, LANES), lambda i: (i, 0)),
                   per_tile, per_tile],
        out_shape=[jax.ShapeDtypeStruct((TOP_K, s), I32), jax.ShapeDtypeStruct((s, LANES), F32),
                   jax.ShapeDtypeStruct((nt, e, LANES), I32), jax.ShapeDtypeStruct((nt, e, LANES), I32)],
        scratch_shapes=[pltpu.VMEM((e, 1), F32)],
        compiler_params=_params(("arbitrary",)),
    )(x1, g2, sc2, sh2, w_router_t, bias_col, upper)


def _run_sizes(limit):
    return [1 << b for b in range(limit.bit_length() - 1, -1, -1)]


def _for_each_run(step, run_len_ref, run_dst_ref, tm, make_copy, fn):
    def per_expert(ex, tile_slot):
        n = run_len_ref[step * N_EXPERTS + ex]
        dst = run_dst_ref[step * N_EXPERTS + ex]
        for size in _run_sizes(tm):
            take = (n & size) != 0

            @pl.when(take)
            def _(tile_slot=tile_slot, dst=dst, size=size):
                fn(make_copy(tile_slot, dst, size))

            inc = jnp.where(take, size, 0)
            tile_slot = tile_slot + inc
            dst = dst + inc
        return tile_slot

    lax.fori_loop(0, N_EXPERTS, per_expert, jnp.int32(0))


def _slot_rows(slot, n_slots):
    return pl.ds(pl.multiple_of(slot * SUBLANES, SUBLANES), n_slots * SUBLANES)


def _dispatch_kernel(run_len_ref, run_dst_ref, pad_lo_ref, pad_hi_ref, pos_ref, h_ref, xs_ref, stage, zero_scr,
                     sem, pad_sem, *, tm, rows_per_pass):
    step = pl.program_id(0)
    na = TOP_K * tm
    d = h_ref.shape[1]
    half = d // 2
    n_words = half // LANES

    def pad_copy(slot, n_slots):
        return pltpu.make_async_copy(zero_scr.at[pl.ds(0, n_slots * SUBLANES), :],
                                     xs_ref.at[_slot_rows(slot, n_slots), :], pad_sem)

    def for_each_pad(fn):
        def per_expert(ex, carry):
            slot = pad_lo_ref[ex]
            n = pad_hi_ref[ex] - slot
            for size in _run_sizes(EXPERT_BLOCK - 1):
                take = (n & size) != 0

                @pl.when(take)
                def _(slot=slot, size=size):
                    fn(pad_copy(slot, size))

                slot = slot + jnp.where(take, size, 0)
            return carry
        lax.fori_loop(0, N_EXPERTS, per_expert, 0)

    @pl.when(step == 0)
    def _():
        zero_scr[...] = jnp.zeros_like(zero_scr)
        for_each_pad(lambda cp: cp.start())

    buf = step % 2
    pos = pos_ref[...]
    h = h_ref[...]
    for a0 in range(0, na, rows_per_pass):
        slot_id = a0 + _iota2((rows_per_pass, tm), 0)
        hit = pos[0:1, :] == slot_id
        for k in range(1, TOP_K):
            hit = jnp.logical_or(hit, pos[k:k + 1, :] == slot_id)
        rows = _dot(hit.astype(BF16), h)
        for i in range(n_words):
            word = _pack_halves(rows[:, i * LANES:(i + 1) * LANES], rows[:, half + i * LANES:half + (i + 1) * LANES])
            stage[buf, pl.ds(a0 * SUBLANES + i, rows_per_pass, stride=SUBLANES), :] = word

    def run_copy(tile_slot, sorted_slot, n_slots):
        return pltpu.make_async_copy(stage.at[buf, _slot_rows(tile_slot, n_slots), :],
                                     xs_ref.at[_slot_rows(sorted_slot, n_slots), :], sem.at[buf])

    def wait_tile(which):
        pltpu.make_async_copy(stage.at[which], xs_ref.at[pl.ds(0, na * SUBLANES), :], sem.at[which]).wait()

    _for_each_run(step, run_len_ref, run_dst_ref, tm, run_copy, lambda cp: cp.start())

    @pl.when(step > 0)
    def _():
        wait_tile(1 - buf)

    @pl.when(step == pl.num_programs(0) - 1)
    def _():
        wait_tile(buf)

    @pl.when(step == 0)
    def _():
        for_each_pad(lambda cp: cp.wait())


def _dispatch(run_len, run_dst, pad_lo, pad_hi, pos_t, h2, n_slots, tm):
    s, d = h2.shape
    assert (d // 2) % LANES == 0 and (d // 2) // LANES == SUBLANES, "one token row must pack into one (8, 128) tile"
    na = TOP_K * tm
    return pl.pallas_call(
        functools.partial(_dispatch_kernel, tm=tm, rows_per_pass=min(512, na)),
        grid_spec=pltpu.PrefetchScalarGridSpec(
            num_scalar_prefetch=4,
            grid=(s // tm,),
            in_specs=[pl.BlockSpec((TOP_K, tm), lambda i, *_: (0, i)),
                      pl.BlockSpec((tm, d), lambda i, *_: (i, 0))],
            out_specs=pl.BlockSpec(memory_space=pl.ANY),
            scratch_shapes=[pltpu.VMEM((2, na * SUBLANES, LANES), U32),
                            pltpu.VMEM((EXPERT_BLOCK // 2 * SUBLANES, LANES), U32),
                            pltpu.SemaphoreType.DMA((2,)), pltpu.SemaphoreType.DMA(())]),
        out_shape=jax.ShapeDtypeStruct((n_slots * SUBLANES, LANES), U32),
        compiler_params=_params(("arbitrary",), has_side_effects=True, disable_bounds_checks=True),
    )(run_len, run_dst, pad_lo, pad_hi, pos_t, h2)


def _expert_kernel(be_ref, nu_ref, next_ref, par_ref, x_ref, wg_hbm, wu_hbm, wd_hbm, y_ref, wg_f32, wu_f32, wd_f32,
                   wg_scr, wu_scr, wd_scr, sem):
    b = pl.program_id(0)
    bm = EXPERT_BLOCK
    active = b < nu_ref[0]
    new_expert = jnp.logical_or(b == 0, be_ref[b] != be_ref[jnp.maximum(b - 1, 0)])

    def weight_copies(ex, which):
        return [pltpu.make_async_copy(src.at[ex], dst.at[which], sem.at[which])
                for src, dst in ((wg_hbm, wg_f32), (wu_hbm, wu_f32), (wd_hbm, wd_f32))]

    @pl.when(jnp.logical_and(active, new_expert))
    def _():
        which = par_ref[b]

        @pl.when(b == 0)
        def _():
            for cp in weight_copies(be_ref[b], which):
                cp.start()

        for cp in weight_copies(be_ref[b], which):
            cp.wait()
        wg_scr[...] = wg_f32[which].astype(BF16)
        wu_scr[...] = wu_f32[which].astype(BF16)
        wd_scr[...] = wd_f32[which].astype(BF16)
        nb = next_ref[b]

        @pl.when(nb < nu_ref[0])
        def _():
            for cp in weight_copies(be_ref[nb], 1 - which):
                cp.start()

    @pl.when(active)
    def _():
        los, his = [], []
        for i in range(SUBLANES):
            lo, hi = _unpack_halves(x_ref[pl.ds(i, bm, stride=SUBLANES), :])
            los.append(lo.astype(BF16))
            his.append(hi.astype(BF16))
        xb = jnp.concatenate(los + his, axis=1)
        hid = _silu(_dot(xb, wg_scr[...])) * _dot(xb, wu_scr[...])
        y = _dot(hid.astype(BF16), wd_scr[...])
        half = y.shape[1] // 2
        for i in range(SUBLANES):
            word = _pack_halves(_round_bf16(y[:, i * LANES:(i + 1) * LANES]),
                                _round_bf16(y[:, half + i * LANES:half + (i + 1) * LANES]))
            y_ref[pl.ds(i, bm, stride=SUBLANES), :] = word


def _experts(block_e, n_used, next_block, parity, xs, w_gate, w_up, w_down):
    d, ff = w_gate.shape[1], w_gate.shape[2]
    bm = EXPERT_BLOCK
    n_blocks = xs.shape[0] // (bm * SUBLANES)
    blk = lambda b, be, nu, *_: (jnp.minimum(b, nu[0] - 1), 0)
    hbm = pl.BlockSpec(memory_space=pl.ANY)
    return pl.pallas_call(
        _expert_kernel,
        grid_spec=pltpu.PrefetchScalarGridSpec(
            num_scalar_prefetch=4,
            grid=(n_blocks,),
            in_specs=[pl.BlockSpec((bm * SUBLANES, LANES), blk), hbm, hbm, hbm],
            out_specs=pl.BlockSpec((bm * SUBLANES, LANES), blk),
            scratch_shapes=[pltpu.VMEM((2, d, ff), F32), pltpu.VMEM((2, d, ff), F32), pltpu.VMEM((2, ff, d), F32),
                            pltpu.VMEM((d, ff), BF16), pltpu.VMEM((d, ff), BF16), pltpu.VMEM((ff, d), BF16),
                            pltpu.SemaphoreType.DMA((2,))]),
        out_shape=jax.ShapeDtypeStruct(xs.shape, U32),
        compiler_params=_params(("arbitrary",)),
    )(block_e, n_used, next_block, parity, xs, w_gate, w_up, w_down)


def _combine_kernel(run_len_ref, run_src_ref, ys_ref, h_ref, x1_ref, wcol_ref, wg_ref, wu_ref, wd_ref, gt_ref, gf_ref,
                    o_ref, stage, sem, *, tm, rows_per_pass):
    step = pl.program_id(0)
    na = TOP_K * tm
    buf = step % 2

    def fetch_tile(tile, which):
        def run_copy(tile_slot, sorted_slot, n_slots):
            return pltpu.make_async_copy(ys_ref.at[_slot_rows(sorted_slot, n_slots), :],
                                         stage.at[which, _slot_rows(tile_slot, n_slots), :], sem.at[which])
        _for_each_run(tile, run_len_ref, run_src_ref, tm, run_copy, lambda cp: cp.start())

    @pl.when(step == 0)
    def _():
        fetch_tile(step, buf)

    @pl.when(step + 1 < pl.num_programs(0))
    def _():
        fetch_tile(step + 1, 1 - buf)

    hb = h_ref[...]
    hid = _silu(_dot(hb, wg_ref[...])) * _dot(hb, wu_ref[...])
    acc = _dot(hid.astype(BF16), wd_ref[...])

    pltpu.make_async_copy(ys_ref.at[pl.ds(0, na * SUBLANES), :], stage.at[buf], sem.at[buf]).wait()

    wcol = wcol_ref[...]
    at = [wcol[:, TOP_K + k:TOP_K + k + 1].astype(I32) for k in range(TOP_K)]
    for a0 in range(0, na, rows_per_pass):
        los, his = [], []
        for i in range(SUBLANES):
            lo, hi = _unpack_halves(stage[buf, pl.ds(a0 * SUBLANES + i, rows_per_pass, stride=SUBLANES), :])
            los.append(lo.astype(BF16))
            his.append(hi.astype(BF16))
        y_rows = jnp.concatenate(los + his, axis=1)
        slot_id = a0 + _iota2((tm, rows_per_pass), 1)
        wmat = jnp.zeros((tm, rows_per_pass), F32)
        for k in range(TOP_K):
            wmat = wmat + jnp.where(at[k] == slot_id, wcol[:, k:k + 1], 0.0)
        acc = acc + _dot(wmat.astype(BF16), y_rows)
    x2 = x1_ref[...] + gt_ref[...] * acc
    o_ref[...] = _rms(x2, NORM_EPS) * gf_ref[...]


def _combine(run_len, run_src, ys, h2, x1, wcol, w_gate, w_up, w_down, gt2, gf, tm):
    s, d = x1.shape
    ff = w_gate.shape[1]
    na = TOP_K * tm
    const = lambda shape: pl.BlockSpec(shape, lambda i, *_: (0, 0), pipeline_mode=pl.Buffered(1))
    tile = lambda cols: pl.BlockSpec((tm, cols), lambda i, *_: (i, 0))
    return pl.pallas_call(
        functools.partial(_combine_kernel, tm=tm, rows_per_pass=min(512, na)),
        grid_spec=pltpu.PrefetchScalarGridSpec(
            num_scalar_prefetch=2,
            grid=(s // tm,),
            in_specs=[pl.BlockSpec(memory_space=pl.ANY),
                      tile(d), tile(d), tile(LANES),
                      const((d, ff)), const((d, ff)), const((ff, d)), const((1, d)), const((1, d))],
            out_specs=tile(d),
            scratch_shapes=[pltpu.VMEM((2, na * SUBLANES, LANES), U32), pltpu.SemaphoreType.DMA((2,))]),
        out_shape=jax.ShapeDtypeStruct((s, d), F32),
        compiler_params=_params(("arbitrary",), disable_bounds_checks=True),
    )(run_len, run_src, ys, h2, x1, wcol, w_gate, w_up, w_down, gt2, gf)


def _mixer(x2d, mod, norm1_g, norm2_g, w_in, lb, hgrn_onorm_g, gdn_conv_w, gdn_a_log, gdn_dt_bias, gdn_onorm_g,
           w_branch_hgrn, w_branch_gdn, w_out, tiles):
    d = x2d.shape[1]
    sh1, sc1, gt1, sh2, sc2, _ = [mod[:, i * d:(i + 1) * d] for i in range(6)]
    key = HEADS * HEAD_DIM
    small0 = 4 * key + 3 * key
    small1 = small0 + 2 * HEADS
    w_main = jnp.concatenate([w_in[:, :small0], w_in[:, small1:]], axis=1).astype(BF16)
    w_small_t = w_in[:, small0:small1].T.astype(BF16)
    proj, ab_t = _inproj(x2d, norm1_g, sc1, sh1, w_main, w_small_t, tiles["in_tm"], tiles["in_tn"])
    o_a = _hgrn(proj, lb, hgrn_onorm_g, tiles["mix_ts"])
    u, wqd, ku, attn, dl = _gdn_prep(proj, gdn_conv_w, ab_t, gdn_a_log, gdn_dt_bias, tiles["mix_ts"])
    o_b = _gdn_scan(u, wqd, ku, attn, dl, proj, gdn_onorm_g, tiles["mix_ts"])
    return _merge(o_a, o_b, proj, x2d, w_branch_hgrn.astype(BF16), w_branch_gdn.astype(BF16),
                  w_out.astype(BF16), gt1, norm2_g, sc2, sh2, tiles["merge_tm"])


def _moe(x1, h2, mod, norm2_g, normf_g, w_router, router_bias, w_exp_gate, w_exp_up, w_exp_down, w_sh_gate,
         w_sh_up, w_sh_down, tiles):
    s, d = x1.shape
    tm = tiles["moe_tm"]
    sh2, sc2, gt2 = [mod[:, i * d:(i + 1) * d] for i in (3, 4, 5)]
    pos_t, wcol, before, ntile = _router(x1, norm2_g, sc2, sh2, w_router.T, router_bias.reshape(-1, 1), tm)
    bm = EXPERT_BLOCK
    n_blocks = -(-(s * TOP_K + N_EXPERTS * (bm - 1)) // bm)
    before = before[:, :, 0]
    ntile = ntile[:, :, 0]
    counts = before[-1] + ntile[-1]
    padded = (counts + bm - 1) // bm * bm
    pend = jnp.cumsum(padded).astype(I32)
    pstart = pend - padded
    block_start = jnp.arange(n_blocks, dtype=I32) * bm
    block_e = jnp.minimum(jnp.sum(pend[None, :] <= block_start[:, None], axis=1), N_EXPERTS - 1).astype(I32)
    n_used = pend[-1:] // bm
    run_len = ntile.reshape(-1)
    run_dst = (before + pstart[None, :]).reshape(-1)
    xs = _dispatch(run_len, run_dst, pstart + counts, pend, pos_t, h2, n_blocks * bm, tm)
    next_block = pend[block_e] // bm
    switches = jnp.concatenate([jnp.zeros((1,), I32), (block_e[1:] != block_e[:-1]).astype(I32)])
    parity = jnp.cumsum(switches).astype(I32) % 2
    ys = _experts(block_e, n_used, next_block, parity, xs, w_exp_gate, w_exp_up, w_exp_down)
    return _combine(run_len, run_dst, ys, h2, x1, wcol, w_sh_gate.astype(BF16), w_sh_up.astype(BF16),
                    w_sh_down.astype(BF16), gt2, normf_g, tm)


def _tiles(s):
    pick = lambda want: min(want, s)
    return dict(in_tm=pick(1024), in_tn=1536, mix_ts=pick(512), merge_tm=pick(512), moe_tm=pick(256))


def kernel(x, c, w_ada, b_ada, norm1_g, norm2_g, w_in, hgrn_lb_table, hgrn_onorm_g, gdn_conv_w, gdn_a_log, gdn_dt_bias, gdn_onorm_g, w_branch_hgrn, w_branch_gdn, w_out, w_router, router_bias, w_exp_gate, w_exp_up, w_exp_down, w_sh_gate, w_sh_up, w_sh_down, normf_g):
    b, s, d = x.shape
    assert b == 1 and w_ada.shape[0] == 1, "one sequence, one layer"
    tiles = _tiles(s)
    lb = jnp.cumsum(jax.nn.softmax(hgrn_lb_table.astype(F32), axis=0), axis=0)[0:1]
    mod = _ada(c, w_ada[0], b_ada[0])
    row = lambda v: v.reshape(1, -1)
    x1, h2 = _mixer(x[0], mod, row(norm1_g[0]), row(norm2_g[0]), w_in[0], lb, row(hgrn_onorm_g[0]), gdn_conv_w[0],
                    gdn_a_log[0], gdn_dt_bias[0], row(gdn_onorm_g[0]), w_branch_hgrn[0], w_branch_gdn[0], w_out[0],
                    tiles)
    out = _moe(x1, h2, mod, row(norm2_g[0]), row(normf_g), w_router[0], router_bias[0], w_exp_gate[0],
               w_exp_up[0], w_exp_down[0], w_sh_gate[0], w_sh_up[0], w_sh_down[0], tiles)
    return out[None]
```

```python
import functools

import jax
import jax.numpy as jnp
from jax import lax
from jax.experimental import pallas as pl
from jax.experimental.pallas import tpu as pltpu

F32 = jnp.float32
BF16 = jnp.bfloat16
I32 = jnp.int32
U32 = jnp.uint32

NORM_EPS = 1e-6
L2_EPS = 1e-6
HEADS = 8
HEAD_DIM = 128
CONV_WIDTH = 4
CHUNK = 64
N_EXPERTS = 64
N_GROUPS = 8
GROUP_SIZE = N_EXPERTS // N_GROUPS
TOPK_GROUPS = 4
TOP_K = 8
ROUTED_SCALE = 2.5
EXPERT_BLOCK = 512

LANES = 128
SUBLANES = 8
VMEM_LIMIT = 56 * 1024 * 1024

NT = (((1,), (1,)), ((), ()))
TN = (((0,), (0,)), ((), ()))


def _params(sem, **kw):
    return pltpu.CompilerParams(dimension_semantics=sem, vmem_limit_bytes=VMEM_LIMIT, **kw)


def _dot(a, b):
    return jnp.dot(a, b, preferred_element_type=F32)


def _dg(a, b, dims):
    return lax.dot_general(a, b, dims, preferred_element_type=F32)


def _split(x):
    hi = x.astype(BF16)
    lo = (x - hi.astype(F32)).astype(BF16)
    return hi, lo


def _dot_exact_lhs(a_bf16, x, dims=None):
    hi, lo = _split(x)
    if dims is None:
        return _dot(a_bf16, hi) + _dot(a_bf16, lo)
    return _dg(a_bf16, hi, dims) + _dg(a_bf16, lo, dims)


def _sigmoid(x):
    return 1.0 / (1.0 + jnp.exp(-x))


def _silu(x):
    return x * _sigmoid(x)


def _rms(x, eps):
    return x * lax.rsqrt(jnp.mean(x * x, axis=-1, keepdims=True) + eps)


def _iota2(shape, dim):
    return lax.broadcasted_iota(I32, shape, dim)


def _pack_halves(lo, hi):
    lo_bits = lax.shift_right_logical(pltpu.bitcast(lo, U32), U32(16))
    hi_bits = pltpu.bitcast(hi, U32) & U32(0xFFFF0000)
    return lo_bits | hi_bits


def _unpack_halves(word):
    lo = pltpu.bitcast(lax.shift_left(word, U32(16)), F32)
    hi = pltpu.bitcast(word & U32(0xFFFF0000), F32)
    return lo, hi


def _round_bf16(x):
    return x.astype(BF16).astype(F32)


def _ada_kernel(c_ref, w_ref, b_ref, o_ref):
    cond = _silu(c_ref[...])
    o_ref[...] = jnp.dot(cond, w_ref[...], preferred_element_type=F32,
                         precision=lax.Precision.HIGHEST) + b_ref[...]


def _ada(c, w_ada, b_ada):
    d, n = w_ada.shape
    tn = 1024
    c8 = jnp.broadcast_to(c, (SUBLANES, d))
    out = pl.pallas_call(
        _ada_kernel,
        grid=(n // tn,),
        in_specs=[pl.BlockSpec((SUBLANES, d), lambda j: (0, 0)),
                  pl.BlockSpec((d, tn), lambda j: (0, j)),
                  pl.BlockSpec((1, tn), lambda j: (0, j))],
        out_specs=pl.BlockSpec((SUBLANES, tn), lambda j: (0, j)),
        out_shape=jax.ShapeDtypeStruct((SUBLANES, n), F32),
        compiler_params=_params(("arbitrary",)),
    )(c8, w_ada, b_ada.reshape(1, n))
    return out[0:1]


def _inproj_kernel(x_ref, g_ref, sc_ref, sh_ref, w_ref, wst_ref, proj_ref, smallt_ref, h_scr):
    @pl.when(pl.program_id(1) == 0)
    def _():
        h = _rms(x_ref[...], NORM_EPS) * g_ref[...] * (1.0 + sc_ref[...]) + sh_ref[...]
        hb = h.astype(BF16)
        h_scr[...] = hb
        smallt_ref[...] = _dg(wst_ref[...], hb, NT)

    proj_ref[...] = _dot(h_scr[...], w_ref[...]).astype(BF16)


def _inproj(x, g, sc, sh, w_main, w_small_t, tm, tn):
    s, d = x.shape
    n = w_main.shape[1]
    ns = w_small_t.shape[0]
    row = lambda i, j: (0, 0)
    return pl.pallas_call(
        _inproj_kernel,
        grid=(s // tm, n // tn),
        in_specs=[pl.BlockSpec((tm, d), lambda i, j: (i, 0)),
                  pl.BlockSpec((1, d), row), pl.BlockSpec((1, d), row), pl.BlockSpec((1, d), row),
                  pl.BlockSpec((d, tn), lambda i, j: (0, j)),
                  pl.BlockSpec((ns, d), row)],
        out_specs=[pl.BlockSpec((tm, tn), lambda i, j: (i, j)),
                   pl.BlockSpec((ns, tm), lambda i, j: (0, i))],
        out_shape=[jax.ShapeDtypeStruct((s, n), BF16), jax.ShapeDtypeStruct((ns, s), F32)],
        scratch_shapes=[pltpu.VMEM((tm, d), BF16)],
        compiler_params=_params(("arbitrary", "arbitrary")),
    )(x, g, sc, sh, w_main, w_small_t)


def _hgrn_kernel(q_ref, f_ref, i_ref, g_ref, lb_ref, on_ref, o_ref, st_scr, *, n_chunks):
    @pl.when(pl.program_id(0) == 0)
    def _():
        st_scr[...] = jnp.zeros_like(st_scr)

    c = CHUNK
    hd = HEAD_DIM
    causal = _iota2((c, c), 1) <= _iota2((c, c), 0)
    tri = causal.astype(BF16)
    lb = lb_ref[...]
    on_g = on_ref[...]
    heads = [slice(h * hd, (h + 1) * hd) for h in range(HEADS)]

    def chunk(n, carry):
        rows = pl.ds(pl.multiple_of(n * c, c), c)
        f = lb + (1.0 - lb) * _sigmoid(f_ref[rows, :].astype(F32))
        b = _dot_exact_lhs(tri, jnp.log(f))
        k = 1.0 - f
        q = _silu(q_ref[rows, :].astype(F32)) * (hd ** -0.5)
        v = i_ref[rows, :]
        b_mid = b[c // 2:c // 2 + 1, :]
        b_last = b[c - 1:c, :]
        qa = (q * jnp.exp(b - b_mid)).astype(BF16)
        ka = (k * jnp.exp(b_mid - b)).astype(BF16)
        qi = (q * jnp.exp(b)).astype(BF16)
        ku = (k * jnp.exp(b_last - b)).astype(BF16)
        dec = jnp.exp(b_last)
        gate = on_g * _silu(g_ref[rows, :].astype(F32))
        sts = [st_scr[h] for h in range(HEADS)]
        scores = [jnp.where(causal, _dg(qa[:, sl], ka[:, sl], NT), 0.0).astype(BF16) for sl in heads]
        inter = [_dg(qi[:, sl], st.astype(BF16), NT) for sl, st in zip(heads, sts)]
        kv = [_dg(v[:, sl], ku[:, sl], TN) for sl in heads]
        for h, sl in enumerate(heads):
            st_scr[h] = dec[:, sl] * sts[h] + kv[h]
        outs = [_rms(_dot(sc, v[:, sl]) + it, NORM_EPS) for sc, sl, it in zip(scores, heads, inter)]
        o_ref[rows, :] = (jnp.concatenate(outs, axis=1) * gate).astype(BF16)
        return carry

    lax.fori_loop(0, n_chunks, chunk, 0)


def _hgrn(proj, lb, onorm_g, ts):
    s = proj.shape[0]
    width = HEADS * HEAD_DIM
    col = lambda blk: pl.BlockSpec((ts, width), lambda j, blk=blk: (j, blk))
    const = pl.BlockSpec((1, width), lambda j: (0, 0))
    return pl.pallas_call(
        functools.partial(_hgrn_kernel, n_chunks=ts // CHUNK),
        grid=(s // ts,),
        in_specs=[col(0), col(1), col(2), col(3), const, const],
        out_specs=pl.BlockSpec((ts, width), lambda j: (j, 0)),
        out_shape=jax.ShapeDtypeStruct((s, width), BF16),
        scratch_shapes=[pltpu.VMEM((HEADS, HEAD_DIM, HEAD_DIM), F32)],
        compiler_params=_params(("arbitrary",)),
    )(proj, proj, proj, proj, lb, jnp.tile(onorm_g, (1, HEADS)))


def _gdn_prep_kernel(q_ref, k_ref, v_ref, qp_ref, kp_ref, vp_ref, wq_ref, wk_ref, wv_ref, ab_ref, alog_ref,
                     dtb_ref, tri_ref, eye_ref, u_ref, wqd_ref, ku_ref, attn_ref, dl_ref, cat_scr, rows_scr, cols_scr,
                     *, n_chunks, ts):
    h = pl.program_id(1)
    first = pl.program_id(0) == 0
    c = CHUNK
    hd = HEAD_DIM

    def conv_silu(cur_ref, prev_ref, w_ref):
        cat_scr[0:8, :] = jnp.where(first, 0.0, prev_ref[...].astype(F32))
        cat_scr[8:8 + ts, :] = cur_ref[...].astype(F32)
        acc = None
        for j in range(CONV_WIDTH):
            off = 8 - (CONV_WIDTH - 1) + j
            term = cat_scr[off:off + ts, :] * w_ref[j:j + 1, :]
            acc = term if acc is None else acc + term
        return _silu(acc)

    def l2n(x):
        return x * lax.rsqrt(jnp.sum(x * x, axis=-1, keepdims=True) + L2_EPS)

    q_all = l2n(conv_silu(q_ref, qp_ref, wq_ref)) * (hd ** -0.5)
    k_all = l2n(conv_silu(k_ref, kp_ref, wk_ref))
    v_all = conv_silu(v_ref, vp_ref, wv_ref)

    @pl.when(h == 0)
    def _():
        z = ab_ref[0:HEADS, :] + dtb_ref[...]
        softplus = jnp.maximum(z, 0.0) + jnp.log(1.0 + jnp.exp(-jnp.abs(z)))
        ld_rows = -jnp.exp(alog_ref[...]) * softplus
        hi, lo = _split(ld_rows)
        tri_blocks = tri_ref[...]
        g_rows = _dg(hi, tri_blocks, NT) + _dg(lo, tri_blocks, NT)
        beta_rows = _sigmoid(ab_ref[HEADS:2 * HEADS, :])
        rows_scr[...] = g_rows
        rows = jnp.concatenate([g_rows, beta_rows, jnp.zeros((LANES - 2 * HEADS, ts), F32)], axis=0)
        r_hi, r_lo = _split(rows)
        r_lo2 = (rows - r_hi.astype(F32) - r_lo.astype(F32)).astype(BF16)
        eye_ts = eye_ref[...]
        cols_scr[...] = _dg(eye_ts, r_hi, NT) + _dg(eye_ts, r_lo, NT) + _dg(eye_ts, r_lo2, NT)

    lane = _iota2((ts, LANES), 1)
    cols = cols_scr[...]
    gc_all = jnp.sum(jnp.where(lane == h, cols, 0.0), axis=1, keepdims=True)
    bc_all = jnp.sum(jnp.where(lane == h + HEADS, cols, 0.0), axis=1, keepdims=True)
    g_row = rows_scr[pl.ds(h, 1), :]
    egc_all = jnp.exp(gc_all)

    r = _iota2((c, c), 0)
    cidx = _iota2((c, c), 1)
    causal = cidx <= r
    strict = cidx < r
    eye_f = (r == cidx).astype(F32)
    chunks = [slice(n * c, (n + 1) * c) for n in range(n_chunks)]

    q16 = q_all.astype(BF16)
    k16 = k_all.astype(BF16)
    kq = [_dg(jnp.concatenate([k16[sl], q16[sl]], axis=0), k16[sl], NT) for sl in chunks]
    dm = []
    for sl in chunks:
        diff = gc_all[sl] - g_row[:, sl]
        dm.append(jnp.where(causal, jnp.exp(jnp.where(causal, diff, 0.0)), 0.0))
    bm = [-jnp.where(strict, bc_all[sl] * x[0:c] * d, 0.0) for sl, x, d in zip(chunks, kq, dm)]
    p = [eye_f + b for b in bm]
    bm = [_dot(b.astype(BF16), b.astype(BF16)) for b in bm]
    for _ in range(c.bit_length() - 3):
        res = [_dot(b.astype(BF16), jnp.concatenate([b, pp], axis=1).astype(BF16)) for b, pp in zip(bm, p)]
        p = [pp + x[:, c:2 * c] for pp, x in zip(p, res)]
        bm = [x[:, 0:c] for x in res]
    p = [pp + _dot(b.astype(BF16), pp.astype(BF16)) for b, pp in zip(bm, p)]
    rhs = jnp.concatenate([v_all * bc_all, k_all * (bc_all * egc_all)], axis=1).astype(BF16)
    sol = [_dot(pp.astype(BF16), rhs[sl]) for pp, sl in zip(p, chunks)]
    qd_all = (q_all * egc_all).astype(BF16)
    for n, sl in enumerate(chunks):
        g_last = gc_all[(n + 1) * c - 1:(n + 1) * c, :]
        u_ref[sl, :] = sol[n][:, 0:hd].astype(BF16)
        wqd_ref[2 * n * c:(2 * n + 1) * c, :] = sol[n][:, hd:2 * hd].astype(BF16)
        wqd_ref[(2 * n + 1) * c:(2 * n + 2) * c, :] = qd_all[sl]
        ku_ref[sl, :] = (k_all[sl] * jnp.exp(g_last - gc_all[sl])).astype(BF16)
        attn_ref[sl, :] = (kq[n][c:2 * c] * dm[n]).astype(BF16)
        dl_ref[n:n + 1, :] = jnp.broadcast_to(jnp.exp(g_last), (1, hd))


def _gdn_prep(proj, conv_w, ab_t, a_log, dt_bias, ts):
    s = proj.shape[0]
    hd = HEAD_DIM
    c = CHUNK
    q0 = 4 * HEADS
    cur = lambda off: pl.BlockSpec((ts, hd), lambda j, h, off=off: (j, off + h))
    prev = lambda off: pl.BlockSpec((8, hd), lambda j, h, off=off: (jnp.maximum(j * (ts // 8) - 1, 0), off + h))
    cw = lambda off: pl.BlockSpec((CONV_WIDTH, hd), lambda j, h, off=off: (0, off + h))
    per_head_scalar = pl.BlockSpec((HEADS, 1), lambda j, h: (0, 0))
    const = pl.BlockSpec((ts, ts), lambda j, h: (0, 0))
    pos = jnp.arange(ts)
    tri_blocks = ((pos[:, None] // c == pos[None, :] // c) & (pos[None, :] <= pos[:, None])).astype(BF16)
    eye = (pos[:, None] == pos[None, :]).astype(BF16)
    per_head = lambda rows, cols: pl.BlockSpec((None, rows, cols), lambda j, h: (h, j, 0))
    return pl.pallas_call(
        functools.partial(_gdn_prep_kernel, n_chunks=ts // c, ts=ts),
        grid=(s // ts, HEADS),
        in_specs=[cur(q0), cur(q0 + HEADS), cur(q0 + 2 * HEADS),
                  prev(q0), prev(q0 + HEADS), prev(q0 + 2 * HEADS),
                  cw(0), cw(HEADS), cw(2 * HEADS),
                  pl.BlockSpec((2 * HEADS, ts), lambda j, h: (0, j)),
                  per_head_scalar, per_head_scalar, const, const],
        out_specs=[pl.BlockSpec((ts, hd), lambda j, h: (j, h)),
                   pl.BlockSpec((2 * ts, hd), lambda j, h: (j, h)),
                   pl.BlockSpec((ts, hd), lambda j, h: (j, h)),
                   per_head(ts, c),
                   per_head(ts // c, hd)],
        out_shape=[jax.ShapeDtypeStruct((s, HEADS * hd), BF16),
                   jax.ShapeDtypeStruct((2 * s, HEADS * hd), BF16),
                   jax.ShapeDtypeStruct((s, HEADS * hd), BF16),
                   jax.ShapeDtypeStruct((HEADS, s, c), BF16),
                   jax.ShapeDtypeStruct((HEADS, s // c, hd), F32)],
        scratch_shapes=[pltpu.VMEM((ts + 8, hd), F32), pltpu.VMEM((HEADS, ts), F32), pltpu.VMEM((ts, LANES), F32)],
        compiler_params=_params(("arbitrary", "arbitrary")),
    )(proj, proj, proj, proj, proj, proj, conv_w, conv_w, conv_w,
      ab_t, a_log.reshape(HEADS, 1), dt_bias.reshape(HEADS, 1), tri_blocks, eye)


def _gdn_scan_kernel(u_ref, wqd_ref, ku_ref, attn_ref, dl_ref, g_ref, on_ref, o_ref, st_scr, *, n_chunks):
    @pl.when(pl.program_id(0) == 0)
    def _():
        st_scr[...] = jnp.zeros_like(st_scr)

    c = CHUNK
    hd = HEAD_DIM
    on_g = on_ref[...]
    heads = [slice(h * hd, (h + 1) * hd) for h in range(HEADS)]

    def chunk(n, carry):
        rows = pl.ds(pl.multiple_of(n * c, c), c)
        rows2 = pl.ds(pl.multiple_of(2 * n * c, 2 * c), 2 * c)
        sts = [st_scr[h] for h in range(HEADS)]
        wq = [_dot(wqd_ref[rows2, sl], st.astype(BF16)) for sl, st in zip(heads, sts)]
        vn = [(u_ref[rows, sl].astype(F32) - x[0:c]).astype(BF16) for sl, x in zip(heads, wq)]
        upd = [_dg(ku_ref[rows, sl], v, TN) for sl, v in zip(heads, vn)]
        for h in range(HEADS):
            st_scr[h] = dl_ref[h, pl.ds(n, 1), :] * sts[h] + upd[h]
        outs = [_rms(x[c:2 * c] + _dot(attn_ref[h, rows, :], v), NORM_EPS)
                for h, (x, v) in enumerate(zip(wq, vn))]
        gate = jnp.tile(on_g, (1, HEADS)) * _silu(g_ref[rows, :].astype(F32))
        o_ref[rows, :] = (jnp.concatenate(outs, axis=1) * gate).astype(BF16)
        return carry

    lax.fori_loop(0, n_chunks, chunk, 0)


def _gdn_scan(u, wqd, ku, attn, dl, proj, onorm_g, ts):
    s, width = u.shape
    c = CHUNK
    gate_blk = (4 * HEADS + 3 * HEADS) * HEAD_DIM // width
    return pl.pallas_call(
        functools.partial(_gdn_scan_kernel, n_chunks=ts // c),
        grid=(s // ts,),
        in_specs=[pl.BlockSpec((ts, width), lambda j: (j, 0)),
                  pl.BlockSpec((2 * ts, width), lambda j: (j, 0)),
                  pl.BlockSpec((ts, width), lambda j: (j, 0)),
                  pl.BlockSpec((HEADS, ts, c), lambda j: (0, j, 0)),
                  pl.BlockSpec((HEADS, ts // c, HEAD_DIM), lambda j: (0, j, 0)),
                  pl.BlockSpec((ts, width), lambda j: (j, gate_blk)),
                  pl.BlockSpec((1, HEAD_DIM), lambda j: (0, 0))],
        out_specs=pl.BlockSpec((ts, width), lambda j: (j, 0)),
        out_shape=jax.ShapeDtypeStruct((s, width), BF16),
        scratch_shapes=[pltpu.VMEM((HEADS, HEAD_DIM, HEAD_DIM), F32)],
        compiler_params=_params(("arbitrary",)),
    )(u, wqd, ku, attn, dl, proj, onorm_g)


def _merge_kernel(oa_ref, ob_ref, mga_ref, mgb_ref, x_ref, wa_ref, wb_ref, wo_ref, gt_ref, g2_ref, sc_ref,
                  sh_ref, x1_ref, h2_ref):
    ya = _dot(oa_ref[...], wa_ref[...])
    yb = _dot(ob_ref[...], wb_ref[...])
    merged = _sigmoid(mga_ref[...].astype(F32)) * ya + _sigmoid(mgb_ref[...].astype(F32)) * yb
    x1 = x_ref[...] + gt_ref[...] * _dot(merged.astype(BF16), wo_ref[...])
    x1_ref[...] = x1
    h2 = _rms(x1, NORM_EPS) * g2_ref[...] * (1.0 + sc_ref[...]) + sh_ref[...]
    h2_ref[...] = h2.astype(BF16)


def _merge(o_a, o_b, proj, x, w_a, w_b, w_o, gt1, g2, sc2, sh2, tm):
    s, d = x.shape
    dv = o_a.shape[1]
    mg0 = (8 * HEADS * HEAD_DIM) // d
    const = lambda shape: pl.BlockSpec(shape, lambda i: (0, 0), pipeline_mode=pl.Buffered(1))
    return pl.pallas_call(
        _merge_kernel,
        grid=(s // tm,),
        in_specs=[pl.BlockSpec((tm, dv), lambda i: (i, 0)),
                  pl.BlockSpec((tm, dv), lambda i: (i, 0)),
                  pl.BlockSpec((tm, d), lambda i: (i, mg0)),
                  pl.BlockSpec((tm, d), lambda i: (i, mg0 + 1)),
                  pl.BlockSpec((tm, d), lambda i: (i, 0)),
                  const((dv, d)), const((dv, d)), const((d, d)),
                  const((1, d)), const((1, d)), const((1, d)), const((1, d))],
        out_specs=[pl.BlockSpec((tm, d), lambda i: (i, 0)), pl.BlockSpec((tm, d), lambda i: (i, 0))],
        out_shape=[jax.ShapeDtypeStruct((s, d), F32), jax.ShapeDtypeStruct((s, d), BF16)],
        compiler_params=_params(("arbitrary",)),
    )(o_a, o_b, proj, proj, x, w_a, w_b, w_o, gt1, g2, sc2, sh2)


def _first_max(vals, iota, size, axis):
    m = jnp.max(vals, axis=axis, keepdims=True)
    idx = jnp.min(jnp.where(vals == m, iota, size), axis=axis, keepdims=True)
    return m, idx


def _router_kernel(x1_ref, g2_ref, sc_ref, sh_ref, wrt_ref, bias_ref, upper_ref, pos_ref, wts_ref, before_ref,
                   ntile_ref, cnt_scr, *, tm):
    @pl.when(pl.program_id(0) == 0)
    def _():
        cnt_scr[...] = jnp.zeros_like(cnt_scr)

    e = N_EXPERTS
    h2 = _rms(x1_ref[...], NORM_EPS) * g2_ref[...] * (1.0 + sc_ref[...]) + sh_ref[...]
    logits = lax.dot_general(wrt_ref[...], h2, NT, preferred_element_type=F32,
                             precision=lax.Precision.HIGHEST)
    scores = _sigmoid(logits)
    biased = scores + bias_ref[...]
    neg = -jnp.inf

    g3 = biased.reshape(N_GROUPS, GROUP_SIZE, tm)
    i3 = lax.broadcasted_iota(I32, g3.shape, 1)
    m1, a1 = _first_max(g3, i3, GROUP_SIZE, 1)
    m2 = jnp.max(jnp.where(i3 == a1, neg, g3), axis=1, keepdims=True)
    gs = (m1 + m2).reshape(N_GROUPS, tm)
    ig = _iota2(gs.shape, 0)
    gmask = jnp.zeros(gs.shape, jnp.bool_)
    for _ in range(TOPK_GROUPS):
        _, a = _first_max(gs, ig, N_GROUPS, 0)
        pick = ig == a
        gmask = jnp.logical_or(gmask, pick)
        gs = jnp.where(pick, neg, gs)
    emask = jnp.broadcast_to(gmask.reshape(N_GROUPS, 1, tm), (N_GROUPS, GROUP_SIZE, tm)).reshape(e, tm)

    cand = jnp.where(emask, biased, neg)
    ie = _iota2((e, tm), 0)
    sel_all = jnp.zeros((e, tm), jnp.bool_)
    w_rows, picks = [], []
    for _ in range(TOP_K):
        _, a = _first_max(cand, ie, e, 0)
        pick = ie == a
        picks.append(pick)
        w_rows.append(jnp.sum(jnp.where(pick, scores, 0.0), axis=0, keepdims=True))
        sel_all = jnp.logical_or(sel_all, pick)
        cand = jnp.where(pick, neg, cand)
    w_sum = w_rows[0]
    for wr in w_rows[1:]:
        w_sum = w_sum + wr
    wts = jnp.concatenate(w_rows, axis=0) / w_sum * ROUTED_SCALE

    sel = sel_all.astype(BF16)
    in_expert = _dot(sel, upper_ref[...])
    n_tile = jnp.sum(sel_all.astype(F32), axis=1, keepdims=True)
    lower = (_iota2((e, e), 1) < _iota2((e, e), 0)).astype(BF16)
    expert_off = _dot_exact_lhs(lower, jnp.broadcast_to(n_tile, (e, LANES)))[:, 0:1]
    place = in_expert + expert_off
    pos = jnp.concatenate([jnp.sum(jnp.where(pk, place, 0.0), axis=0, keepdims=True) for pk in picks], axis=0)
    pos_ref[...] = pos.astype(I32)
    before_ref[...] = jnp.broadcast_to(cnt_scr[...], before_ref.shape).astype(I32)
    ntile_ref[...] = jnp.broadcast_to(n_tile, ntile_ref.shape).astype(I32)
    cnt_scr[...] = cnt_scr[...] + n_tile
    wts_ref[...] = wts


def _router(x1, g2, sc2, sh2, w_router_t, bias_col, tm):
    s, d = x1.shape
    e = N_EXPERTS
    nt = s // tm
    upper = (jnp.arange(tm)[:, None] < jnp.arange(tm)[None, :]).astype(BF16)
    const = lambda shape: pl.BlockSpec(shape, lambda i: (0, 0))
    per_tile = pl.BlockSpec((None, e, LANES), lambda i: (i, 0, 0))
    return pl.pallas_call(
        functools.partial(_router_kernel, tm=tm),
        grid=(nt,),
        in_specs=[pl.BlockSpec((tm, d), lambda i: (i, 0)),
                  const((1, d)), const((1, d)), const((1, d)),
                  const((e, d)), const((e, 1)), const((tm, tm))],
        out_specs=[pl.BlockSpec((TOP_K, tm), lambda i: (0, i)),
                   pl.BlockSpec((TOP_K, tm), lambda i: (0, i)),
                   per_tile, per_tile],
        out_shape=[jax.ShapeDtypeStruct((TOP_K, s), I32), jax.ShapeDtypeStruct((TOP_K, s), F32),
                   jax.ShapeDtypeStruct((nt, e, LANES), I32), jax.ShapeDtypeStruct((nt, e, LANES), I32)],
        scratch_shapes=[pltpu.VMEM((e, 1), F32)],
        compiler_params=_params(("arbitrary",)),
    )(x1, g2, sc2, sh2, w_router_t, bias_col, upper)


def _run_sizes(limit):
    return [1 << b for b in range(limit.bit_length() - 1, -1, -1)]


def _for_each_run(step, run_len_ref, run_dst_ref, tm, make_copy, fn):
    def per_expert(ex, tile_slot):
        n = run_len_ref[step * N_EXPERTS + ex]
        dst = run_dst_ref[step * N_EXPERTS + ex]
        for size in _run_sizes(tm):
            take = (n & size) != 0

            @pl.when(take)
            def _(tile_slot=tile_slot, dst=dst, size=size):
                fn(make_copy(tile_slot, dst, size))

            inc = jnp.where(take, size, 0)
            tile_slot = tile_slot + inc
            dst = dst + inc
        return tile_slot

    lax.fori_loop(0, N_EXPERTS, per_expert, jnp.int32(0))


def _slot_rows(slot, n_slots):
    return pl.ds(pl.multiple_of(slot * SUBLANES, SUBLANES), n_slots * SUBLANES)


def _dispatch_kernel(run_len_ref, run_dst_ref, pad_lo_ref, pad_hi_ref, pos_ref, h_ref, xs_ref, stage, zero_scr,
                     sem, pad_sem, *, tm, rows_per_pass):
    step = pl.program_id(0)
    na = TOP_K * tm
    d = h_ref.shape[1]
    half = d // 2
    n_words = half // LANES

    def pad_copy(slot, n_slots):
        return pltpu.make_async_copy(zero_scr.at[pl.ds(0, n_slots * SUBLANES), :],
                                     xs_ref.at[_slot_rows(slot, n_slots), :], pad_sem)

    def for_each_pad(fn):
        def per_expert(ex, carry):
            slot = pad_lo_ref[ex]
            n = pad_hi_ref[ex] - slot
            for size in _run_sizes(EXPERT_BLOCK - 1):
                take = (n & size) != 0

                @pl.when(take)
                def _(slot=slot, size=size):
                    fn(pad_copy(slot, size))

                slot = slot + jnp.where(take, size, 0)
            return carry
        lax.fori_loop(0, N_EXPERTS, per_expert, 0)

    @pl.when(step == 0)
    def _():
        zero_scr[...] = jnp.zeros_like(zero_scr)
        for_each_pad(lambda cp: cp.start())

    buf = step % 2
    pos = pos_ref[...]
    h = h_ref[...]
    for a0 in range(0, na, rows_per_pass):
        slot_id = a0 + _iota2((rows_per_pass, tm), 0)
        hit = pos[0:1, :] == slot_id
        for k in range(1, TOP_K):
            hit = jnp.logical_or(hit, pos[k:k + 1, :] == slot_id)
        rows = _dot(hit.astype(BF16), h)
        for i in range(n_words):
            word = _pack_halves(rows[:, i * LANES:(i + 1) * LANES], rows[:, half + i * LANES:half + (i + 1) * LANES])
            stage[buf, pl.ds(a0 * SUBLANES + i, rows_per_pass, stride=SUBLANES), :] = word

    def run_copy(tile_slot, sorted_slot, n_slots):
        return pltpu.make_async_copy(stage.at[buf, _slot_rows(tile_slot, n_slots), :],
                                     xs_ref.at[_slot_rows(sorted_slot, n_slots), :], sem.at[buf])

    def wait_tile(which):
        pltpu.make_async_copy(stage.at[which], xs_ref.at[pl.ds(0, na * SUBLANES), :], sem.at[which]).wait()

    _for_each_run(step, run_len_ref, run_dst_ref, tm, run_copy, lambda cp: cp.start())

    @pl.when(step > 0)
    def _():
        wait_tile(1 - buf)

    @pl.when(step == pl.num_programs(0) - 1)
    def _():
        wait_tile(buf)

    @pl.when(step == 0)
    def _():
        for_each_pad(lambda cp: cp.wait())


def _dispatch(run_len, run_dst, pad_lo, pad_hi, pos_t, h2, n_slots, tm):
    s, d = h2.shape
    assert (d // 2) % LANES == 0 and (d // 2) // LANES == SUBLANES, "one token row must pack into one (8, 128) tile"
    na = TOP_K * tm
    return pl.pallas_call(
        functools.partial(_dispatch_kernel, tm=tm, rows_per_pass=min(512, na)),
        grid_spec=pltpu.PrefetchScalarGridSpec(
            num_scalar_prefetch=4,
            grid=(s // tm,),
            in_specs=[pl.BlockSpec((TOP_K, tm), lambda i, *_: (0, i)),
                      pl.BlockSpec((tm, d), lambda i, *_: (i, 0))],
            out_specs=pl.BlockSpec(memory_space=pl.ANY),
            scratch_shapes=[pltpu.VMEM((2, na * SUBLANES, LANES), U32),
                            pltpu.VMEM((EXPERT_BLOCK // 2 * SUBLANES, LANES), U32),
                            pltpu.SemaphoreType.DMA((2,)), pltpu.SemaphoreType.DMA(())]),
        out_shape=jax.ShapeDtypeStruct((n_slots * SUBLANES, LANES), U32),
        compiler_params=_params(("arbitrary",), has_side_effects=True, disable_bounds_checks=True),
    )(run_len, run_dst, pad_lo, pad_hi, pos_t, h2)


def _expert_kernel(be_ref, nu_ref, next_ref, par_ref, x_ref, wg_hbm, wu_hbm, wd_hbm, y_ref, wg_f32, wu_f32, wd_f32,
                   wg_scr, wu_scr, wd_scr, sem):
    b = pl.program_id(0)
    bm = EXPERT_BLOCK
    active = b < nu_ref[0]
    new_expert = jnp.logical_or(b == 0, be_ref[b] != be_ref[jnp.maximum(b - 1, 0)])

    def weight_copies(ex, which):
        return [pltpu.make_async_copy(src.at[ex], dst.at[which], sem.at[which])
                for src, dst in ((wg_hbm, wg_f32), (wu_hbm, wu_f32), (wd_hbm, wd_f32))]

    @pl.when(jnp.logical_and(active, new_expert))
    def _():
        which = par_ref[b]

        @pl.when(b == 0)
        def _():
            for cp in weight_copies(be_ref[b], which):
                cp.start()

        for cp in weight_copies(be_ref[b], which):
            cp.wait()
        wg_scr[...] = wg_f32[which].astype(BF16)
        wu_scr[...] = wu_f32[which].astype(BF16)
        wd_scr[...] = wd_f32[which].astype(BF16)
        nb = next_ref[b]

        @pl.when(nb < nu_ref[0])
        def _():
            for cp in weight_copies(be_ref[nb], 1 - which):
                cp.start()

    @pl.when(active)
    def _():
        los, his = [], []
        for i in range(SUBLANES):
            lo, hi = _unpack_halves(x_ref[pl.ds(i, bm, stride=SUBLANES), :])
            los.append(lo.astype(BF16))
            his.append(hi.astype(BF16))
        xb = jnp.concatenate(los + his, axis=1)
        hid = _silu(_dot(xb, wg_scr[...])) * _dot(xb, wu_scr[...])
        y = _dot(hid.astype(BF16), wd_scr[...])
        half = y.shape[1] // 2
        for i in range(SUBLANES):
            word = _pack_halves(_round_bf16(y[:, i * LANES:(i + 1) * LANES]),
                                _round_bf16(y[:, half + i * LANES:half + (i + 1) * LANES]))
            y_ref[pl.ds(i, bm, stride=SUBLANES), :] = word


def _experts(block_e, n_used, next_block, parity, xs, w_gate, w_up, w_down):
    d, ff = w_gate.shape[1], w_gate.shape[2]
    bm = EXPERT_BLOCK
    n_blocks = xs.shape[0] // (bm * SUBLANES)
    blk = lambda b, be, nu, *_: (jnp.minimum(b, nu[0] - 1), 0)
    hbm = pl.BlockSpec(memory_space=pl.ANY)
    return pl.pallas_call(
        _expert_kernel,
        grid_spec=pltpu.PrefetchScalarGridSpec(
            num_scalar_prefetch=4,
            grid=(n_blocks,),
            in_specs=[pl.BlockSpec((bm * SUBLANES, LANES), blk), hbm, hbm, hbm],
            out_specs=pl.BlockSpec((bm * SUBLANES, LANES), blk),
            scratch_shapes=[pltpu.VMEM((2, d, ff), F32), pltpu.VMEM((2, d, ff), F32), pltpu.VMEM((2, ff, d), F32),
                            pltpu.VMEM((d, ff), BF16), pltpu.VMEM((d, ff), BF16), pltpu.VMEM((ff, d), BF16),
                            pltpu.SemaphoreType.DMA((2,))]),
        out_shape=jax.ShapeDtypeStruct(xs.shape, U32),
        compiler_params=_params(("arbitrary",)),
    )(block_e, n_used, next_block, parity, xs, w_gate, w_up, w_down)


def _combine_kernel(run_len_ref, run_src_ref, ys_ref, h_ref, x1_ref, pos_ref, wts_ref, wg_ref, wu_ref, wd_ref, gt_ref, gf_ref,
                    o_ref, stage, sem, *, tm, rows_per_pass):
    step = pl.program_id(0)
    na = TOP_K * tm
    buf = step % 2

    def fetch_tile(tile, which):
        def run_copy(tile_slot, sorted_slot, n_slots):
            return pltpu.make_async_copy(ys_ref.at[_slot_rows(sorted_slot, n_slots), :],
                                         stage.at[which, _slot_rows(tile_slot, n_slots), :], sem.at[which])
        _for_each_run(tile, run_len_ref, run_src_ref, tm, run_copy, lambda cp: cp.start())

    @pl.when(step == 0)
    def _():
        fetch_tile(step, buf)

    @pl.when(step + 1 < pl.num_programs(0))
    def _():
        fetch_tile(step + 1, 1 - buf)

    hb = h_ref[...]
    hid = _silu(_dot(hb, wg_ref[...])) * _dot(hb, wu_ref[...])
    acc = _dot(hid.astype(BF16), wd_ref[...])

    pltpu.make_async_copy(ys_ref.at[pl.ds(0, na * SUBLANES), :], stage.at[buf], sem.at[buf]).wait()

    pos = pos_ref[...]
    wts = wts_ref[...]
    for a0 in range(0, na, rows_per_pass):
        los, his = [], []
        for i in range(SUBLANES):
            lo, hi = _unpack_halves(stage[buf, pl.ds(a0 * SUBLANES + i, rows_per_pass, stride=SUBLANES), :])
            los.append(lo.astype(BF16))
            his.append(hi.astype(BF16))
        y_rows = jnp.concatenate(los + his, axis=1)
        slot_id = a0 + _iota2((rows_per_pass, tm), 0)
        wmat = jnp.zeros((rows_per_pass, tm), F32)
        for k in range(TOP_K):
            wmat = wmat + jnp.where(pos[k:k + 1, :] == slot_id, wts[k:k + 1, :], 0.0)
        acc = acc + _dg(wmat.astype(BF16), y_rows, TN)
    x2 = x1_ref[...] + gt_ref[...] * acc
    o_ref[...] = _rms(x2, NORM_EPS) * gf_ref[...]


def _combine(run_len, run_src, ys, h2, x1, pos_t, wts_t, w_gate, w_up, w_down, gt2, gf, tm):
    s, d = x1.shape
    ff = w_gate.shape[1]
    na = TOP_K * tm
    const = lambda shape: pl.BlockSpec(shape, lambda i, *_: (0, 0), pipeline_mode=pl.Buffered(1))
    tile = lambda cols: pl.BlockSpec((tm, cols), lambda i, *_: (i, 0))
    per_k = pl.BlockSpec((TOP_K, tm), lambda i, *_: (0, i))
    return pl.pallas_call(
        functools.partial(_combine_kernel, tm=tm, rows_per_pass=min(512, na)),
        grid_spec=pltpu.PrefetchScalarGridSpec(
            num_scalar_prefetch=2,
            grid=(s // tm,),
            in_specs=[pl.BlockSpec(memory_space=pl.ANY),
                      tile(d), tile(d), per_k, per_k,
                      const((d, ff)), const((d, ff)), const((ff, d)), const((1, d)), const((1, d))],
            out_specs=tile(d),
            scratch_shapes=[pltpu.VMEM((2, na * SUBLANES, LANES), U32), pltpu.SemaphoreType.DMA((2,))]),
        out_shape=jax.ShapeDtypeStruct((s, d), F32),
        compiler_params=_params(("arbitrary",), disable_bounds_checks=True),
    )(run_len, run_src, ys, h2, x1, pos_t, wts_t, w_gate, w_up, w_down, gt2, gf)


def _mixer(x2d, mod, norm1_g, norm2_g, w_in, lb, hgrn_onorm_g, gdn_conv_w, gdn_a_log, gdn_dt_bias, gdn_onorm_g,
           w_branch_hgrn, w_branch_gdn, w_out, tiles):
    d = x2d.shape[1]
    sh1, sc1, gt1, sh2, sc2, _ = [mod[:, i * d:(i + 1) * d] for i in range(6)]
    key = HEADS * HEAD_DIM
    small0 = 4 * key + 3 * key
    small1 = small0 + 2 * HEADS
    w_main = jnp.concatenate([w_in[:, :small0], w_in[:, small1:]], axis=1).astype(BF16)
    w_small_t = w_in[:, small0:small1].T.astype(BF16)
    proj, ab_t = _inproj(x2d, norm1_g, sc1, sh1, w_main, w_small_t, tiles["in_tm"], tiles["in_tn"])
    o_a = _hgrn(proj, lb, hgrn_onorm_g, tiles["mix_ts"])
    u, wqd, ku, attn, dl = _gdn_prep(proj, gdn_conv_w, ab_t, gdn_a_log, gdn_dt_bias, tiles["prep_ts"])
    o_b = _gdn_scan(u, wqd, ku, attn, dl, proj, gdn_onorm_g, tiles["mix_ts"])
    return _merge(o_a, o_b, proj, x2d, w_branch_hgrn.astype(BF16), w_branch_gdn.astype(BF16),
                  w_out.astype(BF16), gt1, norm2_g, sc2, sh2, tiles["merge_tm"])


def _moe(x1, h2, mod, norm2_g, normf_g, w_router, router_bias, w_exp_gate, w_exp_up, w_exp_down, w_sh_gate,
         w_sh_up, w_sh_down, tiles):
    s, d = x1.shape
    tm = tiles["moe_tm"]
    sh2, sc2, gt2 = [mod[:, i * d:(i + 1) * d] for i in (3, 4, 5)]
    pos_t, wts_t, before, ntile = _router(x1, norm2_g, sc2, sh2, w_router.T, router_bias.reshape(-1, 1), tm)
    bm = EXPERT_BLOCK
    n_blocks = -(-(s * TOP_K + N_EXPERTS * (bm - 1)) // bm)
    before = before[:, :, 0]
    ntile = ntile[:, :, 0]
    counts = before[-1] + ntile[-1]
    padded = (counts + bm - 1) // bm * bm
    pend = jnp.cumsum(padded).astype(I32)
    pstart = pend - padded
    block_start = jnp.arange(n_blocks, dtype=I32) * bm
    block_e = jnp.minimum(jnp.sum(pend[None, :] <= block_start[:, None], axis=1), N_EXPERTS - 1).astype(I32)
    n_used = pend[-1:] // bm
    run_len = ntile.reshape(-1)
    run_dst = (before + pstart[None, :]).reshape(-1)
    xs = _dispatch(run_len, run_dst, pstart + counts, pend, pos_t, h2, n_blocks * bm, tm)
    next_block = pend[block_e] // bm
    switches = jnp.concatenate([jnp.zeros((1,), I32), (block_e[1:] != block_e[:-1]).astype(I32)])
    parity = jnp.cumsum(switches).astype(I32) % 2
    ys = _experts(block_e, n_used, next_block, parity, xs, w_exp_gate, w_exp_up, w_exp_down)
    return _combine(run_len, run_dst, ys, h2, x1, pos_t, wts_t, w_sh_gate.astype(BF16), w_sh_up.astype(BF16),
                    w_sh_down.astype(BF16), gt2, normf_g, tm)


def _tiles(s):
    pick = lambda want: min(want, s)
    return dict(in_tm=pick(1024), in_tn=1536, mix_ts=pick(512), prep_ts=pick(1024), merge_tm=pick(512),
                moe_tm=pick(256))


def kernel(x, c, w_ada, b_ada, norm1_g, norm2_g, w_in, hgrn_lb_table, hgrn_onorm_g, gdn_conv_w, gdn_a_log, gdn_dt_bias, gdn_onorm_g, w_branch_hgrn, w_branch_gdn, w_out, w_router, router_bias, w_exp_gate, w_exp_up, w_exp_down, w_sh_gate, w_sh_up, w_sh_down, normf_g):
    b, s, d = x.shape
    assert b == 1 and w_ada.shape[0] == 1, "one sequence, one layer"
    tiles = _tiles(s)
    lb = jnp.cumsum(jax.nn.softmax(hgrn_lb_table.astype(F32), axis=0), axis=0)[0:1]
    mod = _ada(c, w_ada[0], b_ada[0])
    row = lambda v: v.reshape(1, -1)
    x1, h2 = _mixer(x[0], mod, row(norm1_g[0]), row(norm2_g[0]), w_in[0], lb, row(hgrn_onorm_g[0]), gdn_conv_w[0],
                    gdn_a_log[0], gdn_dt_bias[0], row(gdn_onorm_g[0]), w_branch_hgrn[0], w_branch_gdn[0], w_out[0],
                    tiles)
    out = _moe(x1, h2, mod, row(norm2_g[0]), row(normf_g), w_router[0], router_bias[0], w_exp_gate[0],
               w_exp_up[0], w_exp_down[0], w_sh_gate[0], w_sh_up[0], w_sh_down[0], tiles)
    return out[None]
```

```python
import functools

import jax
import jax.numpy as jnp
from jax import lax
from jax.experimental import pallas as pl
from jax.experimental.pallas import tpu as pltpu

F32 = jnp.float32
BF16 = jnp.bfloat16
I32 = jnp.int32
U32 = jnp.uint32

NORM_EPS = 1e-6
L2_EPS = 1e-6
HEADS = 8
HEAD_DIM = 128
CONV_WIDTH = 4
CHUNK = 64
N_EXPERTS = 64
N_GROUPS = 8
GROUP_SIZE = N_EXPERTS // N_GROUPS
TOPK_GROUPS = 4
TOP_K = 8
ROUTED_SCALE = 2.5
EXPERT_BLOCK = 512

LANES = 128
SUBLANES = 8
VMEM_LIMIT = 56 * 1024 * 1024

NT = (((1,), (1,)), ((), ()))
TN = (((0,), (0,)), ((), ()))


def _params(sem, **kw):
    return pltpu.CompilerParams(dimension_semantics=sem, vmem_limit_bytes=VMEM_LIMIT, **kw)


def _dot(a, b):
    return jnp.dot(a, b, preferred_element_type=F32)


def _dg(a, b, dims):
    return lax.dot_general(a, b, dims, preferred_element_type=F32)


def _split(x):
    hi = x.astype(BF16)
    lo = (x - hi.astype(F32)).astype(BF16)
    return hi, lo


def _dot_exact_lhs(a_bf16, x, dims=None):
    hi, lo = _split(x)
    if dims is None:
        return _dot(a_bf16, hi) + _dot(a_bf16, lo)
    return _dg(a_bf16, hi, dims) + _dg(a_bf16, lo, dims)


def _sigmoid(x):
    return 1.0 / (1.0 + jnp.exp(-x))


def _silu(x):
    return x * _sigmoid(x)


def _rms(x, eps):
    return x * lax.rsqrt(jnp.mean(x * x, axis=-1, keepdims=True) + eps)


def _iota2(shape, dim):
    return lax.broadcasted_iota(I32, shape, dim)


def _pack_halves(lo, hi):
    lo_bits = lax.shift_right_logical(pltpu.bitcast(lo, U32), U32(16))
    hi_bits = pltpu.bitcast(hi, U32) & U32(0xFFFF0000)
    return lo_bits | hi_bits


def _unpack_halves(word):
    lo = pltpu.bitcast(lax.shift_left(word, U32(16)), F32)
    hi = pltpu.bitcast(word & U32(0xFFFF0000), F32)
    return lo, hi


def _round_bf16(x):
    return x.astype(BF16).astype(F32)


def _ada_kernel(c_ref, w_ref, b_ref, o_ref):
    cond = _silu(c_ref[...])
    o_ref[...] = jnp.dot(cond, w_ref[...], preferred_element_type=F32,
                         precision=lax.Precision.HIGHEST) + b_ref[...]


def _ada(c, w_ada, b_ada):
    d, n = w_ada.shape
    tn = 1024
    c8 = jnp.broadcast_to(c, (SUBLANES, d))
    out = pl.pallas_call(
        _ada_kernel,
        grid=(n // tn,),
        in_specs=[pl.BlockSpec((SUBLANES, d), lambda j: (0, 0)),
                  pl.BlockSpec((d, tn), lambda j: (0, j)),
                  pl.BlockSpec((1, tn), lambda j: (0, j))],
        out_specs=pl.BlockSpec((SUBLANES, tn), lambda j: (0, j)),
        out_shape=jax.ShapeDtypeStruct((SUBLANES, n), F32),
        compiler_params=_params(("arbitrary",)),
    )(c8, w_ada, b_ada.reshape(1, n))
    return out[0:1]


def _inproj_kernel(x_ref, g_ref, sc_ref, sh_ref, w_ref, wst_ref, proj_ref, smallt_ref, h_scr):
    @pl.when(pl.program_id(1) == 0)
    def _():
        h = _rms(x_ref[...], NORM_EPS) * g_ref[...] * (1.0 + sc_ref[...]) + sh_ref[...]
        hb = h.astype(BF16)
        h_scr[...] = hb
        smallt_ref[...] = _dg(wst_ref[...], hb, NT)

    proj_ref[...] = _dot(h_scr[...], w_ref[...]).astype(BF16)


def _inproj(x, g, sc, sh, w_main, w_small_t, tm, tn):
    s, d = x.shape
    n = w_main.shape[1]
    ns = w_small_t.shape[0]
    row = lambda i, j: (0, 0)
    return pl.pallas_call(
        _inproj_kernel,
        grid=(s // tm, n // tn),
        in_specs=[pl.BlockSpec((tm, d), lambda i, j: (i, 0)),
                  pl.BlockSpec((1, d), row), pl.BlockSpec((1, d), row), pl.BlockSpec((1, d), row),
                  pl.BlockSpec((d, tn), lambda i, j: (0, j)),
                  pl.BlockSpec((ns, d), row)],
        out_specs=[pl.BlockSpec((tm, tn), lambda i, j: (i, j)),
                   pl.BlockSpec((ns, tm), lambda i, j: (0, i))],
        out_shape=[jax.ShapeDtypeStruct((s, n), BF16), jax.ShapeDtypeStruct((ns, s), F32)],
        scratch_shapes=[pltpu.VMEM((tm, d), BF16)],
        compiler_params=_params(("arbitrary", "arbitrary")),
    )(x, g, sc, sh, w_main, w_small_t)


def _hgrn_kernel(q_ref, f_ref, i_ref, g_ref, lb_ref, on_ref, o_ref, st_scr, *, n_chunks):
    @pl.when(pl.program_id(0) == 0)
    def _():
        st_scr[...] = jnp.zeros_like(st_scr)

    c = CHUNK
    hd = HEAD_DIM
    causal = _iota2((c, c), 1) <= _iota2((c, c), 0)
    tri = causal.astype(BF16)
    lb = lb_ref[...]
    on_g = on_ref[...]
    heads = [slice(h * hd, (h + 1) * hd) for h in range(HEADS)]

    def chunk(n, carry):
        rows = pl.ds(pl.multiple_of(n * c, c), c)
        f = lb + (1.0 - lb) * _sigmoid(f_ref[rows, :].astype(F32))
        b = _dot_exact_lhs(tri, jnp.log(f))
        k = 1.0 - f
        q = _silu(q_ref[rows, :].astype(F32)) * (hd ** -0.5)
        v = i_ref[rows, :]
        b_mid = b[c // 2:c // 2 + 1, :]
        b_last = b[c - 1:c, :]
        qa = (q * jnp.exp(b - b_mid)).astype(BF16)
        ka = (k * jnp.exp(b_mid - b)).astype(BF16)
        qi = (q * jnp.exp(b)).astype(BF16)
        ku = (k * jnp.exp(b_last - b)).astype(BF16)
        dec = jnp.exp(b_last)
        gate = on_g * _silu(g_ref[rows, :].astype(F32))
        sts = [st_scr[h] for h in range(HEADS)]
        scores = [jnp.where(causal, _dg(qa[:, sl], ka[:, sl], NT), 0.0).astype(BF16) for sl in heads]
        inter = [_dg(qi[:, sl], st.astype(BF16), NT) for sl, st in zip(heads, sts)]
        kv = [_dg(v[:, sl], ku[:, sl], TN) for sl in heads]
        for h, sl in enumerate(heads):
            st_scr[h] = dec[:, sl] * sts[h] + kv[h]
        outs = [_rms(_dot(sc, v[:, sl]) + it, NORM_EPS) for sc, sl, it in zip(scores, heads, inter)]
        o_ref[rows, :] = (jnp.concatenate(outs, axis=1) * gate).astype(BF16)
        return carry

    lax.fori_loop(0, n_chunks, chunk, 0)


def _hgrn(proj, lb, onorm_g, ts):
    s = proj.shape[0]
    width = HEADS * HEAD_DIM
    col = lambda blk: pl.BlockSpec((ts, width), lambda j, blk=blk: (j, blk))
    const = pl.BlockSpec((1, width), lambda j: (0, 0))
    return pl.pallas_call(
        functools.partial(_hgrn_kernel, n_chunks=ts // CHUNK),
        grid=(s // ts,),
        in_specs=[col(0), col(1), col(2), col(3), const, const],
        out_specs=pl.BlockSpec((ts, width), lambda j: (j, 0)),
        out_shape=jax.ShapeDtypeStruct((s, width), BF16),
        scratch_shapes=[pltpu.VMEM((HEADS, HEAD_DIM, HEAD_DIM), F32)],
        compiler_params=_params(("arbitrary",)),
    )(proj, proj, proj, proj, lb, jnp.tile(onorm_g, (1, HEADS)))


def _gdn_prep_kernel(q_ref, k_ref, v_ref, qp_ref, kp_ref, vp_ref, wq_ref, wk_ref, wv_ref, ab_ref, alog_ref,
                     dtb_ref, tri_ref, eye_ref, u_ref, wqd_ref, ku_ref, attn_ref, dl_ref, cat_scr, rows_scr, cols_scr,
                     *, n_chunks, ts):
    h = pl.program_id(1)
    first = pl.program_id(0) == 0
    c = CHUNK
    hd = HEAD_DIM

    def conv_silu(cur_ref, prev_ref, w_ref):
        cat_scr[0:8, :] = jnp.where(first, 0.0, prev_ref[...].astype(F32))
        cat_scr[8:8 + ts, :] = cur_ref[...].astype(F32)
        acc = None
        for j in range(CONV_WIDTH):
            off = 8 - (CONV_WIDTH - 1) + j
            term = cat_scr[off:off + ts, :] * w_ref[j:j + 1, :]
            acc = term if acc is None else acc + term
        return _silu(acc)

    def l2n(x):
        return x * lax.rsqrt(jnp.sum(x * x, axis=-1, keepdims=True) + L2_EPS)

    q_all = l2n(conv_silu(q_ref, qp_ref, wq_ref)) * (hd ** -0.5)
    k_all = l2n(conv_silu(k_ref, kp_ref, wk_ref))
    v_all = conv_silu(v_ref, vp_ref, wv_ref)

    @pl.when(h == 0)
    def _():
        z = ab_ref[0:HEADS, :] + dtb_ref[...]
        softplus = jnp.maximum(z, 0.0) + jnp.log(1.0 + jnp.exp(-jnp.abs(z)))
        ld_rows = -jnp.exp(alog_ref[...]) * softplus
        hi, lo = _split(ld_rows)
        tri_blocks = tri_ref[...]
        g_rows = _dg(hi, tri_blocks, NT) + _dg(lo, tri_blocks, NT)
        beta_rows = _sigmoid(ab_ref[HEADS:2 * HEADS, :])
        rows_scr[...] = g_rows
        rows = jnp.concatenate([g_rows, beta_rows, jnp.zeros((LANES - 2 * HEADS, ts), F32)], axis=0)
        r_hi, r_lo = _split(rows)
        r_lo2 = (rows - r_hi.astype(F32) - r_lo.astype(F32)).astype(BF16)
        eye_ts = eye_ref[...]
        cols_scr[...] = _dg(eye_ts, r_hi, NT) + _dg(eye_ts, r_lo, NT) + _dg(eye_ts, r_lo2, NT)

    lane = _iota2((ts, LANES), 1)
    cols = cols_scr[...]
    gc_all = jnp.sum(jnp.where(lane == h, cols, 0.0), axis=1, keepdims=True)
    bc_all = jnp.sum(jnp.where(lane == h + HEADS, cols, 0.0), axis=1, keepdims=True)
    g_row = rows_scr[pl.ds(h, 1), :]
    egc_all = jnp.exp(gc_all)

    r = _iota2((c, c), 0)
    cidx = _iota2((c, c), 1)
    causal = cidx <= r
    strict = cidx < r
    eye_f = (r == cidx).astype(F32)
    chunks = [slice(n * c, (n + 1) * c) for n in range(n_chunks)]

    q16 = q_all.astype(BF16)
    k16 = k_all.astype(BF16)
    kq = [_dg(jnp.concatenate([k16[sl], q16[sl]], axis=0), k16[sl], NT) for sl in chunks]
    dm = []
    for sl in chunks:
        diff = gc_all[sl] - g_row[:, sl]
        dm.append(jnp.where(causal, jnp.exp(jnp.where(causal, diff, 0.0)), 0.0))
    bm = [-jnp.where(strict, bc_all[sl] * x[0:c] * d, 0.0) for sl, x, d in zip(chunks, kq, dm)]
    p = [eye_f + b for b in bm]
    bm = [_dot(b.astype(BF16), b.astype(BF16)) for b in bm]
    for _ in range(c.bit_length() - 3):
        res = [_dot(b.astype(BF16), jnp.concatenate([b, pp], axis=1).astype(BF16)) for b, pp in zip(bm, p)]
        p = [pp + x[:, c:2 * c] for pp, x in zip(p, res)]
        bm = [x[:, 0:c] for x in res]
    p = [pp + _dot(b.astype(BF16), pp.astype(BF16)) for b, pp in zip(bm, p)]
    rhs = jnp.concatenate([v_all * bc_all, k_all * (bc_all * egc_all)], axis=1).astype(BF16)
    sol = [_dot(pp.astype(BF16), rhs[sl]) for pp, sl in zip(p, chunks)]
    qd_all = (q_all * egc_all).astype(BF16)
    for n, sl in enumerate(chunks):
        g_last = gc_all[(n + 1) * c - 1:(n + 1) * c, :]
        u_ref[sl, :] = sol[n][:, 0:hd].astype(BF16)
        wqd_ref[2 * n * c:(2 * n + 1) * c, :] = sol[n][:, hd:2 * hd].astype(BF16)
        wqd_ref[(2 * n + 1) * c:(2 * n + 2) * c, :] = qd_all[sl]
        ku_ref[sl, :] = (k_all[sl] * jnp.exp(g_last - gc_all[sl])).astype(BF16)
        attn_ref[sl, :] = (kq[n][c:2 * c] * dm[n]).astype(BF16)
        dl_ref[n:n + 1, :] = jnp.broadcast_to(jnp.exp(g_last), (1, hd))


def _gdn_prep(proj, conv_w, ab_t, a_log, dt_bias, ts):
    s = proj.shape[0]
    hd = HEAD_DIM
    c = CHUNK
    q0 = 4 * HEADS
    cur = lambda off: pl.BlockSpec((ts, hd), lambda j, h, off=off: (j, off + h))
    prev = lambda off: pl.BlockSpec((8, hd), lambda j, h, off=off: (jnp.maximum(j * (ts // 8) - 1, 0), off + h))
    cw = lambda off: pl.BlockSpec((CONV_WIDTH, hd), lambda j, h, off=off: (0, off + h))
    per_head_scalar = pl.BlockSpec((HEADS, 1), lambda j, h: (0, 0))
    const = pl.BlockSpec((ts, ts), lambda j, h: (0, 0), pipeline_mode=pl.Buffered(1))
    pos = jnp.arange(ts)
    tri_blocks = ((pos[:, None] // c == pos[None, :] // c) & (pos[None, :] <= pos[:, None])).astype(BF16)
    eye = (pos[:, None] == pos[None, :]).astype(BF16)
    per_head = lambda rows, cols: pl.BlockSpec((None, rows, cols), lambda j, h: (h, j, 0))
    return pl.pallas_call(
        functools.partial(_gdn_prep_kernel, n_chunks=ts // c, ts=ts),
        grid=(s // ts, HEADS),
        in_specs=[cur(q0), cur(q0 + HEADS), cur(q0 + 2 * HEADS),
                  prev(q0), prev(q0 + HEADS), prev(q0 + 2 * HEADS),
                  cw(0), cw(HEADS), cw(2 * HEADS),
                  pl.BlockSpec((2 * HEADS, ts), lambda j, h: (0, j)),
                  per_head_scalar, per_head_scalar, const, const],
        out_specs=[pl.BlockSpec((ts, hd), lambda j, h: (j, h)),
                   pl.BlockSpec((2 * ts, hd), lambda j, h: (j, h)),
                   pl.BlockSpec((ts, hd), lambda j, h: (j, h)),
                   per_head(ts, c),
                   per_head(ts // c, hd)],
        out_shape=[jax.ShapeDtypeStruct((s, HEADS * hd), BF16),
                   jax.ShapeDtypeStruct((2 * s, HEADS * hd), BF16),
                   jax.ShapeDtypeStruct((s, HEADS * hd), BF16),
                   jax.ShapeDtypeStruct((HEADS, s, c), BF16),
                   jax.ShapeDtypeStruct((HEADS, s // c, hd), F32)],
        scratch_shapes=[pltpu.VMEM((ts + 8, hd), F32), pltpu.VMEM((HEADS, ts), F32), pltpu.VMEM((ts, LANES), F32)],
        compiler_params=_params(("arbitrary", "arbitrary")),
    )(proj, proj, proj, proj, proj, proj, conv_w, conv_w, conv_w,
      ab_t, a_log.reshape(HEADS, 1), dt_bias.reshape(HEADS, 1), tri_blocks, eye)


def _gdn_scan_kernel(u_ref, wqd_ref, ku_ref, attn_ref, dl_ref, g_ref, on_ref, o_ref, st_scr, *, n_chunks):
    @pl.when(pl.program_id(0) == 0)
    def _():
        st_scr[...] = jnp.zeros_like(st_scr)

    c = CHUNK
    hd = HEAD_DIM
    on_g = on_ref[...]
    heads = [slice(h * hd, (h + 1) * hd) for h in range(HEADS)]

    def chunk(n, carry):
        rows = pl.ds(pl.multiple_of(n * c, c), c)
        rows2 = pl.ds(pl.multiple_of(2 * n * c, 2 * c), 2 * c)
        sts = [st_scr[h] for h in range(HEADS)]
        wq = [_dot(wqd_ref[rows2, sl], st.astype(BF16)) for sl, st in zip(heads, sts)]
        vn = [(u_ref[rows, sl].astype(F32) - x[0:c]).astype(BF16) for sl, x in zip(heads, wq)]
        upd = [_dg(ku_ref[rows, sl], v, TN) for sl, v in zip(heads, vn)]
        for h in range(HEADS):
            st_scr[h] = dl_ref[h, pl.ds(n, 1), :] * sts[h] + upd[h]
        outs = [_rms(x[c:2 * c] + _dot(attn_ref[h, rows, :], v), NORM_EPS)
                for h, (x, v) in enumerate(zip(wq, vn))]
        gate = jnp.tile(on_g, (1, HEADS)) * _silu(g_ref[rows, :].astype(F32))
        o_ref[rows, :] = (jnp.concatenate(outs, axis=1) * gate).astype(BF16)
        return carry

    lax.fori_loop(0, n_chunks, chunk, 0)


def _gdn_scan(u, wqd, ku, attn, dl, proj, onorm_g, ts):
    s, width = u.shape
    c = CHUNK
    gate_blk = (4 * HEADS + 3 * HEADS) * HEAD_DIM // width
    return pl.pallas_call(
        functools.partial(_gdn_scan_kernel, n_chunks=ts // c),
        grid=(s // ts,),
        in_specs=[pl.BlockSpec((ts, width), lambda j: (j, 0)),
                  pl.BlockSpec((2 * ts, width), lambda j: (j, 0)),
                  pl.BlockSpec((ts, width), lambda j: (j, 0)),
                  pl.BlockSpec((HEADS, ts, c), lambda j: (0, j, 0)),
                  pl.BlockSpec((HEADS, ts // c, HEAD_DIM), lambda j: (0, j, 0)),
                  pl.BlockSpec((ts, width), lambda j: (j, gate_blk)),
                  pl.BlockSpec((1, HEAD_DIM), lambda j: (0, 0))],
        out_specs=pl.BlockSpec((ts, width), lambda j: (j, 0)),
        out_shape=jax.ShapeDtypeStruct((s, width), BF16),
        scratch_shapes=[pltpu.VMEM((HEADS, HEAD_DIM, HEAD_DIM), F32)],
        compiler_params=_params(("arbitrary",)),
    )(u, wqd, ku, attn, dl, proj, onorm_g)


def _merge_kernel(oa_ref, ob_ref, mga_ref, mgb_ref, x_ref, wa_ref, wb_ref, wo_ref, gt_ref, g2_ref, sc_ref,
                  sh_ref, x1_ref, h2_ref):
    ya = _dot(oa_ref[...], wa_ref[...])
    yb = _dot(ob_ref[...], wb_ref[...])
    merged = _sigmoid(mga_ref[...].astype(F32)) * ya + _sigmoid(mgb_ref[...].astype(F32)) * yb
    x1 = x_ref[...] + gt_ref[...] * _dot(merged.astype(BF16), wo_ref[...])
    x1_ref[...] = x1
    h2 = _rms(x1, NORM_EPS) * g2_ref[...] * (1.0 + sc_ref[...]) + sh_ref[...]
    h2_ref[...] = h2.astype(BF16)


def _merge(o_a, o_b, proj, x, w_a, w_b, w_o, gt1, g2, sc2, sh2, tm):
    s, d = x.shape
    dv = o_a.shape[1]
    mg0 = (8 * HEADS * HEAD_DIM) // d
    const = lambda shape: pl.BlockSpec(shape, lambda i: (0, 0), pipeline_mode=pl.Buffered(1))
    return pl.pallas_call(
        _merge_kernel,
        grid=(s // tm,),
        in_specs=[pl.BlockSpec((tm, dv), lambda i: (i, 0)),
                  pl.BlockSpec((tm, dv), lambda i: (i, 0)),
                  pl.BlockSpec((tm, d), lambda i: (i, mg0)),
                  pl.BlockSpec((tm, d), lambda i: (i, mg0 + 1)),
                  pl.BlockSpec((tm, d), lambda i: (i, 0)),
                  const((dv, d)), const((dv, d)), const((d, d)),
                  const((1, d)), const((1, d)), const((1, d)), const((1, d))],
        out_specs=[pl.BlockSpec((tm, d), lambda i: (i, 0)), pl.BlockSpec((tm, d), lambda i: (i, 0))],
        out_shape=[jax.ShapeDtypeStruct((s, d), F32), jax.ShapeDtypeStruct((s, d), BF16)],
        compiler_params=_params(("arbitrary",)),
    )(o_a, o_b, proj, proj, x, w_a, w_b, w_o, gt1, g2, sc2, sh2)


def _first_max(vals, iota, size, axis):
    m = jnp.max(vals, axis=axis, keepdims=True)
    idx = jnp.min(jnp.where(vals == m, iota, size), axis=axis, keepdims=True)
    return m, idx


def _router_kernel(x1_ref, g2_ref, sc_ref, sh_ref, wrt_ref, bias_ref, upper_ref, pos_ref, wts_ref, before_ref,
                   ntile_ref, cnt_scr, *, tm):
    @pl.when(pl.program_id(0) == 0)
    def _():
        cnt_scr[...] = jnp.zeros_like(cnt_scr)

    e = N_EXPERTS
    h2 = _rms(x1_ref[...], NORM_EPS) * g2_ref[...] * (1.0 + sc_ref[...]) + sh_ref[...]
    logits = lax.dot_general(wrt_ref[...], h2, NT, preferred_element_type=F32,
                             precision=lax.Precision.HIGHEST)
    scores = _sigmoid(logits)
    biased = scores + bias_ref[...]
    neg = -jnp.inf

    g3 = biased.reshape(N_GROUPS, GROUP_SIZE, tm)
    i3 = lax.broadcasted_iota(I32, g3.shape, 1)
    m1, a1 = _first_max(g3, i3, GROUP_SIZE, 1)
    m2 = jnp.max(jnp.where(i3 == a1, neg, g3), axis=1, keepdims=True)
    gs = (m1 + m2).reshape(N_GROUPS, tm)
    ig = _iota2(gs.shape, 0)
    gmask = jnp.zeros(gs.shape, jnp.bool_)
    for _ in range(TOPK_GROUPS):
        _, a = _first_max(gs, ig, N_GROUPS, 0)
        pick = ig == a
        gmask = jnp.logical_or(gmask, pick)
        gs = jnp.where(pick, neg, gs)
    emask = jnp.broadcast_to(gmask.reshape(N_GROUPS, 1, tm), (N_GROUPS, GROUP_SIZE, tm)).reshape(e, tm)

    cand = jnp.where(emask, biased, neg)
    ie = _iota2((e, tm), 0)
    sel_all = jnp.zeros((e, tm), jnp.bool_)
    w_rows, picks = [], []
    for _ in range(TOP_K):
        _, a = _first_max(cand, ie, e, 0)
        pick = ie == a
        picks.append(pick)
        w_rows.append(jnp.sum(jnp.where(pick, scores, 0.0), axis=0, keepdims=True))
        sel_all = jnp.logical_or(sel_all, pick)
        cand = jnp.where(pick, neg, cand)
    w_sum = w_rows[0]
    for wr in w_rows[1:]:
        w_sum = w_sum + wr
    wts = jnp.concatenate(w_rows, axis=0) / w_sum * ROUTED_SCALE

    sel = sel_all.astype(BF16)
    in_expert = _dot(sel, upper_ref[...])
    n_tile = jnp.sum(sel_all.astype(F32), axis=1, keepdims=True)
    lower = (_iota2((e, e), 1) < _iota2((e, e), 0)).astype(BF16)
    expert_off = _dot_exact_lhs(lower, jnp.broadcast_to(n_tile, (e, LANES)))[:, 0:1]
    place = in_expert + expert_off
    pos = jnp.concatenate([jnp.sum(jnp.where(pk, place, 0.0), axis=0, keepdims=True) for pk in picks], axis=0)
    pos_ref[...] = pos.astype(I32)
    before_ref[...] = jnp.broadcast_to(cnt_scr[...], before_ref.shape).astype(I32)
    ntile_ref[...] = jnp.broadcast_to(n_tile, ntile_ref.shape).astype(I32)
    cnt_scr[...] = cnt_scr[...] + n_tile
    wts_ref[...] = wts


def _router(x1, g2, sc2, sh2, w_router_t, bias_col, tm):
    s, d = x1.shape
    e = N_EXPERTS
    nt = s // tm
    upper = (jnp.arange(tm)[:, None] < jnp.arange(tm)[None, :]).astype(BF16)
    const = lambda shape: pl.BlockSpec(shape, lambda i: (0, 0))
    per_tile = pl.BlockSpec((None, e, LANES), lambda i: (i, 0, 0))
    return pl.pallas_call(
        functools.partial(_router_kernel, tm=tm),
        grid=(nt,),
        in_specs=[pl.BlockSpec((tm, d), lambda i: (i, 0)),
                  const((1, d)), const((1, d)), const((1, d)),
                  const((e, d)), const((e, 1)), const((tm, tm))],
        out_specs=[pl.BlockSpec((TOP_K, tm), lambda i: (0, i)),
                   pl.BlockSpec((TOP_K, tm), lambda i: (0, i)),
                   per_tile, per_tile],
        out_shape=[jax.ShapeDtypeStruct((TOP_K, s), I32), jax.ShapeDtypeStruct((TOP_K, s), F32),
                   jax.ShapeDtypeStruct((nt, e, LANES), I32), jax.ShapeDtypeStruct((nt, e, LANES), I32)],
        scratch_shapes=[pltpu.VMEM((e, 1), F32)],
        compiler_params=_params(("arbitrary",)),
    )(x1, g2, sc2, sh2, w_router_t, bias_col, upper)


def _run_sizes(limit):
    return [1 << b for b in range(limit.bit_length() - 1, -1, -1)]


def _for_each_run(tile, run_refs, tm, make_copy, fn):
    run_len_ref, run_off_ref, run_dst_ref = run_refs

    def per_expert(ex, carry):
        n = run_len_ref[tile * N_EXPERTS + ex]
        off = run_off_ref[tile * N_EXPERTS + ex]
        dst = run_dst_ref[tile * N_EXPERTS + ex]
        for size in _run_sizes(tm):
            done = n & (-2 * size)

            @pl.when((n & size) != 0)
            def _(done=done, size=size):
                fn(make_copy(off + done, dst + done, size))

        return carry

    lax.fori_loop(0, N_EXPERTS, per_expert, 0)


def _slot_rows(slot, n_slots):
    return pl.ds(pl.multiple_of(slot * SUBLANES, SUBLANES), n_slots * SUBLANES)


def _dispatch_kernel(run_len_ref, run_off_ref, run_dst_ref, pad_lo_ref, pad_hi_ref, pos_ref, h_ref, xs_ref, stage, zero_scr,
                     sem, pad_sem, *, tm, rows_per_pass):
    step = pl.program_id(0)
    na = TOP_K * tm
    d = h_ref.shape[1]
    half = d // 2
    n_words = half // LANES

    def pad_copy(slot, n_slots):
        return pltpu.make_async_copy(zero_scr.at[pl.ds(0, n_slots * SUBLANES), :],
                                     xs_ref.at[_slot_rows(slot, n_slots), :], pad_sem)

    def for_each_pad(fn):
        def per_expert(ex, carry):
            slot = pad_lo_ref[ex]
            n = pad_hi_ref[ex] - slot
            for size in _run_sizes(EXPERT_BLOCK - 1):
                take = (n & size) != 0

                @pl.when(take)
                def _(slot=slot, size=size):
                    fn(pad_copy(slot, size))

                slot = slot + jnp.where(take, size, 0)
            return carry
        lax.fori_loop(0, N_EXPERTS, per_expert, 0)

    @pl.when(step == 0)
    def _():
        zero_scr[...] = jnp.zeros_like(zero_scr)
        for_each_pad(lambda cp: cp.start())

    buf = step % 2
    pos = pos_ref[...]
    h = h_ref[...]
    for a0 in range(0, na, rows_per_pass):
        slot_id = a0 + _iota2((rows_per_pass, tm), 0)
        hit = pos[0:1, :] == slot_id
        for k in range(1, TOP_K):
            hit = jnp.logical_or(hit, pos[k:k + 1, :] == slot_id)
        rows = _dot(hit.astype(BF16), h)
        for i in range(n_words):
            word = _pack_halves(rows[:, i * LANES:(i + 1) * LANES], rows[:, half + i * LANES:half + (i + 1) * LANES])
            stage[buf, pl.ds(a0 * SUBLANES + i, rows_per_pass, stride=SUBLANES), :] = word

    def run_copy(tile_slot, sorted_slot, n_slots):
        return pltpu.make_async_copy(stage.at[buf, _slot_rows(tile_slot, n_slots), :],
                                     xs_ref.at[_slot_rows(sorted_slot, n_slots), :], sem.at[buf])

    def wait_tile(which):
        pltpu.make_async_copy(stage.at[which], xs_ref.at[pl.ds(0, na * SUBLANES), :], sem.at[which]).wait()

    _for_each_run(step, (run_len_ref, run_off_ref, run_dst_ref), tm, run_copy, lambda cp: cp.start())

    @pl.when(step > 0)
    def _():
        wait_tile(1 - buf)

    @pl.when(step == pl.num_programs(0) - 1)
    def _():
        wait_tile(buf)

    @pl.when(step == 0)
    def _():
        for_each_pad(lambda cp: cp.wait())


def _dispatch(runs, pad_lo, pad_hi, pos_t, h2, n_slots, tm):
    s, d = h2.shape
    assert (d // 2) % LANES == 0 and (d // 2) // LANES == SUBLANES, "one token row must pack into one (8, 128) tile"
    na = TOP_K * tm
    return pl.pallas_call(
        functools.partial(_dispatch_kernel, tm=tm, rows_per_pass=min(512, na)),
        grid_spec=pltpu.PrefetchScalarGridSpec(
            num_scalar_prefetch=5,
            grid=(s // tm,),
            in_specs=[pl.BlockSpec((TOP_K, tm), lambda i, *_: (0, i)),
                      pl.BlockSpec((tm, d), lambda i, *_: (i, 0))],
            out_specs=pl.BlockSpec(memory_space=pl.ANY),
            scratch_shapes=[pltpu.VMEM((2, na * SUBLANES, LANES), U32),
                            pltpu.VMEM((EXPERT_BLOCK // 2 * SUBLANES, LANES), U32),
                            pltpu.SemaphoreType.DMA((2,)), pltpu.SemaphoreType.DMA(())]),
        out_shape=jax.ShapeDtypeStruct((n_slots * SUBLANES, LANES), U32),
        compiler_params=_params(("arbitrary",), has_side_effects=True, disable_bounds_checks=True),
    )(*runs, pad_lo, pad_hi, pos_t, h2)


def _expert_kernel(be_ref, nu_ref, next_ref, par_ref, x_ref, wg_hbm, wu_hbm, wd_hbm, y_ref, wg_f32, wu_f32, wd_f32,
                   wg_scr, wu_scr, wd_scr, sem):
    b = pl.program_id(0)
    bm = EXPERT_BLOCK
    active = b < nu_ref[0]
    new_expert = jnp.logical_or(b == 0, be_ref[b] != be_ref[jnp.maximum(b - 1, 0)])

    def weight_copies(ex, which):
        return [pltpu.make_async_copy(src.at[ex], dst.at[which], sem.at[which])
                for src, dst in ((wg_hbm, wg_f32), (wu_hbm, wu_f32), (wd_hbm, wd_f32))]

    @pl.when(jnp.logical_and(active, new_expert))
    def _():
        which = par_ref[b]

        @pl.when(b == 0)
        def _():
            for cp in weight_copies(be_ref[b], which):
                cp.start()

        for cp in weight_copies(be_ref[b], which):
            cp.wait()
        wg_scr[...] = wg_f32[which].astype(BF16)
        wu_scr[...] = wu_f32[which].astype(BF16)
        wd_scr[...] = wd_f32[which].astype(BF16)
        nb = next_ref[b]

        @pl.when(nb < nu_ref[0])
        def _():
            for cp in weight_copies(be_ref[nb], 1 - which):
                cp.start()

    @pl.when(active)
    def _():
        los, his = [], []
        for i in range(SUBLANES):
            lo, hi = _unpack_halves(x_ref[pl.ds(i, bm, stride=SUBLANES), :])
            los.append(lo.astype(BF16))
            his.append(hi.astype(BF16))
        xb = jnp.concatenate(los + his, axis=1)
        hid = _silu(_dot(xb, wg_scr[...])) * _dot(xb, wu_scr[...])
        y = _dot(hid.astype(BF16), wd_scr[...])
        half = y.shape[1] // 2
        for i in range(SUBLANES):
            word = _pack_halves(_round_bf16(y[:, i * LANES:(i + 1) * LANES]),
                                _round_bf16(y[:, half + i * LANES:half + (i + 1) * LANES]))
            y_ref[pl.ds(i, bm, stride=SUBLANES), :] = word


def _experts(block_e, n_used, next_block, parity, xs, w_gate, w_up, w_down):
    d, ff = w_gate.shape[1], w_gate.shape[2]
    bm = EXPERT_BLOCK
    n_blocks = xs.shape[0] // (bm * SUBLANES)
    blk = lambda b, be, nu, *_: (jnp.minimum(b, nu[0] - 1), 0)
    hbm = pl.BlockSpec(memory_space=pl.ANY)
    return pl.pallas_call(
        _expert_kernel,
        grid_spec=pltpu.PrefetchScalarGridSpec(
            num_scalar_prefetch=4,
            grid=(n_blocks,),
            in_specs=[pl.BlockSpec((bm * SUBLANES, LANES), blk), hbm, hbm, hbm],
            out_specs=pl.BlockSpec((bm * SUBLANES, LANES), blk),
            scratch_shapes=[pltpu.VMEM((2, d, ff), F32), pltpu.VMEM((2, d, ff), F32), pltpu.VMEM((2, ff, d), F32),
                            pltpu.VMEM((d, ff), BF16), pltpu.VMEM((d, ff), BF16), pltpu.VMEM((ff, d), BF16),
                            pltpu.SemaphoreType.DMA((2,))]),
        out_shape=jax.ShapeDtypeStruct(xs.shape, U32),
        compiler_params=_params(("arbitrary",)),
    )(block_e, n_used, next_block, parity, xs, w_gate, w_up, w_down)


def _combine_kernel(run_len_ref, run_off_ref, run_src_ref, ys_ref, h_ref, x1_ref, pos_ref, wts_ref, wg_ref, wu_ref, wd_ref, gt_ref, gf_ref,
                    o_ref, stage, sem, *, tm, rows_per_pass):
    step = pl.program_id(0)
    na = TOP_K * tm
    buf = step % 2

    def fetch_tile(tile, which):
        def run_copy(tile_slot, sorted_slot, n_slots):
            return pltpu.make_async_copy(ys_ref.at[_slot_rows(sorted_slot, n_slots), :],
                                         stage.at[which, _slot_rows(tile_slot, n_slots), :], sem.at[which])
        _for_each_run(tile, (run_len_ref, run_off_ref, run_src_ref), tm, run_copy, lambda cp: cp.start())

    @pl.when(step == 0)
    def _():
        fetch_tile(step, buf)

    @pl.when(step + 1 < pl.num_programs(0))
    def _():
        fetch_tile(step + 1, 1 - buf)

    hb = h_ref[...]
    hid = _silu(_dot(hb, wg_ref[...])) * _dot(hb, wu_ref[...])
    acc = _dot(hid.astype(BF16), wd_ref[...])

    pltpu.make_async_copy(ys_ref.at[pl.ds(0, na * SUBLANES), :], stage.at[buf], sem.at[buf]).wait()

    pos = pos_ref[...]
    wts = wts_ref[...]
    for a0 in range(0, na, rows_per_pass):
        los, his = [], []
        for i in range(SUBLANES):
            lo, hi = _unpack_halves(stage[buf, pl.ds(a0 * SUBLANES + i, rows_per_pass, stride=SUBLANES), :])
            los.append(lo.astype(BF16))
            his.append(hi.astype(BF16))
        y_rows = jnp.concatenate(los + his, axis=1)
        slot_id = a0 + _iota2((rows_per_pass, tm), 0)
        wmat = jnp.zeros((rows_per_pass, tm), F32)
        for k in range(TOP_K):
            wmat = wmat + jnp.where(pos[k:k + 1, :] == slot_id, wts[k:k + 1, :], 0.0)
        acc = acc + _dg(wmat.astype(BF16), y_rows, TN)
    x2 = x1_ref[...] + gt_ref[...] * acc
    o_ref[...] = _rms(x2, NORM_EPS) * gf_ref[...]


def _combine(runs, ys, h2, x1, pos_t, wts_t, w_gate, w_up, w_down, gt2, gf, tm):
    s, d = x1.shape
    ff = w_gate.shape[1]
    na = TOP_K * tm
    const = lambda shape: pl.BlockSpec(shape, lambda i, *_: (0, 0), pipeline_mode=pl.Buffered(1))
    tile = lambda cols: pl.BlockSpec((tm, cols), lambda i, *_: (i, 0))
    per_k = pl.BlockSpec((TOP_K, tm), lambda i, *_: (0, i))
    return pl.pallas_call(
        functools.partial(_combine_kernel, tm=tm, rows_per_pass=min(512, na)),
        grid_spec=pltpu.PrefetchScalarGridSpec(
            num_scalar_prefetch=3,
            grid=(s // tm,),
            in_specs=[pl.BlockSpec(memory_space=pl.ANY),
                      tile(d), tile(d), per_k, per_k,
                      const((d, ff)), const((d, ff)), const((ff, d)), const((1, d)), const((1, d))],
            out_specs=tile(d),
            scratch_shapes=[pltpu.VMEM((2, na * SUBLANES, LANES), U32), pltpu.SemaphoreType.DMA((2,))]),
        out_shape=jax.ShapeDtypeStruct((s, d), F32),
        compiler_params=_params(("arbitrary",), disable_bounds_checks=True),
    )(*runs, ys, h2, x1, pos_t, wts_t, w_gate, w_up, w_down, gt2, gf)


def _mixer(x2d, mod, norm1_g, norm2_g, w_in, lb, hgrn_onorm_g, gdn_conv_w, gdn_a_log, gdn_dt_bias, gdn_onorm_g,
           w_branch_hgrn, w_branch_gdn, w_out, tiles):
    d = x2d.shape[1]
    sh1, sc1, gt1, sh2, sc2, _ = [mod[:, i * d:(i + 1) * d] for i in range(6)]
    key = HEADS * HEAD_DIM
    small0 = 4 * key + 3 * key
    small1 = small0 + 2 * HEADS
    w_bf = w_in.astype(BF16)
    w_main = jnp.concatenate([w_bf[:, :small0], w_bf[:, small1:]], axis=1)
    w_small_t = w_bf[:, small0:small1].T
    proj, ab_t = _inproj(x2d, norm1_g, sc1, sh1, w_main, w_small_t, tiles["in_tm"], tiles["in_tn"])
    o_a = _hgrn(proj, lb, hgrn_onorm_g, tiles["mix_ts"])
    u, wqd, ku, attn, dl = _gdn_prep(proj, gdn_conv_w, ab_t, gdn_a_log, gdn_dt_bias, tiles["prep_ts"])
    o_b = _gdn_scan(u, wqd, ku, attn, dl, proj, gdn_onorm_g, tiles["mix_ts"])
    return _merge(o_a, o_b, proj, x2d, w_branch_hgrn.astype(BF16), w_branch_gdn.astype(BF16),
                  w_out.astype(BF16), gt1, norm2_g, sc2, sh2, tiles["merge_tm"])


def _moe(x1, h2, mod, norm2_g, normf_g, w_router, router_bias, w_exp_gate, w_exp_up, w_exp_down, w_sh_gate,
         w_sh_up, w_sh_down, tiles):
    s, d = x1.shape
    tm = tiles["moe_tm"]
    sh2, sc2, gt2 = [mod[:, i * d:(i + 1) * d] for i in (3, 4, 5)]
    pos_t, wts_t, before, ntile = _router(x1, norm2_g, sc2, sh2, w_router.T, router_bias.reshape(-1, 1), tm)
    bm = EXPERT_BLOCK
    n_blocks = -(-(s * TOP_K + N_EXPERTS * (bm - 1)) // bm)
    before = before[:, :, 0]
    ntile = ntile[:, :, 0]
    counts = before[-1] + ntile[-1]
    padded = (counts + bm - 1) // bm * bm
    pend = jnp.cumsum(padded).astype(I32)
    pstart = pend - padded
    block_start = jnp.arange(n_blocks, dtype=I32) * bm
    block_e = jnp.minimum(jnp.sum(pend[None, :] <= block_start[:, None], axis=1), N_EXPERTS - 1).astype(I32)
    n_used = pend[-1:] // bm
    run_off = (jnp.cumsum(ntile, axis=1) - ntile).astype(I32)
    runs = (ntile.reshape(-1), run_off.reshape(-1), (before + pstart[None, :]).reshape(-1))
    xs = _dispatch(runs, pstart + counts, pend, pos_t, h2, n_blocks * bm, tm)
    next_block = pend[block_e] // bm
    switches = jnp.concatenate([jnp.zeros((1,), I32), (block_e[1:] != block_e[:-1]).astype(I32)])
    parity = jnp.cumsum(switches).astype(I32) % 2
    ys = _experts(block_e, n_used, next_block, parity, xs, w_exp_gate, w_exp_up, w_exp_down)
    return _combine(runs, ys, h2, x1, pos_t, wts_t, w_sh_gate.astype(BF16), w_sh_up.astype(BF16),
                    w_sh_down.astype(BF16), gt2, normf_g, tm)


def _tiles(s):
    pick = lambda want: min(want, s)
    return dict(in_tm=pick(1024), in_tn=1536, mix_ts=pick(512), prep_ts=pick(2048), merge_tm=pick(512),
                moe_tm=pick(256))


def kernel(x, c, w_ada, b_ada, norm1_g, norm2_g, w_in, hgrn_lb_table, hgrn_onorm_g, gdn_conv_w, gdn_a_log, gdn_dt_bias, gdn_onorm_g, w_branch_hgrn, w_branch_gdn, w_out, w_router, router_bias, w_exp_gate, w_exp_up, w_exp_down, w_sh_gate, w_sh_up, w_sh_down, normf_g):
    b, s, d = x.shape
    assert b == 1 and w_ada.shape[0] == 1, "one sequence, one layer"
    tiles = _tiles(s)
    lb = jnp.cumsum(jax.nn.softmax(hgrn_lb_table.astype(F32), axis=0), axis=0)[0:1]
    mod = _ada(c, w_ada[0], b_ada[0])
    row = lambda v: v.reshape(1, -1)
    x1, h2 = _mixer(x[0], mod, row(norm1_g[0]), row(norm2_g[0]), w_in[0], lb, row(hgrn_onorm_g[0]), gdn_conv_w[0],
                    gdn_a_log[0], gdn_dt_bias[0], row(gdn_onorm_g[0]), w_branch_hgrn[0], w_branch_gdn[0], w_out[0],
                    tiles)
    out = _moe(x1, h2, mod, row(norm2_g[0]), row(normf_g), w_router[0], router_bias[0], w_exp_gate[0],
               w_exp_up[0], w_exp_down[0], w_sh_gate[0], w_sh_up[0], w_sh_down[0], tiles)
    return out[None]
```

```python
import functools

import jax
import jax.numpy as jnp
from jax import lax
from jax.experimental import pallas as pl
from jax.experimental.pallas import tpu as pltpu

F32 = jnp.float32
BF16 = jnp.bfloat16
I32 = jnp.int32
U32 = jnp.uint32

NORM_EPS = 1e-6
L2_EPS = 1e-6
HEADS = 8
HEAD_DIM = 128
CONV_WIDTH = 4
CHUNK = 64
N_EXPERTS = 64
N_GROUPS = 8
GROUP_SIZE = N_EXPERTS // N_GROUPS
TOPK_GROUPS = 4
TOP_K = 8
ROUTED_SCALE = 2.5
EXPERT_BLOCK = 512

LANES = 128
SUBLANES = 8
VMEM_LIMIT = 56 * 1024 * 1024

NT = (((1,), (1,)), ((), ()))
TN = (((0,), (0,)), ((), ()))


def _params(sem, **kw):
    return pltpu.CompilerParams(dimension_semantics=sem, vmem_limit_bytes=VMEM_LIMIT, **kw)


def _dot(a, b):
    return jnp.dot(a, b, preferred_element_type=F32)


def _dg(a, b, dims):
    return lax.dot_general(a, b, dims, preferred_element_type=F32)


def _split(x):
    hi = x.astype(BF16)
    lo = (x - hi.astype(F32)).astype(BF16)
    return hi, lo


def _dot_exact_lhs(a_bf16, x, dims=None):
    hi, lo = _split(x)
    if dims is None:
        return _dot(a_bf16, hi) + _dot(a_bf16, lo)
    return _dg(a_bf16, hi, dims) + _dg(a_bf16, lo, dims)


def _sigmoid(x):
    return 1.0 / (1.0 + jnp.exp(-x))


def _silu(x):
    return x * _sigmoid(x)


def _rms(x, eps):
    return x * lax.rsqrt(jnp.mean(x * x, axis=-1, keepdims=True) + eps)


def _iota2(shape, dim):
    return lax.broadcasted_iota(I32, shape, dim)


def _pack_halves(lo, hi):
    lo_bits = lax.shift_right_logical(pltpu.bitcast(lo, U32), U32(16))
    hi_bits = pltpu.bitcast(hi, U32) & U32(0xFFFF0000)
    return lo_bits | hi_bits


def _unpack_halves(word):
    lo = pltpu.bitcast(lax.shift_left(word, U32(16)), F32)
    hi = pltpu.bitcast(word & U32(0xFFFF0000), F32)
    return lo, hi


def _round_bf16(x):
    return x.astype(BF16).astype(F32)


def _ada_kernel(c_ref, w_ref, b_ref, o_ref):
    cond = _silu(c_ref[...])
    o_ref[...] = jnp.dot(cond, w_ref[...], preferred_element_type=F32,
                         precision=lax.Precision.HIGHEST) + b_ref[...]


def _ada(c, w_ada, b_ada):
    d, n = w_ada.shape
    tn = 1024
    c8 = jnp.broadcast_to(c, (SUBLANES, d))
    out = pl.pallas_call(
        _ada_kernel,
        grid=(n // tn,),
        in_specs=[pl.BlockSpec((SUBLANES, d), lambda j: (0, 0)),
                  pl.BlockSpec((d, tn), lambda j: (0, j)),
                  pl.BlockSpec((1, tn), lambda j: (0, j))],
        out_specs=pl.BlockSpec((SUBLANES, tn), lambda j: (0, j)),
        out_shape=jax.ShapeDtypeStruct((SUBLANES, n), F32),
        compiler_params=_params(("arbitrary",)),
    )(c8, w_ada, b_ada.reshape(1, n))
    return out[0:1]


def _wprep_kernel(a_ref, b_ref, o_ref, *, first_shifted, shift):
    j = pl.program_id(0)

    @pl.when(j < first_shifted)
    def _():
        o_ref[...] = a_ref[...].astype(BF16)

    @pl.when(j >= first_shifted)
    def _():
        tn = a_ref.shape[1]
        both = jnp.concatenate([a_ref[...], b_ref[...]], axis=1)
        o_ref[...] = pltpu.roll(both, both.shape[1] - shift, axis=1)[:, 0:tn].astype(BF16)


def _wprep(w_in, cut0, cut1, tn):
    d, n_in = w_in.shape
    n_out = n_in - (cut1 - cut0)
    assert cut0 % tn == 0 and n_out % tn == 0 and tn % LANES == 0 and 0 < cut1 - cut0 < LANES
    return pl.pallas_call(
        functools.partial(_wprep_kernel, first_shifted=cut0 // tn, shift=cut1 - cut0),
        grid=(n_out // tn,),
        in_specs=[pl.BlockSpec((d, tn), lambda j: (0, j)),
                  pl.BlockSpec((d, LANES), lambda j: (0, (j + 1) * (tn // LANES)))],
        out_specs=pl.BlockSpec((d, tn), lambda j: (0, j)),
        out_shape=jax.ShapeDtypeStruct((d, n_out), BF16),
        compiler_params=_params(("arbitrary",)),
    )(w_in, w_in)


def _inproj_kernel(x_ref, g_ref, sc_ref, sh_ref, w_ref, wst_ref, proj_ref, smallt_ref, h_scr):
    @pl.when(pl.program_id(1) == 0)
    def _():
        h = _rms(x_ref[...], NORM_EPS) * g_ref[...] * (1.0 + sc_ref[...]) + sh_ref[...]
        hb = h.astype(BF16)
        h_scr[...] = hb
        smallt_ref[...] = _dg(wst_ref[...], hb, NT)

    proj_ref[...] = _dot(h_scr[...], w_ref[...]).astype(BF16)


def _inproj(x, g, sc, sh, w_main, w_small_t, tm, tn):
    s, d = x.shape
    n = w_main.shape[1]
    ns = w_small_t.shape[0]
    row = lambda i, j: (0, 0)
    return pl.pallas_call(
        _inproj_kernel,
        grid=(s // tm, n // tn),
        in_specs=[pl.BlockSpec((tm, d), lambda i, j: (i, 0)),
                  pl.BlockSpec((1, d), row), pl.BlockSpec((1, d), row), pl.BlockSpec((1, d), row),
                  pl.BlockSpec((d, tn), lambda i, j: (0, j)),
                  pl.BlockSpec((ns, d), row)],
        out_specs=[pl.BlockSpec((tm, tn), lambda i, j: (i, j)),
                   pl.BlockSpec((ns, tm), lambda i, j: (0, i))],
        out_shape=[jax.ShapeDtypeStruct((s, n), BF16), jax.ShapeDtypeStruct((ns, s), F32)],
        scratch_shapes=[pltpu.VMEM((tm, d), BF16)],
        compiler_params=_params(("arbitrary", "arbitrary")),
    )(x, g, sc, sh, w_main, w_small_t)


def _hgrn_kernel(q_ref, f_ref, i_ref, g_ref, lb_ref, on_ref, o_ref, st_scr, *, n_chunks):
    @pl.when(pl.program_id(0) == 0)
    def _():
        st_scr[...] = jnp.zeros_like(st_scr)

    c = CHUNK
    hd = HEAD_DIM
    causal = _iota2((c, c), 1) <= _iota2((c, c), 0)
    tri = causal.astype(BF16)
    lb = lb_ref[...]
    on_g = on_ref[...]
    heads = [slice(h * hd, (h + 1) * hd) for h in range(HEADS)]

    def chunk(n, carry):
        rows = pl.ds(pl.multiple_of(n * c, c), c)
        f = lb + (1.0 - lb) * _sigmoid(f_ref[rows, :].astype(F32))
        b = _dot_exact_lhs(tri, jnp.log(f))
        k = 1.0 - f
        q = _silu(q_ref[rows, :].astype(F32)) * (hd ** -0.5)
        v = i_ref[rows, :]
        b_mid = b[c // 2:c // 2 + 1, :]
        b_last = b[c - 1:c, :]
        qa = (q * jnp.exp(b - b_mid)).astype(BF16)
        ka = (k * jnp.exp(b_mid - b)).astype(BF16)
        qi = (q * jnp.exp(b)).astype(BF16)
        ku = (k * jnp.exp(b_last - b)).astype(BF16)
        dec = jnp.exp(b_last)
        gate = on_g * _silu(g_ref[rows, :].astype(F32))
        sts = [st_scr[h] for h in range(HEADS)]
        scores = [jnp.where(causal, _dg(qa[:, sl], ka[:, sl], NT), 0.0).astype(BF16) for sl in heads]
        inter = [_dg(qi[:, sl], st.astype(BF16), NT) for sl, st in zip(heads, sts)]
        kv = [_dg(v[:, sl], ku[:, sl], TN) for sl in heads]
        for h, sl in enumerate(heads):
            st_scr[h] = dec[:, sl] * sts[h] + kv[h]
        outs = [_rms(_dot(sc, v[:, sl]) + it, NORM_EPS) for sc, sl, it in zip(scores, heads, inter)]
        o_ref[rows, :] = (jnp.concatenate(outs, axis=1) * gate).astype(BF16)
        return carry

    lax.fori_loop(0, n_chunks, chunk, 0)


def _hgrn(proj, lb, onorm_g, ts):
    s = proj.shape[0]
    width = HEADS * HEAD_DIM
    col = lambda blk: pl.BlockSpec((ts, width), lambda j, blk=blk: (j, blk))
    const = pl.BlockSpec((1, width), lambda j: (0, 0))
    return pl.pallas_call(
        functools.partial(_hgrn_kernel, n_chunks=ts // CHUNK),
        grid=(s // ts,),
        in_specs=[col(0), col(1), col(2), col(3), const, const],
        out_specs=pl.BlockSpec((ts, width), lambda j: (j, 0)),
        out_shape=jax.ShapeDtypeStruct((s, width), BF16),
        scratch_shapes=[pltpu.VMEM((HEADS, HEAD_DIM, HEAD_DIM), F32)],
        compiler_params=_params(("arbitrary",)),
    )(proj, proj, proj, proj, lb, jnp.tile(onorm_g, (1, HEADS)))


def _gdn_prep_kernel(q_ref, k_ref, v_ref, qp_ref, kp_ref, vp_ref, wq_ref, wk_ref, wv_ref, ab_ref, alog_ref,
                     dtb_ref, tri_ref, eye_ref, u_ref, wqd_ref, ku_ref, attn_ref, dl_ref, cat_scr, rows_scr, cols_scr,
                     *, n_chunks, ts):
    h = pl.program_id(1)
    first = pl.program_id(0) == 0
    c = CHUNK
    hd = HEAD_DIM

    def conv_silu(cur_ref, prev_ref, w_ref):
        cat_scr[0:8, :] = jnp.where(first, 0.0, prev_ref[...].astype(F32))
        cat_scr[8:8 + ts, :] = cur_ref[...].astype(F32)
        acc = None
        for j in range(CONV_WIDTH):
            off = 8 - (CONV_WIDTH - 1) + j
            term = cat_scr[off:off + ts, :] * w_ref[j:j + 1, :]
            acc = term if acc is None else acc + term
        return _silu(acc)

    def l2n(x):
        return x * lax.rsqrt(jnp.sum(x * x, axis=-1, keepdims=True) + L2_EPS)

    q_all = l2n(conv_silu(q_ref, qp_ref, wq_ref)) * (hd ** -0.5)
    k_all = l2n(conv_silu(k_ref, kp_ref, wk_ref))
    v_all = conv_silu(v_ref, vp_ref, wv_ref)

    @pl.when(h == 0)
    def _():
        z = ab_ref[0:HEADS, :] + dtb_ref[...]
        softplus = jnp.maximum(z, 0.0) + jnp.log(1.0 + jnp.exp(-jnp.abs(z)))
        ld_rows = -jnp.exp(alog_ref[...]) * softplus
        hi, lo = _split(ld_rows)
        tri_blocks = tri_ref[...]
        g_rows = _dg(hi, tri_blocks, NT) + _dg(lo, tri_blocks, NT)
        beta_rows = _sigmoid(ab_ref[HEADS:2 * HEADS, :])
        rows_scr[...] = g_rows
        rows = jnp.concatenate([g_rows, beta_rows, jnp.zeros((LANES - 2 * HEADS, ts), F32)], axis=0)
        r_hi, r_lo = _split(rows)
        r_lo2 = (rows - r_hi.astype(F32) - r_lo.astype(F32)).astype(BF16)
        eye_ts = eye_ref[...]
        cols_scr[...] = _dg(eye_ts, r_hi, NT) + _dg(eye_ts, r_lo, NT) + _dg(eye_ts, r_lo2, NT)

    lane = _iota2((ts, LANES), 1)
    cols = cols_scr[...]
    gc_all = jnp.sum(jnp.where(lane == h, cols, 0.0), axis=1, keepdims=True)
    bc_all = jnp.sum(jnp.where(lane == h + HEADS, cols, 0.0), axis=1, keepdims=True)
    g_row = rows_scr[pl.ds(h, 1), :]
    egc_all = jnp.exp(gc_all)

    r = _iota2((c, c), 0)
    cidx = _iota2((c, c), 1)
    causal = cidx <= r
    strict = cidx < r
    eye_f = (r == cidx).astype(F32)
    chunks = [slice(n * c, (n + 1) * c) for n in range(n_chunks)]

    q16 = q_all.astype(BF16)
    k16 = k_all.astype(BF16)
    kq = [_dg(jnp.concatenate([k16[sl], q16[sl]], axis=0), k16[sl], NT) for sl in chunks]
    dm = []
    for sl in chunks:
        diff = gc_all[sl] - g_row[:, sl]
        dm.append(jnp.where(causal, jnp.exp(jnp.where(causal, diff, 0.0)), 0.0))
    bm = [-jnp.where(strict, bc_all[sl] * x[0:c] * d, 0.0) for sl, x, d in zip(chunks, kq, dm)]
    p = [eye_f + b for b in bm]
    bm = [_dot(b.astype(BF16), b.astype(BF16)) for b in bm]
    for _ in range(c.bit_length() - 3):
        res = [_dot(b.astype(BF16), jnp.concatenate([b, pp], axis=1).astype(BF16)) for b, pp in zip(bm, p)]
        p = [pp + x[:, c:2 * c] for pp, x in zip(p, res)]
        bm = [x[:, 0:c] for x in res]
    p = [pp + _dot(b.astype(BF16), pp.astype(BF16)) for b, pp in zip(bm, p)]
    rhs = jnp.concatenate([v_all * bc_all, k_all * (bc_all * egc_all)], axis=1).astype(BF16)
    sol = [_dot(pp.astype(BF16), rhs[sl]) for pp, sl in zip(p, chunks)]
    qd_all = (q_all * egc_all).astype(BF16)
    for n, sl in enumerate(chunks):
        g_last = gc_all[(n + 1) * c - 1:(n + 1) * c, :]
        u_ref[sl, :] = sol[n][:, 0:hd].astype(BF16)
        wqd_ref[2 * n * c:(2 * n + 1) * c, :] = sol[n][:, hd:2 * hd].astype(BF16)
        wqd_ref[(2 * n + 1) * c:(2 * n + 2) * c, :] = qd_all[sl]
        ku_ref[sl, :] = (k_all[sl] * jnp.exp(g_last - gc_all[sl])).astype(BF16)
        attn_ref[sl, :] = (kq[n][c:2 * c] * dm[n]).astype(BF16)
        dl_ref[n:n + 1, :] = jnp.broadcast_to(jnp.exp(g_last), (1, hd))


def _gdn_prep(proj, conv_w, ab_t, a_log, dt_bias, ts):
    s = proj.shape[0]
    hd = HEAD_DIM
    c = CHUNK
    q0 = 4 * HEADS
    cur = lambda off: pl.BlockSpec((ts, hd), lambda j, h, off=off: (j, off + h))
    prev = lambda off: pl.BlockSpec((8, hd), lambda j, h, off=off: (jnp.maximum(j * (ts // 8) - 1, 0), off + h))
    cw = lambda off: pl.BlockSpec((CONV_WIDTH, hd), lambda j, h, off=off: (0, off + h))
    per_head_scalar = pl.BlockSpec((HEADS, 1), lambda j, h: (0, 0))
    const = pl.BlockSpec((ts, ts), lambda j, h: (0, 0), pipeline_mode=pl.Buffered(1))
    pos = jnp.arange(ts)
    tri_blocks = ((pos[:, None] // c == pos[None, :] // c) & (pos[None, :] <= pos[:, None])).astype(BF16)
    eye = (pos[:, None] == pos[None, :]).astype(BF16)
    per_head = lambda rows, cols: pl.BlockSpec((None, rows, cols), lambda j, h: (h, j, 0))
    return pl.pallas_call(
        functools.partial(_gdn_prep_kernel, n_chunks=ts // c, ts=ts),
        grid=(s // ts, HEADS),
        in_specs=[cur(q0), cur(q0 + HEADS), cur(q0 + 2 * HEADS),
                  prev(q0), prev(q0 + HEADS), prev(q0 + 2 * HEADS),
                  cw(0), cw(HEADS), cw(2 * HEADS),
                  pl.BlockSpec((2 * HEADS, ts), lambda j, h: (0, j)),
                  per_head_scalar, per_head_scalar, const, const],
        out_specs=[pl.BlockSpec((ts, hd), lambda j, h: (j, h)),
                   pl.BlockSpec((2 * ts, hd), lambda j, h: (j, h)),
                   pl.BlockSpec((ts, hd), lambda j, h: (j, h)),
                   per_head(ts, c),
                   per_head(ts // c, hd)],
        out_shape=[jax.ShapeDtypeStruct((s, HEADS * hd), BF16),
                   jax.ShapeDtypeStruct((2 * s, HEADS * hd), BF16),
                   jax.ShapeDtypeStruct((s, HEADS * hd), BF16),
                   jax.ShapeDtypeStruct((HEADS, s, c), BF16),
                   jax.ShapeDtypeStruct((HEADS, s // c, hd), F32)],
        scratch_shapes=[pltpu.VMEM((ts + 8, hd), F32), pltpu.VMEM((HEADS, ts), F32), pltpu.VMEM((ts, LANES), F32)],
        compiler_params=_params(("arbitrary", "arbitrary")),
    )(proj, proj, proj, proj, proj, proj, conv_w, conv_w, conv_w,
      ab_t, a_log.reshape(HEADS, 1), dt_bias.reshape(HEADS, 1), tri_blocks, eye)


def _gdn_scan_kernel(u_ref, wqd_ref, ku_ref, attn_ref, dl_ref, g_ref, on_ref, o_ref, st_scr, *, n_chunks):
    @pl.when(pl.program_id(0) == 0)
    def _():
        st_scr[...] = jnp.zeros_like(st_scr)

    c = CHUNK
    hd = HEAD_DIM
    on_g = on_ref[...]
    heads = [slice(h * hd, (h + 1) * hd) for h in range(HEADS)]

    def chunk(n, carry):
        rows = pl.ds(pl.multiple_of(n * c, c), c)
        rows2 = pl.ds(pl.multiple_of(2 * n * c, 2 * c), 2 * c)
        sts = [st_scr[h] for h in range(HEADS)]
        wq = [_dot(wqd_ref[rows2, sl], st.astype(BF16)) for sl, st in zip(heads, sts)]
        vn = [(u_ref[rows, sl].astype(F32) - x[0:c]).astype(BF16) for sl, x in zip(heads, wq)]
        upd = [_dg(ku_ref[rows, sl], v, TN) for sl, v in zip(heads, vn)]
        for h in range(HEADS):
            st_scr[h] = dl_ref[h, pl.ds(n, 1), :] * sts[h] + upd[h]
        outs = [_rms(x[c:2 * c] + _dot(attn_ref[h, rows, :], v), NORM_EPS)
                for h, (x, v) in enumerate(zip(wq, vn))]
        gate = jnp.tile(on_g, (1, HEADS)) * _silu(g_ref[rows, :].astype(F32))
        o_ref[rows, :] = (jnp.concatenate(outs, axis=1) * gate).astype(BF16)
        return carry

    lax.fori_loop(0, n_chunks, chunk, 0)


def _gdn_scan(u, wqd, ku, attn, dl, proj, onorm_g, ts):
    s, width = u.shape
    c = CHUNK
    gate_blk = (4 * HEADS + 3 * HEADS) * HEAD_DIM // width
    return pl.pallas_call(
        functools.partial(_gdn_scan_kernel, n_chunks=ts // c),
        grid=(s // ts,),
        in_specs=[pl.BlockSpec((ts, width), lambda j: (j, 0)),
                  pl.BlockSpec((2 * ts, width), lambda j: (j, 0)),
                  pl.BlockSpec((ts, width), lambda j: (j, 0)),
                  pl.BlockSpec((HEADS, ts, c), lambda j: (0, j, 0)),
                  pl.BlockSpec((HEADS, ts // c, HEAD_DIM), lambda j: (0, j, 0)),
                  pl.BlockSpec((ts, width), lambda j: (j, gate_blk)),
                  pl.BlockSpec((1, HEAD_DIM), lambda j: (0, 0))],
        out_specs=pl.BlockSpec((ts, width), lambda j: (j, 0)),
        out_shape=jax.ShapeDtypeStruct((s, width), BF16),
        scratch_shapes=[pltpu.VMEM((HEADS, HEAD_DIM, HEAD_DIM), F32)],
        compiler_params=_params(("arbitrary",)),
    )(u, wqd, ku, attn, dl, proj, onorm_g)


def _merge_kernel(oa_ref, ob_ref, mga_ref, mgb_ref, x_ref, wa_ref, wb_ref, wo_ref, gt_ref, g2_ref, sc_ref,
                  sh_ref, x1_ref, h2_ref):
    ya = _dot(oa_ref[...], wa_ref[...])
    yb = _dot(ob_ref[...], wb_ref[...])
    merged = _sigmoid(mga_ref[...].astype(F32)) * ya + _sigmoid(mgb_ref[...].astype(F32)) * yb
    x1 = x_ref[...] + gt_ref[...] * _dot(merged.astype(BF16), wo_ref[...])
    x1_ref[...] = x1
    h2 = _rms(x1, NORM_EPS) * g2_ref[...] * (1.0 + sc_ref[...]) + sh_ref[...]
    h2_ref[...] = h2.astype(BF16)


def _merge(o_a, o_b, proj, x, w_a, w_b, w_o, gt1, g2, sc2, sh2, tm):
    s, d = x.shape
    dv = o_a.shape[1]
    mg0 = (8 * HEADS * HEAD_DIM) // d
    const = lambda shape: pl.BlockSpec(shape, lambda i: (0, 0), pipeline_mode=pl.Buffered(1))
    return pl.pallas_call(
        _merge_kernel,
        grid=(s // tm,),
        in_specs=[pl.BlockSpec((tm, dv), lambda i: (i, 0)),
                  pl.BlockSpec((tm, dv), lambda i: (i, 0)),
                  pl.BlockSpec((tm, d), lambda i: (i, mg0)),
                  pl.BlockSpec((tm, d), lambda i: (i, mg0 + 1)),
                  pl.BlockSpec((tm, d), lambda i: (i, 0)),
                  const((dv, d)), const((dv, d)), const((d, d)),
                  const((1, d)), const((1, d)), const((1, d)), const((1, d))],
        out_specs=[pl.BlockSpec((tm, d), lambda i: (i, 0)), pl.BlockSpec((tm, d), lambda i: (i, 0))],
        out_shape=[jax.ShapeDtypeStruct((s, d), F32), jax.ShapeDtypeStruct((s, d), BF16)],
        compiler_params=_params(("arbitrary",)),
    )(o_a, o_b, proj, proj, x, w_a, w_b, w_o, gt1, g2, sc2, sh2)


def _first_max(vals, iota, size, axis):
    m = jnp.max(vals, axis=axis, keepdims=True)
    idx = jnp.min(jnp.where(vals == m, iota, size), axis=axis, keepdims=True)
    return m, idx


def _router_kernel(x1_ref, g2_ref, sc_ref, sh_ref, wrt_ref, bias_ref, upper_ref, pos_ref, wts_ref, before_ref,
                   ntile_ref, cnt_scr, *, tm):
    @pl.when(pl.program_id(0) == 0)
    def _():
        cnt_scr[...] = jnp.zeros_like(cnt_scr)

    e = N_EXPERTS
    h2 = _rms(x1_ref[...], NORM_EPS) * g2_ref[...] * (1.0 + sc_ref[...]) + sh_ref[...]
    logits = lax.dot_general(wrt_ref[...], h2, NT, preferred_element_type=F32,
                             precision=lax.Precision.HIGHEST)
    scores = _sigmoid(logits)
    biased = scores + bias_ref[...]
    neg = -jnp.inf

    g3 = biased.reshape(N_GROUPS, GROUP_SIZE, tm)
    i3 = lax.broadcasted_iota(I32, g3.shape, 1)
    m1, a1 = _first_max(g3, i3, GROUP_SIZE, 1)
    m2 = jnp.max(jnp.where(i3 == a1, neg, g3), axis=1, keepdims=True)
    gs = (m1 + m2).reshape(N_GROUPS, tm)
    ig = _iota2(gs.shape, 0)
    gmask = jnp.zeros(gs.shape, jnp.bool_)
    for _ in range(TOPK_GROUPS):
        _, a = _first_max(gs, ig, N_GROUPS, 0)
        pick = ig == a
        gmask = jnp.logical_or(gmask, pick)
        gs = jnp.where(pick, neg, gs)
    emask = jnp.broadcast_to(gmask.reshape(N_GROUPS, 1, tm), (N_GROUPS, GROUP_SIZE, tm)).reshape(e, tm)

    cand = jnp.where(emask, biased, neg)
    ie = _iota2((e, tm), 0)
    sel_all = jnp.zeros((e, tm), jnp.bool_)
    w_rows, picks = [], []
    for _ in range(TOP_K):
        _, a = _first_max(cand, ie, e, 0)
        pick = ie == a
        picks.append(pick)
        w_rows.append(jnp.sum(jnp.where(pick, scores, 0.0), axis=0, keepdims=True))
        sel_all = jnp.logical_or(sel_all, pick)
        cand = jnp.where(pick, neg, cand)
    w_sum = w_rows[0]
    for wr in w_rows[1:]:
        w_sum = w_sum + wr
    wts = jnp.concatenate(w_rows, axis=0) / w_sum * ROUTED_SCALE

    sel = sel_all.astype(BF16)
    in_expert = _dot(sel, upper_ref[...])
    n_tile = jnp.sum(sel_all.astype(F32), axis=1, keepdims=True)
    lower = (_iota2((e, e), 1) < _iota2((e, e), 0)).astype(BF16)
    expert_off = _dot_exact_lhs(lower, jnp.broadcast_to(n_tile, (e, LANES)))[:, 0:1]
    place = in_expert + expert_off
    pos = jnp.concatenate([jnp.sum(jnp.where(pk, place, 0.0), axis=0, keepdims=True) for pk in picks], axis=0)
    pos_ref[...] = pos.astype(I32)
    before_ref[...] = jnp.broadcast_to(cnt_scr[...], before_ref.shape).astype(I32)
    ntile_ref[...] = jnp.broadcast_to(n_tile, ntile_ref.shape).astype(I32)
    cnt_scr[...] = cnt_scr[...] + n_tile
    wts_ref[...] = wts


def _router(x1, g2, sc2, sh2, w_router_t, bias_col, tm):
    s, d = x1.shape
    e = N_EXPERTS
    nt = s // tm
    upper = (jnp.arange(tm)[:, None] < jnp.arange(tm)[None, :]).astype(BF16)
    const = lambda shape: pl.BlockSpec(shape, lambda i: (0, 0))
    per_tile = pl.BlockSpec((None, e, LANES), lambda i: (i, 0, 0))
    return pl.pallas_call(
        functools.partial(_router_kernel, tm=tm),
        grid=(nt,),
        in_specs=[pl.BlockSpec((tm, d), lambda i: (i, 0)),
                  const((1, d)), const((1, d)), const((1, d)),
                  const((e, d)), const((e, 1)), const((tm, tm))],
        out_specs=[pl.BlockSpec((TOP_K, tm), lambda i: (0, i)),
                   pl.BlockSpec((TOP_K, tm), lambda i: (0, i)),
                   per_tile, per_tile],
        out_shape=[jax.ShapeDtypeStruct((TOP_K, s), I32), jax.ShapeDtypeStruct((TOP_K, s), F32),
                   jax.ShapeDtypeStruct((nt, e, LANES), I32), jax.ShapeDtypeStruct((nt, e, LANES), I32)],
        scratch_shapes=[pltpu.VMEM((e, 1), F32)],
        compiler_params=_params(("arbitrary",)),
    )(x1, g2, sc2, sh2, w_router_t, bias_col, upper)


def _run_sizes(limit):
    return [1 << b for b in range(limit.bit_length() - 1, -1, -1)]


def _for_each_run(tile, run_refs, tm, make_copy, fn):
    run_len_ref, run_off_ref, run_dst_ref = run_refs

    def per_expert(ex, carry):
        n = run_len_ref[tile * N_EXPERTS + ex]
        off = run_off_ref[tile * N_EXPERTS + ex]
        dst = run_dst_ref[tile * N_EXPERTS + ex]
        for size in _run_sizes(tm):
            done = n & (-2 * size)

            @pl.when((n & size) != 0)
            def _(done=done, size=size):
                fn(make_copy(off + done, dst + done, size))

        return carry

    lax.fori_loop(0, N_EXPERTS, per_expert, 0)


def _slot_rows(slot, n_slots):
    return pl.ds(pl.multiple_of(slot * SUBLANES, SUBLANES), n_slots * SUBLANES)


def _dispatch_kernel(run_len_ref, run_off_ref, run_dst_ref, pad_lo_ref, pad_hi_ref, pos_ref, h_ref, xs_ref, stage, zero_scr,
                     sem, pad_sem, *, tm, rows_per_pass):
    step = pl.program_id(0)
    na = TOP_K * tm
    d = h_ref.shape[1]
    half = d // 2
    n_words = half // LANES

    def pad_copy(slot, n_slots):
        return pltpu.make_async_copy(zero_scr.at[pl.ds(0, n_slots * SUBLANES), :],
                                     xs_ref.at[_slot_rows(slot, n_slots), :], pad_sem)

    def for_each_pad(fn):
        def per_expert(ex, carry):
            slot = pad_lo_ref[ex]
            n = pad_hi_ref[ex] - slot
            for size in _run_sizes(EXPERT_BLOCK - 1):
                take = (n & size) != 0

                @pl.when(take)
                def _(slot=slot, size=size):
                    fn(pad_copy(slot, size))

                slot = slot + jnp.where(take, size, 0)
            return carry
        lax.fori_loop(0, N_EXPERTS, per_expert, 0)

    @pl.when(step == 0)
    def _():
        zero_scr[...] = jnp.zeros_like(zero_scr)
        for_each_pad(lambda cp: cp.start())

    buf = step % 2
    pos = pos_ref[...]
    h = h_ref[...]
    for a0 in range(0, na, rows_per_pass):
        slot_id = a0 + _iota2((rows_per_pass, tm), 0)
        hit = pos[0:1, :] == slot_id
        for k in range(1, TOP_K):
            hit = jnp.logical_or(hit, pos[k:k + 1, :] == slot_id)
        rows = _dot(hit.astype(BF16), h)
        for i in range(n_words):
            word = _pack_halves(rows[:, i * LANES:(i + 1) * LANES], rows[:, half + i * LANES:half + (i + 1) * LANES])
            stage[buf, pl.ds(a0 * SUBLANES + i, rows_per_pass, stride=SUBLANES), :] = word

    def run_copy(tile_slot, sorted_slot, n_slots):
        return pltpu.make_async_copy(stage.at[buf, _slot_rows(tile_slot, n_slots), :],
                                     xs_ref.at[_slot_rows(sorted_slot, n_slots), :], sem.at[buf])

    def wait_tile(which):
        pltpu.make_async_copy(stage.at[which], xs_ref.at[pl.ds(0, na * SUBLANES), :], sem.at[which]).wait()

    _for_each_run(step, (run_len_ref, run_off_ref, run_dst_ref), tm, run_copy, lambda cp: cp.start())

    @pl.when(step > 0)
    def _():
        wait_tile(1 - buf)

    @pl.when(step == pl.num_programs(0) - 1)
    def _():
        wait_tile(buf)

    @pl.when(step == 0)
    def _():
        for_each_pad(lambda cp: cp.wait())


def _dispatch(runs, pad_lo, pad_hi, pos_t, h2, n_slots, tm):
    s, d = h2.shape
    assert (d // 2) % LANES == 0 and (d // 2) // LANES == SUBLANES, "one token row must pack into one (8, 128) tile"
    na = TOP_K * tm
    return pl.pallas_call(
        functools.partial(_dispatch_kernel, tm=tm, rows_per_pass=min(512, na)),
        grid_spec=pltpu.PrefetchScalarGridSpec(
            num_scalar_prefetch=5,
            grid=(s // tm,),
            in_specs=[pl.BlockSpec((TOP_K, tm), lambda i, *_: (0, i)),
                      pl.BlockSpec((tm, d), lambda i, *_: (i, 0))],
            out_specs=pl.BlockSpec(memory_space=pl.ANY),
            scratch_shapes=[pltpu.VMEM((2, na * SUBLANES, LANES), U32),
                            pltpu.VMEM((EXPERT_BLOCK // 2 * SUBLANES, LANES), U32),
                            pltpu.SemaphoreType.DMA((2,)), pltpu.SemaphoreType.DMA(())]),
        out_shape=jax.ShapeDtypeStruct((n_slots * SUBLANES, LANES), U32),
        compiler_params=_params(("arbitrary",), has_side_effects=True, disable_bounds_checks=True),
    )(*runs, pad_lo, pad_hi, pos_t, h2)


def _expert_kernel(be_ref, nu_ref, next_ref, par_ref, x_ref, wg_hbm, wu_hbm, wd_hbm, y_ref, wg_f32, wu_f32, wd_f32,
                   wg_scr, wu_scr, wd_scr, sem):
    b = pl.program_id(0)
    bm = EXPERT_BLOCK
    active = b < nu_ref[0]
    new_expert = jnp.logical_or(b == 0, be_ref[b] != be_ref[jnp.maximum(b - 1, 0)])

    def weight_copies(ex, which):
        return [pltpu.make_async_copy(src.at[ex], dst.at[which], sem.at[which])
                for src, dst in ((wg_hbm, wg_f32), (wu_hbm, wu_f32), (wd_hbm, wd_f32))]

    @pl.when(jnp.logical_and(active, new_expert))
    def _():
        which = par_ref[b]

        @pl.when(b == 0)
        def _():
            for cp in weight_copies(be_ref[b], which):
                cp.start()

        for cp in weight_copies(be_ref[b], which):
            cp.wait()
        wg_scr[...] = wg_f32[which].astype(BF16)
        wu_scr[...] = wu_f32[which].astype(BF16)
        wd_scr[...] = wd_f32[which].astype(BF16)
        nb = next_ref[b]

        @pl.when(nb < nu_ref[0])
        def _():
            for cp in weight_copies(be_ref[nb], 1 - which):
                cp.start()

    @pl.when(active)
    def _():
        los, his = [], []
        for i in range(SUBLANES):
            lo, hi = _unpack_halves(x_ref[pl.ds(i, bm, stride=SUBLANES), :])
            los.append(lo.astype(BF16))
            his.append(hi.astype(BF16))
        xb = jnp.concatenate(los + his, axis=1)
        hid = _silu(_dot(xb, wg_scr[...])) * _dot(xb, wu_scr[...])
        y = _dot(hid.astype(BF16), wd_scr[...])
        half = y.shape[1] // 2
        for i in range(SUBLANES):
            word = _pack_halves(_round_bf16(y[:, i * LANES:(i + 1) * LANES]),
                                _round_bf16(y[:, half + i * LANES:half + (i + 1) * LANES]))
            y_ref[pl.ds(i, bm, stride=SUBLANES), :] = word


def _experts(block_e, n_used, next_block, parity, xs, w_gate, w_up, w_down):
    d, ff = w_gate.shape[1], w_gate.shape[2]
    bm = EXPERT_BLOCK
    n_blocks = xs.shape[0] // (bm * SUBLANES)
    blk = lambda b, be, nu, *_: (jnp.minimum(b, nu[0] - 1), 0)
    hbm = pl.BlockSpec(memory_space=pl.ANY)
    return pl.pallas_call(
        _expert_kernel,
        grid_spec=pltpu.PrefetchScalarGridSpec(
            num_scalar_prefetch=4,
            grid=(n_blocks,),
            in_specs=[pl.BlockSpec((bm * SUBLANES, LANES), blk), hbm, hbm, hbm],
            out_specs=pl.BlockSpec((bm * SUBLANES, LANES), blk),
            scratch_shapes=[pltpu.VMEM((2, d, ff), F32), pltpu.VMEM((2, d, ff), F32), pltpu.VMEM((2, ff, d), F32),
                            pltpu.VMEM((d, ff), BF16), pltpu.VMEM((d, ff), BF16), pltpu.VMEM((ff, d), BF16),
                            pltpu.SemaphoreType.DMA((2,))]),
        out_shape=jax.ShapeDtypeStruct(xs.shape, U32),
        compiler_params=_params(("arbitrary",)),
    )(block_e, n_used, next_block, parity, xs, w_gate, w_up, w_down)


def _combine_kernel(run_len_ref, run_off_ref, run_src_ref, ys_ref, h_ref, x1_ref, pos_ref, wts_ref, wg_ref, wu_ref, wd_ref, gt_ref, gf_ref,
                    o_ref, stage, sem, *, tm, rows_per_pass):
    step = pl.program_id(0)
    na = TOP_K * tm
    buf = step % 2

    def fetch_tile(tile, which):
        def run_copy(tile_slot, sorted_slot, n_slots):
            return pltpu.make_async_copy(ys_ref.at[_slot_rows(sorted_slot, n_slots), :],
                                         stage.at[which, _slot_rows(tile_slot, n_slots), :], sem.at[which])
        _for_each_run(tile, (run_len_ref, run_off_ref, run_src_ref), tm, run_copy, lambda cp: cp.start())

    @pl.when(step == 0)
    def _():
        fetch_tile(step, buf)

    @pl.when(step + 1 < pl.num_programs(0))
    def _():
        fetch_tile(step + 1, 1 - buf)

    hb = h_ref[...]
    hid = _silu(_dot(hb, wg_ref[...])) * _dot(hb, wu_ref[...])
    acc = _dot(hid.astype(BF16), wd_ref[...])

    pltpu.make_async_copy(ys_ref.at[pl.ds(0, na * SUBLANES), :], stage.at[buf], sem.at[buf]).wait()

    pos = pos_ref[...]
    wts = wts_ref[...]
    for a0 in range(0, na, rows_per_pass):
        los, his = [], []
        for i in range(SUBLANES):
            lo, hi = _unpack_halves(stage[buf, pl.ds(a0 * SUBLANES + i, rows_per_pass, stride=SUBLANES), :])
            los.append(lo.astype(BF16))
            his.append(hi.astype(BF16))
        y_rows = jnp.concatenate(los + his, axis=1)
        slot_id = a0 + _iota2((rows_per_pass, tm), 0)
        wmat = jnp.zeros((rows_per_pass, tm), F32)
        for k in range(TOP_K):
            wmat = wmat + jnp.where(pos[k:k + 1, :] == slot_id, wts[k:k + 1, :], 0.0)
        acc = acc + _dg(wmat.astype(BF16), y_rows, TN)
    x2 = x1_ref[...] + gt_ref[...] * acc
    o_ref[...] = _rms(x2, NORM_EPS) * gf_ref[...]


def _combine(runs, ys, h2, x1, pos_t, wts_t, w_gate, w_up, w_down, gt2, gf, tm):
    s, d = x1.shape
    ff = w_gate.shape[1]
    na = TOP_K * tm
    const = lambda shape: pl.BlockSpec(shape, lambda i, *_: (0, 0), pipeline_mode=pl.Buffered(1))
    tile = lambda cols: pl.BlockSpec((tm, cols), lambda i, *_: (i, 0))
    per_k = pl.BlockSpec((TOP_K, tm), lambda i, *_: (0, i))
    return pl.pallas_call(
        functools.partial(_combine_kernel, tm=tm, rows_per_pass=min(512, na)),
        grid_spec=pltpu.PrefetchScalarGridSpec(
            num_scalar_prefetch=3,
            grid=(s // tm,),
            in_specs=[pl.BlockSpec(memory_space=pl.ANY),
                      tile(d), tile(d), per_k, per_k,
                      const((d, ff)), const((d, ff)), const((ff, d)), const((1, d)), const((1, d))],
            out_specs=tile(d),
            scratch_shapes=[pltpu.VMEM((2, na * SUBLANES, LANES), U32), pltpu.SemaphoreType.DMA((2,))]),
        out_shape=jax.ShapeDtypeStruct((s, d), F32),
        compiler_params=_params(("arbitrary",), disable_bounds_checks=True),
    )(*runs, ys, h2, x1, pos_t, wts_t, w_gate, w_up, w_down, gt2, gf)


def _mixer(x2d, mod, norm1_g, norm2_g, w_in, lb, hgrn_onorm_g, gdn_conv_w, gdn_a_log, gdn_dt_bias, gdn_onorm_g,
           w_branch_hgrn, w_branch_gdn, w_out, tiles):
    d = x2d.shape[1]
    sh1, sc1, gt1, sh2, sc2, _ = [mod[:, i * d:(i + 1) * d] for i in range(6)]
    key = HEADS * HEAD_DIM
    small0 = 4 * key + 3 * key
    small1 = small0 + 2 * HEADS
    w_main = _wprep(w_in, small0, small1, tiles["wprep_tn"])
    w_small_t = w_in[:, small0:small1].T.astype(BF16)
    proj, ab_t = _inproj(x2d, norm1_g, sc1, sh1, w_main, w_small_t, tiles["in_tm"], tiles["in_tn"])
    o_a = _hgrn(proj, lb, hgrn_onorm_g, tiles["mix_ts"])
    u, wqd, ku, attn, dl = _gdn_prep(proj, gdn_conv_w, ab_t, gdn_a_log, gdn_dt_bias, tiles["prep_ts"])
    o_b = _gdn_scan(u, wqd, ku, attn, dl, proj, gdn_onorm_g, tiles["mix_ts"])
    return _merge(o_a, o_b, proj, x2d, w_branch_hgrn.astype(BF16), w_branch_gdn.astype(BF16),
                  w_out.astype(BF16), gt1, norm2_g, sc2, sh2, tiles["merge_tm"])


def _moe(x1, h2, mod, norm2_g, normf_g, w_router, router_bias, w_exp_gate, w_exp_up, w_exp_down, w_sh_gate,
         w_sh_up, w_sh_down, tiles):
    s, d = x1.shape
    tm = tiles["moe_tm"]
    sh2, sc2, gt2 = [mod[:, i * d:(i + 1) * d] for i in (3, 4, 5)]
    pos_t, wts_t, before, ntile = _router(x1, norm2_g, sc2, sh2, w_router.T, router_bias.reshape(-1, 1), tm)
    bm = EXPERT_BLOCK
    n_blocks = -(-(s * TOP_K + N_EXPERTS * (bm - 1)) // bm)
    before = before[:, :, 0]
    ntile = ntile[:, :, 0]
    counts = before[-1] + ntile[-1]
    padded = (counts + bm - 1) // bm * bm
    earlier = jnp.arange(N_EXPERTS)[None, :] < jnp.arange(N_EXPERTS)[:, None]
    pstart = jnp.sum(jnp.where(earlier, padded[None, :], 0), axis=1).astype(I32)
    pend = pstart + padded
    block_start = jnp.arange(n_blocks, dtype=I32) * bm
    block_e = jnp.minimum(jnp.sum(pend[None, :] <= block_start[:, None], axis=1), N_EXPERTS - 1).astype(I32)
    n_used = pend[-1:] // bm
    run_off = jnp.sum(jnp.where(earlier[None], ntile[:, None, :], 0), axis=2)
    runs = (ntile.reshape(-1), run_off.reshape(-1), (before + pstart[None, :]).reshape(-1))
    xs = _dispatch(runs, pstart + counts, pend, pos_t, h2, n_blocks * bm, tm)
    next_block = pend[block_e] // bm
    switches = jnp.concatenate([jnp.zeros((1,), I32), (block_e[1:] != block_e[:-1]).astype(I32)])
    upto = jnp.arange(n_blocks)[None, :] <= jnp.arange(n_blocks)[:, None]
    parity = jnp.sum(jnp.where(upto, switches[None, :], 0), axis=1).astype(I32) % 2
    ys = _experts(block_e, n_used, next_block, parity, xs, w_exp_gate, w_exp_up, w_exp_down)
    return _combine(runs, ys, h2, x1, pos_t, wts_t, w_sh_gate.astype(BF16), w_sh_up.astype(BF16),
                    w_sh_down.astype(BF16), gt2, normf_g, tm)


def _tiles(s):
    pick = lambda want: min(want, s)
    return dict(wprep_tn=512, in_tm=pick(1024), in_tn=1536, mix_ts=pick(512), prep_ts=pick(2048), merge_tm=pick(512),
                moe_tm=pick(256))


def kernel(x, c, w_ada, b_ada, norm1_g, norm2_g, w_in, hgrn_lb_table, hgrn_onorm_g, gdn_conv_w, gdn_a_log, gdn_dt_bias, gdn_onorm_g, w_branch_hgrn, w_branch_gdn, w_out, w_router, router_bias, w_exp_gate, w_exp_up, w_exp_down, w_sh_gate, w_sh_up, w_sh_down, normf_g):
    b, s, d = x.shape
    assert b == 1 and w_ada.shape[0] == 1, "one sequence, one layer"
    tiles = _tiles(s)
    lb = jnp.sum(jax.nn.softmax(hgrn_lb_table.astype(F32), axis=0)[0:1], axis=0, keepdims=True)
    mod = _ada(c, w_ada[0], b_ada[0])
    row = lambda v: v.reshape(1, -1)
    x1, h2 = _mixer(x[0], mod, row(norm1_g[0]), row(norm2_g[0]), w_in[0], lb, row(hgrn_onorm_g[0]), gdn_conv_w[0],
                    gdn_a_log[0], gdn_dt_bias[0], row(gdn_onorm_g[0]), w_branch_hgrn[0], w_branch_gdn[0], w_out[0],
                    tiles)
    out = _moe(x1, h2, mod, row(norm2_g[0]), row(normf_g), w_router[0], router_bias[0], w_exp_gate[0],
               w_exp_up[0], w_exp_down[0], w_sh_gate[0], w_sh_up[0], w_sh_down[0], tiles)
    return out[None]
```

```python
import functools

import jax
import jax.numpy as jnp
from jax import lax
from jax.experimental import pallas as pl
from jax.experimental.pallas import tpu as pltpu

F32 = jnp.float32
BF16 = jnp.bfloat16
I32 = jnp.int32
U32 = jnp.uint32

NORM_EPS = 1e-6
L2_EPS = 1e-6
HEADS = 8
HEAD_DIM = 128
CONV_WIDTH = 4
CHUNK = 64
N_EXPERTS = 64
N_GROUPS = 8
GROUP_SIZE = N_EXPERTS // N_GROUPS
TOPK_GROUPS = 4
TOP_K = 8
ROUTED_SCALE = 2.5
EXPERT_BLOCK = 512

LANES = 128
SUBLANES = 8
VMEM_LIMIT = 56 * 1024 * 1024

NT = (((1,), (1,)), ((), ()))
TN = (((0,), (0,)), ((), ()))


def _params(sem, **kw):
    return pltpu.CompilerParams(dimension_semantics=sem, vmem_limit_bytes=VMEM_LIMIT, **kw)


def _dot(a, b):
    return jnp.dot(a, b, preferred_element_type=F32)


def _dg(a, b, dims):
    return lax.dot_general(a, b, dims, preferred_element_type=F32)


def _split(x):
    hi = x.astype(BF16)
    lo = (x - hi.astype(F32)).astype(BF16)
    return hi, lo


def _dot_exact_lhs(a_bf16, x, dims=None):
    hi, lo = _split(x)
    if dims is None:
        return _dot(a_bf16, hi) + _dot(a_bf16, lo)
    return _dg(a_bf16, hi, dims) + _dg(a_bf16, lo, dims)


def _sigmoid(x):
    return 1.0 / (1.0 + jnp.exp(-x))


def _silu(x):
    return x * _sigmoid(x)


def _rms(x, eps):
    return x * lax.rsqrt(jnp.mean(x * x, axis=-1, keepdims=True) + eps)


def _iota2(shape, dim):
    return lax.broadcasted_iota(I32, shape, dim)


def _pack_halves(lo, hi):
    lo_bits = lax.shift_right_logical(pltpu.bitcast(lo, U32), U32(16))
    hi_bits = pltpu.bitcast(hi, U32) & U32(0xFFFF0000)
    return lo_bits | hi_bits


def _unpack_halves(word):
    lo = pltpu.bitcast(lax.shift_left(word, U32(16)), F32)
    hi = pltpu.bitcast(word & U32(0xFFFF0000), F32)
    return lo, hi


def _round_bf16(x):
    return x.astype(BF16).astype(F32)


def _ada_kernel(c_ref, w_ref, b_ref, o_ref):
    cond = _silu(c_ref[...])
    o_ref[...] = jnp.dot(cond, w_ref[...], preferred_element_type=F32,
                         precision=lax.Precision.HIGHEST) + b_ref[...]


def _ada(c, w_ada, b_ada):
    d, n = w_ada.shape
    tn = 1024
    c8 = jnp.broadcast_to(c, (SUBLANES, d))
    out = pl.pallas_call(
        _ada_kernel,
        grid=(n // tn,),
        in_specs=[pl.BlockSpec((SUBLANES, d), lambda j: (0, 0)),
                  pl.BlockSpec((d, tn), lambda j: (0, j)),
                  pl.BlockSpec((1, tn), lambda j: (0, j))],
        out_specs=pl.BlockSpec((SUBLANES, tn), lambda j: (0, j)),
        out_shape=jax.ShapeDtypeStruct((SUBLANES, n), F32),
        compiler_params=_params(("arbitrary",)),
    )(c8, w_ada, b_ada.reshape(1, n))
    return out[0:1]


def _wprep_kernel(a_ref, b_ref, o_ref, *, first_shifted, shift):
    j = pl.program_id(0)

    @pl.when(j < first_shifted)
    def _():
        o_ref[...] = a_ref[...].astype(BF16)

    @pl.when(j >= first_shifted)
    def _():
        tn = a_ref.shape[1]
        both = jnp.concatenate([a_ref[...], b_ref[...]], axis=1)
        o_ref[...] = pltpu.roll(both, both.shape[1] - shift, axis=1)[:, 0:tn].astype(BF16)


def _wprep(w_in, cut0, cut1, tn):
    _, d, n_in = w_in.shape
    n_out = n_in - (cut1 - cut0)
    assert cut0 % tn == 0 and n_out % tn == 0 and tn % LANES == 0 and 0 < cut1 - cut0 < LANES
    return pl.pallas_call(
        functools.partial(_wprep_kernel, first_shifted=cut0 // tn, shift=cut1 - cut0),
        grid=(n_out // tn,),
        in_specs=[pl.BlockSpec((None, d, tn), lambda j: (0, 0, j)),
                  pl.BlockSpec((None, d, LANES), lambda j: (0, 0, (j + 1) * (tn // LANES)))],
        out_specs=pl.BlockSpec((d, tn), lambda j: (0, j)),
        out_shape=jax.ShapeDtypeStruct((d, n_out), BF16),
        compiler_params=_params(("arbitrary",)),
    )(w_in, w_in)


def _inproj_kernel(x_ref, g_ref, sc_ref, sh_ref, w_ref, wst_ref, proj_ref, smallt_ref, h_scr):
    @pl.when(pl.program_id(1) == 0)
    def _():
        h = _rms(x_ref[...], NORM_EPS) * g_ref[...] * (1.0 + sc_ref[...]) + sh_ref[...]
        hb = h.astype(BF16)
        h_scr[...] = hb
        smallt_ref[...] = _dg(wst_ref[...], hb, NT)

    proj_ref[...] = _dot(h_scr[...], w_ref[...]).astype(BF16)


def _inproj(x, g, sc, sh, w_main, w_small_t, tm, tn):
    s, d = x.shape
    n = w_main.shape[1]
    ns = w_small_t.shape[0]
    row = lambda i, j: (0, 0)
    return pl.pallas_call(
        _inproj_kernel,
        grid=(s // tm, n // tn),
        in_specs=[pl.BlockSpec((tm, d), lambda i, j: (i, 0)),
                  pl.BlockSpec((1, d), row), pl.BlockSpec((1, d), row), pl.BlockSpec((1, d), row),
                  pl.BlockSpec((d, tn), lambda i, j: (0, j)),
                  pl.BlockSpec((ns, d), row)],
        out_specs=[pl.BlockSpec((tm, tn), lambda i, j: (i, j)),
                   pl.BlockSpec((ns, tm), lambda i, j: (0, i))],
        out_shape=[jax.ShapeDtypeStruct((s, n), BF16), jax.ShapeDtypeStruct((ns, s), F32)],
        scratch_shapes=[pltpu.VMEM((tm, d), BF16)],
        compiler_params=_params(("arbitrary", "arbitrary")),
    )(x, g, sc, sh, w_main, w_small_t)


def _hgrn_kernel(q_ref, f_ref, i_ref, g_ref, lb_ref, on_ref, o_ref, st_scr, *, n_chunks):
    @pl.when(pl.program_id(0) == 0)
    def _():
        st_scr[...] = jnp.zeros_like(st_scr)

    c = CHUNK
    hd = HEAD_DIM
    causal = _iota2((c, c), 1) <= _iota2((c, c), 0)
    tri = causal.astype(BF16)
    lb = lb_ref[...]
    on_g = on_ref[...]
    heads = [slice(h * hd, (h + 1) * hd) for h in range(HEADS)]

    def chunk(n, carry):
        rows = pl.ds(pl.multiple_of(n * c, c), c)
        f = lb + (1.0 - lb) * _sigmoid(f_ref[rows, :].astype(F32))
        b = _dot_exact_lhs(tri, jnp.log(f))
        k = 1.0 - f
        q = _silu(q_ref[rows, :].astype(F32)) * (hd ** -0.5)
        v = i_ref[rows, :]
        b_mid = b[c // 2:c // 2 + 1, :]
        b_last = b[c - 1:c, :]
        qa = (q * jnp.exp(b - b_mid)).astype(BF16)
        ka = (k * jnp.exp(b_mid - b)).astype(BF16)
        qi = (q * jnp.exp(b)).astype(BF16)
        ku = (k * jnp.exp(b_last - b)).astype(BF16)
        dec = jnp.exp(b_last)
        gate = on_g * _silu(g_ref[rows, :].astype(F32))
        sts = [st_scr[h] for h in range(HEADS)]
        scores = [jnp.where(causal, _dg(qa[:, sl], ka[:, sl], NT), 0.0).astype(BF16) for sl in heads]
        inter = [_dg(qi[:, sl], st.astype(BF16), NT) for sl, st in zip(heads, sts)]
        kv = [_dg(v[:, sl], ku[:, sl], TN) for sl in heads]
        for h, sl in enumerate(heads):
            st_scr[h] = dec[:, sl] * sts[h] + kv[h]
        outs = [_rms(_dot(sc, v[:, sl]) + it, NORM_EPS) for sc, sl, it in zip(scores, heads, inter)]
        o_ref[rows, :] = (jnp.concatenate(outs, axis=1) * gate).astype(BF16)
        return carry

    lax.fori_loop(0, n_chunks, chunk, 0)


def _hgrn(proj, lb, onorm_g, ts):
    s = proj.shape[0]
    width = HEADS * HEAD_DIM
    col = lambda blk: pl.BlockSpec((ts, width), lambda j, blk=blk: (j, blk))
    const = pl.BlockSpec((1, width), lambda j: (0, 0))
    return pl.pallas_call(
        functools.partial(_hgrn_kernel, n_chunks=ts // CHUNK),
        grid=(s // ts,),
        in_specs=[col(0), col(1), col(2), col(3), const, const],
        out_specs=pl.BlockSpec((ts, width), lambda j: (j, 0)),
        out_shape=jax.ShapeDtypeStruct((s, width), BF16),
        scratch_shapes=[pltpu.VMEM((HEADS, HEAD_DIM, HEAD_DIM), F32)],
        compiler_params=_params(("arbitrary",)),
    )(proj, proj, proj, proj, lb, jnp.tile(onorm_g, (1, HEADS)))


def _gdn_prep_kernel(q_ref, k_ref, v_ref, qp_ref, kp_ref, vp_ref, wq_ref, wk_ref, wv_ref, ab_ref, alog_ref,
                     dtb_ref, tri_ref, eye_ref, u_ref, wqd_ref, ku_ref, attn_ref, dl_ref, cat_scr, rows_scr, cols_scr,
                     *, n_chunks, ts):
    h = pl.program_id(1)
    first = pl.program_id(0) == 0
    c = CHUNK
    hd = HEAD_DIM

    def conv_silu(cur_ref, prev_ref, w_ref):
        cat_scr[0:8, :] = jnp.where(first, 0.0, prev_ref[...].astype(F32))
        cat_scr[8:8 + ts, :] = cur_ref[...].astype(F32)
        acc = None
        for j in range(CONV_WIDTH):
            off = 8 - (CONV_WIDTH - 1) + j
            term = cat_scr[off:off + ts, :] * w_ref[j:j + 1, :]
            acc = term if acc is None else acc + term
        return _silu(acc)

    def l2n(x):
        return x * lax.rsqrt(jnp.sum(x * x, axis=-1, keepdims=True) + L2_EPS)

    q_all = l2n(conv_silu(q_ref, qp_ref, wq_ref)) * (hd ** -0.5)
    k_all = l2n(conv_silu(k_ref, kp_ref, wk_ref))
    v_all = conv_silu(v_ref, vp_ref, wv_ref)

    @pl.when(h == 0)
    def _():
        z = ab_ref[0:HEADS, :] + dtb_ref[...]
        softplus = jnp.maximum(z, 0.0) + jnp.log(1.0 + jnp.exp(-jnp.abs(z)))
        ld_rows = -jnp.exp(alog_ref[...]) * softplus
        hi, lo = _split(ld_rows)
        tri_blocks = tri_ref[...]
        g_rows = _dg(hi, tri_blocks, NT) + _dg(lo, tri_blocks, NT)
        beta_rows = _sigmoid(ab_ref[HEADS:2 * HEADS, :])
        rows_scr[...] = g_rows
        rows = jnp.concatenate([g_rows, beta_rows, jnp.zeros((LANES - 2 * HEADS, ts), F32)], axis=0)
        r_hi, r_lo = _split(rows)
        r_lo2 = (rows - r_hi.astype(F32) - r_lo.astype(F32)).astype(BF16)
        eye_ts = eye_ref[...]
        cols_scr[...] = _dg(eye_ts, r_hi, NT) + _dg(eye_ts, r_lo, NT) + _dg(eye_ts, r_lo2, NT)

    lane = _iota2((ts, LANES), 1)
    cols = cols_scr[...]
    gc_all = jnp.sum(jnp.where(lane == h, cols, 0.0), axis=1, keepdims=True)
    bc_all = jnp.sum(jnp.where(lane == h + HEADS, cols, 0.0), axis=1, keepdims=True)
    g_row = rows_scr[pl.ds(h, 1), :]
    egc_all = jnp.exp(gc_all)

    r = _iota2((c, c), 0)
    cidx = _iota2((c, c), 1)
    causal = cidx <= r
    strict = cidx < r
    eye_f = (r == cidx).astype(F32)
    chunks = [slice(n * c, (n + 1) * c) for n in range(n_chunks)]

    q16 = q_all.astype(BF16)
    k16 = k_all.astype(BF16)
    kq = [_dg(jnp.concatenate([k16[sl], q16[sl]], axis=0), k16[sl], NT) for sl in chunks]
    dm = []
    for sl in chunks:
        diff = gc_all[sl] - g_row[:, sl]
        dm.append(jnp.where(causal, jnp.exp(jnp.where(causal, diff, 0.0)), 0.0))
    bm = [-jnp.where(strict, bc_all[sl] * x[0:c] * d, 0.0) for sl, x, d in zip(chunks, kq, dm)]
    p = [eye_f + b for b in bm]
    bm = [_dot(b.astype(BF16), b.astype(BF16)) for b in bm]
    for _ in range(c.bit_length() - 3):
        res = [_dot(b.astype(BF16), jnp.concatenate([b, pp], axis=1).astype(BF16)) for b, pp in zip(bm, p)]
        p = [pp + x[:, c:2 * c] for pp, x in zip(p, res)]
        bm = [x[:, 0:c] for x in res]
    p = [pp + _dot(b.astype(BF16), pp.astype(BF16)) for b, pp in zip(bm, p)]
    rhs = jnp.concatenate([v_all * bc_all, k_all * (bc_all * egc_all)], axis=1).astype(BF16)
    sol = [_dot(pp.astype(BF16), rhs[sl]) for pp, sl in zip(p, chunks)]
    qd_all = (q_all * egc_all).astype(BF16)
    for n, sl in enumerate(chunks):
        g_last = gc_all[(n + 1) * c - 1:(n + 1) * c, :]
        u_ref[sl, :] = sol[n][:, 0:hd].astype(BF16)
        wqd_ref[2 * n * c:(2 * n + 1) * c, :] = sol[n][:, hd:2 * hd].astype(BF16)
        wqd_ref[(2 * n + 1) * c:(2 * n + 2) * c, :] = qd_all[sl]
        ku_ref[sl, :] = (k_all[sl] * jnp.exp(g_last - gc_all[sl])).astype(BF16)
        attn_ref[sl, :] = (kq[n][c:2 * c] * dm[n]).astype(BF16)
        dl_ref[n:n + 1, :] = jnp.broadcast_to(jnp.exp(g_last), (1, hd))


def _gdn_prep(proj, conv_w, ab_t, a_log, dt_bias, ts):
    s = proj.shape[0]
    hd = HEAD_DIM
    c = CHUNK
    q0 = 4 * HEADS
    cur = lambda off: pl.BlockSpec((ts, hd), lambda j, h, off=off: (j, off + h))
    prev = lambda off: pl.BlockSpec((8, hd), lambda j, h, off=off: (jnp.maximum(j * (ts // 8) - 1, 0), off + h))
    cw = lambda off: pl.BlockSpec((CONV_WIDTH, hd), lambda j, h, off=off: (0, off + h))
    per_head_scalar = pl.BlockSpec((HEADS, 1), lambda j, h: (0, 0))
    const = pl.BlockSpec((ts, ts), lambda j, h: (0, 0), pipeline_mode=pl.Buffered(1))
    pos = jnp.arange(ts)
    tri_blocks = ((pos[:, None] // c == pos[None, :] // c) & (pos[None, :] <= pos[:, None])).astype(BF16)
    eye = (pos[:, None] == pos[None, :]).astype(BF16)
    per_head = lambda rows, cols: pl.BlockSpec((None, rows, cols), lambda j, h: (h, j, 0))
    return pl.pallas_call(
        functools.partial(_gdn_prep_kernel, n_chunks=ts // c, ts=ts),
        grid=(s // ts, HEADS),
        in_specs=[cur(q0), cur(q0 + HEADS), cur(q0 + 2 * HEADS),
                  prev(q0), prev(q0 + HEADS), prev(q0 + 2 * HEADS),
                  cw(0), cw(HEADS), cw(2 * HEADS),
                  pl.BlockSpec((2 * HEADS, ts), lambda j, h: (0, j)),
                  per_head_scalar, per_head_scalar, const, const],
        out_specs=[pl.BlockSpec((ts, hd), lambda j, h: (j, h)),
                   pl.BlockSpec((2 * ts, hd), lambda j, h: (j, h)),
                   pl.BlockSpec((ts, hd), lambda j, h: (j, h)),
                   per_head(ts, c),
                   per_head(ts // c, hd)],
        out_shape=[jax.ShapeDtypeStruct((s, HEADS * hd), BF16),
                   jax.ShapeDtypeStruct((2 * s, HEADS * hd), BF16),
                   jax.ShapeDtypeStruct((s, HEADS * hd), BF16),
                   jax.ShapeDtypeStruct((HEADS, s, c), BF16),
                   jax.ShapeDtypeStruct((HEADS, s // c, hd), F32)],
        scratch_shapes=[pltpu.VMEM((ts + 8, hd), F32), pltpu.VMEM((HEADS, ts), F32), pltpu.VMEM((ts, LANES), F32)],
        compiler_params=_params(("arbitrary", "arbitrary")),
    )(proj, proj, proj, proj, proj, proj, conv_w, conv_w, conv_w,
      ab_t, a_log.reshape(HEADS, 1), dt_bias.reshape(HEADS, 1), tri_blocks, eye)


def _gdn_scan_kernel(u_ref, wqd_ref, ku_ref, attn_ref, dl_ref, g_ref, on_ref, o_ref, st_scr, *, n_chunks):
    @pl.when(pl.program_id(0) == 0)
    def _():
        st_scr[...] = jnp.zeros_like(st_scr)

    c = CHUNK
    hd = HEAD_DIM
    on_g = on_ref[...]
    heads = [slice(h * hd, (h + 1) * hd) for h in range(HEADS)]

    def chunk(n, carry):
        rows = pl.ds(pl.multiple_of(n * c, c), c)
        rows2 = pl.ds(pl.multiple_of(2 * n * c, 2 * c), 2 * c)
        sts = [st_scr[h] for h in range(HEADS)]
        wq = [_dot(wqd_ref[rows2, sl], st.astype(BF16)) for sl, st in zip(heads, sts)]
        vn = [(u_ref[rows, sl].astype(F32) - x[0:c]).astype(BF16) for sl, x in zip(heads, wq)]
        upd = [_dg(ku_ref[rows, sl], v, TN) for sl, v in zip(heads, vn)]
        for h in range(HEADS):
            st_scr[h] = dl_ref[h, pl.ds(n, 1), :] * sts[h] + upd[h]
        outs = [_rms(x[c:2 * c] + _dot(attn_ref[h, rows, :], v), NORM_EPS)
                for h, (x, v) in enumerate(zip(wq, vn))]
        gate = jnp.tile(on_g, (1, HEADS)) * _silu(g_ref[rows, :].astype(F32))
        o_ref[rows, :] = (jnp.concatenate(outs, axis=1) * gate).astype(BF16)
        return carry

    lax.fori_loop(0, n_chunks, chunk, 0)


def _gdn_scan(u, wqd, ku, attn, dl, proj, onorm_g, ts):
    s, width = u.shape
    c = CHUNK
    gate_blk = (4 * HEADS + 3 * HEADS) * HEAD_DIM // width
    return pl.pallas_call(
        functools.partial(_gdn_scan_kernel, n_chunks=ts // c),
        grid=(s // ts,),
        in_specs=[pl.BlockSpec((ts, width), lambda j: (j, 0)),
                  pl.BlockSpec((2 * ts, width), lambda j: (j, 0)),
                  pl.BlockSpec((ts, width), lambda j: (j, 0)),
                  pl.BlockSpec((HEADS, ts, c), lambda j: (0, j, 0)),
                  pl.BlockSpec((HEADS, ts // c, HEAD_DIM), lambda j: (0, j, 0)),
                  pl.BlockSpec((ts, width), lambda j: (j, gate_blk)),
                  pl.BlockSpec((1, HEAD_DIM), lambda j: (0, 0))],
        out_specs=pl.BlockSpec((ts, width), lambda j: (j, 0)),
        out_shape=jax.ShapeDtypeStruct((s, width), BF16),
        scratch_shapes=[pltpu.VMEM((HEADS, HEAD_DIM, HEAD_DIM), F32)],
        compiler_params=_params(("arbitrary",)),
    )(u, wqd, ku, attn, dl, proj, onorm_g)


def _merge_kernel(oa_ref, ob_ref, mga_ref, mgb_ref, x_ref, wa_ref, wb_ref, wo_ref, gt_ref, g2_ref, sc_ref,
                  sh_ref, x1_ref, h2_ref):
    ya = _dot(oa_ref[...], wa_ref[...])
    yb = _dot(ob_ref[...], wb_ref[...])
    merged = _sigmoid(mga_ref[...].astype(F32)) * ya + _sigmoid(mgb_ref[...].astype(F32)) * yb
    x1 = x_ref[...] + gt_ref[...] * _dot(merged.astype(BF16), wo_ref[...])
    x1_ref[...] = x1
    h2 = _rms(x1, NORM_EPS) * g2_ref[...] * (1.0 + sc_ref[...]) + sh_ref[...]
    h2_ref[...] = h2.astype(BF16)


def _merge(o_a, o_b, proj, x, w_a, w_b, w_o, gt1, g2, sc2, sh2, tm):
    s, d = x.shape
    dv = o_a.shape[1]
    mg0 = (8 * HEADS * HEAD_DIM) // d
    const = lambda shape: pl.BlockSpec(shape, lambda i: (0, 0), pipeline_mode=pl.Buffered(1))
    return pl.pallas_call(
        _merge_kernel,
        grid=(s // tm,),
        in_specs=[pl.BlockSpec((tm, dv), lambda i: (i, 0)),
                  pl.BlockSpec((tm, dv), lambda i: (i, 0)),
                  pl.BlockSpec((tm, d), lambda i: (i, mg0)),
                  pl.BlockSpec((tm, d), lambda i: (i, mg0 + 1)),
                  pl.BlockSpec((tm, d), lambda i: (i, 0)),
                  const((dv, d)), const((dv, d)), const((d, d)),
                  const((1, d)), const((1, d)), const((1, d)), const((1, d))],
        out_specs=[pl.BlockSpec((tm, d), lambda i: (i, 0)), pl.BlockSpec((tm, d), lambda i: (i, 0))],
        out_shape=[jax.ShapeDtypeStruct((s, d), F32), jax.ShapeDtypeStruct((s, d), BF16)],
        compiler_params=_params(("arbitrary",)),
    )(o_a, o_b, proj, proj, x, w_a, w_b, w_o, gt1, g2, sc2, sh2)


def _first_max(vals, iota, size, axis):
    m = jnp.max(vals, axis=axis, keepdims=True)
    idx = jnp.min(jnp.where(vals == m, iota, size), axis=axis, keepdims=True)
    return m, idx


def _router_kernel(x1_ref, g2_ref, sc_ref, sh_ref, wrt_ref, bias_ref, upper_ref, pos_ref, wts_ref, before_ref,
                   ntile_ref, cnt_scr, *, tm):
    @pl.when(pl.program_id(0) == 0)
    def _():
        cnt_scr[...] = jnp.zeros_like(cnt_scr)

    e = N_EXPERTS
    h2 = _rms(x1_ref[...], NORM_EPS) * g2_ref[...] * (1.0 + sc_ref[...]) + sh_ref[...]
    logits = lax.dot_general(wrt_ref[...], h2, NT, preferred_element_type=F32,
                             precision=lax.Precision.HIGHEST)
    scores = _sigmoid(logits)
    biased = scores + bias_ref[...]
    neg = -jnp.inf

    g3 = biased.reshape(N_GROUPS, GROUP_SIZE, tm)
    i3 = lax.broadcasted_iota(I32, g3.shape, 1)
    m1, a1 = _first_max(g3, i3, GROUP_SIZE, 1)
    m2 = jnp.max(jnp.where(i3 == a1, neg, g3), axis=1, keepdims=True)
    gs = (m1 + m2).reshape(N_GROUPS, tm)
    ig = _iota2(gs.shape, 0)
    gmask = jnp.zeros(gs.shape, jnp.bool_)
    for _ in range(TOPK_GROUPS):
        _, a = _first_max(gs, ig, N_GROUPS, 0)
        pick = ig == a
        gmask = jnp.logical_or(gmask, pick)
        gs = jnp.where(pick, neg, gs)
    emask = jnp.broadcast_to(gmask.reshape(N_GROUPS, 1, tm), (N_GROUPS, GROUP_SIZE, tm)).reshape(e, tm)

    cand = jnp.where(emask, biased, neg)
    ie = _iota2((e, tm), 0)
    sel_all = jnp.zeros((e, tm), jnp.bool_)
    w_rows, picks = [], []
    for _ in range(TOP_K):
        _, a = _first_max(cand, ie, e, 0)
        pick = ie == a
        picks.append(pick)
        w_rows.append(jnp.sum(jnp.where(pick, scores, 0.0), axis=0, keepdims=True))
        sel_all = jnp.logical_or(sel_all, pick)
        cand = jnp.where(pick, neg, cand)
    w_sum = w_rows[0]
    for wr in w_rows[1:]:
        w_sum = w_sum + wr
    wts = jnp.concatenate(w_rows, axis=0) / w_sum * ROUTED_SCALE

    sel = sel_all.astype(BF16)
    in_expert = _dot(sel, upper_ref[...])
    n_tile = jnp.sum(sel_all.astype(F32), axis=1, keepdims=True)
    lower = (_iota2((e, e), 1) < _iota2((e, e), 0)).astype(BF16)
    expert_off = _dot_exact_lhs(lower, jnp.broadcast_to(n_tile, (e, LANES)))[:, 0:1]
    place = in_expert + expert_off
    pos = jnp.concatenate([jnp.sum(jnp.where(pk, place, 0.0), axis=0, keepdims=True) for pk in picks], axis=0)
    pos_ref[...] = pos.astype(I32)
    before_ref[...] = jnp.broadcast_to(cnt_scr[...], before_ref.shape).astype(I32)
    ntile_ref[...] = jnp.broadcast_to(n_tile, ntile_ref.shape).astype(I32)
    cnt_scr[...] = cnt_scr[...] + n_tile
    wts_ref[...] = wts


def _router(x1, g2, sc2, sh2, w_router_t, bias_col, tm):
    s, d = x1.shape
    e = N_EXPERTS
    nt = s // tm
    upper = (jnp.arange(tm)[:, None] < jnp.arange(tm)[None, :]).astype(BF16)
    const = lambda shape: pl.BlockSpec(shape, lambda i: (0, 0))
    per_tile = pl.BlockSpec((None, e, LANES), lambda i: (i, 0, 0))
    return pl.pallas_call(
        functools.partial(_router_kernel, tm=tm),
        grid=(nt,),
        in_specs=[pl.BlockSpec((tm, d), lambda i: (i, 0)),
                  const((1, d)), const((1, d)), const((1, d)),
                  const((e, d)), const((e, 1)), const((tm, tm))],
        out_specs=[pl.BlockSpec((TOP_K, tm), lambda i: (0, i)),
                   pl.BlockSpec((TOP_K, tm), lambda i: (0, i)),
                   per_tile, per_tile],
        out_shape=[jax.ShapeDtypeStruct((TOP_K, s), I32), jax.ShapeDtypeStruct((TOP_K, s), F32),
                   jax.ShapeDtypeStruct((nt, e, LANES), I32), jax.ShapeDtypeStruct((nt, e, LANES), I32)],
        scratch_shapes=[pltpu.VMEM((e, 1), F32)],
        compiler_params=_params(("arbitrary",)),
    )(x1, g2, sc2, sh2, w_router_t, bias_col, upper)


LONG_RUN = 64


def _run_sizes(limit):
    return [1 << b for b in range(limit.bit_length() - 1, -1, -1)]


def _for_each_run(tile, run_refs, tm, make_copy, fn):
    run_len_ref, run_off_ref, run_dst_ref = run_refs

    def per_expert(ex, carry):
        n = run_len_ref[tile * N_EXPERTS + ex]
        off = run_off_ref[tile * N_EXPERTS + ex]
        dst = run_dst_ref[tile * N_EXPERTS + ex]
        def pieces(sizes):
            for size in sizes:
                done = n & (-2 * size)

                @pl.when((n & size) != 0)
                def _(done=done, size=size):
                    fn(make_copy(off + done, dst + done, size))

        sizes = _run_sizes(tm)
        pieces([size for size in sizes if size < LONG_RUN])

        @pl.when(n >= LONG_RUN)
        def _():
            pieces([size for size in sizes if size >= LONG_RUN])

        return carry

    lax.fori_loop(0, N_EXPERTS, per_expert, 0)


def _slot_rows(slot, n_slots):
    return pl.ds(pl.multiple_of(slot * SUBLANES, SUBLANES), n_slots * SUBLANES)


def _dispatch_kernel(run_len_ref, run_off_ref, run_dst_ref, pad_lo_ref, pad_hi_ref, pos_ref, h_ref, xs_ref, stage, zero_scr,
                     sem, pad_sem, *, tm, rows_per_pass):
    step = pl.program_id(0)
    na = TOP_K * tm
    d = h_ref.shape[1]
    half = d // 2
    n_words = half // LANES

    def pad_copy(slot, n_slots):
        return pltpu.make_async_copy(zero_scr.at[pl.ds(0, n_slots * SUBLANES), :],
                                     xs_ref.at[_slot_rows(slot, n_slots), :], pad_sem)

    def for_each_pad(fn):
        def per_expert(ex, carry):
            slot = pad_lo_ref[ex]
            n = pad_hi_ref[ex] - slot
            for size in _run_sizes(EXPERT_BLOCK - 1):
                take = (n & size) != 0

                @pl.when(take)
                def _(slot=slot, size=size):
                    fn(pad_copy(slot, size))

                slot = slot + jnp.where(take, size, 0)
            return carry
        lax.fori_loop(0, N_EXPERTS, per_expert, 0)

    @pl.when(step == 0)
    def _():
        zero_scr[...] = jnp.zeros_like(zero_scr)
        for_each_pad(lambda cp: cp.start())

    buf = step % 2
    pos = pos_ref[...]
    h = h_ref[...]
    for a0 in range(0, na, rows_per_pass):
        slot_id = a0 + _iota2((rows_per_pass, tm), 0)
        hit = pos[0:1, :] == slot_id
        for k in range(1, TOP_K):
            hit = jnp.logical_or(hit, pos[k:k + 1, :] == slot_id)
        rows = _dot(hit.astype(BF16), h)
        for i in range(n_words):
            word = _pack_halves(rows[:, i * LANES:(i + 1) * LANES], rows[:, half + i * LANES:half + (i + 1) * LANES])
            stage[buf, pl.ds(a0 * SUBLANES + i, rows_per_pass, stride=SUBLANES), :] = word

    def run_copy(tile_slot, sorted_slot, n_slots):
        return pltpu.make_async_copy(stage.at[buf, _slot_rows(tile_slot, n_slots), :],
                                     xs_ref.at[_slot_rows(sorted_slot, n_slots), :], sem.at[buf])

    def wait_tile(which):
        pltpu.make_async_copy(stage.at[which], xs_ref.at[pl.ds(0, na * SUBLANES), :], sem.at[which]).wait()

    _for_each_run(step, (run_len_ref, run_off_ref, run_dst_ref), tm, run_copy, lambda cp: cp.start())

    @pl.when(step > 0)
    def _():
        wait_tile(1 - buf)

    @pl.when(step == pl.num_programs(0) - 1)
    def _():
        wait_tile(buf)

    @pl.when(step == 0)
    def _():
        for_each_pad(lambda cp: cp.wait())


def _dispatch(runs, pad_lo, pad_hi, pos_t, h2, n_slots, tm):
    s, d = h2.shape
    assert (d // 2) % LANES == 0 and (d // 2) // LANES == SUBLANES, "one token row must pack into one (8, 128) tile"
    na = TOP_K * tm
    return pl.pallas_call(
        functools.partial(_dispatch_kernel, tm=tm, rows_per_pass=min(512, na)),
        grid_spec=pltpu.PrefetchScalarGridSpec(
            num_scalar_prefetch=5,
            grid=(s // tm,),
            in_specs=[pl.BlockSpec((TOP_K, tm), lambda i, *_: (0, i)),
                      pl.BlockSpec((tm, d), lambda i, *_: (i, 0))],
            out_specs=pl.BlockSpec(memory_space=pl.ANY),
            scratch_shapes=[pltpu.VMEM((2, na * SUBLANES, LANES), U32),
                            pltpu.VMEM((EXPERT_BLOCK // 2 * SUBLANES, LANES), U32),
                            pltpu.SemaphoreType.DMA((2,)), pltpu.SemaphoreType.DMA(())]),
        out_shape=jax.ShapeDtypeStruct((n_slots * SUBLANES, LANES), U32),
        compiler_params=_params(("arbitrary",), has_side_effects=True, disable_bounds_checks=True),
    )(*runs, pad_lo, pad_hi, pos_t, h2)


def _expert_kernel(be_ref, nu_ref, next_ref, par_ref, x_ref, wg_hbm, wu_hbm, wd_hbm, y_ref, wg_f32, wu_f32, wd_f32,
                   wg_scr, wu_scr, wd_scr, sem):
    b = pl.program_id(0)
    bm = EXPERT_BLOCK
    active = b < nu_ref[0]
    new_expert = jnp.logical_or(b == 0, be_ref[b] != be_ref[jnp.maximum(b - 1, 0)])

    def weight_copies(ex, which):
        return [pltpu.make_async_copy(src.at[ex], dst.at[which], sem.at[which])
                for src, dst in ((wg_hbm, wg_f32), (wu_hbm, wu_f32), (wd_hbm, wd_f32))]

    @pl.when(jnp.logical_and(active, new_expert))
    def _():
        which = par_ref[b]

        @pl.when(b == 0)
        def _():
            for cp in weight_copies(be_ref[b], which):
                cp.start()

        for cp in weight_copies(be_ref[b], which):
            cp.wait()
        wg_scr[...] = wg_f32[which].astype(BF16)
        wu_scr[...] = wu_f32[which].astype(BF16)
        wd_scr[...] = wd_f32[which].astype(BF16)
        nb = next_ref[b]

        @pl.when(nb < nu_ref[0])
        def _():
            for cp in weight_copies(be_ref[nb], 1 - which):
                cp.start()

    @pl.when(active)
    def _():
        los, his = [], []
        for i in range(SUBLANES):
            lo, hi = _unpack_halves(x_ref[pl.ds(i, bm, stride=SUBLANES), :])
            los.append(lo.astype(BF16))
            his.append(hi.astype(BF16))
        xb = jnp.concatenate(los + his, axis=1)
        hid = _silu(_dot(xb, wg_scr[...])) * _dot(xb, wu_scr[...])
        y = _dot(hid.astype(BF16), wd_scr[...])
        half = y.shape[1] // 2
        for i in range(SUBLANES):
            word = _pack_halves(_round_bf16(y[:, i * LANES:(i + 1) * LANES]),
                                _round_bf16(y[:, half + i * LANES:half + (i + 1) * LANES]))
            y_ref[pl.ds(i, bm, stride=SUBLANES), :] = word


def _experts(block_e, n_used, next_block, parity, xs, w_gate, w_up, w_down):
    d, ff = w_gate.shape[1], w_gate.shape[2]
    bm = EXPERT_BLOCK
    n_blocks = xs.shape[0] // (bm * SUBLANES)
    blk = lambda b, be, nu, *_: (jnp.minimum(b, nu[0] - 1), 0)
    hbm = pl.BlockSpec(memory_space=pl.ANY)
    return pl.pallas_call(
        _expert_kernel,
        grid_spec=pltpu.PrefetchScalarGridSpec(
            num_scalar_prefetch=4,
            grid=(n_blocks,),
            in_specs=[pl.BlockSpec((bm * SUBLANES, LANES), blk), hbm, hbm, hbm],
            out_specs=pl.BlockSpec((bm * SUBLANES, LANES), blk),
            scratch_shapes=[pltpu.VMEM((2, d, ff), F32), pltpu.VMEM((2, d, ff), F32), pltpu.VMEM((2, ff, d), F32),
                            pltpu.VMEM((d, ff), BF16), pltpu.VMEM((d, ff), BF16), pltpu.VMEM((ff, d), BF16),
                            pltpu.SemaphoreType.DMA((2,))]),
        out_shape=jax.ShapeDtypeStruct(xs.shape, U32),
        compiler_params=_params(("arbitrary",)),
    )(block_e, n_used, next_block, parity, xs, w_gate, w_up, w_down)


def _combine_kernel(run_len_ref, run_off_ref, run_src_ref, ys_ref, h_ref, x1_ref, pos_ref, wts_ref, wg_ref, wu_ref, wd_ref, gt_ref, gf_ref,
                    o_ref, stage, sem, *, tm, rows_per_pass):
    step = pl.program_id(0)
    na = TOP_K * tm
    buf = step % 2

    def fetch_tile(tile, which):
        def run_copy(tile_slot, sorted_slot, n_slots):
            return pltpu.make_async_copy(ys_ref.at[_slot_rows(sorted_slot, n_slots), :],
                                         stage.at[which, _slot_rows(tile_slot, n_slots), :], sem.at[which])
        _for_each_run(tile, (run_len_ref, run_off_ref, run_src_ref), tm, run_copy, lambda cp: cp.start())

    @pl.when(step == 0)
    def _():
        fetch_tile(step, buf)

    @pl.when(step + 1 < pl.num_programs(0))
    def _():
        fetch_tile(step + 1, 1 - buf)

    hb = h_ref[...]
    hid = _silu(_dot(hb, wg_ref[...])) * _dot(hb, wu_ref[...])
    acc = _dot(hid.astype(BF16), wd_ref[...])

    pltpu.make_async_copy(ys_ref.at[pl.ds(0, na * SUBLANES), :], stage.at[buf], sem.at[buf]).wait()

    pos = pos_ref[...]
    wts = wts_ref[...]
    for a0 in range(0, na, rows_per_pass):
        los, his = [], []
        for i in range(SUBLANES):
            lo, hi = _unpack_halves(stage[buf, pl.ds(a0 * SUBLANES + i, rows_per_pass, stride=SUBLANES), :])
            los.append(lo.astype(BF16))
            his.append(hi.astype(BF16))
        y_rows = jnp.concatenate(los + his, axis=1)
        slot_id = a0 + _iota2((rows_per_pass, tm), 0)
        wmat = jnp.zeros((rows_per_pass, tm), F32)
        for k in range(TOP_K):
            wmat = wmat + jnp.where(pos[k:k + 1, :] == slot_id, wts[k:k + 1, :], 0.0)
        acc = acc + _dg(wmat.astype(BF16), y_rows, TN)
    x2 = x1_ref[...] + gt_ref[...] * acc
    o_ref[...] = _rms(x2, NORM_EPS) * gf_ref[...]


def _combine(runs, ys, h2, x1, pos_t, wts_t, w_gate, w_up, w_down, gt2, gf, tm):
    s, d = x1.shape
    ff = w_gate.shape[1]
    na = TOP_K * tm
    const = lambda shape: pl.BlockSpec(shape, lambda i, *_: (0, 0), pipeline_mode=pl.Buffered(1))
    tile = lambda cols: pl.BlockSpec((tm, cols), lambda i, *_: (i, 0))
    per_k = pl.BlockSpec((TOP_K, tm), lambda i, *_: (0, i))
    return pl.pallas_call(
        functools.partial(_combine_kernel, tm=tm, rows_per_pass=min(512, na)),
        grid_spec=pltpu.PrefetchScalarGridSpec(
            num_scalar_prefetch=3,
            grid=(s // tm,),
            in_specs=[pl.BlockSpec(memory_space=pl.ANY),
                      tile(d), tile(d), per_k, per_k,
                      const((d, ff)), const((d, ff)), const((ff, d)), const((1, d)), const((1, d))],
            out_specs=tile(d),
            scratch_shapes=[pltpu.VMEM((2, na * SUBLANES, LANES), U32), pltpu.SemaphoreType.DMA((2,))]),
        out_shape=jax.ShapeDtypeStruct((s, d), F32),
        compiler_params=_params(("arbitrary",), disable_bounds_checks=True),
    )(*runs, ys, h2, x1, pos_t, wts_t, w_gate, w_up, w_down, gt2, gf)


def _mixer(x2d, mod, norm1_g, norm2_g, w_in, lb, hgrn_onorm_g, gdn_conv_w, gdn_a_log, gdn_dt_bias, gdn_onorm_g,
           w_branch_hgrn, w_branch_gdn, w_out, tiles):
    d = x2d.shape[1]
    sh1, sc1, gt1, sh2, sc2, _ = [mod[:, i * d:(i + 1) * d] for i in range(6)]
    key = HEADS * HEAD_DIM
    small0 = 4 * key + 3 * key
    small1 = small0 + 2 * HEADS
    w_main = _wprep(w_in, small0, small1, tiles["wprep_tn"])
    w_small_t = w_in[0, :, small0:small1].T.astype(BF16)
    proj, ab_t = _inproj(x2d, norm1_g, sc1, sh1, w_main, w_small_t, tiles["in_tm"], tiles["in_tn"])
    o_a = _hgrn(proj, lb, hgrn_onorm_g, tiles["mix_ts"])
    u, wqd, ku, attn, dl = _gdn_prep(proj, gdn_conv_w, ab_t, gdn_a_log, gdn_dt_bias, tiles["prep_ts"])
    o_b = _gdn_scan(u, wqd, ku, attn, dl, proj, gdn_onorm_g, tiles["mix_ts"])
    return _merge(o_a, o_b, proj, x2d, w_branch_hgrn.astype(BF16), w_branch_gdn.astype(BF16),
                  w_out.astype(BF16), gt1, norm2_g, sc2, sh2, tiles["merge_tm"])


def _moe(x1, h2, mod, norm2_g, normf_g, w_router, router_bias, w_exp_gate, w_exp_up, w_exp_down, w_sh_gate,
         w_sh_up, w_sh_down, tiles):
    s, d = x1.shape
    tm = tiles["moe_tm"]
    sh2, sc2, gt2 = [mod[:, i * d:(i + 1) * d] for i in (3, 4, 5)]
    pos_t, wts_t, before, ntile = _router(x1, norm2_g, sc2, sh2, w_router.T, router_bias.reshape(-1, 1), tm)
    bm = EXPERT_BLOCK
    n_blocks = -(-(s * TOP_K + N_EXPERTS * (bm - 1)) // bm)
    before = before[:, :, 0]
    ntile = ntile[:, :, 0]
    counts = before[-1] + ntile[-1]
    padded = (counts + bm - 1) // bm * bm
    earlier = jnp.arange(N_EXPERTS)[None, :] < jnp.arange(N_EXPERTS)[:, None]
    pstart = jnp.sum(jnp.where(earlier, padded[None, :], 0), axis=1).astype(I32)
    pend = pstart + padded
    block_start = jnp.arange(n_blocks, dtype=I32) * bm
    block_e = jnp.minimum(jnp.sum(pend[None, :] <= block_start[:, None], axis=1), N_EXPERTS - 1).astype(I32)
    n_used = pend[-1:] // bm
    run_off = jnp.sum(jnp.where(earlier[None], ntile[:, None, :], 0), axis=2)
    runs = (ntile.reshape(-1), run_off.reshape(-1), (before + pstart[None, :]).reshape(-1))
    xs = _dispatch(runs, pstart + counts, pend, pos_t, h2, n_blocks * bm, tm)
    own = block_e[:, None] == jnp.arange(N_EXPERTS)[None, :]
    next_block = jnp.sum(jnp.where(own, pend[None, :], 0), axis=1) // bm
    switches = jnp.concatenate([jnp.zeros((1,), I32), (block_e[1:] != block_e[:-1]).astype(I32)])
    upto = jnp.arange(n_blocks)[None, :] <= jnp.arange(n_blocks)[:, None]
    parity = jnp.sum(jnp.where(upto, switches[None, :], 0), axis=1).astype(I32) % 2
    ys = _experts(block_e, n_used, next_block, parity, xs, w_exp_gate, w_exp_up, w_exp_down)
    return _combine(runs, ys, h2, x1, pos_t, wts_t, w_sh_gate.astype(BF16), w_sh_up.astype(BF16),
                    w_sh_down.astype(BF16), gt2, normf_g, tm)


def _tiles(s):
    pick = lambda want: min(want, s)
    return dict(wprep_tn=512, in_tm=pick(1024), in_tn=1536, mix_ts=pick(512), prep_ts=pick(2048), merge_tm=pick(512),
                moe_tm=pick(256))


def kernel(x, c, w_ada, b_ada, norm1_g, norm2_g, w_in, hgrn_lb_table, hgrn_onorm_g, gdn_conv_w, gdn_a_log, gdn_dt_bias, gdn_onorm_g, w_branch_hgrn, w_branch_gdn, w_out, w_router, router_bias, w_exp_gate, w_exp_up, w_exp_down, w_sh_gate, w_sh_up, w_sh_down, normf_g):
    b, s, d = x.shape
    assert b == 1 and w_ada.shape[0] == 1, "one sequence, one layer"
    tiles = _tiles(s)
    lb = jnp.sum(jax.nn.softmax(hgrn_lb_table.astype(F32), axis=0)[0:1], axis=0, keepdims=True)
    mod = _ada(c, w_ada[0], b_ada[0])
    row = lambda v: v.reshape(1, -1)
    x1, h2 = _mixer(x[0], mod, row(norm1_g[0]), row(norm2_g[0]), w_in, lb, row(hgrn_onorm_g[0]), gdn_conv_w[0],
                    gdn_a_log[0], gdn_dt_bias[0], row(gdn_onorm_g[0]), w_branch_hgrn[0], w_branch_gdn[0], w_out[0],
                    tiles)
    out = _moe(x1, h2, mod, row(norm2_g[0]), row(normf_g), w_router[0], router_bias[0], w_exp_gate[0],
               w_exp_up[0], w_exp_down[0], w_sh_gate[0], w_sh_up[0], w_sh_down[0], tiles)
    return out[None]
```

```python
import functools

import jax
import jax.numpy as jnp
from jax import lax
from jax.experimental import pallas as pl
from jax.experimental.pallas import tpu as pltpu

F32 = jnp.float32
BF16 = jnp.bfloat16
I32 = jnp.int32
U32 = jnp.uint32

NORM_EPS = 1e-6
L2_EPS = 1e-6
HEADS = 8
HEAD_DIM = 128
CONV_WIDTH = 4
CHUNK = 64
N_EXPERTS = 64
N_GROUPS = 8
GROUP_SIZE = N_EXPERTS // N_GROUPS
TOPK_GROUPS = 4
TOP_K = 8
ROUTED_SCALE = 2.5
EXPERT_BLOCK = 512

LANES = 128
SUBLANES = 8
VMEM_LIMIT = 56 * 1024 * 1024

NT = (((1,), (1,)), ((), ()))
TN = (((0,), (0,)), ((), ()))


def _params(sem, **kw):
    return pltpu.CompilerParams(dimension_semantics=sem, vmem_limit_bytes=VMEM_LIMIT, **kw)


def _dot(a, b):
    return jnp.dot(a, b, preferred_element_type=F32)


def _dg(a, b, dims):
    return lax.dot_general(a, b, dims, preferred_element_type=F32)


def _split(x):
    hi = x.astype(BF16)
    lo = (x - hi.astype(F32)).astype(BF16)
    return hi, lo


def _dot_exact_lhs(a_bf16, x, dims=None):
    hi, lo = _split(x)
    if dims is None:
        return _dot(a_bf16, hi) + _dot(a_bf16, lo)
    return _dg(a_bf16, hi, dims) + _dg(a_bf16, lo, dims)


def _sigmoid(x):
    return 1.0 / (1.0 + jnp.exp(-x))


def _silu(x):
    return x * _sigmoid(x)


def _rms(x, eps):
    return x * lax.rsqrt(jnp.mean(x * x, axis=-1, keepdims=True) + eps)


def _iota2(shape, dim):
    return lax.broadcasted_iota(I32, shape, dim)


def _pack_halves(lo, hi):
    lo_bits = lax.shift_right_logical(pltpu.bitcast(lo, U32), U32(16))
    hi_bits = pltpu.bitcast(hi, U32) & U32(0xFFFF0000)
    return lo_bits | hi_bits


def _unpack_halves(word):
    lo = pltpu.bitcast(lax.shift_left(word, U32(16)), F32)
    hi = pltpu.bitcast(word & U32(0xFFFF0000), F32)
    return lo, hi


def _round_bf16(x):
    return x.astype(BF16).astype(F32)


def _ada_kernel(c_ref, w_ref, b_ref, o_ref):
    cond = _silu(c_ref[...])
    o_ref[...] = jnp.dot(cond, w_ref[...], preferred_element_type=F32,
                         precision=lax.Precision.HIGHEST) + b_ref[...]


def _ada(c, w_ada, b_ada):
    d, n = w_ada.shape
    tn = 1024
    c8 = jnp.broadcast_to(c, (SUBLANES, d))
    out = pl.pallas_call(
        _ada_kernel,
        grid=(n // tn,),
        in_specs=[pl.BlockSpec((SUBLANES, d), lambda j: (0, 0)),
                  pl.BlockSpec((d, tn), lambda j: (0, j)),
                  pl.BlockSpec((1, tn), lambda j: (0, j))],
        out_specs=pl.BlockSpec((SUBLANES, tn), lambda j: (0, j)),
        out_shape=jax.ShapeDtypeStruct((SUBLANES, n), F32),
        compiler_params=_params(("arbitrary",)),
    )(c8, w_ada, b_ada.reshape(1, n))
    return out[0:1]


def _wprep_kernel(a_ref, b_ref, o_ref, *, first_shifted, shift):
    j = pl.program_id(0)

    @pl.when(j < first_shifted)
    def _():
        o_ref[...] = a_ref[...].astype(BF16)

    @pl.when(j >= first_shifted)
    def _():
        tn = a_ref.shape[0]
        o_ref[0:tn - shift, :] = a_ref[shift:tn, :].astype(BF16)
        o_ref[tn - shift:tn, :] = b_ref[...].astype(BF16)


def _wprep(w_in_t, cut0, cut1, tn):
    _, n_in, d = w_in_t.shape
    shift = cut1 - cut0
    n_out = n_in - shift
    assert cut0 % tn == 0 and n_out % tn == 0 and tn % shift == 0 and shift % (2 * SUBLANES) == 0
    return pl.pallas_call(
        functools.partial(_wprep_kernel, first_shifted=cut0 // tn, shift=shift),
        grid=(n_out // tn,),
        in_specs=[pl.BlockSpec((None, tn, d), lambda j: (0, j, 0)),
                  pl.BlockSpec((None, shift, d), lambda j: (0, (j + 1) * (tn // shift), 0))],
        out_specs=pl.BlockSpec((tn, d), lambda j: (j, 0)),
        out_shape=jax.ShapeDtypeStruct((n_out, d), BF16),
        compiler_params=_params(("arbitrary",)),
    )(w_in_t, w_in_t)


def _inproj_kernel(x_ref, g_ref, sc_ref, sh_ref, w_ref, wst_ref, proj_ref, smallt_ref, h_scr):
    @pl.when(pl.program_id(1) == 0)
    def _():
        h = _rms(x_ref[...], NORM_EPS) * g_ref[...] * (1.0 + sc_ref[...]) + sh_ref[...]
        hb = h.astype(BF16)
        h_scr[...] = hb
        smallt_ref[...] = _dg(wst_ref[...], hb, NT)

    proj_ref[...] = _dg(h_scr[...], w_ref[...], NT).astype(BF16)


def _inproj(x, g, sc, sh, w_main_t, w_small_t, tm, tn):
    s, d = x.shape
    n = w_main_t.shape[0]
    ns = w_small_t.shape[0]
    row = lambda i, j: (0, 0)
    return pl.pallas_call(
        _inproj_kernel,
        grid=(s // tm, n // tn),
        in_specs=[pl.BlockSpec((tm, d), lambda i, j: (i, 0)),
                  pl.BlockSpec((1, d), row), pl.BlockSpec((1, d), row), pl.BlockSpec((1, d), row),
                  pl.BlockSpec((tn, d), lambda i, j: (j, 0)),
                  pl.BlockSpec((ns, d), row)],
        out_specs=[pl.BlockSpec((tm, tn), lambda i, j: (i, j)),
                   pl.BlockSpec((ns, tm), lambda i, j: (0, i))],
        out_shape=[jax.ShapeDtypeStruct((s, n), BF16), jax.ShapeDtypeStruct((ns, s), F32)],
        scratch_shapes=[pltpu.VMEM((tm, d), BF16)],
        compiler_params=_params(("arbitrary", "arbitrary")),
    )(x, g, sc, sh, w_main_t, w_small_t)


def _hgrn_kernel(q_ref, f_ref, i_ref, g_ref, lb_ref, on_ref, o_ref, st_scr, *, n_chunks):
    @pl.when(pl.program_id(0) == 0)
    def _():
        st_scr[...] = jnp.zeros_like(st_scr)

    c = CHUNK
    hd = HEAD_DIM
    causal = _iota2((c, c), 1) <= _iota2((c, c), 0)
    tri = causal.astype(BF16)
    lb = lb_ref[...]
    on_g = on_ref[...]
    heads = [slice(h * hd, (h + 1) * hd) for h in range(HEADS)]

    def chunk(n, carry):
        rows = pl.ds(pl.multiple_of(n * c, c), c)
        f = lb + (1.0 - lb) * _sigmoid(f_ref[rows, :].astype(F32))
        b = _dot_exact_lhs(tri, jnp.log(f))
        k = 1.0 - f
        q = _silu(q_ref[rows, :].astype(F32)) * (hd ** -0.5)
        v = i_ref[rows, :]
        b_mid = b[c // 2:c // 2 + 1, :]
        b_last = b[c - 1:c, :]
        qa = (q * jnp.exp(b - b_mid)).astype(BF16)
        ka = (k * jnp.exp(b_mid - b)).astype(BF16)
        qi = (q * jnp.exp(b)).astype(BF16)
        ku = (k * jnp.exp(b_last - b)).astype(BF16)
        dec = jnp.exp(b_last)
        gate = on_g * _silu(g_ref[rows, :].astype(F32))
        sts = [st_scr[h] for h in range(HEADS)]
        scores = [jnp.where(causal, _dg(qa[:, sl], ka[:, sl], NT), 0.0).astype(BF16) for sl in heads]
        inter = [_dg(qi[:, sl], st.astype(BF16), NT) for sl, st in zip(heads, sts)]
        kv = [_dg(v[:, sl], ku[:, sl], TN) for sl in heads]
        for h, sl in enumerate(heads):
            st_scr[h] = dec[:, sl] * sts[h] + kv[h]
        outs = [_rms(_dot(sc, v[:, sl]) + it, NORM_EPS) for sc, sl, it in zip(scores, heads, inter)]
        o_ref[rows, :] = (jnp.concatenate(outs, axis=1) * gate).astype(BF16)
        return carry

    lax.fori_loop(0, n_chunks, chunk, 0)


def _hgrn(proj, lb, onorm_g, ts):
    s = proj.shape[0]
    width = HEADS * HEAD_DIM
    col = lambda blk: pl.BlockSpec((ts, width), lambda j, blk=blk: (j, blk))
    const = pl.BlockSpec((1, width), lambda j: (0, 0))
    return pl.pallas_call(
        functools.partial(_hgrn_kernel, n_chunks=ts // CHUNK),
        grid=(s // ts,),
        in_specs=[col(0), col(1), col(2), col(3), const, const],
        out_specs=pl.BlockSpec((ts, width), lambda j: (j, 0)),
        out_shape=jax.ShapeDtypeStruct((s, width), BF16),
        scratch_shapes=[pltpu.VMEM((HEADS, HEAD_DIM, HEAD_DIM), F32)],
        compiler_params=_params(("arbitrary",)),
    )(proj, proj, proj, proj, lb, jnp.tile(onorm_g, (1, HEADS)))


def _gdn_prep_kernel(q_ref, k_ref, v_ref, qp_ref, kp_ref, vp_ref, wq_ref, wk_ref, wv_ref, ab_ref, alog_ref,
                     dtb_ref, tri_ref, eye_ref, u_ref, wqd_ref, ku_ref, attn_ref, dl_ref, cat_scr, rows_scr, cols_scr,
                     *, n_chunks, ts):
    h = pl.program_id(1)
    first = pl.program_id(0) == 0
    c = CHUNK
    hd = HEAD_DIM

    def conv_silu(cur_ref, prev_ref, w_ref):
        cat_scr[0:8, :] = jnp.where(first, 0.0, prev_ref[...].astype(F32))
        cat_scr[8:8 + ts, :] = cur_ref[...].astype(F32)
        acc = None
        for j in range(CONV_WIDTH):
            off = 8 - (CONV_WIDTH - 1) + j
            term = cat_scr[off:off + ts, :] * w_ref[j:j + 1, :]
            acc = term if acc is None else acc + term
        return _silu(acc)

    def l2n(x):
        return x * lax.rsqrt(jnp.sum(x * x, axis=-1, keepdims=True) + L2_EPS)

    q_all = l2n(conv_silu(q_ref, qp_ref, wq_ref)) * (hd ** -0.5)
    k_all = l2n(conv_silu(k_ref, kp_ref, wk_ref))
    v_all = conv_silu(v_ref, vp_ref, wv_ref)

    @pl.when(h == 0)
    def _():
        z = ab_ref[0:HEADS, :] + dtb_ref[...]
        softplus = jnp.maximum(z, 0.0) + jnp.log(1.0 + jnp.exp(-jnp.abs(z)))
        ld_rows = -jnp.exp(alog_ref[...]) * softplus
        hi, lo = _split(ld_rows)
        tri_blocks = tri_ref[...]
        g_rows = _dg(hi, tri_blocks, NT) + _dg(lo, tri_blocks, NT)
        beta_rows = _sigmoid(ab_ref[HEADS:2 * HEADS, :])
        rows_scr[...] = g_rows
        rows = jnp.concatenate([g_rows, beta_rows, jnp.zeros((LANES - 2 * HEADS, ts), F32)], axis=0)
        r_hi, r_lo = _split(rows)
        r_lo2 = (rows - r_hi.astype(F32) - r_lo.astype(F32)).astype(BF16)
        eye_ts = eye_ref[...]
        cols_scr[...] = _dg(eye_ts, r_hi, NT) + _dg(eye_ts, r_lo, NT) + _dg(eye_ts, r_lo2, NT)

    lane = _iota2((ts, LANES), 1)
    cols = cols_scr[...]
    gc_all = jnp.sum(jnp.where(lane == h, cols, 0.0), axis=1, keepdims=True)
    bc_all = jnp.sum(jnp.where(lane == h + HEADS, cols, 0.0), axis=1, keepdims=True)
    g_row = rows_scr[pl.ds(h, 1), :]
    egc_all = jnp.exp(gc_all)

    r = _iota2((c, c), 0)
    cidx = _iota2((c, c), 1)
    causal = cidx <= r
    strict = cidx < r
    eye_f = (r == cidx).astype(F32)
    chunks = [slice(n * c, (n + 1) * c) for n in range(n_chunks)]

    q16 = q_all.astype(BF16)
    k16 = k_all.astype(BF16)
    kq = [_dg(jnp.concatenate([k16[sl], q16[sl]], axis=0), k16[sl], NT) for sl in chunks]
    dm = []
    for sl in chunks:
        diff = gc_all[sl] - g_row[:, sl]
        dm.append(jnp.where(causal, jnp.exp(jnp.where(causal, diff, 0.0)), 0.0))
    bm = [-jnp.where(strict, bc_all[sl] * x[0:c] * d, 0.0) for sl, x, d in zip(chunks, kq, dm)]
    p = [eye_f + b for b in bm]
    bm = [_dot(b.astype(BF16), b.astype(BF16)) for b in bm]
    for _ in range(c.bit_length() - 3):
        res = [_dot(b.astype(BF16), jnp.concatenate([b, pp], axis=1).astype(BF16)) for b, pp in zip(bm, p)]
        p = [pp + x[:, c:2 * c] for pp, x in zip(p, res)]
        bm = [x[:, 0:c] for x in res]
    p = [pp + _dot(b.astype(BF16), pp.astype(BF16)) for b, pp in zip(bm, p)]
    rhs = jnp.concatenate([v_all * bc_all, k_all * (bc_all * egc_all)], axis=1).astype(BF16)
    sol = [_dot(pp.astype(BF16), rhs[sl]) for pp, sl in zip(p, chunks)]
    qd_all = (q_all * egc_all).astype(BF16)
    for n, sl in enumerate(chunks):
        g_last = gc_all[(n + 1) * c - 1:(n + 1) * c, :]
        u_ref[sl, :] = sol[n][:, 0:hd].astype(BF16)
        wqd_ref[2 * n * c:(2 * n + 1) * c, :] = sol[n][:, hd:2 * hd].astype(BF16)
        wqd_ref[(2 * n + 1) * c:(2 * n + 2) * c, :] = qd_all[sl]
        ku_ref[sl, :] = (k_all[sl] * jnp.exp(g_last - gc_all[sl])).astype(BF16)
        attn_ref[sl, :] = (kq[n][c:2 * c] * dm[n]).astype(BF16)
        dl_ref[n:n + 1, :] = jnp.broadcast_to(jnp.exp(g_last), (1, hd))


def _gdn_prep(proj, conv_w, ab_t, a_log, dt_bias, ts):
    s = proj.shape[0]
    hd = HEAD_DIM
    c = CHUNK
    q0 = 4 * HEADS
    cur = lambda off: pl.BlockSpec((ts, hd), lambda j, h, off=off: (j, off + h))
    prev = lambda off: pl.BlockSpec((8, hd), lambda j, h, off=off: (jnp.maximum(j * (ts // 8) - 1, 0), off + h))
    cw = lambda off: pl.BlockSpec((CONV_WIDTH, hd), lambda j, h, off=off: (0, off + h))
    per_head_scalar = pl.BlockSpec((HEADS, 1), lambda j, h: (0, 0))
    const = pl.BlockSpec((ts, ts), lambda j, h: (0, 0), pipeline_mode=pl.Buffered(1))
    pos = jnp.arange(ts)
    tri_blocks = ((pos[:, None] // c == pos[None, :] // c) & (pos[None, :] <= pos[:, None])).astype(BF16)
    eye = (pos[:, None] == pos[None, :]).astype(BF16)
    per_head = lambda rows, cols: pl.BlockSpec((None, rows, cols), lambda j, h: (h, j, 0))
    return pl.pallas_call(
        functools.partial(_gdn_prep_kernel, n_chunks=ts // c, ts=ts),
        grid=(s // ts, HEADS),
        in_specs=[cur(q0), cur(q0 + HEADS), cur(q0 + 2 * HEADS),
                  prev(q0), prev(q0 + HEADS), prev(q0 + 2 * HEADS),
                  cw(0), cw(HEADS), cw(2 * HEADS),
                  pl.BlockSpec((2 * HEADS, ts), lambda j, h: (0, j)),
                  per_head_scalar, per_head_scalar, const, const],
        out_specs=[pl.BlockSpec((ts, hd), lambda j, h: (j, h)),
                   pl.BlockSpec((2 * ts, hd), lambda j, h: (j, h)),
                   pl.BlockSpec((ts, hd), lambda j, h: (j, h)),
                   per_head(ts, c),
                   per_head(ts // c, hd)],
        out_shape=[jax.ShapeDtypeStruct((s, HEADS * hd), BF16),
                   jax.ShapeDtypeStruct((2 * s, HEADS * hd), BF16),
                   jax.ShapeDtypeStruct((s, HEADS * hd), BF16),
                   jax.ShapeDtypeStruct((HEADS, s, c), BF16),
                   jax.ShapeDtypeStruct((HEADS, s // c, hd), F32)],
        scratch_shapes=[pltpu.VMEM((ts + 8, hd), F32), pltpu.VMEM((HEADS, ts), F32), pltpu.VMEM((ts, LANES), F32)],
        compiler_params=_params(("arbitrary", "arbitrary")),
    )(proj, proj, proj, proj, proj, proj, conv_w, conv_w, conv_w,
      ab_t, a_log.reshape(HEADS, 1), dt_bias.reshape(HEADS, 1), tri_blocks, eye)


def _gdn_scan_kernel(u_ref, wqd_ref, ku_ref, attn_ref, dl_ref, g_ref, on_ref, o_ref, st_scr, *, n_chunks):
    @pl.when(pl.program_id(0) == 0)
    def _():
        st_scr[...] = jnp.zeros_like(st_scr)

    c = CHUNK
    hd = HEAD_DIM
    on_g = on_ref[...]
    heads = [slice(h * hd, (h + 1) * hd) for h in range(HEADS)]

    def chunk(n, carry):
        rows = pl.ds(pl.multiple_of(n * c, c), c)
        rows2 = pl.ds(pl.multiple_of(2 * n * c, 2 * c), 2 * c)
        sts = [st_scr[h] for h in range(HEADS)]
        wq = [_dot(wqd_ref[rows2, sl], st.astype(BF16)) for sl, st in zip(heads, sts)]
        vn = [(u_ref[rows, sl].astype(F32) - x[0:c]).astype(BF16) for sl, x in zip(heads, wq)]
        upd = [_dg(ku_ref[rows, sl], v, TN) for sl, v in zip(heads, vn)]
        for h in range(HEADS):
            st_scr[h] = dl_ref[h, pl.ds(n, 1), :] * sts[h] + upd[h]
        outs = [_rms(x[c:2 * c] + _dot(attn_ref[h, rows, :], v), NORM_EPS)
                for h, (x, v) in enumerate(zip(wq, vn))]
        gate = jnp.tile(on_g, (1, HEADS)) * _silu(g_ref[rows, :].astype(F32))
        o_ref[rows, :] = (jnp.concatenate(outs, axis=1) * gate).astype(BF16)
        return carry

    lax.fori_loop(0, n_chunks, chunk, 0)


def _gdn_scan(u, wqd, ku, attn, dl, proj, onorm_g, ts):
    s, width = u.shape
    c = CHUNK
    gate_blk = (4 * HEADS + 3 * HEADS) * HEAD_DIM // width
    return pl.pallas_call(
        functools.partial(_gdn_scan_kernel, n_chunks=ts // c),
        grid=(s // ts,),
        in_specs=[pl.BlockSpec((ts, width), lambda j: (j, 0)),
                  pl.BlockSpec((2 * ts, width), lambda j: (j, 0)),
                  pl.BlockSpec((ts, width), lambda j: (j, 0)),
                  pl.BlockSpec((HEADS, ts, c), lambda j: (0, j, 0)),
                  pl.BlockSpec((HEADS, ts // c, HEAD_DIM), lambda j: (0, j, 0)),
                  pl.BlockSpec((ts, width), lambda j: (j, gate_blk)),
                  pl.BlockSpec((1, HEAD_DIM), lambda j: (0, 0))],
        out_specs=pl.BlockSpec((ts, width), lambda j: (j, 0)),
        out_shape=jax.ShapeDtypeStruct((s, width), BF16),
        scratch_shapes=[pltpu.VMEM((HEADS, HEAD_DIM, HEAD_DIM), F32)],
        compiler_params=_params(("arbitrary",)),
    )(u, wqd, ku, attn, dl, proj, onorm_g)


def _merge_kernel(oa_ref, ob_ref, mga_ref, mgb_ref, x_ref, wa_ref, wb_ref, wo_ref, gt_ref, g2_ref, sc_ref,
                  sh_ref, x1_ref, h2_ref):
    ya = _dot(oa_ref[...], wa_ref[...])
    yb = _dot(ob_ref[...], wb_ref[...])
    merged = _sigmoid(mga_ref[...].astype(F32)) * ya + _sigmoid(mgb_ref[...].astype(F32)) * yb
    x1 = x_ref[...] + gt_ref[...] * _dot(merged.astype(BF16), wo_ref[...])
    x1_ref[...] = x1
    h2 = _rms(x1, NORM_EPS) * g2_ref[...] * (1.0 + sc_ref[...]) + sh_ref[...]
    h2_ref[...] = h2.astype(BF16)


def _merge(o_a, o_b, proj, x, w_a, w_b, w_o, gt1, g2, sc2, sh2, tm):
    s, d = x.shape
    dv = o_a.shape[1]
    mg0 = (8 * HEADS * HEAD_DIM) // d
    const = lambda shape: pl.BlockSpec(shape, lambda i: (0, 0), pipeline_mode=pl.Buffered(1))
    return pl.pallas_call(
        _merge_kernel,
        grid=(s // tm,),
        in_specs=[pl.BlockSpec((tm, dv), lambda i: (i, 0)),
                  pl.BlockSpec((tm, dv), lambda i: (i, 0)),
                  pl.BlockSpec((tm, d), lambda i: (i, mg0)),
                  pl.BlockSpec((tm, d), lambda i: (i, mg0 + 1)),
                  pl.BlockSpec((tm, d), lambda i: (i, 0)),
                  const((dv, d)), const((dv, d)), const((d, d)),
                  const((1, d)), const((1, d)), const((1, d)), const((1, d))],
        out_specs=[pl.BlockSpec((tm, d), lambda i: (i, 0)), pl.BlockSpec((tm, d), lambda i: (i, 0))],
        out_shape=[jax.ShapeDtypeStruct((s, d), F32), jax.ShapeDtypeStruct((s, d), BF16)],
        compiler_params=_params(("arbitrary",)),
    )(o_a, o_b, proj, proj, x, w_a, w_b, w_o, gt1, g2, sc2, sh2)


def _first_max(vals, iota, size, axis):
    m = jnp.max(vals, axis=axis, keepdims=True)
    idx = jnp.min(jnp.where(vals == m, iota, size), axis=axis, keepdims=True)
    return m, idx


def _router_kernel(x1_ref, g2_ref, sc_ref, sh_ref, wrt_ref, bias_ref, upper_ref, pos_ref, wts_ref, before_ref,
                   ntile_ref, cnt_scr, *, tm):
    @pl.when(pl.program_id(0) == 0)
    def _():
        cnt_scr[...] = jnp.zeros_like(cnt_scr)

    e = N_EXPERTS
    h2 = _rms(x1_ref[...], NORM_EPS) * g2_ref[...] * (1.0 + sc_ref[...]) + sh_ref[...]
    logits = lax.dot_general(wrt_ref[...], h2, NT, preferred_element_type=F32,
                             precision=lax.Precision.HIGHEST)
    scores = _sigmoid(logits)
    biased = scores + bias_ref[...]
    neg = -jnp.inf

    g3 = biased.reshape(N_GROUPS, GROUP_SIZE, tm)
    i3 = lax.broadcasted_iota(I32, g3.shape, 1)
    m1, a1 = _first_max(g3, i3, GROUP_SIZE, 1)
    m2 = jnp.max(jnp.where(i3 == a1, neg, g3), axis=1, keepdims=True)
    gs = (m1 + m2).reshape(N_GROUPS, tm)
    ig = _iota2(gs.shape, 0)
    gmask = jnp.zeros(gs.shape, jnp.bool_)
    for _ in range(TOPK_GROUPS):
        _, a = _first_max(gs, ig, N_GROUPS, 0)
        pick = ig == a
        gmask = jnp.logical_or(gmask, pick)
        gs = jnp.where(pick, neg, gs)
    emask = jnp.broadcast_to(gmask.reshape(N_GROUPS, 1, tm), (N_GROUPS, GROUP_SIZE, tm)).reshape(e, tm)

    cand = jnp.where(emask, biased, neg)
    ie = _iota2((e, tm), 0)
    sel_all = jnp.zeros((e, tm), jnp.bool_)
    w_rows, picks = [], []
    for _ in range(TOP_K):
        _, a = _first_max(cand, ie, e, 0)
        pick = ie == a
        picks.append(pick)
        w_rows.append(jnp.sum(jnp.where(pick, scores, 0.0), axis=0, keepdims=True))
        sel_all = jnp.logical_or(sel_all, pick)
        cand = jnp.where(pick, neg, cand)
    w_sum = w_rows[0]
    for wr in w_rows[1:]:
        w_sum = w_sum + wr
    wts = jnp.concatenate(w_rows, axis=0) / w_sum * ROUTED_SCALE

    sel = sel_all.astype(BF16)
    in_expert = _dot(sel, upper_ref[...])
    n_tile = jnp.sum(sel_all.astype(F32), axis=1, keepdims=True)
    lower = (_iota2((e, e), 1) < _iota2((e, e), 0)).astype(BF16)
    expert_off = _dot_exact_lhs(lower, jnp.broadcast_to(n_tile, (e, LANES)))[:, 0:1]
    place = in_expert + expert_off
    pos = jnp.concatenate([jnp.sum(jnp.where(pk, place, 0.0), axis=0, keepdims=True) for pk in picks], axis=0)
    pos_ref[...] = pos.astype(I32)
    before_ref[...] = jnp.broadcast_to(cnt_scr[...], before_ref.shape).astype(I32)
    ntile_ref[...] = jnp.broadcast_to(n_tile, ntile_ref.shape).astype(I32)
    cnt_scr[...] = cnt_scr[...] + n_tile
    wts_ref[...] = wts


def _router(x1, g2, sc2, sh2, w_router_t, bias_col, tm):
    s, d = x1.shape
    e = N_EXPERTS
    nt = s // tm
    upper = (jnp.arange(tm)[:, None] < jnp.arange(tm)[None, :]).astype(BF16)
    const = lambda shape: pl.BlockSpec(shape, lambda i: (0, 0))
    per_tile = pl.BlockSpec((None, e, LANES), lambda i: (i, 0, 0))
    return pl.pallas_call(
        functools.partial(_router_kernel, tm=tm),
        grid=(nt,),
        in_specs=[pl.BlockSpec((tm, d), lambda i: (i, 0)),
                  const((1, d)), const((1, d)), const((1, d)),
                  const((e, d)), const((e, 1)), const((tm, tm))],
        out_specs=[pl.BlockSpec((TOP_K, tm), lambda i: (0, i)),
                   pl.BlockSpec((TOP_K, tm), lambda i: (0, i)),
                   per_tile, per_tile],
        out_shape=[jax.ShapeDtypeStruct((TOP_K, s), I32), jax.ShapeDtypeStruct((TOP_K, s), F32),
                   jax.ShapeDtypeStruct((nt, e, LANES), I32), jax.ShapeDtypeStruct((nt, e, LANES), I32)],
        scratch_shapes=[pltpu.VMEM((e, 1), F32)],
        compiler_params=_params(("arbitrary",)),
    )(x1, g2, sc2, sh2, w_router_t, bias_col, upper)


LONG_RUN = 64


def _run_sizes(limit):
    return [1 << b for b in range(limit.bit_length() - 1, -1, -1)]


def _for_each_run(tile, run_refs, tm, make_copy, fn):
    run_len_ref, run_off_ref, run_dst_ref = run_refs

    def per_expert(ex, carry):
        n = run_len_ref[tile * N_EXPERTS + ex]
        off = run_off_ref[tile * N_EXPERTS + ex]
        dst = run_dst_ref[tile * N_EXPERTS + ex]
        def pieces(sizes):
            for size in sizes:
                done = n & (-2 * size)

                @pl.when((n & size) != 0)
                def _(done=done, size=size):
                    fn(make_copy(off + done, dst + done, size))

        sizes = _run_sizes(tm)
        pieces([size for size in sizes if size < LONG_RUN])

        @pl.when(n >= LONG_RUN)
        def _():
            pieces([size for size in sizes if size >= LONG_RUN])

        return carry

    lax.fori_loop(0, N_EXPERTS, per_expert, 0)


def _slot_rows(slot, n_slots):
    return pl.ds(pl.multiple_of(slot * SUBLANES, SUBLANES), n_slots * SUBLANES)


def _dispatch_kernel(run_len_ref, run_off_ref, run_dst_ref, pad_lo_ref, pad_hi_ref, pos_ref, h_ref, xs_ref, stage, zero_scr,
                     sem, pad_sem, *, tm, rows_per_pass):
    step = pl.program_id(0)
    na = TOP_K * tm
    d = h_ref.shape[1]
    half = d // 2
    n_words = half // LANES

    def pad_copy(slot, n_slots):
        return pltpu.make_async_copy(zero_scr.at[pl.ds(0, n_slots * SUBLANES), :],
                                     xs_ref.at[_slot_rows(slot, n_slots), :], pad_sem)

    def for_each_pad(fn):
        def per_expert(ex, carry):
            slot = pad_lo_ref[ex]
            n = pad_hi_ref[ex] - slot
            for size in _run_sizes(EXPERT_BLOCK - 1):
                take = (n & size) != 0

                @pl.when(take)
                def _(slot=slot, size=size):
                    fn(pad_copy(slot, size))

                slot = slot + jnp.where(take, size, 0)
            return carry
        lax.fori_loop(0, N_EXPERTS, per_expert, 0)

    @pl.when(step == 0)
    def _():
        zero_scr[...] = jnp.zeros_like(zero_scr)
        for_each_pad(lambda cp: cp.start())

    buf = step % 2
    pos = pos_ref[...]
    h = h_ref[...]
    for a0 in range(0, na, rows_per_pass):
        slot_id = a0 + _iota2((rows_per_pass, tm), 0)
        hit = pos[0:1, :] == slot_id
        for k in range(1, TOP_K):
            hit = jnp.logical_or(hit, pos[k:k + 1, :] == slot_id)
        rows = _dot(hit.astype(BF16), h)
        for i in range(n_words):
            word = _pack_halves(rows[:, i * LANES:(i + 1) * LANES], rows[:, half + i * LANES:half + (i + 1) * LANES])
            stage[buf, pl.ds(a0 * SUBLANES + i, rows_per_pass, stride=SUBLANES), :] = word

    def run_copy(tile_slot, sorted_slot, n_slots):
        return pltpu.make_async_copy(stage.at[buf, _slot_rows(tile_slot, n_slots), :],
                                     xs_ref.at[_slot_rows(sorted_slot, n_slots), :], sem.at[buf])

    def wait_tile(which):
        pltpu.make_async_copy(stage.at[which], xs_ref.at[pl.ds(0, na * SUBLANES), :], sem.at[which]).wait()

    _for_each_run(step, (run_len_ref, run_off_ref, run_dst_ref), tm, run_copy, lambda cp: cp.start())

    @pl.when(step > 0)
    def _():
        wait_tile(1 - buf)

    @pl.when(step == pl.num_programs(0) - 1)
    def _():
        wait_tile(buf)

    @pl.when(step == 0)
    def _():
        for_each_pad(lambda cp: cp.wait())


def _dispatch(runs, pad_lo, pad_hi, pos_t, h2, n_slots, tm):
    s, d = h2.shape
    assert (d // 2) % LANES == 0 and (d // 2) // LANES == SUBLANES, "one token row must pack into one (8, 128) tile"
    na = TOP_K * tm
    return pl.pallas_call(
        functools.partial(_dispatch_kernel, tm=tm, rows_per_pass=min(512, na)),
        grid_spec=pltpu.PrefetchScalarGridSpec(
            num_scalar_prefetch=5,
            grid=(s // tm,),
            in_specs=[pl.BlockSpec((TOP_K, tm), lambda i, *_: (0, i)),
                      pl.BlockSpec((tm, d), lambda i, *_: (i, 0))],
            out_specs=pl.BlockSpec(memory_space=pl.ANY),
            scratch_shapes=[pltpu.VMEM((2, na * SUBLANES, LANES), U32),
                            pltpu.VMEM((EXPERT_BLOCK // 2 * SUBLANES, LANES), U32),
                            pltpu.SemaphoreType.DMA((2,)), pltpu.SemaphoreType.DMA(())]),
        out_shape=jax.ShapeDtypeStruct((n_slots * SUBLANES, LANES), U32),
        compiler_params=_params(("arbitrary",), has_side_effects=True, disable_bounds_checks=True),
    )(*runs, pad_lo, pad_hi, pos_t, h2)


def _expert_kernel(be_ref, nu_ref, next_ref, par_ref, x_ref, wg_hbm, wu_hbm, wd_hbm, y_ref, wg_f32, wu_f32, wd_f32,
                   wg_scr, wu_scr, wd_scr, sem):
    b = pl.program_id(0)
    bm = EXPERT_BLOCK
    active = b < nu_ref[0]
    new_expert = jnp.logical_or(b == 0, be_ref[b] != be_ref[jnp.maximum(b - 1, 0)])

    def weight_copies(ex, which):
        return [pltpu.make_async_copy(src.at[ex], dst.at[which], sem.at[which])
                for src, dst in ((wg_hbm, wg_f32), (wu_hbm, wu_f32), (wd_hbm, wd_f32))]

    @pl.when(jnp.logical_and(active, new_expert))
    def _():
        which = par_ref[b]

        @pl.when(b == 0)
        def _():
            for cp in weight_copies(be_ref[b], which):
                cp.start()

        for cp in weight_copies(be_ref[b], which):
            cp.wait()
        wg_scr[...] = wg_f32[which].astype(BF16)
        wu_scr[...] = wu_f32[which].astype(BF16)
        wd_scr[...] = wd_f32[which].astype(BF16)
        nb = next_ref[b]

        @pl.when(nb < nu_ref[0])
        def _():
            for cp in weight_copies(be_ref[nb], 1 - which):
                cp.start()

    @pl.when(active)
    def _():
        los, his = [], []
        for i in range(SUBLANES):
            lo, hi = _unpack_halves(x_ref[pl.ds(i, bm, stride=SUBLANES), :])
            los.append(lo.astype(BF16))
            his.append(hi.astype(BF16))
        xb = jnp.concatenate(los + his, axis=1)
        hid = _silu(_dot(xb, wg_scr[...])) * _dot(xb, wu_scr[...])
        y = _dot(hid.astype(BF16), wd_scr[...])
        half = y.shape[1] // 2
        for i in range(SUBLANES):
            word = _pack_halves(_round_bf16(y[:, i * LANES:(i + 1) * LANES]),
                                _round_bf16(y[:, half + i * LANES:half + (i + 1) * LANES]))
            y_ref[pl.ds(i, bm, stride=SUBLANES), :] = word


def _experts(block_e, n_used, next_block, parity, xs, w_gate, w_up, w_down):
    d, ff = w_gate.shape[1], w_gate.shape[2]
    bm = EXPERT_BLOCK
    n_blocks = xs.shape[0] // (bm * SUBLANES)
    blk = lambda b, be, nu, *_: (jnp.minimum(b, nu[0] - 1), 0)
    hbm = pl.BlockSpec(memory_space=pl.ANY)
    return pl.pallas_call(
        _expert_kernel,
        grid_spec=pltpu.PrefetchScalarGridSpec(
            num_scalar_prefetch=4,
            grid=(n_blocks,),
            in_specs=[pl.BlockSpec((bm * SUBLANES, LANES), blk), hbm, hbm, hbm],
            out_specs=pl.BlockSpec((bm * SUBLANES, LANES), blk),
            scratch_shapes=[pltpu.VMEM((2, d, ff), F32), pltpu.VMEM((2, d, ff), F32), pltpu.VMEM((2, ff, d), F32),
                            pltpu.VMEM((d, ff), BF16), pltpu.VMEM((d, ff), BF16), pltpu.VMEM((ff, d), BF16),
                            pltpu.SemaphoreType.DMA((2,))]),
        out_shape=jax.ShapeDtypeStruct(xs.shape, U32),
        compiler_params=_params(("arbitrary",)),
    )(block_e, n_used, next_block, parity, xs, w_gate, w_up, w_down)


def _combine_kernel(run_len_ref, run_off_ref, run_src_ref, ys_ref, h_ref, x1_ref, pos_ref, wts_ref, wg_ref, wu_ref, wd_ref, gt_ref, gf_ref,
                    o_ref, stage, sem, *, tm, rows_per_pass):
    step = pl.program_id(0)
    na = TOP_K * tm
    buf = step % 2

    def fetch_tile(tile, which):
        def run_copy(tile_slot, sorted_slot, n_slots):
            return pltpu.make_async_copy(ys_ref.at[_slot_rows(sorted_slot, n_slots), :],
                                         stage.at[which, _slot_rows(tile_slot, n_slots), :], sem.at[which])
        _for_each_run(tile, (run_len_ref, run_off_ref, run_src_ref), tm, run_copy, lambda cp: cp.start())

    @pl.when(step == 0)
    def _():
        fetch_tile(step, buf)

    @pl.when(step + 1 < pl.num_programs(0))
    def _():
        fetch_tile(step + 1, 1 - buf)

    hb = h_ref[...]
    hid = _silu(_dot(hb, wg_ref[...])) * _dot(hb, wu_ref[...])
    acc = _dot(hid.astype(BF16), wd_ref[...])

    pltpu.make_async_copy(ys_ref.at[pl.ds(0, na * SUBLANES), :], stage.at[buf], sem.at[buf]).wait()

    pos = pos_ref[...]
    wts = wts_ref[...]
    for a0 in range(0, na, rows_per_pass):
        los, his = [], []
        for i in range(SUBLANES):
            lo, hi = _unpack_halves(stage[buf, pl.ds(a0 * SUBLANES + i, rows_per_pass, stride=SUBLANES), :])
            los.append(lo.astype(BF16))
            his.append(hi.astype(BF16))
        y_rows = jnp.concatenate(los + his, axis=1)
        slot_id = a0 + _iota2((rows_per_pass, tm), 0)
        wmat = jnp.zeros((rows_per_pass, tm), F32)
        for k in range(TOP_K):
            wmat = wmat + jnp.where(pos[k:k + 1, :] == slot_id, wts[k:k + 1, :], 0.0)
        acc = acc + _dg(wmat.astype(BF16), y_rows, TN)
    x2 = x1_ref[...] + gt_ref[...] * acc
    o_ref[...] = _rms(x2, NORM_EPS) * gf_ref[...]


def _combine(runs, ys, h2, x1, pos_t, wts_t, w_gate, w_up, w_down, gt2, gf, tm):
    s, d = x1.shape
    ff = w_gate.shape[1]
    na = TOP_K * tm
    const = lambda shape: pl.BlockSpec(shape, lambda i, *_: (0, 0), pipeline_mode=pl.Buffered(1))
    tile = lambda cols: pl.BlockSpec((tm, cols), lambda i, *_: (i, 0))
    per_k = pl.BlockSpec((TOP_K, tm), lambda i, *_: (0, i))
    return pl.pallas_call(
        functools.partial(_combine_kernel, tm=tm, rows_per_pass=min(512, na)),
        grid_spec=pltpu.PrefetchScalarGridSpec(
            num_scalar_prefetch=3,
            grid=(s // tm,),
            in_specs=[pl.BlockSpec(memory_space=pl.ANY),
                      tile(d), tile(d), per_k, per_k,
                      const((d, ff)), const((d, ff)), const((ff, d)), const((1, d)), const((1, d))],
            out_specs=tile(d),
            scratch_shapes=[pltpu.VMEM((2, na * SUBLANES, LANES), U32), pltpu.SemaphoreType.DMA((2,))]),
        out_shape=jax.ShapeDtypeStruct((s, d), F32),
        compiler_params=_params(("arbitrary",), disable_bounds_checks=True),
    )(*runs, ys, h2, x1, pos_t, wts_t, w_gate, w_up, w_down, gt2, gf)


def _mixer(x2d, mod, norm1_g, norm2_g, w_in, lb, hgrn_onorm_g, gdn_conv_w, gdn_a_log, gdn_dt_bias, gdn_onorm_g,
           w_branch_hgrn, w_branch_gdn, w_out, tiles):
    d = x2d.shape[1]
    sh1, sc1, gt1, sh2, sc2, _ = [mod[:, i * d:(i + 1) * d] for i in range(6)]
    key = HEADS * HEAD_DIM
    small0 = 4 * key + 3 * key
    small1 = small0 + 2 * HEADS
    w_in_t = jnp.swapaxes(w_in, 1, 2)
    w_main_t = _wprep(w_in_t, small0, small1, tiles["wprep_tn"])
    w_small_t = w_in_t[0, small0:small1, :].astype(BF16)
    proj, ab_t = _inproj(x2d, norm1_g, sc1, sh1, w_main_t, w_small_t, tiles["in_tm"], tiles["in_tn"])
    o_a = _hgrn(proj, lb, hgrn_onorm_g, tiles["mix_ts"])
    u, wqd, ku, attn, dl = _gdn_prep(proj, gdn_conv_w, ab_t, gdn_a_log, gdn_dt_bias, tiles["prep_ts"])
    o_b = _gdn_scan(u, wqd, ku, attn, dl, proj, gdn_onorm_g, tiles["mix_ts"])
    return _merge(o_a, o_b, proj, x2d, w_branch_hgrn.astype(BF16), w_branch_gdn.astype(BF16),
                  w_out.astype(BF16), gt1, norm2_g, sc2, sh2, tiles["merge_tm"])


def _moe(x1, h2, mod, norm2_g, normf_g, w_router, router_bias, w_exp_gate, w_exp_up, w_exp_down, w_sh_gate,
         w_sh_up, w_sh_down, tiles):
    s, d = x1.shape
    tm = tiles["moe_tm"]
    sh2, sc2, gt2 = [mod[:, i * d:(i + 1) * d] for i in (3, 4, 5)]
    pos_t, wts_t, before, ntile = _router(x1, norm2_g, sc2, sh2, w_router.T, router_bias.reshape(-1, 1), tm)
    bm = EXPERT_BLOCK
    n_blocks = -(-(s * TOP_K + N_EXPERTS * (bm - 1)) // bm)
    before = before[:, :, 0]
    ntile = ntile[:, :, 0]
    counts = before[-1] + ntile[-1]
    padded = (counts + bm - 1) // bm * bm
    earlier = jnp.arange(N_EXPERTS)[None, :] < jnp.arange(N_EXPERTS)[:, None]
    pstart = jnp.sum(jnp.where(earlier, padded[None, :], 0), axis=1).astype(I32)
    pend = pstart + padded
    block_start = jnp.arange(n_blocks, dtype=I32) * bm
    block_e = jnp.minimum(jnp.sum(pend[None, :] <= block_start[:, None], axis=1), N_EXPERTS - 1).astype(I32)
    n_used = pend[-1:] // bm
    run_off = jnp.sum(jnp.where(earlier[None], ntile[:, None, :], 0), axis=2)
    runs = (ntile.reshape(-1), run_off.reshape(-1), (before + pstart[None, :]).reshape(-1))
    xs = _dispatch(runs, pstart + counts, pend, pos_t, h2, n_blocks * bm, tm)
    own = block_e[:, None] == jnp.arange(N_EXPERTS)[None, :]
    next_block = jnp.sum(jnp.where(own, pend[None, :], 0), axis=1) // bm
    switches = jnp.concatenate([jnp.zeros((1,), I32), (block_e[1:] != block_e[:-1]).astype(I32)])
    upto = jnp.arange(n_blocks)[None, :] <= jnp.arange(n_blocks)[:, None]
    parity = jnp.sum(jnp.where(upto, switches[None, :], 0), axis=1).astype(I32) % 2
    ys = _experts(block_e, n_used, next_block, parity, xs, w_exp_gate, w_exp_up, w_exp_down)
    return _combine(runs, ys, h2, x1, pos_t, wts_t, w_sh_gate.astype(BF16), w_sh_up.astype(BF16),
                    w_sh_down.astype(BF16), gt2, normf_g, tm)


def _tiles(s):
    pick = lambda want: min(want, s)
    return dict(wprep_tn=512, in_tm=pick(1024), in_tn=1536, mix_ts=pick(512), prep_ts=pick(2048), merge_tm=pick(512),
                moe_tm=pick(256))


def kernel(x, c, w_ada, b_ada, norm1_g, norm2_g, w_in, hgrn_lb_table, hgrn_onorm_g, gdn_conv_w, gdn_a_log, gdn_dt_bias, gdn_onorm_g, w_branch_hgrn, w_branch_gdn, w_out, w_router, router_bias, w_exp_gate, w_exp_up, w_exp_down, w_sh_gate, w_sh_up, w_sh_down, normf_g):
    b, s, d = x.shape
    assert b == 1 and w_ada.shape[0] == 1, "one sequence, one layer"
    tiles = _tiles(s)
    lb = jnp.sum(jax.nn.softmax(hgrn_lb_table.astype(F32), axis=0)[0:1], axis=0, keepdims=True)
    mod = _ada(c, w_ada[0], b_ada[0])
    row = lambda v: v.reshape(1, -1)
    x1, h2 = _mixer(x[0], mod, row(norm1_g[0]), row(norm2_g[0]), w_in, lb, row(hgrn_onorm_g[0]), gdn_conv_w[0],
                    gdn_a_log[0], gdn_dt_bias[0], row(gdn_onorm_g[0]), w_branch_hgrn[0], w_branch_gdn[0], w_out[0],
                    tiles)
    out = _moe(x1, h2, mod, row(norm2_g[0]), row(normf_g), w_router[0], router_bias[0], w_exp_gate[0],
               w_exp_up[0], w_exp_down[0], w_sh_gate[0], w_sh_up[0], w_sh_down[0], tiles)
    return out[None]
```

```python
import functools

import jax
import jax.numpy as jnp
from jax import lax
from jax.experimental import pallas as pl
from jax.experimental.pallas import tpu as pltpu

F32 = jnp.float32
BF16 = jnp.bfloat16
I32 = jnp.int32
U32 = jnp.uint32

NORM_EPS = 1e-6
L2_EPS = 1e-6
HEADS = 8
HEAD_DIM = 128
CONV_WIDTH = 4
CHUNK = 64
N_EXPERTS = 64
N_GROUPS = 8
GROUP_SIZE = N_EXPERTS // N_GROUPS
TOPK_GROUPS = 4
TOP_K = 8
ROUTED_SCALE = 2.5
EXPERT_BLOCK = 512

LANES = 128
SUBLANES = 8
VMEM_LIMIT = 56 * 1024 * 1024

NT = (((1,), (1,)), ((), ()))
TN = (((0,), (0,)), ((), ()))


def _params(sem, **kw):
    return pltpu.CompilerParams(dimension_semantics=sem, vmem_limit_bytes=VMEM_LIMIT, **kw)


def _dot(a, b):
    return jnp.dot(a, b, preferred_element_type=F32)


def _dg(a, b, dims):
    return lax.dot_general(a, b, dims, preferred_element_type=F32)


def _split(x):
    hi = x.astype(BF16)
    lo = (x - hi.astype(F32)).astype(BF16)
    return hi, lo


def _dot_exact_lhs(a_bf16, x, dims=None):
    hi, lo = _split(x)
    if dims is None:
        return _dot(a_bf16, hi) + _dot(a_bf16, lo)
    return _dg(a_bf16, hi, dims) + _dg(a_bf16, lo, dims)


def _sigmoid(x):
    return 1.0 / (1.0 + jnp.exp(-x))


def _silu(x):
    return x * _sigmoid(x)


def _rms(x, eps):
    return x * lax.rsqrt(jnp.mean(x * x, axis=-1, keepdims=True) + eps)


def _iota2(shape, dim):
    return lax.broadcasted_iota(I32, shape, dim)


def _pack_halves(lo, hi):
    lo_bits = lax.shift_right_logical(pltpu.bitcast(lo, U32), U32(16))
    hi_bits = pltpu.bitcast(hi, U32) & U32(0xFFFF0000)
    return lo_bits | hi_bits


def _unpack_halves(word):
    lo = pltpu.bitcast(lax.shift_left(word, U32(16)), F32)
    hi = pltpu.bitcast(word & U32(0xFFFF0000), F32)
    return lo, hi


def _round_bf16(x):
    return x.astype(BF16).astype(F32)


def _ada_kernel(c_ref, w_ref, b_ref, o_ref):
    cond = _silu(c_ref[...])
    o_ref[...] = jnp.dot(cond, w_ref[...], preferred_element_type=F32,
                         precision=lax.Precision.HIGHEST) + b_ref[...]


def _ada(c, w_ada, b_ada):
    d, n = w_ada.shape
    tn = 1024
    c8 = jnp.broadcast_to(c, (SUBLANES, d))
    out = pl.pallas_call(
        _ada_kernel,
        grid=(n // tn,),
        in_specs=[pl.BlockSpec((SUBLANES, d), lambda j: (0, 0)),
                  pl.BlockSpec((d, tn), lambda j: (0, j)),
                  pl.BlockSpec((1, tn), lambda j: (0, j))],
        out_specs=pl.BlockSpec((SUBLANES, tn), lambda j: (0, j)),
        out_shape=jax.ShapeDtypeStruct((SUBLANES, n), F32),
        compiler_params=_params(("arbitrary",)),
    )(c8, w_ada, b_ada.reshape(1, n))
    return out[0:1]


def _wprep_kernel(a_ref, b_ref, o_ref, *, first_shifted, shift):
    j = pl.program_id(0)

    @pl.when(j < first_shifted)
    def _():
        o_ref[...] = a_ref[...].astype(BF16)

    @pl.when(j >= first_shifted)
    def _():
        tn = a_ref.shape[0]
        o_ref[0:tn - shift, :] = a_ref[shift:tn, :].astype(BF16)
        o_ref[tn - shift:tn, :] = b_ref[...].astype(BF16)


def _wprep(w_in_t, cut0, cut1, tn):
    _, n_in, d = w_in_t.shape
    shift = cut1 - cut0
    n_out = n_in - shift
    assert cut0 % tn == 0 and n_out % tn == 0 and tn % shift == 0 and shift % (2 * SUBLANES) == 0
    return pl.pallas_call(
        functools.partial(_wprep_kernel, first_shifted=cut0 // tn, shift=shift),
        grid=(n_out // tn,),
        in_specs=[pl.BlockSpec((None, tn, d), lambda j: (0, j, 0)),
                  pl.BlockSpec((None, shift, d), lambda j: (0, (j + 1) * (tn // shift), 0))],
        out_specs=pl.BlockSpec((tn, d), lambda j: (j, 0)),
        out_shape=jax.ShapeDtypeStruct((n_out, d), BF16),
        compiler_params=_params(("arbitrary",)),
    )(w_in_t, w_in_t)


def _inproj_kernel(x_ref, g_ref, sc_ref, sh_ref, w_ref, wst_ref, proj_ref, smallt_ref, h_scr):
    @pl.when(pl.program_id(1) == 0)
    def _():
        h = _rms(x_ref[...], NORM_EPS) * g_ref[...] * (1.0 + sc_ref[...]) + sh_ref[...]
        hb = h.astype(BF16)
        h_scr[...] = hb
        smallt_ref[...] = _dg(wst_ref[...], hb, NT)

    proj_ref[...] = _dg(h_scr[...], w_ref[...], NT).astype(BF16)


def _inproj(x, g, sc, sh, w_main_t, w_small_t, tm, tn):
    s, d = x.shape
    n = w_main_t.shape[0]
    ns = w_small_t.shape[0]
    row = lambda i, j: (0, 0)
    return pl.pallas_call(
        _inproj_kernel,
        grid=(s // tm, n // tn),
        in_specs=[pl.BlockSpec((tm, d), lambda i, j: (i, 0)),
                  pl.BlockSpec((1, d), row), pl.BlockSpec((1, d), row), pl.BlockSpec((1, d), row),
                  pl.BlockSpec((tn, d), lambda i, j: (j, 0)),
                  pl.BlockSpec((ns, d), row)],
        out_specs=[pl.BlockSpec((tm, tn), lambda i, j: (i, j)),
                   pl.BlockSpec((ns, tm), lambda i, j: (0, i))],
        out_shape=[jax.ShapeDtypeStruct((s, n), BF16), jax.ShapeDtypeStruct((ns, s), F32)],
        scratch_shapes=[pltpu.VMEM((tm, d), BF16)],
        compiler_params=_params(("arbitrary", "arbitrary")),
    )(x, g, sc, sh, w_main_t, w_small_t)


def _hgrn_kernel(q_ref, f_ref, i_ref, g_ref, lb_ref, on_ref, o_ref, st_scr, *, n_chunks):
    @pl.when(pl.program_id(0) == 0)
    def _():
        st_scr[...] = jnp.zeros_like(st_scr)

    c = CHUNK
    hd = HEAD_DIM
    causal = _iota2((c, c), 1) <= _iota2((c, c), 0)
    tri = causal.astype(BF16)
    lb = lb_ref[...]
    on_g = on_ref[...]
    heads = [slice(h * hd, (h + 1) * hd) for h in range(HEADS)]

    def chunk(n, carry):
        rows = pl.ds(pl.multiple_of(n * c, c), c)
        f = lb + (1.0 - lb) * _sigmoid(f_ref[rows, :].astype(F32))
        b = _dot_exact_lhs(tri, jnp.log(f))
        k = 1.0 - f
        q = _silu(q_ref[rows, :].astype(F32)) * (hd ** -0.5)
        v = i_ref[rows, :]
        b_mid = b[c // 2:c // 2 + 1, :]
        b_last = b[c - 1:c, :]
        qa = (q * jnp.exp(b - b_mid)).astype(BF16)
        ka = (k * jnp.exp(b_mid - b)).astype(BF16)
        qi = (q * jnp.exp(b)).astype(BF16)
        ku = (k * jnp.exp(b_last - b)).astype(BF16)
        dec = jnp.exp(b_last)
        gate = on_g * _silu(g_ref[rows, :].astype(F32))
        sts = [st_scr[h] for h in range(HEADS)]
        scores = [jnp.where(causal, _dg(qa[:, sl], ka[:, sl], NT), 0.0).astype(BF16) for sl in heads]
        inter = [_dg(qi[:, sl], st.astype(BF16), NT) for sl, st in zip(heads, sts)]
        kv = [_dg(v[:, sl], ku[:, sl], TN) for sl in heads]
        for h, sl in enumerate(heads):
            st_scr[h] = dec[:, sl] * sts[h] + kv[h]
        outs = [_rms(_dot(sc, v[:, sl]) + it, NORM_EPS) for sc, sl, it in zip(scores, heads, inter)]
        o_ref[rows, :] = (jnp.concatenate(outs, axis=1) * gate).astype(BF16)
        return carry

    lax.fori_loop(0, n_chunks, chunk, 0, unroll=4)


def _hgrn(proj, lb, onorm_g, ts):
    s = proj.shape[0]
    width = HEADS * HEAD_DIM
    col = lambda blk: pl.BlockSpec((ts, width), lambda j, blk=blk: (j, blk))
    const = pl.BlockSpec((1, width), lambda j: (0, 0))
    return pl.pallas_call(
        functools.partial(_hgrn_kernel, n_chunks=ts // CHUNK),
        grid=(s // ts,),
        in_specs=[col(0), col(1), col(2), col(3), const, const],
        out_specs=pl.BlockSpec((ts, width), lambda j: (j, 0)),
        out_shape=jax.ShapeDtypeStruct((s, width), BF16),
        scratch_shapes=[pltpu.VMEM((HEADS, HEAD_DIM, HEAD_DIM), F32)],
        compiler_params=_params(("arbitrary",)),
    )(proj, proj, proj, proj, lb, jnp.tile(onorm_g, (1, HEADS)))


def _gdn_prep_kernel(q_ref, k_ref, v_ref, qp_ref, kp_ref, vp_ref, wq_ref, wk_ref, wv_ref, ab_ref, alog_ref,
                     dtb_ref, tri_ref, eye_ref, u_ref, wqd_ref, ku_ref, attn_ref, dl_ref, cat_scr, rows_scr, cols_scr,
                     *, n_chunks, ts):
    h = pl.program_id(1)
    first = pl.program_id(0) == 0
    c = CHUNK
    hd = HEAD_DIM

    def conv_silu(cur_ref, prev_ref, w_ref):
        cat_scr[0:8, :] = jnp.where(first, 0.0, prev_ref[...].astype(F32))
        cat_scr[8:8 + ts, :] = cur_ref[...].astype(F32)
        acc = None
        for j in range(CONV_WIDTH):
            off = 8 - (CONV_WIDTH - 1) + j
            term = cat_scr[off:off + ts, :] * w_ref[j:j + 1, :]
            acc = term if acc is None else acc + term
        return _silu(acc)

    def l2n(x):
        return x * lax.rsqrt(jnp.sum(x * x, axis=-1, keepdims=True) + L2_EPS)

    q_all = l2n(conv_silu(q_ref, qp_ref, wq_ref)) * (hd ** -0.5)
    k_all = l2n(conv_silu(k_ref, kp_ref, wk_ref))
    v_all = conv_silu(v_ref, vp_ref, wv_ref)

    @pl.when(h == 0)
    def _():
        z = ab_ref[0:HEADS, :] + dtb_ref[...]
        softplus = jnp.maximum(z, 0.0) + jnp.log(1.0 + jnp.exp(-jnp.abs(z)))
        ld_rows = -jnp.exp(alog_ref[...]) * softplus
        hi, lo = _split(ld_rows)
        tri_blocks = tri_ref[...]
        g_rows = _dg(hi, tri_blocks, NT) + _dg(lo, tri_blocks, NT)
        beta_rows = _sigmoid(ab_ref[HEADS:2 * HEADS, :])
        rows_scr[...] = g_rows
        rows = jnp.concatenate([g_rows, beta_rows, jnp.zeros((LANES - 2 * HEADS, ts), F32)], axis=0)
        r_hi, r_lo = _split(rows)
        r_lo2 = (rows - r_hi.astype(F32) - r_lo.astype(F32)).astype(BF16)
        eye_ts = eye_ref[...]
        cols_scr[...] = _dg(eye_ts, r_hi, NT) + _dg(eye_ts, r_lo, NT) + _dg(eye_ts, r_lo2, NT)

    lane = _iota2((ts, LANES), 1)
    cols = cols_scr[...]
    gc_all = jnp.sum(jnp.where(lane == h, cols, 0.0), axis=1, keepdims=True)
    bc_all = jnp.sum(jnp.where(lane == h + HEADS, cols, 0.0), axis=1, keepdims=True)
    g_row = rows_scr[pl.ds(h, 1), :]
    egc_all = jnp.exp(gc_all)

    r = _iota2((c, c), 0)
    cidx = _iota2((c, c), 1)
    causal = cidx <= r
    strict = cidx < r
    eye_f = (r == cidx).astype(F32)
    chunks = [slice(n * c, (n + 1) * c) for n in range(n_chunks)]

    q16 = q_all.astype(BF16)
    k16 = k_all.astype(BF16)
    kq = [_dg(jnp.concatenate([k16[sl], q16[sl]], axis=0), k16[sl], NT) for sl in chunks]
    dm = []
    for sl in chunks:
        diff = gc_all[sl] - g_row[:, sl]
        dm.append(jnp.where(causal, jnp.exp(jnp.where(causal, diff, 0.0)), 0.0))
    bm = [-jnp.where(strict, bc_all[sl] * x[0:c] * d, 0.0) for sl, x, d in zip(chunks, kq, dm)]
    p = [eye_f + b for b in bm]
    bm = [_dot(b.astype(BF16), b.astype(BF16)) for b in bm]
    for _ in range(c.bit_length() - 3):
        res = [_dot(b.astype(BF16), jnp.concatenate([b, pp], axis=1).astype(BF16)) for b, pp in zip(bm, p)]
        p = [pp + x[:, c:2 * c] for pp, x in zip(p, res)]
        bm = [x[:, 0:c] for x in res]
    p = [pp + _dot(b.astype(BF16), pp.astype(BF16)) for b, pp in zip(bm, p)]
    rhs = jnp.concatenate([v_all * bc_all, k_all * (bc_all * egc_all)], axis=1).astype(BF16)
    sol = [_dot(pp.astype(BF16), rhs[sl]) for pp, sl in zip(p, chunks)]
    qd_all = (q_all * egc_all).astype(BF16)
    for n, sl in enumerate(chunks):
        g_last = gc_all[(n + 1) * c - 1:(n + 1) * c, :]
        u_ref[sl, :] = sol[n][:, 0:hd].astype(BF16)
        wqd_ref[2 * n * c:(2 * n + 1) * c, :] = sol[n][:, hd:2 * hd].astype(BF16)
        wqd_ref[(2 * n + 1) * c:(2 * n + 2) * c, :] = qd_all[sl]
        ku_ref[sl, :] = (k_all[sl] * jnp.exp(g_last - gc_all[sl])).astype(BF16)
        attn_ref[sl, :] = (kq[n][c:2 * c] * dm[n]).astype(BF16)
        dl_ref[n:n + 1, :] = jnp.broadcast_to(jnp.exp(g_last), (1, hd))


def _gdn_prep(proj, conv_w, ab_t, a_log, dt_bias, ts):
    s = proj.shape[0]
    hd = HEAD_DIM
    c = CHUNK
    q0 = 4 * HEADS
    cur = lambda off: pl.BlockSpec((ts, hd), lambda j, h, off=off: (j, off + h))
    prev = lambda off: pl.BlockSpec((8, hd), lambda j, h, off=off: (jnp.maximum(j * (ts // 8) - 1, 0), off + h))
    cw = lambda off: pl.BlockSpec((CONV_WIDTH, hd), lambda j, h, off=off: (0, off + h))
    per_head_scalar = pl.BlockSpec((HEADS, 1), lambda j, h: (0, 0))
    const = pl.BlockSpec((ts, ts), lambda j, h: (0, 0), pipeline_mode=pl.Buffered(1))
    pos = jnp.arange(ts)
    tri_blocks = ((pos[:, None] // c == pos[None, :] // c) & (pos[None, :] <= pos[:, None])).astype(BF16)
    eye = (pos[:, None] == pos[None, :]).astype(BF16)
    per_head = lambda rows, cols: pl.BlockSpec((None, rows, cols), lambda j, h: (h, j, 0))
    return pl.pallas_call(
        functools.partial(_gdn_prep_kernel, n_chunks=ts // c, ts=ts),
        grid=(s // ts, HEADS),
        in_specs=[cur(q0), cur(q0 + HEADS), cur(q0 + 2 * HEADS),
                  prev(q0), prev(q0 + HEADS), prev(q0 + 2 * HEADS),
                  cw(0), cw(HEADS), cw(2 * HEADS),
                  pl.BlockSpec((2 * HEADS, ts), lambda j, h: (0, j)),
                  per_head_scalar, per_head_scalar, const, const],
        out_specs=[pl.BlockSpec((ts, hd), lambda j, h: (j, h)),
                   pl.BlockSpec((2 * ts, hd), lambda j, h: (j, h)),
                   pl.BlockSpec((ts, hd), lambda j, h: (j, h)),
                   per_head(ts, c),
                   per_head(ts // c, hd)],
        out_shape=[jax.ShapeDtypeStruct((s, HEADS * hd), BF16),
                   jax.ShapeDtypeStruct((2 * s, HEADS * hd), BF16),
                   jax.ShapeDtypeStruct((s, HEADS * hd), BF16),
                   jax.ShapeDtypeStruct((HEADS, s, c), BF16),
                   jax.ShapeDtypeStruct((HEADS, s // c, hd), F32)],
        scratch_shapes=[pltpu.VMEM((ts + 8, hd), F32), pltpu.VMEM((HEADS, ts), F32), pltpu.VMEM((ts, LANES), F32)],
        compiler_params=_params(("arbitrary", "arbitrary")),
    )(proj, proj, proj, proj, proj, proj, conv_w, conv_w, conv_w,
      ab_t, a_log.reshape(HEADS, 1), dt_bias.reshape(HEADS, 1), tri_blocks, eye)


def _gdn_scan_kernel(u_ref, wqd_ref, ku_ref, attn_ref, dl_ref, g_ref, on_ref, o_ref, st_scr, *, n_chunks):
    @pl.when(pl.program_id(0) == 0)
    def _():
        st_scr[...] = jnp.zeros_like(st_scr)

    c = CHUNK
    hd = HEAD_DIM
    on_g = on_ref[...]
    heads = [slice(h * hd, (h + 1) * hd) for h in range(HEADS)]

    def chunk(n, carry):
        rows = pl.ds(pl.multiple_of(n * c, c), c)
        rows2 = pl.ds(pl.multiple_of(2 * n * c, 2 * c), 2 * c)
        sts = [st_scr[h] for h in range(HEADS)]
        wq = [_dot(wqd_ref[rows2, sl], st.astype(BF16)) for sl, st in zip(heads, sts)]
        vn = [(u_ref[rows, sl].astype(F32) - x[0:c]).astype(BF16) for sl, x in zip(heads, wq)]
        upd = [_dg(ku_ref[rows, sl], v, TN) for sl, v in zip(heads, vn)]
        for h in range(HEADS):
            st_scr[h] = dl_ref[h, pl.ds(n, 1), :] * sts[h] + upd[h]
        outs = [_rms(x[c:2 * c] + _dot(attn_ref[h, rows, :], v), NORM_EPS)
                for h, (x, v) in enumerate(zip(wq, vn))]
        gate = jnp.tile(on_g, (1, HEADS)) * _silu(g_ref[rows, :].astype(F32))
        o_ref[rows, :] = (jnp.concatenate(outs, axis=1) * gate).astype(BF16)
        return carry

    lax.fori_loop(0, n_chunks, chunk, 0, unroll=4)


def _gdn_scan(u, wqd, ku, attn, dl, proj, onorm_g, ts):
    s, width = u.shape
    c = CHUNK
    gate_blk = (4 * HEADS + 3 * HEADS) * HEAD_DIM // width
    return pl.pallas_call(
        functools.partial(_gdn_scan_kernel, n_chunks=ts // c),
        grid=(s // ts,),
        in_specs=[pl.BlockSpec((ts, width), lambda j: (j, 0)),
                  pl.BlockSpec((2 * ts, width), lambda j: (j, 0)),
                  pl.BlockSpec((ts, width), lambda j: (j, 0)),
                  pl.BlockSpec((HEADS, ts, c), lambda j: (0, j, 0)),
                  pl.BlockSpec((HEADS, ts // c, HEAD_DIM), lambda j: (0, j, 0)),
                  pl.BlockSpec((ts, width), lambda j: (j, gate_blk)),
                  pl.BlockSpec((1, HEAD_DIM), lambda j: (0, 0))],
        out_specs=pl.BlockSpec((ts, width), lambda j: (j, 0)),
        out_shape=jax.ShapeDtypeStruct((s, width), BF16),
        scratch_shapes=[pltpu.VMEM((HEADS, HEAD_DIM, HEAD_DIM), F32)],
        compiler_params=_params(("arbitrary",)),
    )(u, wqd, ku, attn, dl, proj, onorm_g)


def _merge_kernel(oa_ref, ob_ref, mga_ref, mgb_ref, x_ref, wa_ref, wb_ref, wo_ref, gt_ref, g2_ref, sc_ref,
                  sh_ref, x1_ref, h2_ref):
    ya = _dot(oa_ref[...], wa_ref[...])
    yb = _dot(ob_ref[...], wb_ref[...])
    merged = _sigmoid(mga_ref[...].astype(F32)) * ya + _sigmoid(mgb_ref[...].astype(F32)) * yb
    x1 = x_ref[...] + gt_ref[...] * _dot(merged.astype(BF16), wo_ref[...])
    x1_ref[...] = x1
    h2 = _rms(x1, NORM_EPS) * g2_ref[...] * (1.0 + sc_ref[...]) + sh_ref[...]
    h2_ref[...] = h2.astype(BF16)


def _merge(o_a, o_b, proj, x, w_a, w_b, w_o, gt1, g2, sc2, sh2, tm):
    s, d = x.shape
    dv = o_a.shape[1]
    mg0 = (8 * HEADS * HEAD_DIM) // d
    const = lambda shape: pl.BlockSpec(shape, lambda i: (0, 0), pipeline_mode=pl.Buffered(1))
    return pl.pallas_call(
        _merge_kernel,
        grid=(s // tm,),
        in_specs=[pl.BlockSpec((tm, dv), lambda i: (i, 0)),
                  pl.BlockSpec((tm, dv), lambda i: (i, 0)),
                  pl.BlockSpec((tm, d), lambda i: (i, mg0)),
                  pl.BlockSpec((tm, d), lambda i: (i, mg0 + 1)),
                  pl.BlockSpec((tm, d), lambda i: (i, 0)),
                  const((dv, d)), const((dv, d)), const((d, d)),
                  const((1, d)), const((1, d)), const((1, d)), const((1, d))],
        out_specs=[pl.BlockSpec((tm, d), lambda i: (i, 0)), pl.BlockSpec((tm, d), lambda i: (i, 0))],
        out_shape=[jax.ShapeDtypeStruct((s, d), F32), jax.ShapeDtypeStruct((s, d), BF16)],
        compiler_params=_params(("arbitrary",)),
    )(o_a, o_b, proj, proj, x, w_a, w_b, w_o, gt1, g2, sc2, sh2)


def _first_max(vals, iota, size, axis):
    m = jnp.max(vals, axis=axis, keepdims=True)
    idx = jnp.min(jnp.where(vals == m, iota, size), axis=axis, keepdims=True)
    return m, idx


def _router_kernel(x1_ref, g2_ref, sc_ref, sh_ref, wrt_ref, bias_ref, upper_ref, pos_ref, wts_ref, before_ref,
                   ntile_ref, cnt_scr, *, tm):
    @pl.when(pl.program_id(0) == 0)
    def _():
        cnt_scr[...] = jnp.zeros_like(cnt_scr)

    e = N_EXPERTS
    h2 = _rms(x1_ref[...], NORM_EPS) * g2_ref[...] * (1.0 + sc_ref[...]) + sh_ref[...]
    logits = lax.dot_general(wrt_ref[...], h2, NT, preferred_element_type=F32,
                             precision=lax.Precision.HIGHEST)
    scores = _sigmoid(logits)
    biased = scores + bias_ref[...]
    neg = -jnp.inf

    g3 = biased.reshape(N_GROUPS, GROUP_SIZE, tm)
    i3 = lax.broadcasted_iota(I32, g3.shape, 1)
    m1, a1 = _first_max(g3, i3, GROUP_SIZE, 1)
    m2 = jnp.max(jnp.where(i3 == a1, neg, g3), axis=1, keepdims=True)
    gs = (m1 + m2).reshape(N_GROUPS, tm)
    ig = _iota2(gs.shape, 0)
    gmask = jnp.zeros(gs.shape, jnp.bool_)
    for _ in range(TOPK_GROUPS):
        _, a = _first_max(gs, ig, N_GROUPS, 0)
        pick = ig == a
        gmask = jnp.logical_or(gmask, pick)
        gs = jnp.where(pick, neg, gs)
    emask = jnp.broadcast_to(gmask.reshape(N_GROUPS, 1, tm), (N_GROUPS, GROUP_SIZE, tm)).reshape(e, tm)

    cand = jnp.where(emask, biased, neg)
    ie = _iota2((e, tm), 0)
    sel_all = jnp.zeros((e, tm), jnp.bool_)
    w_rows, picks = [], []
    for _ in range(TOP_K):
        _, a = _first_max(cand, ie, e, 0)
        pick = ie == a
        picks.append(pick)
        w_rows.append(jnp.sum(jnp.where(pick, scores, 0.0), axis=0, keepdims=True))
        sel_all = jnp.logical_or(sel_all, pick)
        cand = jnp.where(pick, neg, cand)
    w_sum = w_rows[0]
    for wr in w_rows[1:]:
        w_sum = w_sum + wr
    wts = jnp.concatenate(w_rows, axis=0) / w_sum * ROUTED_SCALE

    sel = sel_all.astype(BF16)
    in_expert = _dot(sel, upper_ref[...])
    n_tile = jnp.sum(sel_all.astype(F32), axis=1, keepdims=True)
    lower = (_iota2((e, e), 1) < _iota2((e, e), 0)).astype(BF16)
    expert_off = _dot_exact_lhs(lower, jnp.broadcast_to(n_tile, (e, LANES)))[:, 0:1]
    place = in_expert + expert_off
    pos = jnp.concatenate([jnp.sum(jnp.where(pk, place, 0.0), axis=0, keepdims=True) for pk in picks], axis=0)
    pos_ref[...] = pos.astype(I32)
    before_ref[...] = jnp.broadcast_to(cnt_scr[...], before_ref.shape).astype(I32)
    ntile_ref[...] = jnp.broadcast_to(n_tile, ntile_ref.shape).astype(I32)
    cnt_scr[...] = cnt_scr[...] + n_tile
    wts_ref[...] = wts


def _router(x1, g2, sc2, sh2, w_router_t, bias_col, tm):
    s, d = x1.shape
    e = N_EXPERTS
    nt = s // tm
    upper = (jnp.arange(tm)[:, None] < jnp.arange(tm)[None, :]).astype(BF16)
    const = lambda shape: pl.BlockSpec(shape, lambda i: (0, 0))
    per_tile = pl.BlockSpec((None, e, LANES), lambda i: (i, 0, 0))
    return pl.pallas_call(
        functools.partial(_router_kernel, tm=tm),
        grid=(nt,),
        in_specs=[pl.BlockSpec((tm, d), lambda i: (i, 0)),
                  const((1, d)), const((1, d)), const((1, d)),
                  const((e, d)), const((e, 1)), const((tm, tm))],
        out_specs=[pl.BlockSpec((TOP_K, tm), lambda i: (0, i)),
                   pl.BlockSpec((TOP_K, tm), lambda i: (0, i)),
                   per_tile, per_tile],
        out_shape=[jax.ShapeDtypeStruct((TOP_K, s), I32), jax.ShapeDtypeStruct((TOP_K, s), F32),
                   jax.ShapeDtypeStruct((nt, e, LANES), I32), jax.ShapeDtypeStruct((nt, e, LANES), I32)],
        scratch_shapes=[pltpu.VMEM((e, 1), F32)],
        compiler_params=_params(("arbitrary",)),
    )(x1, g2, sc2, sh2, w_router_t, bias_col, upper)


LONG_RUN = 64


def _run_sizes(limit):
    return [1 << b for b in range(limit.bit_length() - 1, -1, -1)]


def _for_each_run(tile, run_refs, tm, make_copy, fn):
    run_len_ref, run_off_ref, run_dst_ref = run_refs

    def per_expert(ex, carry):
        n = run_len_ref[tile * N_EXPERTS + ex]
        off = run_off_ref[tile * N_EXPERTS + ex]
        dst = run_dst_ref[tile * N_EXPERTS + ex]
        def pieces(sizes):
            for size in sizes:
                done = n & (-2 * size)

                @pl.when((n & size) != 0)
                def _(done=done, size=size):
                    fn(make_copy(off + done, dst + done, size))

        sizes = _run_sizes(tm)
        pieces([size for size in sizes if size < LONG_RUN])

        @pl.when(n >= LONG_RUN)
        def _():
            pieces([size for size in sizes if size >= LONG_RUN])

        return carry

    lax.fori_loop(0, N_EXPERTS, per_expert, 0)


def _slot_rows(slot, n_slots):
    return pl.ds(pl.multiple_of(slot * SUBLANES, SUBLANES), n_slots * SUBLANES)


def _dispatch_kernel(run_len_ref, run_off_ref, run_dst_ref, pad_lo_ref, pad_hi_ref, pos_ref, h_ref, xs_ref, stage, zero_scr,
                     sem, pad_sem, *, tm, rows_per_pass):
    step = pl.program_id(0)
    na = TOP_K * tm
    d = h_ref.shape[1]
    half = d // 2
    n_words = half // LANES

    def pad_copy(slot, n_slots):
        return pltpu.make_async_copy(zero_scr.at[pl.ds(0, n_slots * SUBLANES), :],
                                     xs_ref.at[_slot_rows(slot, n_slots), :], pad_sem)

    def for_each_pad(fn):
        def per_expert(ex, carry):
            slot = pad_lo_ref[ex]
            n = pad_hi_ref[ex] - slot
            for size in _run_sizes(EXPERT_BLOCK - 1):
                take = (n & size) != 0

                @pl.when(take)
                def _(slot=slot, size=size):
                    fn(pad_copy(slot, size))

                slot = slot + jnp.where(take, size, 0)
            return carry
        lax.fori_loop(0, N_EXPERTS, per_expert, 0)

    @pl.when(step == 0)
    def _():
        zero_scr[...] = jnp.zeros_like(zero_scr)
        for_each_pad(lambda cp: cp.start())

    buf = step % 2
    pos = pos_ref[...]
    h = h_ref[...]
    for a0 in range(0, na, rows_per_pass):
        slot_id = a0 + _iota2((rows_per_pass, tm), 0)
        hit = pos[0:1, :] == slot_id
        for k in range(1, TOP_K):
            hit = jnp.logical_or(hit, pos[k:k + 1, :] == slot_id)
        rows = _dot(hit.astype(BF16), h)
        for i in range(n_words):
            word = _pack_halves(rows[:, i * LANES:(i + 1) * LANES], rows[:, half + i * LANES:half + (i + 1) * LANES])
            stage[buf, pl.ds(a0 * SUBLANES + i, rows_per_pass, stride=SUBLANES), :] = word

    def run_copy(tile_slot, sorted_slot, n_slots):
        return pltpu.make_async_copy(stage.at[buf, _slot_rows(tile_slot, n_slots), :],
                                     xs_ref.at[_slot_rows(sorted_slot, n_slots), :], sem.at[buf])

    def wait_tile(which):
        pltpu.make_async_copy(stage.at[which], xs_ref.at[pl.ds(0, na * SUBLANES), :], sem.at[which]).wait()

    _for_each_run(step, (run_len_ref, run_off_ref, run_dst_ref), tm, run_copy, lambda cp: cp.start())

    @pl.when(step > 0)
    def _():
        wait_tile(1 - buf)

    @pl.when(step == pl.num_programs(0) - 1)
    def _():
        wait_tile(buf)

    @pl.when(step == 0)
    def _():
        for_each_pad(lambda cp: cp.wait())


def _dispatch(runs, pad_lo, pad_hi, pos_t, h2, n_slots, tm):
    s, d = h2.shape
    assert (d // 2) % LANES == 0 and (d // 2) // LANES == SUBLANES, "one token row must pack into one (8, 128) tile"
    na = TOP_K * tm
    return pl.pallas_call(
        functools.partial(_dispatch_kernel, tm=tm, rows_per_pass=min(128, na)),
        grid_spec=pltpu.PrefetchScalarGridSpec(
            num_scalar_prefetch=5,
            grid=(s // tm,),
            in_specs=[pl.BlockSpec((TOP_K, tm), lambda i, *_: (0, i)),
                      pl.BlockSpec((tm, d), lambda i, *_: (i, 0))],
            out_specs=pl.BlockSpec(memory_space=pl.ANY),
            scratch_shapes=[pltpu.VMEM((2, na * SUBLANES, LANES), U32),
                            pltpu.VMEM((EXPERT_BLOCK // 2 * SUBLANES, LANES), U32),
                            pltpu.SemaphoreType.DMA((2,)), pltpu.SemaphoreType.DMA(())]),
        out_shape=jax.ShapeDtypeStruct((n_slots * SUBLANES, LANES), U32),
        compiler_params=_params(("arbitrary",), has_side_effects=True, disable_bounds_checks=True),
    )(*runs, pad_lo, pad_hi, pos_t, h2)


def _expert_kernel(be_ref, nu_ref, next_ref, par_ref, x_ref, wg_hbm, wu_hbm, wd_hbm, y_ref, wg_f32, wu_f32, wd_f32,
                   wg_scr, wu_scr, wd_scr, sem):
    b = pl.program_id(0)
    bm = EXPERT_BLOCK
    active = b < nu_ref[0]
    new_expert = jnp.logical_or(b == 0, be_ref[b] != be_ref[jnp.maximum(b - 1, 0)])

    def weight_copies(ex, which):
        return [pltpu.make_async_copy(src.at[ex], dst.at[which], sem.at[which])
                for src, dst in ((wg_hbm, wg_f32), (wu_hbm, wu_f32), (wd_hbm, wd_f32))]

    @pl.when(jnp.logical_and(active, new_expert))
    def _():
        which = par_ref[b]

        @pl.when(b == 0)
        def _():
            for cp in weight_copies(be_ref[b], which):
                cp.start()

        for cp in weight_copies(be_ref[b], which):
            cp.wait()
        wg_scr[...] = wg_f32[which].astype(BF16)
        wu_scr[...] = wu_f32[which].astype(BF16)
        wd_scr[...] = wd_f32[which].astype(BF16)
        nb = next_ref[b]

        @pl.when(nb < nu_ref[0])
        def _():
            for cp in weight_copies(be_ref[nb], 1 - which):
                cp.start()

    @pl.when(active)
    def _():
        los, his = [], []
        for i in range(SUBLANES):
            lo, hi = _unpack_halves(x_ref[pl.ds(i, bm, stride=SUBLANES), :])
            los.append(lo.astype(BF16))
            his.append(hi.astype(BF16))
        xb = jnp.concatenate(los + his, axis=1)
        hid = _silu(_dot(xb, wg_scr[...])) * _dot(xb, wu_scr[...])
        y = _dot(hid.astype(BF16), wd_scr[...])
        half = y.shape[1] // 2
        for i in range(SUBLANES):
            word = _pack_halves(_round_bf16(y[:, i * LANES:(i + 1) * LANES]),
                                _round_bf16(y[:, half + i * LANES:half + (i + 1) * LANES]))
            y_ref[pl.ds(i, bm, stride=SUBLANES), :] = word


def _experts(block_e, n_used, next_block, parity, xs, w_gate, w_up, w_down):
    d, ff = w_gate.shape[1], w_gate.shape[2]
    bm = EXPERT_BLOCK
    n_blocks = xs.shape[0] // (bm * SUBLANES)
    blk = lambda b, be, nu, *_: (jnp.minimum(b, nu[0] - 1), 0)
    hbm = pl.BlockSpec(memory_space=pl.ANY)
    return pl.pallas_call(
        _expert_kernel,
        grid_spec=pltpu.PrefetchScalarGridSpec(
            num_scalar_prefetch=4,
            grid=(n_blocks,),
            in_specs=[pl.BlockSpec((bm * SUBLANES, LANES), blk), hbm, hbm, hbm],
            out_specs=pl.BlockSpec((bm * SUBLANES, LANES), blk),
            scratch_shapes=[pltpu.VMEM((2, d, ff), F32), pltpu.VMEM((2, d, ff), F32), pltpu.VMEM((2, ff, d), F32),
                            pltpu.VMEM((d, ff), BF16), pltpu.VMEM((d, ff), BF16), pltpu.VMEM((ff, d), BF16),
                            pltpu.SemaphoreType.DMA((2,))]),
        out_shape=jax.ShapeDtypeStruct(xs.shape, U32),
        compiler_params=_params(("arbitrary",)),
    )(block_e, n_used, next_block, parity, xs, w_gate, w_up, w_down)


def _combine_kernel(run_len_ref, run_off_ref, run_src_ref, ys_ref, h_ref, x1_ref, pos_ref, wts_ref, wg_ref, wu_ref, wd_ref, gt_ref, gf_ref,
                    o_ref, stage, sem, *, tm, rows_per_pass):
    step = pl.program_id(0)
    na = TOP_K * tm
    buf = step % 2

    def fetch_tile(tile, which):
        def run_copy(tile_slot, sorted_slot, n_slots):
            return pltpu.make_async_copy(ys_ref.at[_slot_rows(sorted_slot, n_slots), :],
                                         stage.at[which, _slot_rows(tile_slot, n_slots), :], sem.at[which])
        _for_each_run(tile, (run_len_ref, run_off_ref, run_src_ref), tm, run_copy, lambda cp: cp.start())

    @pl.when(step == 0)
    def _():
        fetch_tile(step, buf)

    @pl.when(step + 1 < pl.num_programs(0))
    def _():
        fetch_tile(step + 1, 1 - buf)

    hb = h_ref[...]
    hid = _silu(_dot(hb, wg_ref[...])) * _dot(hb, wu_ref[...])
    acc = _dot(hid.astype(BF16), wd_ref[...])

    pltpu.make_async_copy(ys_ref.at[pl.ds(0, na * SUBLANES), :], stage.at[buf], sem.at[buf]).wait()

    pos = pos_ref[...]
    wts = wts_ref[...]
    for a0 in range(0, na, rows_per_pass):
        los, his = [], []
        for i in range(SUBLANES):
            lo, hi = _unpack_halves(stage[buf, pl.ds(a0 * SUBLANES + i, rows_per_pass, stride=SUBLANES), :])
            los.append(lo.astype(BF16))
            his.append(hi.astype(BF16))
        y_rows = jnp.concatenate(los + his, axis=1)
        slot_id = a0 + _iota2((rows_per_pass, tm), 0)
        wmat = jnp.zeros((rows_per_pass, tm), F32)
        for k in range(TOP_K):
            wmat = wmat + jnp.where(pos[k:k + 1, :] == slot_id, wts[k:k + 1, :], 0.0)
        acc = acc + _dg(wmat.astype(BF16), y_rows, TN)
    x2 = x1_ref[...] + gt_ref[...] * acc
    o_ref[...] = _rms(x2, NORM_EPS) * gf_ref[...]


def _combine(runs, ys, h2, x1, pos_t, wts_t, w_gate, w_up, w_down, gt2, gf, tm):
    s, d = x1.shape
    ff = w_gate.shape[1]
    na = TOP_K * tm
    const = lambda shape: pl.BlockSpec(shape, lambda i, *_: (0, 0), pipeline_mode=pl.Buffered(1))
    tile = lambda cols: pl.BlockSpec((tm, cols), lambda i, *_: (i, 0))
    per_k = pl.BlockSpec((TOP_K, tm), lambda i, *_: (0, i))
    return pl.pallas_call(
        functools.partial(_combine_kernel, tm=tm, rows_per_pass=min(256, na)),
        grid_spec=pltpu.PrefetchScalarGridSpec(
            num_scalar_prefetch=3,
            grid=(s // tm,),
            in_specs=[pl.BlockSpec(memory_space=pl.ANY),
                      tile(d), tile(d), per_k, per_k,
                      const((d, ff)), const((d, ff)), const((ff, d)), const((1, d)), const((1, d))],
            out_specs=tile(d),
            scratch_shapes=[pltpu.VMEM((2, na * SUBLANES, LANES), U32), pltpu.SemaphoreType.DMA((2,))]),
        out_shape=jax.ShapeDtypeStruct((s, d), F32),
        compiler_params=_params(("arbitrary",), disable_bounds_checks=True),
    )(*runs, ys, h2, x1, pos_t, wts_t, w_gate, w_up, w_down, gt2, gf)


def _mixer(x2d, mod, norm1_g, norm2_g, w_in, lb, hgrn_onorm_g, gdn_conv_w, gdn_a_log, gdn_dt_bias, gdn_onorm_g,
           w_branch_hgrn, w_branch_gdn, w_out, tiles):
    d = x2d.shape[1]
    sh1, sc1, gt1, sh2, sc2, _ = [mod[:, i * d:(i + 1) * d] for i in range(6)]
    key = HEADS * HEAD_DIM
    small0 = 4 * key + 3 * key
    small1 = small0 + 2 * HEADS
    w_in_t = jnp.swapaxes(w_in, 1, 2)
    w_main_t = _wprep(w_in_t, small0, small1, tiles["wprep_tn"])
    w_small_t = w_in_t[0, small0:small1, :].astype(BF16)
    proj, ab_t = _inproj(x2d, norm1_g, sc1, sh1, w_main_t, w_small_t, tiles["in_tm"], tiles["in_tn"])
    o_a = _hgrn(proj, lb, hgrn_onorm_g, tiles["mix_ts"])
    u, wqd, ku, attn, dl = _gdn_prep(proj, gdn_conv_w, ab_t, gdn_a_log, gdn_dt_bias, tiles["prep_ts"])
    o_b = _gdn_scan(u, wqd, ku, attn, dl, proj, gdn_onorm_g, tiles["mix_ts"])
    return _merge(o_a, o_b, proj, x2d, w_branch_hgrn.astype(BF16), w_branch_gdn.astype(BF16),
                  w_out.astype(BF16), gt1, norm2_g, sc2, sh2, tiles["merge_tm"])


def _moe(x1, h2, mod, norm2_g, normf_g, w_router, router_bias, w_exp_gate, w_exp_up, w_exp_down, w_sh_gate,
         w_sh_up, w_sh_down, tiles):
    s, d = x1.shape
    tm = tiles["moe_tm"]
    sh2, sc2, gt2 = [mod[:, i * d:(i + 1) * d] for i in (3, 4, 5)]
    pos_t, wts_t, before, ntile = _router(x1, norm2_g, sc2, sh2, w_router.T, router_bias.reshape(-1, 1), tm)
    bm = EXPERT_BLOCK
    n_blocks = -(-(s * TOP_K + N_EXPERTS * (bm - 1)) // bm)
    before = before[:, :, 0]
    ntile = ntile[:, :, 0]
    counts = before[-1] + ntile[-1]
    padded = (counts + bm - 1) // bm * bm
    earlier = jnp.arange(N_EXPERTS)[None, :] < jnp.arange(N_EXPERTS)[:, None]
    pstart = jnp.sum(jnp.where(earlier, padded[None, :], 0), axis=1).astype(I32)
    pend = pstart + padded
    block_start = jnp.arange(n_blocks, dtype=I32) * bm
    block_e = jnp.minimum(jnp.sum(pend[None, :] <= block_start[:, None], axis=1), N_EXPERTS - 1).astype(I32)
    n_used = pend[-1:] // bm
    run_off = jnp.sum(jnp.where(earlier[None], ntile[:, None, :], 0), axis=2)
    runs = (ntile.reshape(-1), run_off.reshape(-1), (before + pstart[None, :]).reshape(-1))
    xs = _dispatch(runs, pstart + counts, pend, pos_t, h2, n_blocks * bm, tm)
    own = block_e[:, None] == jnp.arange(N_EXPERTS)[None, :]
    next_block = jnp.sum(jnp.where(own, pend[None, :], 0), axis=1) // bm
    switches = jnp.concatenate([jnp.zeros((1,), I32), (block_e[1:] != block_e[:-1]).astype(I32)])
    upto = jnp.arange(n_blocks)[None, :] <= jnp.arange(n_blocks)[:, None]
    parity = jnp.sum(jnp.where(upto, switches[None, :], 0), axis=1).astype(I32) % 2
    ys = _experts(block_e, n_used, next_block, parity, xs, w_exp_gate, w_exp_up, w_exp_down)
    return _combine(runs, ys, h2, x1, pos_t, wts_t, w_sh_gate.astype(BF16), w_sh_up.astype(BF16),
                    w_sh_down.astype(BF16), gt2, normf_g, tm)


def _tiles(s):
    pick = lambda want: min(want, s)
    return dict(wprep_tn=512, in_tm=pick(1024), in_tn=1536, mix_ts=pick(512), prep_ts=pick(2048), merge_tm=pick(512),
                moe_tm=pick(256))


def kernel(x, c, w_ada, b_ada, norm1_g, norm2_g, w_in, hgrn_lb_table, hgrn_onorm_g, gdn_conv_w, gdn_a_log, gdn_dt_bias, gdn_onorm_g, w_branch_hgrn, w_branch_gdn, w_out, w_router, router_bias, w_exp_gate, w_exp_up, w_exp_down, w_sh_gate, w_sh_up, w_sh_down, normf_g):
    b, s, d = x.shape
    assert b == 1 and w_ada.shape[0] == 1, "one sequence, one layer"
    tiles = _tiles(s)
    lb = jnp.sum(jax.nn.softmax(hgrn_lb_table.astype(F32), axis=0)[0:1], axis=0, keepdims=True)
    mod = _ada(c, w_ada[0], b_ada[0])
    row = lambda v: v.reshape(1, -1)
    x1, h2 = _mixer(x[0], mod, row(norm1_g[0]), row(norm2_g[0]), w_in, lb, row(hgrn_onorm_g[0]), gdn_conv_w[0],
                    gdn_a_log[0], gdn_dt_bias[0], row(gdn_onorm_g[0]), w_branch_hgrn[0], w_branch_gdn[0], w_out[0],
                    tiles)
    out = _moe(x1, h2, mod, row(norm2_g[0]), row(normf_g), w_router[0], router_bias[0], w_exp_gate[0],
               w_exp_up[0], w_exp_down[0], w_sh_gate[0], w_sh_up[0], w_sh_down[0], tiles)
    return out[None]
```

```python
import functools

import jax
import jax.numpy as jnp
from jax import lax
from jax.experimental import pallas as pl
from jax.experimental.pallas import tpu as pltpu

F32 = jnp.float32
BF16 = jnp.bfloat16
I32 = jnp.int32
U32 = jnp.uint32

NORM_EPS = 1e-6
L2_EPS = 1e-6
HEADS = 8
HEAD_DIM = 128
CONV_WIDTH = 4
CHUNK = 64
N_EXPERTS = 64
N_GROUPS = 8
GROUP_SIZE = N_EXPERTS // N_GROUPS
TOPK_GROUPS = 4
TOP_K = 8
ROUTED_SCALE = 2.5
EXPERT_BLOCK = 512

LANES = 128
SUBLANES = 8
VMEM_LIMIT = 56 * 1024 * 1024

NT = (((1,), (1,)), ((), ()))
TN = (((0,), (0,)), ((), ()))


def _params(sem, **kw):
    return pltpu.CompilerParams(dimension_semantics=sem, vmem_limit_bytes=VMEM_LIMIT, **kw)


def _dot(a, b):
    return jnp.dot(a, b, preferred_element_type=F32)


def _dg(a, b, dims):
    return lax.dot_general(a, b, dims, preferred_element_type=F32)


def _split(x):
    hi = x.astype(BF16)
    lo = (x - hi.astype(F32)).astype(BF16)
    return hi, lo


def _dot_exact_lhs(a_bf16, x, dims=None):
    hi, lo = _split(x)
    if dims is None:
        return _dot(a_bf16, hi) + _dot(a_bf16, lo)
    return _dg(a_bf16, hi, dims) + _dg(a_bf16, lo, dims)


def _sigmoid(x):
    return 1.0 / (1.0 + jnp.exp(-x))


def _silu(x):
    return x * _sigmoid(x)


def _rms(x, eps):
    return x * lax.rsqrt(jnp.mean(x * x, axis=-1, keepdims=True) + eps)


def _iota2(shape, dim):
    return lax.broadcasted_iota(I32, shape, dim)


def _pack_halves(lo, hi):
    lo_bits = lax.shift_right_logical(pltpu.bitcast(lo, U32), U32(16))
    hi_bits = pltpu.bitcast(hi, U32) & U32(0xFFFF0000)
    return lo_bits | hi_bits


def _unpack_halves(word):
    lo = pltpu.bitcast(lax.shift_left(word, U32(16)), F32)
    hi = pltpu.bitcast(word & U32(0xFFFF0000), F32)
    return lo, hi


def _round_bf16(x):
    return x.astype(BF16).astype(F32)


def _ada_kernel(c_ref, w_ref, b_ref, o_ref):
    cond = _silu(c_ref[...])
    o_ref[...] = jnp.dot(cond, w_ref[...], preferred_element_type=F32,
                         precision=lax.Precision.HIGHEST) + b_ref[...]


def _ada(c, w_ada, b_ada):
    d, n = w_ada.shape
    tn = 1024
    c8 = jnp.broadcast_to(c, (SUBLANES, d))
    out = pl.pallas_call(
        _ada_kernel,
        grid=(n // tn,),
        in_specs=[pl.BlockSpec((SUBLANES, d), lambda j: (0, 0)),
                  pl.BlockSpec((d, tn), lambda j: (0, j)),
                  pl.BlockSpec((1, tn), lambda j: (0, j))],
        out_specs=pl.BlockSpec((SUBLANES, tn), lambda j: (0, j)),
        out_shape=jax.ShapeDtypeStruct((SUBLANES, n), F32),
        compiler_params=_params(("arbitrary",)),
    )(c8, w_ada, b_ada.reshape(1, n))
    return out[0:1]


def _wprep_kernel(a_ref, b_ref, o_ref, *, first_shifted, shift):
    j = pl.program_id(0)

    @pl.when(j < first_shifted)
    def _():
        o_ref[...] = a_ref[...].astype(BF16)

    @pl.when(j >= first_shifted)
    def _():
        tn = a_ref.shape[0]
        o_ref[0:tn - shift, :] = a_ref[shift:tn, :].astype(BF16)
        o_ref[tn - shift:tn, :] = b_ref[...].astype(BF16)


def _wprep(w_in_t, cut0, cut1, tn):
    _, n_in, d = w_in_t.shape
    shift = cut1 - cut0
    n_out = n_in - shift
    assert cut0 % tn == 0 and n_out % tn == 0 and tn % shift == 0 and shift % (2 * SUBLANES) == 0
    return pl.pallas_call(
        functools.partial(_wprep_kernel, first_shifted=cut0 // tn, shift=shift),
        grid=(n_out // tn,),
        in_specs=[pl.BlockSpec((None, tn, d), lambda j: (0, j, 0)),
                  pl.BlockSpec((None, shift, d), lambda j: (0, (j + 1) * (tn // shift), 0))],
        out_specs=pl.BlockSpec((tn, d), lambda j: (j, 0)),
        out_shape=jax.ShapeDtypeStruct((n_out, d), BF16),
        compiler_params=_params(("arbitrary",)),
    )(w_in_t, w_in_t)


def _inproj_kernel(x_ref, g_ref, sc_ref, sh_ref, w_ref, wst_ref, proj_ref, smallt_ref, h_scr):
    @pl.when(pl.program_id(1) == 0)
    def _():
        h = _rms(x_ref[...], NORM_EPS) * g_ref[...] * (1.0 + sc_ref[...]) + sh_ref[...]
        hb = h.astype(BF16)
        h_scr[...] = hb
        smallt_ref[...] = _dg(wst_ref[...], hb, NT)

    proj_ref[...] = _dg(h_scr[...], w_ref[...], NT).astype(BF16)


def _inproj(x, g, sc, sh, w_main_t, w_small_t, tm, tn):
    s, d = x.shape
    n = w_main_t.shape[0]
    ns = w_small_t.shape[0]
    row = lambda i, j: (0, 0)
    return pl.pallas_call(
        _inproj_kernel,
        grid=(s // tm, n // tn),
        in_specs=[pl.BlockSpec((tm, d), lambda i, j: (i, 0)),
                  pl.BlockSpec((1, d), row), pl.BlockSpec((1, d), row), pl.BlockSpec((1, d), row),
                  pl.BlockSpec((tn, d), lambda i, j: (j, 0)),
                  pl.BlockSpec((ns, d), row)],
        out_specs=[pl.BlockSpec((tm, tn), lambda i, j: (i, j)),
                   pl.BlockSpec((ns, tm), lambda i, j: (0, i))],
        out_shape=[jax.ShapeDtypeStruct((s, n), BF16), jax.ShapeDtypeStruct((ns, s), F32)],
        scratch_shapes=[pltpu.VMEM((tm, d), BF16)],
        compiler_params=_params(("arbitrary", "arbitrary")),
    )(x, g, sc, sh, w_main_t, w_small_t)


def _hgrn_kernel(q_ref, f_ref, i_ref, g_ref, lb_ref, on_ref, o_ref, st_scr, *, n_chunks):
    @pl.when(pl.program_id(0) == 0)
    def _():
        st_scr[...] = jnp.zeros_like(st_scr)

    c = CHUNK
    hd = HEAD_DIM
    causal = _iota2((c, c), 1) <= _iota2((c, c), 0)
    tri = causal.astype(BF16)
    lb = lb_ref[...]
    on_g = on_ref[...]
    heads = [slice(h * hd, (h + 1) * hd) for h in range(HEADS)]

    def chunk(n, carry):
        rows = pl.ds(pl.multiple_of(n * c, c), c)
        f = lb + (1.0 - lb) * _sigmoid(f_ref[rows, :].astype(F32))
        b = _dot_exact_lhs(tri, jnp.log(f))
        k = 1.0 - f
        q = _silu(q_ref[rows, :].astype(F32)) * (hd ** -0.5)
        v = i_ref[rows, :]
        b_mid = b[c // 2:c // 2 + 1, :]
        b_last = b[c - 1:c, :]
        qa = (q * jnp.exp(b - b_mid)).astype(BF16)
        ka = (k * jnp.exp(b_mid - b)).astype(BF16)
        qi = (q * jnp.exp(b)).astype(BF16)
        ku = (k * jnp.exp(b_last - b)).astype(BF16)
        dec = jnp.exp(b_last)
        gate = on_g * _silu(g_ref[rows, :].astype(F32))
        sts = [st_scr[h] for h in range(HEADS)]
        scores = [jnp.where(causal, _dg(qa[:, sl], ka[:, sl], NT), 0.0).astype(BF16) for sl in heads]
        inter = [_dg(qi[:, sl], st.astype(BF16), NT) for sl, st in zip(heads, sts)]
        kv = [_dg(v[:, sl], ku[:, sl], TN) for sl in heads]
        for h, sl in enumerate(heads):
            st_scr[h] = dec[:, sl] * sts[h] + kv[h]
        outs = [_rms(_dot(sc, v[:, sl]) + it, NORM_EPS) for sc, sl, it in zip(scores, heads, inter)]
        o_ref[rows, :] = (jnp.concatenate(outs, axis=1) * gate).astype(BF16)
        return carry

    lax.fori_loop(0, n_chunks, chunk, 0, unroll=4)


def _hgrn(proj, lb, onorm_g, ts):
    s = proj.shape[0]
    width = HEADS * HEAD_DIM
    col = lambda blk: pl.BlockSpec((ts, width), lambda j, blk=blk: (j, blk))
    const = pl.BlockSpec((1, width), lambda j: (0, 0))
    return pl.pallas_call(
        functools.partial(_hgrn_kernel, n_chunks=ts // CHUNK),
        grid=(s // ts,),
        in_specs=[col(0), col(1), col(2), col(3), const, const],
        out_specs=pl.BlockSpec((ts, width), lambda j: (j, 0)),
        out_shape=jax.ShapeDtypeStruct((s, width), BF16),
        scratch_shapes=[pltpu.VMEM((HEADS, HEAD_DIM, HEAD_DIM), F32)],
        compiler_params=_params(("arbitrary",)),
    )(proj, proj, proj, proj, lb, jnp.tile(onorm_g, (1, HEADS)))


def _gdn_prep_kernel(q_ref, k_ref, v_ref, qp_ref, kp_ref, vp_ref, wq_ref, wk_ref, wv_ref, ab_ref, alog_ref,
                     dtb_ref, tri_ref, eye_ref, u_ref, wqd_ref, ku_ref, attn_ref, dl_ref, cat_scr, rows_scr, cols_scr,
                     *, n_chunks, ts):
    h = pl.program_id(1)
    first = pl.program_id(0) == 0
    c = CHUNK
    hd = HEAD_DIM

    def conv_silu(cur_ref, prev_ref, w_ref):
        cat_scr[0:8, :] = jnp.where(first, 0.0, prev_ref[...].astype(F32))
        cat_scr[8:8 + ts, :] = cur_ref[...].astype(F32)
        acc = None
        for j in range(CONV_WIDTH):
            off = 8 - (CONV_WIDTH - 1) + j
            term = cat_scr[off:off + ts, :] * w_ref[j:j + 1, :]
            acc = term if acc is None else acc + term
        return _silu(acc)

    def l2n(x):
        return x * lax.rsqrt(jnp.sum(x * x, axis=-1, keepdims=True) + L2_EPS)

    q_all = l2n(conv_silu(q_ref, qp_ref, wq_ref)) * (hd ** -0.5)
    k_all = l2n(conv_silu(k_ref, kp_ref, wk_ref))
    v_all = conv_silu(v_ref, vp_ref, wv_ref)

    @pl.when(h == 0)
    def _():
        z = ab_ref[0:HEADS, :] + dtb_ref[...]
        softplus = jnp.maximum(z, 0.0) + jnp.log(1.0 + jnp.exp(-jnp.abs(z)))
        ld_rows = -jnp.exp(alog_ref[...]) * softplus
        hi, lo = _split(ld_rows)
        tri_blocks = tri_ref[...]
        w = tri_blocks.shape[0]
        spans = [slice(t0, t0 + w) for t0 in range(0, ts, w)]
        g_rows = jnp.concatenate([_dg(hi[:, sp], tri_blocks, NT) + _dg(lo[:, sp], tri_blocks, NT) for sp in spans],
                                 axis=1)
        beta_rows = _sigmoid(ab_ref[HEADS:2 * HEADS, :])
        rows_scr[...] = g_rows
        rows = jnp.concatenate([g_rows, beta_rows, jnp.zeros((LANES - 2 * HEADS, ts), F32)], axis=0)
        r_hi, r_lo = _split(rows)
        r_lo2 = (rows - r_hi.astype(F32) - r_lo.astype(F32)).astype(BF16)
        eye_w = eye_ref[...]
        for sp in spans:
            cols_scr[sp, :] = _dg(eye_w, r_hi[:, sp], NT) + _dg(eye_w, r_lo[:, sp], NT) + _dg(eye_w, r_lo2[:, sp], NT)

    lane = _iota2((ts, LANES), 1)
    cols = cols_scr[...]
    gc_all = jnp.sum(jnp.where(lane == h, cols, 0.0), axis=1, keepdims=True)
    bc_all = jnp.sum(jnp.where(lane == h + HEADS, cols, 0.0), axis=1, keepdims=True)
    g_row = rows_scr[pl.ds(h, 1), :]
    egc_all = jnp.exp(gc_all)

    r = _iota2((c, c), 0)
    cidx = _iota2((c, c), 1)
    causal = cidx <= r
    strict = cidx < r
    eye_f = (r == cidx).astype(F32)
    chunks = [slice(n * c, (n + 1) * c) for n in range(n_chunks)]

    q16 = q_all.astype(BF16)
    k16 = k_all.astype(BF16)
    kq = [_dg(jnp.concatenate([k16[sl], q16[sl]], axis=0), k16[sl], NT) for sl in chunks]
    dm = []
    for sl in chunks:
        diff = gc_all[sl] - g_row[:, sl]
        dm.append(jnp.where(causal, jnp.exp(jnp.where(causal, diff, 0.0)), 0.0))
    bm = [-jnp.where(strict, bc_all[sl] * x[0:c] * d, 0.0) for sl, x, d in zip(chunks, kq, dm)]
    p = [eye_f + b for b in bm]
    bm = [_dot(b.astype(BF16), b.astype(BF16)) for b in bm]
    for _ in range(c.bit_length() - 3):
        res = [_dot(b.astype(BF16), jnp.concatenate([b, pp], axis=1).astype(BF16)) for b, pp in zip(bm, p)]
        p = [pp + x[:, c:2 * c] for pp, x in zip(p, res)]
        bm = [x[:, 0:c] for x in res]
    p = [pp + _dot(b.astype(BF16), pp.astype(BF16)) for b, pp in zip(bm, p)]
    rhs = jnp.concatenate([v_all * bc_all, k_all * (bc_all * egc_all)], axis=1).astype(BF16)
    sol = [_dot(pp.astype(BF16), rhs[sl]) for pp, sl in zip(p, chunks)]
    qd_all = (q_all * egc_all).astype(BF16)
    for n, sl in enumerate(chunks):
        g_last = gc_all[(n + 1) * c - 1:(n + 1) * c, :]
        u_ref[sl, :] = sol[n][:, 0:hd].astype(BF16)
        wqd_ref[2 * n * c:(2 * n + 1) * c, :] = sol[n][:, hd:2 * hd].astype(BF16)
        wqd_ref[(2 * n + 1) * c:(2 * n + 2) * c, :] = qd_all[sl]
        ku_ref[sl, :] = (k_all[sl] * jnp.exp(g_last - gc_all[sl])).astype(BF16)
        attn_ref[sl, :] = (kq[n][c:2 * c] * dm[n]).astype(BF16)
        dl_ref[n:n + 1, :] = jnp.broadcast_to(jnp.exp(g_last), (1, hd))


def _gdn_prep(proj, conv_w, ab_t, a_log, dt_bias, ts):
    s = proj.shape[0]
    hd = HEAD_DIM
    c = CHUNK
    q0 = 4 * HEADS
    cur = lambda off: pl.BlockSpec((ts, hd), lambda j, h, off=off: (j, off + h))
    prev = lambda off: pl.BlockSpec((8, hd), lambda j, h, off=off: (jnp.maximum(j * (ts // 8) - 1, 0), off + h))
    cw = lambda off: pl.BlockSpec((CONV_WIDTH, hd), lambda j, h, off=off: (0, off + h))
    per_head_scalar = pl.BlockSpec((HEADS, 1), lambda j, h: (0, 0))
    w = min(ts, 2 * LANES)
    const = pl.BlockSpec((w, w), lambda j, h: (0, 0))
    pos = jnp.arange(w)
    tri_blocks = ((pos[:, None] // c == pos[None, :] // c) & (pos[None, :] <= pos[:, None])).astype(BF16)
    eye = (pos[:, None] == pos[None, :]).astype(BF16)
    per_head = lambda rows, cols: pl.BlockSpec((None, rows, cols), lambda j, h: (h, j, 0))
    return pl.pallas_call(
        functools.partial(_gdn_prep_kernel, n_chunks=ts // c, ts=ts),
        grid=(s // ts, HEADS),
        in_specs=[cur(q0), cur(q0 + HEADS), cur(q0 + 2 * HEADS),
                  prev(q0), prev(q0 + HEADS), prev(q0 + 2 * HEADS),
                  cw(0), cw(HEADS), cw(2 * HEADS),
                  pl.BlockSpec((2 * HEADS, ts), lambda j, h: (0, j)),
                  per_head_scalar, per_head_scalar, const, const],
        out_specs=[pl.BlockSpec((ts, hd), lambda j, h: (j, h)),
                   pl.BlockSpec((2 * ts, hd), lambda j, h: (j, h)),
                   pl.BlockSpec((ts, hd), lambda j, h: (j, h)),
                   per_head(ts, c),
                   per_head(ts // c, hd)],
        out_shape=[jax.ShapeDtypeStruct((s, HEADS * hd), BF16),
                   jax.ShapeDtypeStruct((2 * s, HEADS * hd), BF16),
                   jax.ShapeDtypeStruct((s, HEADS * hd), BF16),
                   jax.ShapeDtypeStruct((HEADS, s, c), BF16),
                   jax.ShapeDtypeStruct((HEADS, s // c, hd), F32)],
        scratch_shapes=[pltpu.VMEM((ts + 8, hd), F32), pltpu.VMEM((HEADS, ts), F32), pltpu.VMEM((ts, LANES), F32)],
        compiler_params=_params(("arbitrary", "arbitrary")),
    )(proj, proj, proj, proj, proj, proj, conv_w, conv_w, conv_w,
      ab_t, a_log.reshape(HEADS, 1), dt_bias.reshape(HEADS, 1), tri_blocks, eye)


def _gdn_scan_kernel(u_ref, wqd_ref, ku_ref, attn_ref, dl_ref, g_ref, on_ref, o_ref, st_scr, *, n_chunks):
    @pl.when(pl.program_id(0) == 0)
    def _():
        st_scr[...] = jnp.zeros_like(st_scr)

    c = CHUNK
    hd = HEAD_DIM
    on_g = on_ref[...]
    heads = [slice(h * hd, (h + 1) * hd) for h in range(HEADS)]

    def chunk(n, carry):
        rows = pl.ds(pl.multiple_of(n * c, c), c)
        rows2 = pl.ds(pl.multiple_of(2 * n * c, 2 * c), 2 * c)
        sts = [st_scr[h] for h in range(HEADS)]
        wq = [_dot(wqd_ref[rows2, sl], st.astype(BF16)) for sl, st in zip(heads, sts)]
        vn = [(u_ref[rows, sl].astype(F32) - x[0:c]).astype(BF16) for sl, x in zip(heads, wq)]
        upd = [_dg(ku_ref[rows, sl], v, TN) for sl, v in zip(heads, vn)]
        for h in range(HEADS):
            st_scr[h] = dl_ref[h, pl.ds(n, 1), :] * sts[h] + upd[h]
        outs = [_rms(x[c:2 * c] + _dot(attn_ref[h, rows, :], v), NORM_EPS)
                for h, (x, v) in enumerate(zip(wq, vn))]
        gate = jnp.tile(on_g, (1, HEADS)) * _silu(g_ref[rows, :].astype(F32))
        o_ref[rows, :] = (jnp.concatenate(outs, axis=1) * gate).astype(BF16)
        return carry

    lax.fori_loop(0, n_chunks, chunk, 0, unroll=4)


def _gdn_scan(u, wqd, ku, attn, dl, proj, onorm_g, ts):
    s, width = u.shape
    c = CHUNK
    gate_blk = (4 * HEADS + 3 * HEADS) * HEAD_DIM // width
    return pl.pallas_call(
        functools.partial(_gdn_scan_kernel, n_chunks=ts // c),
        grid=(s // ts,),
        in_specs=[pl.BlockSpec((ts, width), lambda j: (j, 0)),
                  pl.BlockSpec((2 * ts, width), lambda j: (j, 0)),
                  pl.BlockSpec((ts, width), lambda j: (j, 0)),
                  pl.BlockSpec((HEADS, ts, c), lambda j: (0, j, 0)),
                  pl.BlockSpec((HEADS, ts // c, HEAD_DIM), lambda j: (0, j, 0)),
                  pl.BlockSpec((ts, width), lambda j: (j, gate_blk)),
                  pl.BlockSpec((1, HEAD_DIM), lambda j: (0, 0))],
        out_specs=pl.BlockSpec((ts, width), lambda j: (j, 0)),
        out_shape=jax.ShapeDtypeStruct((s, width), BF16),
        scratch_shapes=[pltpu.VMEM((HEADS, HEAD_DIM, HEAD_DIM), F32)],
        compiler_params=_params(("arbitrary",)),
    )(u, wqd, ku, attn, dl, proj, onorm_g)


def _merge_kernel(oa_ref, ob_ref, mga_ref, mgb_ref, x_ref, wa_ref, wb_ref, wo_ref, gt_ref, g2_ref, sc_ref,
                  sh_ref, x1_ref, h2_ref):
    ya = _dot(oa_ref[...], wa_ref[...])
    yb = _dot(ob_ref[...], wb_ref[...])
    merged = _sigmoid(mga_ref[...].astype(F32)) * ya + _sigmoid(mgb_ref[...].astype(F32)) * yb
    x1 = x_ref[...] + gt_ref[...] * _dot(merged.astype(BF16), wo_ref[...])
    x1_ref[...] = x1
    h2 = _rms(x1, NORM_EPS) * g2_ref[...] * (1.0 + sc_ref[...]) + sh_ref[...]
    h2_ref[...] = h2.astype(BF16)


def _merge(o_a, o_b, proj, x, w_a, w_b, w_o, gt1, g2, sc2, sh2, tm):
    s, d = x.shape
    dv = o_a.shape[1]
    mg0 = (8 * HEADS * HEAD_DIM) // d
    const = lambda shape: pl.BlockSpec(shape, lambda i: (0, 0), pipeline_mode=pl.Buffered(1))
    return pl.pallas_call(
        _merge_kernel,
        grid=(s // tm,),
        in_specs=[pl.BlockSpec((tm, dv), lambda i: (i, 0)),
                  pl.BlockSpec((tm, dv), lambda i: (i, 0)),
                  pl.BlockSpec((tm, d), lambda i: (i, mg0)),
                  pl.BlockSpec((tm, d), lambda i: (i, mg0 + 1)),
                  pl.BlockSpec((tm, d), lambda i: (i, 0)),
                  const((dv, d)), const((dv, d)), const((d, d)),
                  const((1, d)), const((1, d)), const((1, d)), const((1, d))],
        out_specs=[pl.BlockSpec((tm, d), lambda i: (i, 0)), pl.BlockSpec((tm, d), lambda i: (i, 0))],
        out_shape=[jax.ShapeDtypeStruct((s, d), F32), jax.ShapeDtypeStruct((s, d), BF16)],
        compiler_params=_params(("arbitrary",)),
    )(o_a, o_b, proj, proj, x, w_a, w_b, w_o, gt1, g2, sc2, sh2)


def _first_max(vals, iota, size, axis):
    m = jnp.max(vals, axis=axis, keepdims=True)
    idx = jnp.min(jnp.where(vals == m, iota, size), axis=axis, keepdims=True)
    return m, idx


def _router_kernel(x1_ref, g2_ref, sc_ref, sh_ref, wrt_ref, bias_ref, upper_ref, pos_ref, wts_ref, before_ref,
                   ntile_ref, cnt_scr, *, tm):
    @pl.when(pl.program_id(0) == 0)
    def _():
        cnt_scr[...] = jnp.zeros_like(cnt_scr)

    e = N_EXPERTS
    h2 = _rms(x1_ref[...], NORM_EPS) * g2_ref[...] * (1.0 + sc_ref[...]) + sh_ref[...]
    logits = lax.dot_general(wrt_ref[...], h2, NT, preferred_element_type=F32,
                             precision=lax.Precision.HIGHEST)
    scores = _sigmoid(logits)
    biased = scores + bias_ref[...]
    neg = -jnp.inf

    g3 = biased.reshape(N_GROUPS, GROUP_SIZE, tm)
    i3 = lax.broadcasted_iota(I32, g3.shape, 1)
    m1, a1 = _first_max(g3, i3, GROUP_SIZE, 1)
    m2 = jnp.max(jnp.where(i3 == a1, neg, g3), axis=1, keepdims=True)
    gs = (m1 + m2).reshape(N_GROUPS, tm)
    ig = _iota2(gs.shape, 0)
    gmask = jnp.zeros(gs.shape, jnp.bool_)
    for _ in range(TOPK_GROUPS):
        _, a = _first_max(gs, ig, N_GROUPS, 0)
        pick = ig == a
        gmask = jnp.logical_or(gmask, pick)
        gs = jnp.where(pick, neg, gs)
    emask = jnp.broadcast_to(gmask.reshape(N_GROUPS, 1, tm), (N_GROUPS, GROUP_SIZE, tm)).reshape(e, tm)

    cand = jnp.where(emask, biased, neg)
    ie = _iota2((e, tm), 0)
    sel_all = jnp.zeros((e, tm), jnp.bool_)
    w_rows, picks = [], []
    for _ in range(TOP_K):
        _, a = _first_max(cand, ie, e, 0)
        pick = ie == a
        picks.append(pick)
        w_rows.append(jnp.sum(jnp.where(pick, scores, 0.0), axis=0, keepdims=True))
        sel_all = jnp.logical_or(sel_all, pick)
        cand = jnp.where(pick, neg, cand)
    w_sum = w_rows[0]
    for wr in w_rows[1:]:
        w_sum = w_sum + wr
    wts = jnp.concatenate(w_rows, axis=0) / w_sum * ROUTED_SCALE

    sel = sel_all.astype(BF16)
    in_expert = _dot(sel, upper_ref[...])
    n_tile = jnp.sum(sel_all.astype(F32), axis=1, keepdims=True)
    lower = (_iota2((e, e), 1) < _iota2((e, e), 0)).astype(BF16)
    expert_off = _dot_exact_lhs(lower, jnp.broadcast_to(n_tile, (e, LANES)))[:, 0:1]
    place = in_expert + expert_off
    pos = jnp.concatenate([jnp.sum(jnp.where(pk, place, 0.0), axis=0, keepdims=True) for pk in picks], axis=0)
    pos_ref[...] = pos.astype(I32)
    before_ref[...] = jnp.broadcast_to(cnt_scr[...], before_ref.shape).astype(I32)
    ntile_ref[...] = jnp.broadcast_to(n_tile, ntile_ref.shape).astype(I32)
    cnt_scr[...] = cnt_scr[...] + n_tile
    wts_ref[...] = wts


def _router(x1, g2, sc2, sh2, w_router_t, bias_col, tm):
    s, d = x1.shape
    e = N_EXPERTS
    nt = s // tm
    upper = (jnp.arange(tm)[:, None] < jnp.arange(tm)[None, :]).astype(BF16)
    const = lambda shape: pl.BlockSpec(shape, lambda i: (0, 0))
    per_tile = pl.BlockSpec((None, e, LANES), lambda i: (i, 0, 0))
    return pl.pallas_call(
        functools.partial(_router_kernel, tm=tm),
        grid=(nt,),
        in_specs=[pl.BlockSpec((tm, d), lambda i: (i, 0)),
                  const((1, d)), const((1, d)), const((1, d)),
                  const((e, d)), const((e, 1)), const((tm, tm))],
        out_specs=[pl.BlockSpec((TOP_K, tm), lambda i: (0, i)),
                   pl.BlockSpec((TOP_K, tm), lambda i: (0, i)),
                   per_tile, per_tile],
        out_shape=[jax.ShapeDtypeStruct((TOP_K, s), I32), jax.ShapeDtypeStruct((TOP_K, s), F32),
                   jax.ShapeDtypeStruct((nt, e, LANES), I32), jax.ShapeDtypeStruct((nt, e, LANES), I32)],
        scratch_shapes=[pltpu.VMEM((e, 1), F32)],
        compiler_params=_params(("arbitrary",)),
    )(x1, g2, sc2, sh2, w_router_t, bias_col, upper)


LONG_RUN = 64


def _run_sizes(limit):
    return [1 << b for b in range(limit.bit_length() - 1, -1, -1)]


def _for_each_run(tile, run_refs, tm, make_copy, fn):
    run_len_ref, run_off_ref, run_dst_ref = run_refs

    def per_expert(ex, carry):
        n = run_len_ref[tile * N_EXPERTS + ex]
        off = run_off_ref[tile * N_EXPERTS + ex]
        dst = run_dst_ref[tile * N_EXPERTS + ex]
        def pieces(sizes):
            for size in sizes:
                done = n & (-2 * size)

                @pl.when((n & size) != 0)
                def _(done=done, size=size):
                    fn(make_copy(off + done, dst + done, size))

        sizes = _run_sizes(tm)
        pieces([size for size in sizes if size < LONG_RUN])

        @pl.when(n >= LONG_RUN)
        def _():
            pieces([size for size in sizes if size >= LONG_RUN])

        return carry

    lax.fori_loop(0, N_EXPERTS, per_expert, 0)


def _slot_rows(slot, n_slots):
    return pl.ds(pl.multiple_of(slot * SUBLANES, SUBLANES), n_slots * SUBLANES)


def _dispatch_kernel(run_len_ref, run_off_ref, run_dst_ref, pad_lo_ref, pad_hi_ref, pos_ref, h_ref, xs_ref, stage, zero_scr,
                     sem, pad_sem, *, tm, rows_per_pass):
    step = pl.program_id(0)
    na = TOP_K * tm
    d = h_ref.shape[1]
    half = d // 2
    n_words = half // LANES

    def pad_copy(slot, n_slots):
        return pltpu.make_async_copy(zero_scr.at[pl.ds(0, n_slots * SUBLANES), :],
                                     xs_ref.at[_slot_rows(slot, n_slots), :], pad_sem)

    def for_each_pad(fn):
        def per_expert(ex, carry):
            slot = pad_lo_ref[ex]
            n = pad_hi_ref[ex] - slot
            for size in _run_sizes(EXPERT_BLOCK - 1):
                take = (n & size) != 0

                @pl.when(take)
                def _(slot=slot, size=size):
                    fn(pad_copy(slot, size))

                slot = slot + jnp.where(take, size, 0)
            return carry
        lax.fori_loop(0, N_EXPERTS, per_expert, 0)

    @pl.when(step == 0)
    def _():
        zero_scr[...] = jnp.zeros_like(zero_scr)
        for_each_pad(lambda cp: cp.start())

    buf = step % 2
    pos = pos_ref[...]
    h = h_ref[...]
    for a0 in range(0, na, rows_per_pass):
        slot_id = a0 + _iota2((rows_per_pass, tm), 0)
        hit = pos[0:1, :] == slot_id
        for k in range(1, TOP_K):
            hit = jnp.logical_or(hit, pos[k:k + 1, :] == slot_id)
        rows = _dot(hit.astype(BF16), h)
        for i in range(n_words):
            word = _pack_halves(rows[:, i * LANES:(i + 1) * LANES], rows[:, half + i * LANES:half + (i + 1) * LANES])
            stage[buf, pl.ds(a0 * SUBLANES + i, rows_per_pass, stride=SUBLANES), :] = word

    def run_copy(tile_slot, sorted_slot, n_slots):
        return pltpu.make_async_copy(stage.at[buf, _slot_rows(tile_slot, n_slots), :],
                                     xs_ref.at[_slot_rows(sorted_slot, n_slots), :], sem.at[buf])

    def wait_tile(which):
        pltpu.make_async_copy(stage.at[which], xs_ref.at[pl.ds(0, na * SUBLANES), :], sem.at[which]).wait()

    _for_each_run(step, (run_len_ref, run_off_ref, run_dst_ref), tm, run_copy, lambda cp: cp.start())

    @pl.when(step > 0)
    def _():
        wait_tile(1 - buf)

    @pl.when(step == pl.num_programs(0) - 1)
    def _():
        wait_tile(buf)

    @pl.when(step == 0)
    def _():
        for_each_pad(lambda cp: cp.wait())


def _dispatch(runs, pad_lo, pad_hi, pos_t, h2, n_slots, tm):
    s, d = h2.shape
    assert (d // 2) % LANES == 0 and (d // 2) // LANES == SUBLANES, "one token row must pack into one (8, 128) tile"
    na = TOP_K * tm
    return pl.pallas_call(
        functools.partial(_dispatch_kernel, tm=tm, rows_per_pass=min(512, na)),
        grid_spec=pltpu.PrefetchScalarGridSpec(
            num_scalar_prefetch=5,
            grid=(s // tm,),
            in_specs=[pl.BlockSpec((TOP_K, tm), lambda i, *_: (0, i)),
                      pl.BlockSpec((tm, d), lambda i, *_: (i, 0))],
            out_specs=pl.BlockSpec(memory_space=pl.ANY),
            scratch_shapes=[pltpu.VMEM((2, na * SUBLANES, LANES), U32),
                            pltpu.VMEM((EXPERT_BLOCK // 2 * SUBLANES, LANES), U32),
                            pltpu.SemaphoreType.DMA((2,)), pltpu.SemaphoreType.DMA(())]),
        out_shape=jax.ShapeDtypeStruct((n_slots * SUBLANES, LANES), U32),
        compiler_params=_params(("arbitrary",), has_side_effects=True, disable_bounds_checks=True),
    )(*runs, pad_lo, pad_hi, pos_t, h2)


def _expert_kernel(be_ref, nu_ref, next_ref, par_ref, x_ref, wg_hbm, wu_hbm, wd_hbm, y_ref, wg_f32, wu_f32, wd_f32,
                   wg_scr, wu_scr, wd_scr, sem):
    b = pl.program_id(0)
    bm = EXPERT_BLOCK
    active = b < nu_ref[0]
    new_expert = jnp.logical_or(b == 0, be_ref[b] != be_ref[jnp.maximum(b - 1, 0)])

    def weight_copies(ex, which):
        return [pltpu.make_async_copy(src.at[ex], dst.at[which], sem.at[which])
                for src, dst in ((wg_hbm, wg_f32), (wu_hbm, wu_f32), (wd_hbm, wd_f32))]

    @pl.when(jnp.logical_and(active, new_expert))
    def _():
        which = par_ref[b]

        @pl.when(b == 0)
        def _():
            for cp in weight_copies(be_ref[b], which):
                cp.start()

        for cp in weight_copies(be_ref[b], which):
            cp.wait()
        wg_scr[...] = wg_f32[which].astype(BF16)
        wu_scr[...] = wu_f32[which].astype(BF16)
        wd_scr[...] = wd_f32[which].astype(BF16)
        nb = next_ref[b]

        @pl.when(nb < nu_ref[0])
        def _():
            for cp in weight_copies(be_ref[nb], 1 - which):
                cp.start()

    @pl.when(active)
    def _():
        los, his = [], []
        for i in range(SUBLANES):
            lo, hi = _unpack_halves(x_ref[pl.ds(i, bm, stride=SUBLANES), :])
            los.append(lo.astype(BF16))
            his.append(hi.astype(BF16))
        xb = jnp.concatenate(los + his, axis=1)
        hid = _silu(_dot(xb, wg_scr[...])) * _dot(xb, wu_scr[...])
        y = _dot(hid.astype(BF16), wd_scr[...])
        half = y.shape[1] // 2
        for i in range(SUBLANES):
            word = _pack_halves(_round_bf16(y[:, i * LANES:(i + 1) * LANES]),
                                _round_bf16(y[:, half + i * LANES:half + (i + 1) * LANES]))
            y_ref[pl.ds(i, bm, stride=SUBLANES), :] = word


def _experts(block_e, n_used, next_block, parity, xs, w_gate, w_up, w_down):
    d, ff = w_gate.shape[1], w_gate.shape[2]
    bm = EXPERT_BLOCK
    n_blocks = xs.shape[0] // (bm * SUBLANES)
    blk = lambda b, be, nu, *_: (jnp.minimum(b, nu[0] - 1), 0)
    hbm = pl.BlockSpec(memory_space=pl.ANY)
    return pl.pallas_call(
        _expert_kernel,
        grid_spec=pltpu.PrefetchScalarGridSpec(
            num_scalar_prefetch=4,
            grid=(n_blocks,),
            in_specs=[pl.BlockSpec((bm * SUBLANES, LANES), blk), hbm, hbm, hbm],
            out_specs=pl.BlockSpec((bm * SUBLANES, LANES), blk),
            scratch_shapes=[pltpu.VMEM((2, d, ff), F32), pltpu.VMEM((2, d, ff), F32), pltpu.VMEM((2, ff, d), F32),
                            pltpu.VMEM((d, ff), BF16), pltpu.VMEM((d, ff), BF16), pltpu.VMEM((ff, d), BF16),
                            pltpu.SemaphoreType.DMA((2,))]),
        out_shape=jax.ShapeDtypeStruct(xs.shape, U32),
        compiler_params=_params(("arbitrary",)),
    )(block_e, n_used, next_block, parity, xs, w_gate, w_up, w_down)


def _combine_kernel(run_len_ref, run_off_ref, run_src_ref, ys_ref, h_ref, x1_ref, pos_ref, wts_ref, wg_ref, wu_ref, wd_ref, gt_ref, gf_ref,
                    o_ref, stage, sem, *, tm, rows_per_pass):
    step = pl.program_id(0)
    na = TOP_K * tm
    buf = step % 2

    def fetch_tile(tile, which):
        def run_copy(tile_slot, sorted_slot, n_slots):
            return pltpu.make_async_copy(ys_ref.at[_slot_rows(sorted_slot, n_slots), :],
                                         stage.at[which, _slot_rows(tile_slot, n_slots), :], sem.at[which])
        _for_each_run(tile, (run_len_ref, run_off_ref, run_src_ref), tm, run_copy, lambda cp: cp.start())

    @pl.when(step == 0)
    def _():
        fetch_tile(step, buf)

    @pl.when(step + 1 < pl.num_programs(0))
    def _():
        fetch_tile(step + 1, 1 - buf)

    hb = h_ref[...]
    hid = _silu(_dot(hb, wg_ref[...])) * _dot(hb, wu_ref[...])
    acc = _dot(hid.astype(BF16), wd_ref[...])

    pltpu.make_async_copy(ys_ref.at[pl.ds(0, na * SUBLANES), :], stage.at[buf], sem.at[buf]).wait()

    pos = pos_ref[...]
    wts = wts_ref[...]
    for a0 in range(0, na, rows_per_pass):
        los, his = [], []
        for i in range(SUBLANES):
            lo, hi = _unpack_halves(stage[buf, pl.ds(a0 * SUBLANES + i, rows_per_pass, stride=SUBLANES), :])
            los.append(lo.astype(BF16))
            his.append(hi.astype(BF16))
        y_rows = jnp.concatenate(los + his, axis=1)
        slot_id = a0 + _iota2((rows_per_pass, tm), 0)
        wmat = jnp.zeros((rows_per_pass, tm), F32)
        for k in range(TOP_K):
            wmat = wmat + jnp.where(pos[k:k + 1, :] == slot_id, wts[k:k + 1, :], 0.0)
        acc = acc + _dg(wmat.astype(BF16), y_rows, TN)
    x2 = x1_ref[...] + gt_ref[...] * acc
    o_ref[...] = _rms(x2, NORM_EPS) * gf_ref[...]


def _combine(runs, ys, h2, x1, pos_t, wts_t, w_gate, w_up, w_down, gt2, gf, tm):
    s, d = x1.shape
    ff = w_gate.shape[1]
    na = TOP_K * tm
    const = lambda shape: pl.BlockSpec(shape, lambda i, *_: (0, 0), pipeline_mode=pl.Buffered(1))
    tile = lambda cols: pl.BlockSpec((tm, cols), lambda i, *_: (i, 0))
    per_k = pl.BlockSpec((TOP_K, tm), lambda i, *_: (0, i))
    return pl.pallas_call(
        functools.partial(_combine_kernel, tm=tm, rows_per_pass=min(256, na)),
        grid_spec=pltpu.PrefetchScalarGridSpec(
            num_scalar_prefetch=3,
            grid=(s // tm,),
            in_specs=[pl.BlockSpec(memory_space=pl.ANY),
                      tile(d), tile(d), per_k, per_k,
                      const((d, ff)), const((d, ff)), const((ff, d)), const((1, d)), const((1, d))],
            out_specs=tile(d),
            scratch_shapes=[pltpu.VMEM((2, na * SUBLANES, LANES), U32), pltpu.SemaphoreType.DMA((2,))]),
        out_shape=jax.ShapeDtypeStruct((s, d), F32),
        compiler_params=_params(("arbitrary",), disable_bounds_checks=True),
    )(*runs, ys, h2, x1, pos_t, wts_t, w_gate, w_up, w_down, gt2, gf)


def _mixer(x2d, mod, norm1_g, norm2_g, w_in, lb, hgrn_onorm_g, gdn_conv_w, gdn_a_log, gdn_dt_bias, gdn_onorm_g,
           w_branch_hgrn, w_branch_gdn, w_out, tiles):
    d = x2d.shape[1]
    sh1, sc1, gt1, sh2, sc2, _ = [mod[:, i * d:(i + 1) * d] for i in range(6)]
    key = HEADS * HEAD_DIM
    small0 = 4 * key + 3 * key
    small1 = small0 + 2 * HEADS
    w_in_t = jnp.swapaxes(w_in, 1, 2)
    w_main_t = _wprep(w_in_t, small0, small1, tiles["wprep_tn"])
    w_small_t = w_in_t[0, small0:small1, :].astype(BF16)
    proj, ab_t = _inproj(x2d, norm1_g, sc1, sh1, w_main_t, w_small_t, tiles["in_tm"], tiles["in_tn"])
    o_a = _hgrn(proj, lb, hgrn_onorm_g, tiles["mix_ts"])
    u, wqd, ku, attn, dl = _gdn_prep(proj, gdn_conv_w, ab_t, gdn_a_log, gdn_dt_bias, tiles["prep_ts"])
    o_b = _gdn_scan(u, wqd, ku, attn, dl, proj, gdn_onorm_g, tiles["mix_ts"])
    return _merge(o_a, o_b, proj, x2d, w_branch_hgrn.astype(BF16), w_branch_gdn.astype(BF16),
                  w_out.astype(BF16), gt1, norm2_g, sc2, sh2, tiles["merge_tm"])


def _moe(x1, h2, mod, norm2_g, normf_g, w_router, router_bias, w_exp_gate, w_exp_up, w_exp_down, w_sh_gate,
         w_sh_up, w_sh_down, tiles):
    s, d = x1.shape
    tm = tiles["moe_tm"]
    sh2, sc2, gt2 = [mod[:, i * d:(i + 1) * d] for i in (3, 4, 5)]
    pos_t, wts_t, before, ntile = _router(x1, norm2_g, sc2, sh2, w_router.T, router_bias.reshape(-1, 1), tm)
    bm = EXPERT_BLOCK
    n_blocks = -(-(s * TOP_K + N_EXPERTS * (bm - 1)) // bm)
    before = before[:, :, 0]
    ntile = ntile[:, :, 0]
    counts = before[-1] + ntile[-1]
    padded = (counts + bm - 1) // bm * bm
    earlier = jnp.arange(N_EXPERTS)[None, :] < jnp.arange(N_EXPERTS)[:, None]
    pstart = jnp.sum(jnp.where(earlier, padded[None, :], 0), axis=1).astype(I32)
    pend = pstart + padded
    block_start = jnp.arange(n_blocks, dtype=I32) * bm
    block_e = jnp.minimum(jnp.sum(pend[None, :] <= block_start[:, None], axis=1), N_EXPERTS - 1).astype(I32)
    n_used = pend[-1:] // bm
    run_off = jnp.sum(jnp.where(earlier[None], ntile[:, None, :], 0), axis=2)
    runs = (ntile.reshape(-1), run_off.reshape(-1), (before + pstart[None, :]).reshape(-1))
    xs = _dispatch(runs, pstart + counts, pend, pos_t, h2, n_blocks * bm, tm)
    own = block_e[:, None] == jnp.arange(N_EXPERTS)[None, :]
    next_block = jnp.sum(jnp.where(own, pend[None, :], 0), axis=1) // bm
    switches = jnp.concatenate([jnp.zeros((1,), I32), (block_e[1:] != block_e[:-1]).astype(I32)])
    upto = jnp.arange(n_blocks)[None, :] <= jnp.arange(n_blocks)[:, None]
    parity = jnp.sum(jnp.where(upto, switches[None, :], 0), axis=1).astype(I32) % 2
    ys = _experts(block_e, n_used, next_block, parity, xs, w_exp_gate, w_exp_up, w_exp_down)
    return _combine(runs, ys, h2, x1, pos_t, wts_t, w_sh_gate.astype(BF16), w_sh_up.astype(BF16),
                    w_sh_down.astype(BF16), gt2, normf_g, tm)


def _tiles(s):
    pick = lambda want: min(want, s)
    return dict(wprep_tn=512, in_tm=pick(1024), in_tn=1536, mix_ts=pick(512), prep_ts=pick(2048), merge_tm=pick(512),
                moe_tm=pick(256))


def kernel(x, c, w_ada, b_ada, norm1_g, norm2_g, w_in, hgrn_lb_table, hgrn_onorm_g, gdn_conv_w, gdn_a_log, gdn_dt_bias, gdn_onorm_g, w_branch_hgrn, w_branch_gdn, w_out, w_router, router_bias, w_exp_gate, w_exp_up, w_exp_down, w_sh_gate, w_sh_up, w_sh_down, normf_g):
    b, s, d = x.shape
    assert b == 1 and w_ada.shape[0] == 1, "one sequence, one layer"
    tiles = _tiles(s)
    lb = jnp.sum(jax.nn.softmax(hgrn_lb_table.astype(F32), axis=0)[0:1], axis=0, keepdims=True)
    mod = _ada(c, w_ada[0], b_ada[0])
    row = lambda v: v.reshape(1, -1)
    x1, h2 = _mixer(x[0], mod, row(norm1_g[0]), row(norm2_g[0]), w_in, lb, row(hgrn_onorm_g[0]), gdn_conv_w[0],
                    gdn_a_log[0], gdn_dt_bias[0], row(gdn_onorm_g[0]), w_branch_hgrn[0], w_branch_gdn[0], w_out[0],
                    tiles)
    out = _moe(x1, h2, mod, row(norm2_g[0]), row(normf_g), w_router[0], router_bias[0], w_exp_gate[0],
               w_exp_up[0], w_exp_down[0], w_sh_gate[0], w_sh_up[0], w_sh_down[0], tiles)
    return out[None]
```

```python
import functools

import jax
import jax.numpy as jnp
from jax import lax
from jax.experimental import pallas as pl
from jax.experimental.pallas import tpu as pltpu

F32 = jnp.float32
BF16 = jnp.bfloat16
I32 = jnp.int32
U32 = jnp.uint32

NORM_EPS = 1e-6
L2_EPS = 1e-6
HEADS = 8
HEAD_DIM = 128
CONV_WIDTH = 4
CHUNK = 64
N_EXPERTS = 64
N_GROUPS = 8
GROUP_SIZE = N_EXPERTS // N_GROUPS
TOPK_GROUPS = 4
TOP_K = 8
ROUTED_SCALE = 2.5
EXPERT_BLOCK = 512

LANES = 128
SUBLANES = 8
VMEM_LIMIT = 56 * 1024 * 1024

NT = (((1,), (1,)), ((), ()))
TN = (((0,), (0,)), ((), ()))


def _params(sem, **kw):
    return pltpu.CompilerParams(dimension_semantics=sem, vmem_limit_bytes=VMEM_LIMIT, **kw)


def _dot(a, b):
    return jnp.dot(a, b, preferred_element_type=F32)


def _dg(a, b, dims):
    return lax.dot_general(a, b, dims, preferred_element_type=F32)


def _split(x):
    hi = x.astype(BF16)
    lo = (x - hi.astype(F32)).astype(BF16)
    return hi, lo


def _dot_exact_lhs(a_bf16, x, dims=None):
    hi, lo = _split(x)
    if dims is None:
        return _dot(a_bf16, hi) + _dot(a_bf16, lo)
    return _dg(a_bf16, hi, dims) + _dg(a_bf16, lo, dims)


def _sigmoid(x):
    return 1.0 / (1.0 + jnp.exp(-x))


def _silu(x):
    return x * _sigmoid(x)


def _rms(x, eps):
    return x * lax.rsqrt(jnp.mean(x * x, axis=-1, keepdims=True) + eps)


def _iota2(shape, dim):
    return lax.broadcasted_iota(I32, shape, dim)


def _pack_halves(lo, hi):
    lo_bits = lax.shift_right_logical(pltpu.bitcast(lo, U32), U32(16))
    hi_bits = pltpu.bitcast(hi, U32) & U32(0xFFFF0000)
    return lo_bits | hi_bits


def _unpack_halves(word):
    lo = pltpu.bitcast(lax.shift_left(word, U32(16)), F32)
    hi = pltpu.bitcast(word & U32(0xFFFF0000), F32)
    return lo, hi


def _round_bf16(x):
    return x.astype(BF16).astype(F32)


def _ada_kernel(c_ref, w_ref, b_ref, o_ref):
    cond = _silu(c_ref[...])
    o_ref[...] = jnp.dot(cond, w_ref[...], preferred_element_type=F32,
                         precision=lax.Precision.HIGHEST) + b_ref[...]


def _ada(c, w_ada, b_ada):
    d, n = w_ada.shape
    tn = 1024
    c8 = jnp.broadcast_to(c, (SUBLANES, d))
    out = pl.pallas_call(
        _ada_kernel,
        grid=(n // tn,),
        in_specs=[pl.BlockSpec((SUBLANES, d), lambda j: (0, 0)),
                  pl.BlockSpec((d, tn), lambda j: (0, j)),
                  pl.BlockSpec((1, tn), lambda j: (0, j))],
        out_specs=pl.BlockSpec((SUBLANES, tn), lambda j: (0, j)),
        out_shape=jax.ShapeDtypeStruct((SUBLANES, n), F32),
        compiler_params=_params(("arbitrary",)),
    )(c8, w_ada, b_ada.reshape(1, n))
    return out[0:1]


def _wprep_kernel(a_ref, b_ref, o_ref, *, first_shifted, shift):
    j = pl.program_id(0)

    @pl.when(j < first_shifted)
    def _():
        o_ref[...] = a_ref[...].astype(BF16)

    @pl.when(j >= first_shifted)
    def _():
        tn = a_ref.shape[0]
        o_ref[0:tn - shift, :] = a_ref[shift:tn, :].astype(BF16)
        o_ref[tn - shift:tn, :] = b_ref[...].astype(BF16)


def _wprep(w_in_t, cut0, cut1, tn):
    _, n_in, d = w_in_t.shape
    shift = cut1 - cut0
    n_out = n_in - shift
    assert cut0 % tn == 0 and n_out % tn == 0 and tn % shift == 0 and shift % (2 * SUBLANES) == 0
    return pl.pallas_call(
        functools.partial(_wprep_kernel, first_shifted=cut0 // tn, shift=shift),
        grid=(n_out // tn,),
        in_specs=[pl.BlockSpec((None, tn, d), lambda j: (0, j, 0)),
                  pl.BlockSpec((None, shift, d), lambda j: (0, (j + 1) * (tn // shift), 0))],
        out_specs=pl.BlockSpec((tn, d), lambda j: (j, 0)),
        out_shape=jax.ShapeDtypeStruct((n_out, d), BF16),
        compiler_params=_params(("arbitrary",)),
    )(w_in_t, w_in_t)


def _inproj_kernel(x_ref, g_ref, sc_ref, sh_ref, w_ref, wst_ref, proj_ref, smallt_ref, h_scr):
    @pl.when(pl.program_id(1) == 0)
    def _():
        h = _rms(x_ref[...], NORM_EPS) * g_ref[...] * (1.0 + sc_ref[...]) + sh_ref[...]
        hb = h.astype(BF16)
        h_scr[...] = hb
        smallt_ref[...] = _dg(wst_ref[...], hb, NT)

    proj_ref[...] = _dg(h_scr[...], w_ref[...], NT).astype(BF16)


def _inproj(x, g, sc, sh, w_main_t, w_small_t, tm, tn):
    s, d = x.shape
    n = w_main_t.shape[0]
    ns = w_small_t.shape[0]
    row = lambda i, j: (0, 0)
    return pl.pallas_call(
        _inproj_kernel,
        grid=(s // tm, n // tn),
        in_specs=[pl.BlockSpec((tm, d), lambda i, j: (i, 0)),
                  pl.BlockSpec((1, d), row), pl.BlockSpec((1, d), row), pl.BlockSpec((1, d), row),
                  pl.BlockSpec((tn, d), lambda i, j: (j, 0)),
                  pl.BlockSpec((ns, d), row)],
        out_specs=[pl.BlockSpec((tm, tn), lambda i, j: (i, j)),
                   pl.BlockSpec((ns, tm), lambda i, j: (0, i))],
        out_shape=[jax.ShapeDtypeStruct((s, n), BF16), jax.ShapeDtypeStruct((ns, s), F32)],
        scratch_shapes=[pltpu.VMEM((tm, d), BF16)],
        compiler_params=_params(("arbitrary", "arbitrary")),
    )(x, g, sc, sh, w_main_t, w_small_t)


def _hgrn_kernel(q_ref, f_ref, i_ref, g_ref, lb_ref, on_ref, o_ref, st_scr, *, n_chunks):
    @pl.when(pl.program_id(0) == 0)
    def _():
        st_scr[...] = jnp.zeros_like(st_scr)

    c = CHUNK
    hd = HEAD_DIM
    causal = _iota2((c, c), 1) <= _iota2((c, c), 0)
    tri = causal.astype(BF16)
    lb = lb_ref[...]
    on_g = on_ref[...]
    heads = [slice(h * hd, (h + 1) * hd) for h in range(HEADS)]

    def chunk(n, carry):
        rows = pl.ds(pl.multiple_of(n * c, c), c)
        f = lb + (1.0 - lb) * _sigmoid(f_ref[rows, :].astype(F32))
        b = _dot_exact_lhs(tri, jnp.log(f))
        k = 1.0 - f
        q = _silu(q_ref[rows, :].astype(F32)) * (hd ** -0.5)
        v = i_ref[rows, :]
        b_mid = b[c // 2:c // 2 + 1, :]
        b_last = b[c - 1:c, :]
        qa = (q * jnp.exp(b - b_mid)).astype(BF16)
        ka = (k * jnp.exp(b_mid - b)).astype(BF16)
        qi = (q * jnp.exp(b)).astype(BF16)
        ku = (k * jnp.exp(b_last - b)).astype(BF16)
        dec = jnp.exp(b_last)
        gate = on_g * _silu(g_ref[rows, :].astype(F32))
        sts = [st_scr[h] for h in range(HEADS)]
        scores = [jnp.where(causal, _dg(qa[:, sl], ka[:, sl], NT), 0.0).astype(BF16) for sl in heads]
        inter = [_dg(qi[:, sl], st.astype(BF16), NT) for sl, st in zip(heads, sts)]
        kv = [_dg(v[:, sl], ku[:, sl], TN) for sl in heads]
        for h, sl in enumerate(heads):
            st_scr[h] = dec[:, sl] * sts[h] + kv[h]
        outs = [_rms(_dot(sc, v[:, sl]) + it, NORM_EPS) for sc, sl, it in zip(scores, heads, inter)]
        o_ref[rows, :] = (jnp.concatenate(outs, axis=1) * gate).astype(BF16)
        return carry

    lax.fori_loop(0, n_chunks, chunk, 0, unroll=4)


def _hgrn(proj, lb, onorm_g, ts):
    s = proj.shape[0]
    width = HEADS * HEAD_DIM
    col = lambda blk: pl.BlockSpec((ts, width), lambda j, blk=blk: (j, blk))
    const = pl.BlockSpec((1, width), lambda j: (0, 0))
    return pl.pallas_call(
        functools.partial(_hgrn_kernel, n_chunks=ts // CHUNK),
        grid=(s // ts,),
        in_specs=[col(0), col(1), col(2), col(3), const, const],
        out_specs=pl.BlockSpec((ts, width), lambda j: (j, 0)),
        out_shape=jax.ShapeDtypeStruct((s, width), BF16),
        scratch_shapes=[pltpu.VMEM((HEADS, HEAD_DIM, HEAD_DIM), F32)],
        compiler_params=_params(("arbitrary",)),
    )(proj, proj, proj, proj, lb, jnp.tile(onorm_g, (1, HEADS)))


def _gdn_prep_kernel(q_ref, k_ref, v_ref, qp_ref, kp_ref, vp_ref, wq_ref, wk_ref, wv_ref, ab_ref, alog_ref,
                     dtb_ref, tri_ref, eye_ref, u_ref, wqd_ref, ku_ref, attn_ref, dl_ref, cat_scr, rows_scr, cols_scr,
                     *, n_chunks, ts):
    h = pl.program_id(1)
    first = pl.program_id(0) == 0
    c = CHUNK
    hd = HEAD_DIM

    def conv_silu(cur_ref, prev_ref, w_ref):
        cat_scr[0:8, :] = jnp.where(first, 0.0, prev_ref[...].astype(F32))
        cat_scr[8:8 + ts, :] = cur_ref[...].astype(F32)
        acc = None
        for j in range(CONV_WIDTH):
            off = 8 - (CONV_WIDTH - 1) + j
            term = cat_scr[off:off + ts, :] * w_ref[j:j + 1, :]
            acc = term if acc is None else acc + term
        return _silu(acc)

    def l2n(x):
        return x * lax.rsqrt(jnp.sum(x * x, axis=-1, keepdims=True) + L2_EPS)

    q_all = l2n(conv_silu(q_ref, qp_ref, wq_ref)) * (hd ** -0.5)
    k_all = l2n(conv_silu(k_ref, kp_ref, wk_ref))
    v_all = conv_silu(v_ref, vp_ref, wv_ref)

    @pl.when(h == 0)
    def _():
        z = ab_ref[0:HEADS, :] + dtb_ref[...]
        softplus = jnp.maximum(z, 0.0) + jnp.log(1.0 + jnp.exp(-jnp.abs(z)))
        ld_rows = -jnp.exp(alog_ref[...]) * softplus
        hi, lo = _split(ld_rows)
        tri_blocks = tri_ref[...]
        w = tri_blocks.shape[0]
        spans = [slice(t0, t0 + w) for t0 in range(0, ts, w)]
        g_rows = jnp.concatenate([_dg(hi[:, sp], tri_blocks, NT) + _dg(lo[:, sp], tri_blocks, NT) for sp in spans],
                                 axis=1)
        beta_rows = _sigmoid(ab_ref[HEADS:2 * HEADS, :])
        rows_scr[...] = g_rows
        rows = jnp.concatenate([g_rows, beta_rows, jnp.zeros((LANES - 2 * HEADS, ts), F32)], axis=0)
        r_hi, r_lo = _split(rows)
        r_lo2 = (rows - r_hi.astype(F32) - r_lo.astype(F32)).astype(BF16)
        eye_w = eye_ref[...]
        for sp in spans:
            cols_scr[sp, :] = _dg(eye_w, r_hi[:, sp], NT) + _dg(eye_w, r_lo[:, sp], NT) + _dg(eye_w, r_lo2[:, sp], NT)

    lane = _iota2((ts, LANES), 1)
    cols = cols_scr[...]
    gc_all = jnp.sum(jnp.where(lane == h, cols, 0.0), axis=1, keepdims=True)
    bc_all = jnp.sum(jnp.where(lane == h + HEADS, cols, 0.0), axis=1, keepdims=True)
    g_row = rows_scr[pl.ds(h, 1), :]
    egc_all = jnp.exp(gc_all)

    r = _iota2((c, c), 0)
    cidx = _iota2((c, c), 1)
    causal = cidx <= r
    strict = cidx < r
    eye_f = (r == cidx).astype(F32)
    chunks = [slice(n * c, (n + 1) * c) for n in range(n_chunks)]

    q16 = q_all.astype(BF16)
    k16 = k_all.astype(BF16)
    kq = [_dg(jnp.concatenate([k16[sl], q16[sl]], axis=0), k16[sl], NT) for sl in chunks]
    dm = []
    for sl in chunks:
        diff = gc_all[sl] - g_row[:, sl]
        dm.append(jnp.where(causal, jnp.exp(jnp.where(causal, diff, 0.0)), 0.0))
    bm = [-jnp.where(strict, bc_all[sl] * x[0:c] * d, 0.0) for sl, x, d in zip(chunks, kq, dm)]
    p = [eye_f + b for b in bm]
    bm = [_dot(b.astype(BF16), b.astype(BF16)) for b in bm]
    for _ in range(c.bit_length() - 3):
        res = [_dot(b.astype(BF16), jnp.concatenate([b, pp], axis=1).astype(BF16)) for b, pp in zip(bm, p)]
        p = [pp + x[:, c:2 * c] for pp, x in zip(p, res)]
        bm = [x[:, 0:c] for x in res]
    p = [pp + _dot(b.astype(BF16), pp.astype(BF16)) for b, pp in zip(bm, p)]
    rhs = jnp.concatenate([v_all * bc_all, k_all * (bc_all * egc_all)], axis=1).astype(BF16)
    sol = [_dot(pp.astype(BF16), rhs[sl]) for pp, sl in zip(p, chunks)]
    qd_all = (q_all * egc_all).astype(BF16)
    for n, sl in enumerate(chunks):
        g_last = gc_all[(n + 1) * c - 1:(n + 1) * c, :]
        u_ref[sl, :] = sol[n][:, 0:hd].astype(BF16)
        wqd_ref[2 * n * c:(2 * n + 1) * c, :] = sol[n][:, hd:2 * hd].astype(BF16)
        wqd_ref[(2 * n + 1) * c:(2 * n + 2) * c, :] = qd_all[sl]
        ku_ref[sl, :] = (k_all[sl] * jnp.exp(g_last - gc_all[sl])).astype(BF16)
        attn_ref[sl, :] = (kq[n][c:2 * c] * dm[n]).astype(BF16)
        dl_ref[n:n + 1, :] = jnp.broadcast_to(jnp.exp(g_last), (1, hd))


def _gdn_prep(proj, conv_w, ab_t, a_log, dt_bias, ts):
    s = proj.shape[0]
    hd = HEAD_DIM
    c = CHUNK
    q0 = 4 * HEADS
    cur = lambda off: pl.BlockSpec((ts, hd), lambda j, h, off=off: (j, off + h))
    prev = lambda off: pl.BlockSpec((8, hd), lambda j, h, off=off: (jnp.maximum(j * (ts // 8) - 1, 0), off + h))
    cw = lambda off: pl.BlockSpec((CONV_WIDTH, hd), lambda j, h, off=off: (0, off + h))
    per_head_scalar = pl.BlockSpec((HEADS, 1), lambda j, h: (0, 0))
    w = min(ts, 2 * LANES)
    const = pl.BlockSpec((w, w), lambda j, h: (0, 0))
    pos = jnp.arange(w)
    tri_blocks = ((pos[:, None] // c == pos[None, :] // c) & (pos[None, :] <= pos[:, None])).astype(BF16)
    eye = (pos[:, None] == pos[None, :]).astype(BF16)
    per_head = lambda rows, cols: pl.BlockSpec((None, rows, cols), lambda j, h: (h, j, 0))
    return pl.pallas_call(
        functools.partial(_gdn_prep_kernel, n_chunks=ts // c, ts=ts),
        grid=(s // ts, HEADS),
        in_specs=[cur(q0), cur(q0 + HEADS), cur(q0 + 2 * HEADS),
                  prev(q0), prev(q0 + HEADS), prev(q0 + 2 * HEADS),
                  cw(0), cw(HEADS), cw(2 * HEADS),
                  pl.BlockSpec((2 * HEADS, ts), lambda j, h: (0, j)),
                  per_head_scalar, per_head_scalar, const, const],
        out_specs=[pl.BlockSpec((ts, hd), lambda j, h: (j, h)),
                   pl.BlockSpec((2 * ts, hd), lambda j, h: (j, h)),
                   pl.BlockSpec((ts, hd), lambda j, h: (j, h)),
                   per_head(ts, c),
                   per_head(ts // c, hd)],
        out_shape=[jax.ShapeDtypeStruct((s, HEADS * hd), BF16),
                   jax.ShapeDtypeStruct((2 * s, HEADS * hd), BF16),
                   jax.ShapeDtypeStruct((s, HEADS * hd), BF16),
                   jax.ShapeDtypeStruct((HEADS, s, c), BF16),
                   jax.ShapeDtypeStruct((HEADS, s // c, hd), F32)],
        scratch_shapes=[pltpu.VMEM((ts + 8, hd), F32), pltpu.VMEM((HEADS, ts), F32), pltpu.VMEM((ts, LANES), F32)],
        compiler_params=_params(("arbitrary", "arbitrary")),
    )(proj, proj, proj, proj, proj, proj, conv_w, conv_w, conv_w,
      ab_t, a_log.reshape(HEADS, 1), dt_bias.reshape(HEADS, 1), tri_blocks, eye)


def _gdn_scan_kernel(u_ref, wqd_ref, ku_ref, attn_ref, dl_ref, g_ref, on_ref, o_ref, st_scr, *, n_chunks):
    @pl.when(pl.program_id(0) == 0)
    def _():
        st_scr[...] = jnp.zeros_like(st_scr)

    c = CHUNK
    hd = HEAD_DIM
    on_g = on_ref[...]
    heads = [slice(h * hd, (h + 1) * hd) for h in range(HEADS)]

    def chunk(n, carry):
        rows = pl.ds(pl.multiple_of(n * c, c), c)
        rows2 = pl.ds(pl.multiple_of(2 * n * c, 2 * c), 2 * c)
        sts = [st_scr[h] for h in range(HEADS)]
        wq = [_dot(wqd_ref[rows2, sl], st.astype(BF16)) for sl, st in zip(heads, sts)]
        vn = [(u_ref[rows, sl].astype(F32) - x[0:c]).astype(BF16) for sl, x in zip(heads, wq)]
        upd = [_dg(ku_ref[rows, sl], v, TN) for sl, v in zip(heads, vn)]
        for h in range(HEADS):
            st_scr[h] = dl_ref[h, pl.ds(n, 1), :] * sts[h] + upd[h]
        outs = [_rms(x[c:2 * c] + _dot(attn_ref[h, rows, :], v), NORM_EPS)
                for h, (x, v) in enumerate(zip(wq, vn))]
        gate = jnp.tile(on_g, (1, HEADS)) * _silu(g_ref[rows, :].astype(F32))
        o_ref[rows, :] = (jnp.concatenate(outs, axis=1) * gate).astype(BF16)
        return carry

    lax.fori_loop(0, n_chunks, chunk, 0, unroll=4)


def _gdn_scan(u, wqd, ku, attn, dl, proj, onorm_g, ts):
    s, width = u.shape
    c = CHUNK
    gate_blk = (4 * HEADS + 3 * HEADS) * HEAD_DIM // width
    return pl.pallas_call(
        functools.partial(_gdn_scan_kernel, n_chunks=ts // c),
        grid=(s // ts,),
        in_specs=[pl.BlockSpec((ts, width), lambda j: (j, 0)),
                  pl.BlockSpec((2 * ts, width), lambda j: (j, 0)),
                  pl.BlockSpec((ts, width), lambda j: (j, 0)),
                  pl.BlockSpec((HEADS, ts, c), lambda j: (0, j, 0)),
                  pl.BlockSpec((HEADS, ts // c, HEAD_DIM), lambda j: (0, j, 0)),
                  pl.BlockSpec((ts, width), lambda j: (j, gate_blk)),
                  pl.BlockSpec((1, HEAD_DIM), lambda j: (0, 0))],
        out_specs=pl.BlockSpec((ts, width), lambda j: (j, 0)),
        out_shape=jax.ShapeDtypeStruct((s, width), BF16),
        scratch_shapes=[pltpu.VMEM((HEADS, HEAD_DIM, HEAD_DIM), F32)],
        compiler_params=_params(("arbitrary",)),
    )(u, wqd, ku, attn, dl, proj, onorm_g)


def _merge_kernel(oa_ref, ob_ref, mga_ref, mgb_ref, x_ref, wa_ref, wb_ref, wo_ref, gt_ref, g2_ref, sc_ref,
                  sh_ref, x1_ref, h2_ref):
    ya = _dot(oa_ref[...], wa_ref[...])
    yb = _dot(ob_ref[...], wb_ref[...])
    merged = _sigmoid(mga_ref[...].astype(F32)) * ya + _sigmoid(mgb_ref[...].astype(F32)) * yb
    x1 = x_ref[...] + gt_ref[...] * _dot(merged.astype(BF16), wo_ref[...])
    x1_ref[...] = x1
    h2 = _rms(x1, NORM_EPS) * g2_ref[...] * (1.0 + sc_ref[...]) + sh_ref[...]
    h2_ref[...] = h2.astype(BF16)


def _merge(o_a, o_b, proj, x, w_a, w_b, w_o, gt1, g2, sc2, sh2, tm):
    s, d = x.shape
    dv = o_a.shape[1]
    mg0 = (8 * HEADS * HEAD_DIM) // d
    const = lambda shape: pl.BlockSpec(shape, lambda i: (0, 0), pipeline_mode=pl.Buffered(1))
    return pl.pallas_call(
        _merge_kernel,
        grid=(s // tm,),
        in_specs=[pl.BlockSpec((tm, dv), lambda i: (i, 0)),
                  pl.BlockSpec((tm, dv), lambda i: (i, 0)),
                  pl.BlockSpec((tm, d), lambda i: (i, mg0)),
                  pl.BlockSpec((tm, d), lambda i: (i, mg0 + 1)),
                  pl.BlockSpec((tm, d), lambda i: (i, 0)),
                  const((dv, d)), const((dv, d)), const((d, d)),
                  const((1, d)), const((1, d)), const((1, d)), const((1, d))],
        out_specs=[pl.BlockSpec((tm, d), lambda i: (i, 0)), pl.BlockSpec((tm, d), lambda i: (i, 0))],
        out_shape=[jax.ShapeDtypeStruct((s, d), F32), jax.ShapeDtypeStruct((s, d), BF16)],
        compiler_params=_params(("arbitrary",)),
    )(o_a, o_b, proj, proj, x, w_a, w_b, w_o, gt1, g2, sc2, sh2)


def _first_max(vals, iota, size, axis):
    m = jnp.max(vals, axis=axis, keepdims=True)
    idx = jnp.min(jnp.where(vals == m, iota, size), axis=axis, keepdims=True)
    return m, idx


def _router_kernel(x1_ref, g2_ref, sc_ref, sh_ref, wrt_ref, bias_ref, upper_ref, pos_ref, wts_ref, before_ref,
                   ntile_ref, cnt_scr, *, tm):
    @pl.when(pl.program_id(0) == 0)
    def _():
        cnt_scr[...] = jnp.zeros_like(cnt_scr)

    e = N_EXPERTS
    h2 = _rms(x1_ref[...], NORM_EPS) * g2_ref[...] * (1.0 + sc_ref[...]) + sh_ref[...]
    logits = lax.dot_general(wrt_ref[...], h2, NT, preferred_element_type=F32,
                             precision=lax.Precision.HIGHEST)
    scores = _sigmoid(logits)
    biased = scores + bias_ref[...]
    neg = -jnp.inf

    g3 = biased.reshape(N_GROUPS, GROUP_SIZE, tm)
    i3 = lax.broadcasted_iota(I32, g3.shape, 1)
    m1, a1 = _first_max(g3, i3, GROUP_SIZE, 1)
    m2 = jnp.max(jnp.where(i3 == a1, neg, g3), axis=1, keepdims=True)
    gs = (m1 + m2).reshape(N_GROUPS, tm)
    ig = _iota2(gs.shape, 0)
    gmask = jnp.zeros(gs.shape, jnp.bool_)
    for _ in range(TOPK_GROUPS):
        _, a = _first_max(gs, ig, N_GROUPS, 0)
        pick = ig == a
        gmask = jnp.logical_or(gmask, pick)
        gs = jnp.where(pick, neg, gs)
    emask = jnp.broadcast_to(gmask.reshape(N_GROUPS, 1, tm), (N_GROUPS, GROUP_SIZE, tm)).reshape(e, tm)

    cand = jnp.where(emask, biased, neg)
    ie = _iota2((e, tm), 0)
    sel_all = jnp.zeros((e, tm), jnp.bool_)
    w_rows, picks = [], []
    for _ in range(TOP_K):
        _, a = _first_max(cand, ie, e, 0)
        pick = ie == a
        picks.append(pick)
        w_rows.append(jnp.sum(jnp.where(pick, scores, 0.0), axis=0, keepdims=True))
        sel_all = jnp.logical_or(sel_all, pick)
        cand = jnp.where(pick, neg, cand)
    w_sum = w_rows[0]
    for wr in w_rows[1:]:
        w_sum = w_sum + wr
    wts = jnp.concatenate(w_rows, axis=0) / w_sum * ROUTED_SCALE

    sel = sel_all.astype(BF16)
    in_expert = _dot(sel, upper_ref[...])
    n_tile = jnp.sum(sel_all.astype(F32), axis=1, keepdims=True)
    lower = (_iota2((e, e), 1) < _iota2((e, e), 0)).astype(BF16)
    expert_off = _dot_exact_lhs(lower, jnp.broadcast_to(n_tile, (e, LANES)))[:, 0:1]
    place = in_expert + expert_off
    pos = jnp.concatenate([jnp.sum(jnp.where(pk, place, 0.0), axis=0, keepdims=True) for pk in picks], axis=0)
    pos_ref[...] = pos.astype(I32)
    before_ref[...] = jnp.broadcast_to(cnt_scr[...], before_ref.shape).astype(I32)
    ntile_ref[...] = jnp.broadcast_to(n_tile, ntile_ref.shape).astype(I32)
    cnt_scr[...] = cnt_scr[...] + n_tile
    wts_ref[...] = wts


def _router(x1, g2, sc2, sh2, w_router_t, bias_col, tm):
    s, d = x1.shape
    e = N_EXPERTS
    nt = s // tm
    upper = (jnp.arange(tm)[:, None] < jnp.arange(tm)[None, :]).astype(BF16)
    const = lambda shape: pl.BlockSpec(shape, lambda i: (0, 0))
    per_tile = pl.BlockSpec((None, e, LANES), lambda i: (i, 0, 0))
    return pl.pallas_call(
        functools.partial(_router_kernel, tm=tm),
        grid=(nt,),
        in_specs=[pl.BlockSpec((tm, d), lambda i: (i, 0)),
                  const((1, d)), const((1, d)), const((1, d)),
                  const((e, d)), const((e, 1)), const((tm, tm))],
        out_specs=[pl.BlockSpec((TOP_K, tm), lambda i: (0, i)),
                   pl.BlockSpec((TOP_K, tm), lambda i: (0, i)),
                   per_tile, per_tile],
        out_shape=[jax.ShapeDtypeStruct((TOP_K, s), I32), jax.ShapeDtypeStruct((TOP_K, s), F32),
                   jax.ShapeDtypeStruct((nt, e, LANES), I32), jax.ShapeDtypeStruct((nt, e, LANES), I32)],
        scratch_shapes=[pltpu.VMEM((e, 1), F32)],
        compiler_params=_params(("arbitrary",)),
    )(x1, g2, sc2, sh2, w_router_t, bias_col, upper)


LONG_RUN = 64


def _run_sizes(limit):
    return [1 << b for b in range(limit.bit_length() - 1, -1, -1)]


def _for_each_run(tile, run_refs, tm, make_copy, fn, unroll=False):
    run_len_ref, run_off_ref, run_dst_ref = run_refs

    def per_expert(ex, carry):
        n = run_len_ref[tile * N_EXPERTS + ex]
        off = run_off_ref[tile * N_EXPERTS + ex]
        dst = run_dst_ref[tile * N_EXPERTS + ex]
        def pieces(sizes):
            for size in sizes:
                done = n & (-2 * size)

                @pl.when((n & size) != 0)
                def _(done=done, size=size):
                    fn(make_copy(off + done, dst + done, size))

        sizes = _run_sizes(tm)
        pieces([size for size in sizes if size < LONG_RUN])

        @pl.when(n >= LONG_RUN)
        def _():
            pieces([size for size in sizes if size >= LONG_RUN])

        return carry

    lax.fori_loop(0, N_EXPERTS, per_expert, 0, unroll=unroll)


def _slot_rows(slot, n_slots):
    return pl.ds(pl.multiple_of(slot * SUBLANES, SUBLANES), n_slots * SUBLANES)


def _dispatch_kernel(run_len_ref, run_off_ref, run_dst_ref, pad_lo_ref, pad_hi_ref, pos_ref, h_ref, xs_ref, stage, zero_scr,
                     sem, pad_sem, *, tm, rows_per_pass):
    step = pl.program_id(0)
    na = TOP_K * tm
    d = h_ref.shape[1]
    half = d // 2
    n_words = half // LANES

    def pad_copy(slot, n_slots):
        return pltpu.make_async_copy(zero_scr.at[pl.ds(0, n_slots * SUBLANES), :],
                                     xs_ref.at[_slot_rows(slot, n_slots), :], pad_sem)

    def for_each_pad(fn):
        def per_expert(ex, carry):
            slot = pad_lo_ref[ex]
            n = pad_hi_ref[ex] - slot
            for size in _run_sizes(EXPERT_BLOCK - 1):
                take = (n & size) != 0

                @pl.when(take)
                def _(slot=slot, size=size):
                    fn(pad_copy(slot, size))

                slot = slot + jnp.where(take, size, 0)
            return carry
        lax.fori_loop(0, N_EXPERTS, per_expert, 0)

    @pl.when(step == 0)
    def _():
        zero_scr[...] = jnp.zeros_like(zero_scr)
        for_each_pad(lambda cp: cp.start())

    buf = step % 2
    pos = pos_ref[...]
    h = h_ref[...]
    for a0 in range(0, na, rows_per_pass):
        slot_id = a0 + _iota2((rows_per_pass, tm), 0)
        hit = pos[0:1, :] == slot_id
        for k in range(1, TOP_K):
            hit = jnp.logical_or(hit, pos[k:k + 1, :] == slot_id)
        rows = _dot(hit.astype(BF16), h)
        for i in range(n_words):
            word = _pack_halves(rows[:, i * LANES:(i + 1) * LANES], rows[:, half + i * LANES:half + (i + 1) * LANES])
            stage[buf, pl.ds(a0 * SUBLANES + i, rows_per_pass, stride=SUBLANES), :] = word

    def run_copy(tile_slot, sorted_slot, n_slots):
        return pltpu.make_async_copy(stage.at[buf, _slot_rows(tile_slot, n_slots), :],
                                     xs_ref.at[_slot_rows(sorted_slot, n_slots), :], sem.at[buf])

    def wait_tile(which):
        pltpu.make_async_copy(stage.at[which], xs_ref.at[pl.ds(0, na * SUBLANES), :], sem.at[which]).wait()

    _for_each_run(step, (run_len_ref, run_off_ref, run_dst_ref), tm, run_copy, lambda cp: cp.start())

    @pl.when(step > 0)
    def _():
        wait_tile(1 - buf)

    @pl.when(step == pl.num_programs(0) - 1)
    def _():
        wait_tile(buf)

    @pl.when(step == 0)
    def _():
        for_each_pad(lambda cp: cp.wait())


def _dispatch(runs, pad_lo, pad_hi, pos_t, h2, n_slots, tm):
    s, d = h2.shape
    assert (d // 2) % LANES == 0 and (d // 2) // LANES == SUBLANES, "one token row must pack into one (8, 128) tile"
    na = TOP_K * tm
    return pl.pallas_call(
        functools.partial(_dispatch_kernel, tm=tm, rows_per_pass=min(512, na)),
        grid_spec=pltpu.PrefetchScalarGridSpec(
            num_scalar_prefetch=5,
            grid=(s // tm,),
            in_specs=[pl.BlockSpec((TOP_K, tm), lambda i, *_: (0, i)),
                      pl.BlockSpec((tm, d), lambda i, *_: (i, 0))],
            out_specs=pl.BlockSpec(memory_space=pl.ANY),
            scratch_shapes=[pltpu.VMEM((2, na * SUBLANES, LANES), U32),
                            pltpu.VMEM((EXPERT_BLOCK // 2 * SUBLANES, LANES), U32),
                            pltpu.SemaphoreType.DMA((2,)), pltpu.SemaphoreType.DMA(())]),
        out_shape=jax.ShapeDtypeStruct((n_slots * SUBLANES, LANES), U32),
        compiler_params=_params(("arbitrary",), has_side_effects=True, disable_bounds_checks=True),
    )(*runs, pad_lo, pad_hi, pos_t, h2)


def _expert_kernel(be_ref, nu_ref, next_ref, par_ref, x_ref, wg_hbm, wu_hbm, wd_hbm, y_ref, wg_f32, wu_f32, wd_f32,
                   wg_scr, wu_scr, wd_scr, sem):
    b = pl.program_id(0)
    bm = EXPERT_BLOCK
    active = b < nu_ref[0]
    new_expert = jnp.logical_or(b == 0, be_ref[b] != be_ref[jnp.maximum(b - 1, 0)])

    def weight_copies(ex, which):
        return [pltpu.make_async_copy(src.at[ex], dst.at[which], sem.at[which])
                for src, dst in ((wg_hbm, wg_f32), (wu_hbm, wu_f32), (wd_hbm, wd_f32))]

    @pl.when(jnp.logical_and(active, new_expert))
    def _():
        which = par_ref[b]

        @pl.when(b == 0)
        def _():
            for cp in weight_copies(be_ref[b], which):
                cp.start()

        for cp in weight_copies(be_ref[b], which):
            cp.wait()
        wg_scr[...] = wg_f32[which].astype(BF16)
        wu_scr[...] = wu_f32[which].astype(BF16)
        wd_scr[...] = wd_f32[which].astype(BF16)
        nb = next_ref[b]

        @pl.when(nb < nu_ref[0])
        def _():
            for cp in weight_copies(be_ref[nb], 1 - which):
                cp.start()

    @pl.when(active)
    def _():
        los, his = [], []
        for i in range(SUBLANES):
            lo, hi = _unpack_halves(x_ref[pl.ds(i, bm, stride=SUBLANES), :])
            los.append(lo.astype(BF16))
            his.append(hi.astype(BF16))
        xb = jnp.concatenate(los + his, axis=1)
        hid = _silu(_dot(xb, wg_scr[...])) * _dot(xb, wu_scr[...])
        y = _dot(hid.astype(BF16), wd_scr[...])
        half = y.shape[1] // 2
        for i in range(SUBLANES):
            word = _pack_halves(_round_bf16(y[:, i * LANES:(i + 1) * LANES]),
                                _round_bf16(y[:, half + i * LANES:half + (i + 1) * LANES]))
            y_ref[pl.ds(i, bm, stride=SUBLANES), :] = word


def _experts(block_e, n_used, next_block, parity, xs, w_gate, w_up, w_down):
    d, ff = w_gate.shape[1], w_gate.shape[2]
    bm = EXPERT_BLOCK
    n_blocks = xs.shape[0] // (bm * SUBLANES)
    blk = lambda b, be, nu, *_: (jnp.minimum(b, nu[0] - 1), 0)
    hbm = pl.BlockSpec(memory_space=pl.ANY)
    return pl.pallas_call(
        _expert_kernel,
        grid_spec=pltpu.PrefetchScalarGridSpec(
            num_scalar_prefetch=4,
            grid=(n_blocks,),
            in_specs=[pl.BlockSpec((bm * SUBLANES, LANES), blk), hbm, hbm, hbm],
            out_specs=pl.BlockSpec((bm * SUBLANES, LANES), blk),
            scratch_shapes=[pltpu.VMEM((2, d, ff), F32), pltpu.VMEM((2, d, ff), F32), pltpu.VMEM((2, ff, d), F32),
                            pltpu.VMEM((d, ff), BF16), pltpu.VMEM((d, ff), BF16), pltpu.VMEM((ff, d), BF16),
                            pltpu.SemaphoreType.DMA((2,))]),
        out_shape=jax.ShapeDtypeStruct(xs.shape, U32),
        compiler_params=_params(("arbitrary",)),
    )(block_e, n_used, next_block, parity, xs, w_gate, w_up, w_down)


def _combine_kernel(run_len_ref, run_off_ref, run_src_ref, ys_ref, h_ref, x1_ref, pos_ref, wts_ref, wg_ref, wu_ref, wd_ref, gt_ref, gf_ref,
                    o_ref, stage, sem, *, tm, rows_per_pass):
    step = pl.program_id(0)
    na = TOP_K * tm
    buf = step % 2

    last = pl.num_programs(0) - 1

    def fetch_tile(tile, which, unroll=False):
        def run_copy(tile_slot, sorted_slot, n_slots):
            return pltpu.make_async_copy(ys_ref.at[_slot_rows(sorted_slot, n_slots), :],
                                         stage.at[which, _slot_rows(tile_slot, n_slots), :], sem.at[which])
        _for_each_run(tile, (run_len_ref, run_off_ref, run_src_ref), tm, run_copy, lambda cp: cp.start(), unroll)

    def wait_tile(which):
        pltpu.make_async_copy(ys_ref.at[pl.ds(0, na * SUBLANES), :], stage.at[which], sem.at[which]).wait()

    @pl.when(step == 0)
    def _():
        fetch_tile(step, buf)

    fetch_tile(jnp.minimum(step + 1, last), 1 - buf, unroll=True)

    hb = h_ref[...]
    hid = _silu(_dot(hb, wg_ref[...])) * _dot(hb, wu_ref[...])
    acc = _dot(hid.astype(BF16), wd_ref[...])

    wait_tile(buf)

    @pl.when(step == last)
    def _():
        wait_tile(1 - buf)

    pos = pos_ref[...]
    wts = wts_ref[...]
    for a0 in range(0, na, rows_per_pass):
        los, his = [], []
        for i in range(SUBLANES):
            lo, hi = _unpack_halves(stage[buf, pl.ds(a0 * SUBLANES + i, rows_per_pass, stride=SUBLANES), :])
            los.append(lo.astype(BF16))
            his.append(hi.astype(BF16))
        y_rows = jnp.concatenate(los + his, axis=1)
        slot_id = a0 + _iota2((rows_per_pass, tm), 0)
        wmat = jnp.zeros((rows_per_pass, tm), F32)
        for k in range(TOP_K):
            wmat = wmat + jnp.where(pos[k:k + 1, :] == slot_id, wts[k:k + 1, :], 0.0)
        acc = acc + _dg(wmat.astype(BF16), y_rows, TN)
    x2 = x1_ref[...] + gt_ref[...] * acc
    o_ref[...] = _rms(x2, NORM_EPS) * gf_ref[...]


def _combine(runs, ys, h2, x1, pos_t, wts_t, w_gate, w_up, w_down, gt2, gf, tm):
    s, d = x1.shape
    ff = w_gate.shape[1]
    na = TOP_K * tm
    const = lambda shape: pl.BlockSpec(shape, lambda i, *_: (0, 0), pipeline_mode=pl.Buffered(1))
    tile = lambda cols: pl.BlockSpec((tm, cols), lambda i, *_: (i, 0))
    per_k = pl.BlockSpec((TOP_K, tm), lambda i, *_: (0, i))
    return pl.pallas_call(
        functools.partial(_combine_kernel, tm=tm, rows_per_pass=min(256, na)),
        grid_spec=pltpu.PrefetchScalarGridSpec(
            num_scalar_prefetch=3,
            grid=(s // tm,),
            in_specs=[pl.BlockSpec(memory_space=pl.ANY),
                      tile(d), tile(d), per_k, per_k,
                      const((d, ff)), const((d, ff)), const((ff, d)), const((1, d)), const((1, d))],
            out_specs=tile(d),
            scratch_shapes=[pltpu.VMEM((2, na * SUBLANES, LANES), U32), pltpu.SemaphoreType.DMA((2,))]),
        out_shape=jax.ShapeDtypeStruct((s, d), F32),
        compiler_params=_params(("arbitrary",), disable_bounds_checks=True),
    )(*runs, ys, h2, x1, pos_t, wts_t, w_gate, w_up, w_down, gt2, gf)


def _mixer(x2d, mod, norm1_g, norm2_g, w_in, lb, hgrn_onorm_g, gdn_conv_w, gdn_a_log, gdn_dt_bias, gdn_onorm_g,
           w_branch_hgrn, w_branch_gdn, w_out, tiles):
    d = x2d.shape[1]
    sh1, sc1, gt1, sh2, sc2, _ = [mod[:, i * d:(i + 1) * d] for i in range(6)]
    key = HEADS * HEAD_DIM
    small0 = 4 * key + 3 * key
    small1 = small0 + 2 * HEADS
    w_in_t = jnp.swapaxes(w_in, 1, 2)
    w_main_t = _wprep(w_in_t, small0, small1, tiles["wprep_tn"])
    w_small_t = w_in_t[0, small0:small1, :].astype(BF16)
    proj, ab_t = _inproj(x2d, norm1_g, sc1, sh1, w_main_t, w_small_t, tiles["in_tm"], tiles["in_tn"])
    o_a = _hgrn(proj, lb, hgrn_onorm_g, tiles["mix_ts"])
    u, wqd, ku, attn, dl = _gdn_prep(proj, gdn_conv_w, ab_t, gdn_a_log, gdn_dt_bias, tiles["prep_ts"])
    o_b = _gdn_scan(u, wqd, ku, attn, dl, proj, gdn_onorm_g, tiles["mix_ts"])
    return _merge(o_a, o_b, proj, x2d, w_branch_hgrn.astype(BF16), w_branch_gdn.astype(BF16),
                  w_out.astype(BF16), gt1, norm2_g, sc2, sh2, tiles["merge_tm"])


def _moe(x1, h2, mod, norm2_g, normf_g, w_router, router_bias, w_exp_gate, w_exp_up, w_exp_down, w_sh_gate,
         w_sh_up, w_sh_down, tiles):
    s, d = x1.shape
    tm = tiles["moe_tm"]
    sh2, sc2, gt2 = [mod[:, i * d:(i + 1) * d] for i in (3, 4, 5)]
    pos_t, wts_t, before, ntile = _router(x1, norm2_g, sc2, sh2, w_router.T, router_bias.reshape(-1, 1), tm)
    bm = EXPERT_BLOCK
    n_blocks = -(-(s * TOP_K + N_EXPERTS * (bm - 1)) // bm)
    before = before[:, :, 0]
    ntile = ntile[:, :, 0]
    counts = before[-1] + ntile[-1]
    padded = (counts + bm - 1) // bm * bm
    earlier = jnp.arange(N_EXPERTS)[None, :] < jnp.arange(N_EXPERTS)[:, None]
    pstart = jnp.sum(jnp.where(earlier, padded[None, :], 0), axis=1).astype(I32)
    pend = pstart + padded
    block_start = jnp.arange(n_blocks, dtype=I32) * bm
    block_e = jnp.minimum(jnp.sum(pend[None, :] <= block_start[:, None], axis=1), N_EXPERTS - 1).astype(I32)
    n_used = pend[-1:] // bm
    run_off = jnp.sum(jnp.where(earlier[None], ntile[:, None, :], 0), axis=2)
    runs = (ntile.reshape(-1), run_off.reshape(-1), (before + pstart[None, :]).reshape(-1))
    xs = _dispatch(runs, pstart + counts, pend, pos_t, h2, n_blocks * bm, tm)
    own = block_e[:, None] == jnp.arange(N_EXPERTS)[None, :]
    next_block = jnp.sum(jnp.where(own, pend[None, :], 0), axis=1) // bm
    switches = jnp.concatenate([jnp.zeros((1,), I32), (block_e[1:] != block_e[:-1]).astype(I32)])
    upto = jnp.arange(n_blocks)[None, :] <= jnp.arange(n_blocks)[:, None]
    parity = jnp.sum(jnp.where(upto, switches[None, :], 0), axis=1).astype(I32) % 2
    ys = _experts(block_e, n_used, next_block, parity, xs, w_exp_gate, w_exp_up, w_exp_down)
    return _combine(runs, ys, h2, x1, pos_t, wts_t, w_sh_gate.astype(BF16), w_sh_up.astype(BF16),
                    w_sh_down.astype(BF16), gt2, normf_g, tm)


def _tiles(s):
    pick = lambda want: min(want, s)
    return dict(wprep_tn=512, in_tm=pick(1024), in_tn=1536, mix_ts=pick(512), prep_ts=pick(2048), merge_tm=pick(512),
                moe_tm=pick(256))


def kernel(x, c, w_ada, b_ada, norm1_g, norm2_g, w_in, hgrn_lb_table, hgrn_onorm_g, gdn_conv_w, gdn_a_log, gdn_dt_bias, gdn_onorm_g, w_branch_hgrn, w_branch_gdn, w_out, w_router, router_bias, w_exp_gate, w_exp_up, w_exp_down, w_sh_gate, w_sh_up, w_sh_down, normf_g):
    b, s, d = x.shape
    assert b == 1 and w_ada.shape[0] == 1, "one sequence, one layer"
    tiles = _tiles(s)
    lb = jnp.sum(jax.nn.softmax(hgrn_lb_table.astype(F32), axis=0)[0:1], axis=0, keepdims=True)
    mod = _ada(c, w_ada[0], b_ada[0])
    row = lambda v: v.reshape(1, -1)
    x1, h2 = _mixer(x[0], mod, row(norm1_g[0]), row(norm2_g[0]), w_in, lb, row(hgrn_onorm_g[0]), gdn_conv_w[0],
                    gdn_a_log[0], gdn_dt_bias[0], row(gdn_onorm_g[0]), w_branch_hgrn[0], w_branch_gdn[0], w_out[0],
                    tiles)
    out = _moe(x1, h2, mod, row(norm2_g[0]), row(normf_g), w_router[0], router_bias[0], w_exp_gate[0],
               w_exp_up[0], w_exp_down[0], w_sh_gate[0], w_sh_up[0], w_sh_down[0], tiles)
    return out[None]
```

```python
import functools

import jax
import jax.numpy as jnp
from jax import lax
from jax.experimental import pallas as pl
from jax.experimental.pallas import tpu as pltpu

F32 = jnp.float32
BF16 = jnp.bfloat16
I32 = jnp.int32
U32 = jnp.uint32

NORM_EPS = 1e-6
L2_EPS = 1e-6
HEADS = 8
HEAD_DIM = 128
CONV_WIDTH = 4
CHUNK = 64
N_EXPERTS = 64
N_GROUPS = 8
GROUP_SIZE = N_EXPERTS // N_GROUPS
TOPK_GROUPS = 4
TOP_K = 8
ROUTED_SCALE = 2.5
EXPERT_BLOCK = 512

LANES = 128
SUBLANES = 8
VMEM_LIMIT = 56 * 1024 * 1024

NT = (((1,), (1,)), ((), ()))
TN = (((0,), (0,)), ((), ()))


def _params(sem, **kw):
    return pltpu.CompilerParams(dimension_semantics=sem, vmem_limit_bytes=VMEM_LIMIT, **kw)


def _dot(a, b):
    return jnp.dot(a, b, preferred_element_type=F32)


def _dg(a, b, dims):
    return lax.dot_general(a, b, dims, preferred_element_type=F32)


def _split(x):
    hi = x.astype(BF16)
    lo = (x - hi.astype(F32)).astype(BF16)
    return hi, lo


def _dot_exact_lhs(a_bf16, x, dims=None):
    hi, lo = _split(x)
    if dims is None:
        return _dot(a_bf16, hi) + _dot(a_bf16, lo)
    return _dg(a_bf16, hi, dims) + _dg(a_bf16, lo, dims)


def _sigmoid(x):
    return 1.0 / (1.0 + jnp.exp(-x))


def _silu(x):
    return x * _sigmoid(x)


def _rms(x, eps):
    return x * lax.rsqrt(jnp.mean(x * x, axis=-1, keepdims=True) + eps)


def _iota2(shape, dim):
    return lax.broadcasted_iota(I32, shape, dim)


def _pack_halves(lo, hi):
    lo_bits = lax.shift_right_logical(pltpu.bitcast(lo, U32), U32(16))
    hi_bits = pltpu.bitcast(hi, U32) & U32(0xFFFF0000)
    return lo_bits | hi_bits


def _unpack_halves(word):
    lo = pltpu.bitcast(lax.shift_left(word, U32(16)), F32)
    hi = pltpu.bitcast(word & U32(0xFFFF0000), F32)
    return lo, hi


def _round_bf16(x):
    return x.astype(BF16).astype(F32)


def _ada_kernel(c_ref, w_ref, b_ref, o_ref):
    cond = _silu(c_ref[...])
    o_ref[...] = jnp.dot(cond, w_ref[...], preferred_element_type=F32,
                         precision=lax.Precision.HIGHEST) + b_ref[...]


def _ada(c, w_ada, b_ada):
    d, n = w_ada.shape
    tn = 1024
    c8 = jnp.broadcast_to(c, (SUBLANES, d))
    out = pl.pallas_call(
        _ada_kernel,
        grid=(n // tn,),
        in_specs=[pl.BlockSpec((SUBLANES, d), lambda j: (0, 0)),
                  pl.BlockSpec((d, tn), lambda j: (0, j)),
                  pl.BlockSpec((1, tn), lambda j: (0, j))],
        out_specs=pl.BlockSpec((SUBLANES, tn), lambda j: (0, j)),
        out_shape=jax.ShapeDtypeStruct((SUBLANES, n), F32),
        compiler_params=_params(("arbitrary",)),
    )(c8, w_ada, b_ada.reshape(1, n))
    return out[0:1]


def _wprep_kernel(a_ref, b_ref, o_ref, *, first_shifted, shift):
    j = pl.program_id(0)

    @pl.when(j < first_shifted)
    def _():
        o_ref[...] = a_ref[...].astype(BF16)

    @pl.when(j >= first_shifted)
    def _():
        tn = a_ref.shape[0]
        o_ref[0:tn - shift, :] = a_ref[shift:tn, :].astype(BF16)
        o_ref[tn - shift:tn, :] = b_ref[...].astype(BF16)


def _wprep(w_in_t, cut0, cut1, tn):
    _, n_in, d = w_in_t.shape
    shift = cut1 - cut0
    n_out = n_in - shift
    assert cut0 % tn == 0 and n_out % tn == 0 and tn % shift == 0 and shift % (2 * SUBLANES) == 0
    return pl.pallas_call(
        functools.partial(_wprep_kernel, first_shifted=cut0 // tn, shift=shift),
        grid=(n_out // tn,),
        in_specs=[pl.BlockSpec((None, tn, d), lambda j: (0, j, 0)),
                  pl.BlockSpec((None, shift, d), lambda j: (0, (j + 1) * (tn // shift), 0))],
        out_specs=pl.BlockSpec((tn, d), lambda j: (j, 0)),
        out_shape=jax.ShapeDtypeStruct((n_out, d), BF16),
        compiler_params=_params(("arbitrary",)),
    )(w_in_t, w_in_t)


def _inproj_kernel(x_ref, g_ref, sc_ref, sh_ref, w_ref, wst_ref, proj_ref, smallt_ref, h_scr):
    @pl.when(pl.program_id(1) == 0)
    def _():
        h = _rms(x_ref[...], NORM_EPS) * g_ref[...] * (1.0 + sc_ref[...]) + sh_ref[...]
        hb = h.astype(BF16)
        h_scr[...] = hb
        smallt_ref[...] = _dg(wst_ref[...], hb, NT)

    proj_ref[...] = _dg(h_scr[...], w_ref[...], NT).astype(BF16)


def _inproj(x, g, sc, sh, w_main_t, w_small_t, tm, tn):
    s, d = x.shape
    n = w_main_t.shape[0]
    ns = w_small_t.shape[0]
    row = lambda i, j: (0, 0)
    return pl.pallas_call(
        _inproj_kernel,
        grid=(s // tm, n // tn),
        in_specs=[pl.BlockSpec((tm, d), lambda i, j: (i, 0)),
                  pl.BlockSpec((1, d), row), pl.BlockSpec((1, d), row), pl.BlockSpec((1, d), row),
                  pl.BlockSpec((tn, d), lambda i, j: (j, 0)),
                  pl.BlockSpec((ns, d), row)],
        out_specs=[pl.BlockSpec((tm, tn), lambda i, j: (i, j)),
                   pl.BlockSpec((ns, tm), lambda i, j: (0, i))],
        out_shape=[jax.ShapeDtypeStruct((s, n), BF16), jax.ShapeDtypeStruct((ns, s), F32)],
        scratch_shapes=[pltpu.VMEM((tm, d), BF16)],
        compiler_params=_params(("arbitrary", "arbitrary")),
    )(x, g, sc, sh, w_main_t, w_small_t)


def _hgrn_kernel(q_ref, f_ref, i_ref, g_ref, lb_ref, on_ref, o_ref, st_scr, *, n_chunks):
    @pl.when(pl.program_id(0) == 0)
    def _():
        st_scr[...] = jnp.zeros_like(st_scr)

    c = CHUNK
    hd = HEAD_DIM
    causal = _iota2((c, c), 1) <= _iota2((c, c), 0)
    tri = causal.astype(BF16)
    lb = lb_ref[...]
    on_g = on_ref[...]
    heads = [slice(h * hd, (h + 1) * hd) for h in range(HEADS)]

    def chunk(n, carry):
        rows = pl.ds(pl.multiple_of(n * c, c), c)
        f = lb + (1.0 - lb) * _sigmoid(f_ref[rows, :].astype(F32))
        b = _dot_exact_lhs(tri, jnp.log(f))
        k = 1.0 - f
        q = _silu(q_ref[rows, :].astype(F32)) * (hd ** -0.5)
        v = i_ref[rows, :]
        b_mid = b[c // 2:c // 2 + 1, :]
        b_last = b[c - 1:c, :]
        qa = (q * jnp.exp(b - b_mid)).astype(BF16)
        ka = (k * jnp.exp(b_mid - b)).astype(BF16)
        qi = (q * jnp.exp(b)).astype(BF16)
        ku = (k * jnp.exp(b_last - b)).astype(BF16)
        dec = jnp.exp(b_last)
        gate = on_g * _silu(g_ref[rows, :].astype(F32))
        sts = [st_scr[h] for h in range(HEADS)]
        scores = [jnp.where(causal, _dg(qa[:, sl], ka[:, sl], NT), 0.0).astype(BF16) for sl in heads]
        inter = [_dg(qi[:, sl], st.astype(BF16), NT) for sl, st in zip(heads, sts)]
        kv = [_dg(v[:, sl], ku[:, sl], TN) for sl in heads]
        for h, sl in enumerate(heads):
            st_scr[h] = dec[:, sl] * sts[h] + kv[h]
        outs = [_rms(_dot(sc, v[:, sl]) + it, NORM_EPS) for sc, sl, it in zip(scores, heads, inter)]
        o_ref[rows, :] = (jnp.concatenate(outs, axis=1) * gate).astype(BF16)
        return carry

    lax.fori_loop(0, n_chunks, chunk, 0, unroll=4)


def _hgrn(proj, lb, onorm_g, ts):
    s = proj.shape[0]
    width = HEADS * HEAD_DIM
    col = lambda blk: pl.BlockSpec((ts, width), lambda j, blk=blk: (j, blk))
    const = pl.BlockSpec((1, width), lambda j: (0, 0))
    return pl.pallas_call(
        functools.partial(_hgrn_kernel, n_chunks=ts // CHUNK),
        grid=(s // ts,),
        in_specs=[col(0), col(1), col(2), col(3), const, const],
        out_specs=pl.BlockSpec((ts, width), lambda j: (j, 0)),
        out_shape=jax.ShapeDtypeStruct((s, width), BF16),
        scratch_shapes=[pltpu.VMEM((HEADS, HEAD_DIM, HEAD_DIM), F32)],
        compiler_params=_params(("arbitrary",)),
    )(proj, proj, proj, proj, lb, jnp.tile(onorm_g, (1, HEADS)))


def _gdn_prep_kernel(q_ref, k_ref, v_ref, qp_ref, kp_ref, vp_ref, wq_ref, wk_ref, wv_ref, ab_ref, alog_ref,
                     dtb_ref, tri_ref, eye_ref, u_ref, wqd_ref, ku_ref, attn_ref, dl_ref, cat_scr, rows_scr, cols_scr,
                     *, n_chunks, ts):
    h = pl.program_id(1)
    first = pl.program_id(0) == 0
    c = CHUNK
    hd = HEAD_DIM

    def conv_silu(cur_ref, prev_ref, w_ref):
        cat_scr[0:8, :] = jnp.where(first, 0.0, prev_ref[...].astype(F32))
        cat_scr[8:8 + ts, :] = cur_ref[...].astype(F32)
        acc = None
        for j in range(CONV_WIDTH):
            off = 8 - (CONV_WIDTH - 1) + j
            term = cat_scr[off:off + ts, :] * w_ref[j:j + 1, :]
            acc = term if acc is None else acc + term
        return _silu(acc)

    def l2n(x):
        return x * lax.rsqrt(jnp.sum(x * x, axis=-1, keepdims=True) + L2_EPS)

    q_all = l2n(conv_silu(q_ref, qp_ref, wq_ref)) * (hd ** -0.5)
    k_all = l2n(conv_silu(k_ref, kp_ref, wk_ref))
    v_all = conv_silu(v_ref, vp_ref, wv_ref)

    @pl.when(h == 0)
    def _():
        z = ab_ref[0:HEADS, :] + dtb_ref[...]
        softplus = jnp.maximum(z, 0.0) + jnp.log(1.0 + jnp.exp(-jnp.abs(z)))
        ld_rows = -jnp.exp(alog_ref[...]) * softplus
        hi, lo = _split(ld_rows)
        tri_blocks = tri_ref[...]
        w = tri_blocks.shape[0]
        spans = [slice(t0, t0 + w) for t0 in range(0, ts, w)]
        g_rows = jnp.concatenate([_dg(hi[:, sp], tri_blocks, NT) + _dg(lo[:, sp], tri_blocks, NT) for sp in spans],
                                 axis=1)
        beta_rows = _sigmoid(ab_ref[HEADS:2 * HEADS, :])
        rows_scr[...] = g_rows
        rows = jnp.concatenate([g_rows, beta_rows, jnp.zeros((LANES - 2 * HEADS, ts), F32)], axis=0)
        r_hi, r_lo = _split(rows)
        r_lo2 = (rows - r_hi.astype(F32) - r_lo.astype(F32)).astype(BF16)
        eye_w = eye_ref[...]
        for sp in spans:
            cols_scr[sp, :] = _dg(eye_w, r_hi[:, sp], NT) + _dg(eye_w, r_lo[:, sp], NT) + _dg(eye_w, r_lo2[:, sp], NT)

    lane = _iota2((ts, LANES), 1)
    cols = cols_scr[...]
    gc_all = jnp.sum(jnp.where(lane == h, cols, 0.0), axis=1, keepdims=True)
    bc_all = jnp.sum(jnp.where(lane == h + HEADS, cols, 0.0), axis=1, keepdims=True)
    g_row = rows_scr[pl.ds(h, 1), :]
    egc_all = jnp.exp(gc_all)

    r = _iota2((c, c), 0)
    cidx = _iota2((c, c), 1)
    causal = cidx <= r
    strict = cidx < r
    eye_f = (r == cidx).astype(F32)
    chunks = [slice(n * c, (n + 1) * c) for n in range(n_chunks)]

    q16 = q_all.astype(BF16)
    k16 = k_all.astype(BF16)
    kq = [_dg(jnp.concatenate([k16[sl], q16[sl]], axis=0), k16[sl], NT) for sl in chunks]
    dm = []
    for sl in chunks:
        diff = gc_all[sl] - g_row[:, sl]
        dm.append(jnp.where(causal, jnp.exp(jnp.where(causal, diff, 0.0)), 0.0))
    bm = [-jnp.where(strict, bc_all[sl] * x[0:c] * d, 0.0) for sl, x, d in zip(chunks, kq, dm)]
    p = [eye_f + b for b in bm]
    bm = [_dot(b.astype(BF16), b.astype(BF16)) for b in bm]
    for _ in range(c.bit_length() - 3):
        res = [_dot(b.astype(BF16), jnp.concatenate([b, pp], axis=1).astype(BF16)) for b, pp in zip(bm, p)]
        p = [pp + x[:, c:2 * c] for pp, x in zip(p, res)]
        bm = [x[:, 0:c] for x in res]
    p = [pp + _dot(b.astype(BF16), pp.astype(BF16)) for b, pp in zip(bm, p)]
    rhs = jnp.concatenate([v_all * bc_all, k_all * (bc_all * egc_all)], axis=1).astype(BF16)
    sol = [_dot(pp.astype(BF16), rhs[sl]) for pp, sl in zip(p, chunks)]
    qd_all = (q_all * egc_all).astype(BF16)
    for n, sl in enumerate(chunks):
        g_last = gc_all[(n + 1) * c - 1:(n + 1) * c, :]
        u_ref[sl, :] = sol[n][:, 0:hd].astype(BF16)
        wqd_ref[2 * n * c:(2 * n + 1) * c, :] = sol[n][:, hd:2 * hd].astype(BF16)
        wqd_ref[(2 * n + 1) * c:(2 * n + 2) * c, :] = qd_all[sl]
        ku_ref[sl, :] = (k_all[sl] * jnp.exp(g_last - gc_all[sl])).astype(BF16)
        attn_ref[sl, :] = (kq[n][c:2 * c] * dm[n]).astype(BF16)
        dl_ref[n:n + 1, :] = jnp.broadcast_to(jnp.exp(g_last), (1, hd))


def _gdn_prep(proj, conv_w, ab_t, a_log, dt_bias, ts):
    s = proj.shape[0]
    hd = HEAD_DIM
    c = CHUNK
    q0 = 4 * HEADS
    cur = lambda off: pl.BlockSpec((ts, hd), lambda j, h, off=off: (j, off + h))
    prev = lambda off: pl.BlockSpec((8, hd), lambda j, h, off=off: (jnp.maximum(j * (ts // 8) - 1, 0), off + h))
    cw = lambda off: pl.BlockSpec((CONV_WIDTH, hd), lambda j, h, off=off: (0, off + h))
    per_head_scalar = pl.BlockSpec((HEADS, 1), lambda j, h: (0, 0))
    w = min(ts, 2 * LANES)
    const = pl.BlockSpec((w, w), lambda j, h: (0, 0))
    pos = jnp.arange(w)
    tri_blocks = ((pos[:, None] // c == pos[None, :] // c) & (pos[None, :] <= pos[:, None])).astype(BF16)
    eye = (pos[:, None] == pos[None, :]).astype(BF16)
    per_head = lambda rows, cols: pl.BlockSpec((None, rows, cols), lambda j, h: (h, j, 0))
    return pl.pallas_call(
        functools.partial(_gdn_prep_kernel, n_chunks=ts // c, ts=ts),
        grid=(s // ts, HEADS),
        in_specs=[cur(q0), cur(q0 + HEADS), cur(q0 + 2 * HEADS),
                  prev(q0), prev(q0 + HEADS), prev(q0 + 2 * HEADS),
                  cw(0), cw(HEADS), cw(2 * HEADS),
                  pl.BlockSpec((2 * HEADS, ts), lambda j, h: (0, j)),
                  per_head_scalar, per_head_scalar, const, const],
        out_specs=[pl.BlockSpec((ts, hd), lambda j, h: (j, h)),
                   pl.BlockSpec((2 * ts, hd), lambda j, h: (j, h)),
                   pl.BlockSpec((ts, hd), lambda j, h: (j, h)),
                   per_head(ts, c),
                   per_head(ts // c, hd)],
        out_shape=[jax.ShapeDtypeStruct((s, HEADS * hd), BF16),
                   jax.ShapeDtypeStruct((2 * s, HEADS * hd), BF16),
                   jax.ShapeDtypeStruct((s, HEADS * hd), BF16),
                   jax.ShapeDtypeStruct((HEADS, s, c), BF16),
                   jax.ShapeDtypeStruct((HEADS, s // c, hd), F32)],
        scratch_shapes=[pltpu.VMEM((ts + 8, hd), F32), pltpu.VMEM((HEADS, ts), F32), pltpu.VMEM((ts, LANES), F32)],
        compiler_params=_params(("arbitrary", "arbitrary")),
    )(proj, proj, proj, proj, proj, proj, conv_w, conv_w, conv_w,
      ab_t, a_log.reshape(HEADS, 1), dt_bias.reshape(HEADS, 1), tri_blocks, eye)


def _gdn_scan_kernel(u_ref, wqd_ref, ku_ref, attn_ref, dl_ref, g_ref, on_ref, o_ref, st_scr, *, n_chunks):
    @pl.when(pl.program_id(0) == 0)
    def _():
        st_scr[...] = jnp.zeros_like(st_scr)

    c = CHUNK
    hd = HEAD_DIM
    on_g = on_ref[...]
    heads = [slice(h * hd, (h + 1) * hd) for h in range(HEADS)]

    def chunk(n, carry):
        rows = pl.ds(pl.multiple_of(n * c, c), c)
        rows2 = pl.ds(pl.multiple_of(2 * n * c, 2 * c), 2 * c)
        sts = [st_scr[h] for h in range(HEADS)]
        wq = [_dot(wqd_ref[rows2, sl], st.astype(BF16)) for sl, st in zip(heads, sts)]
        vn = [(u_ref[rows, sl].astype(F32) - x[0:c]).astype(BF16) for sl, x in zip(heads, wq)]
        upd = [_dg(ku_ref[rows, sl], v, TN) for sl, v in zip(heads, vn)]
        for h in range(HEADS):
            st_scr[h] = dl_ref[h, pl.ds(n, 1), :] * sts[h] + upd[h]
        outs = [_rms(x[c:2 * c] + _dot(attn_ref[h, rows, :], v), NORM_EPS)
                for h, (x, v) in enumerate(zip(wq, vn))]
        gate = jnp.tile(on_g, (1, HEADS)) * _silu(g_ref[rows, :].astype(F32))
        o_ref[rows, :] = (jnp.concatenate(outs, axis=1) * gate).astype(BF16)
        return carry

    lax.fori_loop(0, n_chunks, chunk, 0, unroll=4)


def _gdn_scan(u, wqd, ku, attn, dl, proj, onorm_g, ts):
    s, width = u.shape
    c = CHUNK
    gate_blk = (4 * HEADS + 3 * HEADS) * HEAD_DIM // width
    return pl.pallas_call(
        functools.partial(_gdn_scan_kernel, n_chunks=ts // c),
        grid=(s // ts,),
        in_specs=[pl.BlockSpec((ts, width), lambda j: (j, 0)),
                  pl.BlockSpec((2 * ts, width), lambda j: (j, 0)),
                  pl.BlockSpec((ts, width), lambda j: (j, 0)),
                  pl.BlockSpec((HEADS, ts, c), lambda j: (0, j, 0)),
                  pl.BlockSpec((HEADS, ts // c, HEAD_DIM), lambda j: (0, j, 0)),
                  pl.BlockSpec((ts, width), lambda j: (j, gate_blk)),
                  pl.BlockSpec((1, HEAD_DIM), lambda j: (0, 0))],
        out_specs=pl.BlockSpec((ts, width), lambda j: (j, 0)),
        out_shape=jax.ShapeDtypeStruct((s, width), BF16),
        scratch_shapes=[pltpu.VMEM((HEADS, HEAD_DIM, HEAD_DIM), F32)],
        compiler_params=_params(("arbitrary",)),
    )(u, wqd, ku, attn, dl, proj, onorm_g)


def _merge_kernel(oa_ref, ob_ref, mga_ref, mgb_ref, x_ref, wa_ref, wb_ref, wo_ref, gt_ref, g2_ref, sc_ref,
                  sh_ref, x1_ref, h2_ref):
    ya = _dot(oa_ref[...], wa_ref[...])
    yb = _dot(ob_ref[...], wb_ref[...])
    merged = _sigmoid(mga_ref[...].astype(F32)) * ya + _sigmoid(mgb_ref[...].astype(F32)) * yb
    x1 = x_ref[...] + gt_ref[...] * _dot(merged.astype(BF16), wo_ref[...])
    x1_ref[...] = x1
    h2 = _rms(x1, NORM_EPS) * g2_ref[...] * (1.0 + sc_ref[...]) + sh_ref[...]
    h2_ref[...] = h2.astype(BF16)


def _merge(o_a, o_b, proj, x, w_a, w_b, w_o, gt1, g2, sc2, sh2, tm):
    s, d = x.shape
    dv = o_a.shape[1]
    mg0 = (8 * HEADS * HEAD_DIM) // d
    const = lambda shape: pl.BlockSpec(shape, lambda i: (0, 0), pipeline_mode=pl.Buffered(1))
    return pl.pallas_call(
        _merge_kernel,
        grid=(s // tm,),
        in_specs=[pl.BlockSpec((tm, dv), lambda i: (i, 0)),
                  pl.BlockSpec((tm, dv), lambda i: (i, 0)),
                  pl.BlockSpec((tm, d), lambda i: (i, mg0)),
                  pl.BlockSpec((tm, d), lambda i: (i, mg0 + 1)),
                  pl.BlockSpec((tm, d), lambda i: (i, 0)),
                  const((dv, d)), const((dv, d)), const((d, d)),
                  const((1, d)), const((1, d)), const((1, d)), const((1, d))],
        out_specs=[pl.BlockSpec((tm, d), lambda i: (i, 0)), pl.BlockSpec((tm, d), lambda i: (i, 0))],
        out_shape=[jax.ShapeDtypeStruct((s, d), F32), jax.ShapeDtypeStruct((s, d), BF16)],
        compiler_params=_params(("arbitrary",)),
    )(o_a, o_b, proj, proj, x, w_a, w_b, w_o, gt1, g2, sc2, sh2)


def _first_max(vals, iota, size, axis):
    m = jnp.max(vals, axis=axis, keepdims=True)
    idx = jnp.min(jnp.where(vals == m, iota, size), axis=axis, keepdims=True)
    return m, idx


def _router_kernel(x1_ref, g2_ref, sc_ref, sh_ref, wrt_ref, bias_ref, upper_ref, pos_ref, wts_ref, before_ref,
                   ntile_ref, cnt_scr, *, tm):
    @pl.when(pl.program_id(0) == 0)
    def _():
        cnt_scr[...] = jnp.zeros_like(cnt_scr)

    e = N_EXPERTS
    h2 = _rms(x1_ref[...], NORM_EPS) * g2_ref[...] * (1.0 + sc_ref[...]) + sh_ref[...]
    logits = lax.dot_general(wrt_ref[...], h2, NT, preferred_element_type=F32,
                             precision=lax.Precision.HIGHEST)
    scores = _sigmoid(logits)
    biased = scores + bias_ref[...]
    neg = -jnp.inf

    g3 = biased.reshape(N_GROUPS, GROUP_SIZE, tm)
    i3 = lax.broadcasted_iota(I32, g3.shape, 1)
    m1, a1 = _first_max(g3, i3, GROUP_SIZE, 1)
    m2 = jnp.max(jnp.where(i3 == a1, neg, g3), axis=1, keepdims=True)
    gs = (m1 + m2).reshape(N_GROUPS, tm)
    ig = _iota2(gs.shape, 0)
    gmask = jnp.zeros(gs.shape, jnp.bool_)
    for _ in range(TOPK_GROUPS):
        _, a = _first_max(gs, ig, N_GROUPS, 0)
        pick = ig == a
        gmask = jnp.logical_or(gmask, pick)
        gs = jnp.where(pick, neg, gs)
    emask = jnp.broadcast_to(gmask.reshape(N_GROUPS, 1, tm), (N_GROUPS, GROUP_SIZE, tm)).reshape(e, tm)

    cand = jnp.where(emask, biased, neg)
    ie = _iota2((e, tm), 0)
    sel_all = jnp.zeros((e, tm), jnp.bool_)
    w_rows, picks = [], []
    for _ in range(TOP_K):
        _, a = _first_max(cand, ie, e, 0)
        pick = ie == a
        picks.append(pick)
        w_rows.append(jnp.sum(jnp.where(pick, scores, 0.0), axis=0, keepdims=True))
        sel_all = jnp.logical_or(sel_all, pick)
        cand = jnp.where(pick, neg, cand)
    w_sum = w_rows[0]
    for wr in w_rows[1:]:
        w_sum = w_sum + wr
    wts = jnp.concatenate(w_rows, axis=0) / w_sum * ROUTED_SCALE

    sel = sel_all.astype(BF16)
    in_expert = _dot(sel, upper_ref[...])
    n_tile = jnp.sum(sel_all.astype(F32), axis=1, keepdims=True)
    lower = (_iota2((e, e), 1) < _iota2((e, e), 0)).astype(BF16)
    expert_off = _dot_exact_lhs(lower, jnp.broadcast_to(n_tile, (e, LANES)))[:, 0:1]
    place = in_expert + expert_off
    pos = jnp.concatenate([jnp.sum(jnp.where(pk, place, 0.0), axis=0, keepdims=True) for pk in picks], axis=0)
    pos_ref[...] = pos.astype(I32)
    before_ref[...] = jnp.broadcast_to(cnt_scr[...], before_ref.shape).astype(I32)
    ntile_ref[...] = jnp.broadcast_to(n_tile, ntile_ref.shape).astype(I32)
    cnt_scr[...] = cnt_scr[...] + n_tile
    wts_ref[...] = wts


def _router(x1, g2, sc2, sh2, w_router_t, bias_col, tm):
    s, d = x1.shape
    e = N_EXPERTS
    nt = s // tm
    upper = (jnp.arange(tm)[:, None] < jnp.arange(tm)[None, :]).astype(BF16)
    const = lambda shape: pl.BlockSpec(shape, lambda i: (0, 0))
    per_tile = pl.BlockSpec((None, e, LANES), lambda i: (i, 0, 0))
    return pl.pallas_call(
        functools.partial(_router_kernel, tm=tm),
        grid=(nt,),
        in_specs=[pl.BlockSpec((tm, d), lambda i: (i, 0)),
                  const((1, d)), const((1, d)), const((1, d)),
                  const((e, d)), const((e, 1)), const((tm, tm))],
        out_specs=[pl.BlockSpec((TOP_K, tm), lambda i: (0, i)),
                   pl.BlockSpec((TOP_K, tm), lambda i: (0, i)),
                   per_tile, per_tile],
        out_shape=[jax.ShapeDtypeStruct((TOP_K, s), I32), jax.ShapeDtypeStruct((TOP_K, s), F32),
                   jax.ShapeDtypeStruct((nt, e, LANES), I32), jax.ShapeDtypeStruct((nt, e, LANES), I32)],
        scratch_shapes=[pltpu.VMEM((e, 1), F32)],
        compiler_params=_params(("arbitrary",)),
    )(x1, g2, sc2, sh2, w_router_t, bias_col, upper)


LONG_RUN = 64


def _run_sizes(limit):
    return [1 << b for b in range(limit.bit_length() - 1, -1, -1)]


def _for_each_run(tile, run_refs, tm, make_copy, fn, unroll=False, enable=None):
    run_len_ref, run_off_ref, run_dst_ref = run_refs

    def per_expert(ex, carry):
        n = run_len_ref[tile * N_EXPERTS + ex]
        off = run_off_ref[tile * N_EXPERTS + ex]
        dst = run_dst_ref[tile * N_EXPERTS + ex]
        def pieces(sizes):
            for size in sizes:
                done = n & (-2 * size)

                take = (n & size) != 0
                if enable is not None:
                    take = jnp.logical_and(take, enable)

                @pl.when(take)
                def _(done=done, size=size):
                    fn(make_copy(off + done, dst + done, size))

        sizes = _run_sizes(tm)
        pieces([size for size in sizes if size < LONG_RUN])

        @pl.when(n >= LONG_RUN)
        def _():
            pieces([size for size in sizes if size >= LONG_RUN])

        return carry

    lax.fori_loop(0, N_EXPERTS, per_expert, 0, unroll=unroll)


def _slot_rows(slot, n_slots):
    return pl.ds(pl.multiple_of(slot * SUBLANES, SUBLANES), n_slots * SUBLANES)


def _dispatch_kernel(run_len_ref, run_off_ref, run_dst_ref, pad_lo_ref, pad_hi_ref, pos_ref, h_ref, xs_ref, stage, zero_scr,
                     sem, pad_sem, *, tm, rows_per_pass):
    step = pl.program_id(0)
    na = TOP_K * tm
    d = h_ref.shape[1]
    half = d // 2
    n_words = half // LANES

    def pad_copy(slot, n_slots):
        return pltpu.make_async_copy(zero_scr.at[pl.ds(0, n_slots * SUBLANES), :],
                                     xs_ref.at[_slot_rows(slot, n_slots), :], pad_sem)

    def for_each_pad(fn):
        def per_expert(ex, carry):
            slot = pad_lo_ref[ex]
            n = pad_hi_ref[ex] - slot
            for size in _run_sizes(EXPERT_BLOCK - 1):
                take = (n & size) != 0

                @pl.when(take)
                def _(slot=slot, size=size):
                    fn(pad_copy(slot, size))

                slot = slot + jnp.where(take, size, 0)
            return carry
        lax.fori_loop(0, N_EXPERTS, per_expert, 0)

    @pl.when(step == 0)
    def _():
        zero_scr[...] = jnp.zeros_like(zero_scr)
        for_each_pad(lambda cp: cp.start())

    buf = step % 2
    last = pl.num_programs(0) - 1
    runs = (run_len_ref, run_off_ref, run_dst_ref)

    def run_copy_from(which):
        def run_copy(tile_slot, sorted_slot, n_slots):
            return pltpu.make_async_copy(stage.at[which, _slot_rows(tile_slot, n_slots), :],
                                         xs_ref.at[_slot_rows(sorted_slot, n_slots), :], sem.at[which])
        return run_copy

    def wait_tile(which):
        pltpu.make_async_copy(stage.at[which], xs_ref.at[pl.ds(0, na * SUBLANES), :], sem.at[which]).wait()

    _for_each_run(jnp.maximum(step - 1, 0), runs, tm, run_copy_from(1 - buf), lambda cp: cp.start(),
                  unroll=True, enable=step > 0)

    pos = pos_ref[...]
    h = h_ref[...]
    for a0 in range(0, na, rows_per_pass):
        slot_id = a0 + _iota2((rows_per_pass, tm), 0)
        hit = pos[0:1, :] == slot_id
        for k in range(1, TOP_K):
            hit = jnp.logical_or(hit, pos[k:k + 1, :] == slot_id)
        rows = _dot(hit.astype(BF16), h)
        for i in range(n_words):
            word = _pack_halves(rows[:, i * LANES:(i + 1) * LANES], rows[:, half + i * LANES:half + (i + 1) * LANES])
            stage[buf, pl.ds(a0 * SUBLANES + i, rows_per_pass, stride=SUBLANES), :] = word

    @pl.when(step > 0)
    def _():
        wait_tile(1 - buf)

    @pl.when(step == last)
    def _():
        _for_each_run(step, runs, tm, run_copy_from(buf), lambda cp: cp.start())
        wait_tile(buf)

    @pl.when(step == 0)
    def _():
        for_each_pad(lambda cp: cp.wait())


def _dispatch(runs, pad_lo, pad_hi, pos_t, h2, n_slots, tm):
    s, d = h2.shape
    assert (d // 2) % LANES == 0 and (d // 2) // LANES == SUBLANES, "one token row must pack into one (8, 128) tile"
    na = TOP_K * tm
    return pl.pallas_call(
        functools.partial(_dispatch_kernel, tm=tm, rows_per_pass=min(512, na)),
        grid_spec=pltpu.PrefetchScalarGridSpec(
            num_scalar_prefetch=5,
            grid=(s // tm,),
            in_specs=[pl.BlockSpec((TOP_K, tm), lambda i, *_: (0, i)),
                      pl.BlockSpec((tm, d), lambda i, *_: (i, 0))],
            out_specs=pl.BlockSpec(memory_space=pl.ANY),
            scratch_shapes=[pltpu.VMEM((2, na * SUBLANES, LANES), U32),
                            pltpu.VMEM((EXPERT_BLOCK // 2 * SUBLANES, LANES), U32),
                            pltpu.SemaphoreType.DMA((2,)), pltpu.SemaphoreType.DMA(())]),
        out_shape=jax.ShapeDtypeStruct((n_slots * SUBLANES, LANES), U32),
        compiler_params=_params(("arbitrary",), has_side_effects=True, disable_bounds_checks=True),
    )(*runs, pad_lo, pad_hi, pos_t, h2)


def _expert_kernel(be_ref, nu_ref, next_ref, par_ref, x_ref, wg_hbm, wu_hbm, wd_hbm, y_ref, wg_f32, wu_f32, wd_f32,
                   wg_scr, wu_scr, wd_scr, sem):
    b = pl.program_id(0)
    bm = EXPERT_BLOCK
    active = b < nu_ref[0]
    new_expert = jnp.logical_or(b == 0, be_ref[b] != be_ref[jnp.maximum(b - 1, 0)])

    def weight_copies(ex, which):
        return [pltpu.make_async_copy(src.at[ex], dst.at[which], sem.at[which])
                for src, dst in ((wg_hbm, wg_f32), (wu_hbm, wu_f32), (wd_hbm, wd_f32))]

    @pl.when(jnp.logical_and(active, new_expert))
    def _():
        which = par_ref[b]

        @pl.when(b == 0)
        def _():
            for cp in weight_copies(be_ref[b], which):
                cp.start()

        for cp in weight_copies(be_ref[b], which):
            cp.wait()
        wg_scr[...] = wg_f32[which].astype(BF16)
        wu_scr[...] = wu_f32[which].astype(BF16)
        wd_scr[...] = wd_f32[which].astype(BF16)
        nb = next_ref[b]

        @pl.when(nb < nu_ref[0])
        def _():
            for cp in weight_copies(be_ref[nb], 1 - which):
                cp.start()

    @pl.when(active)
    def _():
        los, his = [], []
        for i in range(SUBLANES):
            lo, hi = _unpack_halves(x_ref[pl.ds(i, bm, stride=SUBLANES), :])
            los.append(lo.astype(BF16))
            his.append(hi.astype(BF16))
        xb = jnp.concatenate(los + his, axis=1)
        hid = _silu(_dot(xb, wg_scr[...])) * _dot(xb, wu_scr[...])
        y = _dot(hid.astype(BF16), wd_scr[...])
        half = y.shape[1] // 2
        for i in range(SUBLANES):
            word = _pack_halves(_round_bf16(y[:, i * LANES:(i + 1) * LANES]),
                                _round_bf16(y[:, half + i * LANES:half + (i + 1) * LANES]))
            y_ref[pl.ds(i, bm, stride=SUBLANES), :] = word


def _experts(block_e, n_used, next_block, parity, xs, w_gate, w_up, w_down):
    d, ff = w_gate.shape[1], w_gate.shape[2]
    bm = EXPERT_BLOCK
    n_blocks = xs.shape[0] // (bm * SUBLANES)
    blk = lambda b, be, nu, *_: (jnp.minimum(b, nu[0] - 1), 0)
    hbm = pl.BlockSpec(memory_space=pl.ANY)
    return pl.pallas_call(
        _expert_kernel,
        grid_spec=pltpu.PrefetchScalarGridSpec(
            num_scalar_prefetch=4,
            grid=(n_blocks,),
            in_specs=[pl.BlockSpec((bm * SUBLANES, LANES), blk), hbm, hbm, hbm],
            out_specs=pl.BlockSpec((bm * SUBLANES, LANES), blk),
            scratch_shapes=[pltpu.VMEM((2, d, ff), F32), pltpu.VMEM((2, d, ff), F32), pltpu.VMEM((2, ff, d), F32),
                            pltpu.VMEM((d, ff), BF16), pltpu.VMEM((d, ff), BF16), pltpu.VMEM((ff, d), BF16),
                            pltpu.SemaphoreType.DMA((2,))]),
        out_shape=jax.ShapeDtypeStruct(xs.shape, U32),
        compiler_params=_params(("arbitrary",)),
    )(block_e, n_used, next_block, parity, xs, w_gate, w_up, w_down)


def _combine_kernel(run_len_ref, run_off_ref, run_src_ref, ys_ref, h_ref, x1_ref, pos_ref, wts_ref, wg_ref, wu_ref, wd_ref, gt_ref, gf_ref,
                    o_ref, stage, sem, *, tm, rows_per_pass):
    step = pl.program_id(0)
    na = TOP_K * tm
    buf = step % 2

    last = pl.num_programs(0) - 1

    def fetch_tile(tile, which, unroll=False):
        def run_copy(tile_slot, sorted_slot, n_slots):
            return pltpu.make_async_copy(ys_ref.at[_slot_rows(sorted_slot, n_slots), :],
                                         stage.at[which, _slot_rows(tile_slot, n_slots), :], sem.at[which])
        _for_each_run(tile, (run_len_ref, run_off_ref, run_src_ref), tm, run_copy, lambda cp: cp.start(), unroll)

    def wait_tile(which):
        pltpu.make_async_copy(ys_ref.at[pl.ds(0, na * SUBLANES), :], stage.at[which], sem.at[which]).wait()

    @pl.when(step == 0)
    def _():
        fetch_tile(step, buf)

    wait_tile(buf)

    fetch_tile(jnp.minimum(step + 1, last), 1 - buf, unroll=True)

    hb = h_ref[...]
    hid = _silu(_dot(hb, wg_ref[...])) * _dot(hb, wu_ref[...])
    acc = _dot(hid.astype(BF16), wd_ref[...])

    pos = pos_ref[...]
    wts = wts_ref[...]
    for a0 in range(0, na, rows_per_pass):
        los, his = [], []
        for i in range(SUBLANES):
            lo, hi = _unpack_halves(stage[buf, pl.ds(a0 * SUBLANES + i, rows_per_pass, stride=SUBLANES), :])
            los.append(lo.astype(BF16))
            his.append(hi.astype(BF16))
        y_rows = jnp.concatenate(los + his, axis=1)
        slot_id = a0 + _iota2((rows_per_pass, tm), 0)
        wmat = jnp.zeros((rows_per_pass, tm), F32)
        for k in range(TOP_K):
            wmat = wmat + jnp.where(pos[k:k + 1, :] == slot_id, wts[k:k + 1, :], 0.0)
        acc = acc + _dg(wmat.astype(BF16), y_rows, TN)
    x2 = x1_ref[...] + gt_ref[...] * acc
    o_ref[...] = _rms(x2, NORM_EPS) * gf_ref[...]

    @pl.when(step == last)
    def _():
        wait_tile(1 - buf)


def _combine(runs, ys, h2, x1, pos_t, wts_t, w_gate, w_up, w_down, gt2, gf, tm):
    s, d = x1.shape
    ff = w_gate.shape[1]
    na = TOP_K * tm
    const = lambda shape: pl.BlockSpec(shape, lambda i, *_: (0, 0), pipeline_mode=pl.Buffered(1))
    tile = lambda cols: pl.BlockSpec((tm, cols), lambda i, *_: (i, 0))
    per_k = pl.BlockSpec((TOP_K, tm), lambda i, *_: (0, i))
    return pl.pallas_call(
        functools.partial(_combine_kernel, tm=tm, rows_per_pass=min(256, na)),
        grid_spec=pltpu.PrefetchScalarGridSpec(
            num_scalar_prefetch=3,
            grid=(s // tm,),
            in_specs=[pl.BlockSpec(memory_space=pl.ANY),
                      tile(d), tile(d), per_k, per_k,
                      const((d, ff)), const((d, ff)), const((ff, d)), const((1, d)), const((1, d))],
            out_specs=tile(d),
            scratch_shapes=[pltpu.VMEM((2, na * SUBLANES, LANES), U32), pltpu.SemaphoreType.DMA((2,))]),
        out_shape=jax.ShapeDtypeStruct((s, d), F32),
        compiler_params=_params(("arbitrary",), disable_bounds_checks=True),
    )(*runs, ys, h2, x1, pos_t, wts_t, w_gate, w_up, w_down, gt2, gf)


def _mixer(x2d, mod, norm1_g, norm2_g, w_in, lb, hgrn_onorm_g, gdn_conv_w, gdn_a_log, gdn_dt_bias, gdn_onorm_g,
           w_branch_hgrn, w_branch_gdn, w_out, tiles):
    d = x2d.shape[1]
    sh1, sc1, gt1, sh2, sc2, _ = [mod[:, i * d:(i + 1) * d] for i in range(6)]
    key = HEADS * HEAD_DIM
    small0 = 4 * key + 3 * key
    small1 = small0 + 2 * HEADS
    w_in_t = jnp.swapaxes(w_in, 1, 2)
    w_main_t = _wprep(w_in_t, small0, small1, tiles["wprep_tn"])
    w_small_t = w_in_t[0, small0:small1, :].astype(BF16)
    proj, ab_t = _inproj(x2d, norm1_g, sc1, sh1, w_main_t, w_small_t, tiles["in_tm"], tiles["in_tn"])
    o_a = _hgrn(proj, lb, hgrn_onorm_g, tiles["mix_ts"])
    u, wqd, ku, attn, dl = _gdn_prep(proj, gdn_conv_w, ab_t, gdn_a_log, gdn_dt_bias, tiles["prep_ts"])
    o_b = _gdn_scan(u, wqd, ku, attn, dl, proj, gdn_onorm_g, tiles["mix_ts"])
    return _merge(o_a, o_b, proj, x2d, w_branch_hgrn.astype(BF16), w_branch_gdn.astype(BF16),
                  w_out.astype(BF16), gt1, norm2_g, sc2, sh2, tiles["merge_tm"])


def _moe(x1, h2, mod, norm2_g, normf_g, w_router, router_bias, w_exp_gate, w_exp_up, w_exp_down, w_sh_gate,
         w_sh_up, w_sh_down, tiles):
    s, d = x1.shape
    tm = tiles["moe_tm"]
    sh2, sc2, gt2 = [mod[:, i * d:(i + 1) * d] for i in (3, 4, 5)]
    pos_t, wts_t, before, ntile = _router(x1, norm2_g, sc2, sh2, w_router.T, router_bias.reshape(-1, 1), tm)
    bm = EXPERT_BLOCK
    n_blocks = -(-(s * TOP_K + N_EXPERTS * (bm - 1)) // bm)
    before = before[:, :, 0]
    ntile = ntile[:, :, 0]
    counts = before[-1] + ntile[-1]
    padded = (counts + bm - 1) // bm * bm
    earlier = jnp.arange(N_EXPERTS)[None, :] < jnp.arange(N_EXPERTS)[:, None]
    pstart = jnp.sum(jnp.where(earlier, padded[None, :], 0), axis=1).astype(I32)
    pend = pstart + padded
    block_start = jnp.arange(n_blocks, dtype=I32) * bm
    block_e = jnp.minimum(jnp.sum(pend[None, :] <= block_start[:, None], axis=1), N_EXPERTS - 1).astype(I32)
    n_used = pend[-1:] // bm
    run_off = jnp.sum(jnp.where(earlier[None], ntile[:, None, :], 0), axis=2)
    runs = (ntile.reshape(-1), run_off.reshape(-1), (before + pstart[None, :]).reshape(-1))
    xs = _dispatch(runs, pstart + counts, pend, pos_t, h2, n_blocks * bm, tm)
    own = block_e[:, None] == jnp.arange(N_EXPERTS)[None, :]
    next_block = jnp.sum(jnp.where(own, pend[None, :], 0), axis=1) // bm
    switches = jnp.concatenate([jnp.zeros((1,), I32), (block_e[1:] != block_e[:-1]).astype(I32)])
    upto = jnp.arange(n_blocks)[None, :] <= jnp.arange(n_blocks)[:, None]
    parity = jnp.sum(jnp.where(upto, switches[None, :], 0), axis=1).astype(I32) % 2
    ys = _experts(block_e, n_used, next_block, parity, xs, w_exp_gate, w_exp_up, w_exp_down)
    return _combine(runs, ys, h2, x1, pos_t, wts_t, w_sh_gate.astype(BF16), w_sh_up.astype(BF16),
                    w_sh_down.astype(BF16), gt2, normf_g, tm)


def _tiles(s):
    pick = lambda want: min(want, s)
    return dict(wprep_tn=512, in_tm=pick(1024), in_tn=1536, mix_ts=pick(512), prep_ts=pick(2048), merge_tm=pick(512),
                moe_tm=pick(256))


def kernel(x, c, w_ada, b_ada, norm1_g, norm2_g, w_in, hgrn_lb_table, hgrn_onorm_g, gdn_conv_w, gdn_a_log, gdn_dt_bias, gdn_onorm_g, w_branch_hgrn, w_branch_gdn, w_out, w_router, router_bias, w_exp_gate, w_exp_up, w_exp_down, w_sh_gate, w_sh_up, w_sh_down, normf_g):
    b, s, d = x.shape
    assert b == 1 and w_ada.shape[0] == 1, "one sequence, one layer"
    tiles = _tiles(s)
    lb = jnp.sum(jax.nn.softmax(hgrn_lb_table.astype(F32), axis=0)[0:1], axis=0, keepdims=True)
    mod = _ada(c, w_ada[0], b_ada[0])
    row = lambda v: v.reshape(1, -1)
    x1, h2 = _mixer(x[0], mod, row(norm1_g[0]), row(norm2_g[0]), w_in, lb, row(hgrn_onorm_g[0]), gdn_conv_w[0],
                    gdn_a_log[0], gdn_dt_bias[0], row(gdn_onorm_g[0]), w_branch_hgrn[0], w_branch_gdn[0], w_out[0],
                    tiles)
    out = _moe(x1, h2, mod, row(norm2_g[0]), row(normf_g), w_router[0], router_bias[0], w_exp_gate[0],
               w_exp_up[0], w_exp_down[0], w_sh_gate[0], w_sh_up[0], w_sh_down[0], tiles)
    return out[None]
```

```python
import functools

import jax
import jax.numpy as jnp
from jax import lax
from jax.experimental import pallas as pl
from jax.experimental.pallas import tpu as pltpu

F32 = jnp.float32
BF16 = jnp.bfloat16
I32 = jnp.int32
U32 = jnp.uint32

NORM_EPS = 1e-6
L2_EPS = 1e-6
HEADS = 8
HEAD_DIM = 128
CONV_WIDTH = 4
CHUNK = 64
N_EXPERTS = 64
N_GROUPS = 8
GROUP_SIZE = N_EXPERTS // N_GROUPS
TOPK_GROUPS = 4
TOP_K = 8
ROUTED_SCALE = 2.5
EXPERT_BLOCK = 512

LANES = 128
SUBLANES = 8
VMEM_LIMIT = 56 * 1024 * 1024

NT = (((1,), (1,)), ((), ()))
TN = (((0,), (0,)), ((), ()))


def _params(sem, **kw):
    return pltpu.CompilerParams(dimension_semantics=sem, vmem_limit_bytes=VMEM_LIMIT, **kw)


def _dot(a, b):
    return jnp.dot(a, b, preferred_element_type=F32)


def _dg(a, b, dims):
    return lax.dot_general(a, b, dims, preferred_element_type=F32)


def _split(x):
    hi = x.astype(BF16)
    lo = (x - hi.astype(F32)).astype(BF16)
    return hi, lo


def _dot_exact_lhs(a_bf16, x, dims=None):
    hi, lo = _split(x)
    if dims is None:
        return _dot(a_bf16, hi) + _dot(a_bf16, lo)
    return _dg(a_bf16, hi, dims) + _dg(a_bf16, lo, dims)


def _sigmoid(x):
    return 1.0 / (1.0 + jnp.exp(-x))


def _silu(x):
    return x * _sigmoid(x)


def _rms(x, eps):
    return x * lax.rsqrt(jnp.mean(x * x, axis=-1, keepdims=True) + eps)


def _iota2(shape, dim):
    return lax.broadcasted_iota(I32, shape, dim)


def _pack_halves(lo, hi):
    lo_bits = lax.shift_right_logical(pltpu.bitcast(lo, U32), U32(16))
    hi_bits = pltpu.bitcast(hi, U32) & U32(0xFFFF0000)
    return lo_bits | hi_bits


def _unpack_halves(word):
    lo = pltpu.bitcast(lax.shift_left(word, U32(16)), F32)
    hi = pltpu.bitcast(word & U32(0xFFFF0000), F32)
    return lo, hi


def _round_bf16(x):
    return x.astype(BF16).astype(F32)


def _ada_kernel(c_ref, w_ref, b_ref, o_ref):
    cond = _silu(c_ref[...])
    o_ref[...] = jnp.sum(w_ref[...] * cond, axis=0, keepdims=True) + b_ref[...]


def _ada(c, w_ada, b_ada):
    d, n = w_ada.shape
    tn = 1024
    return pl.pallas_call(
        _ada_kernel,
        grid=(n // tn,),
        in_specs=[pl.BlockSpec((d, 1), lambda j: (0, 0)),
                  pl.BlockSpec((d, tn), lambda j: (0, j)),
                  pl.BlockSpec((1, tn), lambda j: (0, j))],
        out_specs=pl.BlockSpec((1, tn), lambda j: (0, j)),
        out_shape=jax.ShapeDtypeStruct((1, n), F32),
        compiler_params=_params(("arbitrary",)),
    )(c.reshape(d, 1), w_ada, b_ada.reshape(1, n))


def _wprep_kernel(a_ref, b_ref, o_ref, *, first_shifted, shift):
    j = pl.program_id(0)

    @pl.when(j < first_shifted)
    def _():
        o_ref[...] = a_ref[...].astype(BF16)

    @pl.when(j >= first_shifted)
    def _():
        tn = a_ref.shape[0]
        o_ref[0:tn - shift, :] = a_ref[shift:tn, :].astype(BF16)
        o_ref[tn - shift:tn, :] = b_ref[...].astype(BF16)


def _wprep(w_in_t, cut0, cut1, tn):
    _, n_in, d = w_in_t.shape
    shift = cut1 - cut0
    n_out = n_in - shift
    assert cut0 % tn == 0 and n_out % tn == 0 and tn % shift == 0 and shift % (2 * SUBLANES) == 0
    return pl.pallas_call(
        functools.partial(_wprep_kernel, first_shifted=cut0 // tn, shift=shift),
        grid=(n_out // tn,),
        in_specs=[pl.BlockSpec((None, tn, d), lambda j: (0, j, 0)),
                  pl.BlockSpec((None, shift, d), lambda j: (0, (j + 1) * (tn // shift), 0))],
        out_specs=pl.BlockSpec((tn, d), lambda j: (j, 0)),
        out_shape=jax.ShapeDtypeStruct((n_out, d), BF16),
        compiler_params=_params(("arbitrary",)),
    )(w_in_t, w_in_t)


def _inproj_kernel(x_ref, g_ref, sc_ref, sh_ref, w_ref, wst_ref, proj_ref, smallt_ref, h_scr):
    @pl.when(pl.program_id(1) == 0)
    def _():
        h = _rms(x_ref[...], NORM_EPS) * g_ref[...] * (1.0 + sc_ref[...]) + sh_ref[...]
        hb = h.astype(BF16)
        h_scr[...] = hb
        smallt_ref[...] = _dg(wst_ref[...], hb, NT)

    proj_ref[...] = _dg(h_scr[...], w_ref[...], NT).astype(BF16)


def _inproj(x, g, sc, sh, w_main_t, w_small_t, tm, tn):
    s, d = x.shape
    n = w_main_t.shape[0]
    ns = w_small_t.shape[0]
    row = lambda i, j: (0, 0)
    return pl.pallas_call(
        _inproj_kernel,
        grid=(s // tm, n // tn),
        in_specs=[pl.BlockSpec((tm, d), lambda i, j: (i, 0)),
                  pl.BlockSpec((1, d), row), pl.BlockSpec((1, d), row), pl.BlockSpec((1, d), row),
                  pl.BlockSpec((tn, d), lambda i, j: (j, 0)),
                  pl.BlockSpec((ns, d), row)],
        out_specs=[pl.BlockSpec((tm, tn), lambda i, j: (i, j)),
                   pl.BlockSpec((ns, tm), lambda i, j: (0, i))],
        out_shape=[jax.ShapeDtypeStruct((s, n), BF16), jax.ShapeDtypeStruct((ns, s), F32)],
        scratch_shapes=[pltpu.VMEM((tm, d), BF16)],
        compiler_params=_params(("arbitrary", "arbitrary")),
    )(x, g, sc, sh, w_main_t, w_small_t)


def _hgrn_kernel(q_ref, f_ref, i_ref, g_ref, lb_ref, on_ref, o_ref, st_scr, *, n_chunks):
    @pl.when(pl.program_id(0) == 0)
    def _():
        st_scr[...] = jnp.zeros_like(st_scr)

    c = CHUNK
    hd = HEAD_DIM
    causal = _iota2((c, c), 1) <= _iota2((c, c), 0)
    tri = causal.astype(BF16)
    lb = lb_ref[...]
    on_g = on_ref[...]
    heads = [slice(h * hd, (h + 1) * hd) for h in range(HEADS)]

    def chunk(n, carry):
        rows = pl.ds(pl.multiple_of(n * c, c), c)
        f = lb + (1.0 - lb) * _sigmoid(f_ref[rows, :].astype(F32))
        b = _dot_exact_lhs(tri, jnp.log(f))
        k = 1.0 - f
        q = _silu(q_ref[rows, :].astype(F32)) * (hd ** -0.5)
        v = i_ref[rows, :]
        b_mid = b[c // 2:c // 2 + 1, :]
        b_last = b[c - 1:c, :]
        qa = (q * jnp.exp(b - b_mid)).astype(BF16)
        ka = (k * jnp.exp(b_mid - b)).astype(BF16)
        qi = (q * jnp.exp(b)).astype(BF16)
        ku = (k * jnp.exp(b_last - b)).astype(BF16)
        dec = jnp.exp(b_last)
        gate = on_g * _silu(g_ref[rows, :].astype(F32))
        sts = [st_scr[h] for h in range(HEADS)]
        scores = [jnp.where(causal, _dg(qa[:, sl], ka[:, sl], NT), 0.0).astype(BF16) for sl in heads]
        inter = [_dg(qi[:, sl], st.astype(BF16), NT) for sl, st in zip(heads, sts)]
        kv = [_dg(v[:, sl], ku[:, sl], TN) for sl in heads]
        for h, sl in enumerate(heads):
            st_scr[h] = dec[:, sl] * sts[h] + kv[h]
        outs = [_rms(_dot(sc, v[:, sl]) + it, NORM_EPS) for sc, sl, it in zip(scores, heads, inter)]
        o_ref[rows, :] = (jnp.concatenate(outs, axis=1) * gate).astype(BF16)
        return carry

    lax.fori_loop(0, n_chunks, chunk, 0, unroll=4)


def _hgrn(proj, lb, onorm_g, ts):
    s = proj.shape[0]
    width = HEADS * HEAD_DIM
    col = lambda blk: pl.BlockSpec((ts, width), lambda j, blk=blk: (j, blk))
    const = pl.BlockSpec((1, width), lambda j: (0, 0))
    return pl.pallas_call(
        functools.partial(_hgrn_kernel, n_chunks=ts // CHUNK),
        grid=(s // ts,),
        in_specs=[col(0), col(1), col(2), col(3), const, const],
        out_specs=pl.BlockSpec((ts, width), lambda j: (j, 0)),
        out_shape=jax.ShapeDtypeStruct((s, width), BF16),
        scratch_shapes=[pltpu.VMEM((HEADS, HEAD_DIM, HEAD_DIM), F32)],
        compiler_params=_params(("arbitrary",)),
    )(proj, proj, proj, proj, lb, jnp.tile(onorm_g, (1, HEADS)))


def _gdn_prep_kernel(q_ref, k_ref, v_ref, qp_ref, kp_ref, vp_ref, wq_ref, wk_ref, wv_ref, ab_ref, alog_ref,
                     dtb_ref, tri_ref, eye_ref, u_ref, wqd_ref, ku_ref, attn_ref, dl_ref, cat_scr, rows_scr, cols_scr,
                     *, n_chunks, ts):
    h = pl.program_id(1)
    first = pl.program_id(0) == 0
    c = CHUNK
    hd = HEAD_DIM

    def conv_silu(cur_ref, prev_ref, w_ref):
        cat_scr[0:8, :] = jnp.where(first, 0.0, prev_ref[...].astype(F32))
        cat_scr[8:8 + ts, :] = cur_ref[...].astype(F32)
        acc = None
        for j in range(CONV_WIDTH):
            off = 8 - (CONV_WIDTH - 1) + j
            term = cat_scr[off:off + ts, :] * w_ref[j:j + 1, :]
            acc = term if acc is None else acc + term
        return _silu(acc)

    def l2n(x):
        return x * lax.rsqrt(jnp.sum(x * x, axis=-1, keepdims=True) + L2_EPS)

    q_all = l2n(conv_silu(q_ref, qp_ref, wq_ref)) * (hd ** -0.5)
    k_all = l2n(conv_silu(k_ref, kp_ref, wk_ref))
    v_all = conv_silu(v_ref, vp_ref, wv_ref)

    @pl.when(h == 0)
    def _():
        z = ab_ref[0:HEADS, :] + dtb_ref[...]
        softplus = jnp.maximum(z, 0.0) + jnp.log(1.0 + jnp.exp(-jnp.abs(z)))
        ld_rows = -jnp.exp(alog_ref[...]) * softplus
        hi, lo = _split(ld_rows)
        tri_blocks = tri_ref[...]
        w = tri_blocks.shape[0]
        spans = [slice(t0, t0 + w) for t0 in range(0, ts, w)]
        g_rows = jnp.concatenate([_dg(hi[:, sp], tri_blocks, NT) + _dg(lo[:, sp], tri_blocks, NT) for sp in spans],
                                 axis=1)
        beta_rows = _sigmoid(ab_ref[HEADS:2 * HEADS, :])
        rows_scr[...] = g_rows
        rows = jnp.concatenate([g_rows, beta_rows, jnp.zeros((LANES - 2 * HEADS, ts), F32)], axis=0)
        r_hi, r_lo = _split(rows)
        r_lo2 = (rows - r_hi.astype(F32) - r_lo.astype(F32)).astype(BF16)
        eye_w = eye_ref[...]
        for sp in spans:
            cols_scr[sp, :] = _dg(eye_w, r_hi[:, sp], NT) + _dg(eye_w, r_lo[:, sp], NT) + _dg(eye_w, r_lo2[:, sp], NT)

    lane = _iota2((ts, LANES), 1)
    cols = cols_scr[...]
    gc_all = jnp.sum(jnp.where(lane == h, cols, 0.0), axis=1, keepdims=True)
    bc_all = jnp.sum(jnp.where(lane == h + HEADS, cols, 0.0), axis=1, keepdims=True)
    g_row = rows_scr[pl.ds(h, 1), :]
    egc_all = jnp.exp(gc_all)

    r = _iota2((c, c), 0)
    cidx = _iota2((c, c), 1)
    causal = cidx <= r
    strict = cidx < r
    eye_f = (r == cidx).astype(F32)
    chunks = [slice(n * c, (n + 1) * c) for n in range(n_chunks)]

    q16 = q_all.astype(BF16)
    k16 = k_all.astype(BF16)
    kq = [_dg(jnp.concatenate([k16[sl], q16[sl]], axis=0), k16[sl], NT) for sl in chunks]
    dm = []
    for sl in chunks:
        diff = gc_all[sl] - g_row[:, sl]
        dm.append(jnp.where(causal, jnp.exp(jnp.where(causal, diff, 0.0)), 0.0))
    bm = [-jnp.where(strict, bc_all[sl] * x[0:c] * d, 0.0) for sl, x, d in zip(chunks, kq, dm)]
    p = [eye_f + b for b in bm]
    bm = [_dot(b.astype(BF16), b.astype(BF16)) for b in bm]
    for _ in range(c.bit_length() - 3):
        res = [_dot(b.astype(BF16), jnp.concatenate([b, pp], axis=1).astype(BF16)) for b, pp in zip(bm, p)]
        p = [pp + x[:, c:2 * c] for pp, x in zip(p, res)]
        bm = [x[:, 0:c] for x in res]
    p = [pp + _dot(b.astype(BF16), pp.astype(BF16)) for b, pp in zip(bm, p)]
    rhs = jnp.concatenate([v_all * bc_all, k_all * (bc_all * egc_all)], axis=1).astype(BF16)
    sol = [_dot(pp.astype(BF16), rhs[sl]) for pp, sl in zip(p, chunks)]
    qd_all = (q_all * egc_all).astype(BF16)
    for n, sl in enumerate(chunks):
        g_last = gc_all[(n + 1) * c - 1:(n + 1) * c, :]
        u_ref[sl, :] = sol[n][:, 0:hd].astype(BF16)
        wqd_ref[2 * n * c:(2 * n + 1) * c, :] = sol[n][:, hd:2 * hd].astype(BF16)
        wqd_ref[(2 * n + 1) * c:(2 * n + 2) * c, :] = qd_all[sl]
        ku_ref[sl, :] = (k_all[sl] * jnp.exp(g_last - gc_all[sl])).astype(BF16)
        attn_ref[sl, :] = (kq[n][c:2 * c] * dm[n]).astype(BF16)
        dl_ref[n:n + 1, :] = jnp.broadcast_to(jnp.exp(g_last), (1, hd))


def _gdn_prep(proj, conv_w, ab_t, a_log, dt_bias, ts):
    s = proj.shape[0]
    hd = HEAD_DIM
    c = CHUNK
    q0 = 4 * HEADS
    cur = lambda off: pl.BlockSpec((ts, hd), lambda j, h, off=off: (j, off + h))
    prev = lambda off: pl.BlockSpec((8, hd), lambda j, h, off=off: (jnp.maximum(j * (ts // 8) - 1, 0), off + h))
    cw = lambda off: pl.BlockSpec((CONV_WIDTH, hd), lambda j, h, off=off: (0, off + h))
    per_head_scalar = pl.BlockSpec((HEADS, 1), lambda j, h: (0, 0))
    w = min(ts, 2 * LANES)
    const = pl.BlockSpec((w, w), lambda j, h: (0, 0))
    pos = jnp.arange(w)
    tri_blocks = ((pos[:, None] // c == pos[None, :] // c) & (pos[None, :] <= pos[:, None])).astype(BF16)
    eye = (pos[:, None] == pos[None, :]).astype(BF16)
    per_head = lambda rows, cols: pl.BlockSpec((None, rows, cols), lambda j, h: (h, j, 0))
    return pl.pallas_call(
        functools.partial(_gdn_prep_kernel, n_chunks=ts // c, ts=ts),
        grid=(s // ts, HEADS),
        in_specs=[cur(q0), cur(q0 + HEADS), cur(q0 + 2 * HEADS),
                  prev(q0), prev(q0 + HEADS), prev(q0 + 2 * HEADS),
                  cw(0), cw(HEADS), cw(2 * HEADS),
                  pl.BlockSpec((2 * HEADS, ts), lambda j, h: (0, j)),
                  per_head_scalar, per_head_scalar, const, const],
        out_specs=[pl.BlockSpec((ts, hd), lambda j, h: (j, h)),
                   pl.BlockSpec((2 * ts, hd), lambda j, h: (j, h)),
                   pl.BlockSpec((ts, hd), lambda j, h: (j, h)),
                   per_head(ts, c),
                   per_head(ts // c, hd)],
        out_shape=[jax.ShapeDtypeStruct((s, HEADS * hd), BF16),
                   jax.ShapeDtypeStruct((2 * s, HEADS * hd), BF16),
                   jax.ShapeDtypeStruct((s, HEADS * hd), BF16),
                   jax.ShapeDtypeStruct((HEADS, s, c), BF16),
                   jax.ShapeDtypeStruct((HEADS, s // c, hd), F32)],
        scratch_shapes=[pltpu.VMEM((ts + 8, hd), F32), pltpu.VMEM((HEADS, ts), F32), pltpu.VMEM((ts, LANES), F32)],
        compiler_params=_params(("arbitrary", "arbitrary")),
    )(proj, proj, proj, proj, proj, proj, conv_w, conv_w, conv_w,
      ab_t, a_log.reshape(HEADS, 1), dt_bias.reshape(HEADS, 1), tri_blocks, eye)


def _gdn_scan_kernel(u_ref, wqd_ref, ku_ref, attn_ref, dl_ref, g_ref, on_ref, o_ref, st_scr, *, n_chunks):
    @pl.when(pl.program_id(0) == 0)
    def _():
        st_scr[...] = jnp.zeros_like(st_scr)

    c = CHUNK
    hd = HEAD_DIM
    on_g = on_ref[...]
    heads = [slice(h * hd, (h + 1) * hd) for h in range(HEADS)]

    def chunk(n, carry):
        rows = pl.ds(pl.multiple_of(n * c, c), c)
        rows2 = pl.ds(pl.multiple_of(2 * n * c, 2 * c), 2 * c)
        sts = [st_scr[h] for h in range(HEADS)]
        wq = [_dot(wqd_ref[rows2, sl], st.astype(BF16)) for sl, st in zip(heads, sts)]
        vn = [(u_ref[rows, sl].astype(F32) - x[0:c]).astype(BF16) for sl, x in zip(heads, wq)]
        upd = [_dg(ku_ref[rows, sl], v, TN) for sl, v in zip(heads, vn)]
        for h in range(HEADS):
            st_scr[h] = dl_ref[h, pl.ds(n, 1), :] * sts[h] + upd[h]
        outs = [_rms(x[c:2 * c] + _dot(attn_ref[h, rows, :], v), NORM_EPS)
                for h, (x, v) in enumerate(zip(wq, vn))]
        gate = jnp.tile(on_g, (1, HEADS)) * _silu(g_ref[rows, :].astype(F32))
        o_ref[rows, :] = (jnp.concatenate(outs, axis=1) * gate).astype(BF16)
        return carry

    lax.fori_loop(0, n_chunks, chunk, 0, unroll=4)


def _gdn_scan(u, wqd, ku, attn, dl, proj, onorm_g, ts):
    s, width = u.shape
    c = CHUNK
    gate_blk = (4 * HEADS + 3 * HEADS) * HEAD_DIM // width
    return pl.pallas_call(
        functools.partial(_gdn_scan_kernel, n_chunks=ts // c),
        grid=(s // ts,),
        in_specs=[pl.BlockSpec((ts, width), lambda j: (j, 0)),
                  pl.BlockSpec((2 * ts, width), lambda j: (j, 0)),
                  pl.BlockSpec((ts, width), lambda j: (j, 0)),
                  pl.BlockSpec((HEADS, ts, c), lambda j: (0, j, 0)),
                  pl.BlockSpec((HEADS, ts // c, HEAD_DIM), lambda j: (0, j, 0)),
                  pl.BlockSpec((ts, width), lambda j: (j, gate_blk)),
                  pl.BlockSpec((1, HEAD_DIM), lambda j: (0, 0))],
        out_specs=pl.BlockSpec((ts, width), lambda j: (j, 0)),
        out_shape=jax.ShapeDtypeStruct((s, width), BF16),
        scratch_shapes=[pltpu.VMEM((HEADS, HEAD_DIM, HEAD_DIM), F32)],
        compiler_params=_params(("arbitrary",)),
    )(u, wqd, ku, attn, dl, proj, onorm_g)


def _merge_kernel(oa_ref, ob_ref, mga_ref, mgb_ref, x_ref, wa_ref, wb_ref, wo_ref, gt_ref, g2_ref, sc_ref,
                  sh_ref, x1_ref, h2_ref):
    ya = _dot(oa_ref[...], wa_ref[...])
    yb = _dot(ob_ref[...], wb_ref[...])
    merged = _sigmoid(mga_ref[...].astype(F32)) * ya + _sigmoid(mgb_ref[...].astype(F32)) * yb
    x1 = x_ref[...] + gt_ref[...] * _dot(merged.astype(BF16), wo_ref[...])
    x1_ref[...] = x1
    h2 = _rms(x1, NORM_EPS) * g2_ref[...] * (1.0 + sc_ref[...]) + sh_ref[...]
    h2_ref[...] = h2.astype(BF16)


def _merge(o_a, o_b, proj, x, w_a, w_b, w_o, gt1, g2, sc2, sh2, tm):
    s, d = x.shape
    dv = o_a.shape[1]
    mg0 = (8 * HEADS * HEAD_DIM) // d
    const = lambda shape: pl.BlockSpec(shape, lambda i: (0, 0), pipeline_mode=pl.Buffered(1))
    return pl.pallas_call(
        _merge_kernel,
        grid=(s // tm,),
        in_specs=[pl.BlockSpec((tm, dv), lambda i: (i, 0)),
                  pl.BlockSpec((tm, dv), lambda i: (i, 0)),
                  pl.BlockSpec((tm, d), lambda i: (i, mg0)),
                  pl.BlockSpec((tm, d), lambda i: (i, mg0 + 1)),
                  pl.BlockSpec((tm, d), lambda i: (i, 0)),
                  const((dv, d)), const((dv, d)), const((d, d)),
                  const((1, d)), const((1, d)), const((1, d)), const((1, d))],
        out_specs=[pl.BlockSpec((tm, d), lambda i: (i, 0)), pl.BlockSpec((tm, d), lambda i: (i, 0))],
        out_shape=[jax.ShapeDtypeStruct((s, d), F32), jax.ShapeDtypeStruct((s, d), BF16)],
        compiler_params=_params(("arbitrary",)),
    )(o_a, o_b, proj, proj, x, w_a, w_b, w_o, gt1, g2, sc2, sh2)


def _first_max(vals, iota, size, axis):
    m = jnp.max(vals, axis=axis, keepdims=True)
    idx = jnp.min(jnp.where(vals == m, iota, size), axis=axis, keepdims=True)
    return m, idx


def _router_kernel(x1_ref, g2_ref, sc_ref, sh_ref, wrt_ref, bias_ref, upper_ref, pos_ref, wts_ref, before_ref,
                   ntile_ref, cnt_scr, *, tm):
    @pl.when(pl.program_id(0) == 0)
    def _():
        cnt_scr[...] = jnp.zeros_like(cnt_scr)

    e = N_EXPERTS
    h2 = _rms(x1_ref[...], NORM_EPS) * g2_ref[...] * (1.0 + sc_ref[...]) + sh_ref[...]
    logits = lax.dot_general(wrt_ref[...], h2, NT, preferred_element_type=F32,
                             precision=lax.Precision.HIGHEST)
    scores = _sigmoid(logits)
    biased = scores + bias_ref[...]
    neg = -jnp.inf

    g3 = biased.reshape(N_GROUPS, GROUP_SIZE, tm)
    i3 = lax.broadcasted_iota(I32, g3.shape, 1)
    m1, a1 = _first_max(g3, i3, GROUP_SIZE, 1)
    m2 = jnp.max(jnp.where(i3 == a1, neg, g3), axis=1, keepdims=True)
    gs = (m1 + m2).reshape(N_GROUPS, tm)
    ig = _iota2(gs.shape, 0)
    gmask = jnp.zeros(gs.shape, jnp.bool_)
    for _ in range(TOPK_GROUPS):
        _, a = _first_max(gs, ig, N_GROUPS, 0)
        pick = ig == a
        gmask = jnp.logical_or(gmask, pick)
        gs = jnp.where(pick, neg, gs)
    emask = jnp.broadcast_to(gmask.reshape(N_GROUPS, 1, tm), (N_GROUPS, GROUP_SIZE, tm)).reshape(e, tm)

    cand = jnp.where(emask, biased, neg)
    ie = _iota2((e, tm), 0)
    sel_all = jnp.zeros((e, tm), jnp.bool_)
    w_rows, picks = [], []
    for _ in range(TOP_K):
        _, a = _first_max(cand, ie, e, 0)
        pick = ie == a
        picks.append(pick)
        w_rows.append(jnp.sum(jnp.where(pick, scores, 0.0), axis=0, keepdims=True))
        sel_all = jnp.logical_or(sel_all, pick)
        cand = jnp.where(pick, neg, cand)
    w_sum = w_rows[0]
    for wr in w_rows[1:]:
        w_sum = w_sum + wr
    wts = jnp.concatenate(w_rows, axis=0) / w_sum * ROUTED_SCALE

    sel = sel_all.astype(BF16)
    in_expert = _dot(sel, upper_ref[...])
    n_tile = jnp.sum(sel_all.astype(F32), axis=1, keepdims=True)
    lower = (_iota2((e, e), 1) < _iota2((e, e), 0)).astype(BF16)
    expert_off = _dot_exact_lhs(lower, jnp.broadcast_to(n_tile, (e, LANES)))[:, 0:1]
    place = in_expert + expert_off
    pos = jnp.concatenate([jnp.sum(jnp.where(pk, place, 0.0), axis=0, keepdims=True) for pk in picks], axis=0)
    pos_ref[...] = pos.astype(I32)
    before_ref[...] = jnp.broadcast_to(cnt_scr[...], before_ref.shape).astype(I32)
    ntile_ref[...] = jnp.broadcast_to(n_tile, ntile_ref.shape).astype(I32)
    cnt_scr[...] = cnt_scr[...] + n_tile
    wts_ref[...] = wts


def _router(x1, g2, sc2, sh2, w_router_t, bias_col, tm):
    s, d = x1.shape
    e = N_EXPERTS
    nt = s // tm
    upper = (jnp.arange(tm)[:, None] < jnp.arange(tm)[None, :]).astype(BF16)
    const = lambda shape: pl.BlockSpec(shape, lambda i: (0, 0))
    per_tile = pl.BlockSpec((None, e, LANES), lambda i: (i, 0, 0))
    return pl.pallas_call(
        functools.partial(_router_kernel, tm=tm),
        grid=(nt,),
        in_specs=[pl.BlockSpec((tm, d), lambda i: (i, 0)),
                  const((1, d)), const((1, d)), const((1, d)),
                  const((e, d)), const((e, 1)), const((tm, tm))],
        out_specs=[pl.BlockSpec((TOP_K, tm), lambda i: (0, i)),
                   pl.BlockSpec((TOP_K, tm), lambda i: (0, i)),
                   per_tile, per_tile],
        out_shape=[jax.ShapeDtypeStruct((TOP_K, s), I32), jax.ShapeDtypeStruct((TOP_K, s), F32),
                   jax.ShapeDtypeStruct((nt, e, LANES), I32), jax.ShapeDtypeStruct((nt, e, LANES), I32)],
        scratch_shapes=[pltpu.VMEM((e, 1), F32)],
        compiler_params=_params(("arbitrary",)),
    )(x1, g2, sc2, sh2, w_router_t, bias_col, upper)


LONG_RUN = 64


def _run_sizes(limit):
    return [1 << b for b in range(limit.bit_length() - 1, -1, -1)]


def _for_each_run(tile, run_refs, tm, make_copy, fn, unroll=False, enable=None):
    run_len_ref, run_off_ref, run_dst_ref = run_refs

    def per_expert(ex, carry):
        n = run_len_ref[tile * N_EXPERTS + ex]
        off = run_off_ref[tile * N_EXPERTS + ex]
        dst = run_dst_ref[tile * N_EXPERTS + ex]
        def pieces(sizes):
            for size in sizes:
                done = n & (-2 * size)

                take = (n & size) != 0
                if enable is not None:
                    take = jnp.logical_and(take, enable)

                @pl.when(take)
                def _(done=done, size=size):
                    fn(make_copy(off + done, dst + done, size))

        sizes = _run_sizes(tm)
        pieces([size for size in sizes if size < LONG_RUN])

        @pl.when(n >= LONG_RUN)
        def _():
            pieces([size for size in sizes if size >= LONG_RUN])

        return carry

    lax.fori_loop(0, N_EXPERTS, per_expert, 0, unroll=unroll)


def _slot_rows(slot, n_slots):
    return pl.ds(pl.multiple_of(slot * SUBLANES, SUBLANES), n_slots * SUBLANES)


def _dispatch_kernel(run_len_ref, run_off_ref, run_dst_ref, pad_lo_ref, pad_hi_ref, pos_ref, h_ref, xs_ref, stage, zero_scr,
                     sem, pad_sem, *, tm, rows_per_pass):
    step = pl.program_id(0)
    na = TOP_K * tm
    d = h_ref.shape[1]
    half = d // 2
    n_words = half // LANES

    def pad_copy(slot, n_slots):
        return pltpu.make_async_copy(zero_scr.at[pl.ds(0, n_slots * SUBLANES), :],
                                     xs_ref.at[_slot_rows(slot, n_slots), :], pad_sem)

    def for_each_pad(fn):
        def per_expert(ex, carry):
            slot = pad_lo_ref[ex]
            n = pad_hi_ref[ex] - slot
            for size in _run_sizes(EXPERT_BLOCK - 1):
                take = (n & size) != 0

                @pl.when(take)
                def _(slot=slot, size=size):
                    fn(pad_copy(slot, size))

                slot = slot + jnp.where(take, size, 0)
            return carry
        lax.fori_loop(0, N_EXPERTS, per_expert, 0)

    @pl.when(step == 0)
    def _():
        zero_scr[...] = jnp.zeros_like(zero_scr)
        for_each_pad(lambda cp: cp.start())

    buf = step % 2
    last = pl.num_programs(0) - 1
    runs = (run_len_ref, run_off_ref, run_dst_ref)

    def run_copy_from(which):
        def run_copy(tile_slot, sorted_slot, n_slots):
            return pltpu.make_async_copy(stage.at[which, _slot_rows(tile_slot, n_slots), :],
                                         xs_ref.at[_slot_rows(sorted_slot, n_slots), :], sem.at[which])
        return run_copy

    def wait_tile(which):
        pltpu.make_async_copy(stage.at[which], xs_ref.at[pl.ds(0, na * SUBLANES), :], sem.at[which]).wait()

    _for_each_run(jnp.maximum(step - 1, 0), runs, tm, run_copy_from(1 - buf), lambda cp: cp.start(),
                  unroll=True, enable=step > 0)

    pos = pos_ref[...]
    h = h_ref[...]
    for a0 in range(0, na, rows_per_pass):
        slot_id = a0 + _iota2((rows_per_pass, tm), 0)
        hit = pos[0:1, :] == slot_id
        for k in range(1, TOP_K):
            hit = jnp.logical_or(hit, pos[k:k + 1, :] == slot_id)
        rows = _dot(hit.astype(BF16), h)
        for i in range(n_words):
            word = _pack_halves(rows[:, i * LANES:(i + 1) * LANES], rows[:, half + i * LANES:half + (i + 1) * LANES])
            stage[buf, pl.ds(a0 * SUBLANES + i, rows_per_pass, stride=SUBLANES), :] = word

    @pl.when(step > 0)
    def _():
        wait_tile(1 - buf)

    @pl.when(step == last)
    def _():
        _for_each_run(step, runs, tm, run_copy_from(buf), lambda cp: cp.start())
        wait_tile(buf)

    @pl.when(step == 0)
    def _():
        for_each_pad(lambda cp: cp.wait())


def _dispatch(runs, pad_lo, pad_hi, pos_t, h2, n_slots, tm):
    s, d = h2.shape
    assert (d // 2) % LANES == 0 and (d // 2) // LANES == SUBLANES, "one token row must pack into one (8, 128) tile"
    na = TOP_K * tm
    return pl.pallas_call(
        functools.partial(_dispatch_kernel, tm=tm, rows_per_pass=min(512, na)),
        grid_spec=pltpu.PrefetchScalarGridSpec(
            num_scalar_prefetch=5,
            grid=(s // tm,),
            in_specs=[pl.BlockSpec((TOP_K, tm), lambda i, *_: (0, i)),
                      pl.BlockSpec((tm, d), lambda i, *_: (i, 0))],
            out_specs=pl.BlockSpec(memory_space=pl.ANY),
            scratch_shapes=[pltpu.VMEM((2, na * SUBLANES, LANES), U32),
                            pltpu.VMEM((EXPERT_BLOCK // 2 * SUBLANES, LANES), U32),
                            pltpu.SemaphoreType.DMA((2,)), pltpu.SemaphoreType.DMA(())]),
        out_shape=jax.ShapeDtypeStruct((n_slots * SUBLANES, LANES), U32),
        compiler_params=_params(("arbitrary",), has_side_effects=True, disable_bounds_checks=True),
    )(*runs, pad_lo, pad_hi, pos_t, h2)


def _expert_kernel(be_ref, nu_ref, next_ref, par_ref, x_ref, wg_hbm, wu_hbm, wd_hbm, y_ref, wg_f32, wu_f32, wd_f32,
                   wg_scr, wu_scr, wd_scr, sem):
    b = pl.program_id(0)
    bm = EXPERT_BLOCK
    active = b < nu_ref[0]
    new_expert = jnp.logical_or(b == 0, be_ref[b] != be_ref[jnp.maximum(b - 1, 0)])

    def weight_copies(ex, which):
        return [pltpu.make_async_copy(src.at[ex], dst.at[which], sem.at[which])
                for src, dst in ((wg_hbm, wg_f32), (wu_hbm, wu_f32), (wd_hbm, wd_f32))]

    @pl.when(jnp.logical_and(active, new_expert))
    def _():
        which = par_ref[b]

        @pl.when(b == 0)
        def _():
            for cp in weight_copies(be_ref[b], which):
                cp.start()

        for cp in weight_copies(be_ref[b], which):
            cp.wait()
        wg_scr[...] = wg_f32[which].astype(BF16)
        wu_scr[...] = wu_f32[which].astype(BF16)
        wd_scr[...] = wd_f32[which].astype(BF16)
        nb = next_ref[b]

        @pl.when(nb < nu_ref[0])
        def _():
            for cp in weight_copies(be_ref[nb], 1 - which):
                cp.start()

    @pl.when(active)
    def _():
        los, his = [], []
        for i in range(SUBLANES):
            lo, hi = _unpack_halves(x_ref[pl.ds(i, bm, stride=SUBLANES), :])
            los.append(lo.astype(BF16))
            his.append(hi.astype(BF16))
        xb = jnp.concatenate(los + his, axis=1)
        hid = _silu(_dot(xb, wg_scr[...])) * _dot(xb, wu_scr[...])
        y = _dot(hid.astype(BF16), wd_scr[...])
        half = y.shape[1] // 2
        for i in range(SUBLANES):
            word = _pack_halves(_round_bf16(y[:, i * LANES:(i + 1) * LANES]),
                                _round_bf16(y[:, half + i * LANES:half + (i + 1) * LANES]))
            y_ref[pl.ds(i, bm, stride=SUBLANES), :] = word


def _experts(block_e, n_used, next_block, parity, xs, w_gate, w_up, w_down):
    d, ff = w_gate.shape[1], w_gate.shape[2]
    bm = EXPERT_BLOCK
    n_blocks = xs.shape[0] // (bm * SUBLANES)
    blk = lambda b, be, nu, *_: (jnp.minimum(b, nu[0] - 1), 0)
    hbm = pl.BlockSpec(memory_space=pl.ANY)
    return pl.pallas_call(
        _expert_kernel,
        grid_spec=pltpu.PrefetchScalarGridSpec(
            num_scalar_prefetch=4,
            grid=(n_blocks,),
            in_specs=[pl.BlockSpec((bm * SUBLANES, LANES), blk), hbm, hbm, hbm],
            out_specs=pl.BlockSpec((bm * SUBLANES, LANES), blk),
            scratch_shapes=[pltpu.VMEM((2, d, ff), F32), pltpu.VMEM((2, d, ff), F32), pltpu.VMEM((2, ff, d), F32),
                            pltpu.VMEM((d, ff), BF16), pltpu.VMEM((d, ff), BF16), pltpu.VMEM((ff, d), BF16),
                            pltpu.SemaphoreType.DMA((2,))]),
        out_shape=jax.ShapeDtypeStruct(xs.shape, U32),
        compiler_params=_params(("arbitrary",)),
    )(block_e, n_used, next_block, parity, xs, w_gate, w_up, w_down)


def _combine_kernel(run_len_ref, run_off_ref, run_src_ref, ys_ref, h_ref, x1_ref, pos_ref, wts_ref, wg_ref, wu_ref, wd_ref, gt_ref, gf_ref,
                    o_ref, stage, sem, *, tm, rows_per_pass):
    step = pl.program_id(0)
    na = TOP_K * tm
    buf = step % 2

    last = pl.num_programs(0) - 1

    def fetch_tile(tile, which, unroll=False):
        def run_copy(tile_slot, sorted_slot, n_slots):
            return pltpu.make_async_copy(ys_ref.at[_slot_rows(sorted_slot, n_slots), :],
                                         stage.at[which, _slot_rows(tile_slot, n_slots), :], sem.at[which])
        _for_each_run(tile, (run_len_ref, run_off_ref, run_src_ref), tm, run_copy, lambda cp: cp.start(), unroll)

    def wait_tile(which):
        pltpu.make_async_copy(ys_ref.at[pl.ds(0, na * SUBLANES), :], stage.at[which], sem.at[which]).wait()

    @pl.when(step == 0)
    def _():
        fetch_tile(step, buf)

    fetch_tile(jnp.minimum(step + 1, last), 1 - buf, unroll=True)

    hb = h_ref[...]
    hid = _silu(_dot(hb, wg_ref[...])) * _dot(hb, wu_ref[...])
    acc = _dot(hid.astype(BF16), wd_ref[...])

    wait_tile(buf)

    pos = pos_ref[...]
    wts = wts_ref[...]
    for a0 in range(0, na, rows_per_pass):
        los, his = [], []
        for i in range(SUBLANES):
            lo, hi = _unpack_halves(stage[buf, pl.ds(a0 * SUBLANES + i, rows_per_pass, stride=SUBLANES), :])
            los.append(lo.astype(BF16))
            his.append(hi.astype(BF16))
        y_rows = jnp.concatenate(los + his, axis=1)
        slot_id = a0 + _iota2((rows_per_pass, tm), 0)
        wmat = jnp.zeros((rows_per_pass, tm), F32)
        for k in range(TOP_K):
            wmat = wmat + jnp.where(pos[k:k + 1, :] == slot_id, wts[k:k + 1, :], 0.0)
        acc = acc + _dg(wmat.astype(BF16), y_rows, TN)
    x2 = x1_ref[...] + gt_ref[...] * acc
    o_ref[...] = _rms(x2, NORM_EPS) * gf_ref[...]

    @pl.when(step == last)
    def _():
        wait_tile(1 - buf)


def _combine(runs, ys, h2, x1, pos_t, wts_t, w_gate, w_up, w_down, gt2, gf, tm):
    s, d = x1.shape
    ff = w_gate.shape[1]
    na = TOP_K * tm
    const = lambda shape: pl.BlockSpec(shape, lambda i, *_: (0, 0), pipeline_mode=pl.Buffered(1))
    tile = lambda cols: pl.BlockSpec((tm, cols), lambda i, *_: (i, 0))
    per_k = pl.BlockSpec((TOP_K, tm), lambda i, *_: (0, i))
    return pl.pallas_call(
        functools.partial(_combine_kernel, tm=tm, rows_per_pass=min(256, na)),
        grid_spec=pltpu.PrefetchScalarGridSpec(
            num_scalar_prefetch=3,
            grid=(s // tm,),
            in_specs=[pl.BlockSpec(memory_space=pl.ANY),
                      tile(d), tile(d), per_k, per_k,
                      const((d, ff)), const((d, ff)), const((ff, d)), const((1, d)), const((1, d))],
            out_specs=tile(d),
            scratch_shapes=[pltpu.VMEM((2, na * SUBLANES, LANES), U32), pltpu.SemaphoreType.DMA((2,))]),
        out_shape=jax.ShapeDtypeStruct((s, d), F32),
        compiler_params=_params(("arbitrary",), disable_bounds_checks=True),
    )(*runs, ys, h2, x1, pos_t, wts_t, w_gate, w_up, w_down, gt2, gf)


def _mixer(x2d, mod, norm1_g, norm2_g, w_in, lb, hgrn_onorm_g, gdn_conv_w, gdn_a_log, gdn_dt_bias, gdn_onorm_g,
           w_branch_hgrn, w_branch_gdn, w_out, tiles):
    d = x2d.shape[1]
    sh1, sc1, gt1, sh2, sc2, _ = [mod[:, i * d:(i + 1) * d] for i in range(6)]
    key = HEADS * HEAD_DIM
    small0 = 4 * key + 3 * key
    small1 = small0 + 2 * HEADS
    w_in_t = jnp.swapaxes(w_in, 1, 2)
    w_main_t = _wprep(w_in_t, small0, small1, tiles["wprep_tn"])
    w_small_t = w_in_t[0, small0:small1, :].astype(BF16)
    proj, ab_t = _inproj(x2d, norm1_g, sc1, sh1, w_main_t, w_small_t, tiles["in_tm"], tiles["in_tn"])
    o_a = _hgrn(proj, lb, hgrn_onorm_g, tiles["mix_ts"])
    u, wqd, ku, attn, dl = _gdn_prep(proj, gdn_conv_w, ab_t, gdn_a_log, gdn_dt_bias, tiles["prep_ts"])
    o_b = _gdn_scan(u, wqd, ku, attn, dl, proj, gdn_onorm_g, tiles["mix_ts"])
    return _merge(o_a, o_b, proj, x2d, w_branch_hgrn.astype(BF16), w_branch_gdn.astype(BF16),
                  w_out.astype(BF16), gt1, norm2_g, sc2, sh2, tiles["merge_tm"])


def _moe(x1, h2, mod, norm2_g, normf_g, w_router, router_bias, w_exp_gate, w_exp_up, w_exp_down, w_sh_gate,
         w_sh_up, w_sh_down, tiles):
    s, d = x1.shape
    tm = tiles["moe_tm"]
    sh2, sc2, gt2 = [mod[:, i * d:(i + 1) * d] for i in (3, 4, 5)]
    pos_t, wts_t, before, ntile = _router(x1, norm2_g, sc2, sh2, w_router.T, router_bias.reshape(-1, 1), tm)
    bm = EXPERT_BLOCK
    n_blocks = -(-(s * TOP_K + N_EXPERTS * (bm - 1)) // bm)
    before = before[:, :, 0]
    ntile = ntile[:, :, 0]
    counts = before[-1] + ntile[-1]
    padded = (counts + bm - 1) // bm * bm
    earlier = jnp.arange(N_EXPERTS)[None, :] < jnp.arange(N_EXPERTS)[:, None]
    pstart = jnp.sum(jnp.where(earlier, padded[None, :], 0), axis=1).astype(I32)
    pend = pstart + padded
    block_start = jnp.arange(n_blocks, dtype=I32) * bm
    block_e = jnp.minimum(jnp.sum(pend[None, :] <= block_start[:, None], axis=1), N_EXPERTS - 1).astype(I32)
    n_used = pend[-1:] // bm
    run_off = jnp.sum(jnp.where(earlier[None], ntile[:, None, :], 0), axis=2)
    runs = (ntile.reshape(-1), run_off.reshape(-1), (before + pstart[None, :]).reshape(-1))
    xs = _dispatch(runs, pstart + counts, pend, pos_t, h2, n_blocks * bm, tm)
    own = block_e[:, None] == jnp.arange(N_EXPERTS)[None, :]
    next_block = jnp.sum(jnp.where(own, pend[None, :], 0), axis=1) // bm
    switches = jnp.concatenate([jnp.zeros((1,), I32), (block_e[1:] != block_e[:-1]).astype(I32)])
    upto = jnp.arange(n_blocks)[None, :] <= jnp.arange(n_blocks)[:, None]
    parity = jnp.sum(jnp.where(upto, switches[None, :], 0), axis=1).astype(I32) % 2
    ys = _experts(block_e, n_used, next_block, parity, xs, w_exp_gate, w_exp_up, w_exp_down)
    return _combine(runs, ys, h2, x1, pos_t, wts_t, w_sh_gate.astype(BF16), w_sh_up.astype(BF16),
                    w_sh_down.astype(BF16), gt2, normf_g, tm)


def _tiles(s):
    pick = lambda want: min(want, s)
    return dict(wprep_tn=512, in_tm=pick(1024), in_tn=1536, mix_ts=pick(512), prep_ts=pick(2048), merge_tm=pick(512),
                moe_tm=pick(256))


def kernel(x, c, w_ada, b_ada, norm1_g, norm2_g, w_in, hgrn_lb_table, hgrn_onorm_g, gdn_conv_w, gdn_a_log, gdn_dt_bias, gdn_onorm_g, w_branch_hgrn, w_branch_gdn, w_out, w_router, router_bias, w_exp_gate, w_exp_up, w_exp_down, w_sh_gate, w_sh_up, w_sh_down, normf_g):
    b, s, d = x.shape
    assert b == 1 and w_ada.shape[0] == 1, "one sequence, one layer"
    tiles = _tiles(s)
    lb = jnp.sum(jax.nn.softmax(hgrn_lb_table.astype(F32), axis=0)[0:1], axis=0, keepdims=True)
    mod = _ada(c, w_ada[0], b_ada[0])
    row = lambda v: v.reshape(1, -1)
    x1, h2 = _mixer(x[0], mod, row(norm1_g[0]), row(norm2_g[0]), w_in, lb, row(hgrn_onorm_g[0]), gdn_conv_w[0],
                    gdn_a_log[0], gdn_dt_bias[0], row(gdn_onorm_g[0]), w_branch_hgrn[0], w_branch_gdn[0], w_out[0],
                    tiles)
    out = _moe(x1, h2, mod, row(norm2_g[0]), row(normf_g), w_router[0], router_bias[0], w_exp_gate[0],
               w_exp_up[0], w_exp_down[0], w_sh_gate[0], w_sh_up[0], w_sh_down[0], tiles)
    return out[None]
```

```python
import functools

import jax
import jax.numpy as jnp
from jax import lax
from jax.experimental import pallas as pl
from jax.experimental.pallas import tpu as pltpu

F32 = jnp.float32
BF16 = jnp.bfloat16
I32 = jnp.int32
U32 = jnp.uint32

NORM_EPS = 1e-6
L2_EPS = 1e-6
HEADS = 8
HEAD_DIM = 128
CONV_WIDTH = 4
CHUNK = 64
N_EXPERTS = 64
N_GROUPS = 8
GROUP_SIZE = N_EXPERTS // N_GROUPS
TOPK_GROUPS = 4
TOP_K = 8
ROUTED_SCALE = 2.5
EXPERT_BLOCK = 512

LANES = 128
SUBLANES = 8
VMEM_LIMIT = 56 * 1024 * 1024

NT = (((1,), (1,)), ((), ()))
TN = (((0,), (0,)), ((), ()))


def _params(sem, **kw):
    return pltpu.CompilerParams(dimension_semantics=sem, vmem_limit_bytes=VMEM_LIMIT, **kw)


def _dot(a, b):
    return jnp.dot(a, b, preferred_element_type=F32)


def _dg(a, b, dims):
    return lax.dot_general(a, b, dims, preferred_element_type=F32)


def _split(x):
    hi = x.astype(BF16)
    lo = (x - hi.astype(F32)).astype(BF16)
    return hi, lo


def _dot_exact_lhs(a_bf16, x, dims=None):
    hi, lo = _split(x)
    if dims is None:
        return _dot(a_bf16, hi) + _dot(a_bf16, lo)
    return _dg(a_bf16, hi, dims) + _dg(a_bf16, lo, dims)


def _sigmoid(x):
    return 1.0 / (1.0 + jnp.exp(-x))


def _silu(x):
    return x * _sigmoid(x)


def _rms(x, eps):
    return x * lax.rsqrt(jnp.mean(x * x, axis=-1, keepdims=True) + eps)


def _iota2(shape, dim):
    return lax.broadcasted_iota(I32, shape, dim)


def _pack_halves(lo, hi):
    lo_bits = lax.shift_right_logical(pltpu.bitcast(lo, U32), U32(16))
    hi_bits = pltpu.bitcast(hi, U32) & U32(0xFFFF0000)
    return lo_bits | hi_bits


def _unpack_halves(word):
    lo = pltpu.bitcast(lax.shift_left(word, U32(16)), F32)
    hi = pltpu.bitcast(word & U32(0xFFFF0000), F32)
    return lo, hi


def _round_bf16(x):
    return x.astype(BF16).astype(F32)


def _ada_kernel(c_ref, w_ref, b_ref, o_ref):
    cond = _silu(c_ref[...])
    o_ref[...] = jnp.sum(w_ref[...] * cond, axis=0, keepdims=True) + b_ref[...]


def _ada(c, w_ada, b_ada):
    d, n = w_ada.shape
    tn = 1024
    return pl.pallas_call(
        _ada_kernel,
        grid=(n // tn,),
        in_specs=[pl.BlockSpec((d, 1), lambda j: (0, 0)),
                  pl.BlockSpec((d, tn), lambda j: (0, j)),
                  pl.BlockSpec((1, tn), lambda j: (0, j))],
        out_specs=pl.BlockSpec((1, tn), lambda j: (0, j)),
        out_shape=jax.ShapeDtypeStruct((1, n), F32),
        compiler_params=_params(("arbitrary",)),
    )(c.reshape(d, 1), w_ada, b_ada.reshape(1, n))


def _wprep_kernel(a_ref, b_ref, o_ref, *, first_shifted, shift):
    j = pl.program_id(0)

    @pl.when(j < first_shifted)
    def _():
        o_ref[...] = a_ref[...].astype(BF16)

    @pl.when(j >= first_shifted)
    def _():
        tn = a_ref.shape[0]
        o_ref[0:tn - shift, :] = a_ref[shift:tn, :].astype(BF16)
        o_ref[tn - shift:tn, :] = b_ref[...].astype(BF16)


def _wprep(w_in_t, cut0, cut1, tn):
    _, n_in, d = w_in_t.shape
    shift = cut1 - cut0
    n_out = n_in - shift
    assert cut0 % tn == 0 and n_out % tn == 0 and tn % shift == 0 and shift % (2 * SUBLANES) == 0
    return pl.pallas_call(
        functools.partial(_wprep_kernel, first_shifted=cut0 // tn, shift=shift),
        grid=(n_out // tn,),
        in_specs=[pl.BlockSpec((None, tn, d), lambda j: (0, j, 0)),
                  pl.BlockSpec((None, shift, d), lambda j: (0, (j + 1) * (tn // shift), 0))],
        out_specs=pl.BlockSpec((tn, d), lambda j: (j, 0)),
        out_shape=jax.ShapeDtypeStruct((n_out, d), BF16),
        compiler_params=_params(("arbitrary",)),
    )(w_in_t, w_in_t)


def _inproj_kernel(x_ref, g_ref, sc_ref, sh_ref, w_ref, wst_ref, proj_ref, smallt_ref, h_scr):
    @pl.when(pl.program_id(1) == 0)
    def _():
        h = _rms(x_ref[...], NORM_EPS) * g_ref[...] * (1.0 + sc_ref[...]) + sh_ref[...]
        hb = h.astype(BF16)
        h_scr[...] = hb
        smallt_ref[...] = _dg(wst_ref[...], hb, NT)

    proj_ref[...] = _dg(h_scr[...], w_ref[...], NT).astype(BF16)


def _inproj(x, g, sc, sh, w_main_t, w_small_t, tm, tn):
    s, d = x.shape
    n = w_main_t.shape[0]
    ns = w_small_t.shape[0]
    row = lambda i, j: (0, 0)
    return pl.pallas_call(
        _inproj_kernel,
        grid=(s // tm, n // tn),
        in_specs=[pl.BlockSpec((tm, d), lambda i, j: (i, 0)),
                  pl.BlockSpec((1, d), row), pl.BlockSpec((1, d), row), pl.BlockSpec((1, d), row),
                  pl.BlockSpec((tn, d), lambda i, j: (j, 0)),
                  pl.BlockSpec((ns, d), row)],
        out_specs=[pl.BlockSpec((tm, tn), lambda i, j: (i, j)),
                   pl.BlockSpec((ns, tm), lambda i, j: (0, i))],
        out_shape=[jax.ShapeDtypeStruct((s, n), BF16), jax.ShapeDtypeStruct((ns, s), F32)],
        scratch_shapes=[pltpu.VMEM((tm, d), BF16)],
        compiler_params=_params(("arbitrary", "arbitrary")),
    )(x, g, sc, sh, w_main_t, w_small_t)


def _hgrn_kernel(q_ref, f_ref, i_ref, g_ref, lb_ref, on_ref, o_ref, st_scr, *, n_chunks):
    @pl.when(pl.program_id(0) == 0)
    def _():
        st_scr[...] = jnp.zeros_like(st_scr)

    c = CHUNK
    hd = HEAD_DIM
    causal = _iota2((c, c), 1) <= _iota2((c, c), 0)
    tri = causal.astype(BF16)
    lb = lb_ref[...]
    on_g = on_ref[...]
    heads = [slice(h * hd, (h + 1) * hd) for h in range(HEADS)]

    def chunk(n, carry):
        rows = pl.ds(pl.multiple_of(n * c, c), c)
        f = lb + (1.0 - lb) * _sigmoid(f_ref[rows, :].astype(F32))
        b = _dot_exact_lhs(tri, jnp.log(f))
        k = 1.0 - f
        q = _silu(q_ref[rows, :].astype(F32)) * (hd ** -0.5)
        v = i_ref[rows, :]
        b_mid = b[c // 2:c // 2 + 1, :]
        b_last = b[c - 1:c, :]
        qa = (q * jnp.exp(b - b_mid)).astype(BF16)
        ka = (k * jnp.exp(b_mid - b)).astype(BF16)
        qi = (q * jnp.exp(b)).astype(BF16)
        ku = (k * jnp.exp(b_last - b)).astype(BF16)
        dec = jnp.exp(b_last)
        gate = on_g * _silu(g_ref[rows, :].astype(F32))
        sts = [st_scr[h] for h in range(HEADS)]
        scores = [jnp.where(causal, _dg(qa[:, sl], ka[:, sl], NT), 0.0).astype(BF16) for sl in heads]
        inter = [_dg(qi[:, sl], st.astype(BF16), NT) for sl, st in zip(heads, sts)]
        kv = [_dg(v[:, sl], ku[:, sl], TN) for sl in heads]
        for h, sl in enumerate(heads):
            st_scr[h] = dec[:, sl] * sts[h] + kv[h]
        outs = [_rms(_dot(sc, v[:, sl]) + it, NORM_EPS) for sc, sl, it in zip(scores, heads, inter)]
        o_ref[rows, :] = (jnp.concatenate(outs, axis=1) * gate).astype(BF16)
        return carry

    lax.fori_loop(0, n_chunks, chunk, 0, unroll=4)


def _hgrn(proj, lb, onorm_g, ts):
    s = proj.shape[0]
    width = HEADS * HEAD_DIM
    col = lambda blk: pl.BlockSpec((ts, width), lambda j, blk=blk: (j, blk))
    const = pl.BlockSpec((1, width), lambda j: (0, 0))
    return pl.pallas_call(
        functools.partial(_hgrn_kernel, n_chunks=ts // CHUNK),
        grid=(s // ts,),
        in_specs=[col(0), col(1), col(2), col(3), const, const],
        out_specs=pl.BlockSpec((ts, width), lambda j: (j, 0)),
        out_shape=jax.ShapeDtypeStruct((s, width), BF16),
        scratch_shapes=[pltpu.VMEM((HEADS, HEAD_DIM, HEAD_DIM), F32)],
        compiler_params=_params(("arbitrary",)),
    )(proj, proj, proj, proj, lb, jnp.tile(onorm_g, (1, HEADS)))


def _gdn_prep_kernel(q_ref, k_ref, v_ref, qp_ref, kp_ref, vp_ref, wq_ref, wk_ref, wv_ref, ab_ref, alog_ref,
                     dtb_ref, tri_ref, eye_ref, u_ref, wqd_ref, ku_ref, attn_ref, dl_ref, cat_scr, rows_scr, cols_scr,
                     *, n_chunks, ts):
    h = pl.program_id(1)
    first = pl.program_id(0) == 0
    c = CHUNK
    hd = HEAD_DIM

    def conv_silu(cur_ref, prev_ref, w_ref):
        cat_scr[0:8, :] = jnp.where(first, 0.0, prev_ref[...].astype(F32))
        cat_scr[8:8 + ts, :] = cur_ref[...].astype(F32)
        acc = None
        for j in range(CONV_WIDTH):
            off = 8 - (CONV_WIDTH - 1) + j
            term = cat_scr[off:off + ts, :] * w_ref[j:j + 1, :]
            acc = term if acc is None else acc + term
        return _silu(acc)

    def l2n(x):
        return x * lax.rsqrt(jnp.sum(x * x, axis=-1, keepdims=True) + L2_EPS)

    q_all = l2n(conv_silu(q_ref, qp_ref, wq_ref)) * (hd ** -0.5)
    k_all = l2n(conv_silu(k_ref, kp_ref, wk_ref))
    v_all = conv_silu(v_ref, vp_ref, wv_ref)

    @pl.when(h == 0)
    def _():
        z = ab_ref[0:HEADS, :] + dtb_ref[...]
        softplus = jnp.maximum(z, 0.0) + jnp.log(1.0 + jnp.exp(-jnp.abs(z)))
        ld_rows = -jnp.exp(alog_ref[...]) * softplus
        hi, lo = _split(ld_rows)
        tri_blocks = tri_ref[...]
        w = tri_blocks.shape[0]
        spans = [slice(t0, t0 + w) for t0 in range(0, ts, w)]
        g_rows = jnp.concatenate([_dg(hi[:, sp], tri_blocks, NT) + _dg(lo[:, sp], tri_blocks, NT) for sp in spans],
                                 axis=1)
        beta_rows = _sigmoid(ab_ref[HEADS:2 * HEADS, :])
        rows_scr[...] = g_rows
        rows = jnp.concatenate([g_rows, beta_rows, jnp.zeros((LANES - 2 * HEADS, ts), F32)], axis=0)
        r_hi, r_lo = _split(rows)
        r_lo2 = (rows - r_hi.astype(F32) - r_lo.astype(F32)).astype(BF16)
        eye_w = eye_ref[...]
        for sp in spans:
            cols_scr[sp, :] = _dg(eye_w, r_hi[:, sp], NT) + _dg(eye_w, r_lo[:, sp], NT) + _dg(eye_w, r_lo2[:, sp], NT)

    lane = _iota2((ts, LANES), 1)
    cols = cols_scr[...]
    gc_all = jnp.sum(jnp.where(lane == h, cols, 0.0), axis=1, keepdims=True)
    bc_all = jnp.sum(jnp.where(lane == h + HEADS, cols, 0.0), axis=1, keepdims=True)
    g_row = rows_scr[pl.ds(h, 1), :]
    egc_all = jnp.exp(gc_all)

    r = _iota2((c, c), 0)
    cidx = _iota2((c, c), 1)
    causal = cidx <= r
    strict = cidx < r
    eye_f = (r == cidx).astype(F32)
    chunks = [slice(n * c, (n + 1) * c) for n in range(n_chunks)]

    q16 = q_all.astype(BF16)
    k16 = k_all.astype(BF16)
    kq = [_dg(jnp.concatenate([k16[sl], q16[sl]], axis=0), k16[sl], NT) for sl in chunks]
    dm = []
    for sl in chunks:
        diff = gc_all[sl] - g_row[:, sl]
        dm.append(jnp.where(causal, jnp.exp(jnp.where(causal, diff, 0.0)), 0.0))
    bm = [-jnp.where(strict, bc_all[sl] * x[0:c] * d, 0.0) for sl, x, d in zip(chunks, kq, dm)]
    p = [eye_f + b for b in bm]
    bm = [_dot(b.astype(BF16), b.astype(BF16)) for b in bm]
    for _ in range(c.bit_length() - 3):
        res = [_dot(b.astype(BF16), jnp.concatenate([b, pp], axis=1).astype(BF16)) for b, pp in zip(bm, p)]
        p = [pp + x[:, c:2 * c] for pp, x in zip(p, res)]
        bm = [x[:, 0:c] for x in res]
    p = [pp + _dot(b.astype(BF16), pp.astype(BF16)) for b, pp in zip(bm, p)]
    rhs = jnp.concatenate([v_all * bc_all, k_all * (bc_all * egc_all)], axis=1).astype(BF16)
    sol = [_dot(pp.astype(BF16), rhs[sl]) for pp, sl in zip(p, chunks)]
    qd_all = (q_all * egc_all).astype(BF16)
    for n, sl in enumerate(chunks):
        g_last = gc_all[(n + 1) * c - 1:(n + 1) * c, :]
        u_ref[sl, :] = sol[n][:, 0:hd].astype(BF16)
        wqd_ref[2 * n * c:(2 * n + 1) * c, :] = sol[n][:, hd:2 * hd].astype(BF16)
        wqd_ref[(2 * n + 1) * c:(2 * n + 2) * c, :] = qd_all[sl]
        ku_ref[sl, :] = (k_all[sl] * jnp.exp(g_last - gc_all[sl])).astype(BF16)
        attn_ref[sl, :] = (kq[n][c:2 * c] * dm[n]).astype(BF16)
        dl_ref[n:n + 1, :] = jnp.broadcast_to(jnp.exp(g_last), (1, hd))


def _gdn_prep(proj, conv_w, ab_t, a_log, dt_bias, ts):
    s = proj.shape[0]
    hd = HEAD_DIM
    c = CHUNK
    q0 = 4 * HEADS
    cur = lambda off: pl.BlockSpec((ts, hd), lambda j, h, off=off: (j, off + h))
    prev = lambda off: pl.BlockSpec((8, hd), lambda j, h, off=off: (jnp.maximum(j * (ts // 8) - 1, 0), off + h))
    cw = lambda off: pl.BlockSpec((CONV_WIDTH, hd), lambda j, h, off=off: (0, off + h))
    per_head_scalar = pl.BlockSpec((HEADS, 1), lambda j, h: (0, 0))
    w = min(ts, 2 * LANES)
    const = pl.BlockSpec((w, w), lambda j, h: (0, 0))
    pos = jnp.arange(w)
    tri_blocks = ((pos[:, None] // c == pos[None, :] // c) & (pos[None, :] <= pos[:, None])).astype(BF16)
    eye = (pos[:, None] == pos[None, :]).astype(BF16)
    per_head = lambda rows, cols: pl.BlockSpec((None, rows, cols), lambda j, h: (h, j, 0))
    return pl.pallas_call(
        functools.partial(_gdn_prep_kernel, n_chunks=ts // c, ts=ts),
        grid=(s // ts, HEADS),
        in_specs=[cur(q0), cur(q0 + HEADS), cur(q0 + 2 * HEADS),
                  prev(q0), prev(q0 + HEADS), prev(q0 + 2 * HEADS),
                  cw(0), cw(HEADS), cw(2 * HEADS),
                  pl.BlockSpec((2 * HEADS, ts), lambda j, h: (0, j)),
                  per_head_scalar, per_head_scalar, const, const],
        out_specs=[pl.BlockSpec((ts, hd), lambda j, h: (j, h)),
                   pl.BlockSpec((2 * ts, hd), lambda j, h: (j, h)),
                   pl.BlockSpec((ts, hd), lambda j, h: (j, h)),
                   per_head(ts, c),
                   per_head(ts // c, hd)],
        out_shape=[jax.ShapeDtypeStruct((s, HEADS * hd), BF16),
                   jax.ShapeDtypeStruct((2 * s, HEADS * hd), BF16),
                   jax.ShapeDtypeStruct((s, HEADS * hd), BF16),
                   jax.ShapeDtypeStruct((HEADS, s, c), BF16),
                   jax.ShapeDtypeStruct((HEADS, s // c, hd), F32)],
        scratch_shapes=[pltpu.VMEM((ts + 8, hd), F32), pltpu.VMEM((HEADS, ts), F32), pltpu.VMEM((ts, LANES), F32)],
        compiler_params=_params(("arbitrary", "arbitrary")),
    )(proj, proj, proj, proj, proj, proj, conv_w, conv_w, conv_w,
      ab_t, a_log.reshape(HEADS, 1), dt_bias.reshape(HEADS, 1), tri_blocks, eye)


def _gdn_scan_kernel(u_ref, wqd_ref, ku_ref, attn_ref, dl_ref, g_ref, on_ref, o_ref, st_scr, *, n_chunks):
    @pl.when(pl.program_id(0) == 0)
    def _():
        st_scr[...] = jnp.zeros_like(st_scr)

    c = CHUNK
    hd = HEAD_DIM
    on_g = on_ref[...]
    heads = [slice(h * hd, (h + 1) * hd) for h in range(HEADS)]

    def chunk(n, carry):
        rows = pl.ds(pl.multiple_of(n * c, c), c)
        rows2 = pl.ds(pl.multiple_of(2 * n * c, 2 * c), 2 * c)
        sts = [st_scr[h] for h in range(HEADS)]
        wq = [_dot(wqd_ref[rows2, sl], st.astype(BF16)) for sl, st in zip(heads, sts)]
        vn = [(u_ref[rows, sl].astype(F32) - x[0:c]).astype(BF16) for sl, x in zip(heads, wq)]
        upd = [_dg(ku_ref[rows, sl], v, TN) for sl, v in zip(heads, vn)]
        for h in range(HEADS):
            st_scr[h] = dl_ref[h, pl.ds(n, 1), :] * sts[h] + upd[h]
        outs = [_rms(x[c:2 * c] + _dot(attn_ref[h, rows, :], v), NORM_EPS)
                for h, (x, v) in enumerate(zip(wq, vn))]
        gate = jnp.tile(on_g, (1, HEADS)) * _silu(g_ref[rows, :].astype(F32))
        o_ref[rows, :] = (jnp.concatenate(outs, axis=1) * gate).astype(BF16)
        return carry

    lax.fori_loop(0, n_chunks, chunk, 0, unroll=4)


def _gdn_scan(u, wqd, ku, attn, dl, proj, onorm_g, ts):
    s, width = u.shape
    c = CHUNK
    gate_blk = (4 * HEADS + 3 * HEADS) * HEAD_DIM // width
    return pl.pallas_call(
        functools.partial(_gdn_scan_kernel, n_chunks=ts // c),
        grid=(s // ts,),
        in_specs=[pl.BlockSpec((ts, width), lambda j: (j, 0)),
                  pl.BlockSpec((2 * ts, width), lambda j: (j, 0)),
                  pl.BlockSpec((ts, width), lambda j: (j, 0)),
                  pl.BlockSpec((HEADS, ts, c), lambda j: (0, j, 0)),
                  pl.BlockSpec((HEADS, ts // c, HEAD_DIM), lambda j: (0, j, 0)),
                  pl.BlockSpec((ts, width), lambda j: (j, gate_blk)),
                  pl.BlockSpec((1, HEAD_DIM), lambda j: (0, 0))],
        out_specs=pl.BlockSpec((ts, width), lambda j: (j, 0)),
        out_shape=jax.ShapeDtypeStruct((s, width), BF16),
        scratch_shapes=[pltpu.VMEM((HEADS, HEAD_DIM, HEAD_DIM), F32)],
        compiler_params=_params(("arbitrary",)),
    )(u, wqd, ku, attn, dl, proj, onorm_g)


def _merge_kernel(oa_ref, ob_ref, mga_ref, mgb_ref, x_ref, wa_ref, wb_ref, wo_ref, gt_ref, g2_ref, sc_ref,
                  sh_ref, x1_ref, h2_ref):
    ya = _dot(oa_ref[...], wa_ref[...])
    yb = _dot(ob_ref[...], wb_ref[...])
    merged = _sigmoid(mga_ref[...].astype(F32)) * ya + _sigmoid(mgb_ref[...].astype(F32)) * yb
    x1 = x_ref[...] + gt_ref[...] * _dot(merged.astype(BF16), wo_ref[...])
    x1_ref[...] = x1
    h2 = _rms(x1, NORM_EPS) * g2_ref[...] * (1.0 + sc_ref[...]) + sh_ref[...]
    h2_ref[...] = h2.astype(BF16)


def _merge(o_a, o_b, proj, x, w_a, w_b, w_o, gt1, g2, sc2, sh2, tm):
    s, d = x.shape
    dv = o_a.shape[1]
    mg0 = (8 * HEADS * HEAD_DIM) // d
    const = lambda shape: pl.BlockSpec(shape, lambda i: (0, 0), pipeline_mode=pl.Buffered(1))
    return pl.pallas_call(
        _merge_kernel,
        grid=(s // tm,),
        in_specs=[pl.BlockSpec((tm, dv), lambda i: (i, 0)),
                  pl.BlockSpec((tm, dv), lambda i: (i, 0)),
                  pl.BlockSpec((tm, d), lambda i: (i, mg0)),
                  pl.BlockSpec((tm, d), lambda i: (i, mg0 + 1)),
                  pl.BlockSpec((tm, d), lambda i: (i, 0)),
                  const((dv, d)), const((dv, d)), const((d, d)),
                  const((1, d)), const((1, d)), const((1, d)), const((1, d))],
        out_specs=[pl.BlockSpec((tm, d), lambda i: (i, 0)), pl.BlockSpec((tm, d), lambda i: (i, 0))],
        out_shape=[jax.ShapeDtypeStruct((s, d), F32), jax.ShapeDtypeStruct((s, d), BF16)],
        compiler_params=_params(("arbitrary",)),
    )(o_a, o_b, proj, proj, x, w_a, w_b, w_o, gt1, g2, sc2, sh2)


def _first_max(vals, iota, size, axis):
    m = jnp.max(vals, axis=axis, keepdims=True)
    idx = jnp.min(jnp.where(vals == m, iota, size), axis=axis, keepdims=True)
    return m, idx


def _router_kernel(x1_ref, g2_ref, sc_ref, sh_ref, wrt_ref, bias_ref, upper_ref, pos_ref, wts_ref, before_ref,
                   ntile_ref, cnt_scr, *, tm):
    @pl.when(pl.program_id(0) == 0)
    def _():
        cnt_scr[...] = jnp.zeros_like(cnt_scr)

    e = N_EXPERTS
    h2 = _rms(x1_ref[...], NORM_EPS) * g2_ref[...] * (1.0 + sc_ref[...]) + sh_ref[...]
    logits = lax.dot_general(wrt_ref[...], h2, NT, preferred_element_type=F32,
                             precision=lax.Precision.HIGHEST)
    scores = _sigmoid(logits)
    biased = scores + bias_ref[...]
    neg = -jnp.inf

    g3 = biased.reshape(N_GROUPS, GROUP_SIZE, tm)
    i3 = lax.broadcasted_iota(I32, g3.shape, 1)
    m1, a1 = _first_max(g3, i3, GROUP_SIZE, 1)
    m2 = jnp.max(jnp.where(i3 == a1, neg, g3), axis=1, keepdims=True)
    gs = (m1 + m2).reshape(N_GROUPS, tm)
    ig = _iota2(gs.shape, 0)
    gmask = jnp.zeros(gs.shape, jnp.bool_)
    for _ in range(TOPK_GROUPS):
        _, a = _first_max(gs, ig, N_GROUPS, 0)
        pick = ig == a
        gmask = jnp.logical_or(gmask, pick)
        gs = jnp.where(pick, neg, gs)
    emask = jnp.broadcast_to(gmask.reshape(N_GROUPS, 1, tm), (N_GROUPS, GROUP_SIZE, tm)).reshape(e, tm)

    cand = jnp.where(emask, biased, neg)
    ie = _iota2((e, tm), 0)
    sel_all = jnp.zeros((e, tm), jnp.bool_)
    w_rows, picks = [], []
    for _ in range(TOP_K):
        _, a = _first_max(cand, ie, e, 0)
        pick = ie == a
        picks.append(pick)
        w_rows.append(jnp.sum(jnp.where(pick, scores, 0.0), axis=0, keepdims=True))
        sel_all = jnp.logical_or(sel_all, pick)
        cand = jnp.where(pick, neg, cand)
    w_sum = w_rows[0]
    for wr in w_rows[1:]:
        w_sum = w_sum + wr
    wts = jnp.concatenate(w_rows, axis=0) / w_sum * ROUTED_SCALE

    sel = sel_all.astype(BF16)
    in_expert = _dot(sel, upper_ref[...])
    n_tile = jnp.sum(sel_all.astype(F32), axis=1, keepdims=True)
    lower = (_iota2((e, e), 1) < _iota2((e, e), 0)).astype(BF16)
    expert_off = _dot_exact_lhs(lower, jnp.broadcast_to(n_tile, (e, LANES)))[:, 0:1]
    place = in_expert + expert_off
    pos = jnp.concatenate([jnp.sum(jnp.where(pk, place, 0.0), axis=0, keepdims=True) for pk in picks], axis=0)
    pos_ref[...] = pos.astype(I32)
    before_ref[...] = jnp.broadcast_to(cnt_scr[...], before_ref.shape).astype(I32)
    ntile_ref[...] = jnp.broadcast_to(n_tile, ntile_ref.shape).astype(I32)
    cnt_scr[...] = cnt_scr[...] + n_tile
    wts_ref[...] = wts


def _router(x1, g2, sc2, sh2, w_router_t, bias_col, tm):
    s, d = x1.shape
    e = N_EXPERTS
    nt = s // tm
    upper = (jnp.arange(tm)[:, None] < jnp.arange(tm)[None, :]).astype(BF16)
    const = lambda shape: pl.BlockSpec(shape, lambda i: (0, 0))
    per_tile = pl.BlockSpec((None, e, LANES), lambda i: (i, 0, 0))
    return pl.pallas_call(
        functools.partial(_router_kernel, tm=tm),
        grid=(nt,),
        in_specs=[pl.BlockSpec((tm, d), lambda i: (i, 0)),
                  const((1, d)), const((1, d)), const((1, d)),
                  const((e, d)), const((e, 1)), const((tm, tm))],
        out_specs=[pl.BlockSpec((TOP_K, tm), lambda i: (0, i)),
                   pl.BlockSpec((TOP_K, tm), lambda i: (0, i)),
                   per_tile, per_tile],
        out_shape=[jax.ShapeDtypeStruct((TOP_K, s), I32), jax.ShapeDtypeStruct((TOP_K, s), F32),
                   jax.ShapeDtypeStruct((nt, e, LANES), I32), jax.ShapeDtypeStruct((nt, e, LANES), I32)],
        scratch_shapes=[pltpu.VMEM((e, 1), F32)],
        compiler_params=_params(("arbitrary",)),
    )(x1, g2, sc2, sh2, w_router_t, bias_col, upper)


LONG_RUN = 64


def _run_sizes(limit):
    return [1 << b for b in range(limit.bit_length() - 1, -1, -1)]


def _for_each_run(tile, run_refs, tm, make_copy, fn, unroll=False, enable=None):
    run_len_ref, run_off_ref, run_dst_ref = run_refs

    def per_expert(ex, carry):
        n = run_len_ref[tile * N_EXPERTS + ex]
        off = run_off_ref[tile * N_EXPERTS + ex]
        dst = run_dst_ref[tile * N_EXPERTS + ex]
        def pieces(sizes):
            for size in sizes:
                done = n & (-2 * size)

                take = (n & size) != 0
                if enable is not None:
                    take = jnp.logical_and(take, enable)

                @pl.when(take)
                def _(done=done, size=size):
                    fn(make_copy(off + done, dst + done, size))

        sizes = _run_sizes(tm)
        pieces([size for size in sizes if size < LONG_RUN])

        @pl.when(n >= LONG_RUN)
        def _():
            pieces([size for size in sizes if size >= LONG_RUN])

        return carry

    lax.fori_loop(0, N_EXPERTS, per_expert, 0, unroll=unroll)


def _slot_rows(slot, n_slots):
    return pl.ds(pl.multiple_of(slot * SUBLANES, SUBLANES), n_slots * SUBLANES)


def _dispatch_kernel(run_len_ref, run_off_ref, run_dst_ref, pad_lo_ref, pad_hi_ref, pos_ref, h_ref, xs_ref, stage, zero_scr,
                     sem, pad_sem, *, tm, rows_per_pass):
    step = pl.program_id(0)
    na = TOP_K * tm
    d = h_ref.shape[1]
    half = d // 2
    n_words = half // LANES

    def pad_copy(slot, n_slots):
        return pltpu.make_async_copy(zero_scr.at[pl.ds(0, n_slots * SUBLANES), :],
                                     xs_ref.at[_slot_rows(slot, n_slots), :], pad_sem)

    def for_each_pad(fn):
        def per_expert(ex, carry):
            slot = pad_lo_ref[ex]
            n = pad_hi_ref[ex] - slot
            for size in _run_sizes(EXPERT_BLOCK - 1):
                take = (n & size) != 0

                @pl.when(take)
                def _(slot=slot, size=size):
                    fn(pad_copy(slot, size))

                slot = slot + jnp.where(take, size, 0)
            return carry
        lax.fori_loop(0, N_EXPERTS, per_expert, 0)

    @pl.when(step == 0)
    def _():
        zero_scr[...] = jnp.zeros_like(zero_scr)
        for_each_pad(lambda cp: cp.start())

    buf = step % 2
    last = pl.num_programs(0) - 1
    runs = (run_len_ref, run_off_ref, run_dst_ref)

    def run_copy_from(which):
        def run_copy(tile_slot, sorted_slot, n_slots):
            return pltpu.make_async_copy(stage.at[which, _slot_rows(tile_slot, n_slots), :],
                                         xs_ref.at[_slot_rows(sorted_slot, n_slots), :], sem.at[which])
        return run_copy

    def wait_tile(which):
        pltpu.make_async_copy(stage.at[which], xs_ref.at[pl.ds(0, na * SUBLANES), :], sem.at[which]).wait()

    _for_each_run(jnp.maximum(step - 1, 0), runs, tm, run_copy_from(1 - buf), lambda cp: cp.start(),
                  unroll=True, enable=step > 0)

    pos = pos_ref[...]
    h = h_ref[...]
    for a0 in range(0, na, rows_per_pass):
        slot_id = a0 + _iota2((rows_per_pass, tm), 0)
        hit = pos[0:1, :] == slot_id
        for k in range(1, TOP_K):
            hit = jnp.logical_or(hit, pos[k:k + 1, :] == slot_id)
        rows = _dot(hit.astype(BF16), h)
        for i in range(n_words):
            word = _pack_halves(rows[:, i * LANES:(i + 1) * LANES], rows[:, half + i * LANES:half + (i + 1) * LANES])
            stage[buf, pl.ds(a0 * SUBLANES + i, rows_per_pass, stride=SUBLANES), :] = word

    @pl.when(step > 0)
    def _():
        wait_tile(1 - buf)

    @pl.when(step == last)
    def _():
        _for_each_run(step, runs, tm, run_copy_from(buf), lambda cp: cp.start())
        wait_tile(buf)

    @pl.when(step == 0)
    def _():
        for_each_pad(lambda cp: cp.wait())


def _dispatch(runs, pad_lo, pad_hi, pos_t, h2, n_slots, tm):
    s, d = h2.shape
    assert (d // 2) % LANES == 0 and (d // 2) // LANES == SUBLANES, "one token row must pack into one (8, 128) tile"
    na = TOP_K * tm
    return pl.pallas_call(
        functools.partial(_dispatch_kernel, tm=tm, rows_per_pass=min(512, na)),
        grid_spec=pltpu.PrefetchScalarGridSpec(
            num_scalar_prefetch=5,
            grid=(s // tm,),
            in_specs=[pl.BlockSpec((TOP_K, tm), lambda i, *_: (0, i)),
                      pl.BlockSpec((tm, d), lambda i, *_: (i, 0))],
            out_specs=pl.BlockSpec(memory_space=pl.ANY),
            scratch_shapes=[pltpu.VMEM((2, na * SUBLANES, LANES), U32),
                            pltpu.VMEM((EXPERT_BLOCK // 2 * SUBLANES, LANES), U32),
                            pltpu.SemaphoreType.DMA((2,)), pltpu.SemaphoreType.DMA(())]),
        out_shape=jax.ShapeDtypeStruct((n_slots * SUBLANES, LANES), U32),
        compiler_params=_params(("arbitrary",), has_side_effects=True, disable_bounds_checks=True),
    )(*runs, pad_lo, pad_hi, pos_t, h2)


def _expert_kernel(be_ref, nu_ref, next_ref, par_ref, valid_ref, x_ref, wg_hbm, wu_hbm, wd_hbm, y_ref, wg_f32, wu_f32, wd_f32,
                   wg_scr, wu_scr, wd_scr, sem):
    b = pl.program_id(0)
    bm = EXPERT_BLOCK
    active = b < nu_ref[0]
    new_expert = jnp.logical_or(b == 0, be_ref[b] != be_ref[jnp.maximum(b - 1, 0)])

    def weight_copies(ex, which):
        return [pltpu.make_async_copy(src.at[ex], dst.at[which], sem.at[which])
                for src, dst in ((wg_hbm, wg_f32), (wu_hbm, wu_f32), (wd_hbm, wd_f32))]

    @pl.when(jnp.logical_and(active, new_expert))
    def _():
        which = par_ref[b]

        @pl.when(b == 0)
        def _():
            for cp in weight_copies(be_ref[b], which):
                cp.start()

        for cp in weight_copies(be_ref[b], which):
            cp.wait()
        wg_scr[...] = wg_f32[which].astype(BF16)
        wu_scr[...] = wu_f32[which].astype(BF16)
        wd_scr[...] = wd_f32[which].astype(BF16)
        nb = next_ref[b]

        @pl.when(nb < nu_ref[0])
        def _():
            for cp in weight_copies(be_ref[nb], 1 - which):
                cp.start()

    def ffn_rows(rows):
        los, his = [], []
        for i in range(SUBLANES):
            lo, hi = _unpack_halves(x_ref[pl.ds(i, rows, stride=SUBLANES), :])
            los.append(lo.astype(BF16))
            his.append(hi.astype(BF16))
        xb = jnp.concatenate(los + his, axis=1)
        hid = _silu(_dot(xb, wg_scr[...])) * _dot(xb, wu_scr[...])
        y = _dot(hid.astype(BF16), wd_scr[...])
        half = y.shape[1] // 2
        for i in range(SUBLANES):
            word = _pack_halves(_round_bf16(y[:, i * LANES:(i + 1) * LANES]),
                                _round_bf16(y[:, half + i * LANES:half + (i + 1) * LANES]))
            y_ref[pl.ds(i, rows, stride=SUBLANES), :] = word

    half_full = valid_ref[b] <= bm // 2

    @pl.when(jnp.logical_and(active, jnp.logical_not(half_full)))
    def _():
        ffn_rows(bm)

    @pl.when(jnp.logical_and(active, half_full))
    def _():
        ffn_rows(bm // 2)
        y_ref[bm // 2 * SUBLANES:bm * SUBLANES, :] = jnp.zeros((bm // 2 * SUBLANES, LANES), U32)


def _experts(block_e, n_used, next_block, parity, valid, xs, w_gate, w_up, w_down):
    d, ff = w_gate.shape[1], w_gate.shape[2]
    bm = EXPERT_BLOCK
    n_blocks = xs.shape[0] // (bm * SUBLANES)
    blk = lambda b, be, nu, *_: (jnp.minimum(b, nu[0] - 1), 0)
    hbm = pl.BlockSpec(memory_space=pl.ANY)
    return pl.pallas_call(
        _expert_kernel,
        grid_spec=pltpu.PrefetchScalarGridSpec(
            num_scalar_prefetch=5,
            grid=(n_blocks,),
            in_specs=[pl.BlockSpec((bm * SUBLANES, LANES), blk), hbm, hbm, hbm],
            out_specs=pl.BlockSpec((bm * SUBLANES, LANES), blk),
            scratch_shapes=[pltpu.VMEM((2, d, ff), F32), pltpu.VMEM((2, d, ff), F32), pltpu.VMEM((2, ff, d), F32),
                            pltpu.VMEM((d, ff), BF16), pltpu.VMEM((d, ff), BF16), pltpu.VMEM((ff, d), BF16),
                            pltpu.SemaphoreType.DMA((2,))]),
        out_shape=jax.ShapeDtypeStruct(xs.shape, U32),
        compiler_params=_params(("arbitrary",)),
    )(block_e, n_used, next_block, parity, valid, xs, w_gate, w_up, w_down)


def _combine_kernel(run_len_ref, run_off_ref, run_src_ref, ys_ref, h_ref, x1_ref, pos_ref, wts_ref, wg_ref, wu_ref, wd_ref, gt_ref, gf_ref,
                    o_ref, stage, sem, *, tm, rows_per_pass):
    step = pl.program_id(0)
    na = TOP_K * tm
    buf = step % 2

    last = pl.num_programs(0) - 1

    def fetch_tile(tile, which, unroll=False):
        def run_copy(tile_slot, sorted_slot, n_slots):
            return pltpu.make_async_copy(ys_ref.at[_slot_rows(sorted_slot, n_slots), :],
                                         stage.at[which, _slot_rows(tile_slot, n_slots), :], sem.at[which])
        _for_each_run(tile, (run_len_ref, run_off_ref, run_src_ref), tm, run_copy, lambda cp: cp.start(), unroll)

    def wait_tile(which):
        pltpu.make_async_copy(ys_ref.at[pl.ds(0, na * SUBLANES), :], stage.at[which], sem.at[which]).wait()

    @pl.when(step == 0)
    def _():
        fetch_tile(step, buf)

    fetch_tile(jnp.minimum(step + 1, last), 1 - buf, unroll=True)

    hb = h_ref[...]
    hid = _silu(_dot(hb, wg_ref[...])) * _dot(hb, wu_ref[...])
    acc = _dot(hid.astype(BF16), wd_ref[...])

    wait_tile(buf)

    pos = pos_ref[...]
    wts = wts_ref[...]
    for a0 in range(0, na, rows_per_pass):
        los, his = [], []
        for i in range(SUBLANES):
            lo, hi = _unpack_halves(stage[buf, pl.ds(a0 * SUBLANES + i, rows_per_pass, stride=SUBLANES), :])
            los.append(lo.astype(BF16))
            his.append(hi.astype(BF16))
        y_rows = jnp.concatenate(los + his, axis=1)
        slot_id = a0 + _iota2((rows_per_pass, tm), 0)
        wmat = jnp.zeros((rows_per_pass, tm), F32)
        for k in range(TOP_K):
            wmat = wmat + jnp.where(pos[k:k + 1, :] == slot_id, wts[k:k + 1, :], 0.0)
        acc = acc + _dg(wmat.astype(BF16), y_rows, TN)
    x2 = x1_ref[...] + gt_ref[...] * acc
    o_ref[...] = _rms(x2, NORM_EPS) * gf_ref[...]

    @pl.when(step == last)
    def _():
        wait_tile(1 - buf)


def _combine(runs, ys, h2, x1, pos_t, wts_t, w_gate, w_up, w_down, gt2, gf, tm):
    s, d = x1.shape
    ff = w_gate.shape[1]
    na = TOP_K * tm
    const = lambda shape: pl.BlockSpec(shape, lambda i, *_: (0, 0), pipeline_mode=pl.Buffered(1))
    tile = lambda cols: pl.BlockSpec((tm, cols), lambda i, *_: (i, 0))
    per_k = pl.BlockSpec((TOP_K, tm), lambda i, *_: (0, i))
    return pl.pallas_call(
        functools.partial(_combine_kernel, tm=tm, rows_per_pass=min(256, na)),
        grid_spec=pltpu.PrefetchScalarGridSpec(
            num_scalar_prefetch=3,
            grid=(s // tm,),
            in_specs=[pl.BlockSpec(memory_space=pl.ANY),
                      tile(d), tile(d), per_k, per_k,
                      const((d, ff)), const((d, ff)), const((ff, d)), const((1, d)), const((1, d))],
            out_specs=tile(d),
            scratch_shapes=[pltpu.VMEM((2, na * SUBLANES, LANES), U32), pltpu.SemaphoreType.DMA((2,))]),
        out_shape=jax.ShapeDtypeStruct((s, d), F32),
        compiler_params=_params(("arbitrary",), disable_bounds_checks=True),
    )(*runs, ys, h2, x1, pos_t, wts_t, w_gate, w_up, w_down, gt2, gf)


def _mixer(x2d, mod, norm1_g, norm2_g, w_in, lb, hgrn_onorm_g, gdn_conv_w, gdn_a_log, gdn_dt_bias, gdn_onorm_g,
           w_branch_hgrn, w_branch_gdn, w_out, tiles):
    d = x2d.shape[1]
    sh1, sc1, gt1, sh2, sc2, _ = [mod[:, i * d:(i + 1) * d] for i in range(6)]
    key = HEADS * HEAD_DIM
    small0 = 4 * key + 3 * key
    small1 = small0 + 2 * HEADS
    w_in_t = jnp.swapaxes(w_in, 1, 2)
    w_main_t = _wprep(w_in_t, small0, small1, tiles["wprep_tn"])
    w_small_t = w_in_t[0, small0:small1, :].astype(BF16)
    proj, ab_t = _inproj(x2d, norm1_g, sc1, sh1, w_main_t, w_small_t, tiles["in_tm"], tiles["in_tn"])
    o_a = _hgrn(proj, lb, hgrn_onorm_g, tiles["mix_ts"])
    u, wqd, ku, attn, dl = _gdn_prep(proj, gdn_conv_w, ab_t, gdn_a_log, gdn_dt_bias, tiles["prep_ts"])
    o_b = _gdn_scan(u, wqd, ku, attn, dl, proj, gdn_onorm_g, tiles["mix_ts"])
    return _merge(o_a, o_b, proj, x2d, w_branch_hgrn.astype(BF16), w_branch_gdn.astype(BF16),
                  w_out.astype(BF16), gt1, norm2_g, sc2, sh2, tiles["merge_tm"])


def _moe(x1, h2, mod, norm2_g, normf_g, w_router, router_bias, w_exp_gate, w_exp_up, w_exp_down, w_sh_gate,
         w_sh_up, w_sh_down, tiles):
    s, d = x1.shape
    tm = tiles["moe_tm"]
    sh2, sc2, gt2 = [mod[:, i * d:(i + 1) * d] for i in (3, 4, 5)]
    pos_t, wts_t, before, ntile = _router(x1, norm2_g, sc2, sh2, w_router.T, router_bias.reshape(-1, 1), tm)
    bm = EXPERT_BLOCK
    n_blocks = -(-(s * TOP_K + N_EXPERTS * (bm - 1)) // bm)
    before = before[:, :, 0]
    ntile = ntile[:, :, 0]
    counts = before[-1] + ntile[-1]
    padded = (counts + bm - 1) // bm * bm
    earlier = jnp.arange(N_EXPERTS)[None, :] < jnp.arange(N_EXPERTS)[:, None]
    pstart = jnp.sum(jnp.where(earlier, padded[None, :], 0), axis=1).astype(I32)
    pend = pstart + padded
    block_start = jnp.arange(n_blocks, dtype=I32) * bm
    block_e = jnp.minimum(jnp.sum(pend[None, :] <= block_start[:, None], axis=1), N_EXPERTS - 1).astype(I32)
    n_used = pend[-1:] // bm
    run_off = jnp.sum(jnp.where(earlier[None], ntile[:, None, :], 0), axis=2)
    runs = (ntile.reshape(-1), run_off.reshape(-1), (before + pstart[None, :]).reshape(-1))
    xs = _dispatch(runs, pstart + counts, pend, pos_t, h2, n_blocks * bm, tm)
    own = block_e[:, None] == jnp.arange(N_EXPERTS)[None, :]
    next_block = jnp.sum(jnp.where(own, pend[None, :], 0), axis=1) // bm
    switches = jnp.concatenate([jnp.zeros((1,), I32), (block_e[1:] != block_e[:-1]).astype(I32)])
    upto = jnp.arange(n_blocks)[None, :] <= jnp.arange(n_blocks)[:, None]
    parity = jnp.sum(jnp.where(upto, switches[None, :], 0), axis=1).astype(I32) % 2
    start_b = jnp.sum(jnp.where(own, pstart[None, :], 0), axis=1)
    count_b = jnp.sum(jnp.where(own, counts[None, :], 0), axis=1)
    valid = jnp.clip(count_b - (block_start - start_b), 0, bm).astype(I32)
    ys = _experts(block_e, n_used, next_block, parity, valid, xs, w_exp_gate, w_exp_up, w_exp_down)
    return _combine(runs, ys, h2, x1, pos_t, wts_t, w_sh_gate.astype(BF16), w_sh_up.astype(BF16),
                    w_sh_down.astype(BF16), gt2, normf_g, tm)


def _tiles(s):
    pick = lambda want: min(want, s)
    return dict(wprep_tn=512, in_tm=pick(1024), in_tn=1536, mix_ts=pick(512), prep_ts=pick(2048), merge_tm=pick(512),
                moe_tm=pick(256))


def kernel(x, c, w_ada, b_ada, norm1_g, norm2_g, w_in, hgrn_lb_table, hgrn_onorm_g, gdn_conv_w, gdn_a_log, gdn_dt_bias, gdn_onorm_g, w_branch_hgrn, w_branch_gdn, w_out, w_router, router_bias, w_exp_gate, w_exp_up, w_exp_down, w_sh_gate, w_sh_up, w_sh_down, normf_g):
    b, s, d = x.shape
    assert b == 1 and w_ada.shape[0] == 1, "one sequence, one layer"
    tiles = _tiles(s)
    lb = jnp.sum(jax.nn.softmax(hgrn_lb_table.astype(F32), axis=0)[0:1], axis=0, keepdims=True)
    mod = _ada(c, w_ada[0], b_ada[0])
    row = lambda v: v.reshape(1, -1)
    x1, h2 = _mixer(x[0], mod, row(norm1_g[0]), row(norm2_g[0]), w_in, lb, row(hgrn_onorm_g[0]), gdn_conv_w[0],
                    gdn_a_log[0], gdn_dt_bias[0], row(gdn_onorm_g[0]), w_branch_hgrn[0], w_branch_gdn[0], w_out[0],
                    tiles)
    out = _moe(x1, h2, mod, row(norm2_g[0]), row(normf_g), w_router[0], router_bias[0], w_exp_gate[0],
               w_exp_up[0], w_exp_down[0], w_sh_gate[0], w_sh_up[0], w_sh_down[0], tiles)
    return out[None]
```

```python
import functools

import jax
import jax.numpy as jnp
from jax import lax
from jax.experimental import pallas as pl
from jax.experimental.pallas import tpu as pltpu

F32 = jnp.float32
BF16 = jnp.bfloat16
I32 = jnp.int32
U32 = jnp.uint32

NORM_EPS = 1e-6
L2_EPS = 1e-6
HEADS = 8
HEAD_DIM = 128
CONV_WIDTH = 4
CHUNK = 64
N_EXPERTS = 64
N_GROUPS = 8
GROUP_SIZE = N_EXPERTS // N_GROUPS
TOPK_GROUPS = 4
TOP_K = 8
ROUTED_SCALE = 2.5
EXPERT_BLOCK = 512

LANES = 128
SUBLANES = 8
VMEM_LIMIT = 56 * 1024 * 1024

NT = (((1,), (1,)), ((), ()))
TN = (((0,), (0,)), ((), ()))


def _params(sem, **kw):
    return pltpu.CompilerParams(dimension_semantics=sem, vmem_limit_bytes=VMEM_LIMIT, **kw)


def _dot(a, b):
    return jnp.dot(a, b, preferred_element_type=F32)


def _dg(a, b, dims):
    return lax.dot_general(a, b, dims, preferred_element_type=F32)


def _split(x):
    hi = x.astype(BF16)
    lo = (x - hi.astype(F32)).astype(BF16)
    return hi, lo


def _dot_exact_lhs(a_bf16, x, dims=None):
    hi, lo = _split(x)
    if dims is None:
        return _dot(a_bf16, hi) + _dot(a_bf16, lo)
    return _dg(a_bf16, hi, dims) + _dg(a_bf16, lo, dims)


def _sigmoid(x):
    return 1.0 / (1.0 + jnp.exp(-x))


def _silu(x):
    return x * _sigmoid(x)


def _rms(x, eps):
    return x * lax.rsqrt(jnp.mean(x * x, axis=-1, keepdims=True) + eps)


def _iota2(shape, dim):
    return lax.broadcasted_iota(I32, shape, dim)


def _pack_halves(lo, hi):
    lo_bits = lax.shift_right_logical(pltpu.bitcast(lo, U32), U32(16))
    hi_bits = pltpu.bitcast(hi, U32) & U32(0xFFFF0000)
    return lo_bits | hi_bits


def _unpack_halves(word):
    lo = pltpu.bitcast(lax.shift_left(word, U32(16)), F32)
    hi = pltpu.bitcast(word & U32(0xFFFF0000), F32)
    return lo, hi


def _round_bf16(x):
    return x.astype(BF16).astype(F32)


def _ada_kernel(c_ref, w_ref, b_ref, o_ref):
    cond = _silu(c_ref[...])
    o_ref[...] = jnp.sum(w_ref[...] * cond, axis=0, keepdims=True) + b_ref[...]


def _ada(c, w_ada, b_ada):
    d, n = w_ada.shape
    tn = 1024
    return pl.pallas_call(
        _ada_kernel,
        grid=(n // tn,),
        in_specs=[pl.BlockSpec((d, 1), lambda j: (0, 0)),
                  pl.BlockSpec((d, tn), lambda j: (0, j)),
                  pl.BlockSpec((1, tn), lambda j: (0, j))],
        out_specs=pl.BlockSpec((1, tn), lambda j: (0, j)),
        out_shape=jax.ShapeDtypeStruct((1, n), F32),
        compiler_params=_params(("arbitrary",)),
    )(c.reshape(d, 1), w_ada, b_ada.reshape(1, n))


def _wprep_kernel(a_ref, b_ref, o_ref, *, first_shifted, shift):
    j = pl.program_id(0)

    @pl.when(j < first_shifted)
    def _():
        o_ref[...] = a_ref[...].astype(BF16)

    @pl.when(j >= first_shifted)
    def _():
        tn = a_ref.shape[0]
        o_ref[0:tn - shift, :] = a_ref[shift:tn, :].astype(BF16)
        o_ref[tn - shift:tn, :] = b_ref[...].astype(BF16)


def _wprep(w_in_t, cut0, cut1, tn):
    _, n_in, d = w_in_t.shape
    shift = cut1 - cut0
    n_out = n_in - shift
    assert cut0 % tn == 0 and n_out % tn == 0 and tn % shift == 0 and shift % (2 * SUBLANES) == 0
    return pl.pallas_call(
        functools.partial(_wprep_kernel, first_shifted=cut0 // tn, shift=shift),
        grid=(n_out // tn,),
        in_specs=[pl.BlockSpec((None, tn, d), lambda j: (0, j, 0)),
                  pl.BlockSpec((None, shift, d), lambda j: (0, (j + 1) * (tn // shift), 0))],
        out_specs=pl.BlockSpec((tn, d), lambda j: (j, 0)),
        out_shape=jax.ShapeDtypeStruct((n_out, d), BF16),
        compiler_params=_params(("arbitrary",)),
    )(w_in_t, w_in_t)


def _inproj_kernel(x_ref, g_ref, sc_ref, sh_ref, w_ref, wst_ref, proj_ref, smallt_ref, h_scr):
    @pl.when(pl.program_id(1) == 0)
    def _():
        h = _rms(x_ref[...], NORM_EPS) * g_ref[...] * (1.0 + sc_ref[...]) + sh_ref[...]
        hb = h.astype(BF16)
        h_scr[...] = hb
        smallt_ref[...] = _dg(wst_ref[...], hb, NT)

    proj_ref[...] = _dg(h_scr[...], w_ref[...], NT).astype(BF16)


def _inproj(x, g, sc, sh, w_main_t, w_small_t, tm, tn):
    s, d = x.shape
    n = w_main_t.shape[0]
    ns = w_small_t.shape[0]
    row = lambda i, j: (0, 0)
    return pl.pallas_call(
        _inproj_kernel,
        grid=(s // tm, n // tn),
        in_specs=[pl.BlockSpec((tm, d), lambda i, j: (i, 0)),
                  pl.BlockSpec((1, d), row), pl.BlockSpec((1, d), row), pl.BlockSpec((1, d), row),
                  pl.BlockSpec((tn, d), lambda i, j: (j, 0)),
                  pl.BlockSpec((ns, d), row)],
        out_specs=[pl.BlockSpec((tm, tn), lambda i, j: (i, j)),
                   pl.BlockSpec((ns, tm), lambda i, j: (0, i))],
        out_shape=[jax.ShapeDtypeStruct((s, n), BF16), jax.ShapeDtypeStruct((ns, s), F32)],
        scratch_shapes=[pltpu.VMEM((tm, d), BF16)],
        compiler_params=_params(("arbitrary", "arbitrary")),
    )(x, g, sc, sh, w_main_t, w_small_t)


def _hgrn_kernel(q_ref, f_ref, i_ref, g_ref, lb_ref, on_ref, o_ref, st_scr, *, n_chunks):
    @pl.when(pl.program_id(0) == 0)
    def _():
        st_scr[...] = jnp.zeros_like(st_scr)

    c = CHUNK
    hd = HEAD_DIM
    causal = _iota2((c, c), 1) <= _iota2((c, c), 0)
    tri = causal.astype(BF16)
    lb = lb_ref[...]
    on_g = on_ref[...]
    heads = [slice(h * hd, (h + 1) * hd) for h in range(HEADS)]

    def chunk(n, carry):
        rows = pl.ds(pl.multiple_of(n * c, c), c)
        f = lb + (1.0 - lb) * _sigmoid(f_ref[rows, :].astype(F32))
        b = _dot_exact_lhs(tri, jnp.log(f))
        k = 1.0 - f
        q = _silu(q_ref[rows, :].astype(F32)) * (hd ** -0.5)
        v = i_ref[rows, :]
        b_mid = b[c // 2:c // 2 + 1, :]
        b_last = b[c - 1:c, :]
        qa = (q * jnp.exp(b - b_mid)).astype(BF16)
        ka = (k * jnp.exp(b_mid - b)).astype(BF16)
        qi = (q * jnp.exp(b)).astype(BF16)
        ku = (k * jnp.exp(b_last - b)).astype(BF16)
        dec = jnp.exp(b_last)
        gate = on_g * _silu(g_ref[rows, :].astype(F32))
        sts = [st_scr[h] for h in range(HEADS)]
        scores = [jnp.where(causal, _dg(qa[:, sl], ka[:, sl], NT), 0.0).astype(BF16) for sl in heads]
        inter = [_dg(qi[:, sl], st.astype(BF16), NT) for sl, st in zip(heads, sts)]
        kv = [_dg(v[:, sl], ku[:, sl], TN) for sl in heads]
        for h, sl in enumerate(heads):
            st_scr[h] = dec[:, sl] * sts[h] + kv[h]
        outs = [_rms(_dot(sc, v[:, sl]) + it, NORM_EPS) for sc, sl, it in zip(scores, heads, inter)]
        o_ref[rows, :] = (jnp.concatenate(outs, axis=1) * gate).astype(BF16)
        return carry

    lax.fori_loop(0, n_chunks, chunk, 0, unroll=4)


def _hgrn(proj, lb, onorm_g, ts):
    s = proj.shape[0]
    width = HEADS * HEAD_DIM
    col = lambda blk: pl.BlockSpec((ts, width), lambda j, blk=blk: (j, blk))
    const = pl.BlockSpec((1, width), lambda j: (0, 0))
    return pl.pallas_call(
        functools.partial(_hgrn_kernel, n_chunks=ts // CHUNK),
        grid=(s // ts,),
        in_specs=[col(0), col(1), col(2), col(3), const, const],
        out_specs=pl.BlockSpec((ts, width), lambda j: (j, 0)),
        out_shape=jax.ShapeDtypeStruct((s, width), BF16),
        scratch_shapes=[pltpu.VMEM((HEADS, HEAD_DIM, HEAD_DIM), F32)],
        compiler_params=_params(("arbitrary",)),
    )(proj, proj, proj, proj, lb, jnp.tile(onorm_g, (1, HEADS)))


def _gdn_prep_kernel(q_ref, k_ref, v_ref, qp_ref, kp_ref, vp_ref, wq_ref, wk_ref, wv_ref, ab_ref, alog_ref,
                     dtb_ref, tri_ref, eye_ref, u_ref, wqd_ref, ku_ref, attn_ref, dl_ref, cat_scr, rows_scr, cols_scr,
                     *, n_chunks, ts):
    h = pl.program_id(1)
    first = pl.program_id(0) == 0
    c = CHUNK
    hd = HEAD_DIM

    def conv_silu(cur_ref, prev_ref, w_ref):
        cat_scr[0:8, :] = jnp.where(first, 0.0, prev_ref[...].astype(F32))
        cat_scr[8:8 + ts, :] = cur_ref[...].astype(F32)
        acc = None
        for j in range(CONV_WIDTH):
            off = 8 - (CONV_WIDTH - 1) + j
            term = cat_scr[off:off + ts, :] * w_ref[j:j + 1, :]
            acc = term if acc is None else acc + term
        return _silu(acc)

    def l2n(x):
        return x * lax.rsqrt(jnp.sum(x * x, axis=-1, keepdims=True) + L2_EPS)

    q_all = l2n(conv_silu(q_ref, qp_ref, wq_ref)) * (hd ** -0.5)
    k_all = l2n(conv_silu(k_ref, kp_ref, wk_ref))
    v_all = conv_silu(v_ref, vp_ref, wv_ref)

    @pl.when(h == 0)
    def _():
        z = ab_ref[0:HEADS, :] + dtb_ref[...]
        softplus = jnp.maximum(z, 0.0) + jnp.log(1.0 + jnp.exp(-jnp.abs(z)))
        ld_rows = -jnp.exp(alog_ref[...]) * softplus
        hi, lo = _split(ld_rows)
        tri_blocks = tri_ref[...]
        w = tri_blocks.shape[0]
        spans = [slice(t0, t0 + w) for t0 in range(0, ts, w)]
        g_rows = jnp.concatenate([_dg(hi[:, sp], tri_blocks, NT) + _dg(lo[:, sp], tri_blocks, NT) for sp in spans],
                                 axis=1)
        beta_rows = _sigmoid(ab_ref[HEADS:2 * HEADS, :])
        rows_scr[...] = g_rows
        rows = jnp.concatenate([g_rows, beta_rows, jnp.zeros((LANES - 2 * HEADS, ts), F32)], axis=0)
        r_hi, r_lo = _split(rows)
        r_lo2 = (rows - r_hi.astype(F32) - r_lo.astype(F32)).astype(BF16)
        eye_w = eye_ref[...]
        for sp in spans:
            cols_scr[sp, :] = _dg(eye_w, r_hi[:, sp], NT) + _dg(eye_w, r_lo[:, sp], NT) + _dg(eye_w, r_lo2[:, sp], NT)

    lane = _iota2((ts, LANES), 1)
    cols = cols_scr[...]
    gc_all = jnp.sum(jnp.where(lane == h, cols, 0.0), axis=1, keepdims=True)
    bc_all = jnp.sum(jnp.where(lane == h + HEADS, cols, 0.0), axis=1, keepdims=True)
    g_row = rows_scr[pl.ds(h, 1), :]
    egc_all = jnp.exp(gc_all)

    r = _iota2((c, c), 0)
    cidx = _iota2((c, c), 1)
    causal = cidx <= r
    strict = cidx < r
    eye_f = (r == cidx).astype(F32)
    chunks = [slice(n * c, (n + 1) * c) for n in range(n_chunks)]

    q16 = q_all.astype(BF16)
    k16 = k_all.astype(BF16)
    kq = [_dg(jnp.concatenate([k16[sl], q16[sl]], axis=0), k16[sl], NT) for sl in chunks]
    dm = []
    for sl in chunks:
        diff = gc_all[sl] - g_row[:, sl]
        dm.append(jnp.where(causal, jnp.exp(jnp.where(causal, diff, 0.0)), 0.0))
    bm = [-jnp.where(strict, bc_all[sl] * x[0:c] * d, 0.0) for sl, x, d in zip(chunks, kq, dm)]
    p = [eye_f + b for b in bm]
    bm = [_dot(b.astype(BF16), b.astype(BF16)) for b in bm]
    for _ in range(c.bit_length() - 3):
        res = [_dot(b.astype(BF16), jnp.concatenate([b, pp], axis=1).astype(BF16)) for b, pp in zip(bm, p)]
        p = [pp + x[:, c:2 * c] for pp, x in zip(p, res)]
        bm = [x[:, 0:c] for x in res]
    p = [pp + _dot(b.astype(BF16), pp.astype(BF16)) for b, pp in zip(bm, p)]
    rhs = jnp.concatenate([v_all * bc_all, k_all * (bc_all * egc_all)], axis=1).astype(BF16)
    sol = [_dot(pp.astype(BF16), rhs[sl]) for pp, sl in zip(p, chunks)]
    qd_all = (q_all * egc_all).astype(BF16)
    for n, sl in enumerate(chunks):
        g_last = gc_all[(n + 1) * c - 1:(n + 1) * c, :]
        u_ref[sl, :] = sol[n][:, 0:hd].astype(BF16)
        wqd_ref[2 * n * c:(2 * n + 1) * c, :] = sol[n][:, hd:2 * hd].astype(BF16)
        wqd_ref[(2 * n + 1) * c:(2 * n + 2) * c, :] = qd_all[sl]
        ku_ref[sl, :] = (k_all[sl] * jnp.exp(g_last - gc_all[sl])).astype(BF16)
        attn_ref[sl, :] = (kq[n][c:2 * c] * dm[n]).astype(BF16)
        dl_ref[n:n + 1, :] = jnp.broadcast_to(jnp.exp(g_last), (1, hd))


def _gdn_prep(proj, conv_w, ab_t, a_log, dt_bias, ts):
    s = proj.shape[0]
    hd = HEAD_DIM
    c = CHUNK
    q0 = 4 * HEADS
    cur = lambda off: pl.BlockSpec((ts, hd), lambda j, h, off=off: (j, off + h))
    prev = lambda off: pl.BlockSpec((8, hd), lambda j, h, off=off: (jnp.maximum(j * (ts // 8) - 1, 0), off + h))
    cw = lambda off: pl.BlockSpec((CONV_WIDTH, hd), lambda j, h, off=off: (0, off + h))
    per_head_scalar = pl.BlockSpec((HEADS, 1), lambda j, h: (0, 0))
    w = min(ts, 2 * LANES)
    const = pl.BlockSpec((w, w), lambda j, h: (0, 0))
    pos = jnp.arange(w)
    tri_blocks = ((pos[:, None] // c == pos[None, :] // c) & (pos[None, :] <= pos[:, None])).astype(BF16)
    eye = (pos[:, None] == pos[None, :]).astype(BF16)
    per_head = lambda rows, cols: pl.BlockSpec((None, rows, cols), lambda j, h: (h, j, 0))
    return pl.pallas_call(
        functools.partial(_gdn_prep_kernel, n_chunks=ts // c, ts=ts),
        grid=(s // ts, HEADS),
        in_specs=[cur(q0), cur(q0 + HEADS), cur(q0 + 2 * HEADS),
                  prev(q0), prev(q0 + HEADS), prev(q0 + 2 * HEADS),
                  cw(0), cw(HEADS), cw(2 * HEADS),
                  pl.BlockSpec((2 * HEADS, ts), lambda j, h: (0, j)),
                  per_head_scalar, per_head_scalar, const, const],
        out_specs=[pl.BlockSpec((ts, hd), lambda j, h: (j, h)),
                   pl.BlockSpec((2 * ts, hd), lambda j, h: (j, h)),
                   pl.BlockSpec((ts, hd), lambda j, h: (j, h)),
                   per_head(ts, c),
                   per_head(ts // c, hd)],
        out_shape=[jax.ShapeDtypeStruct((s, HEADS * hd), BF16),
                   jax.ShapeDtypeStruct((2 * s, HEADS * hd), BF16),
                   jax.ShapeDtypeStruct((s, HEADS * hd), BF16),
                   jax.ShapeDtypeStruct((HEADS, s, c), BF16),
                   jax.ShapeDtypeStruct((HEADS, s // c, hd), F32)],
        scratch_shapes=[pltpu.VMEM((ts + 8, hd), F32), pltpu.VMEM((HEADS, ts), F32), pltpu.VMEM((ts, LANES), F32)],
        compiler_params=_params(("arbitrary", "arbitrary")),
    )(proj, proj, proj, proj, proj, proj, conv_w, conv_w, conv_w,
      ab_t, a_log.reshape(HEADS, 1), dt_bias.reshape(HEADS, 1), tri_blocks, eye)


def _gdn_scan_kernel(u_ref, wqd_ref, ku_ref, attn_ref, dl_ref, g_ref, on_ref, o_ref, st_scr, *, n_chunks):
    @pl.when(pl.program_id(0) == 0)
    def _():
        st_scr[...] = jnp.zeros_like(st_scr)

    c = CHUNK
    hd = HEAD_DIM
    on_g = on_ref[...]
    heads = [slice(h * hd, (h + 1) * hd) for h in range(HEADS)]

    def chunk(n, carry):
        rows = pl.ds(pl.multiple_of(n * c, c), c)
        rows2 = pl.ds(pl.multiple_of(2 * n * c, 2 * c), 2 * c)
        sts = [st_scr[h] for h in range(HEADS)]
        wq = [_dot(wqd_ref[rows2, sl], st.astype(BF16)) for sl, st in zip(heads, sts)]
        vn = [(u_ref[rows, sl].astype(F32) - x[0:c]).astype(BF16) for sl, x in zip(heads, wq)]
        upd = [_dg(ku_ref[rows, sl], v, TN) for sl, v in zip(heads, vn)]
        for h in range(HEADS):
            st_scr[h] = dl_ref[h, pl.ds(n, 1), :] * sts[h] + upd[h]
        outs = [_rms(x[c:2 * c] + _dot(attn_ref[h, rows, :], v), NORM_EPS)
                for h, (x, v) in enumerate(zip(wq, vn))]
        gate = jnp.tile(on_g, (1, HEADS)) * _silu(g_ref[rows, :].astype(F32))
        o_ref[rows, :] = (jnp.concatenate(outs, axis=1) * gate).astype(BF16)
        return carry

    lax.fori_loop(0, n_chunks, chunk, 0, unroll=4)


def _gdn_scan(u, wqd, ku, attn, dl, proj, onorm_g, ts):
    s, width = u.shape
    c = CHUNK
    gate_blk = (4 * HEADS + 3 * HEADS) * HEAD_DIM // width
    return pl.pallas_call(
        functools.partial(_gdn_scan_kernel, n_chunks=ts // c),
        grid=(s // ts,),
        in_specs=[pl.BlockSpec((ts, width), lambda j: (j, 0)),
                  pl.BlockSpec((2 * ts, width), lambda j: (j, 0)),
                  pl.BlockSpec((ts, width), lambda j: (j, 0)),
                  pl.BlockSpec((HEADS, ts, c), lambda j: (0, j, 0)),
                  pl.BlockSpec((HEADS, ts // c, HEAD_DIM), lambda j: (0, j, 0)),
                  pl.BlockSpec((ts, width), lambda j: (j, gate_blk)),
                  pl.BlockSpec((1, HEAD_DIM), lambda j: (0, 0))],
        out_specs=pl.BlockSpec((ts, width), lambda j: (j, 0)),
        out_shape=jax.ShapeDtypeStruct((s, width), BF16),
        scratch_shapes=[pltpu.VMEM((HEADS, HEAD_DIM, HEAD_DIM), F32)],
        compiler_params=_params(("arbitrary",)),
    )(u, wqd, ku, attn, dl, proj, onorm_g)


def _merge_kernel(oa_ref, ob_ref, mga_ref, mgb_ref, x_ref, wa_ref, wb_ref, wo_ref, gt_ref, g2_ref, sc_ref,
                  sh_ref, x1_ref, h2_ref):
    ya = _dot(oa_ref[...], wa_ref[...])
    yb = _dot(ob_ref[...], wb_ref[...])
    merged = _sigmoid(mga_ref[...].astype(F32)) * ya + _sigmoid(mgb_ref[...].astype(F32)) * yb
    x1 = x_ref[...] + gt_ref[...] * _dot(merged.astype(BF16), wo_ref[...])
    x1_ref[...] = x1
    h2 = _rms(x1, NORM_EPS) * g2_ref[...] * (1.0 + sc_ref[...]) + sh_ref[...]
    h2_ref[...] = h2.astype(BF16)


def _merge(o_a, o_b, proj, x, w_a, w_b, w_o, gt1, g2, sc2, sh2, tm):
    s, d = x.shape
    dv = o_a.shape[1]
    mg0 = (8 * HEADS * HEAD_DIM) // d
    const = lambda shape: pl.BlockSpec(shape, lambda i: (0, 0), pipeline_mode=pl.Buffered(1))
    return pl.pallas_call(
        _merge_kernel,
        grid=(s // tm,),
        in_specs=[pl.BlockSpec((tm, dv), lambda i: (i, 0)),
                  pl.BlockSpec((tm, dv), lambda i: (i, 0)),
                  pl.BlockSpec((tm, d), lambda i: (i, mg0)),
                  pl.BlockSpec((tm, d), lambda i: (i, mg0 + 1)),
                  pl.BlockSpec((tm, d), lambda i: (i, 0)),
                  const((dv, d)), const((dv, d)), const((d, d)),
                  const((1, d)), const((1, d)), const((1, d)), const((1, d))],
        out_specs=[pl.BlockSpec((tm, d), lambda i: (i, 0)), pl.BlockSpec((tm, d), lambda i: (i, 0))],
        out_shape=[jax.ShapeDtypeStruct((s, d), F32), jax.ShapeDtypeStruct((s, d), BF16)],
        compiler_params=_params(("arbitrary",)),
    )(o_a, o_b, proj, proj, x, w_a, w_b, w_o, gt1, g2, sc2, sh2)


def _first_max(vals, iota, size, axis):
    m = jnp.max(vals, axis=axis, keepdims=True)
    idx = jnp.min(jnp.where(vals == m, iota, size), axis=axis, keepdims=True)
    return m, idx


def _router_kernel(x1_ref, g2_ref, sc_ref, sh_ref, wrt_ref, bias_ref, upper_ref, pos_ref, wts_ref, before_ref,
                   ntile_ref, cnt_scr, *, tm):
    @pl.when(pl.program_id(0) == 0)
    def _():
        cnt_scr[...] = jnp.zeros_like(cnt_scr)

    e = N_EXPERTS
    h2 = _rms(x1_ref[...], NORM_EPS) * g2_ref[...] * (1.0 + sc_ref[...]) + sh_ref[...]
    logits = lax.dot_general(wrt_ref[...], h2, NT, preferred_element_type=F32,
                             precision=lax.Precision.HIGHEST)
    scores = _sigmoid(logits)
    biased = scores + bias_ref[...]
    neg = -jnp.inf

    g3 = biased.reshape(N_GROUPS, GROUP_SIZE, tm)
    i3 = lax.broadcasted_iota(I32, g3.shape, 1)
    m1, a1 = _first_max(g3, i3, GROUP_SIZE, 1)
    m2 = jnp.max(jnp.where(i3 == a1, neg, g3), axis=1, keepdims=True)
    gs = (m1 + m2).reshape(N_GROUPS, tm)
    ig = _iota2(gs.shape, 0)
    gmask = jnp.zeros(gs.shape, jnp.bool_)
    for _ in range(TOPK_GROUPS):
        _, a = _first_max(gs, ig, N_GROUPS, 0)
        pick = ig == a
        gmask = jnp.logical_or(gmask, pick)
        gs = jnp.where(pick, neg, gs)
    emask = jnp.broadcast_to(gmask.reshape(N_GROUPS, 1, tm), (N_GROUPS, GROUP_SIZE, tm)).reshape(e, tm)

    cand = jnp.where(emask, biased, neg)
    ie = _iota2((e, tm), 0)
    sel_all = jnp.zeros((e, tm), jnp.bool_)
    w_rows, picks = [], []
    for _ in range(TOP_K):
        _, a = _first_max(cand, ie, e, 0)
        pick = ie == a
        picks.append(pick)
        w_rows.append(jnp.sum(jnp.where(pick, scores, 0.0), axis=0, keepdims=True))
        sel_all = jnp.logical_or(sel_all, pick)
        cand = jnp.where(pick, neg, cand)
    w_sum = w_rows[0]
    for wr in w_rows[1:]:
        w_sum = w_sum + wr
    wts = jnp.concatenate(w_rows, axis=0) / w_sum * ROUTED_SCALE

    sel = sel_all.astype(BF16)
    in_expert = _dot(sel, upper_ref[...])
    n_tile = jnp.sum(sel_all.astype(F32), axis=1, keepdims=True)
    lower = (_iota2((e, e), 1) < _iota2((e, e), 0)).astype(BF16)
    expert_off = _dot_exact_lhs(lower, jnp.broadcast_to(n_tile, (e, LANES)))[:, 0:1]
    place = in_expert + expert_off
    pos = jnp.concatenate([jnp.sum(jnp.where(pk, place, 0.0), axis=0, keepdims=True) for pk in picks], axis=0)
    pos_ref[...] = pos.astype(I32)
    before_ref[...] = jnp.broadcast_to(cnt_scr[...], before_ref.shape).astype(I32)
    ntile_ref[...] = jnp.broadcast_to(n_tile, ntile_ref.shape).astype(I32)
    cnt_scr[...] = cnt_scr[...] + n_tile
    wts_ref[...] = wts


def _router(x1, g2, sc2, sh2, w_router_t, bias_col, tm):
    s, d = x1.shape
    e = N_EXPERTS
    nt = s // tm
    upper = (jnp.arange(tm)[:, None] < jnp.arange(tm)[None, :]).astype(BF16)
    const = lambda shape: pl.BlockSpec(shape, lambda i: (0, 0))
    per_tile = pl.BlockSpec((None, e, LANES), lambda i: (i, 0, 0))
    return pl.pallas_call(
        functools.partial(_router_kernel, tm=tm),
        grid=(nt,),
        in_specs=[pl.BlockSpec((tm, d), lambda i: (i, 0)),
                  const((1, d)), const((1, d)), const((1, d)),
                  const((e, d)), const((e, 1)), const((tm, tm))],
        out_specs=[pl.BlockSpec((TOP_K, tm), lambda i: (0, i)),
                   pl.BlockSpec((TOP_K, tm), lambda i: (0, i)),
                   per_tile, per_tile],
        out_shape=[jax.ShapeDtypeStruct((TOP_K, s), I32), jax.ShapeDtypeStruct((TOP_K, s), F32),
                   jax.ShapeDtypeStruct((nt, e, LANES), I32), jax.ShapeDtypeStruct((nt, e, LANES), I32)],
        scratch_shapes=[pltpu.VMEM((e, 1), F32)],
        compiler_params=_params(("arbitrary",)),
    )(x1, g2, sc2, sh2, w_router_t, bias_col, upper)


LONG_RUN = 64


def _run_sizes(limit):
    return [1 << b for b in range(limit.bit_length() - 1, -1, -1)]


def _for_each_run(tile, run_refs, tm, make_copy, fn, unroll=False, enable=None):
    run_len_ref, run_off_ref, run_dst_ref = run_refs

    def per_expert(ex, carry):
        n = run_len_ref[tile * N_EXPERTS + ex]
        off = run_off_ref[tile * N_EXPERTS + ex]
        dst = run_dst_ref[tile * N_EXPERTS + ex]
        def pieces(sizes):
            for size in sizes:
                done = n & (-2 * size)

                take = (n & size) != 0
                if enable is not None:
                    take = jnp.logical_and(take, enable)

                @pl.when(take)
                def _(done=done, size=size):
                    fn(make_copy(off + done, dst + done, size))

        sizes = _run_sizes(tm)
        pieces([size for size in sizes if size < LONG_RUN])

        @pl.when(n >= LONG_RUN)
        def _():
            pieces([size for size in sizes if size >= LONG_RUN])

        return carry

    lax.fori_loop(0, N_EXPERTS, per_expert, 0, unroll=unroll)


def _slot_rows(slot, n_slots):
    return pl.ds(pl.multiple_of(slot * SUBLANES, SUBLANES), n_slots * SUBLANES)


def _dispatch_kernel(run_len_ref, run_off_ref, run_dst_ref, pad_lo_ref, pad_hi_ref, pos_ref, h_ref, xs_ref, stage, zero_scr,
                     sem, pad_sem, *, tm, rows_per_pass):
    step = pl.program_id(0)
    na = TOP_K * tm
    d = h_ref.shape[1]
    half = d // 2
    n_words = half // LANES

    def pad_copy(slot, n_slots):
        return pltpu.make_async_copy(zero_scr.at[pl.ds(0, n_slots * SUBLANES), :],
                                     xs_ref.at[_slot_rows(slot, n_slots), :], pad_sem)

    def for_each_pad(fn):
        def per_expert(ex, carry):
            slot = pad_lo_ref[ex]
            n = pad_hi_ref[ex] - slot
            for size in _run_sizes(EXPERT_BLOCK - 1):
                take = (n & size) != 0

                @pl.when(take)
                def _(slot=slot, size=size):
                    fn(pad_copy(slot, size))

                slot = slot + jnp.where(take, size, 0)
            return carry
        lax.fori_loop(0, N_EXPERTS, per_expert, 0)

    @pl.when(step == 0)
    def _():
        zero_scr[...] = jnp.zeros_like(zero_scr)
        for_each_pad(lambda cp: cp.start())

    n_buf = stage.shape[0]
    buf = step % n_buf
    prev = (step + n_buf - 1) % n_buf
    oldest = (step + n_buf - 2) % n_buf
    last = pl.num_programs(0) - 1
    runs = (run_len_ref, run_off_ref, run_dst_ref)

    def run_copy_from(which):
        def run_copy(tile_slot, sorted_slot, n_slots):
            return pltpu.make_async_copy(stage.at[which, _slot_rows(tile_slot, n_slots), :],
                                         xs_ref.at[_slot_rows(sorted_slot, n_slots), :], sem.at[which])
        return run_copy

    def wait_tile(which):
        pltpu.make_async_copy(stage.at[which], xs_ref.at[pl.ds(0, na * SUBLANES), :], sem.at[which]).wait()

    _for_each_run(jnp.maximum(step - 1, 0), runs, tm, run_copy_from(prev), lambda cp: cp.start(),
                  unroll=True, enable=step > 0)

    pos = pos_ref[...]
    h = h_ref[...]
    for a0 in range(0, na, rows_per_pass):
        slot_id = a0 + _iota2((rows_per_pass, tm), 0)
        hit = pos[0:1, :] == slot_id
        for k in range(1, TOP_K):
            hit = jnp.logical_or(hit, pos[k:k + 1, :] == slot_id)
        rows = _dot(hit.astype(BF16), h)
        for i in range(n_words):
            word = _pack_halves(rows[:, i * LANES:(i + 1) * LANES], rows[:, half + i * LANES:half + (i + 1) * LANES])
            stage[buf, pl.ds(a0 * SUBLANES + i, rows_per_pass, stride=SUBLANES), :] = word

    @pl.when(step > 1)
    def _():
        wait_tile(oldest)

    @pl.when(step == last)
    def _():
        _for_each_run(step, runs, tm, run_copy_from(buf), lambda cp: cp.start())

        @pl.when(step > 0)
        def _():
            wait_tile(prev)

        wait_tile(buf)

    @pl.when(step == 0)
    def _():
        for_each_pad(lambda cp: cp.wait())


def _dispatch(runs, pad_lo, pad_hi, pos_t, h2, n_slots, tm):
    s, d = h2.shape
    assert (d // 2) % LANES == 0 and (d // 2) // LANES == SUBLANES, "one token row must pack into one (8, 128) tile"
    na = TOP_K * tm
    return pl.pallas_call(
        functools.partial(_dispatch_kernel, tm=tm, rows_per_pass=min(512, na)),
        grid_spec=pltpu.PrefetchScalarGridSpec(
            num_scalar_prefetch=5,
            grid=(s // tm,),
            in_specs=[pl.BlockSpec((TOP_K, tm), lambda i, *_: (0, i)),
                      pl.BlockSpec((tm, d), lambda i, *_: (i, 0))],
            out_specs=pl.BlockSpec(memory_space=pl.ANY),
            scratch_shapes=[pltpu.VMEM((3, na * SUBLANES, LANES), U32),
                            pltpu.VMEM((EXPERT_BLOCK // 2 * SUBLANES, LANES), U32),
                            pltpu.SemaphoreType.DMA((3,)), pltpu.SemaphoreType.DMA(())]),
        out_shape=jax.ShapeDtypeStruct((n_slots * SUBLANES, LANES), U32),
        compiler_params=_params(("arbitrary",), has_side_effects=True, disable_bounds_checks=True),
    )(*runs, pad_lo, pad_hi, pos_t, h2)


def _expert_kernel(be_ref, nu_ref, next_ref, par_ref, valid_ref, x_ref, wg_hbm, wu_hbm, wd_hbm, y_ref, wg_f32, wu_f32, wd_f32,
                   wg_scr, wu_scr, wd_scr, sem):
    b = pl.program_id(0)
    bm = EXPERT_BLOCK
    active = b < nu_ref[0]
    new_expert = jnp.logical_or(b == 0, be_ref[b] != be_ref[jnp.maximum(b - 1, 0)])

    def weight_copies(ex, which):
        return [pltpu.make_async_copy(src.at[ex], dst.at[which], sem.at[which])
                for src, dst in ((wg_hbm, wg_f32), (wu_hbm, wu_f32), (wd_hbm, wd_f32))]

    @pl.when(jnp.logical_and(active, new_expert))
    def _():
        which = par_ref[b]

        @pl.when(b == 0)
        def _():
            for cp in weight_copies(be_ref[b], which):
                cp.start()

        for cp in weight_copies(be_ref[b], which):
            cp.wait()
        wg_scr[...] = wg_f32[which].astype(BF16)
        wu_scr[...] = wu_f32[which].astype(BF16)
        wd_scr[...] = wd_f32[which].astype(BF16)
        nb = next_ref[b]

        @pl.when(nb < nu_ref[0])
        def _():
            for cp in weight_copies(be_ref[nb], 1 - which):
                cp.start()

    def ffn_rows(rows):
        los, his = [], []
        for i in range(SUBLANES):
            lo, hi = _unpack_halves(x_ref[pl.ds(i, rows, stride=SUBLANES), :])
            los.append(lo.astype(BF16))
            his.append(hi.astype(BF16))
        xb = jnp.concatenate(los + his, axis=1)
        hid = _silu(_dot(xb, wg_scr[...])) * _dot(xb, wu_scr[...])
        y = _dot(hid.astype(BF16), wd_scr[...])
        half = y.shape[1] // 2
        for i in range(SUBLANES):
            word = _pack_halves(_round_bf16(y[:, i * LANES:(i + 1) * LANES]),
                                _round_bf16(y[:, half + i * LANES:half + (i + 1) * LANES]))
            y_ref[pl.ds(i, rows, stride=SUBLANES), :] = word

    half_full = valid_ref[b] <= bm // 2

    @pl.when(jnp.logical_and(active, jnp.logical_not(half_full)))
    def _():
        ffn_rows(bm)

    @pl.when(jnp.logical_and(active, half_full))
    def _():
        ffn_rows(bm // 2)
        y_ref[bm // 2 * SUBLANES:bm * SUBLANES, :] = jnp.zeros((bm // 2 * SUBLANES, LANES), U32)


def _experts(block_e, n_used, next_block, parity, valid, xs, w_gate, w_up, w_down):
    d, ff = w_gate.shape[1], w_gate.shape[2]
    bm = EXPERT_BLOCK
    n_blocks = xs.shape[0] // (bm * SUBLANES)
    blk = lambda b, be, nu, *_: (jnp.minimum(b, nu[0] - 1), 0)
    hbm = pl.BlockSpec(memory_space=pl.ANY)
    return pl.pallas_call(
        _expert_kernel,
        grid_spec=pltpu.PrefetchScalarGridSpec(
            num_scalar_prefetch=5,
            grid=(n_blocks,),
            in_specs=[pl.BlockSpec((bm * SUBLANES, LANES), blk), hbm, hbm, hbm],
            out_specs=pl.BlockSpec((bm * SUBLANES, LANES), blk),
            scratch_shapes=[pltpu.VMEM((2, d, ff), F32), pltpu.VMEM((2, d, ff), F32), pltpu.VMEM((2, ff, d), F32),
                            pltpu.VMEM((d, ff), BF16), pltpu.VMEM((d, ff), BF16), pltpu.VMEM((ff, d), BF16),
                            pltpu.SemaphoreType.DMA((2,))]),
        out_shape=jax.ShapeDtypeStruct(xs.shape, U32),
        compiler_params=_params(("arbitrary",)),
    )(block_e, n_used, next_block, parity, valid, xs, w_gate, w_up, w_down)


def _combine_kernel(run_len_ref, run_off_ref, run_src_ref, ys_ref, h_ref, x1_ref, pos_ref, wts_ref, wg_ref, wu_ref, wd_ref, gt_ref, gf_ref,
                    o_ref, stage, sem, *, tm, rows_per_pass):
    step = pl.program_id(0)
    na = TOP_K * tm
    buf = step % 2

    last = pl.num_programs(0) - 1

    def fetch_tile(tile, which, unroll=False):
        def run_copy(tile_slot, sorted_slot, n_slots):
            return pltpu.make_async_copy(ys_ref.at[_slot_rows(sorted_slot, n_slots), :],
                                         stage.at[which, _slot_rows(tile_slot, n_slots), :], sem.at[which])
        _for_each_run(tile, (run_len_ref, run_off_ref, run_src_ref), tm, run_copy, lambda cp: cp.start(), unroll)

    def wait_tile(which):
        pltpu.make_async_copy(ys_ref.at[pl.ds(0, na * SUBLANES), :], stage.at[which], sem.at[which]).wait()

    @pl.when(step == 0)
    def _():
        fetch_tile(step, buf)

    fetch_tile(jnp.minimum(step + 1, last), 1 - buf, unroll=True)

    hb = h_ref[...]
    hid = _silu(_dot(hb, wg_ref[...])) * _dot(hb, wu_ref[...])
    acc = _dot(hid.astype(BF16), wd_ref[...])

    wait_tile(buf)

    pos = pos_ref[...]
    wts = wts_ref[...]
    for a0 in range(0, na, rows_per_pass):
        los, his = [], []
        for i in range(SUBLANES):
            lo, hi = _unpack_halves(stage[buf, pl.ds(a0 * SUBLANES + i, rows_per_pass, stride=SUBLANES), :])
            los.append(lo.astype(BF16))
            his.append(hi.astype(BF16))
        y_rows = jnp.concatenate(los + his, axis=1)
        slot_id = a0 + _iota2((rows_per_pass, tm), 0)
        wmat = jnp.zeros((rows_per_pass, tm), F32)
        for k in range(TOP_K):
            wmat = wmat + jnp.where(pos[k:k + 1, :] == slot_id, wts[k:k + 1, :], 0.0)
        acc = acc + _dg(wmat.astype(BF16), y_rows, TN)
    x2 = x1_ref[...] + gt_ref[...] * acc
    o_ref[...] = _rms(x2, NORM_EPS) * gf_ref[...]

    @pl.when(step == last)
    def _():
        wait_tile(1 - buf)


def _combine(runs, ys, h2, x1, pos_t, wts_t, w_gate, w_up, w_down, gt2, gf, tm):
    s, d = x1.shape
    ff = w_gate.shape[1]
    na = TOP_K * tm
    const = lambda shape: pl.BlockSpec(shape, lambda i, *_: (0, 0), pipeline_mode=pl.Buffered(1))
    tile = lambda cols: pl.BlockSpec((tm, cols), lambda i, *_: (i, 0))
    per_k = pl.BlockSpec((TOP_K, tm), lambda i, *_: (0, i))
    return pl.pallas_call(
        functools.partial(_combine_kernel, tm=tm, rows_per_pass=min(256, na)),
        grid_spec=pltpu.PrefetchScalarGridSpec(
            num_scalar_prefetch=3,
            grid=(s // tm,),
            in_specs=[pl.BlockSpec(memory_space=pl.ANY),
                      tile(d), tile(d), per_k, per_k,
                      const((d, ff)), const((d, ff)), const((ff, d)), const((1, d)), const((1, d))],
            out_specs=tile(d),
            scratch_shapes=[pltpu.VMEM((2, na * SUBLANES, LANES), U32), pltpu.SemaphoreType.DMA((2,))]),
        out_shape=jax.ShapeDtypeStruct((s, d), F32),
        compiler_params=_params(("arbitrary",), disable_bounds_checks=True),
    )(*runs, ys, h2, x1, pos_t, wts_t, w_gate, w_up, w_down, gt2, gf)


def _mixer(x2d, mod, norm1_g, norm2_g, w_in, lb, hgrn_onorm_g, gdn_conv_w, gdn_a_log, gdn_dt_bias, gdn_onorm_g,
           w_branch_hgrn, w_branch_gdn, w_out, tiles):
    d = x2d.shape[1]
    sh1, sc1, gt1, sh2, sc2, _ = [mod[:, i * d:(i + 1) * d] for i in range(6)]
    key = HEADS * HEAD_DIM
    small0 = 4 * key + 3 * key
    small1 = small0 + 2 * HEADS
    w_in_t = jnp.swapaxes(w_in, 1, 2)
    w_main_t = _wprep(w_in_t, small0, small1, tiles["wprep_tn"])
    w_small_t = w_in_t[0, small0:small1, :].astype(BF16)
    proj, ab_t = _inproj(x2d, norm1_g, sc1, sh1, w_main_t, w_small_t, tiles["in_tm"], tiles["in_tn"])
    o_a = _hgrn(proj, lb, hgrn_onorm_g, tiles["mix_ts"])
    u, wqd, ku, attn, dl = _gdn_prep(proj, gdn_conv_w, ab_t, gdn_a_log, gdn_dt_bias, tiles["prep_ts"])
    o_b = _gdn_scan(u, wqd, ku, attn, dl, proj, gdn_onorm_g, tiles["mix_ts"])
    return _merge(o_a, o_b, proj, x2d, w_branch_hgrn.astype(BF16), w_branch_gdn.astype(BF16),
                  w_out.astype(BF16), gt1, norm2_g, sc2, sh2, tiles["merge_tm"])


def _moe(x1, h2, mod, norm2_g, normf_g, w_router, router_bias, w_exp_gate, w_exp_up, w_exp_down, w_sh_gate,
         w_sh_up, w_sh_down, tiles):
    s, d = x1.shape
    tm = tiles["moe_tm"]
    sh2, sc2, gt2 = [mod[:, i * d:(i + 1) * d] for i in (3, 4, 5)]
    pos_t, wts_t, before, ntile = _router(x1, norm2_g, sc2, sh2, w_router.T, router_bias.reshape(-1, 1), tm)
    bm = EXPERT_BLOCK
    n_blocks = -(-(s * TOP_K + N_EXPERTS * (bm - 1)) // bm)
    before = before[:, :, 0]
    ntile = ntile[:, :, 0]
    counts = before[-1] + ntile[-1]
    padded = (counts + bm - 1) // bm * bm
    earlier = jnp.arange(N_EXPERTS)[None, :] < jnp.arange(N_EXPERTS)[:, None]
    pstart = jnp.sum(jnp.where(earlier, padded[None, :], 0), axis=1).astype(I32)
    pend = pstart + padded
    block_start = jnp.arange(n_blocks, dtype=I32) * bm
    block_e = jnp.minimum(jnp.sum(pend[None, :] <= block_start[:, None], axis=1), N_EXPERTS - 1).astype(I32)
    n_used = pend[-1:] // bm
    run_off = jnp.sum(jnp.where(earlier[None], ntile[:, None, :], 0), axis=2)
    runs = (ntile.reshape(-1), run_off.reshape(-1), (before + pstart[None, :]).reshape(-1))
    xs = _dispatch(runs, pstart + counts, pend, pos_t, h2, n_blocks * bm, tm)
    own = block_e[:, None] == jnp.arange(N_EXPERTS)[None, :]
    next_block = jnp.sum(jnp.where(own, pend[None, :], 0), axis=1) // bm
    switches = jnp.concatenate([jnp.zeros((1,), I32), (block_e[1:] != block_e[:-1]).astype(I32)])
    upto = jnp.arange(n_blocks)[None, :] <= jnp.arange(n_blocks)[:, None]
    parity = jnp.sum(jnp.where(upto, switches[None, :], 0), axis=1).astype(I32) % 2
    start_b = jnp.sum(jnp.where(own, pstart[None, :], 0), axis=1)
    count_b = jnp.sum(jnp.where(own, counts[None, :], 0), axis=1)
    valid = jnp.clip(count_b - (block_start - start_b), 0, bm).astype(I32)
    ys = _experts(block_e, n_used, next_block, parity, valid, xs, w_exp_gate, w_exp_up, w_exp_down)
    return _combine(runs, ys, h2, x1, pos_t, wts_t, w_sh_gate.astype(BF16), w_sh_up.astype(BF16),
                    w_sh_down.astype(BF16), gt2, normf_g, tm)


def _tiles(s):
    pick = lambda want: min(want, s)
    return dict(wprep_tn=512, in_tm=pick(1024), in_tn=1536, mix_ts=pick(512), prep_ts=pick(2048), merge_tm=pick(512),
                moe_tm=pick(256))


def kernel(x, c, w_ada, b_ada, norm1_g, norm2_g, w_in, hgrn_lb_table, hgrn_onorm_g, gdn_conv_w, gdn_a_log, gdn_dt_bias, gdn_onorm_g, w_branch_hgrn, w_branch_gdn, w_out, w_router, router_bias, w_exp_gate, w_exp_up, w_exp_down, w_sh_gate, w_sh_up, w_sh_down, normf_g):
    b, s, d = x.shape
    assert b == 1 and w_ada.shape[0] == 1, "one sequence, one layer"
    tiles = _tiles(s)
    lb = jnp.sum(jax.nn.softmax(hgrn_lb_table.astype(F32), axis=0)[0:1], axis=0, keepdims=True)
    mod = _ada(c, w_ada[0], b_ada[0])
    row = lambda v: v.reshape(1, -1)
    x1, h2 = _mixer(x[0], mod, row(norm1_g[0]), row(norm2_g[0]), w_in, lb, row(hgrn_onorm_g[0]), gdn_conv_w[0],
                    gdn_a_log[0], gdn_dt_bias[0], row(gdn_onorm_g[0]), w_branch_hgrn[0], w_branch_gdn[0], w_out[0],
                    tiles)
    out = _moe(x1, h2, mod, row(norm2_g[0]), row(normf_g), w_router[0], router_bias[0], w_exp_gate[0],
               w_exp_up[0], w_exp_down[0], w_sh_gate[0], w_sh_up[0], w_sh_down[0], tiles)
    return out[None]
```

```python
import functools

import jax
import jax.numpy as jnp
from jax import lax
from jax.experimental import pallas as pl
from jax.experimental.pallas import tpu as pltpu

F32 = jnp.float32
BF16 = jnp.bfloat16
I32 = jnp.int32
U32 = jnp.uint32

NORM_EPS = 1e-6
L2_EPS = 1e-6
HEADS = 8
HEAD_DIM = 128
CONV_WIDTH = 4
CHUNK = 64
N_EXPERTS = 64
N_GROUPS = 8
GROUP_SIZE = N_EXPERTS // N_GROUPS
TOPK_GROUPS = 4
TOP_K = 8
ROUTED_SCALE = 2.5
EXPERT_BLOCK = 512

LANES = 128
SUBLANES = 8
MXU_DIM = 256
VMEM_LIMIT = 56 * 1024 * 1024

NT = (((1,), (1,)), ((), ()))
TN = (((0,), (0,)), ((), ()))


def _params(sem, **kw):
    return pltpu.CompilerParams(dimension_semantics=sem, vmem_limit_bytes=VMEM_LIMIT, **kw)


def _dot(a, b):
    return jnp.dot(a, b, preferred_element_type=F32)


def _dg(a, b, dims):
    return lax.dot_general(a, b, dims, preferred_element_type=F32)


def _split(x):
    hi = x.astype(BF16)
    lo = (x - hi.astype(F32)).astype(BF16)
    return hi, lo


def _dot_exact_lhs(a_bf16, x, dims=None):
    hi, lo = _split(x)
    if dims is None:
        return _dot(a_bf16, hi) + _dot(a_bf16, lo)
    return _dg(a_bf16, hi, dims) + _dg(a_bf16, lo, dims)


def _sigmoid(x):
    return 1.0 / (1.0 + jnp.exp(-x))


def _silu(x):
    return x * _sigmoid(x)


def _rms(x, eps):
    return x * lax.rsqrt(jnp.mean(x * x, axis=-1, keepdims=True) + eps)


def _iota2(shape, dim):
    return lax.broadcasted_iota(I32, shape, dim)


def _pack_halves(lo, hi):
    lo_bits = lax.shift_right_logical(pltpu.bitcast(lo, U32), U32(16))
    hi_bits = pltpu.bitcast(hi, U32) & U32(0xFFFF0000)
    return lo_bits | hi_bits


def _unpack_halves(word):
    lo = pltpu.bitcast(lax.shift_left(word, U32(16)), F32)
    hi = pltpu.bitcast(word & U32(0xFFFF0000), F32)
    return lo, hi


def _round_bf16(x):
    return x.astype(BF16).astype(F32)


def _ada_kernel(c_ref, w_ref, b_ref, o_ref):
    cond = _silu(c_ref[...])
    o_ref[...] = jnp.sum(w_ref[...] * cond, axis=0, keepdims=True) + b_ref[...]


def _ada(c, w_ada, b_ada, tn):
    d, n = w_ada.shape
    return pl.pallas_call(
        _ada_kernel,
        grid=(n // tn,),
        in_specs=[pl.BlockSpec((d, 1), lambda j: (0, 0)),
                  pl.BlockSpec((d, tn), lambda j: (0, j)),
                  pl.BlockSpec((1, tn), lambda j: (0, j))],
        out_specs=pl.BlockSpec((1, tn), lambda j: (0, j)),
        out_shape=jax.ShapeDtypeStruct((1, n), F32),
        compiler_params=_params(("arbitrary",)),
    )(c.reshape(d, 1), w_ada, b_ada.reshape(1, n))


def _wprep_kernel(a_ref, b_ref, o_ref, *, first_shifted, shift):
    j = pl.program_id(0)

    @pl.when(j < first_shifted)
    def _():
        o_ref[...] = a_ref[...].astype(BF16)

    @pl.when(j >= first_shifted)
    def _():
        tn = a_ref.shape[0]
        o_ref[0:tn - shift, :] = a_ref[shift:tn, :].astype(BF16)
        o_ref[tn - shift:tn, :] = b_ref[...].astype(BF16)


def _wprep(w_in_t, cut0, cut1, tn):
    _, n_in, d = w_in_t.shape
    shift = cut1 - cut0
    n_out = n_in - shift
    assert cut0 % tn == 0 and n_out % tn == 0 and tn % shift == 0 and shift % (2 * SUBLANES) == 0
    return pl.pallas_call(
        functools.partial(_wprep_kernel, first_shifted=cut0 // tn, shift=shift),
        grid=(n_out // tn,),
        in_specs=[pl.BlockSpec((None, tn, d), lambda j: (0, j, 0)),
                  pl.BlockSpec((None, shift, d), lambda j: (0, (j + 1) * (tn // shift), 0))],
        out_specs=pl.BlockSpec((tn, d), lambda j: (j, 0)),
        out_shape=jax.ShapeDtypeStruct((n_out, d), BF16),
        compiler_params=_params(("arbitrary",)),
    )(w_in_t, w_in_t)


def _inproj_kernel(x_ref, g_ref, sc_ref, sh_ref, w_ref, wst_ref, proj_ref, smallt_ref, h_scr):
    @pl.when(pl.program_id(1) == 0)
    def _():
        h = _rms(x_ref[...], NORM_EPS) * g_ref[...] * (1.0 + sc_ref[...]) + sh_ref[...]
        hb = h.astype(BF16)
        h_scr[...] = hb
        smallt_ref[...] = _dg(wst_ref[...], hb, NT)

    proj_ref[...] = _dg(h_scr[...], w_ref[...], NT).astype(BF16)


def _inproj(x, g, sc, sh, w_main_t, w_small_t, tm, tn):
    s, d = x.shape
    n = w_main_t.shape[0]
    ns = w_small_t.shape[0]
    row = lambda i, j: (0, 0)
    return pl.pallas_call(
        _inproj_kernel,
        grid=(s // tm, n // tn),
        in_specs=[pl.BlockSpec((tm, d), lambda i, j: (i, 0)),
                  pl.BlockSpec((1, d), row), pl.BlockSpec((1, d), row), pl.BlockSpec((1, d), row),
                  pl.BlockSpec((tn, d), lambda i, j: (j, 0)),
                  pl.BlockSpec((ns, d), row)],
        out_specs=[pl.BlockSpec((tm, tn), lambda i, j: (i, j)),
                   pl.BlockSpec((ns, tm), lambda i, j: (0, i))],
        out_shape=[jax.ShapeDtypeStruct((s, n), BF16), jax.ShapeDtypeStruct((ns, s), F32)],
        scratch_shapes=[pltpu.VMEM((tm, d), BF16)],
        compiler_params=_params(("arbitrary", "arbitrary")),
    )(x, g, sc, sh, w_main_t, w_small_t)


def _hgrn_kernel(q_ref, f_ref, i_ref, g_ref, lb_ref, on_ref, o_ref, st_scr, *, n_chunks):
    @pl.when(pl.program_id(0) == 0)
    def _():
        st_scr[...] = jnp.zeros_like(st_scr)

    c = CHUNK
    hd = HEAD_DIM
    causal = _iota2((c, c), 1) <= _iota2((c, c), 0)
    tri = causal.astype(BF16)
    lb = lb_ref[...]
    on_g = on_ref[...]
    heads = [slice(h * hd, (h + 1) * hd) for h in range(HEADS)]

    def chunk(n, carry):
        rows = pl.ds(pl.multiple_of(n * c, c), c)
        f = lb + (1.0 - lb) * _sigmoid(f_ref[rows, :].astype(F32))
        b = _dot_exact_lhs(tri, jnp.log(f))
        k = 1.0 - f
        q = _silu(q_ref[rows, :].astype(F32)) * (hd ** -0.5)
        v = i_ref[rows, :]
        b_mid = b[c // 2:c // 2 + 1, :]
        b_last = b[c - 1:c, :]
        qa = (q * jnp.exp(b - b_mid)).astype(BF16)
        ka = (k * jnp.exp(b_mid - b)).astype(BF16)
        qi = (q * jnp.exp(b)).astype(BF16)
        ku = (k * jnp.exp(b_last - b)).astype(BF16)
        dec = jnp.exp(b_last)
        gate = on_g * _silu(g_ref[rows, :].astype(F32))
        sts = [st_scr[h] for h in range(HEADS)]
        scores = [jnp.where(causal, _dg(qa[:, sl], ka[:, sl], NT), 0.0).astype(BF16) for sl in heads]
        inter = [_dg(qi[:, sl], st.astype(BF16), NT) for sl, st in zip(heads, sts)]
        kv = [_dg(v[:, sl], ku[:, sl], TN) for sl in heads]
        for h, sl in enumerate(heads):
            st_scr[h] = dec[:, sl] * sts[h] + kv[h]
        outs = [_rms(_dot(sc, v[:, sl]) + it, NORM_EPS) for sc, sl, it in zip(scores, heads, inter)]
        o_ref[rows, :] = (jnp.concatenate(outs, axis=1) * gate).astype(BF16)
        return carry

    lax.fori_loop(0, n_chunks, chunk, 0, unroll=4)


def _hgrn(proj, lb, onorm_g, ts):
    s = proj.shape[0]
    width = HEADS * HEAD_DIM
    col = lambda blk: pl.BlockSpec((ts, width), lambda j, blk=blk: (j, blk))
    const = pl.BlockSpec((1, width), lambda j: (0, 0))
    return pl.pallas_call(
        functools.partial(_hgrn_kernel, n_chunks=ts // CHUNK),
        grid=(s // ts,),
        in_specs=[col(0), col(1), col(2), col(3), const, const],
        out_specs=pl.BlockSpec((ts, width), lambda j: (j, 0)),
        out_shape=jax.ShapeDtypeStruct((s, width), BF16),
        scratch_shapes=[pltpu.VMEM((HEADS, HEAD_DIM, HEAD_DIM), F32)],
        compiler_params=_params(("arbitrary",)),
    )(proj, proj, proj, proj, lb, jnp.tile(onorm_g, (1, HEADS)))


def _gdn_prep_kernel(q_ref, k_ref, v_ref, qp_ref, kp_ref, vp_ref, wq_ref, wk_ref, wv_ref, ab_ref, alog_ref,
                     dtb_ref, tri_ref, eye_ref, u_ref, wqd_ref, ku_ref, attn_ref, dl_ref, cat_scr, rows_scr, cols_scr,
                     *, n_chunks, ts):
    h = pl.program_id(1)
    first = pl.program_id(0) == 0
    c = CHUNK
    hd = HEAD_DIM

    def conv_silu(cur_ref, prev_ref, w_ref):
        cat_scr[0:SUBLANES, :] = jnp.where(first, 0.0, prev_ref[...].astype(F32))
        cat_scr[SUBLANES:SUBLANES + ts, :] = cur_ref[...].astype(F32)
        acc = None
        for j in range(CONV_WIDTH):
            off = SUBLANES - (CONV_WIDTH - 1) + j
            term = cat_scr[off:off + ts, :] * w_ref[j:j + 1, :]
            acc = term if acc is None else acc + term
        return _silu(acc)

    def l2n(x):
        return x * lax.rsqrt(jnp.sum(x * x, axis=-1, keepdims=True) + L2_EPS)

    q_all = l2n(conv_silu(q_ref, qp_ref, wq_ref)) * (hd ** -0.5)
    k_all = l2n(conv_silu(k_ref, kp_ref, wk_ref))
    v_all = conv_silu(v_ref, vp_ref, wv_ref)

    @pl.when(h == 0)
    def _():
        z = ab_ref[0:HEADS, :] + dtb_ref[...]
        softplus = jnp.maximum(z, 0.0) + jnp.log(1.0 + jnp.exp(-jnp.abs(z)))
        ld_rows = -jnp.exp(alog_ref[...]) * softplus
        hi, lo = _split(ld_rows)
        tri_blocks = tri_ref[...]
        w = tri_blocks.shape[0]
        spans = [slice(t0, t0 + w) for t0 in range(0, ts, w)]
        g_rows = jnp.concatenate([_dg(hi[:, sp], tri_blocks, NT) + _dg(lo[:, sp], tri_blocks, NT) for sp in spans],
                                 axis=1)
        beta_rows = _sigmoid(ab_ref[HEADS:2 * HEADS, :])
        rows_scr[...] = g_rows
        rows = jnp.concatenate([g_rows, beta_rows, jnp.zeros((LANES - 2 * HEADS, ts), F32)], axis=0)
        r_hi, r_lo = _split(rows)
        r_lo2 = (rows - r_hi.astype(F32) - r_lo.astype(F32)).astype(BF16)
        eye_w = eye_ref[...]
        for sp in spans:
            cols_scr[sp, :] = _dg(eye_w, r_hi[:, sp], NT) + _dg(eye_w, r_lo[:, sp], NT) + _dg(eye_w, r_lo2[:, sp], NT)

    lane = _iota2((ts, LANES), 1)
    cols = cols_scr[...]
    gc_all = jnp.sum(jnp.where(lane == h, cols, 0.0), axis=1, keepdims=True)
    bc_all = jnp.sum(jnp.where(lane == h + HEADS, cols, 0.0), axis=1, keepdims=True)
    g_row = rows_scr[pl.ds(h, 1), :]
    egc_all = jnp.exp(gc_all)

    r = _iota2((c, c), 0)
    cidx = _iota2((c, c), 1)
    causal = cidx <= r
    strict = cidx < r
    eye_f = (r == cidx).astype(F32)
    chunks = [slice(n * c, (n + 1) * c) for n in range(n_chunks)]

    q16 = q_all.astype(BF16)
    k16 = k_all.astype(BF16)
    kq = [_dg(jnp.concatenate([k16[sl], q16[sl]], axis=0), k16[sl], NT) for sl in chunks]
    dm = []
    for sl in chunks:
        diff = gc_all[sl] - g_row[:, sl]
        dm.append(jnp.where(causal, jnp.exp(jnp.where(causal, diff, 0.0)), 0.0))
    bm = [-jnp.where(strict, bc_all[sl] * x[0:c] * d, 0.0) for sl, x, d in zip(chunks, kq, dm)]
    p = [eye_f + b for b in bm]
    bm = [_dot(b.astype(BF16), b.astype(BF16)) for b in bm]
    for _ in range(c.bit_length() - 3):
        res = [_dot(b.astype(BF16), jnp.concatenate([b, pp], axis=1).astype(BF16)) for b, pp in zip(bm, p)]
        p = [pp + x[:, c:2 * c] for pp, x in zip(p, res)]
        bm = [x[:, 0:c] for x in res]
    p = [pp + _dot(b.astype(BF16), pp.astype(BF16)) for b, pp in zip(bm, p)]
    rhs = jnp.concatenate([v_all * bc_all, k_all * (bc_all * egc_all)], axis=1).astype(BF16)
    sol = [_dot(pp.astype(BF16), rhs[sl]) for pp, sl in zip(p, chunks)]
    qd_all = (q_all * egc_all).astype(BF16)
    for n, sl in enumerate(chunks):
        g_last = gc_all[(n + 1) * c - 1:(n + 1) * c, :]
        u_ref[sl, :] = sol[n][:, 0:hd].astype(BF16)
        wqd_ref[2 * n * c:(2 * n + 1) * c, :] = sol[n][:, hd:2 * hd].astype(BF16)
        wqd_ref[(2 * n + 1) * c:(2 * n + 2) * c, :] = qd_all[sl]
        ku_ref[sl, :] = (k_all[sl] * jnp.exp(g_last - gc_all[sl])).astype(BF16)
        attn_ref[sl, :] = (kq[n][c:2 * c] * dm[n]).astype(BF16)
        dl_ref[n:n + 1, :] = jnp.broadcast_to(jnp.exp(g_last), (1, hd))


def _gdn_prep(proj, conv_w, ab_t, a_log, dt_bias, ts):
    s = proj.shape[0]
    hd = HEAD_DIM
    c = CHUNK
    q0 = 4 * HEADS
    cur = lambda off: pl.BlockSpec((ts, hd), lambda j, h, off=off: (j, off + h))
    prev = lambda off: pl.BlockSpec((SUBLANES, hd),
                                    lambda j, h, off=off: (jnp.maximum(j * (ts // SUBLANES) - 1, 0), off + h))
    cw = lambda off: pl.BlockSpec((CONV_WIDTH, hd), lambda j, h, off=off: (0, off + h))
    per_head_scalar = pl.BlockSpec((HEADS, 1), lambda j, h: (0, 0))
    w = min(ts, MXU_DIM)
    const = pl.BlockSpec((w, w), lambda j, h: (0, 0))
    pos = jnp.arange(w)
    tri_blocks = ((pos[:, None] // c == pos[None, :] // c) & (pos[None, :] <= pos[:, None])).astype(BF16)
    eye = (pos[:, None] == pos[None, :]).astype(BF16)
    per_head = lambda rows, cols: pl.BlockSpec((None, rows, cols), lambda j, h: (h, j, 0))
    return pl.pallas_call(
        functools.partial(_gdn_prep_kernel, n_chunks=ts // c, ts=ts),
        grid=(s // ts, HEADS),
        in_specs=[cur(q0), cur(q0 + HEADS), cur(q0 + 2 * HEADS),
                  prev(q0), prev(q0 + HEADS), prev(q0 + 2 * HEADS),
                  cw(0), cw(HEADS), cw(2 * HEADS),
                  pl.BlockSpec((2 * HEADS, ts), lambda j, h: (0, j)),
                  per_head_scalar, per_head_scalar, const, const],
        out_specs=[pl.BlockSpec((ts, hd), lambda j, h: (j, h)),
                   pl.BlockSpec((2 * ts, hd), lambda j, h: (j, h)),
                   pl.BlockSpec((ts, hd), lambda j, h: (j, h)),
                   per_head(ts, c),
                   per_head(ts // c, hd)],
        out_shape=[jax.ShapeDtypeStruct((s, HEADS * hd), BF16),
                   jax.ShapeDtypeStruct((2 * s, HEADS * hd), BF16),
                   jax.ShapeDtypeStruct((s, HEADS * hd), BF16),
                   jax.ShapeDtypeStruct((HEADS, s, c), BF16),
                   jax.ShapeDtypeStruct((HEADS, s // c, hd), F32)],
        scratch_shapes=[pltpu.VMEM((ts + SUBLANES, hd), F32), pltpu.VMEM((HEADS, ts), F32),
                        pltpu.VMEM((ts, LANES), F32)],
        compiler_params=_params(("arbitrary", "arbitrary")),
    )(proj, proj, proj, proj, proj, proj, conv_w, conv_w, conv_w,
      ab_t, a_log.reshape(HEADS, 1), dt_bias.reshape(HEADS, 1), tri_blocks, eye)


def _gdn_scan_kernel(u_ref, wqd_ref, ku_ref, attn_ref, dl_ref, g_ref, on_ref, o_ref, st_scr, *, n_chunks):
    @pl.when(pl.program_id(0) == 0)
    def _():
        st_scr[...] = jnp.zeros_like(st_scr)

    c = CHUNK
    hd = HEAD_DIM
    on_g = on_ref[...]
    heads = [slice(h * hd, (h + 1) * hd) for h in range(HEADS)]

    def chunk(n, carry):
        rows = pl.ds(pl.multiple_of(n * c, c), c)
        rows2 = pl.ds(pl.multiple_of(2 * n * c, 2 * c), 2 * c)
        sts = [st_scr[h] for h in range(HEADS)]
        wq = [_dot(wqd_ref[rows2, sl], st.astype(BF16)) for sl, st in zip(heads, sts)]
        vn = [(u_ref[rows, sl].astype(F32) - x[0:c]).astype(BF16) for sl, x in zip(heads, wq)]
        upd = [_dg(ku_ref[rows, sl], v, TN) for sl, v in zip(heads, vn)]
        for h in range(HEADS):
            st_scr[h] = dl_ref[h, pl.ds(n, 1), :] * sts[h] + upd[h]
        outs = [_rms(x[c:2 * c] + _dot(attn_ref[h, rows, :], v), NORM_EPS)
                for h, (x, v) in enumerate(zip(wq, vn))]
        gate = jnp.tile(on_g, (1, HEADS)) * _silu(g_ref[rows, :].astype(F32))
        o_ref[rows, :] = (jnp.concatenate(outs, axis=1) * gate).astype(BF16)
        return carry

    lax.fori_loop(0, n_chunks, chunk, 0, unroll=4)


def _gdn_scan(u, wqd, ku, attn, dl, proj, onorm_g, ts):
    s, width = u.shape
    c = CHUNK
    gate_blk = (4 * HEADS + 3 * HEADS) * HEAD_DIM // width
    return pl.pallas_call(
        functools.partial(_gdn_scan_kernel, n_chunks=ts // c),
        grid=(s // ts,),
        in_specs=[pl.BlockSpec((ts, width), lambda j: (j, 0)),
                  pl.BlockSpec((2 * ts, width), lambda j: (j, 0)),
                  pl.BlockSpec((ts, width), lambda j: (j, 0)),
                  pl.BlockSpec((HEADS, ts, c), lambda j: (0, j, 0)),
                  pl.BlockSpec((HEADS, ts // c, HEAD_DIM), lambda j: (0, j, 0)),
                  pl.BlockSpec((ts, width), lambda j: (j, gate_blk)),
                  pl.BlockSpec((1, HEAD_DIM), lambda j: (0, 0))],
        out_specs=pl.BlockSpec((ts, width), lambda j: (j, 0)),
        out_shape=jax.ShapeDtypeStruct((s, width), BF16),
        scratch_shapes=[pltpu.VMEM((HEADS, HEAD_DIM, HEAD_DIM), F32)],
        compiler_params=_params(("arbitrary",)),
    )(u, wqd, ku, attn, dl, proj, onorm_g)


def _merge_kernel(oa_ref, ob_ref, mga_ref, mgb_ref, x_ref, wa_ref, wb_ref, wo_ref, gt_ref, g2_ref, sc_ref,
                  sh_ref, x1_ref, h2_ref):
    ya = _dot(oa_ref[...], wa_ref[...])
    yb = _dot(ob_ref[...], wb_ref[...])
    merged = _sigmoid(mga_ref[...].astype(F32)) * ya + _sigmoid(mgb_ref[...].astype(F32)) * yb
    x1 = x_ref[...] + gt_ref[...] * _dot(merged.astype(BF16), wo_ref[...])
    x1_ref[...] = x1
    h2 = _rms(x1, NORM_EPS) * g2_ref[...] * (1.0 + sc_ref[...]) + sh_ref[...]
    h2_ref[...] = h2.astype(BF16)


def _merge(o_a, o_b, proj, x, w_a, w_b, w_o, gt1, g2, sc2, sh2, tm):
    s, d = x.shape
    dv = o_a.shape[1]
    mg0 = (8 * HEADS * HEAD_DIM) // d
    const = lambda shape: pl.BlockSpec(shape, lambda i: (0, 0), pipeline_mode=pl.Buffered(1))
    return pl.pallas_call(
        _merge_kernel,
        grid=(s // tm,),
        in_specs=[pl.BlockSpec((tm, dv), lambda i: (i, 0)),
                  pl.BlockSpec((tm, dv), lambda i: (i, 0)),
                  pl.BlockSpec((tm, d), lambda i: (i, mg0)),
                  pl.BlockSpec((tm, d), lambda i: (i, mg0 + 1)),
                  pl.BlockSpec((tm, d), lambda i: (i, 0)),
                  const((dv, d)), const((dv, d)), const((d, d)),
                  const((1, d)), const((1, d)), const((1, d)), const((1, d))],
        out_specs=[pl.BlockSpec((tm, d), lambda i: (i, 0)), pl.BlockSpec((tm, d), lambda i: (i, 0))],
        out_shape=[jax.ShapeDtypeStruct((s, d), F32), jax.ShapeDtypeStruct((s, d), BF16)],
        compiler_params=_params(("arbitrary",)),
    )(o_a, o_b, proj, proj, x, w_a, w_b, w_o, gt1, g2, sc2, sh2)


def _first_max(vals, iota, size, axis):
    m = jnp.max(vals, axis=axis, keepdims=True)
    idx = jnp.min(jnp.where(vals == m, iota, size), axis=axis, keepdims=True)
    return m, idx


def _router_kernel(x1_ref, g2_ref, sc_ref, sh_ref, wrt_ref, bias_ref, upper_ref, pos_ref, wts_ref, before_ref,
                   ntile_ref, cnt_scr, *, tm):
    @pl.when(pl.program_id(0) == 0)
    def _():
        cnt_scr[...] = jnp.zeros_like(cnt_scr)

    e = N_EXPERTS
    h2 = _rms(x1_ref[...], NORM_EPS) * g2_ref[...] * (1.0 + sc_ref[...]) + sh_ref[...]
    logits = lax.dot_general(wrt_ref[...], h2, NT, preferred_element_type=F32,
                             precision=lax.Precision.HIGHEST)
    scores = _sigmoid(logits)
    biased = scores + bias_ref[...]
    neg = -jnp.inf

    g3 = biased.reshape(N_GROUPS, GROUP_SIZE, tm)
    i3 = lax.broadcasted_iota(I32, g3.shape, 1)
    m1, a1 = _first_max(g3, i3, GROUP_SIZE, 1)
    m2 = jnp.max(jnp.where(i3 == a1, neg, g3), axis=1, keepdims=True)
    gs = (m1 + m2).reshape(N_GROUPS, tm)
    ig = _iota2(gs.shape, 0)
    gmask = jnp.zeros(gs.shape, jnp.bool_)
    for _ in range(TOPK_GROUPS):
        _, a = _first_max(gs, ig, N_GROUPS, 0)
        pick = ig == a
        gmask = jnp.logical_or(gmask, pick)
        gs = jnp.where(pick, neg, gs)
    emask = jnp.broadcast_to(gmask.reshape(N_GROUPS, 1, tm), (N_GROUPS, GROUP_SIZE, tm)).reshape(e, tm)

    cand = jnp.where(emask, biased, neg)
    ie = _iota2((e, tm), 0)
    sel_all = jnp.zeros((e, tm), jnp.bool_)
    w_rows, picks = [], []
    for _ in range(TOP_K):
        _, a = _first_max(cand, ie, e, 0)
        pick = ie == a
        picks.append(pick)
        w_rows.append(jnp.sum(jnp.where(pick, scores, 0.0), axis=0, keepdims=True))
        sel_all = jnp.logical_or(sel_all, pick)
        cand = jnp.where(pick, neg, cand)
    w_sum = w_rows[0]
    for wr in w_rows[1:]:
        w_sum = w_sum + wr
    wts = jnp.concatenate(w_rows, axis=0) / w_sum * ROUTED_SCALE

    sel = sel_all.astype(BF16)
    in_expert = _dot(sel, upper_ref[...])
    n_tile = jnp.sum(sel_all.astype(F32), axis=1, keepdims=True)
    lower = (_iota2((e, e), 1) < _iota2((e, e), 0)).astype(BF16)
    expert_off = _dot_exact_lhs(lower, jnp.broadcast_to(n_tile, (e, LANES)))[:, 0:1]
    place = in_expert + expert_off
    pos = jnp.concatenate([jnp.sum(jnp.where(pk, place, 0.0), axis=0, keepdims=True) for pk in picks], axis=0)
    pos_ref[...] = pos.astype(I32)
    before_ref[...] = jnp.broadcast_to(cnt_scr[...], before_ref.shape).astype(I32)
    ntile_ref[...] = jnp.broadcast_to(n_tile, ntile_ref.shape).astype(I32)
    cnt_scr[...] = cnt_scr[...] + n_tile
    wts_ref[...] = wts


def _router(x1, g2, sc2, sh2, w_router_t, bias_col, tm):
    s, d = x1.shape
    e = N_EXPERTS
    nt = s // tm
    upper = (jnp.arange(tm)[:, None] < jnp.arange(tm)[None, :]).astype(BF16)
    const = lambda shape: pl.BlockSpec(shape, lambda i: (0, 0))
    per_tile = pl.BlockSpec((None, e, LANES), lambda i: (i, 0, 0))
    return pl.pallas_call(
        functools.partial(_router_kernel, tm=tm),
        grid=(nt,),
        in_specs=[pl.BlockSpec((tm, d), lambda i: (i, 0)),
                  const((1, d)), const((1, d)), const((1, d)),
                  const((e, d)), const((e, 1)), const((tm, tm))],
        out_specs=[pl.BlockSpec((TOP_K, tm), lambda i: (0, i)),
                   pl.BlockSpec((TOP_K, tm), lambda i: (0, i)),
                   per_tile, per_tile],
        out_shape=[jax.ShapeDtypeStruct((TOP_K, s), I32), jax.ShapeDtypeStruct((TOP_K, s), F32),
                   jax.ShapeDtypeStruct((nt, e, LANES), I32), jax.ShapeDtypeStruct((nt, e, LANES), I32)],
        scratch_shapes=[pltpu.VMEM((e, 1), F32)],
        compiler_params=_params(("arbitrary",)),
    )(x1, g2, sc2, sh2, w_router_t, bias_col, upper)


LONG_RUN = 64


def _run_sizes(limit):
    return [1 << b for b in range(limit.bit_length() - 1, -1, -1)]


def _for_each_run(tile, run_refs, tm, make_copy, fn, unroll=False, enable=None):
    run_len_ref, run_off_ref, run_dst_ref = run_refs

    def per_expert(ex, carry):
        n = run_len_ref[tile * N_EXPERTS + ex]
        off = run_off_ref[tile * N_EXPERTS + ex]
        dst = run_dst_ref[tile * N_EXPERTS + ex]
        def pieces(sizes):
            for size in sizes:
                done = n & (-2 * size)

                take = (n & size) != 0
                if enable is not None:
                    take = jnp.logical_and(take, enable)

                @pl.when(take)
                def _(done=done, size=size):
                    fn(make_copy(off + done, dst + done, size))

        sizes = _run_sizes(tm)
        pieces([size for size in sizes if size < LONG_RUN])

        @pl.when(n >= LONG_RUN)
        def _():
            pieces([size for size in sizes if size >= LONG_RUN])

        return carry

    lax.fori_loop(0, N_EXPERTS, per_expert, 0, unroll=unroll)


def _slot_rows(slot, n_slots):
    return pl.ds(pl.multiple_of(slot * SUBLANES, SUBLANES), n_slots * SUBLANES)


def _dispatch_kernel(run_len_ref, run_off_ref, run_dst_ref, pad_lo_ref, pad_hi_ref, pos_ref, h_ref, xs_ref, stage, zero_scr,
                     sem, pad_sem, *, tm, rows_per_pass):
    step = pl.program_id(0)
    na = TOP_K * tm
    d = h_ref.shape[1]
    half = d // 2
    n_words = half // LANES

    def pad_copy(slot, n_slots):
        return pltpu.make_async_copy(zero_scr.at[pl.ds(0, n_slots * SUBLANES), :],
                                     xs_ref.at[_slot_rows(slot, n_slots), :], pad_sem)

    def for_each_pad(fn):
        def per_expert(ex, carry):
            slot = pad_lo_ref[ex]
            n = pad_hi_ref[ex] - slot
            for size in _run_sizes(EXPERT_BLOCK - 1):
                take = (n & size) != 0

                @pl.when(take)
                def _(slot=slot, size=size):
                    fn(pad_copy(slot, size))

                slot = slot + jnp.where(take, size, 0)
            return carry
        lax.fori_loop(0, N_EXPERTS, per_expert, 0)

    @pl.when(step == 0)
    def _():
        zero_scr[...] = jnp.zeros_like(zero_scr)
        for_each_pad(lambda cp: cp.start())

    buf = step % 2
    last = pl.num_programs(0) - 1
    runs = (run_len_ref, run_off_ref, run_dst_ref)

    def run_copy_from(which):
        def run_copy(tile_slot, sorted_slot, n_slots):
            return pltpu.make_async_copy(stage.at[which, _slot_rows(tile_slot, n_slots), :],
                                         xs_ref.at[_slot_rows(sorted_slot, n_slots), :], sem.at[which])
        return run_copy

    def wait_tile(which):
        pltpu.make_async_copy(stage.at[which], xs_ref.at[pl.ds(0, na * SUBLANES), :], sem.at[which]).wait()

    _for_each_run(jnp.maximum(step - 1, 0), runs, tm, run_copy_from(1 - buf), lambda cp: cp.start(),
                  unroll=True, enable=step > 0)

    pos = pos_ref[...]
    h = h_ref[...]
    for a0 in range(0, na, rows_per_pass):
        slot_id = a0 + _iota2((rows_per_pass, tm), 0)
        hit = pos[0:1, :] == slot_id
        for k in range(1, TOP_K):
            hit = jnp.logical_or(hit, pos[k:k + 1, :] == slot_id)
        rows = _dot(hit.astype(BF16), h)
        for i in range(n_words):
            word = _pack_halves(rows[:, i * LANES:(i + 1) * LANES], rows[:, half + i * LANES:half + (i + 1) * LANES])
            stage[buf, pl.ds(a0 * SUBLANES + i, rows_per_pass, stride=SUBLANES), :] = word

    @pl.when(step > 0)
    def _():
        wait_tile(1 - buf)

    @pl.when(step == last)
    def _():
        _for_each_run(step, runs, tm, run_copy_from(buf), lambda cp: cp.start())
        wait_tile(buf)

    @pl.when(step == 0)
    def _():
        for_each_pad(lambda cp: cp.wait())


def _dispatch(runs, pad_lo, pad_hi, pos_t, h2, n_slots, tm, rows_per_pass):
    s, d = h2.shape
    assert (d // 2) % LANES == 0 and (d // 2) // LANES == SUBLANES, "one token row must pack into one (8, 128) tile"
    na = TOP_K * tm
    return pl.pallas_call(
        functools.partial(_dispatch_kernel, tm=tm, rows_per_pass=min(rows_per_pass, na)),
        grid_spec=pltpu.PrefetchScalarGridSpec(
            num_scalar_prefetch=5,
            grid=(s // tm,),
            in_specs=[pl.BlockSpec((TOP_K, tm), lambda i, *_: (0, i)),
                      pl.BlockSpec((tm, d), lambda i, *_: (i, 0))],
            out_specs=pl.BlockSpec(memory_space=pl.ANY),
            scratch_shapes=[pltpu.VMEM((2, na * SUBLANES, LANES), U32),
                            pltpu.VMEM((EXPERT_BLOCK // 2 * SUBLANES, LANES), U32),
                            pltpu.SemaphoreType.DMA((2,)), pltpu.SemaphoreType.DMA(())]),
        out_shape=jax.ShapeDtypeStruct((n_slots * SUBLANES, LANES), U32),
        compiler_params=_params(("arbitrary",), has_side_effects=True, disable_bounds_checks=True),
    )(*runs, pad_lo, pad_hi, pos_t, h2)


def _expert_kernel(be_ref, nu_ref, next_ref, par_ref, valid_ref, x_ref, wg_hbm, wu_hbm, wd_hbm, y_ref, wg_f32, wu_f32, wd_f32,
                   wg_scr, wu_scr, wd_scr, sem):
    b = pl.program_id(0)
    bm = EXPERT_BLOCK
    active = b < nu_ref[0]
    new_expert = jnp.logical_or(b == 0, be_ref[b] != be_ref[jnp.maximum(b - 1, 0)])

    def weight_copies(ex, which):
        return [pltpu.make_async_copy(src.at[ex], dst.at[which], sem.at[which])
                for src, dst in ((wg_hbm, wg_f32), (wu_hbm, wu_f32), (wd_hbm, wd_f32))]

    @pl.when(jnp.logical_and(active, new_expert))
    def _():
        which = par_ref[b]

        @pl.when(b == 0)
        def _():
            for cp in weight_copies(be_ref[b], which):
                cp.start()

        for cp in weight_copies(be_ref[b], which):
            cp.wait()
        wg_scr[...] = wg_f32[which].astype(BF16)
        wu_scr[...] = wu_f32[which].astype(BF16)
        wd_scr[...] = wd_f32[which].astype(BF16)
        nb = next_ref[b]

        @pl.when(nb < nu_ref[0])
        def _():
            for cp in weight_copies(be_ref[nb], 1 - which):
                cp.start()

    def ffn_rows(rows):
        los, his = [], []
        for i in range(SUBLANES):
            lo, hi = _unpack_halves(x_ref[pl.ds(i, rows, stride=SUBLANES), :])
            los.append(lo.astype(BF16))
            his.append(hi.astype(BF16))
        xb = jnp.concatenate(los + his, axis=1)
        hid = _silu(_dot(xb, wg_scr[...])) * _dot(xb, wu_scr[...])
        y = _dot(hid.astype(BF16), wd_scr[...])
        half = y.shape[1] // 2
        for i in range(SUBLANES):
            word = _pack_halves(_round_bf16(y[:, i * LANES:(i + 1) * LANES]),
                                _round_bf16(y[:, half + i * LANES:half + (i + 1) * LANES]))
            y_ref[pl.ds(i, rows, stride=SUBLANES), :] = word

    half_full = valid_ref[b] <= bm // 2

    @pl.when(jnp.logical_and(active, jnp.logical_not(half_full)))
    def _():
        ffn_rows(bm)

    @pl.when(jnp.logical_and(active, half_full))
    def _():
        ffn_rows(bm // 2)
        y_ref[bm // 2 * SUBLANES:bm * SUBLANES, :] = jnp.zeros((bm // 2 * SUBLANES, LANES), U32)


def _experts(block_e, n_used, next_block, parity, valid, xs, w_gate, w_up, w_down):
    d, ff = w_gate.shape[1], w_gate.shape[2]
    bm = EXPERT_BLOCK
    n_blocks = xs.shape[0] // (bm * SUBLANES)
    blk = lambda b, be, nu, *_: (jnp.minimum(b, nu[0] - 1), 0)
    hbm = pl.BlockSpec(memory_space=pl.ANY)
    return pl.pallas_call(
        _expert_kernel,
        grid_spec=pltpu.PrefetchScalarGridSpec(
            num_scalar_prefetch=5,
            grid=(n_blocks,),
            in_specs=[pl.BlockSpec((bm * SUBLANES, LANES), blk), hbm, hbm, hbm],
            out_specs=pl.BlockSpec((bm * SUBLANES, LANES), blk),
            scratch_shapes=[pltpu.VMEM((2, d, ff), F32), pltpu.VMEM((2, d, ff), F32), pltpu.VMEM((2, ff, d), F32),
                            pltpu.VMEM((d, ff), BF16), pltpu.VMEM((d, ff), BF16), pltpu.VMEM((ff, d), BF16),
                            pltpu.SemaphoreType.DMA((2,))]),
        out_shape=jax.ShapeDtypeStruct(xs.shape, U32),
        compiler_params=_params(("arbitrary",)),
    )(block_e, n_used, next_block, parity, valid, xs, w_gate, w_up, w_down)


def _combine_kernel(run_len_ref, run_off_ref, run_src_ref, ys_ref, h_ref, x1_ref, pos_ref, wts_ref, wg_ref, wu_ref, wd_ref, gt_ref, gf_ref,
                    o_ref, stage, sem, *, tm, rows_per_pass):
    step = pl.program_id(0)
    na = TOP_K * tm
    buf = step % 2

    last = pl.num_programs(0) - 1

    def fetch_tile(tile, which, unroll=False):
        def run_copy(tile_slot, sorted_slot, n_slots):
            return pltpu.make_async_copy(ys_ref.at[_slot_rows(sorted_slot, n_slots), :],
                                         stage.at[which, _slot_rows(tile_slot, n_slots), :], sem.at[which])
        _for_each_run(tile, (run_len_ref, run_off_ref, run_src_ref), tm, run_copy, lambda cp: cp.start(), unroll)

    def wait_tile(which):
        pltpu.make_async_copy(ys_ref.at[pl.ds(0, na * SUBLANES), :], stage.at[which], sem.at[which]).wait()

    @pl.when(step == 0)
    def _():
        fetch_tile(step, buf)

    fetch_tile(jnp.minimum(step + 1, last), 1 - buf, unroll=True)

    hb = h_ref[...]
    hid = _silu(_dot(hb, wg_ref[...])) * _dot(hb, wu_ref[...])
    acc = _dot(hid.astype(BF16), wd_ref[...])

    wait_tile(buf)

    pos = pos_ref[...]
    wts = wts_ref[...]
    for a0 in range(0, na, rows_per_pass):
        los, his = [], []
        for i in range(SUBLANES):
            lo, hi = _unpack_halves(stage[buf, pl.ds(a0 * SUBLANES + i, rows_per_pass, stride=SUBLANES), :])
            los.append(lo.astype(BF16))
            his.append(hi.astype(BF16))
        y_rows = jnp.concatenate(los + his, axis=1)
        slot_id = a0 + _iota2((rows_per_pass, tm), 0)
        wmat = jnp.zeros((rows_per_pass, tm), F32)
        for k in range(TOP_K):
            wmat = wmat + jnp.where(pos[k:k + 1, :] == slot_id, wts[k:k + 1, :], 0.0)
        acc = acc + _dg(wmat.astype(BF16), y_rows, TN)
    x2 = x1_ref[...] + gt_ref[...] * acc
    o_ref[...] = _rms(x2, NORM_EPS) * gf_ref[...]

    @pl.when(step == last)
    def _():
        wait_tile(1 - buf)


def _combine(runs, ys, h2, x1, pos_t, wts_t, w_gate, w_up, w_down, gt2, gf, tm, rows_per_pass):
    s, d = x1.shape
    ff = w_gate.shape[1]
    na = TOP_K * tm
    const = lambda shape: pl.BlockSpec(shape, lambda i, *_: (0, 0), pipeline_mode=pl.Buffered(1))
    tile = lambda cols: pl.BlockSpec((tm, cols), lambda i, *_: (i, 0))
    per_k = pl.BlockSpec((TOP_K, tm), lambda i, *_: (0, i))
    return pl.pallas_call(
        functools.partial(_combine_kernel, tm=tm, rows_per_pass=min(rows_per_pass, na)),
        grid_spec=pltpu.PrefetchScalarGridSpec(
            num_scalar_prefetch=3,
            grid=(s // tm,),
            in_specs=[pl.BlockSpec(memory_space=pl.ANY),
                      tile(d), tile(d), per_k, per_k,
                      const((d, ff)), const((d, ff)), const((ff, d)), const((1, d)), const((1, d))],
            out_specs=tile(d),
            scratch_shapes=[pltpu.VMEM((2, na * SUBLANES, LANES), U32), pltpu.SemaphoreType.DMA((2,))]),
        out_shape=jax.ShapeDtypeStruct((s, d), F32),
        compiler_params=_params(("arbitrary",), disable_bounds_checks=True),
    )(*runs, ys, h2, x1, pos_t, wts_t, w_gate, w_up, w_down, gt2, gf)


def _mixer(x2d, mod, norm1_g, norm2_g, w_in, lb, hgrn_onorm_g, gdn_conv_w, gdn_a_log, gdn_dt_bias, gdn_onorm_g,
           w_branch_hgrn, w_branch_gdn, w_out, tiles):
    d = x2d.shape[1]
    sh1, sc1, gt1, sh2, sc2, _ = [mod[:, i * d:(i + 1) * d] for i in range(6)]
    key = HEADS * HEAD_DIM
    small0 = 4 * key + 3 * key
    small1 = small0 + 2 * HEADS
    w_in_t = jnp.swapaxes(w_in, 1, 2)
    w_main_t = _wprep(w_in_t, small0, small1, tiles["wprep_tn"])
    w_small_t = w_in_t[0, small0:small1, :].astype(BF16)
    proj, ab_t = _inproj(x2d, norm1_g, sc1, sh1, w_main_t, w_small_t, tiles["in_tm"], tiles["in_tn"])
    o_a = _hgrn(proj, lb, hgrn_onorm_g, tiles["mix_ts"])
    u, wqd, ku, attn, dl = _gdn_prep(proj, gdn_conv_w, ab_t, gdn_a_log, gdn_dt_bias, tiles["prep_ts"])
    o_b = _gdn_scan(u, wqd, ku, attn, dl, proj, gdn_onorm_g, tiles["mix_ts"])
    return _merge(o_a, o_b, proj, x2d, w_branch_hgrn.astype(BF16), w_branch_gdn.astype(BF16),
                  w_out.astype(BF16), gt1, norm2_g, sc2, sh2, tiles["merge_tm"])


def _moe(x1, h2, mod, norm2_g, normf_g, w_router, router_bias, w_exp_gate, w_exp_up, w_exp_down, w_sh_gate,
         w_sh_up, w_sh_down, tiles):
    s, d = x1.shape
    tm = tiles["moe_tm"]
    sh2, sc2, gt2 = [mod[:, i * d:(i + 1) * d] for i in (3, 4, 5)]
    pos_t, wts_t, before, ntile = _router(x1, norm2_g, sc2, sh2, w_router.T, router_bias.reshape(-1, 1), tm)
    bm = EXPERT_BLOCK
    n_blocks = -(-(s * TOP_K + N_EXPERTS * (bm - 1)) // bm)
    before = before[:, :, 0]
    ntile = ntile[:, :, 0]
    counts = before[-1] + ntile[-1]
    padded = (counts + bm - 1) // bm * bm
    earlier = jnp.arange(N_EXPERTS)[None, :] < jnp.arange(N_EXPERTS)[:, None]
    pstart = jnp.sum(jnp.where(earlier, padded[None, :], 0), axis=1).astype(I32)
    pend = pstart + padded
    block_start = jnp.arange(n_blocks, dtype=I32) * bm
    block_e = jnp.minimum(jnp.sum(pend[None, :] <= block_start[:, None], axis=1), N_EXPERTS - 1).astype(I32)
    n_used = pend[-1:] // bm
    run_off = jnp.sum(jnp.where(earlier[None], ntile[:, None, :], 0), axis=2)
    runs = (ntile.reshape(-1), run_off.reshape(-1), (before + pstart[None, :]).reshape(-1))
    xs = _dispatch(runs, pstart + counts, pend, pos_t, h2, n_blocks * bm, tm, tiles["dispatch_rows"])
    own = block_e[:, None] == jnp.arange(N_EXPERTS)[None, :]
    next_block = jnp.sum(jnp.where(own, pend[None, :], 0), axis=1) // bm
    switches = jnp.concatenate([jnp.zeros((1,), I32), (block_e[1:] != block_e[:-1]).astype(I32)])
    upto = jnp.arange(n_blocks)[None, :] <= jnp.arange(n_blocks)[:, None]
    parity = jnp.sum(jnp.where(upto, switches[None, :], 0), axis=1).astype(I32) % 2
    start_b = jnp.sum(jnp.where(own, pstart[None, :], 0), axis=1)
    count_b = jnp.sum(jnp.where(own, counts[None, :], 0), axis=1)
    valid = jnp.clip(count_b - (block_start - start_b), 0, bm).astype(I32)
    ys = _experts(block_e, n_used, next_block, parity, valid, xs, w_exp_gate, w_exp_up, w_exp_down)
    return _combine(runs, ys, h2, x1, pos_t, wts_t, w_sh_gate.astype(BF16), w_sh_up.astype(BF16),
                    w_sh_down.astype(BF16), gt2, normf_g, tm, tiles["combine_rows"])


def _tiles(s):
    pick = lambda want: min(want, s)
    return dict(ada_tn=1024, wprep_tn=512, in_tm=pick(1024), in_tn=1536, mix_ts=pick(512), prep_ts=pick(2048),
                merge_tm=pick(512), moe_tm=pick(256), dispatch_rows=512, combine_rows=256)


def kernel(x, c, w_ada, b_ada, norm1_g, norm2_g, w_in, hgrn_lb_table, hgrn_onorm_g, gdn_conv_w, gdn_a_log, gdn_dt_bias, gdn_onorm_g, w_branch_hgrn, w_branch_gdn, w_out, w_router, router_bias, w_exp_gate, w_exp_up, w_exp_down, w_sh_gate, w_sh_up, w_sh_down, normf_g):
    b, s, d = x.shape
    assert b == 1 and w_ada.shape[0] == 1, "one sequence, one layer"
    tiles = _tiles(s)
    lb = jnp.sum(jax.nn.softmax(hgrn_lb_table.astype(F32), axis=0)[0:1], axis=0, keepdims=True)
    mod = _ada(c, w_ada[0], b_ada[0], tiles["ada_tn"])
    row = lambda v: v.reshape(1, -1)
    x1, h2 = _mixer(x[0], mod, row(norm1_g[0]), row(norm2_g[0]), w_in, lb, row(hgrn_onorm_g[0]), gdn_conv_w[0],
                    gdn_a_log[0], gdn_dt_bias[0], row(gdn_onorm_g[0]), w_branch_hgrn[0], w_branch_gdn[0], w_out[0],
                    tiles)
    out = _moe(x1, h2, mod, row(norm2_g[0]), row(normf_g), w_router[0], router_bias[0], w_exp_gate[0],
               w_exp_up[0], w_exp_down[0], w_sh_gate[0], w_sh_up[0], w_sh_down[0], tiles)
    return out[None]
```

```python
import functools

import jax
import jax.numpy as jnp
from jax import lax
from jax.experimental import pallas as pl
from jax.experimental.pallas import tpu as pltpu

F32 = jnp.float32
BF16 = jnp.bfloat16
I32 = jnp.int32
U32 = jnp.uint32

NORM_EPS = 1e-6
L2_EPS = 1e-6
HEADS = 8
HEAD_DIM = 128
CONV_WIDTH = 4
CHUNK = 64
N_EXPERTS = 64
N_GROUPS = 8
GROUP_SIZE = N_EXPERTS // N_GROUPS
TOPK_GROUPS = 4
TOP_K = 8
ROUTED_SCALE = 2.5
EXPERT_BLOCK = 512

LANES = 128
SUBLANES = 8
MXU_DIM = 256
VMEM_LIMIT = 56 * 1024 * 1024

NT = (((1,), (1,)), ((), ()))
TN = (((0,), (0,)), ((), ()))


def _params(sem, **kw):
    return pltpu.CompilerParams(dimension_semantics=sem, vmem_limit_bytes=VMEM_LIMIT, **kw)


def _dot(a, b):
    return jnp.dot(a, b, preferred_element_type=F32)


def _dg(a, b, dims):
    return lax.dot_general(a, b, dims, preferred_element_type=F32)


def _split(x):
    hi = x.astype(BF16)
    lo = (x - hi.astype(F32)).astype(BF16)
    return hi, lo


def _dot_exact_lhs(a_bf16, x, dims=None):
    hi, lo = _split(x)
    if dims is None:
        return _dot(a_bf16, hi) + _dot(a_bf16, lo)
    return _dg(a_bf16, hi, dims) + _dg(a_bf16, lo, dims)


def _sigmoid(x):
    return 0.5 * jnp.tanh(0.5 * x) + 0.5


def _silu(x):
    return x * _sigmoid(x)


def _rms(x, eps):
    return x * lax.rsqrt(jnp.mean(x * x, axis=-1, keepdims=True) + eps)


def _iota2(shape, dim):
    return lax.broadcasted_iota(I32, shape, dim)


def _pack_halves(lo, hi):
    lo_bits = lax.shift_right_logical(pltpu.bitcast(lo, U32), U32(16))
    hi_bits = pltpu.bitcast(hi, U32) & U32(0xFFFF0000)
    return lo_bits | hi_bits


def _unpack_halves(word):
    lo = pltpu.bitcast(lax.shift_left(word, U32(16)), F32)
    hi = pltpu.bitcast(word & U32(0xFFFF0000), F32)
    return lo, hi


def _round_bf16(x):
    return x.astype(BF16).astype(F32)


def _ada_kernel(c_ref, w_ref, b_ref, o_ref):
    cond = _silu(c_ref[...])
    o_ref[...] = jnp.sum(w_ref[...] * cond, axis=0, keepdims=True) + b_ref[...]


def _ada(c, w_ada, b_ada, tn):
    d, n = w_ada.shape
    return pl.pallas_call(
        _ada_kernel,
        grid=(n // tn,),
        in_specs=[pl.BlockSpec((d, 1), lambda j: (0, 0)),
                  pl.BlockSpec((d, tn), lambda j: (0, j)),
                  pl.BlockSpec((1, tn), lambda j: (0, j))],
        out_specs=pl.BlockSpec((1, tn), lambda j: (0, j)),
        out_shape=jax.ShapeDtypeStruct((1, n), F32),
        compiler_params=_params(("arbitrary",)),
    )(c.reshape(d, 1), w_ada, b_ada.reshape(1, n))


def _wprep_kernel(a_ref, b_ref, o_ref, *, first_shifted, shift):
    j = pl.program_id(0)

    @pl.when(j < first_shifted)
    def _():
        o_ref[...] = a_ref[...].astype(BF16)

    @pl.when(j >= first_shifted)
    def _():
        tn = a_ref.shape[0]
        o_ref[0:tn - shift, :] = a_ref[shift:tn, :].astype(BF16)
        o_ref[tn - shift:tn, :] = b_ref[...].astype(BF16)


def _wprep(w_in_t, cut0, cut1, tn):
    _, n_in, d = w_in_t.shape
    shift = cut1 - cut0
    n_out = n_in - shift
    assert cut0 % tn == 0 and n_out % tn == 0 and tn % shift == 0 and shift % (2 * SUBLANES) == 0
    return pl.pallas_call(
        functools.partial(_wprep_kernel, first_shifted=cut0 // tn, shift=shift),
        grid=(n_out // tn,),
        in_specs=[pl.BlockSpec((None, tn, d), lambda j: (0, j, 0)),
                  pl.BlockSpec((None, shift, d), lambda j: (0, (j + 1) * (tn // shift), 0))],
        out_specs=pl.BlockSpec((tn, d), lambda j: (j, 0)),
        out_shape=jax.ShapeDtypeStruct((n_out, d), BF16),
        compiler_params=_params(("arbitrary",)),
    )(w_in_t, w_in_t)


def _inproj_kernel(x_ref, g_ref, sc_ref, sh_ref, w_ref, wst_ref, proj_ref, smallt_ref, h_scr):
    @pl.when(pl.program_id(1) == 0)
    def _():
        h = _rms(x_ref[...], NORM_EPS) * g_ref[...] * (1.0 + sc_ref[...]) + sh_ref[...]
        hb = h.astype(BF16)
        h_scr[...] = hb
        smallt_ref[...] = _dg(wst_ref[...], hb, NT)

    proj_ref[...] = _dg(h_scr[...], w_ref[...], NT).astype(BF16)


def _inproj(x, g, sc, sh, w_main_t, w_small_t, tm, tn):
    s, d = x.shape
    n = w_main_t.shape[0]
    ns = w_small_t.shape[0]
    row = lambda i, j: (0, 0)
    return pl.pallas_call(
        _inproj_kernel,
        grid=(s // tm, n // tn),
        in_specs=[pl.BlockSpec((tm, d), lambda i, j: (i, 0)),
                  pl.BlockSpec((1, d), row), pl.BlockSpec((1, d), row), pl.BlockSpec((1, d), row),
                  pl.BlockSpec((tn, d), lambda i, j: (j, 0)),
                  pl.BlockSpec((ns, d), row)],
        out_specs=[pl.BlockSpec((tm, tn), lambda i, j: (i, j)),
                   pl.BlockSpec((ns, tm), lambda i, j: (0, i))],
        out_shape=[jax.ShapeDtypeStruct((s, n), BF16), jax.ShapeDtypeStruct((ns, s), F32)],
        scratch_shapes=[pltpu.VMEM((tm, d), BF16)],
        compiler_params=_params(("arbitrary", "arbitrary")),
    )(x, g, sc, sh, w_main_t, w_small_t)


def _hgrn_kernel(q_ref, f_ref, i_ref, g_ref, lb_ref, on_ref, o_ref, st_scr, *, n_chunks):
    @pl.when(pl.program_id(0) == 0)
    def _():
        st_scr[...] = jnp.zeros_like(st_scr)

    c = CHUNK
    hd = HEAD_DIM
    causal = _iota2((c, c), 1) <= _iota2((c, c), 0)
    tri = causal.astype(BF16)
    lb = lb_ref[...]
    on_g = on_ref[...]
    heads = [slice(h * hd, (h + 1) * hd) for h in range(HEADS)]

    def chunk(n, carry):
        rows = pl.ds(pl.multiple_of(n * c, c), c)
        f = lb + (1.0 - lb) * _sigmoid(f_ref[rows, :].astype(F32))
        b = _dot_exact_lhs(tri, jnp.log(f))
        k = 1.0 - f
        q = _silu(q_ref[rows, :].astype(F32)) * (hd ** -0.5)
        v = i_ref[rows, :]
        b_mid = b[c // 2:c // 2 + 1, :]
        b_last = b[c - 1:c, :]
        qa = (q * jnp.exp(b - b_mid)).astype(BF16)
        ka = (k * jnp.exp(b_mid - b)).astype(BF16)
        qi = (q * jnp.exp(b)).astype(BF16)
        ku = (k * jnp.exp(b_last - b)).astype(BF16)
        dec = jnp.exp(b_last)
        gate = on_g * _silu(g_ref[rows, :].astype(F32))
        sts = [st_scr[h] for h in range(HEADS)]
        scores = [jnp.where(causal, _dg(qa[:, sl], ka[:, sl], NT), 0.0).astype(BF16) for sl in heads]
        inter = [_dg(qi[:, sl], st.astype(BF16), NT) for sl, st in zip(heads, sts)]
        kv = [_dg(v[:, sl], ku[:, sl], TN) for sl in heads]
        for h, sl in enumerate(heads):
            st_scr[h] = dec[:, sl] * sts[h] + kv[h]
        outs = [_rms(_dot(sc, v[:, sl]) + it, NORM_EPS) for sc, sl, it in zip(scores, heads, inter)]
        o_ref[rows, :] = (jnp.concatenate(outs, axis=1) * gate).astype(BF16)
        return carry

    lax.fori_loop(0, n_chunks, chunk, 0, unroll=4)


def _hgrn(proj, lb, onorm_g, ts):
    s = proj.shape[0]
    width = HEADS * HEAD_DIM
    col = lambda blk: pl.BlockSpec((ts, width), lambda j, blk=blk: (j, blk))
    const = pl.BlockSpec((1, width), lambda j: (0, 0))
    return pl.pallas_call(
        functools.partial(_hgrn_kernel, n_chunks=ts // CHUNK),
        grid=(s // ts,),
        in_specs=[col(0), col(1), col(2), col(3), const, const],
        out_specs=pl.BlockSpec((ts, width), lambda j: (j, 0)),
        out_shape=jax.ShapeDtypeStruct((s, width), BF16),
        scratch_shapes=[pltpu.VMEM((HEADS, HEAD_DIM, HEAD_DIM), F32)],
        compiler_params=_params(("arbitrary",)),
    )(proj, proj, proj, proj, lb, jnp.tile(onorm_g, (1, HEADS)))


def _gdn_prep_kernel(q_ref, k_ref, v_ref, qp_ref, kp_ref, vp_ref, wq_ref, wk_ref, wv_ref, ab_ref, alog_ref,
                     dtb_ref, tri_ref, eye_ref, u_ref, wqd_ref, ku_ref, attn_ref, dl_ref, cat_scr, rows_scr, cols_scr,
                     *, n_chunks, ts):
    h = pl.program_id(1)
    first = pl.program_id(0) == 0
    c = CHUNK
    hd = HEAD_DIM

    def conv_silu(cur_ref, prev_ref, w_ref):
        cat_scr[0:SUBLANES, :] = jnp.where(first, 0.0, prev_ref[...].astype(F32))
        cat_scr[SUBLANES:SUBLANES + ts, :] = cur_ref[...].astype(F32)
        acc = None
        for j in range(CONV_WIDTH):
            off = SUBLANES - (CONV_WIDTH - 1) + j
            term = cat_scr[off:off + ts, :] * w_ref[j:j + 1, :]
            acc = term if acc is None else acc + term
        return _silu(acc)

    def l2n(x):
        return x * lax.rsqrt(jnp.sum(x * x, axis=-1, keepdims=True) + L2_EPS)

    q_all = l2n(conv_silu(q_ref, qp_ref, wq_ref)) * (hd ** -0.5)
    k_all = l2n(conv_silu(k_ref, kp_ref, wk_ref))
    v_all = conv_silu(v_ref, vp_ref, wv_ref)

    @pl.when(h == 0)
    def _():
        z = ab_ref[0:HEADS, :] + dtb_ref[...]
        softplus = jnp.maximum(z, 0.0) + jnp.log(1.0 + jnp.exp(-jnp.abs(z)))
        ld_rows = -jnp.exp(alog_ref[...]) * softplus
        hi, lo = _split(ld_rows)
        tri_blocks = tri_ref[...]
        w = tri_blocks.shape[0]
        spans = [slice(t0, t0 + w) for t0 in range(0, ts, w)]
        g_rows = jnp.concatenate([_dg(hi[:, sp], tri_blocks, NT) + _dg(lo[:, sp], tri_blocks, NT) for sp in spans],
                                 axis=1)
        beta_rows = _sigmoid(ab_ref[HEADS:2 * HEADS, :])
        rows_scr[...] = g_rows
        rows = jnp.concatenate([g_rows, beta_rows, jnp.zeros((LANES - 2 * HEADS, ts), F32)], axis=0)
        r_hi, r_lo = _split(rows)
        r_lo2 = (rows - r_hi.astype(F32) - r_lo.astype(F32)).astype(BF16)
        eye_w = eye_ref[...]
        for sp in spans:
            cols_scr[sp, :] = _dg(eye_w, r_hi[:, sp], NT) + _dg(eye_w, r_lo[:, sp], NT) + _dg(eye_w, r_lo2[:, sp], NT)

    lane = _iota2((ts, LANES), 1)
    cols = cols_scr[...]
    gc_all = jnp.sum(jnp.where(lane == h, cols, 0.0), axis=1, keepdims=True)
    bc_all = jnp.sum(jnp.where(lane == h + HEADS, cols, 0.0), axis=1, keepdims=True)
    g_row = rows_scr[pl.ds(h, 1), :]
    egc_all = jnp.exp(gc_all)

    r = _iota2((c, c), 0)
    cidx = _iota2((c, c), 1)
    causal = cidx <= r
    strict = cidx < r
    eye_f = (r == cidx).astype(F32)
    chunks = [slice(n * c, (n + 1) * c) for n in range(n_chunks)]

    q16 = q_all.astype(BF16)
    k16 = k_all.astype(BF16)
    kq = [_dg(jnp.concatenate([k16[sl], q16[sl]], axis=0), k16[sl], NT) for sl in chunks]
    dm = []
    for sl in chunks:
        diff = gc_all[sl] - g_row[:, sl]
        dm.append(jnp.where(causal, jnp.exp(jnp.where(causal, diff, 0.0)), 0.0))
    bm = [-jnp.where(strict, bc_all[sl] * x[0:c] * d, 0.0) for sl, x, d in zip(chunks, kq, dm)]
    p = [eye_f + b for b in bm]
    bm = [_dot(b.astype(BF16), b.astype(BF16)) for b in bm]
    for _ in range(c.bit_length() - 3):
        res = [_dot(b.astype(BF16), jnp.concatenate([b, pp], axis=1).astype(BF16)) for b, pp in zip(bm, p)]
        p = [pp + x[:, c:2 * c] for pp, x in zip(p, res)]
        bm = [x[:, 0:c] for x in res]
    p = [pp + _dot(b.astype(BF16), pp.astype(BF16)) for b, pp in zip(bm, p)]
    rhs = jnp.concatenate([v_all * bc_all, k_all * (bc_all * egc_all)], axis=1).astype(BF16)
    sol = [_dot(pp.astype(BF16), rhs[sl]) for pp, sl in zip(p, chunks)]
    qd_all = (q_all * egc_all).astype(BF16)
    for n, sl in enumerate(chunks):
        g_last = gc_all[(n + 1) * c - 1:(n + 1) * c, :]
        u_ref[sl, :] = sol[n][:, 0:hd].astype(BF16)
        wqd_ref[2 * n * c:(2 * n + 1) * c, :] = sol[n][:, hd:2 * hd].astype(BF16)
        wqd_ref[(2 * n + 1) * c:(2 * n + 2) * c, :] = qd_all[sl]
        ku_ref[sl, :] = (k_all[sl] * jnp.exp(g_last - gc_all[sl])).astype(BF16)
        attn_ref[sl, :] = (kq[n][c:2 * c] * dm[n]).astype(BF16)
        dl_ref[n:n + 1, :] = jnp.broadcast_to(jnp.exp(g_last), (1, hd))


def _gdn_prep(proj, conv_w, ab_t, a_log, dt_bias, ts):
    s = proj.shape[0]
    hd = HEAD_DIM
    c = CHUNK
    q0 = 4 * HEADS
    cur = lambda off: pl.BlockSpec((ts, hd), lambda j, h, off=off: (j, off + h))
    prev = lambda off: pl.BlockSpec((SUBLANES, hd),
                                    lambda j, h, off=off: (jnp.maximum(j * (ts // SUBLANES) - 1, 0), off + h))
    cw = lambda off: pl.BlockSpec((CONV_WIDTH, hd), lambda j, h, off=off: (0, off + h))
    per_head_scalar = pl.BlockSpec((HEADS, 1), lambda j, h: (0, 0))
    w = min(ts, MXU_DIM)
    const = pl.BlockSpec((w, w), lambda j, h: (0, 0))
    pos = jnp.arange(w)
    tri_blocks = ((pos[:, None] // c == pos[None, :] // c) & (pos[None, :] <= pos[:, None])).astype(BF16)
    eye = (pos[:, None] == pos[None, :]).astype(BF16)
    per_head = lambda rows, cols: pl.BlockSpec((None, rows, cols), lambda j, h: (h, j, 0))
    return pl.pallas_call(
        functools.partial(_gdn_prep_kernel, n_chunks=ts // c, ts=ts),
        grid=(s // ts, HEADS),
        in_specs=[cur(q0), cur(q0 + HEADS), cur(q0 + 2 * HEADS),
                  prev(q0), prev(q0 + HEADS), prev(q0 + 2 * HEADS),
                  cw(0), cw(HEADS), cw(2 * HEADS),
                  pl.BlockSpec((2 * HEADS, ts), lambda j, h: (0, j)),
                  per_head_scalar, per_head_scalar, const, const],
        out_specs=[pl.BlockSpec((ts, hd), lambda j, h: (j, h)),
                   pl.BlockSpec((2 * ts, hd), lambda j, h: (j, h)),
                   pl.BlockSpec((ts, hd), lambda j, h: (j, h)),
                   per_head(ts, c),
                   per_head(ts // c, hd)],
        out_shape=[jax.ShapeDtypeStruct((s, HEADS * hd), BF16),
                   jax.ShapeDtypeStruct((2 * s, HEADS * hd), BF16),
                   jax.ShapeDtypeStruct((s, HEADS * hd), BF16),
                   jax.ShapeDtypeStruct((HEADS, s, c), BF16),
                   jax.ShapeDtypeStruct((HEADS, s // c, hd), F32)],
        scratch_shapes=[pltpu.VMEM((ts + SUBLANES, hd), F32), pltpu.VMEM((HEADS, ts), F32),
                        pltpu.VMEM((ts, LANES), F32)],
        compiler_params=_params(("arbitrary", "arbitrary")),
    )(proj, proj, proj, proj, proj, proj, conv_w, conv_w, conv_w,
      ab_t, a_log.reshape(HEADS, 1), dt_bias.reshape(HEADS, 1), tri_blocks, eye)


def _gdn_scan_kernel(u_ref, wqd_ref, ku_ref, attn_ref, dl_ref, g_ref, on_ref, o_ref, st_scr, *, n_chunks):
    @pl.when(pl.program_id(0) == 0)
    def _():
        st_scr[...] = jnp.zeros_like(st_scr)

    c = CHUNK
    hd = HEAD_DIM
    on_g = on_ref[...]
    heads = [slice(h * hd, (h + 1) * hd) for h in range(HEADS)]

    def chunk(n, carry):
        rows = pl.ds(pl.multiple_of(n * c, c), c)
        rows2 = pl.ds(pl.multiple_of(2 * n * c, 2 * c), 2 * c)
        sts = [st_scr[h] for h in range(HEADS)]
        wq = [_dot(wqd_ref[rows2, sl], st.astype(BF16)) for sl, st in zip(heads, sts)]
        vn = [(u_ref[rows, sl].astype(F32) - x[0:c]).astype(BF16) for sl, x in zip(heads, wq)]
        upd = [_dg(ku_ref[rows, sl], v, TN) for sl, v in zip(heads, vn)]
        for h in range(HEADS):
            st_scr[h] = dl_ref[h, pl.ds(n, 1), :] * sts[h] + upd[h]
        outs = [_rms(x[c:2 * c] + _dot(attn_ref[h, rows, :], v), NORM_EPS)
                for h, (x, v) in enumerate(zip(wq, vn))]
        gate = jnp.tile(on_g, (1, HEADS)) * _silu(g_ref[rows, :].astype(F32))
        o_ref[rows, :] = (jnp.concatenate(outs, axis=1) * gate).astype(BF16)
        return carry

    lax.fori_loop(0, n_chunks, chunk, 0, unroll=4)


def _gdn_scan(u, wqd, ku, attn, dl, proj, onorm_g, ts):
    s, width = u.shape
    c = CHUNK
    gate_blk = (4 * HEADS + 3 * HEADS) * HEAD_DIM // width
    return pl.pallas_call(
        functools.partial(_gdn_scan_kernel, n_chunks=ts // c),
        grid=(s // ts,),
        in_specs=[pl.BlockSpec((ts, width), lambda j: (j, 0)),
                  pl.BlockSpec((2 * ts, width), lambda j: (j, 0)),
                  pl.BlockSpec((ts, width), lambda j: (j, 0)),
                  pl.BlockSpec((HEADS, ts, c), lambda j: (0, j, 0)),
                  pl.BlockSpec((HEADS, ts // c, HEAD_DIM), lambda j: (0, j, 0)),
                  pl.BlockSpec((ts, width), lambda j: (j, gate_blk)),
                  pl.BlockSpec((1, HEAD_DIM), lambda j: (0, 0))],
        out_specs=pl.BlockSpec((ts, width), lambda j: (j, 0)),
        out_shape=jax.ShapeDtypeStruct((s, width), BF16),
        scratch_shapes=[pltpu.VMEM((HEADS, HEAD_DIM, HEAD_DIM), F32)],
        compiler_params=_params(("arbitrary",)),
    )(u, wqd, ku, attn, dl, proj, onorm_g)


def _merge_kernel(oa_ref, ob_ref, mga_ref, mgb_ref, x_ref, wa_ref, wb_ref, wo_ref, gt_ref, g2_ref, sc_ref,
                  sh_ref, x1_ref, h2_ref):
    ya = _dot(oa_ref[...], wa_ref[...])
    yb = _dot(ob_ref[...], wb_ref[...])
    merged = _sigmoid(mga_ref[...].astype(F32)) * ya + _sigmoid(mgb_ref[...].astype(F32)) * yb
    x1 = x_ref[...] + gt_ref[...] * _dot(merged.astype(BF16), wo_ref[...])
    x1_ref[...] = x1
    h2 = _rms(x1, NORM_EPS) * g2_ref[...] * (1.0 + sc_ref[...]) + sh_ref[...]
    h2_ref[...] = h2.astype(BF16)


def _merge(o_a, o_b, proj, x, w_a, w_b, w_o, gt1, g2, sc2, sh2, tm):
    s, d = x.shape
    dv = o_a.shape[1]
    mg0 = (8 * HEADS * HEAD_DIM) // d
    const = lambda shape: pl.BlockSpec(shape, lambda i: (0, 0), pipeline_mode=pl.Buffered(1))
    return pl.pallas_call(
        _merge_kernel,
        grid=(s // tm,),
        in_specs=[pl.BlockSpec((tm, dv), lambda i: (i, 0)),
                  pl.BlockSpec((tm, dv), lambda i: (i, 0)),
                  pl.BlockSpec((tm, d), lambda i: (i, mg0)),
                  pl.BlockSpec((tm, d), lambda i: (i, mg0 + 1)),
                  pl.BlockSpec((tm, d), lambda i: (i, 0)),
                  const((dv, d)), const((dv, d)), const((d, d)),
                  const((1, d)), const((1, d)), const((1, d)), const((1, d))],
        out_specs=[pl.BlockSpec((tm, d), lambda i: (i, 0)), pl.BlockSpec((tm, d), lambda i: (i, 0))],
        out_shape=[jax.ShapeDtypeStruct((s, d), F32), jax.ShapeDtypeStruct((s, d), BF16)],
        compiler_params=_params(("arbitrary",)),
    )(o_a, o_b, proj, proj, x, w_a, w_b, w_o, gt1, g2, sc2, sh2)


def _first_max(vals, iota, size, axis):
    m = jnp.max(vals, axis=axis, keepdims=True)
    idx = jnp.min(jnp.where(vals == m, iota, size), axis=axis, keepdims=True)
    return m, idx


def _router_kernel(x1_ref, g2_ref, sc_ref, sh_ref, wrt_ref, bias_ref, upper_ref, pos_ref, wts_ref, before_ref,
                   ntile_ref, cnt_scr, *, tm):
    @pl.when(pl.program_id(0) == 0)
    def _():
        cnt_scr[...] = jnp.zeros_like(cnt_scr)

    e = N_EXPERTS
    h2 = _rms(x1_ref[...], NORM_EPS) * g2_ref[...] * (1.0 + sc_ref[...]) + sh_ref[...]
    logits = lax.dot_general(wrt_ref[...], h2, NT, preferred_element_type=F32,
                             precision=lax.Precision.HIGHEST)
    scores = _sigmoid(logits)
    biased = scores + bias_ref[...]
    neg = -jnp.inf

    g3 = biased.reshape(N_GROUPS, GROUP_SIZE, tm)
    i3 = lax.broadcasted_iota(I32, g3.shape, 1)
    m1, a1 = _first_max(g3, i3, GROUP_SIZE, 1)
    m2 = jnp.max(jnp.where(i3 == a1, neg, g3), axis=1, keepdims=True)
    gs = (m1 + m2).reshape(N_GROUPS, tm)
    ig = _iota2(gs.shape, 0)
    gmask = jnp.zeros(gs.shape, jnp.bool_)
    for _ in range(TOPK_GROUPS):
        _, a = _first_max(gs, ig, N_GROUPS, 0)
        pick = ig == a
        gmask = jnp.logical_or(gmask, pick)
        gs = jnp.where(pick, neg, gs)
    emask = jnp.broadcast_to(gmask.reshape(N_GROUPS, 1, tm), (N_GROUPS, GROUP_SIZE, tm)).reshape(e, tm)

    cand = jnp.where(emask, biased, neg)
    ie = _iota2((e, tm), 0)
    sel_all = jnp.zeros((e, tm), jnp.bool_)
    w_rows, picks = [], []
    for _ in range(TOP_K):
        _, a = _first_max(cand, ie, e, 0)
        pick = ie == a
        picks.append(pick)
        w_rows.append(jnp.sum(jnp.where(pick, scores, 0.0), axis=0, keepdims=True))
        sel_all = jnp.logical_or(sel_all, pick)
        cand = jnp.where(pick, neg, cand)
    w_sum = w_rows[0]
    for wr in w_rows[1:]:
        w_sum = w_sum + wr
    wts = jnp.concatenate(w_rows, axis=0) / w_sum * ROUTED_SCALE

    sel = sel_all.astype(BF16)
    in_expert = _dot(sel, upper_ref[...])
    n_tile = jnp.sum(sel_all.astype(F32), axis=1, keepdims=True)
    lower = (_iota2((e, e), 1) < _iota2((e, e), 0)).astype(BF16)
    expert_off = _dot_exact_lhs(lower, jnp.broadcast_to(n_tile, (e, LANES)))[:, 0:1]
    place = in_expert + expert_off
    pos = jnp.concatenate([jnp.sum(jnp.where(pk, place, 0.0), axis=0, keepdims=True) for pk in picks], axis=0)
    pos_ref[...] = pos.astype(I32)
    before_ref[...] = jnp.broadcast_to(cnt_scr[...], before_ref.shape).astype(I32)
    ntile_ref[...] = jnp.broadcast_to(n_tile, ntile_ref.shape).astype(I32)
    cnt_scr[...] = cnt_scr[...] + n_tile
    wts_ref[...] = wts


def _router(x1, g2, sc2, sh2, w_router_t, bias_col, tm):
    s, d = x1.shape
    e = N_EXPERTS
    nt = s // tm
    upper = (jnp.arange(tm)[:, None] < jnp.arange(tm)[None, :]).astype(BF16)
    const = lambda shape: pl.BlockSpec(shape, lambda i: (0, 0))
    per_tile = pl.BlockSpec((None, e, LANES), lambda i: (i, 0, 0))
    return pl.pallas_call(
        functools.partial(_router_kernel, tm=tm),
        grid=(nt,),
        in_specs=[pl.BlockSpec((tm, d), lambda i: (i, 0)),
                  const((1, d)), const((1, d)), const((1, d)),
                  const((e, d)), const((e, 1)), const((tm, tm))],
        out_specs=[pl.BlockSpec((TOP_K, tm), lambda i: (0, i)),
                   pl.BlockSpec((TOP_K, tm), lambda i: (0, i)),
                   per_tile, per_tile],
        out_shape=[jax.ShapeDtypeStruct((TOP_K, s), I32), jax.ShapeDtypeStruct((TOP_K, s), F32),
                   jax.ShapeDtypeStruct((nt, e, LANES), I32), jax.ShapeDtypeStruct((nt, e, LANES), I32)],
        scratch_shapes=[pltpu.VMEM((e, 1), F32)],
        compiler_params=_params(("arbitrary",)),
    )(x1, g2, sc2, sh2, w_router_t, bias_col, upper)


LONG_RUN = 64


def _run_sizes(limit):
    return [1 << b for b in range(limit.bit_length() - 1, -1, -1)]


def _for_each_run(tile, run_refs, tm, make_copy, fn, unroll=False, enable=None):
    run_len_ref, run_off_ref, run_dst_ref = run_refs

    def per_expert(ex, carry):
        n = run_len_ref[tile * N_EXPERTS + ex]
        off = run_off_ref[tile * N_EXPERTS + ex]
        dst = run_dst_ref[tile * N_EXPERTS + ex]
        def pieces(sizes):
            for size in sizes:
                done = n & (-2 * size)

                take = (n & size) != 0
                if enable is not None:
                    take = jnp.logical_and(take, enable)

                @pl.when(take)
                def _(done=done, size=size):
                    fn(make_copy(off + done, dst + done, size))

        sizes = _run_sizes(tm)
        pieces([size for size in sizes if size < LONG_RUN])

        @pl.when(n >= LONG_RUN)
        def _():
            pieces([size for size in sizes if size >= LONG_RUN])

        return carry

    lax.fori_loop(0, N_EXPERTS, per_expert, 0, unroll=unroll)


def _slot_rows(slot, n_slots):
    return pl.ds(pl.multiple_of(slot * SUBLANES, SUBLANES), n_slots * SUBLANES)


def _dispatch_kernel(run_len_ref, run_off_ref, run_dst_ref, pad_lo_ref, pad_hi_ref, pos_ref, h_ref, xs_ref, stage, zero_scr,
                     sem, pad_sem, *, tm, rows_per_pass):
    step = pl.program_id(0)
    na = TOP_K * tm
    d = h_ref.shape[1]
    half = d // 2
    n_words = half // LANES

    def pad_copy(slot, n_slots):
        return pltpu.make_async_copy(zero_scr.at[pl.ds(0, n_slots * SUBLANES), :],
                                     xs_ref.at[_slot_rows(slot, n_slots), :], pad_sem)

    def for_each_pad(fn):
        def per_expert(ex, carry):
            slot = pad_lo_ref[ex]
            n = pad_hi_ref[ex] - slot
            for size in _run_sizes(EXPERT_BLOCK - 1):
                take = (n & size) != 0

                @pl.when(take)
                def _(slot=slot, size=size):
                    fn(pad_copy(slot, size))

                slot = slot + jnp.where(take, size, 0)
            return carry
        lax.fori_loop(0, N_EXPERTS, per_expert, 0)

    @pl.when(step == 0)
    def _():
        zero_scr[...] = jnp.zeros_like(zero_scr)
        for_each_pad(lambda cp: cp.start())

    buf = step % 2
    last = pl.num_programs(0) - 1
    runs = (run_len_ref, run_off_ref, run_dst_ref)

    def run_copy_from(which):
        def run_copy(tile_slot, sorted_slot, n_slots):
            return pltpu.make_async_copy(stage.at[which, _slot_rows(tile_slot, n_slots), :],
                                         xs_ref.at[_slot_rows(sorted_slot, n_slots), :], sem.at[which])
        return run_copy

    def wait_tile(which):
        pltpu.make_async_copy(stage.at[which], xs_ref.at[pl.ds(0, na * SUBLANES), :], sem.at[which]).wait()

    _for_each_run(jnp.maximum(step - 1, 0), runs, tm, run_copy_from(1 - buf), lambda cp: cp.start(),
                  unroll=True, enable=step > 0)

    pos = pos_ref[...]
    h = h_ref[...]
    for a0 in range(0, na, rows_per_pass):
        slot_id = a0 + _iota2((rows_per_pass, tm), 0)
        hit = pos[0:1, :] == slot_id
        for k in range(1, TOP_K):
            hit = jnp.logical_or(hit, pos[k:k + 1, :] == slot_id)
        rows = _dot(hit.astype(BF16), h)
        for i in range(n_words):
            word = _pack_halves(rows[:, i * LANES:(i + 1) * LANES], rows[:, half + i * LANES:half + (i + 1) * LANES])
            stage[buf, pl.ds(a0 * SUBLANES + i, rows_per_pass, stride=SUBLANES), :] = word

    @pl.when(step > 0)
    def _():
        wait_tile(1 - buf)

    @pl.when(step == last)
    def _():
        _for_each_run(step, runs, tm, run_copy_from(buf), lambda cp: cp.start())
        wait_tile(buf)

    @pl.when(step == 0)
    def _():
        for_each_pad(lambda cp: cp.wait())


def _dispatch(runs, pad_lo, pad_hi, pos_t, h2, n_slots, tm, rows_per_pass):
    s, d = h2.shape
    assert (d // 2) % LANES == 0 and (d // 2) // LANES == SUBLANES, "one token row must pack into one (8, 128) tile"
    na = TOP_K * tm
    return pl.pallas_call(
        functools.partial(_dispatch_kernel, tm=tm, rows_per_pass=min(rows_per_pass, na)),
        grid_spec=pltpu.PrefetchScalarGridSpec(
            num_scalar_prefetch=5,
            grid=(s // tm,),
            in_specs=[pl.BlockSpec((TOP_K, tm), lambda i, *_: (0, i)),
                      pl.BlockSpec((tm, d), lambda i, *_: (i, 0))],
            out_specs=pl.BlockSpec(memory_space=pl.ANY),
            scratch_shapes=[pltpu.VMEM((2, na * SUBLANES, LANES), U32),
                            pltpu.VMEM((EXPERT_BLOCK // 2 * SUBLANES, LANES), U32),
                            pltpu.SemaphoreType.DMA((2,)), pltpu.SemaphoreType.DMA(())]),
        out_shape=jax.ShapeDtypeStruct((n_slots * SUBLANES, LANES), U32),
        compiler_params=_params(("arbitrary",), has_side_effects=True, disable_bounds_checks=True),
    )(*runs, pad_lo, pad_hi, pos_t, h2)


def _expert_kernel(be_ref, nu_ref, next_ref, par_ref, valid_ref, x_ref, wg_hbm, wu_hbm, wd_hbm, y_ref, wg_f32, wu_f32, wd_f32,
                   wg_scr, wu_scr, wd_scr, sem):
    b = pl.program_id(0)
    bm = EXPERT_BLOCK
    active = b < nu_ref[0]
    new_expert = jnp.logical_or(b == 0, be_ref[b] != be_ref[jnp.maximum(b - 1, 0)])

    def weight_copies(ex, which):
        return [pltpu.make_async_copy(src.at[ex], dst.at[which], sem.at[which])
                for src, dst in ((wg_hbm, wg_f32), (wu_hbm, wu_f32), (wd_hbm, wd_f32))]

    @pl.when(jnp.logical_and(active, new_expert))
    def _():
        which = par_ref[b]

        @pl.when(b == 0)
        def _():
            for cp in weight_copies(be_ref[b], which):
                cp.start()

        for cp in weight_copies(be_ref[b], which):
            cp.wait()
        wg_scr[...] = wg_f32[which].astype(BF16)
        wu_scr[...] = wu_f32[which].astype(BF16)
        wd_scr[...] = wd_f32[which].astype(BF16)
        nb = next_ref[b]

        @pl.when(nb < nu_ref[0])
        def _():
            for cp in weight_copies(be_ref[nb], 1 - which):
                cp.start()

    def ffn_rows(rows):
        los, his = [], []
        for i in range(SUBLANES):
            lo, hi = _unpack_halves(x_ref[pl.ds(i, rows, stride=SUBLANES), :])
            los.append(lo.astype(BF16))
            his.append(hi.astype(BF16))
        xb = jnp.concatenate(los + his, axis=1)
        hid = _silu(_dot(xb, wg_scr[...])) * _dot(xb, wu_scr[...])
        y = _dot(hid.astype(BF16), wd_scr[...])
        half = y.shape[1] // 2
        for i in range(SUBLANES):
            word = _pack_halves(_round_bf16(y[:, i * LANES:(i + 1) * LANES]),
                                _round_bf16(y[:, half + i * LANES:half + (i + 1) * LANES]))
            y_ref[pl.ds(i, rows, stride=SUBLANES), :] = word

    half_full = valid_ref[b] <= bm // 2

    @pl.when(jnp.logical_and(active, jnp.logical_not(half_full)))
    def _():
        ffn_rows(bm)

    @pl.when(jnp.logical_and(active, half_full))
    def _():
        ffn_rows(bm // 2)
        y_ref[bm // 2 * SUBLANES:bm * SUBLANES, :] = jnp.zeros((bm // 2 * SUBLANES, LANES), U32)


def _experts(block_e, n_used, next_block, parity, valid, xs, w_gate, w_up, w_down):
    d, ff = w_gate.shape[1], w_gate.shape[2]
    bm = EXPERT_BLOCK
    n_blocks = xs.shape[0] // (bm * SUBLANES)
    blk = lambda b, be, nu, *_: (jnp.minimum(b, nu[0] - 1), 0)
    hbm = pl.BlockSpec(memory_space=pl.ANY)
    return pl.pallas_call(
        _expert_kernel,
        grid_spec=pltpu.PrefetchScalarGridSpec(
            num_scalar_prefetch=5,
            grid=(n_blocks,),
            in_specs=[pl.BlockSpec((bm * SUBLANES, LANES), blk), hbm, hbm, hbm],
            out_specs=pl.BlockSpec((bm * SUBLANES, LANES), blk),
            scratch_shapes=[pltpu.VMEM((2, d, ff), F32), pltpu.VMEM((2, d, ff), F32), pltpu.VMEM((2, ff, d), F32),
                            pltpu.VMEM((d, ff), BF16), pltpu.VMEM((d, ff), BF16), pltpu.VMEM((ff, d), BF16),
                            pltpu.SemaphoreType.DMA((2,))]),
        out_shape=jax.ShapeDtypeStruct(xs.shape, U32),
        compiler_params=_params(("arbitrary",)),
    )(block_e, n_used, next_block, parity, valid, xs, w_gate, w_up, w_down)


def _combine_kernel(run_len_ref, run_off_ref, run_src_ref, ys_ref, h_ref, x1_ref, pos_ref, wts_ref, wg_ref, wu_ref, wd_ref, gt_ref, gf_ref,
                    o_ref, stage, sem, *, tm, rows_per_pass):
    step = pl.program_id(0)
    na = TOP_K * tm
    buf = step % 2

    last = pl.num_programs(0) - 1

    def fetch_tile(tile, which, unroll=False):
        def run_copy(tile_slot, sorted_slot, n_slots):
            return pltpu.make_async_copy(ys_ref.at[_slot_rows(sorted_slot, n_slots), :],
                                         stage.at[which, _slot_rows(tile_slot, n_slots), :], sem.at[which])
        _for_each_run(tile, (run_len_ref, run_off_ref, run_src_ref), tm, run_copy, lambda cp: cp.start(), unroll)

    def wait_tile(which):
        pltpu.make_async_copy(ys_ref.at[pl.ds(0, na * SUBLANES), :], stage.at[which], sem.at[which]).wait()

    @pl.when(step == 0)
    def _():
        fetch_tile(step, buf)

    fetch_tile(jnp.minimum(step + 1, last), 1 - buf, unroll=True)

    hb = h_ref[...]
    hid = _silu(_dot(hb, wg_ref[...])) * _dot(hb, wu_ref[...])
    acc = _dot(hid.astype(BF16), wd_ref[...])

    wait_tile(buf)

    pos = pos_ref[...]
    wts = wts_ref[...]
    for a0 in range(0, na, rows_per_pass):
        los, his = [], []
        for i in range(SUBLANES):
            lo, hi = _unpack_halves(stage[buf, pl.ds(a0 * SUBLANES + i, rows_per_pass, stride=SUBLANES), :])
            los.append(lo.astype(BF16))
            his.append(hi.astype(BF16))
        y_rows = jnp.concatenate(los + his, axis=1)
        slot_id = a0 + _iota2((rows_per_pass, tm), 0)
        wmat = jnp.zeros((rows_per_pass, tm), F32)
        for k in range(TOP_K):
            wmat = wmat + jnp.where(pos[k:k + 1, :] == slot_id, wts[k:k + 1, :], 0.0)
        acc = acc + _dg(wmat.astype(BF16), y_rows, TN)
    x2 = x1_ref[...] + gt_ref[...] * acc
    o_ref[...] = _rms(x2, NORM_EPS) * gf_ref[...]

    @pl.when(step == last)
    def _():
        wait_tile(1 - buf)


def _combine(runs, ys, h2, x1, pos_t, wts_t, w_gate, w_up, w_down, gt2, gf, tm, rows_per_pass):
    s, d = x1.shape
    ff = w_gate.shape[1]
    na = TOP_K * tm
    const = lambda shape: pl.BlockSpec(shape, lambda i, *_: (0, 0), pipeline_mode=pl.Buffered(1))
    tile = lambda cols: pl.BlockSpec((tm, cols), lambda i, *_: (i, 0))
    per_k = pl.BlockSpec((TOP_K, tm), lambda i, *_: (0, i))
    return pl.pallas_call(
        functools.partial(_combine_kernel, tm=tm, rows_per_pass=min(rows_per_pass, na)),
        grid_spec=pltpu.PrefetchScalarGridSpec(
            num_scalar_prefetch=3,
            grid=(s // tm,),
            in_specs=[pl.BlockSpec(memory_space=pl.ANY),
                      tile(d), tile(d), per_k, per_k,
                      const((d, ff)), const((d, ff)), const((ff, d)), const((1, d)), const((1, d))],
            out_specs=tile(d),
            scratch_shapes=[pltpu.VMEM((2, na * SUBLANES, LANES), U32), pltpu.SemaphoreType.DMA((2,))]),
        out_shape=jax.ShapeDtypeStruct((s, d), F32),
        compiler_params=_params(("arbitrary",), disable_bounds_checks=True),
    )(*runs, ys, h2, x1, pos_t, wts_t, w_gate, w_up, w_down, gt2, gf)


def _mixer(x2d, mod, norm1_g, norm2_g, w_in, lb, hgrn_onorm_g, gdn_conv_w, gdn_a_log, gdn_dt_bias, gdn_onorm_g,
           w_branch_hgrn, w_branch_gdn, w_out, tiles):
    d = x2d.shape[1]
    sh1, sc1, gt1, sh2, sc2, _ = [mod[:, i * d:(i + 1) * d] for i in range(6)]
    key = HEADS * HEAD_DIM
    small0 = 4 * key + 3 * key
    small1 = small0 + 2 * HEADS
    w_in_t = jnp.swapaxes(w_in, 1, 2)
    w_main_t = _wprep(w_in_t, small0, small1, tiles["wprep_tn"])
    w_small_t = w_in_t[0, small0:small1, :].astype(BF16)
    proj, ab_t = _inproj(x2d, norm1_g, sc1, sh1, w_main_t, w_small_t, tiles["in_tm"], tiles["in_tn"])
    o_a = _hgrn(proj, lb, hgrn_onorm_g, tiles["mix_ts"])
    u, wqd, ku, attn, dl = _gdn_prep(proj, gdn_conv_w, ab_t, gdn_a_log, gdn_dt_bias, tiles["prep_ts"])
    o_b = _gdn_scan(u, wqd, ku, attn, dl, proj, gdn_onorm_g, tiles["mix_ts"])
    return _merge(o_a, o_b, proj, x2d, w_branch_hgrn.astype(BF16), w_branch_gdn.astype(BF16),
                  w_out.astype(BF16), gt1, norm2_g, sc2, sh2, tiles["merge_tm"])


def _moe(x1, h2, mod, norm2_g, normf_g, w_router, router_bias, w_exp_gate, w_exp_up, w_exp_down, w_sh_gate,
         w_sh_up, w_sh_down, tiles):
    s, d = x1.shape
    tm = tiles["moe_tm"]
    sh2, sc2, gt2 = [mod[:, i * d:(i + 1) * d] for i in (3, 4, 5)]
    pos_t, wts_t, before, ntile = _router(x1, norm2_g, sc2, sh2, w_router.T, router_bias.reshape(-1, 1), tm)
    bm = EXPERT_BLOCK
    n_blocks = -(-(s * TOP_K + N_EXPERTS * (bm - 1)) // bm)
    before = before[:, :, 0]
    ntile = ntile[:, :, 0]
    counts = before[-1] + ntile[-1]
    padded = (counts + bm - 1) // bm * bm
    earlier = jnp.arange(N_EXPERTS)[None, :] < jnp.arange(N_EXPERTS)[:, None]
    pstart = jnp.sum(jnp.where(earlier, padded[None, :], 0), axis=1).astype(I32)
    pend = pstart + padded
    block_start = jnp.arange(n_blocks, dtype=I32) * bm
    block_e = jnp.minimum(jnp.sum(pend[None, :] <= block_start[:, None], axis=1), N_EXPERTS - 1).astype(I32)
    n_used = pend[-1:] // bm
    run_off = jnp.sum(jnp.where(earlier[None], ntile[:, None, :], 0), axis=2)
    runs = (ntile.reshape(-1), run_off.reshape(-1), (before + pstart[None, :]).reshape(-1))
    xs = _dispatch(runs, pstart + counts, pend, pos_t, h2, n_blocks * bm, tm, tiles["dispatch_rows"])
    own = block_e[:, None] == jnp.arange(N_EXPERTS)[None, :]
    next_block = jnp.sum(jnp.where(own, pend[None, :], 0), axis=1) // bm
    switches = jnp.concatenate([jnp.zeros((1,), I32), (block_e[1:] != block_e[:-1]).astype(I32)])
    upto = jnp.arange(n_blocks)[None, :] <= jnp.arange(n_blocks)[:, None]
    parity = jnp.sum(jnp.where(upto, switches[None, :], 0), axis=1).astype(I32) % 2
    start_b = jnp.sum(jnp.where(own, pstart[None, :], 0), axis=1)
    count_b = jnp.sum(jnp.where(own, counts[None, :], 0), axis=1)
    valid = jnp.clip(count_b - (block_start - start_b), 0, bm).astype(I32)
    ys = _experts(block_e, n_used, next_block, parity, valid, xs, w_exp_gate, w_exp_up, w_exp_down)
    return _combine(runs, ys, h2, x1, pos_t, wts_t, w_sh_gate.astype(BF16), w_sh_up.astype(BF16),
                    w_sh_down.astype(BF16), gt2, normf_g, tm, tiles["combine_rows"])


def _tiles(s):
    pick = lambda want: min(want, s)
    return dict(ada_tn=1024, wprep_tn=512, in_tm=pick(1024), in_tn=1536, mix_ts=pick(512), prep_ts=pick(2048),
                merge_tm=pick(512), moe_tm=pick(256), dispatch_rows=512, combine_rows=256)


def kernel(x, c, w_ada, b_ada, norm1_g, norm2_g, w_in, hgrn_lb_table, hgrn_onorm_g, gdn_conv_w, gdn_a_log, gdn_dt_bias, gdn_onorm_g, w_branch_hgrn, w_branch_gdn, w_out, w_router, router_bias, w_exp_gate, w_exp_up, w_exp_down, w_sh_gate, w_sh_up, w_sh_down, normf_g):
    b, s, d = x.shape
    assert b == 1 and w_ada.shape[0] == 1, "one sequence, one layer"
    tiles = _tiles(s)
    lb = jnp.sum(jax.nn.softmax(hgrn_lb_table.astype(F32), axis=0)[0:1], axis=0, keepdims=True)
    mod = _ada(c, w_ada[0], b_ada[0], tiles["ada_tn"])
    row = lambda v: v.reshape(1, -1)
    x1, h2 = _mixer(x[0], mod, row(norm1_g[0]), row(norm2_g[0]), w_in, lb, row(hgrn_onorm_g[0]), gdn_conv_w[0],
                    gdn_a_log[0], gdn_dt_bias[0], row(gdn_onorm_g[0]), w_branch_hgrn[0], w_branch_gdn[0], w_out[0],
                    tiles)
    out = _moe(x1, h2, mod, row(norm2_g[0]), row(normf_g), w_router[0], router_bias[0], w_exp_gate[0],
               w_exp_up[0], w_exp_down[0], w_sh_gate[0], w_sh_up[0], w_sh_down[0], tiles)
    return out[None]
```

```python
import functools

import jax
import jax.numpy as jnp
from jax import lax
from jax.experimental import pallas as pl
from jax.experimental.pallas import tpu as pltpu

F32 = jnp.float32
BF16 = jnp.bfloat16
I32 = jnp.int32
U32 = jnp.uint32

NORM_EPS = 1e-6
L2_EPS = 1e-6
HEADS = 8
HEAD_DIM = 128
CONV_WIDTH = 4
CHUNK = 64
N_EXPERTS = 64
N_GROUPS = 8
GROUP_SIZE = N_EXPERTS // N_GROUPS
TOPK_GROUPS = 4
TOP_K = 8
ROUTED_SCALE = 2.5
EXPERT_BLOCK = 512

LANES = 128
SUBLANES = 8
MXU_DIM = 256
VMEM_LIMIT = 56 * 1024 * 1024

NT = (((1,), (1,)), ((), ()))
TN = (((0,), (0,)), ((), ()))


def _params(sem, **kw):
    return pltpu.CompilerParams(dimension_semantics=sem, vmem_limit_bytes=VMEM_LIMIT, **kw)


def _dot(a, b):
    return jnp.dot(a, b, preferred_element_type=F32)


def _dg(a, b, dims):
    return lax.dot_general(a, b, dims, preferred_element_type=F32)


def _split(x):
    hi = x.astype(BF16)
    lo = (x - hi.astype(F32)).astype(BF16)
    return hi, lo


def _dot_exact_lhs(a_bf16, x, dims=None):
    hi, lo = _split(x)
    if dims is None:
        return _dot(a_bf16, hi) + _dot(a_bf16, lo)
    return _dg(a_bf16, hi, dims) + _dg(a_bf16, lo, dims)


def _sigmoid(x):
    return 0.5 * jnp.tanh(0.5 * x) + 0.5


def _silu(x):
    half = 0.5 * x
    return half * jnp.tanh(half) + half


def _rms(x, eps):
    return x * lax.rsqrt(jnp.mean(x * x, axis=-1, keepdims=True) + eps)


def _iota2(shape, dim):
    return lax.broadcasted_iota(I32, shape, dim)


def _pack_halves(lo, hi):
    lo_bits = lax.shift_right_logical(pltpu.bitcast(lo, U32), U32(16))
    hi_bits = pltpu.bitcast(hi, U32) & U32(0xFFFF0000)
    return lo_bits | hi_bits


def _unpack_halves(word):
    lo = pltpu.bitcast(lax.shift_left(word, U32(16)), F32)
    hi = pltpu.bitcast(word & U32(0xFFFF0000), F32)
    return lo, hi


def _round_bf16(x):
    return x.astype(BF16).astype(F32)


def _ada_kernel(c_ref, w_ref, b_ref, o_ref):
    cond = _silu(c_ref[...])
    o_ref[...] = jnp.sum(w_ref[...] * cond, axis=0, keepdims=True) + b_ref[...]


def _ada(c, w_ada, b_ada, tn):
    d, n = w_ada.shape
    return pl.pallas_call(
        _ada_kernel,
        grid=(n // tn,),
        in_specs=[pl.BlockSpec((d, 1), lambda j: (0, 0)),
                  pl.BlockSpec((d, tn), lambda j: (0, j)),
                  pl.BlockSpec((1, tn), lambda j: (0, j))],
        out_specs=pl.BlockSpec((1, tn), lambda j: (0, j)),
        out_shape=jax.ShapeDtypeStruct((1, n), F32),
        compiler_params=_params(("arbitrary",)),
    )(c.reshape(d, 1), w_ada, b_ada.reshape(1, n))


def _wprep_kernel(a_ref, b_ref, o_ref, *, first_shifted, shift):
    j = pl.program_id(0)

    @pl.when(j < first_shifted)
    def _():
        o_ref[...] = a_ref[...].astype(BF16)

    @pl.when(j >= first_shifted)
    def _():
        tn = a_ref.shape[0]
        o_ref[0:tn - shift, :] = a_ref[shift:tn, :].astype(BF16)
        o_ref[tn - shift:tn, :] = b_ref[...].astype(BF16)


def _wprep(w_in_t, cut0, cut1, tn):
    _, n_in, d = w_in_t.shape
    shift = cut1 - cut0
    n_out = n_in - shift
    assert cut0 % tn == 0 and n_out % tn == 0 and tn % shift == 0 and shift % (2 * SUBLANES) == 0
    return pl.pallas_call(
        functools.partial(_wprep_kernel, first_shifted=cut0 // tn, shift=shift),
        grid=(n_out // tn,),
        in_specs=[pl.BlockSpec((None, tn, d), lambda j: (0, j, 0)),
                  pl.BlockSpec((None, shift, d), lambda j: (0, (j + 1) * (tn // shift), 0))],
        out_specs=pl.BlockSpec((tn, d), lambda j: (j, 0)),
        out_shape=jax.ShapeDtypeStruct((n_out, d), BF16),
        compiler_params=_params(("arbitrary",)),
    )(w_in_t, w_in_t)


def _inproj_kernel(x_ref, g_ref, sc_ref, sh_ref, w_ref, wst_ref, proj_ref, smallt_ref, h_scr):
    @pl.when(pl.program_id(1) == 0)
    def _():
        h = _rms(x_ref[...], NORM_EPS) * g_ref[...] * (1.0 + sc_ref[...]) + sh_ref[...]
        hb = h.astype(BF16)
        h_scr[...] = hb
        smallt_ref[...] = _dg(wst_ref[...], hb, NT)

    proj_ref[...] = _dg(h_scr[...], w_ref[...], NT).astype(BF16)


def _inproj(x, g, sc, sh, w_main_t, w_small_t, tm, tn):
    s, d = x.shape
    n = w_main_t.shape[0]
    ns = w_small_t.shape[0]
    row = lambda i, j: (0, 0)
    return pl.pallas_call(
        _inproj_kernel,
        grid=(s // tm, n // tn),
        in_specs=[pl.BlockSpec((tm, d), lambda i, j: (i, 0)),
                  pl.BlockSpec((1, d), row), pl.BlockSpec((1, d), row), pl.BlockSpec((1, d), row),
                  pl.BlockSpec((tn, d), lambda i, j: (j, 0)),
                  pl.BlockSpec((ns, d), row)],
        out_specs=[pl.BlockSpec((tm, tn), lambda i, j: (i, j)),
                   pl.BlockSpec((ns, tm), lambda i, j: (0, i))],
        out_shape=[jax.ShapeDtypeStruct((s, n), BF16), jax.ShapeDtypeStruct((ns, s), F32)],
        scratch_shapes=[pltpu.VMEM((tm, d), BF16)],
        compiler_params=_params(("arbitrary", "arbitrary")),
    )(x, g, sc, sh, w_main_t, w_small_t)


def _hgrn_kernel(q_ref, f_ref, i_ref, g_ref, lb_ref, on_ref, o_ref, st_scr, *, n_chunks):
    @pl.when(pl.program_id(0) == 0)
    def _():
        st_scr[...] = jnp.zeros_like(st_scr)

    c = CHUNK
    hd = HEAD_DIM
    causal = _iota2((c, c), 1) <= _iota2((c, c), 0)
    tri = causal.astype(BF16)
    lb = lb_ref[...]
    on_g = on_ref[...]
    heads = [slice(h * hd, (h + 1) * hd) for h in range(HEADS)]

    def chunk(n, carry):
        rows = pl.ds(pl.multiple_of(n * c, c), c)
        f = lb + (1.0 - lb) * _sigmoid(f_ref[rows, :].astype(F32))
        b = _dot_exact_lhs(tri, jnp.log(f))
        k = 1.0 - f
        q = _silu(q_ref[rows, :].astype(F32)) * (hd ** -0.5)
        v = i_ref[rows, :]
        b_mid = b[c // 2:c // 2 + 1, :]
        b_last = b[c - 1:c, :]
        qa = (q * jnp.exp(b - b_mid)).astype(BF16)
        ka = (k * jnp.exp(b_mid - b)).astype(BF16)
        qi = (q * jnp.exp(b)).astype(BF16)
        ku = (k * jnp.exp(b_last - b)).astype(BF16)
        dec = jnp.exp(b_last)
        gate = on_g * _silu(g_ref[rows, :].astype(F32))
        sts = [st_scr[h] for h in range(HEADS)]
        scores = [jnp.where(causal, _dg(qa[:, sl], ka[:, sl], NT), 0.0).astype(BF16) for sl in heads]
        inter = [_dg(qi[:, sl], st.astype(BF16), NT) for sl, st in zip(heads, sts)]
        kv = [_dg(v[:, sl], ku[:, sl], TN) for sl in heads]
        for h, sl in enumerate(heads):
            st_scr[h] = dec[:, sl] * sts[h] + kv[h]
        outs = [_rms(_dot(sc, v[:, sl]) + it, NORM_EPS) for sc, sl, it in zip(scores, heads, inter)]
        o_ref[rows, :] = (jnp.concatenate(outs, axis=1) * gate).astype(BF16)
        return carry

    lax.fori_loop(0, n_chunks, chunk, 0, unroll=4)


def _hgrn(proj, lb, onorm_g, ts):
    s = proj.shape[0]
    width = HEADS * HEAD_DIM
    col = lambda blk: pl.BlockSpec((ts, width), lambda j, blk=blk: (j, blk))
    const = pl.BlockSpec((1, width), lambda j: (0, 0))
    return pl.pallas_call(
        functools.partial(_hgrn_kernel, n_chunks=ts // CHUNK),
        grid=(s // ts,),
        in_specs=[col(0), col(1), col(2), col(3), const, const],
        out_specs=pl.BlockSpec((ts, width), lambda j: (j, 0)),
        out_shape=jax.ShapeDtypeStruct((s, width), BF16),
        scratch_shapes=[pltpu.VMEM((HEADS, HEAD_DIM, HEAD_DIM), F32)],
        compiler_params=_params(("arbitrary",)),
    )(proj, proj, proj, proj, lb, jnp.tile(onorm_g, (1, HEADS)))


def _gdn_prep_kernel(q_ref, k_ref, v_ref, qp_ref, kp_ref, vp_ref, wq_ref, wk_ref, wv_ref, ab_ref, alog_ref,
                     dtb_ref, tri_ref, eye_ref, u_ref, wqd_ref, ku_ref, attn_ref, dl_ref, cat_scr, rows_scr, cols_scr,
                     *, n_chunks, ts):
    h = pl.program_id(1)
    first = pl.program_id(0) == 0
    c = CHUNK
    hd = HEAD_DIM

    def conv_silu(cur_ref, prev_ref, w_ref):
        cat_scr[0:SUBLANES, :] = jnp.where(first, 0.0, prev_ref[...].astype(F32))
        cat_scr[SUBLANES:SUBLANES + ts, :] = cur_ref[...].astype(F32)
        acc = None
        for j in range(CONV_WIDTH):
            off = SUBLANES - (CONV_WIDTH - 1) + j
            term = cat_scr[off:off + ts, :] * w_ref[j:j + 1, :]
            acc = term if acc is None else acc + term
        return _silu(acc)

    def l2n(x):
        return x * lax.rsqrt(jnp.sum(x * x, axis=-1, keepdims=True) + L2_EPS)

    q_all = l2n(conv_silu(q_ref, qp_ref, wq_ref)) * (hd ** -0.5)
    k_all = l2n(conv_silu(k_ref, kp_ref, wk_ref))
    v_all = conv_silu(v_ref, vp_ref, wv_ref)

    @pl.when(h == 0)
    def _():
        z = ab_ref[0:HEADS, :] + dtb_ref[...]
        softplus = jnp.maximum(z, 0.0) + jnp.log(1.0 + jnp.exp(-jnp.abs(z)))
        ld_rows = -jnp.exp(alog_ref[...]) * softplus
        hi, lo = _split(ld_rows)
        tri_blocks = tri_ref[...]
        w = tri_blocks.shape[0]
        spans = [slice(t0, t0 + w) for t0 in range(0, ts, w)]
        g_rows = jnp.concatenate([_dg(hi[:, sp], tri_blocks, NT) + _dg(lo[:, sp], tri_blocks, NT) for sp in spans],
                                 axis=1)
        beta_rows = _sigmoid(ab_ref[HEADS:2 * HEADS, :])
        rows_scr[...] = g_rows
        rows = jnp.concatenate([g_rows, beta_rows, jnp.zeros((LANES - 2 * HEADS, ts), F32)], axis=0)
        r_hi, r_lo = _split(rows)
        r_lo2 = (rows - r_hi.astype(F32) - r_lo.astype(F32)).astype(BF16)
        eye_w = eye_ref[...]
        for sp in spans:
            cols_scr[sp, :] = _dg(eye_w, r_hi[:, sp], NT) + _dg(eye_w, r_lo[:, sp], NT) + _dg(eye_w, r_lo2[:, sp], NT)

    lane = _iota2((ts, LANES), 1)
    cols = cols_scr[...]
    gc_all = jnp.sum(jnp.where(lane == h, cols, 0.0), axis=1, keepdims=True)
    bc_all = jnp.sum(jnp.where(lane == h + HEADS, cols, 0.0), axis=1, keepdims=True)
    g_row = rows_scr[pl.ds(h, 1), :]
    egc_all = jnp.exp(gc_all)

    r = _iota2((c, c), 0)
    cidx = _iota2((c, c), 1)
    causal = cidx <= r
    strict = cidx < r
    eye_f = (r == cidx).astype(F32)
    chunks = [slice(n * c, (n + 1) * c) for n in range(n_chunks)]

    q16 = q_all.astype(BF16)
    k16 = k_all.astype(BF16)
    kq = [_dg(jnp.concatenate([k16[sl], q16[sl]], axis=0), k16[sl], NT) for sl in chunks]
    dm = []
    for sl in chunks:
        diff = gc_all[sl] - g_row[:, sl]
        dm.append(jnp.where(causal, jnp.exp(jnp.where(causal, diff, 0.0)), 0.0))
    bm = [-jnp.where(strict, bc_all[sl] * x[0:c] * d, 0.0) for sl, x, d in zip(chunks, kq, dm)]
    p = [eye_f + b for b in bm]
    bm = [_dot(b.astype(BF16), b.astype(BF16)) for b in bm]
    for _ in range(c.bit_length() - 3):
        res = [_dot(b.astype(BF16), jnp.concatenate([b, pp], axis=1).astype(BF16)) for b, pp in zip(bm, p)]
        p = [pp + x[:, c:2 * c] for pp, x in zip(p, res)]
        bm = [x[:, 0:c] for x in res]
    p = [pp + _dot(b.astype(BF16), pp.astype(BF16)) for b, pp in zip(bm, p)]
    rhs = jnp.concatenate([v_all * bc_all, k_all * (bc_all * egc_all)], axis=1).astype(BF16)
    sol = [_dot(pp.astype(BF16), rhs[sl]) for pp, sl in zip(p, chunks)]
    qd_all = (q_all * egc_all).astype(BF16)
    for n, sl in enumerate(chunks):
        g_last = gc_all[(n + 1) * c - 1:(n + 1) * c, :]
        u_ref[sl, :] = sol[n][:, 0:hd].astype(BF16)
        wqd_ref[2 * n * c:(2 * n + 1) * c, :] = sol[n][:, hd:2 * hd].astype(BF16)
        wqd_ref[(2 * n + 1) * c:(2 * n + 2) * c, :] = qd_all[sl]
        ku_ref[sl, :] = (k_all[sl] * jnp.exp(g_last - gc_all[sl])).astype(BF16)
        attn_ref[sl, :] = (kq[n][c:2 * c] * dm[n]).astype(BF16)
        dl_ref[n:n + 1, :] = jnp.broadcast_to(jnp.exp(g_last), (1, hd))


def _gdn_prep(proj, conv_w, ab_t, a_log, dt_bias, ts):
    s = proj.shape[0]
    hd = HEAD_DIM
    c = CHUNK
    q0 = 4 * HEADS
    cur = lambda off: pl.BlockSpec((ts, hd), lambda j, h, off=off: (j, off + h))
    prev = lambda off: pl.BlockSpec((SUBLANES, hd),
                                    lambda j, h, off=off: (jnp.maximum(j * (ts // SUBLANES) - 1, 0), off + h))
    cw = lambda off: pl.BlockSpec((CONV_WIDTH, hd), lambda j, h, off=off: (0, off + h))
    per_head_scalar = pl.BlockSpec((HEADS, 1), lambda j, h: (0, 0))
    w = min(ts, MXU_DIM)
    const = pl.BlockSpec((w, w), lambda j, h: (0, 0))
    pos = jnp.arange(w)
    tri_blocks = ((pos[:, None] // c == pos[None, :] // c) & (pos[None, :] <= pos[:, None])).astype(BF16)
    eye = (pos[:, None] == pos[None, :]).astype(BF16)
    per_head = lambda rows, cols: pl.BlockSpec((None, rows, cols), lambda j, h: (h, j, 0))
    return pl.pallas_call(
        functools.partial(_gdn_prep_kernel, n_chunks=ts // c, ts=ts),
        grid=(s // ts, HEADS),
        in_specs=[cur(q0), cur(q0 + HEADS), cur(q0 + 2 * HEADS),
                  prev(q0), prev(q0 + HEADS), prev(q0 + 2 * HEADS),
                  cw(0), cw(HEADS), cw(2 * HEADS),
                  pl.BlockSpec((2 * HEADS, ts), lambda j, h: (0, j)),
                  per_head_scalar, per_head_scalar, const, const],
        out_specs=[pl.BlockSpec((ts, hd), lambda j, h: (j, h)),
                   pl.BlockSpec((2 * ts, hd), lambda j, h: (j, h)),
                   pl.BlockSpec((ts, hd), lambda j, h: (j, h)),
                   per_head(ts, c),
                   per_head(ts // c, hd)],
        out_shape=[jax.ShapeDtypeStruct((s, HEADS * hd), BF16),
                   jax.ShapeDtypeStruct((2 * s, HEADS * hd), BF16),
                   jax.ShapeDtypeStruct((s, HEADS * hd), BF16),
                   jax.ShapeDtypeStruct((HEADS, s, c), BF16),
                   jax.ShapeDtypeStruct((HEADS, s // c, hd), F32)],
        scratch_shapes=[pltpu.VMEM((ts + SUBLANES, hd), F32), pltpu.VMEM((HEADS, ts), F32),
                        pltpu.VMEM((ts, LANES), F32)],
        compiler_params=_params(("arbitrary", "arbitrary")),
    )(proj, proj, proj, proj, proj, proj, conv_w, conv_w, conv_w,
      ab_t, a_log.reshape(HEADS, 1), dt_bias.reshape(HEADS, 1), tri_blocks, eye)


def _gdn_scan_kernel(u_ref, wqd_ref, ku_ref, attn_ref, dl_ref, g_ref, on_ref, o_ref, st_scr, *, n_chunks):
    @pl.when(pl.program_id(0) == 0)
    def _():
        st_scr[...] = jnp.zeros_like(st_scr)

    c = CHUNK
    hd = HEAD_DIM
    on_g = on_ref[...]
    heads = [slice(h * hd, (h + 1) * hd) for h in range(HEADS)]

    def chunk(n, carry):
        rows = pl.ds(pl.multiple_of(n * c, c), c)
        rows2 = pl.ds(pl.multiple_of(2 * n * c, 2 * c), 2 * c)
        sts = [st_scr[h] for h in range(HEADS)]
        wq = [_dot(wqd_ref[rows2, sl], st.astype(BF16)) for sl, st in zip(heads, sts)]
        vn = [(u_ref[rows, sl].astype(F32) - x[0:c]).astype(BF16) for sl, x in zip(heads, wq)]
        upd = [_dg(ku_ref[rows, sl], v, TN) for sl, v in zip(heads, vn)]
        for h in range(HEADS):
            st_scr[h] = dl_ref[h, pl.ds(n, 1), :] * sts[h] + upd[h]
        outs = [_rms(x[c:2 * c] + _dot(attn_ref[h, rows, :], v), NORM_EPS)
                for h, (x, v) in enumerate(zip(wq, vn))]
        gate = jnp.tile(on_g, (1, HEADS)) * _silu(g_ref[rows, :].astype(F32))
        o_ref[rows, :] = (jnp.concatenate(outs, axis=1) * gate).astype(BF16)
        return carry

    lax.fori_loop(0, n_chunks, chunk, 0, unroll=4)


def _gdn_scan(u, wqd, ku, attn, dl, proj, onorm_g, ts):
    s, width = u.shape
    c = CHUNK
    gate_blk = (4 * HEADS + 3 * HEADS) * HEAD_DIM // width
    return pl.pallas_call(
        functools.partial(_gdn_scan_kernel, n_chunks=ts // c),
        grid=(s // ts,),
        in_specs=[pl.BlockSpec((ts, width), lambda j: (j, 0)),
                  pl.BlockSpec((2 * ts, width), lambda j: (j, 0)),
                  pl.BlockSpec((ts, width), lambda j: (j, 0)),
                  pl.BlockSpec((HEADS, ts, c), lambda j: (0, j, 0)),
                  pl.BlockSpec((HEADS, ts // c, HEAD_DIM), lambda j: (0, j, 0)),
                  pl.BlockSpec((ts, width), lambda j: (j, gate_blk)),
                  pl.BlockSpec((1, HEAD_DIM), lambda j: (0, 0))],
        out_specs=pl.BlockSpec((ts, width), lambda j: (j, 0)),
        out_shape=jax.ShapeDtypeStruct((s, width), BF16),
        scratch_shapes=[pltpu.VMEM((HEADS, HEAD_DIM, HEAD_DIM), F32)],
        compiler_params=_params(("arbitrary",)),
    )(u, wqd, ku, attn, dl, proj, onorm_g)


def _merge_kernel(oa_ref, ob_ref, mga_ref, mgb_ref, x_ref, wa_ref, wb_ref, wo_ref, gt_ref, g2_ref, sc_ref,
                  sh_ref, x1_ref, h2_ref):
    ya = _dot(oa_ref[...], wa_ref[...])
    yb = _dot(ob_ref[...], wb_ref[...])
    merged = _sigmoid(mga_ref[...].astype(F32)) * ya + _sigmoid(mgb_ref[...].astype(F32)) * yb
    x1 = x_ref[...] + gt_ref[...] * _dot(merged.astype(BF16), wo_ref[...])
    x1_ref[...] = x1
    h2 = _rms(x1, NORM_EPS) * g2_ref[...] * (1.0 + sc_ref[...]) + sh_ref[...]
    h2_ref[...] = h2.astype(BF16)


def _merge(o_a, o_b, proj, x, w_a, w_b, w_o, gt1, g2, sc2, sh2, tm):
    s, d = x.shape
    dv = o_a.shape[1]
    mg0 = (8 * HEADS * HEAD_DIM) // d
    const = lambda shape: pl.BlockSpec(shape, lambda i: (0, 0), pipeline_mode=pl.Buffered(1))
    return pl.pallas_call(
        _merge_kernel,
        grid=(s // tm,),
        in_specs=[pl.BlockSpec((tm, dv), lambda i: (i, 0)),
                  pl.BlockSpec((tm, dv), lambda i: (i, 0)),
                  pl.BlockSpec((tm, d), lambda i: (i, mg0)),
                  pl.BlockSpec((tm, d), lambda i: (i, mg0 + 1)),
                  pl.BlockSpec((tm, d), lambda i: (i, 0)),
                  const((dv, d)), const((dv, d)), const((d, d)),
                  const((1, d)), const((1, d)), const((1, d)), const((1, d))],
        out_specs=[pl.BlockSpec((tm, d), lambda i: (i, 0)), pl.BlockSpec((tm, d), lambda i: (i, 0))],
        out_shape=[jax.ShapeDtypeStruct((s, d), F32), jax.ShapeDtypeStruct((s, d), BF16)],
        compiler_params=_params(("arbitrary",)),
    )(o_a, o_b, proj, proj, x, w_a, w_b, w_o, gt1, g2, sc2, sh2)


def _first_max(vals, iota, size, axis):
    m = jnp.max(vals, axis=axis, keepdims=True)
    idx = jnp.min(jnp.where(vals == m, iota, size), axis=axis, keepdims=True)
    return m, idx


def _router_kernel(x1_ref, g2_ref, sc_ref, sh_ref, wrt_ref, bias_ref, upper_ref, pos_ref, wts_ref, before_ref,
                   ntile_ref, cnt_scr, *, tm):
    @pl.when(pl.program_id(0) == 0)
    def _():
        cnt_scr[...] = jnp.zeros_like(cnt_scr)

    e = N_EXPERTS
    h2 = _rms(x1_ref[...], NORM_EPS) * g2_ref[...] * (1.0 + sc_ref[...]) + sh_ref[...]
    logits = lax.dot_general(wrt_ref[...], h2, NT, preferred_element_type=F32,
                             precision=lax.Precision.HIGHEST)
    scores = _sigmoid(logits)
    biased = scores + bias_ref[...]
    neg = -jnp.inf

    g3 = biased.reshape(N_GROUPS, GROUP_SIZE, tm)
    i3 = lax.broadcasted_iota(I32, g3.shape, 1)
    m1, a1 = _first_max(g3, i3, GROUP_SIZE, 1)
    m2 = jnp.max(jnp.where(i3 == a1, neg, g3), axis=1, keepdims=True)
    gs = (m1 + m2).reshape(N_GROUPS, tm)
    ig = _iota2(gs.shape, 0)
    gmask = jnp.zeros(gs.shape, jnp.bool_)
    for _ in range(TOPK_GROUPS):
        _, a = _first_max(gs, ig, N_GROUPS, 0)
        pick = ig == a
        gmask = jnp.logical_or(gmask, pick)
        gs = jnp.where(pick, neg, gs)
    emask = jnp.broadcast_to(gmask.reshape(N_GROUPS, 1, tm), (N_GROUPS, GROUP_SIZE, tm)).reshape(e, tm)

    cand = jnp.where(emask, biased, neg)
    ie = _iota2((e, tm), 0)
    sel_all = jnp.zeros((e, tm), jnp.bool_)
    w_rows, picks = [], []
    for _ in range(TOP_K):
        _, a = _first_max(cand, ie, e, 0)
        pick = ie == a
        picks.append(pick)
        w_rows.append(jnp.sum(jnp.where(pick, scores, 0.0), axis=0, keepdims=True))
        sel_all = jnp.logical_or(sel_all, pick)
        cand = jnp.where(pick, neg, cand)
    w_sum = w_rows[0]
    for wr in w_rows[1:]:
        w_sum = w_sum + wr
    wts = jnp.concatenate(w_rows, axis=0) / w_sum * ROUTED_SCALE

    sel = sel_all.astype(BF16)
    in_expert = _dot(sel, upper_ref[...])
    n_tile = jnp.sum(sel_all.astype(F32), axis=1, keepdims=True)
    lower = (_iota2((e, e), 1) < _iota2((e, e), 0)).astype(BF16)
    expert_off = _dot_exact_lhs(lower, jnp.broadcast_to(n_tile, (e, LANES)))[:, 0:1]
    place = in_expert + expert_off
    pos = jnp.concatenate([jnp.sum(jnp.where(pk, place, 0.0), axis=0, keepdims=True) for pk in picks], axis=0)
    pos_ref[...] = pos.astype(I32)
    before_ref[...] = jnp.broadcast_to(cnt_scr[...], before_ref.shape).astype(I32)
    ntile_ref[...] = jnp.broadcast_to(n_tile, ntile_ref.shape).astype(I32)
    cnt_scr[...] = cnt_scr[...] + n_tile
    wts_ref[...] = wts


def _router(x1, g2, sc2, sh2, w_router_t, bias_col, tm):
    s, d = x1.shape
    e = N_EXPERTS
    nt = s // tm
    upper = (jnp.arange(tm)[:, None] < jnp.arange(tm)[None, :]).astype(BF16)
    const = lambda shape: pl.BlockSpec(shape, lambda i: (0, 0))
    per_tile = pl.BlockSpec((None, e, LANES), lambda i: (i, 0, 0))
    return pl.pallas_call(
        functools.partial(_router_kernel, tm=tm),
        grid=(nt,),
        in_specs=[pl.BlockSpec((tm, d), lambda i: (i, 0)),
                  const((1, d)), const((1, d)), const((1, d)),
                  const((e, d)), const((e, 1)), const((tm, tm))],
        out_specs=[pl.BlockSpec((TOP_K, tm), lambda i: (0, i)),
                   pl.BlockSpec((TOP_K, tm), lambda i: (0, i)),
                   per_tile, per_tile],
        out_shape=[jax.ShapeDtypeStruct((TOP_K, s), I32), jax.ShapeDtypeStruct((TOP_K, s), F32),
                   jax.ShapeDtypeStruct((nt, e, LANES), I32), jax.ShapeDtypeStruct((nt, e, LANES), I32)],
        scratch_shapes=[pltpu.VMEM((e, 1), F32)],
        compiler_params=_params(("arbitrary",)),
    )(x1, g2, sc2, sh2, w_router_t, bias_col, upper)


LONG_RUN = 64


def _run_sizes(limit):
    return [1 << b for b in range(limit.bit_length() - 1, -1, -1)]


def _for_each_run(tile, run_refs, tm, make_copy, fn, unroll=False, enable=None):
    run_len_ref, run_off_ref, run_dst_ref = run_refs

    def per_expert(ex, carry):
        n = run_len_ref[tile * N_EXPERTS + ex]
        off = run_off_ref[tile * N_EXPERTS + ex]
        dst = run_dst_ref[tile * N_EXPERTS + ex]
        def pieces(sizes):
            for size in sizes:
                done = n & (-2 * size)

                take = (n & size) != 0
                if enable is not None:
                    take = jnp.logical_and(take, enable)

                @pl.when(take)
                def _(done=done, size=size):
                    fn(make_copy(off + done, dst + done, size))

        sizes = _run_sizes(tm)
        pieces([size for size in sizes if size < LONG_RUN])

        @pl.when(n >= LONG_RUN)
        def _():
            pieces([size for size in sizes if size >= LONG_RUN])

        return carry

    lax.fori_loop(0, N_EXPERTS, per_expert, 0, unroll=unroll)


def _slot_rows(slot, n_slots):
    return pl.ds(pl.multiple_of(slot * SUBLANES, SUBLANES), n_slots * SUBLANES)


def _dispatch_kernel(run_len_ref, run_off_ref, run_dst_ref, pad_lo_ref, pad_hi_ref, pos_ref, h_ref, xs_ref, stage, zero_scr,
                     sem, pad_sem, *, tm, rows_per_pass):
    step = pl.program_id(0)
    na = TOP_K * tm
    d = h_ref.shape[1]
    half = d // 2
    n_words = half // LANES

    def pad_copy(slot, n_slots):
        return pltpu.make_async_copy(zero_scr.at[pl.ds(0, n_slots * SUBLANES), :],
                                     xs_ref.at[_slot_rows(slot, n_slots), :], pad_sem)

    def for_each_pad(fn):
        def per_expert(ex, carry):
            slot = pad_lo_ref[ex]
            n = pad_hi_ref[ex] - slot
            for size in _run_sizes(EXPERT_BLOCK - 1):
                take = (n & size) != 0

                @pl.when(take)
                def _(slot=slot, size=size):
                    fn(pad_copy(slot, size))

                slot = slot + jnp.where(take, size, 0)
            return carry
        lax.fori_loop(0, N_EXPERTS, per_expert, 0)

    @pl.when(step == 0)
    def _():
        zero_scr[...] = jnp.zeros_like(zero_scr)
        for_each_pad(lambda cp: cp.start())

    buf = step % 2
    last = pl.num_programs(0) - 1
    runs = (run_len_ref, run_off_ref, run_dst_ref)

    def run_copy_from(which):
        def run_copy(tile_slot, sorted_slot, n_slots):
            return pltpu.make_async_copy(stage.at[which, _slot_rows(tile_slot, n_slots), :],
                                         xs_ref.at[_slot_rows(sorted_slot, n_slots), :], sem.at[which])
        return run_copy

    def wait_tile(which):
        pltpu.make_async_copy(stage.at[which], xs_ref.at[pl.ds(0, na * SUBLANES), :], sem.at[which]).wait()

    _for_each_run(jnp.maximum(step - 1, 0), runs, tm, run_copy_from(1 - buf), lambda cp: cp.start(),
                  unroll=True, enable=step > 0)

    pos = pos_ref[...]
    h = h_ref[...]
    for a0 in range(0, na, rows_per_pass):
        slot_id = a0 + _iota2((rows_per_pass, tm), 0)
        hit = pos[0:1, :] == slot_id
        for k in range(1, TOP_K):
            hit = jnp.logical_or(hit, pos[k:k + 1, :] == slot_id)
        rows = _dot(hit.astype(BF16), h)
        for i in range(n_words):
            word = _pack_halves(rows[:, i * LANES:(i + 1) * LANES], rows[:, half + i * LANES:half + (i + 1) * LANES])
            stage[buf, pl.ds(a0 * SUBLANES + i, rows_per_pass, stride=SUBLANES), :] = word

    @pl.when(step > 0)
    def _():
        wait_tile(1 - buf)

    @pl.when(step == last)
    def _():
        _for_each_run(step, runs, tm, run_copy_from(buf), lambda cp: cp.start())
        wait_tile(buf)

    @pl.when(step == 0)
    def _():
        for_each_pad(lambda cp: cp.wait())


def _dispatch(runs, pad_lo, pad_hi, pos_t, h2, n_slots, tm, rows_per_pass):
    s, d = h2.shape
    assert (d // 2) % LANES == 0 and (d // 2) // LANES == SUBLANES, "one token row must pack into one (8, 128) tile"
    na = TOP_K * tm
    return pl.pallas_call(
        functools.partial(_dispatch_kernel, tm=tm, rows_per_pass=min(rows_per_pass, na)),
        grid_spec=pltpu.PrefetchScalarGridSpec(
            num_scalar_prefetch=5,
            grid=(s // tm,),
            in_specs=[pl.BlockSpec((TOP_K, tm), lambda i, *_: (0, i)),
                      pl.BlockSpec((tm, d), lambda i, *_: (i, 0))],
            out_specs=pl.BlockSpec(memory_space=pl.ANY),
            scratch_shapes=[pltpu.VMEM((2, na * SUBLANES, LANES), U32),
                            pltpu.VMEM((EXPERT_BLOCK // 2 * SUBLANES, LANES), U32),
                            pltpu.SemaphoreType.DMA((2,)), pltpu.SemaphoreType.DMA(())]),
        out_shape=jax.ShapeDtypeStruct((n_slots * SUBLANES, LANES), U32),
        compiler_params=_params(("arbitrary",), has_side_effects=True, disable_bounds_checks=True),
    )(*runs, pad_lo, pad_hi, pos_t, h2)


def _expert_kernel(be_ref, nu_ref, next_ref, par_ref, valid_ref, x_ref, wg_hbm, wu_hbm, wd_hbm, y_ref, wg_f32, wu_f32, wd_f32,
                   wg_scr, wu_scr, wd_scr, sem):
    b = pl.program_id(0)
    bm = EXPERT_BLOCK
    active = b < nu_ref[0]
    new_expert = jnp.logical_or(b == 0, be_ref[b] != be_ref[jnp.maximum(b - 1, 0)])

    def weight_copies(ex, which):
        return [pltpu.make_async_copy(src.at[ex], dst.at[which], sem.at[which])
                for src, dst in ((wg_hbm, wg_f32), (wu_hbm, wu_f32), (wd_hbm, wd_f32))]

    @pl.when(jnp.logical_and(active, new_expert))
    def _():
        which = par_ref[b]

        @pl.when(b == 0)
        def _():
            for cp in weight_copies(be_ref[b], which):
                cp.start()

        for cp in weight_copies(be_ref[b], which):
            cp.wait()
        wg_scr[...] = wg_f32[which].astype(BF16)
        wu_scr[...] = wu_f32[which].astype(BF16)
        wd_scr[...] = wd_f32[which].astype(BF16)
        nb = next_ref[b]

        @pl.when(nb < nu_ref[0])
        def _():
            for cp in weight_copies(be_ref[nb], 1 - which):
                cp.start()

    def ffn_rows(rows):
        los, his = [], []
        for i in range(SUBLANES):
            lo, hi = _unpack_halves(x_ref[pl.ds(i, rows, stride=SUBLANES), :])
            los.append(lo.astype(BF16))
            his.append(hi.astype(BF16))
        xb = jnp.concatenate(los + his, axis=1)
        hid = _silu(_dot(xb, wg_scr[...])) * _dot(xb, wu_scr[...])
        y = _dot(hid.astype(BF16), wd_scr[...])
        half = y.shape[1] // 2
        for i in range(SUBLANES):
            word = _pack_halves(_round_bf16(y[:, i * LANES:(i + 1) * LANES]),
                                _round_bf16(y[:, half + i * LANES:half + (i + 1) * LANES]))
            y_ref[pl.ds(i, rows, stride=SUBLANES), :] = word

    half_full = valid_ref[b] <= bm // 2

    @pl.when(jnp.logical_and(active, jnp.logical_not(half_full)))
    def _():
        ffn_rows(bm)

    @pl.when(jnp.logical_and(active, half_full))
    def _():
        ffn_rows(bm // 2)
        y_ref[bm // 2 * SUBLANES:bm * SUBLANES, :] = jnp.zeros((bm // 2 * SUBLANES, LANES), U32)


def _experts(block_e, n_used, next_block, parity, valid, xs, w_gate, w_up, w_down):
    d, ff = w_gate.shape[1], w_gate.shape[2]
    bm = EXPERT_BLOCK
    n_blocks = xs.shape[0] // (bm * SUBLANES)
    blk = lambda b, be, nu, *_: (jnp.minimum(b, nu[0] - 1), 0)
    hbm = pl.BlockSpec(memory_space=pl.ANY)
    return pl.pallas_call(
        _expert_kernel,
        grid_spec=pltpu.PrefetchScalarGridSpec(
            num_scalar_prefetch=5,
            grid=(n_blocks,),
            in_specs=[pl.BlockSpec((bm * SUBLANES, LANES), blk), hbm, hbm, hbm],
            out_specs=pl.BlockSpec((bm * SUBLANES, LANES), blk),
            scratch_shapes=[pltpu.VMEM((2, d, ff), F32), pltpu.VMEM((2, d, ff), F32), pltpu.VMEM((2, ff, d), F32),
                            pltpu.VMEM((d, ff), BF16), pltpu.VMEM((d, ff), BF16), pltpu.VMEM((ff, d), BF16),
                            pltpu.SemaphoreType.DMA((2,))]),
        out_shape=jax.ShapeDtypeStruct(xs.shape, U32),
        compiler_params=_params(("arbitrary",)),
    )(block_e, n_used, next_block, parity, valid, xs, w_gate, w_up, w_down)


def _combine_kernel(run_len_ref, run_off_ref, run_src_ref, ys_ref, h_ref, x1_ref, pos_ref, wts_ref, wg_ref, wu_ref, wd_ref, gt_ref, gf_ref,
                    o_ref, stage, sem, *, tm, rows_per_pass):
    step = pl.program_id(0)
    na = TOP_K * tm
    buf = step % 2

    last = pl.num_programs(0) - 1

    def fetch_tile(tile, which, unroll=False):
        def run_copy(tile_slot, sorted_slot, n_slots):
            return pltpu.make_async_copy(ys_ref.at[_slot_rows(sorted_slot, n_slots), :],
                                         stage.at[which, _slot_rows(tile_slot, n_slots), :], sem.at[which])
        _for_each_run(tile, (run_len_ref, run_off_ref, run_src_ref), tm, run_copy, lambda cp: cp.start(), unroll)

    def wait_tile(which):
        pltpu.make_async_copy(ys_ref.at[pl.ds(0, na * SUBLANES), :], stage.at[which], sem.at[which]).wait()

    @pl.when(step == 0)
    def _():
        fetch_tile(step, buf)

    fetch_tile(jnp.minimum(step + 1, last), 1 - buf, unroll=True)

    hb = h_ref[...]
    hid = _silu(_dot(hb, wg_ref[...])) * _dot(hb, wu_ref[...])
    acc = _dot(hid.astype(BF16), wd_ref[...])

    wait_tile(buf)

    pos = pos_ref[...]
    wts = wts_ref[...]
    for a0 in range(0, na, rows_per_pass):
        los, his = [], []
        for i in range(SUBLANES):
            lo, hi = _unpack_halves(stage[buf, pl.ds(a0 * SUBLANES + i, rows_per_pass, stride=SUBLANES), :])
            los.append(lo.astype(BF16))
            his.append(hi.astype(BF16))
        y_rows = jnp.concatenate(los + his, axis=1)
        slot_id = a0 + _iota2((rows_per_pass, tm), 0)
        wmat = jnp.zeros((rows_per_pass, tm), F32)
        for k in range(TOP_K):
            wmat = wmat + jnp.where(pos[k:k + 1, :] == slot_id, wts[k:k + 1, :], 0.0)
        acc = acc + _dg(wmat.astype(BF16), y_rows, TN)
    x2 = x1_ref[...] + gt_ref[...] * acc
    o_ref[...] = _rms(x2, NORM_EPS) * gf_ref[...]

    @pl.when(step == last)
    def _():
        wait_tile(1 - buf)


def _combine(runs, ys, h2, x1, pos_t, wts_t, w_gate, w_up, w_down, gt2, gf, tm, rows_per_pass):
    s, d = x1.shape
    ff = w_gate.shape[1]
    na = TOP_K * tm
    const = lambda shape: pl.BlockSpec(shape, lambda i, *_: (0, 0), pipeline_mode=pl.Buffered(1))
    tile = lambda cols: pl.BlockSpec((tm, cols), lambda i, *_: (i, 0))
    per_k = pl.BlockSpec((TOP_K, tm), lambda i, *_: (0, i))
    return pl.pallas_call(
        functools.partial(_combine_kernel, tm=tm, rows_per_pass=min(rows_per_pass, na)),
        grid_spec=pltpu.PrefetchScalarGridSpec(
            num_scalar_prefetch=3,
            grid=(s // tm,),
            in_specs=[pl.BlockSpec(memory_space=pl.ANY),
                      tile(d), tile(d), per_k, per_k,
                      const((d, ff)), const((d, ff)), const((ff, d)), const((1, d)), const((1, d))],
            out_specs=tile(d),
            scratch_shapes=[pltpu.VMEM((2, na * SUBLANES, LANES), U32), pltpu.SemaphoreType.DMA((2,))]),
        out_shape=jax.ShapeDtypeStruct((s, d), F32),
        compiler_params=_params(("arbitrary",), disable_bounds_checks=True),
    )(*runs, ys, h2, x1, pos_t, wts_t, w_gate, w_up, w_down, gt2, gf)


def _mixer(x2d, mod, norm1_g, norm2_g, w_in, lb, hgrn_onorm_g, gdn_conv_w, gdn_a_log, gdn_dt_bias, gdn_onorm_g,
           w_branch_hgrn, w_branch_gdn, w_out, tiles):
    d = x2d.shape[1]
    sh1, sc1, gt1, sh2, sc2, _ = [mod[:, i * d:(i + 1) * d] for i in range(6)]
    key = HEADS * HEAD_DIM
    small0 = 4 * key + 3 * key
    small1 = small0 + 2 * HEADS
    w_in_t = jnp.swapaxes(w_in, 1, 2)
    w_main_t = _wprep(w_in_t, small0, small1, tiles["wprep_tn"])
    w_small_t = w_in_t[0, small0:small1, :].astype(BF16)
    proj, ab_t = _inproj(x2d, norm1_g, sc1, sh1, w_main_t, w_small_t, tiles["in_tm"], tiles["in_tn"])
    o_a = _hgrn(proj, lb, hgrn_onorm_g, tiles["mix_ts"])
    u, wqd, ku, attn, dl = _gdn_prep(proj, gdn_conv_w, ab_t, gdn_a_log, gdn_dt_bias, tiles["prep_ts"])
    o_b = _gdn_scan(u, wqd, ku, attn, dl, proj, gdn_onorm_g, tiles["mix_ts"])
    return _merge(o_a, o_b, proj, x2d, w_branch_hgrn.astype(BF16), w_branch_gdn.astype(BF16),
                  w_out.astype(BF16), gt1, norm2_g, sc2, sh2, tiles["merge_tm"])


def _moe(x1, h2, mod, norm2_g, normf_g, w_router, router_bias, w_exp_gate, w_exp_up, w_exp_down, w_sh_gate,
         w_sh_up, w_sh_down, tiles):
    s, d = x1.shape
    tm = tiles["moe_tm"]
    sh2, sc2, gt2 = [mod[:, i * d:(i + 1) * d] for i in (3, 4, 5)]
    pos_t, wts_t, before, ntile = _router(x1, norm2_g, sc2, sh2, w_router.T, router_bias.reshape(-1, 1), tm)
    bm = EXPERT_BLOCK
    n_blocks = -(-(s * TOP_K + N_EXPERTS * (bm - 1)) // bm)
    before = before[:, :, 0]
    ntile = ntile[:, :, 0]
    counts = before[-1] + ntile[-1]
    padded = (counts + bm - 1) // bm * bm
    earlier = jnp.arange(N_EXPERTS)[None, :] < jnp.arange(N_EXPERTS)[:, None]
    pstart = jnp.sum(jnp.where(earlier, padded[None, :], 0), axis=1).astype(I32)
    pend = pstart + padded
    block_start = jnp.arange(n_blocks, dtype=I32) * bm
    block_e = jnp.minimum(jnp.sum(pend[None, :] <= block_start[:, None], axis=1), N_EXPERTS - 1).astype(I32)
    n_used = pend[-1:] // bm
    run_off = jnp.sum(jnp.where(earlier[None], ntile[:, None, :], 0), axis=2)
    runs = (ntile.reshape(-1), run_off.reshape(-1), (before + pstart[None, :]).reshape(-1))
    xs = _dispatch(runs, pstart + counts, pend, pos_t, h2, n_blocks * bm, tm, tiles["dispatch_rows"])
    own = block_e[:, None] == jnp.arange(N_EXPERTS)[None, :]
    next_block = jnp.sum(jnp.where(own, pend[None, :], 0), axis=1) // bm
    switches = jnp.concatenate([jnp.zeros((1,), I32), (block_e[1:] != block_e[:-1]).astype(I32)])
    upto = jnp.arange(n_blocks)[None, :] <= jnp.arange(n_blocks)[:, None]
    parity = jnp.sum(jnp.where(upto, switches[None, :], 0), axis=1).astype(I32) % 2
    start_b = jnp.sum(jnp.where(own, pstart[None, :], 0), axis=1)
    count_b = jnp.sum(jnp.where(own, counts[None, :], 0), axis=1)
    valid = jnp.clip(count_b - (block_start - start_b), 0, bm).astype(I32)
    ys = _experts(block_e, n_used, next_block, parity, valid, xs, w_exp_gate, w_exp_up, w_exp_down)
    return _combine(runs, ys, h2, x1, pos_t, wts_t, w_sh_gate.astype(BF16), w_sh_up.astype(BF16),
                    w_sh_down.astype(BF16), gt2, normf_g, tm, tiles["combine_rows"])


def _tiles(s):
    pick = lambda want: min(want, s)
    return dict(ada_tn=1024, wprep_tn=512, in_tm=pick(1024), in_tn=1536, mix_ts=pick(512), prep_ts=pick(2048),
                merge_tm=pick(512), moe_tm=pick(256), dispatch_rows=512, combine_rows=256)


def kernel(x, c, w_ada, b_ada, norm1_g, norm2_g, w_in, hgrn_lb_table, hgrn_onorm_g, gdn_conv_w, gdn_a_log, gdn_dt_bias, gdn_onorm_g, w_branch_hgrn, w_branch_gdn, w_out, w_router, router_bias, w_exp_gate, w_exp_up, w_exp_down, w_sh_gate, w_sh_up, w_sh_down, normf_g):
    b, s, d = x.shape
    assert b == 1 and w_ada.shape[0] == 1, "one sequence, one layer"
    tiles = _tiles(s)
    lb = jnp.sum(jax.nn.softmax(hgrn_lb_table.astype(F32), axis=0)[0:1], axis=0, keepdims=True)
    mod = _ada(c, w_ada[0], b_ada[0], tiles["ada_tn"])
    row = lambda v: v.reshape(1, -1)
    x1, h2 = _mixer(x[0], mod, row(norm1_g[0]), row(norm2_g[0]), w_in, lb, row(hgrn_onorm_g[0]), gdn_conv_w[0],
                    gdn_a_log[0], gdn_dt_bias[0], row(gdn_onorm_g[0]), w_branch_hgrn[0], w_branch_gdn[0], w_out[0],
                    tiles)
    out = _moe(x1, h2, mod, row(norm2_g[0]), row(normf_g), w_router[0], router_bias[0], w_exp_gate[0],
               w_exp_up[0], w_exp_down[0], w_sh_gate[0], w_sh_up[0], w_sh_down[0], tiles)
    return out[None]
```

```python
import functools

import jax
import jax.numpy as jnp
from jax import lax
from jax.experimental import pallas as pl
from jax.experimental.pallas import tpu as pltpu

F32 = jnp.float32
BF16 = jnp.bfloat16
I32 = jnp.int32
U32 = jnp.uint32

NORM_EPS = 1e-6
L2_EPS = 1e-6
HEADS = 8
HEAD_DIM = 128
CONV_WIDTH = 4
CHUNK = 64
N_EXPERTS = 64
N_GROUPS = 8
GROUP_SIZE = N_EXPERTS // N_GROUPS
TOPK_GROUPS = 4
TOP_K = 8
ROUTED_SCALE = 2.5
EXPERT_BLOCK = 512

LANES = 128
SUBLANES = 8
MXU_DIM = 256
VMEM_LIMIT = 56 * 1024 * 1024

NT = (((1,), (1,)), ((), ()))
TN = (((0,), (0,)), ((), ()))


def _params(sem, **kw):
    return pltpu.CompilerParams(dimension_semantics=sem, vmem_limit_bytes=VMEM_LIMIT, **kw)


def _dot(a, b):
    return jnp.dot(a, b, preferred_element_type=F32)


def _dg(a, b, dims):
    return lax.dot_general(a, b, dims, preferred_element_type=F32)


def _split(x):
    hi = x.astype(BF16)
    lo = (x - hi.astype(F32)).astype(BF16)
    return hi, lo


def _dot_exact_lhs(a_bf16, x, dims=None):
    hi, lo = _split(x)
    if dims is None:
        return _dot(a_bf16, hi) + _dot(a_bf16, lo)
    return _dg(a_bf16, hi, dims) + _dg(a_bf16, lo, dims)


def _sigmoid(x):
    return 0.5 * jnp.tanh(0.5 * x) + 0.5


def _silu(x):
    half = 0.5 * x
    return half * jnp.tanh(half) + half


def _rms(x, eps):
    return x * lax.rsqrt(jnp.mean(x * x, axis=-1, keepdims=True) + eps)


def _iota2(shape, dim):
    return lax.broadcasted_iota(I32, shape, dim)


def _pack_halves(lo, hi):
    lo_bits = lax.shift_right_logical(pltpu.bitcast(lo, U32), U32(16))
    hi_bits = pltpu.bitcast(hi, U32) & U32(0xFFFF0000)
    return lo_bits | hi_bits


def _unpack_halves(word):
    lo = pltpu.bitcast(lax.shift_left(word, U32(16)), F32)
    hi = pltpu.bitcast(word & U32(0xFFFF0000), F32)
    return lo, hi


def _round_bf16(x):
    return x.astype(BF16).astype(F32)


def _ada_kernel(c_ref, w_ref, b_ref, o_ref):
    cond = _silu(c_ref[...])
    o_ref[...] = jnp.sum(w_ref[...] * cond, axis=0, keepdims=True) + b_ref[...]


def _ada(c, w_ada, b_ada, tn):
    d, n = w_ada.shape
    return pl.pallas_call(
        _ada_kernel,
        grid=(n // tn,),
        in_specs=[pl.BlockSpec((d, 1), lambda j: (0, 0)),
                  pl.BlockSpec((d, tn), lambda j: (0, j)),
                  pl.BlockSpec((1, tn), lambda j: (0, j))],
        out_specs=pl.BlockSpec((1, tn), lambda j: (0, j)),
        out_shape=jax.ShapeDtypeStruct((1, n), F32),
        compiler_params=_params(("arbitrary",)),
    )(c.reshape(d, 1), w_ada, b_ada.reshape(1, n))


def _wprep_kernel(a_ref, b_ref, o_ref, *, first_shifted, shift):
    j = pl.program_id(0)

    @pl.when(j < first_shifted)
    def _():
        o_ref[...] = a_ref[...].astype(BF16)

    @pl.when(j >= first_shifted)
    def _():
        tn = a_ref.shape[0]
        o_ref[0:tn - shift, :] = a_ref[shift:tn, :].astype(BF16)
        o_ref[tn - shift:tn, :] = b_ref[...].astype(BF16)


def _wprep(w_in_t, cut0, cut1, tn):
    _, n_in, d = w_in_t.shape
    shift = cut1 - cut0
    n_out = n_in - shift
    assert cut0 % tn == 0 and n_out % tn == 0 and tn % shift == 0 and shift % (2 * SUBLANES) == 0
    return pl.pallas_call(
        functools.partial(_wprep_kernel, first_shifted=cut0 // tn, shift=shift),
        grid=(n_out // tn,),
        in_specs=[pl.BlockSpec((None, tn, d), lambda j: (0, j, 0)),
                  pl.BlockSpec((None, shift, d), lambda j: (0, (j + 1) * (tn // shift), 0))],
        out_specs=pl.BlockSpec((tn, d), lambda j: (j, 0)),
        out_shape=jax.ShapeDtypeStruct((n_out, d), BF16),
        compiler_params=_params(("arbitrary",)),
    )(w_in_t, w_in_t)


def _inproj_kernel(x_ref, g_ref, sc_ref, sh_ref, w_ref, wst_ref, proj_ref, smallt_ref, h_scr):
    @pl.when(pl.program_id(1) == 0)
    def _():
        h = _rms(x_ref[...], NORM_EPS) * g_ref[...] * (1.0 + sc_ref[...]) + sh_ref[...]
        hb = h.astype(BF16)
        h_scr[...] = hb
        smallt_ref[...] = _dg(wst_ref[...], hb, NT)

    proj_ref[...] = _dg(h_scr[...], w_ref[...], NT).astype(BF16)


def _inproj(x, g, sc, sh, w_main_t, w_small_t, tm, tn):
    s, d = x.shape
    n = w_main_t.shape[0]
    ns = w_small_t.shape[0]
    row = lambda i, j: (0, 0)
    return pl.pallas_call(
        _inproj_kernel,
        grid=(s // tm, n // tn),
        in_specs=[pl.BlockSpec((tm, d), lambda i, j: (i, 0)),
                  pl.BlockSpec((1, d), row), pl.BlockSpec((1, d), row), pl.BlockSpec((1, d), row),
                  pl.BlockSpec((tn, d), lambda i, j: (j, 0)),
                  pl.BlockSpec((ns, d), row)],
        out_specs=[pl.BlockSpec((tm, tn), lambda i, j: (i, j)),
                   pl.BlockSpec((ns, tm), lambda i, j: (0, i))],
        out_shape=[jax.ShapeDtypeStruct((s, n), BF16), jax.ShapeDtypeStruct((ns, s), F32)],
        scratch_shapes=[pltpu.VMEM((tm, d), BF16)],
        compiler_params=_params(("arbitrary", "arbitrary")),
    )(x, g, sc, sh, w_main_t, w_small_t)


def _hgrn_kernel(q_ref, f_ref, i_ref, g_ref, lb_ref, on_ref, o_ref, st_scr, *, n_chunks):
    @pl.when(pl.program_id(0) == 0)
    def _():
        st_scr[...] = jnp.zeros_like(st_scr)

    c = CHUNK
    hd = HEAD_DIM
    causal = _iota2((c, c), 1) <= _iota2((c, c), 0)
    tri = causal.astype(BF16)
    lb = lb_ref[...]
    on_g = on_ref[...]
    heads = [slice(h * hd, (h + 1) * hd) for h in range(HEADS)]

    def chunk(n, carry):
        rows = pl.ds(pl.multiple_of(n * c, c), c)
        f = lb + (1.0 - lb) * _sigmoid(f_ref[rows, :].astype(F32))
        b = _dot_exact_lhs(tri, jnp.log(f))
        k = 1.0 - f
        q = _silu(q_ref[rows, :].astype(F32)) * (hd ** -0.5)
        v = i_ref[rows, :]
        b_mid = b[c // 2:c // 2 + 1, :]
        b_last = b[c - 1:c, :]
        qa = (q * jnp.exp(b - b_mid)).astype(BF16)
        ka = (k * jnp.exp(b_mid - b)).astype(BF16)
        qi = (q * jnp.exp(b)).astype(BF16)
        ku = (k * jnp.exp(b_last - b)).astype(BF16)
        dec = jnp.exp(b_last)
        gate = on_g * _silu(g_ref[rows, :].astype(F32))
        sts = [st_scr[h] for h in range(HEADS)]
        scores = [jnp.where(causal, _dg(qa[:, sl], ka[:, sl], NT), 0.0).astype(BF16) for sl in heads]
        inter = [_dg(qi[:, sl], st.astype(BF16), NT) for sl, st in zip(heads, sts)]
        kv = [_dg(v[:, sl], ku[:, sl], TN) for sl in heads]
        for h, sl in enumerate(heads):
            st_scr[h] = dec[:, sl] * sts[h] + kv[h]
        outs = [_rms(_dot(sc, v[:, sl]) + it, NORM_EPS) for sc, sl, it in zip(scores, heads, inter)]
        o_ref[rows, :] = (jnp.concatenate(outs, axis=1) * gate).astype(BF16)
        return carry

    lax.fori_loop(0, n_chunks, chunk, 0, unroll=4)


def _hgrn(proj, lb, onorm_g, ts):
    s = proj.shape[0]
    width = HEADS * HEAD_DIM
    col = lambda blk: pl.BlockSpec((ts, width), lambda j, blk=blk: (j, blk))
    const = pl.BlockSpec((1, width), lambda j: (0, 0))
    return pl.pallas_call(
        functools.partial(_hgrn_kernel, n_chunks=ts // CHUNK),
        grid=(s // ts,),
        in_specs=[col(0), col(1), col(2), col(3), const, const],
        out_specs=pl.BlockSpec((ts, width), lambda j: (j, 0)),
        out_shape=jax.ShapeDtypeStruct((s, width), BF16),
        scratch_shapes=[pltpu.VMEM((HEADS, HEAD_DIM, HEAD_DIM), F32)],
        compiler_params=_params(("arbitrary",)),
    )(proj, proj, proj, proj, lb, jnp.tile(onorm_g, (1, HEADS)))


def _gdn_prep_kernel(q_ref, k_ref, v_ref, qp_ref, kp_ref, vp_ref, wq_ref, wk_ref, wv_ref, ab_ref, alog_ref,
                     dtb_ref, tri_ref, eye_ref, u_ref, wqd_ref, ku_ref, attn_ref, dl_ref, cat_scr, rows_scr, cols_scr,
                     *, n_chunks, ts):
    h = pl.program_id(1)
    first = pl.program_id(0) == 0
    c = CHUNK
    hd = HEAD_DIM

    def conv_silu(cur_ref, prev_ref, w_ref):
        cat_scr[0:SUBLANES, :] = jnp.where(first, 0.0, prev_ref[...].astype(F32))
        cat_scr[SUBLANES:SUBLANES + ts, :] = cur_ref[...].astype(F32)
        acc = None
        for j in range(CONV_WIDTH):
            off = SUBLANES - (CONV_WIDTH - 1) + j
            term = cat_scr[off:off + ts, :] * w_ref[j:j + 1, :]
            acc = term if acc is None else acc + term
        return _silu(acc)

    def l2n(x):
        return x * lax.rsqrt(jnp.sum(x * x, axis=-1, keepdims=True) + L2_EPS)

    q_all = l2n(conv_silu(q_ref, qp_ref, wq_ref)) * (hd ** -0.5)
    k_all = l2n(conv_silu(k_ref, kp_ref, wk_ref))
    v_all = conv_silu(v_ref, vp_ref, wv_ref)

    @pl.when(h == 0)
    def _():
        z = ab_ref[0:HEADS, :] + dtb_ref[...]
        softplus = jnp.maximum(z, 0.0) + jnp.log(1.0 + jnp.exp(-jnp.abs(z)))
        ld_rows = -jnp.exp(alog_ref[...]) * softplus
        hi, lo = _split(ld_rows)
        tri_blocks = tri_ref[...]
        w = tri_blocks.shape[0]
        spans = [slice(t0, t0 + w) for t0 in range(0, ts, w)]
        g_rows = jnp.concatenate([_dg(hi[:, sp], tri_blocks, NT) + _dg(lo[:, sp], tri_blocks, NT) for sp in spans],
                                 axis=1)
        beta_rows = _sigmoid(ab_ref[HEADS:2 * HEADS, :])
        rows_scr[...] = g_rows
        rows = jnp.concatenate([g_rows, beta_rows, jnp.zeros((LANES - 2 * HEADS, ts), F32)], axis=0)
        r_hi, r_lo = _split(rows)
        r_lo2 = (rows - r_hi.astype(F32) - r_lo.astype(F32)).astype(BF16)
        eye_w = eye_ref[...]
        for sp in spans:
            cols_scr[sp, :] = _dg(eye_w, r_hi[:, sp], NT) + _dg(eye_w, r_lo[:, sp], NT) + _dg(eye_w, r_lo2[:, sp], NT)

    lane = _iota2((ts, LANES), 1)
    cols = cols_scr[...]
    gc_all = jnp.sum(jnp.where(lane == h, cols, 0.0), axis=1, keepdims=True)
    bc_all = jnp.sum(jnp.where(lane == h + HEADS, cols, 0.0), axis=1, keepdims=True)
    g_row = rows_scr[pl.ds(h, 1), :]
    egc_all = jnp.exp(gc_all)

    r = _iota2((c, c), 0)
    cidx = _iota2((c, c), 1)
    causal = cidx <= r
    strict = cidx < r
    eye_f = (r == cidx).astype(F32)
    chunks = [slice(n * c, (n + 1) * c) for n in range(n_chunks)]

    q16 = q_all.astype(BF16)
    k16 = k_all.astype(BF16)
    kq = [_dg(jnp.concatenate([k16[sl], q16[sl]], axis=0), k16[sl], NT) for sl in chunks]
    dm = []
    for sl in chunks:
        diff = gc_all[sl] - g_row[:, sl]
        dm.append(jnp.where(causal, jnp.exp(jnp.where(causal, diff, 0.0)), 0.0))
    bm = [-jnp.where(strict, bc_all[sl] * x[0:c] * d, 0.0) for sl, x, d in zip(chunks, kq, dm)]
    p = [eye_f + b for b in bm]
    bm = [_dot(b.astype(BF16), b.astype(BF16)) for b in bm]
    for _ in range(c.bit_length() - 3):
        res = [_dot(b.astype(BF16), jnp.concatenate([b, pp], axis=1).astype(BF16)) for b, pp in zip(bm, p)]
        p = [pp + x[:, c:2 * c] for pp, x in zip(p, res)]
        bm = [x[:, 0:c] for x in res]
    p = [pp + _dot(b.astype(BF16), pp.astype(BF16)) for b, pp in zip(bm, p)]
    rhs = jnp.concatenate([v_all * bc_all, k_all * (bc_all * egc_all)], axis=1).astype(BF16)
    sol = [_dot(pp.astype(BF16), rhs[sl]) for pp, sl in zip(p, chunks)]
    qd_all = (q_all * egc_all).astype(BF16)
    for n, sl in enumerate(chunks):
        g_last = gc_all[(n + 1) * c - 1:(n + 1) * c, :]
        u_ref[sl, :] = sol[n][:, 0:hd].astype(BF16)
        wqd_ref[2 * n * c:(2 * n + 1) * c, :] = sol[n][:, hd:2 * hd].astype(BF16)
        wqd_ref[(2 * n + 1) * c:(2 * n + 2) * c, :] = qd_all[sl]
        ku_ref[sl, :] = (k_all[sl] * jnp.exp(g_last - gc_all[sl])).astype(BF16)
        attn_ref[sl, :] = (kq[n][c:2 * c] * dm[n]).astype(BF16)
        dl_ref[n:n + 1, :] = jnp.broadcast_to(jnp.exp(g_last), (1, hd))


def _gdn_prep(proj, conv_w, ab_t, a_log, dt_bias, ts):
    s = proj.shape[0]
    hd = HEAD_DIM
    c = CHUNK
    q0 = 4 * HEADS
    cur = lambda off: pl.BlockSpec((ts, hd), lambda j, h, off=off: (j, off + h))
    prev = lambda off: pl.BlockSpec((SUBLANES, hd),
                                    lambda j, h, off=off: (jnp.maximum(j * (ts // SUBLANES) - 1, 0), off + h))
    cw = lambda off: pl.BlockSpec((CONV_WIDTH, hd), lambda j, h, off=off: (0, off + h))
    per_head_scalar = pl.BlockSpec((HEADS, 1), lambda j, h: (0, 0))
    w = min(ts, MXU_DIM)
    const = pl.BlockSpec((w, w), lambda j, h: (0, 0))
    pos = jnp.arange(w)
    tri_blocks = ((pos[:, None] // c == pos[None, :] // c) & (pos[None, :] <= pos[:, None])).astype(BF16)
    eye = (pos[:, None] == pos[None, :]).astype(BF16)
    per_head = lambda rows, cols: pl.BlockSpec((None, rows, cols), lambda j, h: (h, j, 0))
    return pl.pallas_call(
        functools.partial(_gdn_prep_kernel, n_chunks=ts // c, ts=ts),
        grid=(s // ts, HEADS),
        in_specs=[cur(q0), cur(q0 + HEADS), cur(q0 + 2 * HEADS),
                  prev(q0), prev(q0 + HEADS), prev(q0 + 2 * HEADS),
                  cw(0), cw(HEADS), cw(2 * HEADS),
                  pl.BlockSpec((2 * HEADS, ts), lambda j, h: (0, j)),
                  per_head_scalar, per_head_scalar, const, const],
        out_specs=[pl.BlockSpec((ts, hd), lambda j, h: (j, h)),
                   pl.BlockSpec((2 * ts, hd), lambda j, h: (j, h)),
                   pl.BlockSpec((ts, hd), lambda j, h: (j, h)),
                   per_head(ts, c),
                   per_head(ts // c, hd)],
        out_shape=[jax.ShapeDtypeStruct((s, HEADS * hd), BF16),
                   jax.ShapeDtypeStruct((2 * s, HEADS * hd), BF16),
                   jax.ShapeDtypeStruct((s, HEADS * hd), BF16),
                   jax.ShapeDtypeStruct((HEADS, s, c), BF16),
                   jax.ShapeDtypeStruct((HEADS, s // c, hd), F32)],
        scratch_shapes=[pltpu.VMEM((ts + SUBLANES, hd), F32), pltpu.VMEM((HEADS, ts), F32),
                        pltpu.VMEM((ts, LANES), F32)],
        compiler_params=_params(("arbitrary", "arbitrary")),
    )(proj, proj, proj, proj, proj, proj, conv_w, conv_w, conv_w,
      ab_t, a_log.reshape(HEADS, 1), dt_bias.reshape(HEADS, 1), tri_blocks, eye)


def _gdn_scan_kernel(u_ref, wqd_ref, ku_ref, attn_ref, dl_ref, g_ref, on_ref, o_ref, st_scr, *, n_chunks):
    @pl.when(pl.program_id(0) == 0)
    def _():
        st_scr[...] = jnp.zeros_like(st_scr)

    c = CHUNK
    hd = HEAD_DIM
    on_g = on_ref[...]
    heads = [slice(h * hd, (h + 1) * hd) for h in range(HEADS)]

    def chunk(n, carry):
        rows = pl.ds(pl.multiple_of(n * c, c), c)
        rows2 = pl.ds(pl.multiple_of(2 * n * c, 2 * c), 2 * c)
        sts = [st_scr[h] for h in range(HEADS)]
        wq = [_dot(wqd_ref[rows2, sl], st.astype(BF16)) for sl, st in zip(heads, sts)]
        vn = [(u_ref[rows, sl].astype(F32) - x[0:c]).astype(BF16) for sl, x in zip(heads, wq)]
        upd = [_dg(ku_ref[rows, sl], v, TN) for sl, v in zip(heads, vn)]
        for h in range(HEADS):
            st_scr[h] = dl_ref[h, pl.ds(n, 1), :] * sts[h] + upd[h]
        outs = [_rms(x[c:2 * c] + _dot(attn_ref[h, rows, :], v), NORM_EPS)
                for h, (x, v) in enumerate(zip(wq, vn))]
        gate = jnp.tile(on_g, (1, HEADS)) * _silu(g_ref[rows, :].astype(F32))
        o_ref[rows, :] = (jnp.concatenate(outs, axis=1) * gate).astype(BF16)
        return carry

    lax.fori_loop(0, n_chunks, chunk, 0, unroll=4)


def _gdn_scan(u, wqd, ku, attn, dl, proj, onorm_g, ts):
    s, width = u.shape
    c = CHUNK
    gate_blk = (4 * HEADS + 3 * HEADS) * HEAD_DIM // width
    return pl.pallas_call(
        functools.partial(_gdn_scan_kernel, n_chunks=ts // c),
        grid=(s // ts,),
        in_specs=[pl.BlockSpec((ts, width), lambda j: (j, 0)),
                  pl.BlockSpec((2 * ts, width), lambda j: (j, 0)),
                  pl.BlockSpec((ts, width), lambda j: (j, 0)),
                  pl.BlockSpec((HEADS, ts, c), lambda j: (0, j, 0)),
                  pl.BlockSpec((HEADS, ts // c, HEAD_DIM), lambda j: (0, j, 0)),
                  pl.BlockSpec((ts, width), lambda j: (j, gate_blk)),
                  pl.BlockSpec((1, HEAD_DIM), lambda j: (0, 0))],
        out_specs=pl.BlockSpec((ts, width), lambda j: (j, 0)),
        out_shape=jax.ShapeDtypeStruct((s, width), BF16),
        scratch_shapes=[pltpu.VMEM((HEADS, HEAD_DIM, HEAD_DIM), F32)],
        compiler_params=_params(("arbitrary",)),
    )(u, wqd, ku, attn, dl, proj, onorm_g)


def _merge_kernel(oa_ref, ob_ref, mga_ref, mgb_ref, x_ref, wa_ref, wb_ref, wo_ref, gt_ref, g2_ref, sc_ref,
                  sh_ref, x1_ref, h2_ref):
    ya = _dot(oa_ref[...], wa_ref[...])
    yb = _dot(ob_ref[...], wb_ref[...])
    merged = _sigmoid(mga_ref[...].astype(F32)) * ya + _sigmoid(mgb_ref[...].astype(F32)) * yb
    x1 = x_ref[...] + gt_ref[...] * _dot(merged.astype(BF16), wo_ref[...])
    x1_ref[...] = x1
    h2 = _rms(x1, NORM_EPS) * g2_ref[...] * (1.0 + sc_ref[...]) + sh_ref[...]
    h2_ref[...] = h2.astype(BF16)


def _merge(o_a, o_b, proj, x, w_a, w_b, w_o, gt1, g2, sc2, sh2, tm):
    s, d = x.shape
    dv = o_a.shape[1]
    mg0 = (8 * HEADS * HEAD_DIM) // d
    const = lambda shape: pl.BlockSpec(shape, lambda i: (0, 0), pipeline_mode=pl.Buffered(1))
    return pl.pallas_call(
        _merge_kernel,
        grid=(s // tm,),
        in_specs=[pl.BlockSpec((tm, dv), lambda i: (i, 0)),
                  pl.BlockSpec((tm, dv), lambda i: (i, 0)),
                  pl.BlockSpec((tm, d), lambda i: (i, mg0)),
                  pl.BlockSpec((tm, d), lambda i: (i, mg0 + 1)),
                  pl.BlockSpec((tm, d), lambda i: (i, 0)),
                  const((dv, d)), const((dv, d)), const((d, d)),
                  const((1, d)), const((1, d)), const((1, d)), const((1, d))],
        out_specs=[pl.BlockSpec((tm, d), lambda i: (i, 0)), pl.BlockSpec((tm, d), lambda i: (i, 0))],
        out_shape=[jax.ShapeDtypeStruct((s, d), F32), jax.ShapeDtypeStruct((s, d), BF16)],
        compiler_params=_params(("arbitrary",)),
    )(o_a, o_b, proj, proj, x, w_a, w_b, w_o, gt1, g2, sc2, sh2)


def _first_max(vals, iota, size, axis):
    m = jnp.max(vals, axis=axis, keepdims=True)
    idx = jnp.min(jnp.where(vals == m, iota, size), axis=axis, keepdims=True)
    return m, idx


def _router_kernel(x1_ref, g2_ref, sc_ref, sh_ref, wrt_ref, bias_ref, upper_ref, pos_ref, wts_ref, before_ref,
                   ntile_ref, cnt_scr, *, tm):
    @pl.when(pl.program_id(0) == 0)
    def _():
        cnt_scr[...] = jnp.zeros_like(cnt_scr)

    e = N_EXPERTS
    h2 = _rms(x1_ref[...], NORM_EPS) * g2_ref[...] * (1.0 + sc_ref[...]) + sh_ref[...]
    logits = lax.dot_general(wrt_ref[...], h2, NT, preferred_element_type=F32,
                             precision=lax.Precision.HIGHEST)
    scores = _sigmoid(logits)
    biased = scores + bias_ref[...]
    neg = -jnp.inf

    g3 = biased.reshape(N_GROUPS, GROUP_SIZE, tm)
    i3 = lax.broadcasted_iota(I32, g3.shape, 1)
    m1, a1 = _first_max(g3, i3, GROUP_SIZE, 1)
    m2 = jnp.max(jnp.where(i3 == a1, neg, g3), axis=1, keepdims=True)
    gs = (m1 + m2).reshape(N_GROUPS, tm)
    ig = _iota2(gs.shape, 0)
    gmask = jnp.zeros(gs.shape, jnp.bool_)
    for _ in range(TOPK_GROUPS):
        _, a = _first_max(gs, ig, N_GROUPS, 0)
        pick = ig == a
        gmask = jnp.logical_or(gmask, pick)
        gs = jnp.where(pick, neg, gs)
    emask = jnp.broadcast_to(gmask.reshape(N_GROUPS, 1, tm), (N_GROUPS, GROUP_SIZE, tm)).reshape(e, tm)

    cand = jnp.where(emask, biased, neg)
    ie = _iota2((e, tm), 0)
    sel_all = jnp.zeros((e, tm), jnp.bool_)
    w_rows, picks = [], []
    for _ in range(TOP_K):
        _, a = _first_max(cand, ie, e, 0)
        pick = ie == a
        picks.append(pick)
        w_rows.append(jnp.sum(jnp.where(pick, scores, 0.0), axis=0, keepdims=True))
        sel_all = jnp.logical_or(sel_all, pick)
        cand = jnp.where(pick, neg, cand)
    w_sum = w_rows[0]
    for wr in w_rows[1:]:
        w_sum = w_sum + wr
    wts = jnp.concatenate(w_rows, axis=0) / w_sum * ROUTED_SCALE

    sel = sel_all.astype(BF16)
    in_expert = _dot(sel, upper_ref[...])
    n_tile = jnp.sum(sel_all.astype(F32), axis=1, keepdims=True)
    lower = (_iota2((e, e), 1) < _iota2((e, e), 0)).astype(BF16)
    expert_off = _dot_exact_lhs(lower, jnp.broadcast_to(n_tile, (e, LANES)))[:, 0:1]
    place = in_expert + expert_off
    pos = jnp.concatenate([jnp.sum(jnp.where(pk, place, 0.0), axis=0, keepdims=True) for pk in picks], axis=0)
    pos_ref[...] = pos.astype(I32)
    before_ref[...] = jnp.broadcast_to(cnt_scr[...], before_ref.shape).astype(I32)
    ntile_ref[...] = jnp.broadcast_to(n_tile, ntile_ref.shape).astype(I32)
    cnt_scr[...] = cnt_scr[...] + n_tile
    wts_ref[...] = wts


def _router(x1, g2, sc2, sh2, w_router_t, bias_col, tm):
    s, d = x1.shape
    e = N_EXPERTS
    nt = s // tm
    upper = (jnp.arange(tm)[:, None] < jnp.arange(tm)[None, :]).astype(BF16)
    const = lambda shape: pl.BlockSpec(shape, lambda i: (0, 0))
    per_tile = pl.BlockSpec((None, e, LANES), lambda i: (i, 0, 0))
    return pl.pallas_call(
        functools.partial(_router_kernel, tm=tm),
        grid=(nt,),
        in_specs=[pl.BlockSpec((tm, d), lambda i: (i, 0)),
                  const((1, d)), const((1, d)), const((1, d)),
                  const((e, d)), const((e, 1)), const((tm, tm))],
        out_specs=[pl.BlockSpec((TOP_K, tm), lambda i: (0, i)),
                   pl.BlockSpec((TOP_K, tm), lambda i: (0, i)),
                   per_tile, per_tile],
        out_shape=[jax.ShapeDtypeStruct((TOP_K, s), I32), jax.ShapeDtypeStruct((TOP_K, s), F32),
                   jax.ShapeDtypeStruct((nt, e, LANES), I32), jax.ShapeDtypeStruct((nt, e, LANES), I32)],
        scratch_shapes=[pltpu.VMEM((e, 1), F32)],
        compiler_params=_params(("arbitrary",)),
    )(x1, g2, sc2, sh2, w_router_t, bias_col, upper)


LONG_RUN = 64


def _run_sizes(limit):
    return [1 << b for b in range(limit.bit_length() - 1, -1, -1)]


def _for_each_run(tile, run_refs, tm, make_copy, fn, unroll=False, enable=None):
    run_len_ref, run_off_ref, run_dst_ref = run_refs

    def per_expert(ex, carry):
        n = run_len_ref[tile * N_EXPERTS + ex]
        off = run_off_ref[tile * N_EXPERTS + ex]
        dst = run_dst_ref[tile * N_EXPERTS + ex]
        def pieces(sizes):
            for size in sizes:
                done = n & (-2 * size)

                take = (n & size) != 0
                if enable is not None:
                    take = jnp.logical_and(take, enable)

                @pl.when(take)
                def _(done=done, size=size):
                    fn(make_copy(off + done, dst + done, size))

        sizes = _run_sizes(tm)
        pieces([size for size in sizes if size < LONG_RUN])

        @pl.when(n >= LONG_RUN)
        def _():
            pieces([size for size in sizes if size >= LONG_RUN])

        return carry

    lax.fori_loop(0, N_EXPERTS, per_expert, 0, unroll=unroll)


def _slot_rows(slot, n_slots):
    return pl.ds(pl.multiple_of(slot * SUBLANES, SUBLANES), n_slots * SUBLANES)


def _dispatch_kernel(run_len_ref, run_off_ref, run_dst_ref, pad_lo_ref, pad_hi_ref, pos_ref, h_ref, xs_ref, stage, zero_scr,
                     sem, pad_sem, *, tm, rows_per_pass):
    step = pl.program_id(0)
    na = TOP_K * tm
    d = h_ref.shape[1]
    half = d // 2
    n_words = half // LANES

    def pad_copy(slot, n_slots):
        return pltpu.make_async_copy(zero_scr.at[pl.ds(0, n_slots * SUBLANES), :],
                                     xs_ref.at[_slot_rows(slot, n_slots), :], pad_sem)

    def for_each_pad(fn):
        def per_expert(ex, carry):
            slot = pad_lo_ref[ex]
            n = pad_hi_ref[ex] - slot
            for size in _run_sizes(EXPERT_BLOCK - 1):
                take = (n & size) != 0

                @pl.when(take)
                def _(slot=slot, size=size):
                    fn(pad_copy(slot, size))

                slot = slot + jnp.where(take, size, 0)
            return carry
        lax.fori_loop(0, N_EXPERTS, per_expert, 0)

    @pl.when(step == 0)
    def _():
        zero_scr[...] = jnp.zeros_like(zero_scr)
        for_each_pad(lambda cp: cp.start())

    buf = step % 2
    last = pl.num_programs(0) - 1
    runs = (run_len_ref, run_off_ref, run_dst_ref)

    def run_copy_from(which):
        def run_copy(tile_slot, sorted_slot, n_slots):
            return pltpu.make_async_copy(stage.at[which, _slot_rows(tile_slot, n_slots), :],
                                         xs_ref.at[_slot_rows(sorted_slot, n_slots), :], sem.at[which])
        return run_copy

    def wait_tile(which):
        pltpu.make_async_copy(stage.at[which], xs_ref.at[pl.ds(0, na * SUBLANES), :], sem.at[which]).wait()

    _for_each_run(jnp.maximum(step - 1, 0), runs, tm, run_copy_from(1 - buf), lambda cp: cp.start(),
                  unroll=True, enable=step > 0)

    pos = pos_ref[...].astype(jnp.int16)
    h = h_ref[...]
    slot0 = _iota2((rows_per_pass, tm), 0).astype(jnp.int16)
    for a0 in range(0, na, rows_per_pass):
        slot_id = slot0 + jnp.int16(a0)
        hit = pos[0:1, :] == slot_id
        for k in range(1, TOP_K):
            hit = jnp.logical_or(hit, pos[k:k + 1, :] == slot_id)
        rows = _dot(hit.astype(BF16), h)
        for i in range(n_words):
            word = _pack_halves(rows[:, i * LANES:(i + 1) * LANES], rows[:, half + i * LANES:half + (i + 1) * LANES])
            stage[buf, pl.ds(a0 * SUBLANES + i, rows_per_pass, stride=SUBLANES), :] = word

    @pl.when(step > 0)
    def _():
        wait_tile(1 - buf)

    @pl.when(step == last)
    def _():
        _for_each_run(step, runs, tm, run_copy_from(buf), lambda cp: cp.start())
        wait_tile(buf)

    @pl.when(step == 0)
    def _():
        for_each_pad(lambda cp: cp.wait())


def _dispatch(runs, pad_lo, pad_hi, pos_t, h2, n_slots, tm, rows_per_pass):
    s, d = h2.shape
    assert (d // 2) % LANES == 0 and (d // 2) // LANES == SUBLANES, "one token row must pack into one (8, 128) tile"
    na = TOP_K * tm
    return pl.pallas_call(
        functools.partial(_dispatch_kernel, tm=tm, rows_per_pass=min(rows_per_pass, na)),
        grid_spec=pltpu.PrefetchScalarGridSpec(
            num_scalar_prefetch=5,
            grid=(s // tm,),
            in_specs=[pl.BlockSpec((TOP_K, tm), lambda i, *_: (0, i)),
                      pl.BlockSpec((tm, d), lambda i, *_: (i, 0))],
            out_specs=pl.BlockSpec(memory_space=pl.ANY),
            scratch_shapes=[pltpu.VMEM((2, na * SUBLANES, LANES), U32),
                            pltpu.VMEM((EXPERT_BLOCK // 2 * SUBLANES, LANES), U32),
                            pltpu.SemaphoreType.DMA((2,)), pltpu.SemaphoreType.DMA(())]),
        out_shape=jax.ShapeDtypeStruct((n_slots * SUBLANES, LANES), U32),
        compiler_params=_params(("arbitrary",), has_side_effects=True, disable_bounds_checks=True),
    )(*runs, pad_lo, pad_hi, pos_t, h2)


def _expert_kernel(be_ref, nu_ref, next_ref, par_ref, valid_ref, x_ref, wg_hbm, wu_hbm, wd_hbm, y_ref, wg_f32, wu_f32, wd_f32,
                   wg_scr, wu_scr, wd_scr, sem):
    b = pl.program_id(0)
    bm = EXPERT_BLOCK
    active = b < nu_ref[0]
    new_expert = jnp.logical_or(b == 0, be_ref[b] != be_ref[jnp.maximum(b - 1, 0)])

    def weight_copies(ex, which):
        return [pltpu.make_async_copy(src.at[ex], dst.at[which], sem.at[which])
                for src, dst in ((wg_hbm, wg_f32), (wu_hbm, wu_f32), (wd_hbm, wd_f32))]

    @pl.when(jnp.logical_and(active, new_expert))
    def _():
        which = par_ref[b]

        @pl.when(b == 0)
        def _():
            for cp in weight_copies(be_ref[b], which):
                cp.start()

        for cp in weight_copies(be_ref[b], which):
            cp.wait()
        wg_scr[...] = wg_f32[which].astype(BF16)
        wu_scr[...] = wu_f32[which].astype(BF16)
        wd_scr[...] = wd_f32[which].astype(BF16)
        nb = next_ref[b]

        @pl.when(nb < nu_ref[0])
        def _():
            for cp in weight_copies(be_ref[nb], 1 - which):
                cp.start()

    def ffn_rows(rows):
        los, his = [], []
        for i in range(SUBLANES):
            lo, hi = _unpack_halves(x_ref[pl.ds(i, rows, stride=SUBLANES), :])
            los.append(lo.astype(BF16))
            his.append(hi.astype(BF16))
        xb = jnp.concatenate(los + his, axis=1)
        hid = _silu(_dot(xb, wg_scr[...])) * _dot(xb, wu_scr[...])
        y = _dot(hid.astype(BF16), wd_scr[...])
        half = y.shape[1] // 2
        for i in range(SUBLANES):
            word = _pack_halves(_round_bf16(y[:, i * LANES:(i + 1) * LANES]),
                                _round_bf16(y[:, half + i * LANES:half + (i + 1) * LANES]))
            y_ref[pl.ds(i, rows, stride=SUBLANES), :] = word

    half_full = valid_ref[b] <= bm // 2

    @pl.when(jnp.logical_and(active, jnp.logical_not(half_full)))
    def _():
        ffn_rows(bm)

    @pl.when(jnp.logical_and(active, half_full))
    def _():
        ffn_rows(bm // 2)
        y_ref[bm // 2 * SUBLANES:bm * SUBLANES, :] = jnp.zeros((bm // 2 * SUBLANES, LANES), U32)


def _experts(block_e, n_used, next_block, parity, valid, xs, w_gate, w_up, w_down):
    d, ff = w_gate.shape[1], w_gate.shape[2]
    bm = EXPERT_BLOCK
    n_blocks = xs.shape[0] // (bm * SUBLANES)
    blk = lambda b, be, nu, *_: (jnp.minimum(b, nu[0] - 1), 0)
    hbm = pl.BlockSpec(memory_space=pl.ANY)
    return pl.pallas_call(
        _expert_kernel,
        grid_spec=pltpu.PrefetchScalarGridSpec(
            num_scalar_prefetch=5,
            grid=(n_blocks,),
            in_specs=[pl.BlockSpec((bm * SUBLANES, LANES), blk), hbm, hbm, hbm],
            out_specs=pl.BlockSpec((bm * SUBLANES, LANES), blk),
            scratch_shapes=[pltpu.VMEM((2, d, ff), F32), pltpu.VMEM((2, d, ff), F32), pltpu.VMEM((2, ff, d), F32),
                            pltpu.VMEM((d, ff), BF16), pltpu.VMEM((d, ff), BF16), pltpu.VMEM((ff, d), BF16),
                            pltpu.SemaphoreType.DMA((2,))]),
        out_shape=jax.ShapeDtypeStruct(xs.shape, U32),
        compiler_params=_params(("arbitrary",)),
    )(block_e, n_used, next_block, parity, valid, xs, w_gate, w_up, w_down)


def _combine_kernel(run_len_ref, run_off_ref, run_src_ref, ys_ref, h_ref, x1_ref, pos_ref, wts_ref, wg_ref, wu_ref, wd_ref, gt_ref, gf_ref,
                    o_ref, stage, sem, *, tm, rows_per_pass):
    step = pl.program_id(0)
    na = TOP_K * tm
    buf = step % 2

    last = pl.num_programs(0) - 1

    def fetch_tile(tile, which, unroll=False):
        def run_copy(tile_slot, sorted_slot, n_slots):
            return pltpu.make_async_copy(ys_ref.at[_slot_rows(sorted_slot, n_slots), :],
                                         stage.at[which, _slot_rows(tile_slot, n_slots), :], sem.at[which])
        _for_each_run(tile, (run_len_ref, run_off_ref, run_src_ref), tm, run_copy, lambda cp: cp.start(), unroll)

    def wait_tile(which):
        pltpu.make_async_copy(ys_ref.at[pl.ds(0, na * SUBLANES), :], stage.at[which], sem.at[which]).wait()

    @pl.when(step == 0)
    def _():
        fetch_tile(step, buf)

    fetch_tile(jnp.minimum(step + 1, last), 1 - buf, unroll=True)

    hb = h_ref[...]
    hid = _silu(_dot(hb, wg_ref[...])) * _dot(hb, wu_ref[...])
    acc = _dot(hid.astype(BF16), wd_ref[...])

    wait_tile(buf)

    pos = pos_ref[...]
    wts = wts_ref[...]
    for a0 in range(0, na, rows_per_pass):
        los, his = [], []
        for i in range(SUBLANES):
            lo, hi = _unpack_halves(stage[buf, pl.ds(a0 * SUBLANES + i, rows_per_pass, stride=SUBLANES), :])
            los.append(lo.astype(BF16))
            his.append(hi.astype(BF16))
        y_rows = jnp.concatenate(los + his, axis=1)
        slot_id = a0 + _iota2((rows_per_pass, tm), 0)
        wmat = jnp.zeros((rows_per_pass, tm), F32)
        for k in range(TOP_K):
            wmat = wmat + jnp.where(pos[k:k + 1, :] == slot_id, wts[k:k + 1, :], 0.0)
        acc = acc + _dg(wmat.astype(BF16), y_rows, TN)
    x2 = x1_ref[...] + gt_ref[...] * acc
    o_ref[...] = _rms(x2, NORM_EPS) * gf_ref[...]

    @pl.when(step == last)
    def _():
        wait_tile(1 - buf)


def _combine(runs, ys, h2, x1, pos_t, wts_t, w_gate, w_up, w_down, gt2, gf, tm, rows_per_pass):
    s, d = x1.shape
    ff = w_gate.shape[1]
    na = TOP_K * tm
    const = lambda shape: pl.BlockSpec(shape, lambda i, *_: (0, 0), pipeline_mode=pl.Buffered(1))
    tile = lambda cols: pl.BlockSpec((tm, cols), lambda i, *_: (i, 0))
    per_k = pl.BlockSpec((TOP_K, tm), lambda i, *_: (0, i))
    return pl.pallas_call(
        functools.partial(_combine_kernel, tm=tm, rows_per_pass=min(rows_per_pass, na)),
        grid_spec=pltpu.PrefetchScalarGridSpec(
            num_scalar_prefetch=3,
            grid=(s // tm,),
            in_specs=[pl.BlockSpec(memory_space=pl.ANY),
                      tile(d), tile(d), per_k, per_k,
                      const((d, ff)), const((d, ff)), const((ff, d)), const((1, d)), const((1, d))],
            out_specs=tile(d),
            scratch_shapes=[pltpu.VMEM((2, na * SUBLANES, LANES), U32), pltpu.SemaphoreType.DMA((2,))]),
        out_shape=jax.ShapeDtypeStruct((s, d), F32),
        compiler_params=_params(("arbitrary",), disable_bounds_checks=True),
    )(*runs, ys, h2, x1, pos_t, wts_t, w_gate, w_up, w_down, gt2, gf)


def _mixer(x2d, mod, norm1_g, norm2_g, w_in, lb, hgrn_onorm_g, gdn_conv_w, gdn_a_log, gdn_dt_bias, gdn_onorm_g,
           w_branch_hgrn, w_branch_gdn, w_out, tiles):
    d = x2d.shape[1]
    sh1, sc1, gt1, sh2, sc2, _ = [mod[:, i * d:(i + 1) * d] for i in range(6)]
    key = HEADS * HEAD_DIM
    small0 = 4 * key + 3 * key
    small1 = small0 + 2 * HEADS
    w_in_t = jnp.swapaxes(w_in, 1, 2)
    w_main_t = _wprep(w_in_t, small0, small1, tiles["wprep_tn"])
    w_small_t = w_in_t[0, small0:small1, :].astype(BF16)
    proj, ab_t = _inproj(x2d, norm1_g, sc1, sh1, w_main_t, w_small_t, tiles["in_tm"], tiles["in_tn"])
    o_a = _hgrn(proj, lb, hgrn_onorm_g, tiles["mix_ts"])
    u, wqd, ku, attn, dl = _gdn_prep(proj, gdn_conv_w, ab_t, gdn_a_log, gdn_dt_bias, tiles["prep_ts"])
    o_b = _gdn_scan(u, wqd, ku, attn, dl, proj, gdn_onorm_g, tiles["mix_ts"])
    return _merge(o_a, o_b, proj, x2d, w_branch_hgrn.astype(BF16), w_branch_gdn.astype(BF16),
                  w_out.astype(BF16), gt1, norm2_g, sc2, sh2, tiles["merge_tm"])


def _moe(x1, h2, mod, norm2_g, normf_g, w_router, router_bias, w_exp_gate, w_exp_up, w_exp_down, w_sh_gate,
         w_sh_up, w_sh_down, tiles):
    s, d = x1.shape
    tm = tiles["moe_tm"]
    sh2, sc2, gt2 = [mod[:, i * d:(i + 1) * d] for i in (3, 4, 5)]
    pos_t, wts_t, before, ntile = _router(x1, norm2_g, sc2, sh2, w_router.T, router_bias.reshape(-1, 1), tm)
    bm = EXPERT_BLOCK
    n_blocks = -(-(s * TOP_K + N_EXPERTS * (bm - 1)) // bm)
    before = before[:, :, 0]
    ntile = ntile[:, :, 0]
    counts = before[-1] + ntile[-1]
    padded = (counts + bm - 1) // bm * bm
    earlier = jnp.arange(N_EXPERTS)[None, :] < jnp.arange(N_EXPERTS)[:, None]
    pstart = jnp.sum(jnp.where(earlier, padded[None, :], 0), axis=1).astype(I32)
    pend = pstart + padded
    block_start = jnp.arange(n_blocks, dtype=I32) * bm
    block_e = jnp.minimum(jnp.sum(pend[None, :] <= block_start[:, None], axis=1), N_EXPERTS - 1).astype(I32)
    n_used = pend[-1:] // bm
    run_off = jnp.sum(jnp.where(earlier[None], ntile[:, None, :], 0), axis=2)
    runs = (ntile.reshape(-1), run_off.reshape(-1), (before + pstart[None, :]).reshape(-1))
    xs = _dispatch(runs, pstart + counts, pend, pos_t, h2, n_blocks * bm, tm, tiles["dispatch_rows"])
    own = block_e[:, None] == jnp.arange(N_EXPERTS)[None, :]
    next_block = jnp.sum(jnp.where(own, pend[None, :], 0), axis=1) // bm
    switches = jnp.concatenate([jnp.zeros((1,), I32), (block_e[1:] != block_e[:-1]).astype(I32)])
    upto = jnp.arange(n_blocks)[None, :] <= jnp.arange(n_blocks)[:, None]
    parity = jnp.sum(jnp.where(upto, switches[None, :], 0), axis=1).astype(I32) % 2
    start_b = jnp.sum(jnp.where(own, pstart[None, :], 0), axis=1)
    count_b = jnp.sum(jnp.where(own, counts[None, :], 0), axis=1)
    valid = jnp.clip(count_b - (block_start - start_b), 0, bm).astype(I32)
    ys = _experts(block_e, n_used, next_block, parity, valid, xs, w_exp_gate, w_exp_up, w_exp_down)
    return _combine(runs, ys, h2, x1, pos_t, wts_t, w_sh_gate.astype(BF16), w_sh_up.astype(BF16),
                    w_sh_down.astype(BF16), gt2, normf_g, tm, tiles["combine_rows"])


def _tiles(s):
    pick = lambda want: min(want, s)
    return dict(ada_tn=1024, wprep_tn=512, in_tm=pick(1024), in_tn=1536, mix_ts=pick(512), prep_ts=pick(2048),
                merge_tm=pick(512), moe_tm=pick(256), dispatch_rows=512, combine_rows=256)


def kernel(x, c, w_ada, b_ada, norm1_g, norm2_g, w_in, hgrn_lb_table, hgrn_onorm_g, gdn_conv_w, gdn_a_log, gdn_dt_bias, gdn_onorm_g, w_branch_hgrn, w_branch_gdn, w_out, w_router, router_bias, w_exp_gate, w_exp_up, w_exp_down, w_sh_gate, w_sh_up, w_sh_down, normf_g):
    b, s, d = x.shape
    assert b == 1 and w_ada.shape[0] == 1, "one sequence, one layer"
    tiles = _tiles(s)
    lb = jnp.sum(jax.nn.softmax(hgrn_lb_table.astype(F32), axis=0)[0:1], axis=0, keepdims=True)
    mod = _ada(c, w_ada[0], b_ada[0], tiles["ada_tn"])
    row = lambda v: v.reshape(1, -1)
    x1, h2 = _mixer(x[0], mod, row(norm1_g[0]), row(norm2_g[0]), w_in, lb, row(hgrn_onorm_g[0]), gdn_conv_w[0],
                    gdn_a_log[0], gdn_dt_bias[0], row(gdn_onorm_g[0]), w_branch_hgrn[0], w_branch_gdn[0], w_out[0],
                    tiles)
    out = _moe(x1, h2, mod, row(norm2_g[0]), row(normf_g), w_router[0], router_bias[0], w_exp_gate[0],
               w_exp_up[0], w_exp_down[0], w_sh_gate[0], w_sh_up[0], w_sh_down[0], tiles)
    return out[None]
```

```python
import functools

import jax
import jax.numpy as jnp
from jax import lax
from jax.experimental import pallas as pl
from jax.experimental.pallas import tpu as pltpu

F32 = jnp.float32
BF16 = jnp.bfloat16
I32 = jnp.int32
U32 = jnp.uint32

NORM_EPS = 1e-6
L2_EPS = 1e-6
HEADS = 8
HEAD_DIM = 128
CONV_WIDTH = 4
CHUNK = 64
N_EXPERTS = 64
N_GROUPS = 8
GROUP_SIZE = N_EXPERTS // N_GROUPS
TOPK_GROUPS = 4
TOP_K = 8
ROUTED_SCALE = 2.5
EXPERT_BLOCK = 512

LANES = 128
SUBLANES = 8
MXU_DIM = 256
VMEM_LIMIT = 56 * 1024 * 1024

NT = (((1,), (1,)), ((), ()))
TN = (((0,), (0,)), ((), ()))


def _params(sem, **kw):
    return pltpu.CompilerParams(dimension_semantics=sem, vmem_limit_bytes=VMEM_LIMIT, **kw)


def _dot(a, b):
    return jnp.dot(a, b, preferred_element_type=F32)


def _dg(a, b, dims):
    return lax.dot_general(a, b, dims, preferred_element_type=F32)


def _split(x):
    hi = x.astype(BF16)
    lo = (x - hi.astype(F32)).astype(BF16)
    return hi, lo


def _dot_exact_lhs(a_bf16, x, dims=None):
    hi, lo = _split(x)
    if dims is None:
        return _dot(a_bf16, hi) + _dot(a_bf16, lo)
    return _dg(a_bf16, hi, dims) + _dg(a_bf16, lo, dims)


def _sigmoid(x):
    return 0.5 * jnp.tanh(0.5 * x) + 0.5


def _silu(x):
    half = 0.5 * x
    return half * jnp.tanh(half) + half


def _rms(x, eps):
    return x * lax.rsqrt(jnp.mean(x * x, axis=-1, keepdims=True) + eps)


def _iota2(shape, dim):
    return lax.broadcasted_iota(I32, shape, dim)


def _pack_halves(lo, hi):
    lo_bits = lax.shift_right_logical(pltpu.bitcast(lo, U32), U32(16))
    hi_bits = pltpu.bitcast(hi, U32) & U32(0xFFFF0000)
    return lo_bits | hi_bits


def _unpack_halves(word):
    lo = pltpu.bitcast(lax.shift_left(word, U32(16)), F32)
    hi = pltpu.bitcast(word & U32(0xFFFF0000), F32)
    return lo, hi


def _round_bf16(x):
    return x.astype(BF16).astype(F32)


def _ada_kernel(c_ref, w_ref, b_ref, o_ref):
    cond = _silu(c_ref[...])
    o_ref[...] = jnp.sum(w_ref[...] * cond, axis=0, keepdims=True) + b_ref[...]


def _ada(c, w_ada, b_ada, tn):
    d, n = w_ada.shape
    return pl.pallas_call(
        _ada_kernel,
        grid=(n // tn,),
        in_specs=[pl.BlockSpec((d, 1), lambda j: (0, 0)),
                  pl.BlockSpec((d, tn), lambda j: (0, j)),
                  pl.BlockSpec((1, tn), lambda j: (0, j))],
        out_specs=pl.BlockSpec((1, tn), lambda j: (0, j)),
        out_shape=jax.ShapeDtypeStruct((1, n), F32),
        compiler_params=_params(("arbitrary",)),
    )(c.reshape(d, 1), w_ada, b_ada.reshape(1, n))


def _wprep_kernel(a_ref, b_ref, o_ref, *, first_shifted, shift):
    j = pl.program_id(0)

    @pl.when(j < first_shifted)
    def _():
        o_ref[...] = a_ref[...].astype(BF16)

    @pl.when(j >= first_shifted)
    def _():
        tn = a_ref.shape[0]
        o_ref[0:tn - shift, :] = a_ref[shift:tn, :].astype(BF16)
        o_ref[tn - shift:tn, :] = b_ref[...].astype(BF16)


def _wprep(w_in_t, cut0, cut1, tn):
    _, n_in, d = w_in_t.shape
    shift = cut1 - cut0
    n_out = n_in - shift
    assert cut0 % tn == 0 and n_out % tn == 0 and tn % shift == 0 and shift % (2 * SUBLANES) == 0
    return pl.pallas_call(
        functools.partial(_wprep_kernel, first_shifted=cut0 // tn, shift=shift),
        grid=(n_out // tn,),
        in_specs=[pl.BlockSpec((None, tn, d), lambda j: (0, j, 0)),
                  pl.BlockSpec((None, shift, d), lambda j: (0, (j + 1) * (tn // shift), 0))],
        out_specs=pl.BlockSpec((tn, d), lambda j: (j, 0)),
        out_shape=jax.ShapeDtypeStruct((n_out, d), BF16),
        compiler_params=_params(("arbitrary",)),
    )(w_in_t, w_in_t)


def _inproj_kernel(x_ref, g_ref, sc_ref, sh_ref, w_ref, wst_ref, proj_ref, smallt_ref, h_scr):
    @pl.when(pl.program_id(1) == 0)
    def _():
        h = _rms(x_ref[...], NORM_EPS) * g_ref[...] * (1.0 + sc_ref[...]) + sh_ref[...]
        hb = h.astype(BF16)
        h_scr[...] = hb
        smallt_ref[...] = _dg(wst_ref[...], hb, NT)

    proj_ref[...] = _dg(h_scr[...], w_ref[...], NT).astype(BF16)


def _inproj(x, g, sc, sh, w_main_t, w_small_t, tm, tn):
    s, d = x.shape
    n = w_main_t.shape[0]
    ns = w_small_t.shape[0]
    row = lambda i, j: (0, 0)
    return pl.pallas_call(
        _inproj_kernel,
        grid=(s // tm, n // tn),
        in_specs=[pl.BlockSpec((tm, d), lambda i, j: (i, 0)),
                  pl.BlockSpec((1, d), row), pl.BlockSpec((1, d), row), pl.BlockSpec((1, d), row),
                  pl.BlockSpec((tn, d), lambda i, j: (j, 0)),
                  pl.BlockSpec((ns, d), row)],
        out_specs=[pl.BlockSpec((tm, tn), lambda i, j: (i, j)),
                   pl.BlockSpec((ns, tm), lambda i, j: (0, i))],
        out_shape=[jax.ShapeDtypeStruct((s, n), BF16), jax.ShapeDtypeStruct((ns, s), F32)],
        scratch_shapes=[pltpu.VMEM((tm, d), BF16)],
        compiler_params=_params(("arbitrary", "arbitrary")),
    )(x, g, sc, sh, w_main_t, w_small_t)


def _hgrn_kernel(q_ref, f_ref, i_ref, g_ref, lb_ref, on_ref, o_ref, st_scr, *, n_chunks):
    @pl.when(pl.program_id(0) == 0)
    def _():
        st_scr[...] = jnp.zeros_like(st_scr)

    c = CHUNK
    hd = HEAD_DIM
    causal = _iota2((c, c), 1) <= _iota2((c, c), 0)
    tri = causal.astype(BF16)
    lb = lb_ref[...]
    on_g = on_ref[...]
    heads = [slice(h * hd, (h + 1) * hd) for h in range(HEADS)]

    def chunk(n, carry):
        rows = pl.ds(pl.multiple_of(n * c, c), c)
        f = lb + (1.0 - lb) * _sigmoid(f_ref[rows, :].astype(F32))
        b = _dot_exact_lhs(tri, jnp.log(f))
        k = 1.0 - f
        q = _silu(q_ref[rows, :].astype(F32)) * (hd ** -0.5)
        v = i_ref[rows, :]
        b_mid = b[c // 2:c // 2 + 1, :]
        b_last = b[c - 1:c, :]
        qa = (q * jnp.exp(b - b_mid)).astype(BF16)
        ka = (k * jnp.exp(b_mid - b)).astype(BF16)
        qi = (q * jnp.exp(b)).astype(BF16)
        ku = (k * jnp.exp(b_last - b)).astype(BF16)
        dec = jnp.exp(b_last)
        gate = on_g * _silu(g_ref[rows, :].astype(F32))
        sts = [st_scr[h] for h in range(HEADS)]
        scores = [jnp.where(causal, _dg(qa[:, sl], ka[:, sl], NT), 0.0).astype(BF16) for sl in heads]
        inter = [_dg(qi[:, sl], st.astype(BF16), NT) for sl, st in zip(heads, sts)]
        kv = [_dg(v[:, sl], ku[:, sl], TN) for sl in heads]
        for h, sl in enumerate(heads):
            st_scr[h] = dec[:, sl] * sts[h] + kv[h]
        outs = [_rms(_dot(sc, v[:, sl]) + it, NORM_EPS) for sc, sl, it in zip(scores, heads, inter)]
        o_ref[rows, :] = (jnp.concatenate(outs, axis=1) * gate).astype(BF16)
        return carry

    lax.fori_loop(0, n_chunks, chunk, 0, unroll=4)


def _hgrn(proj, lb, onorm_g, ts):
    s = proj.shape[0]
    width = HEADS * HEAD_DIM
    col = lambda blk: pl.BlockSpec((ts, width), lambda j, blk=blk: (j, blk))
    const = pl.BlockSpec((1, width), lambda j: (0, 0))
    return pl.pallas_call(
        functools.partial(_hgrn_kernel, n_chunks=ts // CHUNK),
        grid=(s // ts,),
        in_specs=[col(0), col(1), col(2), col(3), const, const],
        out_specs=pl.BlockSpec((ts, width), lambda j: (j, 0)),
        out_shape=jax.ShapeDtypeStruct((s, width), BF16),
        scratch_shapes=[pltpu.VMEM((HEADS, HEAD_DIM, HEAD_DIM), F32)],
        compiler_params=_params(("arbitrary",)),
    )(proj, proj, proj, proj, lb, jnp.tile(onorm_g, (1, HEADS)))


def _gdn_prep_kernel(q_ref, k_ref, v_ref, qp_ref, kp_ref, vp_ref, wq_ref, wk_ref, wv_ref, ab_ref, alog_ref,
                     dtb_ref, tri_ref, eye_ref, u_ref, wqd_ref, ku_ref, attn_ref, dl_ref, cat_scr, rows_scr, cols_scr,
                     *, n_chunks, ts):
    h = pl.program_id(1)
    first = pl.program_id(0) == 0
    c = CHUNK
    hd = HEAD_DIM

    def conv_silu(cur_ref, prev_ref, w_ref):
        cat_scr[0:SUBLANES, :] = jnp.where(first, 0.0, prev_ref[...].astype(F32))
        cat_scr[SUBLANES:SUBLANES + ts, :] = cur_ref[...].astype(F32)
        acc = None
        for j in range(CONV_WIDTH):
            off = SUBLANES - (CONV_WIDTH - 1) + j
            term = cat_scr[off:off + ts, :] * w_ref[j:j + 1, :]
            acc = term if acc is None else acc + term
        return _silu(acc)

    def l2n(x):
        return x * lax.rsqrt(jnp.sum(x * x, axis=-1, keepdims=True) + L2_EPS)

    q_all = l2n(conv_silu(q_ref, qp_ref, wq_ref)) * (hd ** -0.5)
    k_all = l2n(conv_silu(k_ref, kp_ref, wk_ref))
    v_all = conv_silu(v_ref, vp_ref, wv_ref)

    @pl.when(h == 0)
    def _():
        z = ab_ref[0:HEADS, :] + dtb_ref[...]
        softplus = jnp.maximum(z, 0.0) + jnp.log(1.0 + jnp.exp(-jnp.abs(z)))
        ld_rows = -jnp.exp(alog_ref[...]) * softplus
        hi, lo = _split(ld_rows)
        tri_blocks = tri_ref[...]
        w = tri_blocks.shape[0]
        spans = [slice(t0, t0 + w) for t0 in range(0, ts, w)]
        g_rows = jnp.concatenate([_dg(hi[:, sp], tri_blocks, NT) + _dg(lo[:, sp], tri_blocks, NT) for sp in spans],
                                 axis=1)
        beta_rows = _sigmoid(ab_ref[HEADS:2 * HEADS, :])
        rows_scr[...] = g_rows
        rows = jnp.concatenate([g_rows, beta_rows, jnp.zeros((LANES - 2 * HEADS, ts), F32)], axis=0)
        r_hi, r_lo = _split(rows)
        r_lo2 = (rows - r_hi.astype(F32) - r_lo.astype(F32)).astype(BF16)
        eye_w = eye_ref[...]
        for sp in spans:
            cols_scr[sp, :] = _dg(eye_w, r_hi[:, sp], NT) + _dg(eye_w, r_lo[:, sp], NT) + _dg(eye_w, r_lo2[:, sp], NT)

    lane = _iota2((ts, LANES), 1)
    cols = cols_scr[...]
    gc_all = jnp.sum(jnp.where(lane == h, cols, 0.0), axis=1, keepdims=True)
    bc_all = jnp.sum(jnp.where(lane == h + HEADS, cols, 0.0), axis=1, keepdims=True)
    g_row = rows_scr[pl.ds(h, 1), :]
    egc_all = jnp.exp(gc_all)

    r = _iota2((c, c), 0)
    cidx = _iota2((c, c), 1)
    causal = cidx <= r
    strict = cidx < r
    eye_f = (r == cidx).astype(F32)
    chunks = [slice(n * c, (n + 1) * c) for n in range(n_chunks)]

    q16 = q_all.astype(BF16)
    k16 = k_all.astype(BF16)
    kq = [_dg(jnp.concatenate([k16[sl], q16[sl]], axis=0), k16[sl], NT) for sl in chunks]
    dm = []
    for sl in chunks:
        diff = gc_all[sl] - g_row[:, sl]
        dm.append(jnp.where(causal, jnp.exp(jnp.where(causal, diff, 0.0)), 0.0))
    bm = [-jnp.where(strict, bc_all[sl] * x[0:c] * d, 0.0) for sl, x, d in zip(chunks, kq, dm)]
    p = [eye_f + b for b in bm]
    bm = [_dot(b.astype(BF16), b.astype(BF16)) for b in bm]
    for _ in range(c.bit_length() - 3):
        res = [_dot(b.astype(BF16), jnp.concatenate([b, pp], axis=1).astype(BF16)) for b, pp in zip(bm, p)]
        p = [pp + x[:, c:2 * c] for pp, x in zip(p, res)]
        bm = [x[:, 0:c] for x in res]
    p = [pp + _dot(b.astype(BF16), pp.astype(BF16)) for b, pp in zip(bm, p)]
    rhs = jnp.concatenate([v_all * bc_all, k_all * (bc_all * egc_all)], axis=1).astype(BF16)
    sol = [_dot(pp.astype(BF16), rhs[sl]) for pp, sl in zip(p, chunks)]
    qd_all = (q_all * egc_all).astype(BF16)
    for n, sl in enumerate(chunks):
        g_last = gc_all[(n + 1) * c - 1:(n + 1) * c, :]
        u_ref[sl, :] = sol[n][:, 0:hd].astype(BF16)
        wqd_ref[2 * n * c:(2 * n + 1) * c, :] = sol[n][:, hd:2 * hd].astype(BF16)
        wqd_ref[(2 * n + 1) * c:(2 * n + 2) * c, :] = qd_all[sl]
        ku_ref[sl, :] = (k_all[sl] * jnp.exp(g_last - gc_all[sl])).astype(BF16)
        attn_ref[sl, :] = (kq[n][c:2 * c] * dm[n]).astype(BF16)
        dl_ref[n:n + 1, :] = jnp.broadcast_to(jnp.exp(g_last), (1, hd))


def _gdn_prep(proj, conv_w, ab_t, a_log, dt_bias, ts):
    s = proj.shape[0]
    hd = HEAD_DIM
    c = CHUNK
    q0 = 4 * HEADS
    cur = lambda off: pl.BlockSpec((ts, hd), lambda j, h, off=off: (j, off + h))
    prev = lambda off: pl.BlockSpec((SUBLANES, hd),
                                    lambda j, h, off=off: (jnp.maximum(j * (ts // SUBLANES) - 1, 0), off + h))
    cw = lambda off: pl.BlockSpec((CONV_WIDTH, hd), lambda j, h, off=off: (0, off + h))
    per_head_scalar = pl.BlockSpec((HEADS, 1), lambda j, h: (0, 0))
    w = min(ts, MXU_DIM)
    const = pl.BlockSpec((w, w), lambda j, h: (0, 0))
    pos = jnp.arange(w)
    tri_blocks = ((pos[:, None] // c == pos[None, :] // c) & (pos[None, :] <= pos[:, None])).astype(BF16)
    eye = (pos[:, None] == pos[None, :]).astype(BF16)
    per_head = lambda rows, cols: pl.BlockSpec((None, rows, cols), lambda j, h: (h, j, 0))
    return pl.pallas_call(
        functools.partial(_gdn_prep_kernel, n_chunks=ts // c, ts=ts),
        grid=(s // ts, HEADS),
        in_specs=[cur(q0), cur(q0 + HEADS), cur(q0 + 2 * HEADS),
                  prev(q0), prev(q0 + HEADS), prev(q0 + 2 * HEADS),
                  cw(0), cw(HEADS), cw(2 * HEADS),
                  pl.BlockSpec((2 * HEADS, ts), lambda j, h: (0, j)),
                  per_head_scalar, per_head_scalar, const, const],
        out_specs=[pl.BlockSpec((ts, hd), lambda j, h: (j, h)),
                   pl.BlockSpec((2 * ts, hd), lambda j, h: (j, h)),
                   pl.BlockSpec((ts, hd), lambda j, h: (j, h)),
                   per_head(ts, c),
                   per_head(ts // c, hd)],
        out_shape=[jax.ShapeDtypeStruct((s, HEADS * hd), BF16),
                   jax.ShapeDtypeStruct((2 * s, HEADS * hd), BF16),
                   jax.ShapeDtypeStruct((s, HEADS * hd), BF16),
                   jax.ShapeDtypeStruct((HEADS, s, c), BF16),
                   jax.ShapeDtypeStruct((HEADS, s // c, hd), F32)],
        scratch_shapes=[pltpu.VMEM((ts + SUBLANES, hd), F32), pltpu.VMEM((HEADS, ts), F32),
                        pltpu.VMEM((ts, LANES), F32)],
        compiler_params=_params(("arbitrary", "arbitrary")),
    )(proj, proj, proj, proj, proj, proj, conv_w, conv_w, conv_w,
      ab_t, a_log.reshape(HEADS, 1), dt_bias.reshape(HEADS, 1), tri_blocks, eye)


def _gdn_scan_kernel(u_ref, wqd_ref, ku_ref, attn_ref, dl_ref, g_ref, on_ref, o_ref, st_scr, *, n_chunks):
    @pl.when(pl.program_id(0) == 0)
    def _():
        st_scr[...] = jnp.zeros_like(st_scr)

    c = CHUNK
    hd = HEAD_DIM
    on_g = on_ref[...]
    heads = [slice(h * hd, (h + 1) * hd) for h in range(HEADS)]

    def chunk(n, carry):
        rows = pl.ds(pl.multiple_of(n * c, c), c)
        rows2 = pl.ds(pl.multiple_of(2 * n * c, 2 * c), 2 * c)
        sts = [st_scr[h] for h in range(HEADS)]
        wq = [_dot(wqd_ref[rows2, sl], st.astype(BF16)) for sl, st in zip(heads, sts)]
        vn = [(u_ref[rows, sl].astype(F32) - x[0:c]).astype(BF16) for sl, x in zip(heads, wq)]
        upd = [_dg(ku_ref[rows, sl], v, TN) for sl, v in zip(heads, vn)]
        for h in range(HEADS):
            st_scr[h] = dl_ref[h, pl.ds(n, 1), :] * sts[h] + upd[h]
        outs = [_rms(x[c:2 * c] + _dot(attn_ref[h, rows, :], v), NORM_EPS)
                for h, (x, v) in enumerate(zip(wq, vn))]
        gate = jnp.tile(on_g, (1, HEADS)) * _silu(g_ref[rows, :].astype(F32))
        o_ref[rows, :] = (jnp.concatenate(outs, axis=1) * gate).astype(BF16)
        return carry

    lax.fori_loop(0, n_chunks, chunk, 0, unroll=4)


def _gdn_scan(u, wqd, ku, attn, dl, proj, onorm_g, ts):
    s, width = u.shape
    c = CHUNK
    gate_blk = (4 * HEADS + 3 * HEADS) * HEAD_DIM // width
    return pl.pallas_call(
        functools.partial(_gdn_scan_kernel, n_chunks=ts // c),
        grid=(s // ts,),
        in_specs=[pl.BlockSpec((ts, width), lambda j: (j, 0)),
                  pl.BlockSpec((2 * ts, width), lambda j: (j, 0)),
                  pl.BlockSpec((ts, width), lambda j: (j, 0)),
                  pl.BlockSpec((HEADS, ts, c), lambda j: (0, j, 0)),
                  pl.BlockSpec((HEADS, ts // c, HEAD_DIM), lambda j: (0, j, 0)),
                  pl.BlockSpec((ts, width), lambda j: (j, gate_blk)),
                  pl.BlockSpec((1, HEAD_DIM), lambda j: (0, 0))],
        out_specs=pl.BlockSpec((ts, width), lambda j: (j, 0)),
        out_shape=jax.ShapeDtypeStruct((s, width), BF16),
        scratch_shapes=[pltpu.VMEM((HEADS, HEAD_DIM, HEAD_DIM), F32)],
        compiler_params=_params(("arbitrary",)),
    )(u, wqd, ku, attn, dl, proj, onorm_g)


def _merge_kernel(oa_ref, ob_ref, mga_ref, mgb_ref, x_ref, wa_ref, wb_ref, wo_ref, gt_ref, g2_ref, sc_ref,
                  sh_ref, x1_ref, h2_ref):
    ya = _dot(oa_ref[...], wa_ref[...])
    yb = _dot(ob_ref[...], wb_ref[...])
    merged = _sigmoid(mga_ref[...].astype(F32)) * ya + _sigmoid(mgb_ref[...].astype(F32)) * yb
    x1 = x_ref[...] + gt_ref[...] * _dot(merged.astype(BF16), wo_ref[...])
    x1_ref[...] = x1
    h2 = _rms(x1, NORM_EPS) * g2_ref[...] * (1.0 + sc_ref[...]) + sh_ref[...]
    h2_ref[...] = h2.astype(BF16)


def _merge(o_a, o_b, proj, x, w_a, w_b, w_o, gt1, g2, sc2, sh2, tm):
    s, d = x.shape
    dv = o_a.shape[1]
    mg0 = (8 * HEADS * HEAD_DIM) // d
    const = lambda shape: pl.BlockSpec(shape, lambda i: (0, 0), pipeline_mode=pl.Buffered(1))
    return pl.pallas_call(
        _merge_kernel,
        grid=(s // tm,),
        in_specs=[pl.BlockSpec((tm, dv), lambda i: (i, 0)),
                  pl.BlockSpec((tm, dv), lambda i: (i, 0)),
                  pl.BlockSpec((tm, d), lambda i: (i, mg0)),
                  pl.BlockSpec((tm, d), lambda i: (i, mg0 + 1)),
                  pl.BlockSpec((tm, d), lambda i: (i, 0)),
                  const((dv, d)), const((dv, d)), const((d, d)),
                  const((1, d)), const((1, d)), const((1, d)), const((1, d))],
        out_specs=[pl.BlockSpec((tm, d), lambda i: (i, 0)), pl.BlockSpec((tm, d), lambda i: (i, 0))],
        out_shape=[jax.ShapeDtypeStruct((s, d), F32), jax.ShapeDtypeStruct((s, d), BF16)],
        compiler_params=_params(("arbitrary",)),
    )(o_a, o_b, proj, proj, x, w_a, w_b, w_o, gt1, g2, sc2, sh2)


def _first_max(vals, iota, size, axis):
    m = jnp.max(vals, axis=axis, keepdims=True)
    idx = jnp.min(jnp.where(vals == m, iota, size), axis=axis, keepdims=True)
    return m, idx


def _router_kernel(x1_ref, g2_ref, sc_ref, sh_ref, wrt_ref, bias_ref, upper_ref, pos_ref, wts_ref, before_ref,
                   ntile_ref, cnt_scr, *, tm):
    @pl.when(pl.program_id(0) == 0)
    def _():
        cnt_scr[...] = jnp.zeros_like(cnt_scr)

    e = N_EXPERTS
    h2 = _rms(x1_ref[...], NORM_EPS) * g2_ref[...] * (1.0 + sc_ref[...]) + sh_ref[...]
    logits = lax.dot_general(wrt_ref[...], h2, NT, preferred_element_type=F32,
                             precision=lax.Precision.HIGHEST)
    scores = _sigmoid(logits)
    biased = scores + bias_ref[...]
    neg = -jnp.inf

    g3 = biased.reshape(N_GROUPS, GROUP_SIZE, tm)
    i3 = lax.broadcasted_iota(I32, g3.shape, 1)
    m1, a1 = _first_max(g3, i3, GROUP_SIZE, 1)
    m2 = jnp.max(jnp.where(i3 == a1, neg, g3), axis=1, keepdims=True)
    gs = (m1 + m2).reshape(N_GROUPS, tm)
    ig = _iota2(gs.shape, 0)
    gmask = jnp.zeros(gs.shape, jnp.bool_)
    for _ in range(TOPK_GROUPS):
        _, a = _first_max(gs, ig, N_GROUPS, 0)
        pick = ig == a
        gmask = jnp.logical_or(gmask, pick)
        gs = jnp.where(pick, neg, gs)
    emask = jnp.broadcast_to(gmask.reshape(N_GROUPS, 1, tm), (N_GROUPS, GROUP_SIZE, tm)).reshape(e, tm)

    cand = jnp.where(emask, biased, neg)
    ie = _iota2((e, tm), 0)
    sel_all = jnp.zeros((e, tm), jnp.bool_)
    w_rows, picks = [], []
    for _ in range(TOP_K):
        _, a = _first_max(cand, ie, e, 0)
        pick = ie == a
        picks.append(pick)
        w_rows.append(jnp.sum(jnp.where(pick, scores, 0.0), axis=0, keepdims=True))
        sel_all = jnp.logical_or(sel_all, pick)
        cand = jnp.where(pick, neg, cand)
    w_sum = w_rows[0]
    for wr in w_rows[1:]:
        w_sum = w_sum + wr
    wts = jnp.concatenate(w_rows, axis=0) / w_sum * ROUTED_SCALE

    sel = sel_all.astype(BF16)
    in_expert = _dot(sel, upper_ref[...])
    n_tile = jnp.sum(sel_all.astype(F32), axis=1, keepdims=True)
    lower = (_iota2((e, e), 1) < _iota2((e, e), 0)).astype(BF16)
    expert_off = _dot_exact_lhs(lower, jnp.broadcast_to(n_tile, (e, LANES)))[:, 0:1]
    place = in_expert + expert_off
    pos = jnp.concatenate([jnp.sum(jnp.where(pk, place, 0.0), axis=0, keepdims=True) for pk in picks], axis=0)
    pos_ref[...] = pos.astype(I32)
    before_ref[...] = jnp.broadcast_to(cnt_scr[...], before_ref.shape).astype(I32)
    ntile_ref[...] = jnp.broadcast_to(n_tile, ntile_ref.shape).astype(I32)
    cnt_scr[...] = cnt_scr[...] + n_tile
    wts_ref[...] = wts


def _router(x1, g2, sc2, sh2, w_router_t, bias_col, tm):
    s, d = x1.shape
    e = N_EXPERTS
    nt = s // tm
    upper = (jnp.arange(tm)[:, None] < jnp.arange(tm)[None, :]).astype(BF16)
    const = lambda shape: pl.BlockSpec(shape, lambda i: (0, 0))
    per_tile = pl.BlockSpec((None, e, LANES), lambda i: (i, 0, 0))
    return pl.pallas_call(
        functools.partial(_router_kernel, tm=tm),
        grid=(nt,),
        in_specs=[pl.BlockSpec((tm, d), lambda i: (i, 0)),
                  const((1, d)), const((1, d)), const((1, d)),
                  const((e, d)), const((e, 1)), const((tm, tm))],
        out_specs=[pl.BlockSpec((TOP_K, tm), lambda i: (0, i)),
                   pl.BlockSpec((TOP_K, tm), lambda i: (0, i)),
                   per_tile, per_tile],
        out_shape=[jax.ShapeDtypeStruct((TOP_K, s), I32), jax.ShapeDtypeStruct((TOP_K, s), F32),
                   jax.ShapeDtypeStruct((nt, e, LANES), I32), jax.ShapeDtypeStruct((nt, e, LANES), I32)],
        scratch_shapes=[pltpu.VMEM((e, 1), F32)],
        compiler_params=_params(("arbitrary",)),
    )(x1, g2, sc2, sh2, w_router_t, bias_col, upper)


LONG_RUN = 64


def _run_sizes(limit):
    return [1 << b for b in range(limit.bit_length() - 1, -1, -1)]


def _for_each_run(tile, run_refs, tm, make_copy, fn, unroll=False, enable=None):
    run_len_ref, run_off_ref, run_dst_ref = run_refs

    def per_expert(ex, carry):
        n = run_len_ref[tile * N_EXPERTS + ex]
        off = run_off_ref[tile * N_EXPERTS + ex]
        dst = run_dst_ref[tile * N_EXPERTS + ex]
        def pieces(sizes):
            for size in sizes:
                done = n & (-2 * size)

                take = (n & size) != 0
                if enable is not None:
                    take = jnp.logical_and(take, enable)

                lane = size.bit_length() + (ex if isinstance(ex, int) else 0)

                @pl.when(take)
                def _(done=done, size=size, lane=lane):
                    fn(make_copy(off + done, dst + done, size), lane % 2)

        sizes = _run_sizes(tm)
        pieces([size for size in sizes if size < LONG_RUN])

        @pl.when(n >= LONG_RUN)
        def _():
            pieces([size for size in sizes if size >= LONG_RUN])

        return carry

    if unroll:
        for ex in range(N_EXPERTS):
            per_expert(ex, 0)
    else:
        lax.fori_loop(0, N_EXPERTS, per_expert, 0)


def _slot_rows(slot, n_slots):
    return pl.ds(pl.multiple_of(slot * SUBLANES, SUBLANES), n_slots * SUBLANES)


def _dispatch_kernel(run_len_ref, run_off_ref, run_dst_ref, pad_lo_ref, pad_hi_ref, pos_ref, h_ref, xs_ref, stage, zero_scr,
                     sem, pad_sem, *, tm, rows_per_pass):
    step = pl.program_id(0)
    na = TOP_K * tm
    d = h_ref.shape[1]
    half = d // 2
    n_words = half // LANES

    def pad_copy(slot, n_slots):
        return pltpu.make_async_copy(zero_scr.at[pl.ds(0, n_slots * SUBLANES), :],
                                     xs_ref.at[_slot_rows(slot, n_slots), :], pad_sem)

    def for_each_pad(fn):
        def per_expert(ex, carry):
            slot = pad_lo_ref[ex]
            n = pad_hi_ref[ex] - slot
            for size in _run_sizes(EXPERT_BLOCK - 1):
                take = (n & size) != 0

                @pl.when(take)
                def _(slot=slot, size=size):
                    fn(pad_copy(slot, size))

                slot = slot + jnp.where(take, size, 0)
            return carry
        lax.fori_loop(0, N_EXPERTS, per_expert, 0)

    @pl.when(step == 0)
    def _():
        zero_scr[...] = jnp.zeros_like(zero_scr)
        for_each_pad(lambda cp: cp.start())

    buf = step % 2
    last = pl.num_programs(0) - 1
    runs = (run_len_ref, run_off_ref, run_dst_ref)

    def run_copy_from(which):
        def run_copy(tile_slot, sorted_slot, n_slots):
            return pltpu.make_async_copy(stage.at[which, _slot_rows(tile_slot, n_slots), :],
                                         xs_ref.at[_slot_rows(sorted_slot, n_slots), :], sem.at[which])
        return run_copy

    def wait_tile(which):
        pltpu.make_async_copy(stage.at[which], xs_ref.at[pl.ds(0, na * SUBLANES), :], sem.at[which]).wait()

    _for_each_run(jnp.maximum(step - 1, 0), runs, tm, run_copy_from(1 - buf), lambda cp, pr: cp.start(priority=pr),
                  unroll=True, enable=step > 0)

    pos = pos_ref[...].astype(jnp.int16)
    h = h_ref[...]
    slot0 = _iota2((rows_per_pass, tm), 0).astype(jnp.int16)
    for a0 in range(0, na, rows_per_pass):
        slot_id = slot0 + jnp.int16(a0)
        hit = pos[0:1, :] == slot_id
        for k in range(1, TOP_K):
            hit = jnp.logical_or(hit, pos[k:k + 1, :] == slot_id)
        rows = _dot(hit.astype(BF16), h)
        for i in range(n_words):
            word = _pack_halves(rows[:, i * LANES:(i + 1) * LANES], rows[:, half + i * LANES:half + (i + 1) * LANES])
            stage[buf, pl.ds(a0 * SUBLANES + i, rows_per_pass, stride=SUBLANES), :] = word

    @pl.when(step > 0)
    def _():
        wait_tile(1 - buf)

    @pl.when(step == last)
    def _():
        _for_each_run(step, runs, tm, run_copy_from(buf), lambda cp, pr: cp.start(priority=pr))
        wait_tile(buf)

    @pl.when(step == 0)
    def _():
        for_each_pad(lambda cp: cp.wait())


def _dispatch(runs, pad_lo, pad_hi, pos_t, h2, n_slots, tm, rows_per_pass):
    s, d = h2.shape
    assert (d // 2) % LANES == 0 and (d // 2) // LANES == SUBLANES, "one token row must pack into one (8, 128) tile"
    na = TOP_K * tm
    return pl.pallas_call(
        functools.partial(_dispatch_kernel, tm=tm, rows_per_pass=min(rows_per_pass, na)),
        grid_spec=pltpu.PrefetchScalarGridSpec(
            num_scalar_prefetch=5,
            grid=(s // tm,),
            in_specs=[pl.BlockSpec((TOP_K, tm), lambda i, *_: (0, i)),
                      pl.BlockSpec((tm, d), lambda i, *_: (i, 0))],
            out_specs=pl.BlockSpec(memory_space=pl.ANY),
            scratch_shapes=[pltpu.VMEM((2, na * SUBLANES, LANES), U32),
                            pltpu.VMEM((EXPERT_BLOCK // 2 * SUBLANES, LANES), U32),
                            pltpu.SemaphoreType.DMA((2,)), pltpu.SemaphoreType.DMA(())]),
        out_shape=jax.ShapeDtypeStruct((n_slots * SUBLANES, LANES), U32),
        compiler_params=_params(("arbitrary",), has_side_effects=True, disable_bounds_checks=True),
    )(*runs, pad_lo, pad_hi, pos_t, h2)


def _expert_kernel(be_ref, nu_ref, next_ref, par_ref, valid_ref, x_ref, wg_hbm, wu_hbm, wd_hbm, y_ref, wg_f32, wu_f32, wd_f32,
                   wg_scr, wu_scr, wd_scr, sem):
    b = pl.program_id(0)
    bm = EXPERT_BLOCK
    active = b < nu_ref[0]
    new_expert = jnp.logical_or(b == 0, be_ref[b] != be_ref[jnp.maximum(b - 1, 0)])

    def weight_copies(ex, which):
        return [pltpu.make_async_copy(src.at[ex], dst.at[which], sem.at[which])
                for src, dst in ((wg_hbm, wg_f32), (wu_hbm, wu_f32), (wd_hbm, wd_f32))]

    @pl.when(jnp.logical_and(active, new_expert))
    def _():
        which = par_ref[b]

        @pl.when(b == 0)
        def _():
            for cp in weight_copies(be_ref[b], which):
                cp.start()

        for cp in weight_copies(be_ref[b], which):
            cp.wait()
        wg_scr[...] = wg_f32[which].astype(BF16)
        wu_scr[...] = wu_f32[which].astype(BF16)
        wd_scr[...] = wd_f32[which].astype(BF16)
        nb = next_ref[b]

        @pl.when(nb < nu_ref[0])
        def _():
            for cp in weight_copies(be_ref[nb], 1 - which):
                cp.start()

    def ffn_rows(rows):
        los, his = [], []
        for i in range(SUBLANES):
            lo, hi = _unpack_halves(x_ref[pl.ds(i, rows, stride=SUBLANES), :])
            los.append(lo.astype(BF16))
            his.append(hi.astype(BF16))
        xb = jnp.concatenate(los + his, axis=1)
        hid = _silu(_dot(xb, wg_scr[...])) * _dot(xb, wu_scr[...])
        y = _dot(hid.astype(BF16), wd_scr[...])
        half = y.shape[1] // 2
        for i in range(SUBLANES):
            word = _pack_halves(_round_bf16(y[:, i * LANES:(i + 1) * LANES]),
                                _round_bf16(y[:, half + i * LANES:half + (i + 1) * LANES]))
            y_ref[pl.ds(i, rows, stride=SUBLANES), :] = word

    half_full = valid_ref[b] <= bm // 2

    @pl.when(jnp.logical_and(active, jnp.logical_not(half_full)))
    def _():
        ffn_rows(bm)

    @pl.when(jnp.logical_and(active, half_full))
    def _():
        ffn_rows(bm // 2)
        y_ref[bm // 2 * SUBLANES:bm * SUBLANES, :] = jnp.zeros((bm // 2 * SUBLANES, LANES), U32)


def _experts(block_e, n_used, next_block, parity, valid, xs, w_gate, w_up, w_down):
    d, ff = w_gate.shape[1], w_gate.shape[2]
    bm = EXPERT_BLOCK
    n_blocks = xs.shape[0] // (bm * SUBLANES)
    blk = lambda b, be, nu, *_: (jnp.minimum(b, nu[0] - 1), 0)
    hbm = pl.BlockSpec(memory_space=pl.ANY)
    return pl.pallas_call(
        _expert_kernel,
        grid_spec=pltpu.PrefetchScalarGridSpec(
            num_scalar_prefetch=5,
            grid=(n_blocks,),
            in_specs=[pl.BlockSpec((bm * SUBLANES, LANES), blk), hbm, hbm, hbm],
            out_specs=pl.BlockSpec((bm * SUBLANES, LANES), blk),
            scratch_shapes=[pltpu.VMEM((2, d, ff), F32), pltpu.VMEM((2, d, ff), F32), pltpu.VMEM((2, ff, d), F32),
                            pltpu.VMEM((d, ff), BF16), pltpu.VMEM((d, ff), BF16), pltpu.VMEM((ff, d), BF16),
                            pltpu.SemaphoreType.DMA((2,))]),
        out_shape=jax.ShapeDtypeStruct(xs.shape, U32),
        compiler_params=_params(("arbitrary",)),
    )(block_e, n_used, next_block, parity, valid, xs, w_gate, w_up, w_down)


def _combine_kernel(run_len_ref, run_off_ref, run_src_ref, ys_ref, h_ref, x1_ref, pos_ref, wts_ref, wg_ref, wu_ref, wd_ref, gt_ref, gf_ref,
                    o_ref, stage, sem, *, tm, rows_per_pass):
    step = pl.program_id(0)
    na = TOP_K * tm
    buf = step % 2

    last = pl.num_programs(0) - 1

    def fetch_tile(tile, which, unroll=False):
        def run_copy(tile_slot, sorted_slot, n_slots):
            return pltpu.make_async_copy(ys_ref.at[_slot_rows(sorted_slot, n_slots), :],
                                         stage.at[which, _slot_rows(tile_slot, n_slots), :], sem.at[which])
        _for_each_run(tile, (run_len_ref, run_off_ref, run_src_ref), tm, run_copy,
                      lambda cp, pr: cp.start(priority=pr), unroll)

    def wait_tile(which):
        pltpu.make_async_copy(ys_ref.at[pl.ds(0, na * SUBLANES), :], stage.at[which], sem.at[which]).wait()

    @pl.when(step == 0)
    def _():
        fetch_tile(step, buf)

    fetch_tile(jnp.minimum(step + 1, last), 1 - buf, unroll=True)

    hb = h_ref[...]
    hid = _silu(_dot(hb, wg_ref[...])) * _dot(hb, wu_ref[...])
    acc = _dot(hid.astype(BF16), wd_ref[...])

    wait_tile(buf)

    pos = pos_ref[...]
    wts = wts_ref[...]
    for a0 in range(0, na, rows_per_pass):
        los, his = [], []
        for i in range(SUBLANES):
            lo, hi = _unpack_halves(stage[buf, pl.ds(a0 * SUBLANES + i, rows_per_pass, stride=SUBLANES), :])
            los.append(lo.astype(BF16))
            his.append(hi.astype(BF16))
        y_rows = jnp.concatenate(los + his, axis=1)
        slot_id = a0 + _iota2((rows_per_pass, tm), 0)
        wmat = jnp.zeros((rows_per_pass, tm), F32)
        for k in range(TOP_K):
            wmat = wmat + jnp.where(pos[k:k + 1, :] == slot_id, wts[k:k + 1, :], 0.0)
        acc = acc + _dg(wmat.astype(BF16), y_rows, TN)
    x2 = x1_ref[...] + gt_ref[...] * acc
    o_ref[...] = _rms(x2, NORM_EPS) * gf_ref[...]

    @pl.when(step == last)
    def _():
        wait_tile(1 - buf)


def _combine(runs, ys, h2, x1, pos_t, wts_t, w_gate, w_up, w_down, gt2, gf, tm, rows_per_pass):
    s, d = x1.shape
    ff = w_gate.shape[1]
    na = TOP_K * tm
    const = lambda shape: pl.BlockSpec(shape, lambda i, *_: (0, 0), pipeline_mode=pl.Buffered(1))
    tile = lambda cols: pl.BlockSpec((tm, cols), lambda i, *_: (i, 0))
    per_k = pl.BlockSpec((TOP_K, tm), lambda i, *_: (0, i))
    return pl.pallas_call(
        functools.partial(_combine_kernel, tm=tm, rows_per_pass=min(rows_per_pass, na)),
        grid_spec=pltpu.PrefetchScalarGridSpec(
            num_scalar_prefetch=3,
            grid=(s // tm,),
            in_specs=[pl.BlockSpec(memory_space=pl.ANY),
                      tile(d), tile(d), per_k, per_k,
                      const((d, ff)), const((d, ff)), const((ff, d)), const((1, d)), const((1, d))],
            out_specs=tile(d),
            scratch_shapes=[pltpu.VMEM((2, na * SUBLANES, LANES), U32), pltpu.SemaphoreType.DMA((2,))]),
        out_shape=jax.ShapeDtypeStruct((s, d), F32),
        compiler_params=_params(("arbitrary",), disable_bounds_checks=True),
    )(*runs, ys, h2, x1, pos_t, wts_t, w_gate, w_up, w_down, gt2, gf)


def _mixer(x2d, mod, norm1_g, norm2_g, w_in, lb, hgrn_onorm_g, gdn_conv_w, gdn_a_log, gdn_dt_bias, gdn_onorm_g,
           w_branch_hgrn, w_branch_gdn, w_out, tiles):
    d = x2d.shape[1]
    sh1, sc1, gt1, sh2, sc2, _ = [mod[:, i * d:(i + 1) * d] for i in range(6)]
    key = HEADS * HEAD_DIM
    small0 = 4 * key + 3 * key
    small1 = small0 + 2 * HEADS
    w_in_t = jnp.swapaxes(w_in, 1, 2)
    w_main_t = _wprep(w_in_t, small0, small1, tiles["wprep_tn"])
    w_small_t = w_in_t[0, small0:small1, :].astype(BF16)
    proj, ab_t = _inproj(x2d, norm1_g, sc1, sh1, w_main_t, w_small_t, tiles["in_tm"], tiles["in_tn"])
    o_a = _hgrn(proj, lb, hgrn_onorm_g, tiles["mix_ts"])
    u, wqd, ku, attn, dl = _gdn_prep(proj, gdn_conv_w, ab_t, gdn_a_log, gdn_dt_bias, tiles["prep_ts"])
    o_b = _gdn_scan(u, wqd, ku, attn, dl, proj, gdn_onorm_g, tiles["mix_ts"])
    return _merge(o_a, o_b, proj, x2d, w_branch_hgrn.astype(BF16), w_branch_gdn.astype(BF16),
                  w_out.astype(BF16), gt1, norm2_g, sc2, sh2, tiles["merge_tm"])


def _moe(x1, h2, mod, norm2_g, normf_g, w_router, router_bias, w_exp_gate, w_exp_up, w_exp_down, w_sh_gate,
         w_sh_up, w_sh_down, tiles):
    s, d = x1.shape
    tm = tiles["moe_tm"]
    sh2, sc2, gt2 = [mod[:, i * d:(i + 1) * d] for i in (3, 4, 5)]
    pos_t, wts_t, before, ntile = _router(x1, norm2_g, sc2, sh2, w_router.T, router_bias.reshape(-1, 1), tm)
    bm = EXPERT_BLOCK
    n_blocks = -(-(s * TOP_K + N_EXPERTS * (bm - 1)) // bm)
    before = before[:, :, 0]
    ntile = ntile[:, :, 0]
    counts = before[-1] + ntile[-1]
    padded = (counts + bm - 1) // bm * bm
    earlier = jnp.arange(N_EXPERTS)[None, :] < jnp.arange(N_EXPERTS)[:, None]
    pstart = jnp.sum(jnp.where(earlier, padded[None, :], 0), axis=1).astype(I32)
    pend = pstart + padded
    block_start = jnp.arange(n_blocks, dtype=I32) * bm
    block_e = jnp.minimum(jnp.sum(pend[None, :] <= block_start[:, None], axis=1), N_EXPERTS - 1).astype(I32)
    n_used = pend[-1:] // bm
    run_off = jnp.sum(jnp.where(earlier[None], ntile[:, None, :], 0), axis=2)
    runs = (ntile.reshape(-1), run_off.reshape(-1), (before + pstart[None, :]).reshape(-1))
    xs = _dispatch(runs, pstart + counts, pend, pos_t, h2, n_blocks * bm, tm, tiles["dispatch_rows"])
    own = block_e[:, None] == jnp.arange(N_EXPERTS)[None, :]
    next_block = jnp.sum(jnp.where(own, pend[None, :], 0), axis=1) // bm
    switches = jnp.concatenate([jnp.zeros((1,), I32), (block_e[1:] != block_e[:-1]).astype(I32)])
    upto = jnp.arange(n_blocks)[None, :] <= jnp.arange(n_blocks)[:, None]
    parity = jnp.sum(jnp.where(upto, switches[None, :], 0), axis=1).astype(I32) % 2
    start_b = jnp.sum(jnp.where(own, pstart[None, :], 0), axis=1)
    count_b = jnp.sum(jnp.where(own, counts[None, :], 0), axis=1)
    valid = jnp.clip(count_b - (block_start - start_b), 0, bm).astype(I32)
    ys = _experts(block_e, n_used, next_block, parity, valid, xs, w_exp_gate, w_exp_up, w_exp_down)
    return _combine(runs, ys, h2, x1, pos_t, wts_t, w_sh_gate.astype(BF16), w_sh_up.astype(BF16),
                    w_sh_down.astype(BF16), gt2, normf_g, tm, tiles["combine_rows"])


def _tiles(s):
    pick = lambda want: min(want, s)
    return dict(ada_tn=1024, wprep_tn=512, in_tm=pick(1024), in_tn=1536, mix_ts=pick(512), prep_ts=pick(2048),
                merge_tm=pick(512), moe_tm=pick(256), dispatch_rows=512, combine_rows=256)


def kernel(x, c, w_ada, b_ada, norm1_g, norm2_g, w_in, hgrn_lb_table, hgrn_onorm_g, gdn_conv_w, gdn_a_log, gdn_dt_bias, gdn_onorm_g, w_branch_hgrn, w_branch_gdn, w_out, w_router, router_bias, w_exp_gate, w_exp_up, w_exp_down, w_sh_gate, w_sh_up, w_sh_down, normf_g):
    b, s, d = x.shape
    assert b == 1 and w_ada.shape[0] == 1, "one sequence, one layer"
    tiles = _tiles(s)
    lb = jnp.sum(jax.nn.softmax(hgrn_lb_table.astype(F32), axis=0)[0:1], axis=0, keepdims=True)
    mod = _ada(c, w_ada[0], b_ada[0], tiles["ada_tn"])
    row = lambda v: v.reshape(1, -1)
    x1, h2 = _mixer(x[0], mod, row(norm1_g[0]), row(norm2_g[0]), w_in, lb, row(hgrn_onorm_g[0]), gdn_conv_w[0],
                    gdn_a_log[0], gdn_dt_bias[0], row(gdn_onorm_g[0]), w_branch_hgrn[0], w_branch_gdn[0], w_out[0],
                    tiles)
    out = _moe(x1, h2, mod, row(norm2_g[0]), row(normf_g), w_router[0], router_bias[0], w_exp_gate[0],
               w_exp_up[0], w_exp_down[0], w_sh_gate[0], w_sh_up[0], w_sh_down[0], tiles)
    return out[None]
```
